```python
import jax, jax.numpy as jnp
from jax import lax
import numpy as np

D_MODEL = 1024
BATCH = 16
SEQ = 2048
DEPTH = 4

D_CONV = D_MODEL
CONV_A_WIDTH = 31
D_RNN = 1536
N_RNN_HEADS = 16
RNN_HEAD_DIM = D_RNN // N_RNN_HEADS
CONV_B_WIDTH = 4
LRU_C = 8.0
D_FF = 4 * D_MODEL
EPS = 1e-6

SPLITS = (D_CONV, D_CONV, D_RNN, D_RNN, D_MODEL, D_MODEL)
D_IN = sum(SPLITS)

kernel_name = "hybrid_conformer_conv_rglru_gated_parallel"


def rms_norm(x, g):
    xf = x.astype(jnp.float32)
    y = xf * lax.rsqrt(jnp.mean(xf * xf, axis=-1, keepdims=True) + EPS)
    return (y * g.astype(jnp.float32)).astype(x.dtype)


def layer_norm(x, g, b):
    xf = x.astype(jnp.float32)
    mu = jnp.mean(xf, axis=-1, keepdims=True)
    xc = xf - mu
    y = xc * lax.rsqrt(jnp.mean(xc * xc, axis=-1, keepdims=True) + EPS)
    return (y * g.astype(jnp.float32) + b.astype(jnp.float32)).astype(x.dtype)


def causal_depthwise_conv(u, w, b):
    k = w.shape[0]
    y = lax.conv_general_dilated(
        u, w[:, None, :].astype(u.dtype), window_strides=(1,), padding=[(k - 1, 0)],
        dimension_numbers=("NWC", "WIO", "NWC"), feature_group_count=u.shape[-1])
    return y + b


def block_diag_linear(x, w, b):
    bsz, s, c = x.shape
    xh = x.reshape(bsz, s, N_RNN_HEADS, RNN_HEAD_DIM)
    y = jnp.einsum("bshi,hij->bshj", xh, w).reshape(bsz, s, c)
    return y + b


def rg_lru(x, w_a, b_a, w_x, b_x, lam):
    s = x.shape[1]
    r = jax.nn.sigmoid(block_diag_linear(x, w_a, b_a).astype(jnp.float32))
    i = jax.nn.sigmoid(block_diag_linear(x, w_x, b_x).astype(jnp.float32))
    log_a = -LRU_C * r * jax.nn.softplus(-lam.astype(jnp.float32))
    a = jnp.exp(log_a)
    mult = jnp.sqrt(-jnp.expm1(2.0 * log_a))
    is_start = (jnp.arange(s) == 0)[None, :, None]
    mult = jnp.where(is_start, 1.0, mult)
    bterm = mult * i * x.astype(jnp.float32)

    def combine(left, right):
        a_l, b_l = left
        a_r, b_r = right
        return a_r * a_l, a_r * b_l + b_r

    _, h = lax.associative_scan(combine, (a, bterm), axis=1)
    return h.astype(x.dtype)


def _fwd_setup_inputs(seed: int = 0) -> dict:
    key = jax.random.key(seed)
    ks = jax.random.split(key, 32)
    f32 = jnp.float32
    L = DEPTH

    def nrm(k, shape, scale):
        return jax.random.normal(k, shape, f32) * scale

    x = jax.random.normal(ks[0], (BATCH, SEQ, D_MODEL), f32)
    g_mix = 1.0 + nrm(ks[1], (L, D_MODEL), 0.02)
    w_in = nrm(ks[2], (L, D_MODEL, D_IN), D_MODEL ** -0.5)
    b_in = nrm(ks[3], (L, D_IN), 0.02)
    conv_a_w = nrm(ks[4], (L, CONV_A_WIDTH, D_CONV), CONV_A_WIDTH ** -0.5)
    conv_a_b = nrm(ks[5], (L, D_CONV), 0.02)
    ln_g = 1.0 + nrm(ks[6], (L, D_CONV), 0.02)
    ln_b = nrm(ks[7], (L, D_CONV), 0.02)
    w_a_out = nrm(ks[8], (L, D_CONV, D_MODEL), D_CONV ** -0.5)
    conv_b_w = nrm(ks[9], (L, CONV_B_WIDTH, D_RNN), CONV_B_WIDTH ** -0.5)
    conv_b_b = nrm(ks[10], (L, D_RNN), 0.02)
    w_rg_a = nrm(ks[11], (L, N_RNN_HEADS, RNN_HEAD_DIM, RNN_HEAD_DIM), RNN_HEAD_DIM ** -0.5)
    b_rg_a = nrm(ks[12], (L, D_RNN), 0.02)
    w_rg_x = nrm(ks[13], (L, N_RNN_HEADS, RNN_HEAD_DIM, RNN_HEAD_DIM), RNN_HEAD_DIM ** -0.5)
    b_rg_x = nrm(ks[14], (L, D_RNN), 0.02)
    a0 = jax.random.uniform(ks[15], (L, D_RNN), f32, 0.9, 0.999)
    s0 = a0 ** (1.0 / LRU_C)
    lam = jnp.log(s0) - jnp.log1p(-s0)
    w_b_out = nrm(ks[16], (L, D_RNN, D_MODEL), D_RNN ** -0.5)
    w_o = nrm(ks[17], (L, D_MODEL, D_MODEL), D_MODEL ** -0.5)
    g_mlp = 1.0 + nrm(ks[18], (L, D_MODEL), 0.02)
    w_1 = nrm(ks[19], (L, D_MODEL, D_FF), D_MODEL ** -0.5)
    w_2 = nrm(ks[20], (L, D_FF, D_MODEL), D_FF ** -0.5)
    g_final = 1.0 + nrm(ks[21], (D_MODEL,), 0.02)
    return {"x": x, "g_mix": g_mix, "w_in": w_in, "b_in": b_in,
            "conv_a_w": conv_a_w, "conv_a_b": conv_a_b, "ln_g": ln_g, "ln_b": ln_b,
            "w_a_out": w_a_out, "conv_b_w": conv_b_w, "conv_b_b": conv_b_b,
            "w_rg_a": w_rg_a, "b_rg_a": b_rg_a, "w_rg_x": w_rg_x, "b_rg_x": b_rg_x,
            "lam": lam, "w_b_out": w_b_out, "w_o": w_o, "g_mlp": g_mlp,
            "w_1": w_1, "w_2": w_2, "g_final": g_final}


def _fwd_reference(x, g_mix, w_in, b_in, conv_a_w, conv_a_b, ln_g, ln_b, w_a_out,
              conv_b_w, conv_b_b, w_rg_a, b_rg_a, w_rg_x, b_rg_x, lam, w_b_out,
              w_o, g_mlp, w_1, w_2, g_final):
    cuts = np.cumsum(SPLITS)[:-1].tolist()
    for l in range(DEPTH):
        h = rms_norm(x, g_mix[l])
        z = jnp.einsum("bsd,de->bse", h, w_in[l]) + b_in[l]
        va, ga, xb, gb, sa, sb = jnp.split(z, cuts, axis=-1)

        u = va * jax.nn.sigmoid(ga)
        u = causal_depthwise_conv(u, conv_a_w[l], conv_a_b[l])
        u = jax.nn.silu(layer_norm(u, ln_g[l], ln_b[l]))
        y_a = jnp.einsum("bsc,cd->bsd", u, w_a_out[l])

        v = causal_depthwise_conv(xb, conv_b_w[l], conv_b_b[l])
        v = rg_lru(v, w_rg_a[l], b_rg_a[l], w_rg_x[l], b_rg_x[l], lam[l])
        y_b = jnp.einsum("bsc,cd->bsd", v * jax.nn.gelu(gb), w_b_out[l])

        m = jax.nn.sigmoid(sa) * y_a + jax.nn.sigmoid(sb) * y_b
        x = x + jnp.einsum("bsd,de->bse", m, w_o[l])

        h = rms_norm(x, g_mlp[l])
        f = jnp.square(jax.nn.relu(jnp.einsum("bsd,df->bsf", h, w_1[l])))
        x = x + jnp.einsum("bsf,fd->bsd", f, w_2[l])
    return rms_norm(x, g_final)


import jax as _jax
import jax.numpy as _jnp

TWIN_FORMAT = 'train_step'
FWD_PARAMS = ['x', 'g_mix', 'w_in', 'b_in', 'conv_a_w', 'conv_a_b', 'ln_g', 'ln_b', 'w_a_out', 'conv_b_w', 'conv_b_b', 'w_rg_a', 'b_rg_a', 'w_rg_x', 'b_rg_x', 'lam', 'w_b_out', 'w_o', 'g_mlp', 'w_1', 'w_2', 'g_final']
TWIN_WEIGHTS = ['g_mix', 'w_in', 'b_in', 'conv_a_w', 'conv_a_b', 'ln_g', 'ln_b', 'w_a_out', 'conv_b_w', 'conv_b_b', 'w_rg_a', 'b_rg_a', 'w_rg_x', 'b_rg_x', 'lam', 'w_b_out', 'w_o', 'g_mlp', 'w_1', 'w_2', 'g_final']
TWIN_DIFF_INPUT = 'x'
TWIN_INPUTS = ['x', 'g_mix', 'w_in', 'b_in', 'conv_a_w', 'conv_a_b', 'ln_g', 'ln_b', 'w_a_out', 'conv_b_w', 'conv_b_b', 'w_rg_a', 'b_rg_a', 'w_rg_x', 'b_rg_x', 'lam', 'w_b_out', 'w_o', 'g_mlp', 'w_1', 'w_2', 'g_final', 'loss_target', 'm_g_mix', 'm_w_in', 'm_b_in', 'm_conv_a_w', 'm_conv_a_b', 'm_ln_g', 'm_ln_b', 'm_w_a_out', 'm_conv_b_w', 'm_conv_b_b', 'm_w_rg_a', 'm_b_rg_a', 'm_w_rg_x', 'm_b_rg_x', 'm_lam', 'm_w_b_out', 'm_w_o', 'm_g_mlp', 'm_w_1', 'm_w_2', 'm_g_final', 'v_g_mix', 'v_w_in', 'v_b_in', 'v_conv_a_w', 'v_conv_a_b', 'v_ln_g', 'v_ln_b', 'v_w_a_out', 'v_conv_b_w', 'v_conv_b_b', 'v_w_rg_a', 'v_b_rg_a', 'v_w_rg_x', 'v_b_rg_x', 'v_lam', 'v_w_b_out', 'v_w_o', 'v_g_mlp', 'v_w_1', 'v_w_2', 'v_g_final']
TWIN_OUTPUTS = ['loss', 'grad_x', 'grad_g_mix', 'grad_w_in', 'grad_b_in', 'grad_conv_a_w', 'grad_conv_a_b', 'grad_ln_g', 'grad_ln_b', 'grad_w_a_out', 'grad_conv_b_w', 'grad_conv_b_b', 'grad_w_rg_a', 'grad_b_rg_a', 'grad_w_rg_x', 'grad_b_rg_x', 'grad_lam', 'grad_w_b_out', 'grad_w_o', 'grad_g_mlp', 'grad_w_1', 'grad_w_2', 'grad_g_final', 'delta_g_mix', 'delta_w_in', 'delta_b_in', 'delta_conv_a_w', 'delta_conv_a_b', 'delta_ln_g', 'delta_ln_b', 'delta_w_a_out', 'delta_conv_b_w', 'delta_conv_b_b', 'delta_w_rg_a', 'delta_b_rg_a', 'delta_w_rg_x', 'delta_b_rg_x', 'delta_lam', 'delta_w_b_out', 'delta_w_o', 'delta_g_mlp', 'delta_w_1', 'delta_w_2', 'delta_g_final', 'new_m_g_mix', 'new_m_w_in', 'new_m_b_in', 'new_m_conv_a_w', 'new_m_conv_a_b', 'new_m_ln_g', 'new_m_ln_b', 'new_m_w_a_out', 'new_m_conv_b_w', 'new_m_conv_b_b', 'new_m_w_rg_a', 'new_m_b_rg_a', 'new_m_w_rg_x', 'new_m_b_rg_x', 'new_m_lam', 'new_m_w_b_out', 'new_m_w_o', 'new_m_g_mlp', 'new_m_w_1', 'new_m_w_2', 'new_m_g_final', 'new_v_g_mix', 'new_v_w_in', 'new_v_b_in', 'new_v_conv_a_w', 'new_v_conv_a_b', 'new_v_ln_g', 'new_v_ln_b', 'new_v_w_a_out', 'new_v_conv_b_w', 'new_v_conv_b_b', 'new_v_w_rg_a', 'new_v_b_rg_a', 'new_v_w_rg_x', 'new_v_b_rg_x', 'new_v_lam', 'new_v_w_b_out', 'new_v_w_o', 'new_v_g_mlp', 'new_v_w_1', 'new_v_w_2', 'new_v_g_final']
TWIN_LEAF_KINDS = {'loss': 'loss', 'grad_x': 'grad_x', 'grad_g_mix': 'grad_w', 'grad_w_in': 'grad_w', 'grad_b_in': 'grad_w', 'grad_conv_a_w': 'grad_w', 'grad_conv_a_b': 'grad_w', 'grad_ln_g': 'grad_w', 'grad_ln_b': 'grad_w', 'grad_w_a_out': 'grad_w', 'grad_conv_b_w': 'grad_w', 'grad_conv_b_b': 'grad_w', 'grad_w_rg_a': 'grad_w', 'grad_b_rg_a': 'grad_w', 'grad_w_rg_x': 'grad_w', 'grad_b_rg_x': 'grad_w', 'grad_lam': 'grad_w', 'grad_w_b_out': 'grad_w', 'grad_w_o': 'grad_w', 'grad_g_mlp': 'grad_w', 'grad_w_1': 'grad_w', 'grad_w_2': 'grad_w', 'grad_g_final': 'grad_w', 'delta_g_mix': 'delta_w', 'delta_w_in': 'delta_w', 'delta_b_in': 'delta_w', 'delta_conv_a_w': 'delta_w', 'delta_conv_a_b': 'delta_w', 'delta_ln_g': 'delta_w', 'delta_ln_b': 'delta_w', 'delta_w_a_out': 'delta_w', 'delta_conv_b_w': 'delta_w', 'delta_conv_b_b': 'delta_w', 'delta_w_rg_a': 'delta_w', 'delta_b_rg_a': 'delta_w', 'delta_w_rg_x': 'delta_w', 'delta_b_rg_x': 'delta_w', 'delta_lam': 'delta_w', 'delta_w_b_out': 'delta_w', 'delta_w_o': 'delta_w', 'delta_g_mlp': 'delta_w', 'delta_w_1': 'delta_w', 'delta_w_2': 'delta_w', 'delta_g_final': 'delta_w', 'new_m_g_mix': 'new_m', 'new_m_w_in': 'new_m', 'new_m_b_in': 'new_m', 'new_m_conv_a_w': 'new_m', 'new_m_conv_a_b': 'new_m', 'new_m_ln_g': 'new_m', 'new_m_ln_b': 'new_m', 'new_m_w_a_out': 'new_m', 'new_m_conv_b_w': 'new_m', 'new_m_conv_b_b': 'new_m', 'new_m_w_rg_a': 'new_m', 'new_m_b_rg_a': 'new_m', 'new_m_w_rg_x': 'new_m', 'new_m_b_rg_x': 'new_m', 'new_m_lam': 'new_m', 'new_m_w_b_out': 'new_m', 'new_m_w_o': 'new_m', 'new_m_g_mlp': 'new_m', 'new_m_w_1': 'new_m', 'new_m_w_2': 'new_m', 'new_m_g_final': 'new_m', 'new_v_g_mix': 'new_v', 'new_v_w_in': 'new_v', 'new_v_b_in': 'new_v', 'new_v_conv_a_w': 'new_v', 'new_v_conv_a_b': 'new_v', 'new_v_ln_g': 'new_v', 'new_v_ln_b': 'new_v', 'new_v_w_a_out': 'new_v', 'new_v_conv_b_w': 'new_v', 'new_v_conv_b_b': 'new_v', 'new_v_w_rg_a': 'new_v', 'new_v_b_rg_a': 'new_v', 'new_v_w_rg_x': 'new_v', 'new_v_b_rg_x': 'new_v', 'new_v_lam': 'new_v', 'new_v_w_b_out': 'new_v', 'new_v_w_o': 'new_v', 'new_v_g_mlp': 'new_v', 'new_v_w_1': 'new_v', 'new_v_w_2': 'new_v', 'new_v_g_final': 'new_v'}


def _forward(args):
    return _fwd_reference(*[args[k] for k in FWD_PARAMS])


def _output_shape():
    out = _jax.eval_shape(lambda: _forward(_fwd_setup_inputs(0)))
    return out.shape, out.dtype

N_MICROBATCH = 1
ADAM_LR = 0.001
ADAM_B1 = 0.9
ADAM_B2 = 0.999
ADAM_EPS = 1e-08
ADAM_WD = 0.01
ADAM_STEP = 10
PER_EXAMPLE_BATCH_AXIS = {'x': 0, 'loss_target': 0}
SHARED_INPUTS = []
_WEIGHT_DTYPES = {'g_mix': _jnp.float32, 'w_in': _jnp.float32, 'b_in': _jnp.float32, 'conv_a_w': _jnp.float32, 'conv_a_b': _jnp.float32, 'ln_g': _jnp.float32, 'ln_b': _jnp.float32, 'w_a_out': _jnp.float32, 'conv_b_w': _jnp.float32, 'conv_b_b': _jnp.float32, 'w_rg_a': _jnp.float32, 'b_rg_a': _jnp.float32, 'w_rg_x': _jnp.float32, 'b_rg_x': _jnp.float32, 'lam': _jnp.float32, 'w_b_out': _jnp.float32, 'w_o': _jnp.float32, 'g_mlp': _jnp.float32, 'w_1': _jnp.float32, 'w_2': _jnp.float32, 'g_final': _jnp.float32}
MOMENT_SCALE = {'g_mix': 1.482539e-01, 'w_in': 5.667868e-02, 'b_in': 1.981032e-01, 'conv_a_w': 4.548773e-02, 'conv_a_b': 1.281244e-01, 'ln_g': 6.579137e-02, 'ln_b': 7.807261e-02, 'w_a_out': 5.024982e-02, 'conv_b_w': 8.915048e-02, 'conv_b_b': 3.899976e-01, 'w_rg_a': 1.229781e-02, 'b_rg_a': 1.767501e-02, 'w_rg_x': 2.538934e-02, 'b_rg_x': 3.698960e-02, 'lam': 4.443283e-02, 'w_b_out': 1.064940e-01, 'w_o': 1.141158e-01, 'g_mlp': 1.336823e-01, 'w_1': 6.673462e-02, 'w_2': 1.569315e-01, 'g_final': 3.317452e+01}


def _to_microbatches(a, axis):
    t = _jnp.moveaxis(a, axis, 0)
    t = t.reshape((N_MICROBATCH, t.shape[0] // N_MICROBATCH) + t.shape[1:])
    return _jnp.moveaxis(t, 1, axis + 1)


def setup_inputs(seed: int = 0) -> dict:
    inp = _fwd_setup_inputs(seed)
    key = _jax.random.fold_in(_jax.random.key(seed), 7919)
    shape, _ = _output_shape()
    out = dict(inp)
    out["loss_target"] = _jax.random.normal(_jax.random.fold_in(key, 0), shape, _jnp.float32)
    for i, name in enumerate(TWIN_WEIGHTS):
        w = inp[name].astype(_jnp.float32)
        if MOMENT_SCALE is None:
            s = _jnp.sqrt(_jnp.mean(_jnp.square(w)) + 1e-30)
        else:
            s = MOMENT_SCALE[name]
        km, kv = _jax.random.split(_jax.random.fold_in(key, i + 1))
        out[name] = w
        out["m_" + name] = s * _jax.random.normal(km, w.shape, _jnp.float32)
        out["v_" + name] = (s * s) * _jax.random.uniform(kv, w.shape, _jnp.float32, 0.5, 1.5)
    if N_MICROBATCH > 1:
        for name, axis in PER_EXAMPLE_BATCH_AXIS.items():
            out[name] = _to_microbatches(out[name], axis)
    return {'x': out['x'], 'g_mix': out['g_mix'], 'w_in': out['w_in'], 'b_in': out['b_in'], 'conv_a_w': out['conv_a_w'], 'conv_a_b': out['conv_a_b'], 'ln_g': out['ln_g'], 'ln_b': out['ln_b'], 'w_a_out': out['w_a_out'], 'conv_b_w': out['conv_b_w'], 'conv_b_b': out['conv_b_b'], 'w_rg_a': out['w_rg_a'], 'b_rg_a': out['b_rg_a'], 'w_rg_x': out['w_rg_x'], 'b_rg_x': out['b_rg_x'], 'lam': out['lam'], 'w_b_out': out['w_b_out'], 'w_o': out['w_o'], 'g_mlp': out['g_mlp'], 'w_1': out['w_1'], 'w_2': out['w_2'], 'g_final': out['g_final'], 'loss_target': out['loss_target'], 'm_g_mix': out['m_g_mix'], 'm_w_in': out['m_w_in'], 'm_b_in': out['m_b_in'], 'm_conv_a_w': out['m_conv_a_w'], 'm_conv_a_b': out['m_conv_a_b'], 'm_ln_g': out['m_ln_g'], 'm_ln_b': out['m_ln_b'], 'm_w_a_out': out['m_w_a_out'], 'm_conv_b_w': out['m_conv_b_w'], 'm_conv_b_b': out['m_conv_b_b'], 'm_w_rg_a': out['m_w_rg_a'], 'm_b_rg_a': out['m_b_rg_a'], 'm_w_rg_x': out['m_w_rg_x'], 'm_b_rg_x': out['m_b_rg_x'], 'm_lam': out['m_lam'], 'm_w_b_out': out['m_w_b_out'], 'm_w_o': out['m_w_o'], 'm_g_mlp': out['m_g_mlp'], 'm_w_1': out['m_w_1'], 'm_w_2': out['m_w_2'], 'm_g_final': out['m_g_final'], 'v_g_mix': out['v_g_mix'], 'v_w_in': out['v_w_in'], 'v_b_in': out['v_b_in'], 'v_conv_a_w': out['v_conv_a_w'], 'v_conv_a_b': out['v_conv_a_b'], 'v_ln_g': out['v_ln_g'], 'v_ln_b': out['v_ln_b'], 'v_w_a_out': out['v_w_a_out'], 'v_conv_b_w': out['v_conv_b_w'], 'v_conv_b_b': out['v_conv_b_b'], 'v_w_rg_a': out['v_w_rg_a'], 'v_b_rg_a': out['v_b_rg_a'], 'v_w_rg_x': out['v_w_rg_x'], 'v_b_rg_x': out['v_b_rg_x'], 'v_lam': out['v_lam'], 'v_w_b_out': out['v_w_b_out'], 'v_w_o': out['v_w_o'], 'v_g_mlp': out['v_g_mlp'], 'v_w_1': out['v_w_1'], 'v_w_2': out['v_w_2'], 'v_g_final': out['v_g_final']}


def _loss(weights, diff, rest, loss_target):
    with _jax.named_scope("forward"):
        args = {**rest, TWIN_DIFF_INPUT: diff, **{k: w.astype(_WEIGHT_DTYPES[k]) for k, w in weights.items()}}
        y = _forward(args)
    with _jax.named_scope("loss_head"):
        err = _jnp.square(y.astype(_jnp.float32) - loss_target)
        return 0.5 * _jnp.sum(_jnp.mean(err, axis=-1)) if err.ndim else 0.5 * err


def _adamw(w, g, m, v):
    m = ADAM_B1 * m + (1.0 - ADAM_B1) * g
    v = ADAM_B2 * v + (1.0 - ADAM_B2) * _jnp.square(g)
    m_hat = m / (1.0 - ADAM_B1 ** ADAM_STEP)
    v_hat = v / (1.0 - ADAM_B2 ** ADAM_STEP)
    delta = -ADAM_LR * (m_hat / (_jnp.sqrt(v_hat) + ADAM_EPS) + ADAM_WD * w)
    return delta, m, v


def reference(x, g_mix, w_in, b_in, conv_a_w, conv_a_b, ln_g, ln_b, w_a_out, conv_b_w, conv_b_b, w_rg_a, b_rg_a, w_rg_x, b_rg_x, lam, w_b_out, w_o, g_mlp, w_1, w_2, g_final, loss_target, m_g_mix, m_w_in, m_b_in, m_conv_a_w, m_conv_a_b, m_ln_g, m_ln_b, m_w_a_out, m_conv_b_w, m_conv_b_b, m_w_rg_a, m_b_rg_a, m_w_rg_x, m_b_rg_x, m_lam, m_w_b_out, m_w_o, m_g_mlp, m_w_1, m_w_2, m_g_final, v_g_mix, v_w_in, v_b_in, v_conv_a_w, v_conv_a_b, v_ln_g, v_ln_b, v_w_a_out, v_conv_b_w, v_conv_b_b, v_w_rg_a, v_b_rg_a, v_w_rg_x, v_b_rg_x, v_lam, v_w_b_out, v_w_o, v_g_mlp, v_w_1, v_w_2, v_g_final):
    given = dict(x=x, g_mix=g_mix, w_in=w_in, b_in=b_in, conv_a_w=conv_a_w, conv_a_b=conv_a_b, ln_g=ln_g, ln_b=ln_b, w_a_out=w_a_out, conv_b_w=conv_b_w, conv_b_b=conv_b_b, w_rg_a=w_rg_a, b_rg_a=b_rg_a, w_rg_x=w_rg_x, b_rg_x=b_rg_x, lam=lam, w_b_out=w_b_out, w_o=w_o, g_mlp=g_mlp, w_1=w_1, w_2=w_2, g_final=g_final, loss_target=loss_target, m_g_mix=m_g_mix, m_w_in=m_w_in, m_b_in=m_b_in, m_conv_a_w=m_conv_a_w, m_conv_a_b=m_conv_a_b, m_ln_g=m_ln_g, m_ln_b=m_ln_b, m_w_a_out=m_w_a_out, m_conv_b_w=m_conv_b_w, m_conv_b_b=m_conv_b_b, m_w_rg_a=m_w_rg_a, m_b_rg_a=m_b_rg_a, m_w_rg_x=m_w_rg_x, m_b_rg_x=m_b_rg_x, m_lam=m_lam, m_w_b_out=m_w_b_out, m_w_o=m_w_o, m_g_mlp=m_g_mlp, m_w_1=m_w_1, m_w_2=m_w_2, m_g_final=m_g_final, v_g_mix=v_g_mix, v_w_in=v_w_in, v_b_in=v_b_in, v_conv_a_w=v_conv_a_w, v_conv_a_b=v_conv_a_b, v_ln_g=v_ln_g, v_ln_b=v_ln_b, v_w_a_out=v_w_a_out, v_conv_b_w=v_conv_b_w, v_conv_b_b=v_conv_b_b, v_w_rg_a=v_w_rg_a, v_b_rg_a=v_b_rg_a, v_w_rg_x=v_w_rg_x, v_b_rg_x=v_b_rg_x, v_lam=v_lam, v_w_b_out=v_w_b_out, v_w_o=v_w_o, v_g_mlp=v_g_mlp, v_w_1=v_w_1, v_w_2=v_w_2, v_g_final=v_g_final)
    weights = {n: given[n] for n in TWIN_WEIGHTS}
    shared = {n: given[n] for n in SHARED_INPUTS}
    per_example = {n: given[n] for n in ['x']}
    grad_fn = _jax.value_and_grad(_loss, argnums=(0, 1))

    def one_microbatch(ex, loss_target):
        ex = dict(ex)
        diff = ex.pop(TWIN_DIFF_INPUT)
        return grad_fn(weights, diff, {**shared, **ex}, loss_target)

    if N_MICROBATCH == 1:
        loss, (grad_w, grad_x) = one_microbatch(per_example, given["loss_target"])
    else:
        def body(carry, xs):
            loss_sum, grad_sum = carry
            l_k, (gw_k, gx_k) = one_microbatch(xs[0], xs[1])
            with _jax.named_scope("update"):
                return (loss_sum + l_k, _jax.tree.map(_jnp.add, grad_sum, gw_k)), gx_k

        init = (_jnp.zeros((), _jnp.float32), _jax.tree.map(_jnp.zeros_like, weights))
        (loss, grad_w), grad_x = _jax.lax.scan(body, init, (per_example, given["loss_target"]))
    with _jax.named_scope("update"):
        delta_w, new_m, new_v = {}, {}, {}
        for n in TWIN_WEIGHTS:
            delta_w[n], new_m[n], new_v[n] = _adamw(weights[n], grad_w[n], given["m_" + n], given["v_" + n])
    return (loss, grad_x, *[grad_w[n] for n in TWIN_WEIGHTS], *[delta_w[n] for n in TWIN_WEIGHTS],
            *[new_m[n] for n in TWIN_WEIGHTS], *[new_v[n] for n in TWIN_WEIGHTS])
```

```python
import functools

import jax
import jax.numpy as jnp
from jax import lax
from jax.experimental import pallas as pl
from jax.experimental.pallas import tpu as pltpu

F32 = jnp.float32
BF16 = jnp.bfloat16
MESH = pl.DeviceIdType.MESH

EPS = 1e-6
LRU_C = 8.0
ADAM_LR, ADAM_B1, ADAM_B2, ADAM_EPS, ADAM_WD, ADAM_STEP = 0.001, 0.9, 0.999, 1e-08, 0.01, 10

N_CHIPS = 4
HEADS_PER_GROUP = 4
VMEM_LIMIT = 56 * 1024 * 1024


def _cp(**kw):
    return pltpu.CompilerParams(vmem_limit_bytes=VMEM_LIMIT, **kw)


def _sig(x):
    return 1.0 / (1.0 + jnp.exp(-x))


def _gelu(x):
    t = jnp.tanh(0.7978845608028654 * (x + 0.044715 * x * x * x))
    return 0.5 * x * (1.0 + t), t


def _gelu_grad(x, t):
    dt = (1.0 - t * t) * 0.7978845608028654 * (1.0 + 3.0 * 0.044715 * x * x)
    return 0.5 * (1.0 + t) + 0.5 * x * dt


def _rms(xf, g):
    r = lax.rsqrt(jnp.mean(xf * xf, axis=-1, keepdims=True) + EPS)
    return xf * r * g, r


def _rms_bwd(xf, g, r, dh):
    dyg = dh * g
    dx = r * (dyg - xf * (r * r) * jnp.mean(dyg * xf, axis=-1, keepdims=True))
    return dx, dh * xf * r


def _ln_silu(u, g, b):
    mu = jnp.mean(u, axis=-1, keepdims=True)
    uc = u - mu
    rstd = lax.rsqrt(jnp.mean(uc * uc, axis=-1, keepdims=True) + EPS)
    uh = uc * rstd
    u2 = uh * g + b
    s = _sig(u2)
    return u2 * s, uh, rstd, u2, s


_DIMS = {"nn": (((1,), (0,)), ((), ())), "nt": (((1,), (1,)), ((), ())), "tn": (((0,), (0,)), ((), ()))}


def _mm(name, mode, grid, a_ins, a_fn, b_in, e_ins, epi, outs, acc_shape, cache_a=None, alias=(), extra_scratch=()):
    ni, nj, nk = grid
    na, ne, no = len(a_ins), len(e_ins), len(outs)
    assert cache_a is None or nk == 1
    n_fixed = (nk > 1) + (cache_a is not None)

    def body(*refs):
        a_refs = refs[:na]
        b_ref = refs[na]
        e_refs = refs[na + 1:na + 1 + ne]
        out_refs = refs[na + 1 + ne + len(alias):na + 1 + ne + len(alias) + no]
        scratch = refs[na + 1 + ne + len(alias) + no:]
        extra = scratch[n_fixed:]
        i, j, k = pl.program_id(0), pl.program_id(1), pl.program_id(2)
        if cache_a is not None:
            cache_ref = scratch[n_fixed - 1]

            @pl.when(j == 0)
            def _():
                cache_ref[...] = a_fn(a_refs, out_refs, i, j, k)

            a = cache_ref[...]
        else:
            a = a_fn(a_refs, out_refs, i, j, k)
        prod = lax.dot_general(a, b_ref[...], _DIMS[mode], preferred_element_type=F32)
        if nk == 1:
            epi(prod, e_refs, out_refs, i, j, extra)
        else:
            acc_ref = scratch[0]

            @pl.when(k == 0)
            def _():
                acc_ref[...] = prod

            @pl.when(k > 0)
            def _():
                acc_ref[...] += prod

            @pl.when(k == nk - 1)
            def _():
                epi(acc_ref[...], e_refs, out_refs, i, j, extra)

    scratch_shapes = []
    if nk > 1:
        scratch_shapes.append(pltpu.VMEM(acc_shape, F32))
    if cache_a is not None:
        scratch_shapes.append(pltpu.VMEM(cache_a, BF16))
    ins = list(a_ins) + [b_in] + list(e_ins) + [(arr, pl.BlockSpec(memory_space=pl.ANY)) for arr, _ in alias]
    first_alias = na + 1 + ne
    res = pl.pallas_call(
        body, name=name, grid=grid,
        in_specs=[s for _, s in ins], out_specs=[s for _, s in outs],
        out_shape=[o for o, _ in outs], scratch_shapes=scratch_shapes + list(extra_scratch),
        input_output_aliases={first_alias + n: o for n, (_, o) in enumerate(alias)},
        compiler_params=_cp(dimension_semantics=("arbitrary", "arbitrary", "arbitrary")),
    )(*[a for a, _ in ins])
    return res


def _bs(shape, fn):
    return pl.BlockSpec(shape, fn)


def _sds(shape, dt):
    return jax.ShapeDtypeStruct(shape, dt)


def _acc_rows(ref, val, first):
    @pl.when(first)
    def _():
        ref[...] = val

    @pl.when(jnp.logical_not(first))
    def _():
        ref[...] += val


def _fwd_norm_mm(name, x, g, w, bias, tm, tn):
    T, D = x.shape
    N = w.shape[1]

    def a_fn(a_refs, out_refs, i, j, k):
        h, _ = _rms(a_refs[0][...], a_refs[1][...])
        hb = h.astype(BF16)
        out_refs[1][...] = hb
        return hb

    def epi(acc, e_refs, out_refs, i, j, extra):
        if bias is not None:
            acc = acc + e_refs[0][...]
        out_refs[0][...] = acc.astype(BF16)

    e_ins = [] if bias is None else [(bias, _bs((1, tn), lambda i, j, k: (0, j)))]
    return _mm(name, "nn", (T // tm, N // tn, 1),
               [(x, _bs((tm, D), lambda i, j, k: (i, 0))), (g, _bs((1, D), lambda i, j, k: (0, 0)))], a_fn,
               (w, _bs((D, tn), lambda i, j, k: (0, j))), e_ins, epi,
               [(_sds((T, N), BF16), _bs((tm, tn), lambda i, j, k: (i, j))),
                (_sds((T, D), BF16), _bs((tm, D), lambda i, j, k: (i, 0)))],
               None, cache_a=(tm, D))


def _fwd_ya(u1, ln_g, ln_b, w, tm):
    T, C = u1.shape
    N = w.shape[1]

    def a_fn(a_refs, out_refs, i, j, k):
        u3 = _ln_silu(a_refs[0][...].astype(F32), a_refs[1][...], a_refs[2][...])[0]
        return u3.astype(BF16)

    def epi(acc, e_refs, out_refs, i, j, extra):
        out_refs[0][...] = acc.astype(BF16)

    row = _bs((1, C), lambda i, j, k: (0, 0))
    return _mm("fwd_ya", "nn", (T // tm, 1, 1),
               [(u1, _bs((tm, C), lambda i, j, k: (i, 0))), (ln_g, row), (ln_b, row)], a_fn,
               (w, _bs((C, N), lambda i, j, k: (0, 0))), [], epi,
               [(_sds((T, N), BF16), _bs((tm, N), lambda i, j, k: (i, 0)))], None)[0]


def _fwd_yb(h, z, gb_blk, w, tm, tk):
    T, C = h.shape
    N = w.shape[1]

    def a_fn(a_refs, out_refs, i, j, k):
        ge, _ = _gelu(a_refs[1][...].astype(F32))
        return (a_refs[0][...].astype(F32) * ge).astype(BF16)

    def epi(acc, e_refs, out_refs, i, j, extra):
        out_refs[0][...] = acc.astype(BF16)

    return _mm("fwd_yb", "nn", (T // tm, 1, C // tk),
               [(h, _bs((tm, tk), lambda i, j, k: (i, k))), (z, _bs((tm, tk), lambda i, j, k: (i, gb_blk + k)))], a_fn,
               (w, _bs((tk, N), lambda i, j, k: (k, 0))), [], epi,
               [(_sds((T, N), BF16), _bs((tm, N), lambda i, j, k: (i, 0)))], (tm, N))[0]


def _fwd_x1(x, ya, yb, z, sa_blk, w, tm):
    T, D = x.shape

    def a_fn(a_refs, out_refs, i, j, k):
        ya_, yb_, sa_, sb_ = (r[...].astype(F32) for r in a_refs)
        return (_sig(sa_) * ya_ + _sig(sb_) * yb_).astype(BF16)

    def epi(acc, e_refs, out_refs, i, j, extra):
        out_refs[0][...] = e_refs[0][...] + acc

    t = _bs((tm, D), lambda i, j, k: (i, 0))
    return _mm("fwd_x1", "nn", (T // tm, 1, 1),
               [(ya, t), (yb, t), (z, _bs((tm, D), lambda i, j, k: (i, sa_blk))),
                (z, _bs((tm, D), lambda i, j, k: (i, sa_blk + 1)))], a_fn,
               (w, _bs((D, D), lambda i, j, k: (0, 0))), [(x, t)], epi,
               [(_sds((T, D), F32), t)], None)[0]


def _fwd_x2(x1, fp, w, tm, tk):
    T, D = x1.shape
    Fd = fp.shape[1]

    def a_fn(a_refs, out_refs, i, j, k):
        f = jnp.maximum(a_refs[0][...].astype(F32), 0.0)
        return (f * f).astype(BF16)

    def epi(acc, e_refs, out_refs, i, j, extra):
        out_refs[0][...] = e_refs[0][...] + acc

    t = _bs((tm, D), lambda i, j, k: (i, 0))
    return _mm("fwd_x2", "nn", (T // tm, 1, Fd // tk),
               [(fp, _bs((tm, tk), lambda i, j, k: (i, k)))], a_fn,
               (w, _bs((tk, D), lambda i, j, k: (k, 0))), [(x1, t)], epi,
               [(_sds((T, D), F32), t)], (tm, D))[0]


def _loss_head(x, g, target, tm):
    T, D = x.shape

    def body(x_ref, g_ref, t_ref, loss_ref, dx_ref, dxb_ref, dg_ref):
        i = pl.program_id(0)
        xf, gv = x_ref[...], g_ref[...]
        y, r = _rms(xf, gv)
        err = y - t_ref[...]
        part = 0.5 * jnp.sum(jnp.mean(err * err, axis=-1, keepdims=True), axis=0, keepdims=True)
        dx, dg_rows = _rms_bwd(xf, gv, r, err * (1.0 / D))
        dx_ref[...] = dx
        dxb_ref[...] = dx.astype(BF16)
        _acc_rows(loss_ref, jnp.broadcast_to(part, (1, 128)), i == 0)
        _acc_rows(dg_ref, jnp.sum(dg_rows, axis=0, keepdims=True), i == 0)

    t = _bs((tm, D), lambda i: (i, 0))
    row = _bs((1, D), lambda i: (0, 0))
    return pl.pallas_call(
        body, name="loss_head", grid=(T // tm,), in_specs=[t, row, t],
        out_specs=[_bs((1, 128), lambda i: (0, 0)), t, t, row],
        out_shape=[_sds((1, 128), F32), _sds((T, D), F32), _sds((T, D), BF16), _sds((1, D), F32)],
        compiler_params=_cp(dimension_semantics=("arbitrary",)),
    )(x, g, target)


def _adamw(name, w, g, m, v, tr):
    rows, cols = w.shape
    d1 = 1.0 - ADAM_B1 ** ADAM_STEP
    d2 = 1.0 - ADAM_B2 ** ADAM_STEP

    def body(w_ref, g_ref, m_ref, v_ref, d_ref, mo_ref, vo_ref):
        gv = g_ref[...]
        mn = ADAM_B1 * m_ref[...] + (1.0 - ADAM_B1) * gv
        vn = ADAM_B2 * v_ref[...] + (1.0 - ADAM_B2) * (gv * gv)
        d_ref[...] = -ADAM_LR * ((mn / d1) / (jnp.sqrt(vn / d2) + ADAM_EPS) + ADAM_WD * w_ref[...])
        mo_ref[...] = mn
        vo_ref[...] = vn

    t = _bs((tr, cols), lambda i: (i, 0))
    return pl.pallas_call(
        body, name=name, grid=(rows // tr,), in_specs=[t] * 4, out_specs=[t] * 3,
        out_shape=[_sds((rows, cols), F32)] * 3,
        compiler_params=_cp(dimension_semantics=("arbitrary",)),
    )(w, g, m, v)


def _ident(a_refs, out_refs, i, j, k):
    return a_refs[0][...]


def _bwd_dw(name, act, dy, ti, tj, tm, a_fn=None, a_extra=(), shard_cols=None):
    T, J = dy.shape
    I = act.shape[1]

    def epi(acc, e_refs, out_refs, i, j, extra):
        out_refs[0][...] = acc.astype(BF16)

    if shard_cols is None:
        out = (_sds((I, J), BF16), _bs((ti, tj), lambda i, j, k: (i, j)))
    else:
        per = shard_cols // tj
        assert ti == I // 2 and per * tj == shard_cols
        out = (_sds((J // shard_cols, 2, ti, shard_cols), BF16),
               _bs((None, None, ti, tj), lambda i, j, k: (j // per, i, 0, j % per)))
    a_ins = [(act, _bs((tm, ti), lambda i, j, k: (k, i)))] + list(a_extra)
    return _mm(name, "tn", (I // ti, J // tj, T // tm), a_ins, a_fn or _ident,
               (dy, _bs((tm, tj), lambda i, j, k: (k, j))), [], epi, [out], (ti, tj))[0]


def _bwd_df(dxb, w2, fp, tm, tn):
    T, D = dxb.shape
    Fd = w2.shape[0]

    def epi(acc, e_refs, out_refs, i, j, extra):
        out_refs[0][...] = (acc * (2.0 * jnp.maximum(e_refs[0][...].astype(F32), 0.0))).astype(BF16)

    t = _bs((tm, tn), lambda i, j, k: (i, j))
    return _mm("bwd_df", "nt", (T // tm, Fd // tn, 1), [(dxb, _bs((tm, D), lambda i, j, k: (i, 0)))], _ident,
               (w2, _bs((tn, D), lambda i, j, k: (j, 0))), [(fp, t)], epi, [(_sds((T, Fd), BF16), t)], None)[0]


def _bwd_norm(name, dy, w, x, g, dres, tm, tk, colsum=False):
    T, K = dy.shape
    D = w.shape[0]
    nk = K // tk

    def a_fn(a_refs, out_refs, i, j, k):
        a = a_refs[0][...]
        if colsum:
            s = jnp.sum(a.astype(F32), axis=0, keepdims=True)

            @pl.when(i == 0)
            def _():
                out_refs[3][k] = s

            @pl.when(i > 0)
            def _():
                out_refs[3][k] += s
        return a

    def epi(acc, e_refs, out_refs, i, j, extra):
        xf, gv = e_refs[0][...], e_refs[1][...]
        r = lax.rsqrt(jnp.mean(xf * xf, axis=-1, keepdims=True) + EPS)
        dx, dg_rows = _rms_bwd(xf, gv, r, acc)
        dx = dx + e_refs[2][...]
        out_refs[0][...] = dx
        out_refs[1][...] = dx.astype(BF16)
        _acc_rows(out_refs[2], jnp.sum(dg_rows, axis=0, keepdims=True), i == 0)

    t = _bs((tm, D), lambda i, j, k: (i, 0))
    row = _bs((1, D), lambda i, j, k: (0, 0))
    outs = [(_sds((T, D), F32), t), (_sds((T, D), BF16), t), (_sds((1, D), F32), row)]
    if colsum:
        outs.append((_sds((nk, 1, tk), F32), _bs((nk, 1, tk), lambda i, j, k: (0, 0, 0))))
    return _mm(name, "nt", (T // tm, 1, nk), [(dy, _bs((tm, tk), lambda i, j, k: (i, k)))], a_fn,
               (w, _bs((D, tk), lambda i, j, k: (0, k))), [(x, t), (g, row), (dres, t)], epi, outs, (tm, D))


def _bwd_dm(dxb, w_o, ya, yb, z, sa_blk, tm):
    T, D = dxb.shape

    def epi(acc, e_refs, out_refs, i, j, extra):
        ya_, yb_, sa_, sb_ = (r[...].astype(F32) for r in e_refs)
        ga, gb = _sig(sa_), _sig(sb_)
        out_refs[0][...] = (acc * ga).astype(BF16)
        out_refs[1][...] = (acc * gb).astype(BF16)
        stage = extra[0]
        stage[:, 0:D] = (acc * ya_ * ga * (1.0 - ga)).astype(BF16)
        stage[:, D:2 * D] = (acc * yb_ * gb * (1.0 - gb)).astype(BF16)
        pltpu.sync_copy(stage, out_refs[2].at[pl.ds(pl.multiple_of(i * tm, tm), tm), pl.ds(sa_blk * D, 2 * D)])

    t = _bs((tm, D), lambda i, j, k: (i, 0))
    return _mm("bwd_dm", "nt", (T // tm, 1, 1), [(dxb, t)], _ident, (w_o, _bs((D, D), lambda i, j, k: (0, 0))),
               [(ya, t), (yb, t), (z, _bs((tm, D), lambda i, j, k: (i, sa_blk))),
                (z, _bs((tm, D), lambda i, j, k: (i, sa_blk + 1)))], epi,
               [(_sds((T, D), BF16), t), (_sds((T, D), BF16), t),
                (_sds(z.shape, BF16), pl.BlockSpec(memory_space=pl.ANY))], None,
               extra_scratch=[pltpu.VMEM((tm, 2 * D), BF16)])


def _bwd_du3(dya, w, u1, ln_g, ln_b, tm):
    T, D = dya.shape
    C = w.shape[0]

    def epi(acc, e_refs, out_refs, i, j, extra):
        gv = e_refs[1][...]
        _, uh, rstd, u2, s = _ln_silu(e_refs[0][...].astype(F32), gv, e_refs[2][...])
        du2 = acc * (s * (1.0 + u2 * (1.0 - s)))
        duh = du2 * gv
        out_refs[0][...] = rstd * (duh - jnp.mean(duh, axis=-1, keepdims=True)
                                   - uh * jnp.mean(duh * uh, axis=-1, keepdims=True))
        _acc_rows(out_refs[1], jnp.sum(du2 * uh, axis=0, keepdims=True), i == 0)
        _acc_rows(out_refs[2], jnp.sum(du2, axis=0, keepdims=True), i == 0)

    t = _bs((tm, C), lambda i, j, k: (i, 0))
    row = _bs((1, C), lambda i, j, k: (0, 0))
    return _mm("bwd_du3", "nt", (T // tm, 1, 1), [(dya, _bs((tm, D), lambda i, j, k: (i, 0)))], _ident,
               (w, _bs((C, D), lambda i, j, k: (0, 0))), [(u1, t), (ln_g, row), (ln_b, row)], epi,
               [(_sds((T, C), F32), t), (_sds((1, C), F32), row), (_sds((1, C), F32), row)], None)


def _bwd_dp(dyb, w, h, z, dz, gb_blk, tm, tn):
    T, D = dyb.shape
    R = w.shape[0]

    def epi(acc, e_refs, out_refs, i, j, extra):
        gbv = e_refs[1][...].astype(F32)
        ge, th = _gelu(gbv)
        out_refs[0][...] = acc * ge
        out_refs[1][...] = (acc * e_refs[0][...].astype(F32) * _gelu_grad(gbv, th)).astype(BF16)

    t = _bs((tm, tn), lambda i, j, k: (i, j))
    tz = _bs((tm, tn), lambda i, j, k: (i, gb_blk + j))
    return _mm("bwd_dp", "nt", (T // tm, R // tn, 1), [(dyb, _bs((tm, D), lambda i, j, k: (i, 0)))], _ident,
               (w, _bs((tn, D), lambda i, j, k: (j, 0))), [(h, t), (z, tz)], epi,
               [(_sds((T, R), F32), t), (_sds(dz.shape, BF16), tz)], None, cache_a=None, alias=[(dz, 1)])


CONV_ROWS = 32


def _shifted_taps(x, halo, shifts, fn):
    n = CONV_ROWS + halo
    by_r = {}
    for k, s in shifts:
        by_r.setdefault(s % 8, []).append((k, s))
    for r in sorted(by_r):
        xr = x if r == 0 else pltpu.roll(x, n - r, 0)
        for k, s in by_r[r]:
            q = s - r
            fn(k, xr[q:q + CONV_ROWS])


def _conv_fwd(name, z, blk0, gate_blk0, w_pad, bias, taps, seq, tc, out_dtype):
    T = z.shape[0]
    C = w_pad.shape[1]
    nb, nj = T // seq, C // tc
    pad = 8 * ((taps - 1 + 7) // 8)
    halo = pad
    shifts = [(k, pad - (taps - 1) + k) for k in range(taps)]
    glu = gate_blk0 is not None

    def body(*refs):
        if glu:
            v_ref, g_ref, w_ref, b_ref, o_ref, p_ref = refs
        else:
            v_ref, w_ref, b_ref, o_ref, p_ref = refs
        p_ref[pl.ds(0, pad), :] = jnp.zeros((pad, tc), F32)
        u = v_ref[...].astype(F32)
        if glu:
            u = u * _sig(g_ref[...].astype(F32))
        p_ref[pl.ds(pad, seq), :] = u

        def step(c, _):
            base = pl.multiple_of(c * CONV_ROWS, CONV_ROWS)
            x = p_ref[pl.ds(base, CONV_ROWS + halo), :]
            acc = [jnp.zeros((CONV_ROWS, tc), F32) + b_ref[...]]

            def tap(k, xs):
                acc[0] = acc[0] + w_ref[k:k + 1, :] * xs

            _shifted_taps(x, halo, shifts, tap)
            o_ref[pl.ds(base, CONV_ROWS), :] = acc[0].astype(out_dtype)
            return 0

        lax.fori_loop(0, seq // CONV_ROWS, step, 0)

    zin = [(z, _bs((seq, tc), lambda b, j: (b, blk0 + j)))]
    if glu:
        zin.append((z, _bs((seq, tc), lambda b, j: (b, gate_blk0 + j))))
    ins = zin + [(w_pad, _bs((w_pad.shape[0], tc), lambda b, j: (0, j))), (bias, _bs((1, tc), lambda b, j: (0, j)))]
    return pl.pallas_call(
        body, name=name, grid=(nb, nj), in_specs=[s for _, s in ins],
        out_specs=_bs((seq, tc), lambda b, j: (b, j)), out_shape=_sds((T, C), out_dtype),
        scratch_shapes=[pltpu.VMEM((seq + pad, tc), F32)],
        compiler_params=_cp(dimension_semantics=("arbitrary", "arbitrary")),
    )(*[a for a, _ in ins])


def _conv_bwd(name, dy, z, dz, blk0, gate_blk0, w_pad, taps, seq, tc):
    T = z.shape[0]
    C = w_pad.shape[1]
    nb, nj = T // seq, C // tc
    kp = w_pad.shape[0]
    pad = 8 * ((taps - 1 + 7) // 8)
    halo = pad
    sh_du = [(k, taps - 1 - k) for k in range(taps)]
    sh_dw = [(k, pad - (taps - 1) + k) for k in range(taps)]
    glu = gate_blk0 is not None

    def body(*refs):
        if glu:
            dy_ref, v_ref, g_ref, w_ref, _dz_in, dz_out, dw_ref, db_ref, pdy, pu, du_s, wacc, ob, ob2 = refs
        else:
            dy_ref, v_ref, w_ref, _dz_in, dz_out, dw_ref, db_ref, pdy, pu, du_s, wacc, ob = refs
        j = pl.program_id(0)
        b = pl.program_id(1)
        pdy[pl.ds(seq, pad), :] = jnp.zeros((pad, tc), F32)
        pdy[pl.ds(0, seq), :] = dy_ref[...].astype(F32)
        pu[pl.ds(0, pad), :] = jnp.zeros((pad, tc), F32)
        v = v_ref[...].astype(F32)
        if glu:
            sg = _sig(g_ref[...].astype(F32))
            pu[pl.ds(pad, seq), :] = v * sg
        else:
            pu[pl.ds(pad, seq), :] = v
        wacc[...] = jnp.zeros(wacc.shape, F32)

        def step(c, dbacc):
            base = pl.multiple_of(c * CONV_ROWS, CONV_ROWS)
            xdy = pdy[pl.ds(base, CONV_ROWS + halo), :]
            acc = [jnp.zeros((CONV_ROWS, tc), F32)]

            def tap(k, xs):
                acc[0] = acc[0] + w_ref[k:k + 1, :] * xs

            _shifted_taps(xdy, halo, sh_du, tap)
            du_s[pl.ds(base, CONV_ROWS), :] = acc[0]
            dyc = xdy[0:CONV_ROWS]
            xu = pu[pl.ds(base, CONV_ROWS + halo), :]

            def wtap(k, xs):
                p = dyc * xs
                s8 = p[0:8]
                for m in range(1, CONV_ROWS // 8):
                    s8 = s8 + p[8 * m:8 * m + 8]
                wacc[pl.ds(8 * k, 8), :] += s8

            _shifted_taps(xu, halo, sh_dw, wtap)
            d8 = dyc[0:8]
            for m in range(1, CONV_ROWS // 8):
                d8 = d8 + dyc[8 * m:8 * m + 8]
            return dbacc + d8

        dbacc = lax.fori_loop(0, seq // CONV_ROWS, step, jnp.zeros((8, tc), F32))
        du = du_s[...]
        rows = pl.ds(pl.multiple_of(b * seq, seq), seq)
        if glu:
            ob[...] = (du * sg).astype(BF16)
            ob2[...] = (du * v * sg * (1.0 - sg)).astype(BF16)
            pltpu.sync_copy(ob2, dz_out.at[rows, pl.ds(pl.multiple_of((gate_blk0 + j) * tc, tc), tc)])
        else:
            ob[...] = du.astype(BF16)
        pltpu.sync_copy(ob, dz_out.at[rows, pl.ds(pl.multiple_of((blk0 + j) * tc, tc), tc)])
        dw = jnp.sum(wacc[...].reshape(kp, 8, tc), axis=1)
        _acc_rows(dw_ref, dw, b == 0)
        _acc_rows(db_ref, jnp.sum(dbacc, axis=0, keepdims=True), b == 0)

    zin = [(z, _bs((seq, tc), lambda j, b: (b, blk0 + j)))]
    if glu:
        zin.append((z, _bs((seq, tc), lambda j, b: (b, gate_blk0 + j))))
    ins = [(dy, _bs((seq, tc), lambda j, b: (b, j)))] + zin + [(w_pad, _bs((kp, tc), lambda j, b: (0, j))),
                                                               (dz, pl.BlockSpec(memory_space=pl.ANY))]
    dz_idx = len(ins) - 1
    out_specs = [pl.BlockSpec(memory_space=pl.ANY), _bs((kp, tc), lambda j, b: (0, j)), _bs((1, tc), lambda j, b: (0, j))]
    out_shape = [_sds(dz.shape, dz.dtype), _sds((kp, C), F32), _sds((1, C), F32)]
    stage = [pltpu.VMEM((seq, tc), BF16)] * (2 if glu else 1)
    return pl.pallas_call(
        body, name=name, grid=(nj, nb), in_specs=[s for _, s in ins], out_specs=out_specs, out_shape=out_shape,
        input_output_aliases={dz_idx: 0},
        scratch_shapes=[pltpu.VMEM((seq + pad, tc), F32), pltpu.VMEM((seq + pad, tc), F32),
                        pltpu.VMEM((seq, tc), F32), pltpu.VMEM((8 * kp, tc), F32)] + stage,
        compiler_params=_cp(dimension_semantics=("arbitrary", "arbitrary")),
    )(*[a for a, _ in ins])


RG_ROWS = 256


def _softplus_neg(lam):
    return jnp.maximum(-lam, 0.0) + jnp.log(1.0 + jnp.exp(-jnp.abs(lam)))


def _gates(v0c, wa_ref, wx_ref, ba, bx, sp, first_row):
    vb = v0c.astype(BF16)
    r = _sig(jnp.dot(vb, wa_ref[...], preferred_element_type=F32) + ba)
    i = _sig(jnp.dot(vb, wx_ref[...], preferred_element_type=F32) + bx)
    la = -LRU_C * r * sp
    a = jnp.exp(la)
    a2 = a * a
    x = 2.0 * la
    series = -x * (1.0 + x * (1.0 / 2) * (1.0 + x * (1.0 / 3) * (1.0 + x * (1.0 / 4) * (1.0 + x * (1.0 / 5)))))
    mult = jnp.sqrt(jnp.where(x > -0.1, series, 1.0 - a2))
    dmult = jnp.where(first_row, 0.0, -a2 / mult)
    mult = jnp.where(first_row, 1.0, mult)
    return r, i, a, mult, dmult


def _group_scan(a, b, reverse):
    n = a.shape[0]
    row = lax.broadcasted_iota(jnp.int32, a.shape, 0) & 7
    for d in (1, 2, 4):
        sh = n - d if reverse else d
        a_s, b_s = pltpu.roll(a, sh, 0), pltpu.roll(b, sh, 0)
        m = (row < 8 - d) if reverse else (row >= d)
        b = jnp.where(m, a * b_s + b, b)
        a = jnp.where(m, a * a_s, a)
    return a, b


def _group_carry(a_s, b_s, o_s, n_groups, reverse):
    cols = a_s.shape[1]

    def step(g, carry):
        g = n_groups - 1 - g if reverse else g
        rows = pl.ds(pl.multiple_of(g * 8, 8), 8)
        o = a_s[rows, :] * carry + b_s[rows, :]
        o_s[rows, :] = o
        return o[0:1, :] if reverse else o[7:8, :]

    lax.fori_loop(0, n_groups, step, jnp.zeros((1, cols), F32))


def _rglru_fwd(v0, wa, wx, ba, bx, lam, seq):
    T, C = v0.shape
    ng, G = wa.shape[0], wa.shape[1]
    nb = T // seq

    def body(v_ref, wa_ref, wx_ref, ba_ref, bx_ref, lam_ref, h_ref, a_s, b_s, h_s):
        sp = _softplus_neg(lam_ref[...])

        def chunk(c, _):
            rows = pl.ds(pl.multiple_of(c * RG_ROWS, RG_ROWS), RG_ROWS)
            t = lax.broadcasted_iota(jnp.int32, (RG_ROWS, G), 0) + c * RG_ROWS
            v0c = v_ref[rows, :]
            _, i, a, mult, _ = _gates(v0c, wa_ref, wx_ref, ba_ref[...], bx_ref[...], sp, t == 0)
            a_g, b_g = _group_scan(a, mult * i * v0c, False)
            a_s[rows, :] = a_g
            b_s[rows, :] = b_g
            return 0

        lax.fori_loop(0, seq // RG_ROWS, chunk, 0)
        _group_carry(a_s, b_s, h_s, seq // 8, False)
        h_ref[...] = h_s[...].astype(BF16)

    t2 = _bs((seq, G), lambda b, g: (b, g))
    wsp = _bs((None, G, G), lambda b, g: (g, 0, 0))
    row = _bs((1, G), lambda b, g: (0, g))
    return pl.pallas_call(
        body, name="rglru_fwd", grid=(nb, ng), in_specs=[t2, wsp, wsp, row, row, row], out_specs=t2,
        out_shape=_sds((T, C), BF16), scratch_shapes=[pltpu.VMEM((seq, G), F32)] * 3,
        compiler_params=_cp(dimension_semantics=("arbitrary", "arbitrary")),
    )(v0, wa, wx, ba, bx, lam)


def _rglru_bwd(v0, h, dh, wa, wx, ba, bx, lam, seq):
    T, C = v0.shape
    ng, G = wa.shape[0], wa.shape[1]
    nb = T // seq
    R = RG_ROWS

    def body(v_ref, h_ref, dh_ref, wa_ref, wx_ref, ba_ref, bx_ref, lam_ref,
             dv_ref, dwa_ref, dwx_ref, dba_ref, dbx_ref, dlam_ref, a_s, b_s, q_s, hp_s):
        b = pl.program_id(1)
        lam_v = lam_ref[...]
        sp = _softplus_neg(lam_v)
        dsp_dlam = -_sig(-lam_v)

        @pl.when(b == 0)
        def _():
            dwa_ref[...] = jnp.zeros((G, G), F32)
            dwx_ref[...] = jnp.zeros((G, G), F32)
            dba_ref[...] = jnp.zeros((1, G), F32)
            dbx_ref[...] = jnp.zeros((1, G), F32)
            dlam_ref[...] = jnp.zeros((1, G), F32)

        hp_s[pl.ds(0, 8), :] = jnp.zeros((8, G), F32)
        hp_s[pl.ds(8, seq), :] = h_ref[...].astype(F32)
        q_s[pl.ds(seq, 8), :] = jnp.zeros((8, G), F32)

        def chunk1(c, _):
            rows = pl.ds(pl.multiple_of(c * R, R), R)
            t = lax.broadcasted_iota(jnp.int32, (R, G), 0) + c * R
            _, _, a, _, _ = _gates(v_ref[rows, :], wa_ref, wx_ref, ba_ref[...], bx_ref[...], sp, t == 0)
            a_g, b_g = _group_scan(a, a * dh_ref[rows, :].astype(F32), True)
            a_s[rows, :] = a_g
            b_s[rows, :] = b_g
            return 0

        lax.fori_loop(0, seq // R, chunk1, 0)
        _group_carry(a_s, b_s, q_s, seq // 8, True)

        def chunk3(c, _):
            base = pl.multiple_of(c * R, R)
            rows = pl.ds(base, R)
            t = lax.broadcasted_iota(jnp.int32, (R, G), 0) + c * R
            v0c = v_ref[rows, :]
            r, i, a, mult, dmult_dla = _gates(v0c, wa_ref, wx_ref, ba_ref[...], bx_ref[...], sp, t == 0)
            q_next = pltpu.roll(q_s[pl.ds(base, R + 8), :], R + 7, 0)[0:R]
            h_prev = pltpu.roll(hp_s[pl.ds(base, R + 8), :], R + 1, 0)[0:R]
            gt = dh_ref[rows, :].astype(F32) + q_next
            dla = gt * h_prev * a + gt * i * v0c * dmult_dla
            dpa = dla * (-LRU_C * sp) * r * (1.0 - r)
            dpx = gt * mult * v0c * i * (1.0 - i)
            dpa_b, dpx_b, v_b = dpa.astype(BF16), dpx.astype(BF16), v0c.astype(BF16)
            dv_ref[rows, :] = (gt * mult * i
                               + lax.dot_general(dpa_b, wa_ref[...], _DIMS["nt"], preferred_element_type=F32)
                               + lax.dot_general(dpx_b, wx_ref[...], _DIMS["nt"], preferred_element_type=F32))
            dwa_ref[...] += lax.dot_general(v_b, dpa_b, _DIMS["tn"], preferred_element_type=F32)
            dwx_ref[...] += lax.dot_general(v_b, dpx_b, _DIMS["tn"], preferred_element_type=F32)
            dba_ref[...] += jnp.sum(dpa, axis=0, keepdims=True)
            dbx_ref[...] += jnp.sum(dpx, axis=0, keepdims=True)
            dlam_ref[...] += jnp.sum(dla * (-LRU_C * r), axis=0, keepdims=True) * dsp_dlam
            return 0

        lax.fori_loop(0, seq // R, chunk3, 0)

    t2 = _bs((seq, G), lambda g, b: (b, g))
    wsp = _bs((None, G, G), lambda g, b: (g, 0, 0))
    row = _bs((1, G), lambda g, b: (0, g))
    return pl.pallas_call(
        body, name="rglru_bwd", grid=(ng, nb), in_specs=[t2, t2, t2, wsp, wsp, row, row, row],
        out_specs=[t2, wsp, wsp, row, row, row],
        out_shape=[_sds((T, C), F32), _sds((ng, G, G), F32), _sds((ng, G, G), F32),
                   _sds((1, C), F32), _sds((1, C), F32), _sds((1, C), F32)],
        scratch_shapes=[pltpu.VMEM((seq, G), F32), pltpu.VMEM((seq, G), F32),
                        pltpu.VMEM((seq + 8, G), F32), pltpu.VMEM((seq + 8, G), F32)],
        compiler_params=_cp(dimension_semantics=("arbitrary", "arbitrary")),
    )(v0, h, dh, wa, wx, ba, bx, lam)


TC_A = 256
TC_B = 512
TAPS_A, TAPS_B = 31, 4


def _tiles(T):
    return min(512, T)


def _layer_fwd(x, p, seq):
    T, D = x.shape
    C, R = p["ln_g"].shape[1], p["lam"].shape[1]
    tm = _tiles(T)
    gb_blk, sa_blk = (2 * C + R) // TC_B, (2 * C + 2 * R) // D
    z, h = _fwd_norm_mm("fwd_z", x, p["g_mix"], p["w_in"], p["b_in"], tm, 1024)
    u1 = _conv_fwd("conv_a_fwd", z, 0, C // TC_A, p["caw"], p["cab"], TAPS_A, seq, TC_A, BF16)
    ya = _fwd_ya(u1, p["ln_g"], p["ln_b"], p["w_a_out"], tm)
    v0 = _conv_fwd("conv_b_fwd", z, 2 * C // TC_B, None, p["cbw"], p["cbb"], TAPS_B, seq, TC_B, F32)
    hr = _rglru_fwd(v0, p["wa"], p["wx"], p["b_rg_a"], p["b_rg_x"], p["lam"], seq)
    yb = _fwd_yb(hr, z, gb_blk, p["w_b_out"], tm, TC_B)
    x1 = _fwd_x1(x, ya, yb, z, sa_blk, p["w_o"], tm)
    fp, h2 = _fwd_norm_mm("fwd_f", x1, p["g_mlp"], p["w_1"], None, tm, 1024)
    x2 = _fwd_x2(x1, fp, p["w_2"], tm, 1024)
    return x2, dict(x=x, z=z, h=h, u1=u1, ya=ya, v0=v0, hr=hr, yb=yb, x1=x1, fp=fp, h2=h2)


def _layer_bwd(dx2, dx2b, p, s, seq):
    T, D = dx2.shape
    C, R = p["ln_g"].shape[1], p["lam"].shape[1]
    tm = _tiles(T)
    gb_blk, sa_blk = (2 * C + R) // TC_B, (2 * C + 2 * R) // D
    z = s["z"]
    g = {}

    def relu2(a_refs, out_refs, i, j, k):
        f = jnp.maximum(a_refs[0][...].astype(F32), 0.0)
        return (f * f).astype(BF16)

    dfp = _bwd_df(dx2b, p["w_2"], s["fp"], tm, 1024)
    tw = min(2048, T)
    g["w_2"] = _bwd_dw("bwd_dw2", s["fp"], dx2b, 1024, D, tw, a_fn=relu2)
    dx1, dx1b, g["g_mlp"] = _bwd_norm("bwd_dh2", dfp, p["w_1"], s["x1"], p["g_mlp"], dx2, tm, 1024)
    g["w_1"] = _bwd_dw("bwd_dw1", s["h2"], dfp, D // 2, 1024, tw, shard_cols=dfp.shape[1] // N_CHIPS)

    dya, dyb, dz = _bwd_dm(dx1b, p["w_o"], s["ya"], s["yb"], z, sa_blk, tm)

    def merged(a_refs, out_refs, i, j, k):
        ya_, yb_, sa_, sb_ = (r[...].astype(F32) for r in a_refs)
        return (_sig(sa_) * ya_ + _sig(sb_) * yb_).astype(BF16)

    tk = _bs((tm, D), lambda i, j, k: (k, 0))
    g["w_o"] = _bwd_dw("bwd_dwo", s["ya"], dx1b, D, D, tm, a_fn=merged,
                       a_extra=[(s["yb"], tk), (z, _bs((tm, D), lambda i, j, k: (k, sa_blk))),
                                (z, _bs((tm, D), lambda i, j, k: (k, sa_blk + 1)))])

    du1, g["ln_g"], g["ln_b"] = _bwd_du3(dya, p["w_a_out"], s["u1"], p["ln_g"], p["ln_b"], tm)

    def act_a(a_refs, out_refs, i, j, k):
        return _ln_silu(a_refs[0][...].astype(F32), a_refs[1][...], a_refs[2][...])[0].astype(BF16)

    rowc = _bs((1, C), lambda i, j, k: (0, 0))
    g["w_a_out"] = _bwd_dw("bwd_dwa", s["u1"], dya, C, D, tm, a_fn=act_a,
                           a_extra=[(p["ln_g"], rowc), (p["ln_b"], rowc)])
    dz, g["caw"], g["cab"] = _conv_bwd("conv_a_bwd", du1, z, dz, 0, C // TC_A, p["caw"], TAPS_A, seq, TC_A)

    dhr, dz = _bwd_dp(dyb, p["w_b_out"], s["hr"], z, dz, gb_blk, tm, TC_B)

    def act_b(a_refs, out_refs, i, j, k):
        ge, _ = _gelu(a_refs[1][...].astype(F32))
        return (a_refs[0][...].astype(F32) * ge).astype(BF16)

    tb = min(1024, T)
    g["w_b_out"] = _bwd_dw("bwd_dwb", s["hr"], dyb, TC_B, D, tb, a_fn=act_b,
                           a_extra=[(z, _bs((tb, TC_B), lambda i, j, k: (k, gb_blk + i)))])
    dv0, g["wa"], g["wx"], g["b_rg_a"], g["b_rg_x"], g["lam"] = _rglru_bwd(
        s["v0"], s["hr"], dhr, p["wa"], p["wx"], p["b_rg_a"], p["b_rg_x"], p["lam"], seq)
    dz, g["cbw"], g["cbb"] = _conv_bwd("conv_b_bwd", dv0, z, dz, 2 * C // TC_B, None, p["cbw"], TAPS_B, seq, TC_B)

    dx, dxb, g["g_mix"], dbin = _bwd_norm("bwd_dh", dz, p["w_in"], s["x"], p["g_mix"], dx1, tm, 1024, colsum=True)
    g["b_in"] = dbin.reshape(1, -1)
    ns = dz.shape[1] // N_CHIPS
    g["w_in"] = _bwd_dw("bwd_dwin", s["h"], dz, D // 2, ns // 2, tw, shard_cols=ns)
    return dx, dxb, g


ANY = pl.BlockSpec(memory_space=pl.ANY)


def _mesh_pos():
    return lax.axis_index("x"), lax.axis_index("y"), lax.axis_index("c")


def _other_chips(x, y):
    return [(1 - x, y), (x, 1 - y), (1 - x, 1 - y)]


def _remote(src, dst, ssem, rsem, dev):
    return pltpu.make_async_remote_copy(src_ref=src, dst_ref=dst, send_sem=ssem, recv_sem=rsem,
                                        device_id=dev, device_id_type=MESH)


def _gather_layer(shards, kinds):
    n = len(shards)

    def whole(s, by_cols):
        return (s.shape[0], N_CHIPS * s.shape[1]) if by_cols else (N_CHIPS * s.shape[0], s.shape[1])

    def body(*refs):
        src, dst = refs[:n], refs[n:2 * n]
        send, recv, fsend, frecv, lsem = refs[2 * n:]
        x, y, c = _mesh_pos()
        chips = _other_chips(x, y)
        sib = (x, y, 1 - c)

        def region(t, k, half):
            rows, cols = src[t].shape
            nr = rows if half is None else rows // 2
            r0 = 0 if half is None else half * nr
            if kinds[t][0]:
                return dst[t].at[pl.ds(r0, nr), pl.ds(pl.multiple_of(k * cols, 128), cols)]
            return dst[t].at[pl.ds(pl.multiple_of(k * rows + r0, 8), nr), :]

        me_k = 2 * x + y
        local = [pltpu.make_async_copy(src[t], region(t, me_k, None), lsem.at[t]) for t in range(n)]
        for cp in local:
            cp.start()
        started = []
        for t in range(n):
            half = c if kinds[t][1] else None
            hr = src[t].shape[0] // 2
            s_ref = src[t].at[pl.ds(c * hr, hr), :] if kinds[t][1] else src[t]
            for j, chip in enumerate(chips):
                cp = _remote(s_ref, region(t, me_k, half), send.at[t, j], recv.at[t, j], (*chip, c))
                cp.start()
                started.append(cp)
        for t in range(n):
            half = c if kinds[t][1] else None
            for j, chip in enumerate(chips):
                got = region(t, 2 * chip[0] + chip[1], half)
                _remote(got, got, send.at[t, j], recv.at[t, j], (*chip, c)).wait_recv()
                if kinds[t][1]:
                    cp = _remote(got, got, fsend.at[t, j], frecv.at[t, j], sib)
                    cp.start()
                    started.append(cp)
        for t in range(n):
            if kinds[t][1]:
                for j, chip in enumerate(chips):
                    got = region(t, 2 * chip[0] + chip[1], 1 - c)
                    _remote(got, got, fsend.at[t, j], frecv.at[t, j], sib).wait_recv()
        for cp in started:
            cp.wait_send()
        for cp in local:
            cp.wait()

    sem = pltpu.SemaphoreType.DMA
    return pl.pallas_call(
        body, name="gather_layer", in_specs=[ANY] * n, out_specs=[ANY] * n,
        out_shape=[_sds(whole(s, k[0]), s.dtype) for s, k in zip(shards, kinds)],
        scratch_shapes=[sem((n, 3)), sem((n, 3)), sem((n, 3)), sem((n, 3)), sem((n,))],
        compiler_params=_cp(has_side_effects=True),
    )(*shards)


def _swap_halves(pgs):
    n = len(pgs)

    def body(*refs):
        src, dst = refs[:n], refs[n:2 * n]
        send, recv = refs[2 * n:]
        x, y, c = _mesh_pos()
        cps = [_remote(src[t].at[:, 1 - c], dst[t], send.at[t], recv.at[t], (x, y, 1 - c)) for t in range(n)]
        for cp in cps:
            cp.start()
        for cp in cps:
            cp.wait_recv()
        for cp in cps:
            cp.wait_send()

    sem = pltpu.SemaphoreType.DMA
    return pl.pallas_call(
        body, name="swap_halves", in_specs=[ANY] * n, out_specs=[ANY] * n,
        out_shape=[_sds((a.shape[0],) + a.shape[2:], a.dtype) for a in pgs],
        scratch_shapes=[sem((n,)), sem((n,))], compiler_params=_cp(has_side_effects=True),
    )(*pgs)


def _scatter_chips(ps):
    n = len(ps)

    def body(*refs):
        src, dst = refs[:n], refs[n:2 * n]
        send, recv = refs[2 * n:]
        x, y, c = _mesh_pos()
        cps = []
        for t in range(n):
            for j, chip in enumerate(_other_chips(x, y)):
                cps.append(_remote(src[t].at[2 * chip[0] + chip[1]], dst[t].at[j], send.at[t, j], recv.at[t, j], (*chip, c)))
        for cp in cps:
            cp.start()
        for cp in cps:
            cp.wait_recv()
        for cp in cps:
            cp.wait_send()

    sem = pltpu.SemaphoreType.DMA
    return pl.pallas_call(
        body, name="scatter_chips", in_specs=[ANY] * n, out_specs=[ANY] * n,
        out_shape=[_sds((3,) + a.shape[1:], a.dtype) for a in ps],
        scratch_shapes=[sem((n, 3)), sem((n, 3))], compiler_params=_cp(has_side_effects=True),
    )(*ps)


def _join_halves(fins, n_layers):
    n = len(fins)
    nt = n // n_layers

    def body(*refs):
        src, dst = refs[:n], refs[n:n + nt]
        send, recv, lsem = refs[n + nt:]
        x, y, c = _mesh_pos()
        local, cps = [], []
        for t in range(nt):
            for l in range(n_layers):
                q = t * n_layers + l
                local.append(pltpu.make_async_copy(src[q], dst[t].at[l, c], lsem.at[q]))
                cps.append(_remote(src[q], dst[t].at[l, c], send.at[q], recv.at[q], (x, y, 1 - c)))
        for cp in local + cps:
            cp.start()
        for t in range(nt):
            for l in range(n_layers):
                q = t * n_layers + l
                _remote(src[q], dst[t].at[l, 1 - c], send.at[q], recv.at[q], (x, y, 1 - c)).wait_recv()
        for cp in cps:
            cp.wait_send()
        for cp in local:
            cp.wait()

    sem = pltpu.SemaphoreType.DMA
    return pl.pallas_call(
        body, name="join_halves", in_specs=[ANY] * n, out_specs=[ANY] * nt,
        out_shape=[_sds((n_layers, 2) + fins[t * n_layers].shape, F32) for t in range(nt)],
        scratch_shapes=[sem((n,)), sem((n,)), sem((n,))], compiler_params=_cp(has_side_effects=True),
    )(*fins)


def _sum_siblings(pg, rb, c_arr):
    nk, _, hr, cols = pg.shape

    def body(c_ref, a_ref, b_ref, o_ref):
        o_ref[...] = (a_ref[...].astype(F32) + b_ref[...].astype(F32)).astype(BF16)

    return pl.pallas_call(
        body, name="sum_siblings",
        grid_spec=pltpu.PrefetchScalarGridSpec(
            num_scalar_prefetch=1, grid=(nk,),
            in_specs=[pl.BlockSpec((None, None, hr, cols), lambda k, c_ref: (k, c_ref[0], 0, 0)),
                      pl.BlockSpec((None, hr, cols), lambda k, c_ref: (k, 0, 0))],
            out_specs=pl.BlockSpec((None, hr, cols), lambda k, c_ref: (k, 0, 0))),
        out_shape=_sds((nk, hr, cols), BF16), compiler_params=_cp(dimension_semantics=("arbitrary",)),
    )(c_arr, pg, rb)


def _sum_chips(p, rb, k_arr):
    _, hr, cols = p.shape

    def body(k_ref, a_ref, b_ref, o_ref):
        o_ref[...] = a_ref[...].astype(F32) + b_ref[0].astype(F32) + b_ref[1].astype(F32) + b_ref[2].astype(F32)

    return pl.pallas_call(
        body, name="sum_chips",
        grid_spec=pltpu.PrefetchScalarGridSpec(
            num_scalar_prefetch=1, grid=(1,),
            in_specs=[pl.BlockSpec((None, hr, cols), lambda i, k_ref: (k_ref[0], 0, 0)),
                      pl.BlockSpec((3, hr, cols), lambda i, k_ref: (0, 0, 0))],
            out_specs=pl.BlockSpec((hr, cols), lambda i, k_ref: (0, 0))),
        out_shape=_sds((hr, cols), F32), compiler_params=_cp(dimension_semantics=("arbitrary",)),
    )(k_arr, p, rb)


N_DEV = 8


def _allreduce_small(part):
    _, r, lanes = part.shape

    def body(p_ref, o_ref, rbuf, s1, r1, s2, r2):
        x, y, c = _mesh_pos()
        me = 4 * x + 2 * y + c
        devs = [(d // 4, (d // 2) % 2, d % 2) for d in range(N_DEV)]
        rbuf[me] = p_ref[me]

        def each_peer(fn):
            for d in range(N_DEV):
                @pl.when(d != me)
                def _():
                    fn(d)

        each_peer(lambda d: _remote(p_ref.at[d], rbuf.at[me], s1.at[d], r1.at[me], devs[d]).start())
        each_peer(lambda d: _remote(p_ref.at[d], rbuf.at[d], s1.at[d], r1.at[d], devs[d]).wait_recv())
        total = rbuf[0]
        for d in range(1, N_DEV):
            total = total + rbuf[d]
        o_ref[me] = total
        each_peer(lambda d: _remote(o_ref.at[me], o_ref.at[me], s2.at[d], r2.at[me], devs[d]).start())
        each_peer(lambda d: _remote(o_ref.at[d], o_ref.at[d], s2.at[d], r2.at[d], devs[d]).wait_recv())
        each_peer(lambda d: _remote(p_ref.at[d], rbuf.at[me], s1.at[d], r1.at[me], devs[d]).wait_send())
        each_peer(lambda d: _remote(o_ref.at[me], o_ref.at[me], s2.at[d], r2.at[me], devs[d]).wait_send())

    sem = pltpu.SemaphoreType.DMA
    vm = pl.BlockSpec(memory_space=pltpu.VMEM)
    return pl.pallas_call(
        body, name="allreduce_small", in_specs=[vm], out_specs=vm, out_shape=_sds(part.shape, F32),
        scratch_shapes=[pltpu.VMEM(part.shape, F32), sem((N_DEV,)), sem((N_DEV,)), sem((N_DEV,)), sem((N_DEV,))],
        compiler_params=_cp(has_side_effects=True),
    )(part)


BIG = ("w_in", "w_1", "w_a_out", "w_b_out", "w_o", "w_2")
BY_COLS = {"w_in": True, "w_1": True, "w_a_out": False, "w_b_out": False, "w_o": False, "w_2": False}
WEIGHTS = ("g_mix", "w_in", "b_in", "conv_a_w", "conv_a_b", "ln_g", "ln_b", "w_a_out", "conv_b_w", "conv_b_b", "w_rg_a",
           "b_rg_a", "w_rg_x", "b_rg_x", "lam", "w_b_out", "w_o", "g_mlp", "w_1", "w_2", "g_final")
SMALL = tuple(n for n in WEIGHTS if n not in BIG)
ADAM_ROWS = 256


def _block_diag(w):
    nh, dh, _ = w.shape
    ng = nh // HEADS_PER_GROUP
    w4 = w.reshape(ng, HEADS_PER_GROUP, dh, dh)
    eye = jnp.eye(HEADS_PER_GROUP, dtype=w.dtype)
    return jnp.einsum("qhij,hk->qhikj", w4, eye).reshape(ng, HEADS_PER_GROUP * dh, HEADS_PER_GROUP * dh)


def _block_diag_part(d, dh):
    ng = d.shape[0]
    eye = jnp.eye(HEADS_PER_GROUP, dtype=d.dtype)
    d5 = d.reshape(ng, HEADS_PER_GROUP, dh, HEADS_PER_GROUP, dh)
    return jnp.einsum("qhikj,hk->qhij", d5, eye).reshape(ng * HEADS_PER_GROUP, dh, dh)


def _pack(arrays, width, row_multiple):
    flat = jnp.concatenate([a.reshape(-1) for a in arrays])
    per = width * row_multiple
    total = -(-flat.shape[0] // per) * per
    return jnp.pad(flat, (0, total - flat.shape[0])).reshape(total // width, width)


def _unpack(buf, like):
    flat = buf.reshape(-1)
    out, off = [], 0
    for a in like:
        n = 1
        for d in a.shape:
            n *= d
        out.append(flat[off:off + n].reshape(a.shape))
        off += n
    return out


def kernel(x, g_mix, w_in, b_in, conv_a_w, conv_a_b, ln_g, ln_b, w_a_out, conv_b_w, conv_b_b, w_rg_a, b_rg_a, w_rg_x, b_rg_x, lam, w_b_out, w_o, g_mlp, w_1, w_2, g_final, loss_target, m_g_mix, m_w_in, m_b_in, m_conv_a_w, m_conv_a_b, m_ln_g, m_ln_b, m_w_a_out, m_conv_b_w, m_conv_b_b, m_w_rg_a, m_b_rg_a, m_w_rg_x, m_b_rg_x, m_lam, m_w_b_out, m_w_o, m_g_mlp, m_w_1, m_w_2, m_g_final, v_g_mix, v_w_in, v_b_in, v_conv_a_w, v_conv_a_b, v_ln_g, v_ln_b, v_w_a_out, v_conv_b_w, v_conv_b_b, v_w_rg_a, v_b_rg_a, v_w_rg_x, v_b_rg_x, v_lam, v_w_b_out, v_w_o, v_g_mlp, v_w_1, v_w_2, v_g_final):
    w = dict(g_mix=g_mix, w_in=w_in, b_in=b_in, conv_a_w=conv_a_w, conv_a_b=conv_a_b, ln_g=ln_g, ln_b=ln_b, w_a_out=w_a_out,
             conv_b_w=conv_b_w, conv_b_b=conv_b_b, w_rg_a=w_rg_a, b_rg_a=b_rg_a, w_rg_x=w_rg_x, b_rg_x=b_rg_x, lam=lam,
             w_b_out=w_b_out, w_o=w_o, g_mlp=g_mlp, w_1=w_1, w_2=w_2, g_final=g_final)
    m = dict(g_mix=m_g_mix, w_in=m_w_in, b_in=m_b_in, conv_a_w=m_conv_a_w, conv_a_b=m_conv_a_b, ln_g=m_ln_g, ln_b=m_ln_b,
             w_a_out=m_w_a_out, conv_b_w=m_conv_b_w, conv_b_b=m_conv_b_b, w_rg_a=m_w_rg_a, b_rg_a=m_b_rg_a, w_rg_x=m_w_rg_x,
             b_rg_x=m_b_rg_x, lam=m_lam, w_b_out=m_w_b_out, w_o=m_w_o, g_mlp=m_g_mlp, w_1=m_w_1, w_2=m_w_2, g_final=m_g_final)
    v = dict(g_mix=v_g_mix, w_in=v_w_in, b_in=v_b_in, conv_a_w=v_conv_a_w, conv_a_b=v_conv_a_b, ln_g=v_ln_g, ln_b=v_ln_b,
             w_a_out=v_w_a_out, conv_b_w=v_conv_b_w, conv_b_b=v_conv_b_b, w_rg_a=v_w_rg_a, b_rg_a=v_b_rg_a, w_rg_x=v_w_rg_x,
             b_rg_x=v_b_rg_x, lam=v_lam, w_b_out=v_w_b_out, w_o=v_w_o, g_mlp=v_g_mlp, w_1=v_w_1, w_2=v_w_2, g_final=v_g_final)
    B, S, D = x.shape
    T = B * S
    L = w_in.shape[0]
    dh = w_rg_a.shape[-1]
    taps_a, taps_b = conv_a_w.shape[1], conv_b_w.shape[1]
    assert (taps_a, taps_b) == (TAPS_A, TAPS_B)
    xi, yi, ci = _mesh_pos()
    c_arr = jnp.reshape(ci, (1,)).astype(jnp.int32)
    k_me = 2 * xi + yi
    k_arr = jnp.reshape(k_me, (1,)).astype(jnp.int32)

    caw_p = jnp.pad(conv_a_w, ((0, 0), (0, 32 - taps_a), (0, 0)))
    cbw_p = jnp.pad(conv_b_w, ((0, 0), (0, 8 - taps_b), (0, 0)))
    kinds = [(BY_COLS[n], True) for n in BIG] + [(True, False), (True, False)]
    row = lambda a: a.reshape(1, -1)
    params = []
    for l in range(L):
        whole = _gather_layer([w[n][l].astype(BF16) for n in BIG] + [caw_p[l], cbw_p[l]], kinds)
        p = dict(zip(BIG, whole[:len(BIG)]))
        p.update(caw=whole[-2], cbw=whole[-1], cab=row(conv_a_b[l]), cbb=row(conv_b_b[l]),
                 wa=_block_diag(w_rg_a[l]).astype(BF16), wx=_block_diag(w_rg_x[l]).astype(BF16))
        for n in ("g_mix", "b_in", "ln_g", "ln_b", "b_rg_a", "b_rg_x", "lam", "g_mlp"):
            p[n] = row(w[n][l])
        params.append(p)

    xf = x.reshape(T, D)
    saved = []
    for l in range(L):
        xf, s = _layer_fwd(xf, params[l], S)
        saved.append(s)
    loss_part, dx, dxb, dg_final = _loss_head(xf, row(g_final), loss_target.reshape(T, D), _tiles(T))
    loss = lax.psum(loss_part[0, 0], ("x", "y", "c"))

    fins = {n: [None] * L for n in BIG}
    small = {n: [None] * L for n in SMALL if n != "g_final"}
    for l in reversed(range(L)):
        dx, dxb, g = _layer_bwd(dx, dxb, params[l], saved[l], S)
        pgs = []
        for n in BIG:
            a = g[n]
            if not BY_COLS[n]:
                a = a.reshape(N_CHIPS, 2, a.shape[0] // (2 * N_CHIPS), a.shape[1])
            pgs.append(a)
        got = _swap_halves(pgs)
        sums = [_sum_siblings(a, b, c_arr) for a, b in zip(pgs, got)]
        got = _scatter_chips(sums)
        for n, a, b in zip(BIG, sums, got):
            fins[n][l] = _sum_chips(a, b, k_arr)
        small["g_mix"][l], small["b_in"][l], small["g_mlp"][l] = g["g_mix"], g["b_in"], g["g_mlp"]
        small["conv_a_w"][l], small["conv_a_b"][l] = g["caw"], g["cab"]
        small["conv_b_w"][l], small["conv_b_b"][l] = g["cbw"], g["cbb"]
        small["ln_g"][l], small["ln_b"][l], small["lam"][l] = g["ln_g"], g["ln_b"], g["lam"]
        small["w_rg_a"][l], small["w_rg_x"][l] = _block_diag_part(g["wa"], dh), _block_diag_part(g["wx"], dh)
        small["b_rg_a"][l], small["b_rg_x"][l] = g["b_rg_a"], g["b_rg_x"]
    grad_x = dx.reshape(B, S, D)

    joined = _join_halves([fins[n][l] for n in BIG for l in range(L)], L)
    grads = {n: a.reshape(w[n].shape) for n, a in zip(BIG, joined)}

    names = [n for n in SMALL if n != "g_final"]
    parts = [jnp.stack(small[n]) for n in names] + [dg_final]
    packed = _pack(parts, 128, 8 * N_DEV)
    total = _allreduce_small(packed.reshape(N_DEV, packed.shape[0] // N_DEV, 128))
    for n, a in zip(names + ["g_final"], _unpack(total, parts)):
        if n == "conv_a_w":
            a = lax.dynamic_slice_in_dim(a[:, :taps_a], k_me * conv_a_w.shape[2], conv_a_w.shape[2], axis=2)
        elif n == "conv_b_w":
            a = lax.dynamic_slice_in_dim(a[:, :taps_b], k_me * conv_b_w.shape[2], conv_b_w.shape[2], axis=2)
        grads[n] = a.reshape(w[n].shape)

    delta, new_m, new_v = {}, {}, {}
    for n in BIG:
        cols = w[n].shape[-1]
        d_, m_, v_ = _adamw("adamw_" + n, w[n].reshape(-1, cols), grads[n].reshape(-1, cols), m[n].reshape(-1, cols),
                            v[n].reshape(-1, cols), ADAM_ROWS)
        delta[n], new_m[n], new_v[n] = (a.reshape(w[n].shape) for a in (d_, m_, v_))
    like = [w[n] for n in SMALL]
    pk = lambda d: _pack([d[n] for n in SMALL], 1024, ADAM_ROWS)
    d_, m_, v_ = _adamw("adamw_small", pk(w), pk(grads), pk(m), pk(v), ADAM_ROWS)
    for n, a, b_, c_ in zip(SMALL, _unpack(d_, like), _unpack(m_, like), _unpack(v_, like)):
        delta[n], new_m[n], new_v[n] = a, b_, c_

    return (loss, grad_x, *[grads[n] for n in WEIGHTS], *[delta[n] for n in WEIGHTS],
            *[new_m[n] for n in WEIGHTS], *[new_v[n] for n in WEIGHTS])
```

```python
import jax
import jax.numpy as jnp
from jax import lax
from jax.experimental import pallas as pl
from jax.experimental.pallas import tpu as pltpu

F32 = jnp.float32
BF16 = jnp.bfloat16
MESH = pl.DeviceIdType.MESH

EPS = 1e-6
LRU_C = 8.0
ADAM_LR, ADAM_B1, ADAM_B2, ADAM_EPS, ADAM_WD, ADAM_STEP = 0.001, 0.9, 0.999, 1e-08, 0.01, 10

N_CHIPS = 4
HEADS_PER_GROUP = 4
VMEM_LIMIT = 56 * 1024 * 1024


def _cp(**kw):
    return pltpu.CompilerParams(vmem_limit_bytes=VMEM_LIMIT, **kw)


def _sig(x):
    return 1.0 / (1.0 + jnp.exp(-x))


def _gelu(x):
    t = jnp.tanh(0.7978845608028654 * (x + 0.044715 * x * x * x))
    return 0.5 * x * (1.0 + t), t


def _gelu_grad(x, t):
    dt = (1.0 - t * t) * 0.7978845608028654 * (1.0 + 3.0 * 0.044715 * x * x)
    return 0.5 * (1.0 + t) + 0.5 * x * dt


def _rms(xf, g):
    r = lax.rsqrt(jnp.mean(xf * xf, axis=-1, keepdims=True) + EPS)
    return xf * r * g, r


def _rms_bwd(xf, g, r, dh):
    dyg = dh * g
    dx = r * (dyg - xf * (r * r) * jnp.mean(dyg * xf, axis=-1, keepdims=True))
    return dx, dh * xf * r


def _ln_silu(u, g, b):
    mu = jnp.mean(u, axis=-1, keepdims=True)
    uc = u - mu
    rstd = lax.rsqrt(jnp.mean(uc * uc, axis=-1, keepdims=True) + EPS)
    uh = uc * rstd
    u2 = uh * g + b
    s = _sig(u2)
    return u2 * s, uh, rstd, u2, s


_DIMS = {"nn": (((1,), (0,)), ((), ())), "nt": (((1,), (1,)), ((), ())), "tn": (((0,), (0,)), ((), ()))}


def _mm(name, mode, grid, a_ins, a_fn, b_in, e_ins, epi, outs, acc_shape, cache_a=None, alias=(), extra_scratch=()):
    ni, nj, nk = grid
    na, ne, no = len(a_ins), len(e_ins), len(outs)
    assert cache_a is None or nk == 1
    n_fixed = (nk > 1) + (cache_a is not None)

    def body(*refs):
        a_refs = refs[:na]
        b_ref = refs[na]
        e_refs = refs[na + 1:na + 1 + ne]
        out_refs = refs[na + 1 + ne + len(alias):na + 1 + ne + len(alias) + no]
        scratch = refs[na + 1 + ne + len(alias) + no:]
        extra = scratch[n_fixed:]
        i, j, k = pl.program_id(0), pl.program_id(1), pl.program_id(2)
        if cache_a is not None:
            cache_ref = scratch[n_fixed - 1]

            @pl.when(j == 0)
            def _():
                cache_ref[...] = a_fn(a_refs, out_refs, i, j, k)

            a = cache_ref[...]
        else:
            a = a_fn(a_refs, out_refs, i, j, k)
        prod = lax.dot_general(a, b_ref[...], _DIMS[mode], preferred_element_type=F32)
        if nk == 1:
            epi(prod, e_refs, out_refs, i, j, extra)
        else:
            acc_ref = scratch[0]

            @pl.when(k == 0)
            def _():
                acc_ref[...] = prod

            @pl.when(k > 0)
            def _():
                acc_ref[...] += prod

            @pl.when(k == nk - 1)
            def _():
                epi(acc_ref[...], e_refs, out_refs, i, j, extra)

    scratch_shapes = []
    if nk > 1:
        scratch_shapes.append(pltpu.VMEM(acc_shape, F32))
    if cache_a is not None:
        scratch_shapes.append(pltpu.VMEM(cache_a, BF16))
    ins = list(a_ins) + [b_in] + list(e_ins) + [(arr, pl.BlockSpec(memory_space=pl.ANY)) for arr, _ in alias]
    first_alias = na + 1 + ne
    res = pl.pallas_call(
        body, name=name, grid=grid,
        in_specs=[s for _, s in ins], out_specs=[s for _, s in outs],
        out_shape=[o for o, _ in outs], scratch_shapes=scratch_shapes + list(extra_scratch),
        input_output_aliases={first_alias + n: o for n, (_, o) in enumerate(alias)},
        compiler_params=_cp(dimension_semantics=("arbitrary", "arbitrary", "arbitrary")),
    )(*[a for a, _ in ins])
    return res


def _bs(shape, fn):
    return pl.BlockSpec(shape, fn)


def _sds(shape, dt):
    return jax.ShapeDtypeStruct(shape, dt)


def _acc_rows(ref, val, first):
    @pl.when(first)
    def _():
        ref[...] = val

    @pl.when(jnp.logical_not(first))
    def _():
        ref[...] += val


def _fwd_norm_mm(name, x, g, w, bias, tm, tn):
    T, D = x.shape
    N = w.shape[1]

    def a_fn(a_refs, out_refs, i, j, k):
        h, _ = _rms(a_refs[0][...], a_refs[1][...])
        hb = h.astype(BF16)
        out_refs[1][...] = hb
        return hb

    def epi(acc, e_refs, out_refs, i, j, extra):
        if bias is not None:
            acc = acc + e_refs[0][...]
        out_refs[0][...] = acc.astype(BF16)

    e_ins = [] if bias is None else [(bias, _bs((1, tn), lambda i, j, k: (0, j)))]
    return _mm(name, "nn", (T // tm, N // tn, 1),
               [(x, _bs((tm, D), lambda i, j, k: (i, 0))), (g, _bs((1, D), lambda i, j, k: (0, 0)))], a_fn,
               (w, _bs((D, tn), lambda i, j, k: (0, j))), e_ins, epi,
               [(_sds((T, N), BF16), _bs((tm, tn), lambda i, j, k: (i, j))),
                (_sds((T, D), BF16), _bs((tm, D), lambda i, j, k: (i, 0)))],
               None, cache_a=(tm, D))


def _fwd_ya(u1, ln_g, ln_b, w, tm):
    T, C = u1.shape
    N = w.shape[1]

    def a_fn(a_refs, out_refs, i, j, k):
        u3 = _ln_silu(a_refs[0][...].astype(F32), a_refs[1][...], a_refs[2][...])[0]
        return u3.astype(BF16)

    def epi(acc, e_refs, out_refs, i, j, extra):
        out_refs[0][...] = acc.astype(BF16)

    row = _bs((1, C), lambda i, j, k: (0, 0))
    return _mm("fwd_ya", "nn", (T // tm, 1, 1),
               [(u1, _bs((tm, C), lambda i, j, k: (i, 0))), (ln_g, row), (ln_b, row)], a_fn,
               (w, _bs((C, N), lambda i, j, k: (0, 0))), [], epi,
               [(_sds((T, N), BF16), _bs((tm, N), lambda i, j, k: (i, 0)))], None)[0]


def _fwd_yb(h, z, gb_blk, w, tm, tk):
    T, C = h.shape
    N = w.shape[1]

    def a_fn(a_refs, out_refs, i, j, k):
        ge, _ = _gelu(a_refs[1][...].astype(F32))
        return (a_refs[0][...].astype(F32) * ge).astype(BF16)

    def epi(acc, e_refs, out_refs, i, j, extra):
        out_refs[0][...] = acc.astype(BF16)

    return _mm("fwd_yb", "nn", (T // tm, 1, C // tk),
               [(h, _bs((tm, tk), lambda i, j, k: (i, k))), (z, _bs((tm, tk), lambda i, j, k: (i, gb_blk + k)))], a_fn,
               (w, _bs((tk, N), lambda i, j, k: (k, 0))), [], epi,
               [(_sds((T, N), BF16), _bs((tm, N), lambda i, j, k: (i, 0)))], (tm, N))[0]


def _fwd_x1(x, ya, yb, z, sa_blk, w, tm):
    T, D = x.shape

    def a_fn(a_refs, out_refs, i, j, k):
        ya_, yb_, sa_, sb_ = (r[...].astype(F32) for r in a_refs)
        return (_sig(sa_) * ya_ + _sig(sb_) * yb_).astype(BF16)

    def epi(acc, e_refs, out_refs, i, j, extra):
        out_refs[0][...] = e_refs[0][...] + acc

    t = _bs((tm, D), lambda i, j, k: (i, 0))
    return _mm("fwd_x1", "nn", (T // tm, 1, 1),
               [(ya, t), (yb, t), (z, _bs((tm, D), lambda i, j, k: (i, sa_blk))),
                (z, _bs((tm, D), lambda i, j, k: (i, sa_blk + 1)))], a_fn,
               (w, _bs((D, D), lambda i, j, k: (0, 0))), [(x, t)], epi,
               [(_sds((T, D), F32), t)], None)[0]


def _fwd_x2(x1, fp, w, tm, tk):
    T, D = x1.shape
    Fd = fp.shape[1]

    def a_fn(a_refs, out_refs, i, j, k):
        f = jnp.maximum(a_refs[0][...].astype(F32), 0.0)
        return (f * f).astype(BF16)

    def epi(acc, e_refs, out_refs, i, j, extra):
        out_refs[0][...] = e_refs[0][...] + acc

    t = _bs((tm, D), lambda i, j, k: (i, 0))
    return _mm("fwd_x2", "nn", (T // tm, 1, Fd // tk),
               [(fp, _bs((tm, tk), lambda i, j, k: (i, k)))], a_fn,
               (w, _bs((tk, D), lambda i, j, k: (k, 0))), [(x1, t)], epi,
               [(_sds((T, D), F32), t)], (tm, D))[0]


def _loss_head(x, g, target, tm):
    T, D = x.shape

    def body(x_ref, g_ref, t_ref, loss_ref, dx_ref, dxb_ref, dg_ref):
        i = pl.program_id(0)
        xf, gv = x_ref[...], g_ref[...]
        y, r = _rms(xf, gv)
        err = y - t_ref[...]
        part = 0.5 * jnp.sum(jnp.mean(err * err, axis=-1, keepdims=True), axis=0, keepdims=True)
        dx, dg_rows = _rms_bwd(xf, gv, r, err * (1.0 / D))
        dx_ref[...] = dx
        dxb_ref[...] = dx.astype(BF16)
        _acc_rows(loss_ref, jnp.broadcast_to(part, (1, 128)), i == 0)
        _acc_rows(dg_ref, jnp.sum(dg_rows, axis=0, keepdims=True), i == 0)

    t = _bs((tm, D), lambda i: (i, 0))
    row = _bs((1, D), lambda i: (0, 0))
    return pl.pallas_call(
        body, name="loss_head", grid=(T // tm,), in_specs=[t, row, t],
        out_specs=[_bs((1, 128), lambda i: (0, 0)), t, t, row],
        out_shape=[_sds((1, 128), F32), _sds((T, D), F32), _sds((T, D), BF16), _sds((1, D), F32)],
        compiler_params=_cp(dimension_semantics=("arbitrary",)),
    )(x, g, target)


def _adamw(name, w, g, m, v, tr):
    rows, cols = w.shape
    d1 = 1.0 - ADAM_B1 ** ADAM_STEP
    d2 = 1.0 - ADAM_B2 ** ADAM_STEP

    def body(w_ref, g_ref, m_ref, v_ref, d_ref, mo_ref, vo_ref):
        gv = g_ref[...]
        mn = ADAM_B1 * m_ref[...] + (1.0 - ADAM_B1) * gv
        vn = ADAM_B2 * v_ref[...] + (1.0 - ADAM_B2) * (gv * gv)
        d_ref[...] = -ADAM_LR * ((mn / d1) / (jnp.sqrt(vn / d2) + ADAM_EPS) + ADAM_WD * w_ref[...])
        mo_ref[...] = mn
        vo_ref[...] = vn

    t = _bs((tr, cols), lambda i: (i, 0))
    return pl.pallas_call(
        body, name=name, grid=(rows // tr,), in_specs=[t] * 4, out_specs=[t] * 3,
        out_shape=[_sds((rows, cols), F32)] * 3,
        compiler_params=_cp(dimension_semantics=("arbitrary",)),
    )(w, g, m, v)


def _ident(a_refs, out_refs, i, j, k):
    return a_refs[0][...]


def _bwd_dw(name, act, dy, ti, tj, tm, a_fn=None, a_extra=(), shard_cols=None):
    T, J = dy.shape
    I = act.shape[1]

    def epi(acc, e_refs, out_refs, i, j, extra):
        out_refs[0][...] = acc.astype(BF16)

    if shard_cols is None:
        out = (_sds((I, J), BF16), _bs((ti, tj), lambda i, j, k: (i, j)))
    else:
        per = shard_cols // tj
        assert ti == I // 2 and per * tj == shard_cols
        out = (_sds((J // shard_cols, 2, ti, shard_cols), BF16),
               _bs((None, None, ti, tj), lambda i, j, k: (lax.div(j, per), i, 0, lax.rem(j, per))))
    a_ins = [(act, _bs((tm, ti), lambda i, j, k: (k, i)))] + list(a_extra)
    return _mm(name, "tn", (I // ti, J // tj, T // tm), a_ins, a_fn or _ident,
               (dy, _bs((tm, tj), lambda i, j, k: (k, j))), [], epi, [out], (ti, tj))[0]


def _bwd_df(dxb, w2, fp, tm, tn):
    T, D = dxb.shape
    Fd = w2.shape[0]

    def epi(acc, e_refs, out_refs, i, j, extra):
        out_refs[0][...] = (acc * (2.0 * jnp.maximum(e_refs[0][...].astype(F32), 0.0))).astype(BF16)

    t = _bs((tm, tn), lambda i, j, k: (i, j))
    return _mm("bwd_df", "nt", (T // tm, Fd // tn, 1), [(dxb, _bs((tm, D), lambda i, j, k: (i, 0)))], _ident,
               (w2, _bs((tn, D), lambda i, j, k: (j, 0))), [(fp, t)], epi, [(_sds((T, Fd), BF16), t)], None)[0]


def _bwd_norm(name, dy, w, x, g, dres, tm, tk, colsum=False):
    T, K = dy.shape
    D = w.shape[0]
    nk = K // tk

    def a_fn(a_refs, out_refs, i, j, k):
        a = a_refs[0][...]
        if colsum:
            s = jnp.sum(a.astype(F32), axis=0, keepdims=True)

            @pl.when(i == 0)
            def _():
                out_refs[3][k] = s

            @pl.when(i > 0)
            def _():
                out_refs[3][k] += s
        return a

    def epi(acc, e_refs, out_refs, i, j, extra):
        xf, gv = e_refs[0][...], e_refs[1][...]
        r = lax.rsqrt(jnp.mean(xf * xf, axis=-1, keepdims=True) + EPS)
        dx, dg_rows = _rms_bwd(xf, gv, r, acc)
        dx = dx + e_refs[2][...]
        out_refs[0][...] = dx
        out_refs[1][...] = dx.astype(BF16)
        _acc_rows(out_refs[2], jnp.sum(dg_rows, axis=0, keepdims=True), i == 0)

    t = _bs((tm, D), lambda i, j, k: (i, 0))
    row = _bs((1, D), lambda i, j, k: (0, 0))
    outs = [(_sds((T, D), F32), t), (_sds((T, D), BF16), t), (_sds((1, D), F32), row)]
    if colsum:
        outs.append((_sds((nk, 1, tk), F32), _bs((nk, 1, tk), lambda i, j, k: (0, 0, 0))))
    return _mm(name, "nt", (T // tm, 1, nk), [(dy, _bs((tm, tk), lambda i, j, k: (i, k)))], a_fn,
               (w, _bs((D, tk), lambda i, j, k: (0, k))), [(x, t), (g, row), (dres, t)], epi, outs, (tm, D))


def _bwd_dm(dxb, w_o, ya, yb, z, sa_blk, tm):
    T, D = dxb.shape

    def epi(acc, e_refs, out_refs, i, j, extra):
        ya_, yb_, sa_, sb_ = (r[...].astype(F32) for r in e_refs)
        ga, gb = _sig(sa_), _sig(sb_)
        out_refs[0][...] = (acc * ga).astype(BF16)
        out_refs[1][...] = (acc * gb).astype(BF16)
        stage = extra[0]
        stage[:, 0:D] = (acc * ya_ * ga * (1.0 - ga)).astype(BF16)
        stage[:, D:2 * D] = (acc * yb_ * gb * (1.0 - gb)).astype(BF16)
        pltpu.sync_copy(stage, out_refs[2].at[pl.ds(pl.multiple_of(i * tm, tm), tm), pl.ds(sa_blk * D, 2 * D)])

    t = _bs((tm, D), lambda i, j, k: (i, 0))
    return _mm("bwd_dm", "nt", (T // tm, 1, 1), [(dxb, t)], _ident, (w_o, _bs((D, D), lambda i, j, k: (0, 0))),
               [(ya, t), (yb, t), (z, _bs((tm, D), lambda i, j, k: (i, sa_blk))),
                (z, _bs((tm, D), lambda i, j, k: (i, sa_blk + 1)))], epi,
               [(_sds((T, D), BF16), t), (_sds((T, D), BF16), t),
                (_sds(z.shape, BF16), pl.BlockSpec(memory_space=pl.ANY))], None,
               extra_scratch=[pltpu.VMEM((tm, 2 * D), BF16)])


def _bwd_du3(dya, w, u1, ln_g, ln_b, tm):
    T, D = dya.shape
    C = w.shape[0]

    def epi(acc, e_refs, out_refs, i, j, extra):
        gv = e_refs[1][...]
        _, uh, rstd, u2, s = _ln_silu(e_refs[0][...].astype(F32), gv, e_refs[2][...])
        du2 = acc * (s * (1.0 + u2 * (1.0 - s)))
        duh = du2 * gv
        out_refs[0][...] = rstd * (duh - jnp.mean(duh, axis=-1, keepdims=True)
                                   - uh * jnp.mean(duh * uh, axis=-1, keepdims=True))
        _acc_rows(out_refs[1], jnp.sum(du2 * uh, axis=0, keepdims=True), i == 0)
        _acc_rows(out_refs[2], jnp.sum(du2, axis=0, keepdims=True), i == 0)

    t = _bs((tm, C), lambda i, j, k: (i, 0))
    row = _bs((1, C), lambda i, j, k: (0, 0))
    return _mm("bwd_du3", "nt", (T // tm, 1, 1), [(dya, _bs((tm, D), lambda i, j, k: (i, 0)))], _ident,
               (w, _bs((C, D), lambda i, j, k: (0, 0))), [(u1, t), (ln_g, row), (ln_b, row)], epi,
               [(_sds((T, C), F32), t), (_sds((1, C), F32), row), (_sds((1, C), F32), row)], None)


def _bwd_dp(dyb, w, h, z, dz, gb_blk, tm, tn):
    T, D = dyb.shape
    R = w.shape[0]

    def epi(acc, e_refs, out_refs, i, j, extra):
        gbv = e_refs[1][...].astype(F32)
        ge, th = _gelu(gbv)
        out_refs[0][...] = acc * ge
        out_refs[1][...] = (acc * e_refs[0][...].astype(F32) * _gelu_grad(gbv, th)).astype(BF16)

    t = _bs((tm, tn), lambda i, j, k: (i, j))
    tz = _bs((tm, tn), lambda i, j, k: (i, gb_blk + j))
    return _mm("bwd_dp", "nt", (T // tm, R // tn, 1), [(dyb, _bs((tm, D), lambda i, j, k: (i, 0)))], _ident,
               (w, _bs((tn, D), lambda i, j, k: (j, 0))), [(h, t), (z, tz)], epi,
               [(_sds((T, R), F32), t), (_sds(dz.shape, BF16), tz)], None, cache_a=None, alias=[(dz, 1)])


CONV_ROWS = 32


def _shifted_taps(x, halo, shifts, fn):
    n = CONV_ROWS + halo
    by_r = {}
    for k, s in shifts:
        by_r.setdefault(s % 8, []).append((k, s))
    for r in sorted(by_r):
        xr = x if r == 0 else pltpu.roll(x, n - r, 0)
        for k, s in by_r[r]:
            q = s - r
            fn(k, xr[q:q + CONV_ROWS])


def _conv_fwd(name, z, blk0, gate_blk0, w_pad, bias, taps, seq, tc, out_dtype):
    T = z.shape[0]
    C = w_pad.shape[1]
    nb, nj = T // seq, C // tc
    pad = 8 * ((taps - 1 + 7) // 8)
    halo = pad
    shifts = [(k, pad - (taps - 1) + k) for k in range(taps)]
    glu = gate_blk0 is not None

    def body(*refs):
        if glu:
            v_ref, g_ref, w_ref, b_ref, o_ref, p_ref = refs
        else:
            v_ref, w_ref, b_ref, o_ref, p_ref = refs
        p_ref[pl.ds(0, pad), :] = jnp.zeros((pad, tc), F32)
        u = v_ref[...].astype(F32)
        if glu:
            u = u * _sig(g_ref[...].astype(F32))
        p_ref[pl.ds(pad, seq), :] = u

        def step(c, _):
            base = pl.multiple_of(c * CONV_ROWS, CONV_ROWS)
            x = p_ref[pl.ds(base, CONV_ROWS + halo), :]
            acc = [jnp.zeros((CONV_ROWS, tc), F32) + b_ref[...]]

            def tap(k, xs):
                acc[0] = acc[0] + w_ref[k:k + 1, :] * xs

            _shifted_taps(x, halo, shifts, tap)
            o_ref[pl.ds(base, CONV_ROWS), :] = acc[0].astype(out_dtype)
            return 0

        lax.fori_loop(0, seq // CONV_ROWS, step, 0)

    zin = [(z, _bs((seq, tc), lambda b, j: (b, blk0 + j)))]
    if glu:
        zin.append((z, _bs((seq, tc), lambda b, j: (b, gate_blk0 + j))))
    ins = zin + [(w_pad, _bs((w_pad.shape[0], tc), lambda b, j: (0, j))), (bias, _bs((1, tc), lambda b, j: (0, j)))]
    return pl.pallas_call(
        body, name=name, grid=(nb, nj), in_specs=[s for _, s in ins],
        out_specs=_bs((seq, tc), lambda b, j: (b, j)), out_shape=_sds((T, C), out_dtype),
        scratch_shapes=[pltpu.VMEM((seq + pad, tc), F32)],
        compiler_params=_cp(dimension_semantics=("arbitrary", "arbitrary")),
    )(*[a for a, _ in ins])


def _conv_bwd(name, dy, z, dz, blk0, gate_blk0, w_pad, taps, seq, tc):
    T = z.shape[0]
    C = w_pad.shape[1]
    nb, nj = T // seq, C // tc
    kp = w_pad.shape[0]
    pad = 8 * ((taps - 1 + 7) // 8)
    halo = pad
    sh_du = [(k, taps - 1 - k) for k in range(taps)]
    sh_dw = [(k, pad - (taps - 1) + k) for k in range(taps)]
    glu = gate_blk0 is not None

    def body(*refs):
        if glu:
            dy_ref, v_ref, g_ref, w_ref, _dz_in, dz_out, dw_ref, db_ref, pdy, pu, du_s, wacc, ob, ob2 = refs
        else:
            dy_ref, v_ref, w_ref, _dz_in, dz_out, dw_ref, db_ref, pdy, pu, du_s, wacc, ob = refs
        j = pl.program_id(0)
        b = pl.program_id(1)
        pdy[pl.ds(seq, pad), :] = jnp.zeros((pad, tc), F32)
        pdy[pl.ds(0, seq), :] = dy_ref[...].astype(F32)
        pu[pl.ds(0, pad), :] = jnp.zeros((pad, tc), F32)
        v = v_ref[...].astype(F32)
        if glu:
            sg = _sig(g_ref[...].astype(F32))
            pu[pl.ds(pad, seq), :] = v * sg
        else:
            pu[pl.ds(pad, seq), :] = v
        wacc[...] = jnp.zeros(wacc.shape, F32)

        def step(c, dbacc):
            base = pl.multiple_of(c * CONV_ROWS, CONV_ROWS)
            xdy = pdy[pl.ds(base, CONV_ROWS + halo), :]
            acc = [jnp.zeros((CONV_ROWS, tc), F32)]

            def tap(k, xs):
                acc[0] = acc[0] + w_ref[k:k + 1, :] * xs

            _shifted_taps(xdy, halo, sh_du, tap)
            du_s[pl.ds(base, CONV_ROWS), :] = acc[0]
            dyc = xdy[0:CONV_ROWS]
            xu = pu[pl.ds(base, CONV_ROWS + halo), :]

            def wtap(k, xs):
                p = dyc * xs
                s8 = p[0:8]
                for m in range(1, CONV_ROWS // 8):
                    s8 = s8 + p[8 * m:8 * m + 8]
                wacc[pl.ds(8 * k, 8), :] += s8

            _shifted_taps(xu, halo, sh_dw, wtap)
            d8 = dyc[0:8]
            for m in range(1, CONV_ROWS // 8):
                d8 = d8 + dyc[8 * m:8 * m + 8]
            return dbacc + d8

        dbacc = lax.fori_loop(0, seq // CONV_ROWS, step, jnp.zeros((8, tc), F32))
        du = du_s[...]
        rows = pl.ds(pl.multiple_of(b * seq, seq), seq)
        if glu:
            ob[...] = (du * sg).astype(BF16)
            ob2[...] = (du * v * sg * (1.0 - sg)).astype(BF16)
            pltpu.sync_copy(ob2, dz_out.at[rows, pl.ds(pl.multiple_of((gate_blk0 + j) * tc, tc), tc)])
        else:
            ob[...] = du.astype(BF16)
        pltpu.sync_copy(ob, dz_out.at[rows, pl.ds(pl.multiple_of((blk0 + j) * tc, tc), tc)])
        dw = jnp.sum(wacc[...].reshape(kp, 8, tc), axis=1)
        _acc_rows(dw_ref, dw, b == 0)
        _acc_rows(db_ref, jnp.sum(dbacc, axis=0, keepdims=True), b == 0)

    zin = [(z, _bs((seq, tc), lambda j, b: (b, blk0 + j)))]
    if glu:
        zin.append((z, _bs((seq, tc), lambda j, b: (b, gate_blk0 + j))))
    ins = [(dy, _bs((seq, tc), lambda j, b: (b, j)))] + zin + [(w_pad, _bs((kp, tc), lambda j, b: (0, j))),
                                                               (dz, pl.BlockSpec(memory_space=pl.ANY))]
    dz_idx = len(ins) - 1
    out_specs = [pl.BlockSpec(memory_space=pl.ANY), _bs((kp, tc), lambda j, b: (0, j)), _bs((1, tc), lambda j, b: (0, j))]
    out_shape = [_sds(dz.shape, dz.dtype), _sds((kp, C), F32), _sds((1, C), F32)]
    stage = [pltpu.VMEM((seq, tc), BF16)] * (2 if glu else 1)
    return pl.pallas_call(
        body, name=name, grid=(nj, nb), in_specs=[s for _, s in ins], out_specs=out_specs, out_shape=out_shape,
        input_output_aliases={dz_idx: 0},
        scratch_shapes=[pltpu.VMEM((seq + pad, tc), F32), pltpu.VMEM((seq + pad, tc), F32),
                        pltpu.VMEM((seq, tc), F32), pltpu.VMEM((8 * kp, tc), F32)] + stage,
        compiler_params=_cp(dimension_semantics=("arbitrary", "arbitrary")),
    )(*[a for a, _ in ins])


RG_ROWS = 256


def _softplus_neg(lam):
    return jnp.maximum(-lam, 0.0) + jnp.log(1.0 + jnp.exp(-jnp.abs(lam)))


def _gates(v0c, wa_ref, wx_ref, ba, bx, sp, first_row):
    vb = v0c.astype(BF16)
    r = _sig(jnp.dot(vb, wa_ref[...], preferred_element_type=F32) + ba)
    i = _sig(jnp.dot(vb, wx_ref[...], preferred_element_type=F32) + bx)
    la = -LRU_C * r * sp
    a = jnp.exp(la)
    a2 = a * a
    x = 2.0 * la
    series = -x * (1.0 + x * (1.0 / 2) * (1.0 + x * (1.0 / 3) * (1.0 + x * (1.0 / 4) * (1.0 + x * (1.0 / 5)))))
    mult = jnp.sqrt(jnp.where(x > -0.1, series, 1.0 - a2))
    dmult = jnp.where(first_row, 0.0, -a2 / mult)
    mult = jnp.where(first_row, 1.0, mult)
    return r, i, a, mult, dmult


def _group_scan(a, b, reverse):
    n = a.shape[0]
    row = lax.broadcasted_iota(jnp.int32, a.shape, 0) & 7
    for d in (1, 2, 4):
        sh = n - d if reverse else d
        a_s, b_s = pltpu.roll(a, sh, 0), pltpu.roll(b, sh, 0)
        m = (row < 8 - d) if reverse else (row >= d)
        b = jnp.where(m, a * b_s + b, b)
        a = jnp.where(m, a * a_s, a)
    return a, b


def _group_carry(a_s, b_s, o_s, n_groups, reverse):
    cols = a_s.shape[1]

    def step(g, carry):
        g = n_groups - 1 - g if reverse else g
        rows = pl.ds(pl.multiple_of(g * 8, 8), 8)
        o = a_s[rows, :] * carry + b_s[rows, :]
        o_s[rows, :] = o
        return o[0:1, :] if reverse else o[7:8, :]

    lax.fori_loop(0, n_groups, step, jnp.zeros((1, cols), F32))


def _rglru_fwd(v0, wa, wx, ba, bx, lam, seq):
    T, C = v0.shape
    ng, G = wa.shape[0], wa.shape[1]
    nb = T // seq

    def body(v_ref, wa_ref, wx_ref, ba_ref, bx_ref, lam_ref, h_ref, a_s, b_s, h_s):
        sp = _softplus_neg(lam_ref[...])

        def chunk(c, _):
            rows = pl.ds(pl.multiple_of(c * RG_ROWS, RG_ROWS), RG_ROWS)
            t = lax.broadcasted_iota(jnp.int32, (RG_ROWS, G), 0) + c * RG_ROWS
            v0c = v_ref[rows, :]
            _, i, a, mult, _ = _gates(v0c, wa_ref, wx_ref, ba_ref[...], bx_ref[...], sp, t == 0)
            a_g, b_g = _group_scan(a, mult * i * v0c, False)
            a_s[rows, :] = a_g
            b_s[rows, :] = b_g
            return 0

        lax.fori_loop(0, seq // RG_ROWS, chunk, 0)
        _group_carry(a_s, b_s, h_s, seq // 8, False)
        h_ref[...] = h_s[...].astype(BF16)

    t2 = _bs((seq, G), lambda b, g: (b, g))
    wsp = _bs((None, G, G), lambda b, g: (g, 0, 0))
    row = _bs((1, G), lambda b, g: (0, g))
    return pl.pallas_call(
        body, name="rglru_fwd", grid=(nb, ng), in_specs=[t2, wsp, wsp, row, row, row], out_specs=t2,
        out_shape=_sds((T, C), BF16), scratch_shapes=[pltpu.VMEM((seq, G), F32)] * 3,
        compiler_params=_cp(dimension_semantics=("arbitrary", "arbitrary")),
    )(v0, wa, wx, ba, bx, lam)


def _rglru_bwd(v0, h, dh, wa, wx, ba, bx, lam, seq):
    T, C = v0.shape
    ng, G = wa.shape[0], wa.shape[1]
    nb = T // seq
    R = RG_ROWS

    def body(v_ref, h_ref, dh_ref, wa_ref, wx_ref, ba_ref, bx_ref, lam_ref,
             dv_ref, dwa_ref, dwx_ref, dba_ref, dbx_ref, dlam_ref, a_s, b_s, q_s, hp_s):
        b = pl.program_id(1)
        lam_v = lam_ref[...]
        sp = _softplus_neg(lam_v)
        dsp_dlam = -_sig(-lam_v)

        @pl.when(b == 0)
        def _():
            dwa_ref[...] = jnp.zeros((G, G), F32)
            dwx_ref[...] = jnp.zeros((G, G), F32)
            dba_ref[...] = jnp.zeros((1, G), F32)
            dbx_ref[...] = jnp.zeros((1, G), F32)
            dlam_ref[...] = jnp.zeros((1, G), F32)

        hp_s[pl.ds(0, 8), :] = jnp.zeros((8, G), F32)
        hp_s[pl.ds(8, seq), :] = h_ref[...].astype(F32)
        q_s[pl.ds(seq, 8), :] = jnp.zeros((8, G), F32)

        def chunk1(c, _):
            rows = pl.ds(pl.multiple_of(c * R, R), R)
            t = lax.broadcasted_iota(jnp.int32, (R, G), 0) + c * R
            _, _, a, _, _ = _gates(v_ref[rows, :], wa_ref, wx_ref, ba_ref[...], bx_ref[...], sp, t == 0)
            a_g, b_g = _group_scan(a, a * dh_ref[rows, :].astype(F32), True)
            a_s[rows, :] = a_g
            b_s[rows, :] = b_g
            return 0

        lax.fori_loop(0, seq // R, chunk1, 0)
        _group_carry(a_s, b_s, q_s, seq // 8, True)

        def chunk3(c, _):
            base = pl.multiple_of(c * R, R)
            rows = pl.ds(base, R)
            t = lax.broadcasted_iota(jnp.int32, (R, G), 0) + c * R
            v0c = v_ref[rows, :]
            r, i, a, mult, dmult_dla = _gates(v0c, wa_ref, wx_ref, ba_ref[...], bx_ref[...], sp, t == 0)
            q_next = pltpu.roll(q_s[pl.ds(base, R + 8), :], R + 7, 0)[0:R]
            h_prev = pltpu.roll(hp_s[pl.ds(base, R + 8), :], R + 1, 0)[0:R]
            gt = dh_ref[rows, :].astype(F32) + q_next
            dla = gt * h_prev * a + gt * i * v0c * dmult_dla
            dpa = dla * (-LRU_C * sp) * r * (1.0 - r)
            dpx = gt * mult * v0c * i * (1.0 - i)
            dpa_b, dpx_b, v_b = dpa.astype(BF16), dpx.astype(BF16), v0c.astype(BF16)
            dv_ref[rows, :] = (gt * mult * i
                               + lax.dot_general(dpa_b, wa_ref[...], _DIMS["nt"], preferred_element_type=F32)
                               + lax.dot_general(dpx_b, wx_ref[...], _DIMS["nt"], preferred_element_type=F32))
            dwa_ref[...] += lax.dot_general(v_b, dpa_b, _DIMS["tn"], preferred_element_type=F32)
            dwx_ref[...] += lax.dot_general(v_b, dpx_b, _DIMS["tn"], preferred_element_type=F32)
            dba_ref[...] += jnp.sum(dpa, axis=0, keepdims=True)
            dbx_ref[...] += jnp.sum(dpx, axis=0, keepdims=True)
            dlam_ref[...] += jnp.sum(dla * (-LRU_C * r), axis=0, keepdims=True) * dsp_dlam
            return 0

        lax.fori_loop(0, seq // R, chunk3, 0)

    t2 = _bs((seq, G), lambda g, b: (b, g))
    wsp = _bs((None, G, G), lambda g, b: (g, 0, 0))
    row = _bs((1, G), lambda g, b: (0, g))
    return pl.pallas_call(
        body, name="rglru_bwd", grid=(ng, nb), in_specs=[t2, t2, t2, wsp, wsp, row, row, row],
        out_specs=[t2, wsp, wsp, row, row, row],
        out_shape=[_sds((T, C), F32), _sds((ng, G, G), F32), _sds((ng, G, G), F32),
                   _sds((1, C), F32), _sds((1, C), F32), _sds((1, C), F32)],
        scratch_shapes=[pltpu.VMEM((seq, G), F32), pltpu.VMEM((seq, G), F32),
                        pltpu.VMEM((seq + 8, G), F32), pltpu.VMEM((seq + 8, G), F32)],
        compiler_params=_cp(dimension_semantics=("arbitrary", "arbitrary")),
    )(v0, h, dh, wa, wx, ba, bx, lam)


TC_A = 256
TC_B = 512
TAPS_A, TAPS_B = 31, 4


def _tiles(T):
    return min(512, T), min(1024, T)


def _layer_fwd(x, p, seq):
    T, D = x.shape
    C, R = p["ln_g"].shape[1], p["lam"].shape[1]
    tm, tl = _tiles(T)
    gb_blk, sa_blk = (2 * C + R) // TC_B, (2 * C + 2 * R) // D
    z, h = _fwd_norm_mm("fwd_z", x, p["g_mix"], p["w_in"], p["b_in"], tl, 1024)
    u1 = _conv_fwd("conv_a_fwd", z, 0, C // TC_A, p["caw"], p["cab"], TAPS_A, seq, TC_A, BF16)
    ya = _fwd_ya(u1, p["ln_g"], p["ln_b"], p["w_a_out"], tm)
    v0 = _conv_fwd("conv_b_fwd", z, 2 * C // TC_B, None, p["cbw"], p["cbb"], TAPS_B, seq, TC_B, F32)
    hr = _rglru_fwd(v0, p["wa"], p["wx"], p["b_rg_a"], p["b_rg_x"], p["lam"], seq)
    yb = _fwd_yb(hr, z, gb_blk, p["w_b_out"], tl, TC_B)
    x1 = _fwd_x1(x, ya, yb, z, sa_blk, p["w_o"], tm)
    fp, h2 = _fwd_norm_mm("fwd_f", x1, p["g_mlp"], p["w_1"], None, tl, 1024)
    x2 = _fwd_x2(x1, fp, p["w_2"], tm, 2048)
    return x2, dict(x=x, z=z, h=h, u1=u1, ya=ya, v0=v0, hr=hr, yb=yb, x1=x1, fp=fp, h2=h2)


def _layer_bwd(dx2, dx2b, p, s, seq):
    T, D = dx2.shape
    C, R = p["ln_g"].shape[1], p["lam"].shape[1]
    tm, tl = _tiles(T)
    gb_blk, sa_blk = (2 * C + R) // TC_B, (2 * C + 2 * R) // D
    z = s["z"]
    g = {}

    def relu2(a_refs, out_refs, i, j, k):
        f = jnp.maximum(a_refs[0][...].astype(F32), 0.0)
        return (f * f).astype(BF16)

    dfp = _bwd_df(dx2b, p["w_2"], s["fp"], tl, 1024)
    tw = min(2048, T)
    g["w_2"] = _bwd_dw("bwd_dw2", s["fp"], dx2b, 1024, D, tw, a_fn=relu2)
    dx1, dx1b, g["g_mlp"] = _bwd_norm("bwd_dh2", dfp, p["w_1"], s["x1"], p["g_mlp"], dx2, tm, 2048)
    g["w_1"] = _bwd_dw("bwd_dw1", s["h2"], dfp, D // 2, 1024, tw, shard_cols=dfp.shape[1] // N_CHIPS)

    dya, dyb, dz = _bwd_dm(dx1b, p["w_o"], s["ya"], s["yb"], z, sa_blk, tm)

    def merged(a_refs, out_refs, i, j, k):
        ya_, yb_, sa_, sb_ = (r[...].astype(F32) for r in a_refs)
        return (_sig(sa_) * ya_ + _sig(sb_) * yb_).astype(BF16)

    tk = _bs((tm, D), lambda i, j, k: (k, 0))
    g["w_o"] = _bwd_dw("bwd_dwo", s["ya"], dx1b, D, D, tm, a_fn=merged,
                       a_extra=[(s["yb"], tk), (z, _bs((tm, D), lambda i, j, k: (k, sa_blk))),
                                (z, _bs((tm, D), lambda i, j, k: (k, sa_blk + 1)))])

    du1, g["ln_g"], g["ln_b"] = _bwd_du3(dya, p["w_a_out"], s["u1"], p["ln_g"], p["ln_b"], tm)

    def act_a(a_refs, out_refs, i, j, k):
        return _ln_silu(a_refs[0][...].astype(F32), a_refs[1][...], a_refs[2][...])[0].astype(BF16)

    rowc = _bs((1, C), lambda i, j, k: (0, 0))
    g["w_a_out"] = _bwd_dw("bwd_dwa", s["u1"], dya, C, D, tm, a_fn=act_a,
                           a_extra=[(p["ln_g"], rowc), (p["ln_b"], rowc)])
    dz, g["caw"], g["cab"] = _conv_bwd("conv_a_bwd", du1, z, dz, 0, C // TC_A, p["caw"], TAPS_A, seq, TC_A)

    dhr, dz = _bwd_dp(dyb, p["w_b_out"], s["hr"], z, dz, gb_blk, tl, TC_B)

    def act_b(a_refs, out_refs, i, j, k):
        ge, _ = _gelu(a_refs[1][...].astype(F32))
        return (a_refs[0][...].astype(F32) * ge).astype(BF16)

    tb = min(1024, T)
    g["w_b_out"] = _bwd_dw("bwd_dwb", s["hr"], dyb, TC_B, D, tb, a_fn=act_b,
                           a_extra=[(z, _bs((tb, TC_B), lambda i, j, k: (k, gb_blk + i)))])
    dv0, g["wa"], g["wx"], g["b_rg_a"], g["b_rg_x"], g["lam"] = _rglru_bwd(
        s["v0"], s["hr"], dhr, p["wa"], p["wx"], p["b_rg_a"], p["b_rg_x"], p["lam"], seq)
    dz, g["cbw"], g["cbb"] = _conv_bwd("conv_b_bwd", dv0, z, dz, 2 * C // TC_B, None, p["cbw"], TAPS_B, seq, TC_B)

    dx, dxb, g["g_mix"], dbin = _bwd_norm("bwd_dh", dz, p["w_in"], s["x"], p["g_mix"], dx1, tm, dz.shape[1] // 2,
                                          colsum=True)
    g["b_in"] = dbin.reshape(1, -1)
    ns = dz.shape[1] // N_CHIPS
    g["w_in"] = _bwd_dw("bwd_dwin", s["h"], dz, D // 2, ns // 2, tw, shard_cols=ns)
    return dx, dxb, g


ANY = pl.BlockSpec(memory_space=pl.ANY)


def _mesh_pos():
    return lax.axis_index("x"), lax.axis_index("y"), lax.axis_index("c")


def _other_chips(x, y):
    return [(1 - x, y), (x, 1 - y), (1 - x, 1 - y)]


def _remote(src, dst, ssem, rsem, dev):
    return pltpu.make_async_remote_copy(src_ref=src, dst_ref=dst, send_sem=ssem, recv_sem=rsem,
                                        device_id=dev, device_id_type=MESH)


def _place_own(shard, by_cols, k_me):
    rows, cols = shard.shape
    if by_cols:
        return lax.dynamic_update_slice(lax.empty((rows, N_CHIPS * cols), shard.dtype), shard, (0, k_me * cols))
    return lax.dynamic_update_slice(lax.empty((N_CHIPS * rows, cols), shard.dtype), shard, (k_me * rows, 0))


def _gather_region(src, dst, by_cols, k, half):
    rows, cols = src.shape
    nr = rows if half is None else rows // 2
    r0 = 0 if half is None else half * nr
    if by_cols:
        return dst.at[pl.ds(r0, nr), pl.ds(pl.multiple_of(k * cols, 128), cols)]
    return dst.at[pl.ds(pl.multiple_of(k * rows + r0, 8), nr), :]


def _gather_sends(src, dst, kinds, send, recv):
    x, y, c = _mesh_pos()
    cps = []
    for t in range(len(src)):
        half = c if kinds[t][1] else None
        hr = src[t].shape[0] // 2
        s_ref = src[t].at[pl.ds(c * hr, hr), :] if kinds[t][1] else src[t]
        for j, chip in enumerate(_other_chips(x, y)):
            cps.append(_remote(s_ref, _gather_region(src[t], dst[t], kinds[t][0], 2 * x + y, half),
                               send.at[t, j], recv.at[t, j], (*chip, c)))
    return cps


def _gather_finish(src, dst, kinds, send, recv, fsend, frecv):
    x, y, c = _mesh_pos()
    chips = _other_chips(x, y)
    sib = (x, y, 1 - c)
    n = len(src)
    fwd = []
    for t in range(n):
        half = c if kinds[t][1] else None
        for j, chip in enumerate(chips):
            got = _gather_region(src[t], dst[t], kinds[t][0], 2 * chip[0] + chip[1], half)
            _remote(got, got, send.at[t, j], recv.at[t, j], (*chip, c)).wait_recv()
            if kinds[t][1]:
                cp = _remote(got, got, fsend.at[t, j], frecv.at[t, j], sib)
                cp.start()
                fwd.append(cp)
    for t in range(n):
        if kinds[t][1]:
            for j, chip in enumerate(chips):
                got = _gather_region(src[t], dst[t], kinds[t][0], 2 * chip[0] + chip[1], 1 - c)
                _remote(got, got, fsend.at[t, j], frecv.at[t, j], sib).wait_recv()
    for cp in _gather_sends(src, dst, kinds, send, recv) + fwd:
        cp.wait_send()


def _gather_sems(n):
    sem = pltpu.SemaphoreType.DMA
    return [sem((n, 3)), sem((n, 3)), sem((n, 3)), sem((n, 3))]


def _gather_layer(shards, placed, kinds):
    n = len(shards)

    def body(*refs):
        src, dst, sems = refs[:n], refs[2 * n:3 * n], refs[3 * n:]
        for cp in _gather_sends(src, dst, kinds, sems[0], sems[1]):
            cp.start()
        _gather_finish(src, dst, kinds, *sems)

    return pl.pallas_call(
        body, name="gather_layer", in_specs=[ANY] * (2 * n), out_specs=[ANY] * n,
        out_shape=[_sds(a.shape, a.dtype) for a in placed], scratch_shapes=_gather_sems(n),
        input_output_aliases={n + t: t for t in range(n)}, compiler_params=_cp(has_side_effects=True),
    )(*shards, *placed)


def _swap_halves(pgs):
    n = len(pgs)

    def body(*refs):
        src, dst = refs[:n], refs[n:2 * n]
        send, recv = refs[2 * n:]
        x, y, c = _mesh_pos()
        cps = [_remote(src[t].at[:, 1 - c], dst[t], send.at[t], recv.at[t], (x, y, 1 - c)) for t in range(n)]
        for cp in cps:
            cp.start()
        for cp in cps:
            cp.wait_recv()
        for cp in cps:
            cp.wait_send()

    sem = pltpu.SemaphoreType.DMA
    return pl.pallas_call(
        body, name="swap_halves", in_specs=[ANY] * n, out_specs=[ANY] * n,
        out_shape=[_sds((a.shape[0],) + a.shape[2:], a.dtype) for a in pgs],
        scratch_shapes=[sem((n,)), sem((n,))], compiler_params=_cp(has_side_effects=True),
    )(*pgs)


def _scatter_chips(ps):
    n = len(ps)

    def body(*refs):
        src, dst = refs[:n], refs[n:2 * n]
        send, recv = refs[2 * n:]
        x, y, c = _mesh_pos()
        cps = []
        for t in range(n):
            for j, chip in enumerate(_other_chips(x, y)):
                cps.append(_remote(src[t].at[2 * chip[0] + chip[1]], dst[t].at[j], send.at[t, j], recv.at[t, j], (*chip, c)))
        for cp in cps:
            cp.start()
        for cp in cps:
            cp.wait_recv()
        for cp in cps:
            cp.wait_send()

    sem = pltpu.SemaphoreType.DMA
    return pl.pallas_call(
        body, name="scatter_chips", in_specs=[ANY] * n, out_specs=[ANY] * n,
        out_shape=[_sds((3,) + a.shape[1:], a.dtype) for a in ps],
        scratch_shapes=[sem((n, 3)), sem((n, 3))], compiler_params=_cp(has_side_effects=True),
    )(*ps)


def _join_halves(accs):
    n = len(accs)

    def body(*refs):
        buf = refs[n:2 * n]
        send, recv = refs[2 * n:]
        x, y, c = _mesh_pos()
        cps = [_remote(buf[t].at[:, c], buf[t].at[:, c], send.at[t], recv.at[t], (x, y, 1 - c)) for t in range(n)]
        for cp in cps:
            cp.start()
        for t in range(n):
            _remote(buf[t].at[:, c], buf[t].at[:, 1 - c], send.at[t], recv.at[t], (x, y, 1 - c)).wait_recv()
        for cp in cps:
            cp.wait_send()

    sem = pltpu.SemaphoreType.DMA
    return pl.pallas_call(
        body, name="join_halves", in_specs=[ANY] * n, out_specs=[ANY] * n,
        out_shape=[_sds(a.shape, a.dtype) for a in accs], scratch_shapes=[sem((n,)), sem((n,))],
        input_output_aliases={t: t for t in range(n)}, compiler_params=_cp(has_side_effects=True),
    )(*accs)


def _sum_siblings(pg, rb, c_arr):
    nk, _, hr, cols = pg.shape

    def body(c_ref, a_ref, b_ref, o_ref):
        o_ref[...] = (a_ref[...].astype(F32) + b_ref[...].astype(F32)).astype(BF16)

    return pl.pallas_call(
        body, name="sum_siblings",
        grid_spec=pltpu.PrefetchScalarGridSpec(
            num_scalar_prefetch=1, grid=(nk,),
            in_specs=[pl.BlockSpec((None, None, hr, cols), lambda k, c_ref: (k, c_ref[0], 0, 0)),
                      pl.BlockSpec((None, hr, cols), lambda k, c_ref: (k, 0, 0))],
            out_specs=pl.BlockSpec((None, hr, cols), lambda k, c_ref: (k, 0, 0))),
        out_shape=_sds((nk, hr, cols), BF16), compiler_params=_cp(dimension_semantics=("arbitrary",)),
    )(c_arr, pg, rb)


def _sum_chips(p, rb, kcl, acc):
    _, hr, cols = p.shape

    def body(k_ref, a_ref, b_ref, _acc_in, o_ref):
        o_ref[...] = a_ref[...].astype(F32) + b_ref[0].astype(F32) + b_ref[1].astype(F32) + b_ref[2].astype(F32)

    return pl.pallas_call(
        body, name="sum_chips",
        grid_spec=pltpu.PrefetchScalarGridSpec(
            num_scalar_prefetch=1, grid=(1,),
            in_specs=[pl.BlockSpec((None, hr, cols), lambda i, k_ref: (k_ref[0], 0, 0)),
                      pl.BlockSpec((3, hr, cols), lambda i, k_ref: (0, 0, 0)), ANY],
            out_specs=pl.BlockSpec((None, None, hr, cols), lambda i, k_ref: (k_ref[2], k_ref[1], 0, 0))),
        out_shape=_sds(acc.shape, F32), input_output_aliases={3: 0},
        compiler_params=_cp(dimension_semantics=("arbitrary",)),
    )(kcl, p, rb, acc)


N_DEV = 8


def _allreduce_small(part):
    _, r, lanes = part.shape

    def body(p_ref, o_ref, rbuf, s1, r1, s2, r2):
        x, y, c = _mesh_pos()
        me = 4 * x + 2 * y + c
        devs = [(d // 4, (d // 2) % 2, d % 2) for d in range(N_DEV)]
        rbuf[me] = p_ref[me]

        def each_peer(fn):
            for d in range(N_DEV):
                @pl.when(d != me)
                def _():
                    fn(d)

        each_peer(lambda d: _remote(p_ref.at[d], rbuf.at[me], s1.at[d], r1.at[me], devs[d]).start())
        each_peer(lambda d: _remote(p_ref.at[d], rbuf.at[d], s1.at[d], r1.at[d], devs[d]).wait_recv())
        total = rbuf[0]
        for d in range(1, N_DEV):
            total = total + rbuf[d]
        o_ref[me] = total
        each_peer(lambda d: _remote(o_ref.at[me], o_ref.at[me], s2.at[d], r2.at[me], devs[d]).start())
        each_peer(lambda d: _remote(o_ref.at[d], o_ref.at[d], s2.at[d], r2.at[d], devs[d]).wait_recv())
        each_peer(lambda d: _remote(p_ref.at[d], rbuf.at[me], s1.at[d], r1.at[me], devs[d]).wait_send())
        each_peer(lambda d: _remote(o_ref.at[me], o_ref.at[me], s2.at[d], r2.at[me], devs[d]).wait_send())

    sem = pltpu.SemaphoreType.DMA
    vm = pl.BlockSpec(memory_space=pltpu.VMEM)
    return pl.pallas_call(
        body, name="allreduce_small", in_specs=[vm], out_specs=vm, out_shape=_sds(part.shape, F32),
        scratch_shapes=[pltpu.VMEM(part.shape, F32), sem((N_DEV,)), sem((N_DEV,)), sem((N_DEV,)), sem((N_DEV,))],
        compiler_params=_cp(has_side_effects=True),
    )(part)


BIG = ("w_in", "w_1", "w_a_out", "w_b_out", "w_o", "w_2")
BY_COLS = {"w_in": True, "w_1": True, "w_a_out": False, "w_b_out": False, "w_o": False, "w_2": False}
WEIGHTS = ("g_mix", "w_in", "b_in", "conv_a_w", "conv_a_b", "ln_g", "ln_b", "w_a_out", "conv_b_w", "conv_b_b", "w_rg_a",
           "b_rg_a", "w_rg_x", "b_rg_x", "lam", "w_b_out", "w_o", "g_mlp", "w_1", "w_2", "g_final")
SMALL = tuple(n for n in WEIGHTS if n not in BIG)
ADAM_ROWS = 256
ADAM_SMALL_ROWS = 2048


def _block_diag(w):
    nh, dh, _ = w.shape
    ng = nh // HEADS_PER_GROUP
    w4 = w.reshape(ng, HEADS_PER_GROUP, dh, dh)
    eye = jnp.eye(HEADS_PER_GROUP, dtype=w.dtype)
    return jnp.einsum("qhij,hk->qhikj", w4, eye).reshape(ng, HEADS_PER_GROUP * dh, HEADS_PER_GROUP * dh)


def _block_diag_part(d, dh):
    ng = d.shape[0]
    eye = jnp.eye(HEADS_PER_GROUP, dtype=d.dtype)
    d5 = d.reshape(ng, HEADS_PER_GROUP, dh, HEADS_PER_GROUP, dh)
    return jnp.einsum("qhikj,hk->qhij", d5, eye).reshape(ng * HEADS_PER_GROUP, dh, dh)


PACK_LANES = 128


def _pack(arrays, row_multiple):
    parts = [a.reshape(-1, PACK_LANES) for a in arrays]
    parts = [jnp.pad(p, ((0, -p.shape[0] % 8), (0, 0))) if p.shape[0] % 8 else p for p in parts]
    rows = sum(p.shape[0] for p in parts)
    pad = -rows % row_multiple
    if pad:
        parts.append(jnp.zeros((pad, PACK_LANES), parts[0].dtype))
    return jnp.concatenate(parts, axis=0)


def _unpack(buf, like):
    buf = buf.reshape(-1, PACK_LANES)
    out, off = [], 0
    for a in like:
        n = a.size // PACK_LANES
        out.append(buf[off:off + n].reshape(a.shape))
        off += n + (-n % 8)
    return out


def kernel(x, g_mix, w_in, b_in, conv_a_w, conv_a_b, ln_g, ln_b, w_a_out, conv_b_w, conv_b_b, w_rg_a, b_rg_a, w_rg_x, b_rg_x, lam, w_b_out, w_o, g_mlp, w_1, w_2, g_final, loss_target, m_g_mix, m_w_in, m_b_in, m_conv_a_w, m_conv_a_b, m_ln_g, m_ln_b, m_w_a_out, m_conv_b_w, m_conv_b_b, m_w_rg_a, m_b_rg_a, m_w_rg_x, m_b_rg_x, m_lam, m_w_b_out, m_w_o, m_g_mlp, m_w_1, m_w_2, m_g_final, v_g_mix, v_w_in, v_b_in, v_conv_a_w, v_conv_a_b, v_ln_g, v_ln_b, v_w_a_out, v_conv_b_w, v_conv_b_b, v_w_rg_a, v_b_rg_a, v_w_rg_x, v_b_rg_x, v_lam, v_w_b_out, v_w_o, v_g_mlp, v_w_1, v_w_2, v_g_final):
    w = dict(g_mix=g_mix, w_in=w_in, b_in=b_in, conv_a_w=conv_a_w, conv_a_b=conv_a_b, ln_g=ln_g, ln_b=ln_b, w_a_out=w_a_out,
             conv_b_w=conv_b_w, conv_b_b=conv_b_b, w_rg_a=w_rg_a, b_rg_a=b_rg_a, w_rg_x=w_rg_x, b_rg_x=b_rg_x, lam=lam,
             w_b_out=w_b_out, w_o=w_o, g_mlp=g_mlp, w_1=w_1, w_2=w_2, g_final=g_final)
    m = dict(g_mix=m_g_mix, w_in=m_w_in, b_in=m_b_in, conv_a_w=m_conv_a_w, conv_a_b=m_conv_a_b, ln_g=m_ln_g, ln_b=m_ln_b,
             w_a_out=m_w_a_out, conv_b_w=m_conv_b_w, conv_b_b=m_conv_b_b, w_rg_a=m_w_rg_a, b_rg_a=m_b_rg_a, w_rg_x=m_w_rg_x,
             b_rg_x=m_b_rg_x, lam=m_lam, w_b_out=m_w_b_out, w_o=m_w_o, g_mlp=m_g_mlp, w_1=m_w_1, w_2=m_w_2, g_final=m_g_final)
    v = dict(g_mix=v_g_mix, w_in=v_w_in, b_in=v_b_in, conv_a_w=v_conv_a_w, conv_a_b=v_conv_a_b, ln_g=v_ln_g, ln_b=v_ln_b,
             w_a_out=v_w_a_out, conv_b_w=v_conv_b_w, conv_b_b=v_conv_b_b, w_rg_a=v_w_rg_a, b_rg_a=v_b_rg_a, w_rg_x=v_w_rg_x,
             b_rg_x=v_b_rg_x, lam=v_lam, w_b_out=v_w_b_out, w_o=v_w_o, g_mlp=v_g_mlp, w_1=v_w_1, w_2=v_w_2, g_final=v_g_final)
    B, S, D = x.shape
    T = B * S
    L = w_in.shape[0]
    dh = w_rg_a.shape[-1]
    taps_a, taps_b = conv_a_w.shape[1], conv_b_w.shape[1]
    assert (taps_a, taps_b) == (TAPS_A, TAPS_B)
    xi, yi, ci = _mesh_pos()
    c_arr = jnp.reshape(ci, (1,)).astype(jnp.int32)
    k_me = 2 * xi + yi

    caw_p = jnp.pad(conv_a_w, ((0, 0), (0, 32 - taps_a), (0, 0)))
    cbw_p = jnp.pad(conv_b_w, ((0, 0), (0, 8 - taps_b), (0, 0)))
    kinds = [(BY_COLS[n], True) for n in BIG] + [(True, False), (True, False)]
    row = lambda a: a.reshape(1, -1)
    params = []
    for l in range(L):
        shards = [w[n][l].astype(BF16) for n in BIG] + [caw_p[l], cbw_p[l]]
        whole = _gather_layer(shards, [_place_own(a, k[0], k_me) for a, k in zip(shards, kinds)], kinds)
        p = dict(zip(BIG, whole[:len(BIG)]))
        p.update(caw=whole[-2], cbw=whole[-1], cab=row(conv_a_b[l]), cbb=row(conv_b_b[l]),
                 wa=_block_diag(w_rg_a[l]).astype(BF16), wx=_block_diag(w_rg_x[l]).astype(BF16))
        for n in ("g_mix", "b_in", "ln_g", "ln_b", "b_rg_a", "b_rg_x", "lam", "g_mlp"):
            p[n] = row(w[n][l])
        params.append(p)

    xf = x.reshape(T, D)
    saved = []
    for l in range(L):
        xf, s = _layer_fwd(xf, params[l], S)
        saved.append(s)
    loss_part, dx, dxb, dg_final = _loss_head(xf, row(g_final), loss_target.reshape(T, D), _tiles(T)[0])
    loss = lax.psum(loss_part[0, 0], ("x", "y", "c"))

    half_shape = lambda a: (L, 2, a.shape[1] // 2, a.shape[2])
    accs = {n: lax.empty(half_shape(w[n]), F32) for n in BIG}
    small = {n: [None] * L for n in SMALL if n != "g_final"}
    for l in reversed(range(L)):
        dx, dxb, g = _layer_bwd(dx, dxb, params[l], saved[l], S)
        pgs = []
        for n in BIG:
            a = g[n]
            if not BY_COLS[n]:
                a = a.reshape(N_CHIPS, 2, a.shape[0] // (2 * N_CHIPS), a.shape[1])
            pgs.append(a)
        got = _swap_halves(pgs)
        sums = [_sum_siblings(a, b, c_arr) for a, b in zip(pgs, got)]
        got = _scatter_chips(sums)
        kcl = jnp.stack([k_me, ci, jnp.full((), l, ci.dtype)]).astype(jnp.int32)
        for n, a, b in zip(BIG, sums, got):
            accs[n] = _sum_chips(a, b, kcl, accs[n])
        small["g_mix"][l], small["b_in"][l], small["g_mlp"][l] = g["g_mix"], g["b_in"], g["g_mlp"]
        small["conv_a_w"][l], small["conv_a_b"][l] = g["caw"], g["cab"]
        small["conv_b_w"][l], small["conv_b_b"][l] = g["cbw"], g["cbb"]
        small["ln_g"][l], small["ln_b"][l], small["lam"][l] = g["ln_g"], g["ln_b"], g["lam"]
        small["w_rg_a"][l], small["w_rg_x"][l] = _block_diag_part(g["wa"], dh), _block_diag_part(g["wx"], dh)
        small["b_rg_a"][l], small["b_rg_x"][l] = g["b_rg_a"], g["b_rg_x"]
    grad_x = dx.reshape(B, S, D)

    joined = _join_halves([accs[n] for n in BIG])
    grads = {n: a.reshape(w[n].shape) for n, a in zip(BIG, joined)}

    names = [n for n in SMALL if n != "g_final"]
    parts = [jnp.stack(small[n]) for n in names] + [dg_final]
    packed = _pack(parts, 8 * N_DEV)
    total = _allreduce_small(packed.reshape(N_DEV, packed.shape[0] // N_DEV, PACK_LANES))
    for n, a in zip(names + ["g_final"], _unpack(total, parts)):
        if n == "conv_a_w":
            a = lax.dynamic_slice_in_dim(a[:, :taps_a], k_me * conv_a_w.shape[2], conv_a_w.shape[2], axis=2)
        elif n == "conv_b_w":
            a = lax.dynamic_slice_in_dim(a[:, :taps_b], k_me * conv_b_w.shape[2], conv_b_w.shape[2], axis=2)
        grads[n] = a.reshape(w[n].shape)

    delta, new_m, new_v = {}, {}, {}
    for n in BIG:
        cols = w[n].shape[-1]
        d_, m_, v_ = _adamw("adamw_" + n, w[n].reshape(-1, cols), grads[n].reshape(-1, cols), m[n].reshape(-1, cols),
                            v[n].reshape(-1, cols), ADAM_ROWS)
        delta[n], new_m[n], new_v[n] = (a.reshape(w[n].shape) for a in (d_, m_, v_))
    like = [w[n] for n in SMALL]
    pk = lambda d: _pack([d[n] for n in SMALL], ADAM_SMALL_ROWS)
    d_, m_, v_ = _adamw("adamw_small", pk(w), pk(grads), pk(m), pk(v), ADAM_SMALL_ROWS)
    for n, a, b_, c_ in zip(SMALL, _unpack(d_, like), _unpack(m_, like), _unpack(v_, like)):
        delta[n], new_m[n], new_v[n] = a, b_, c_

    return (loss, grad_x, *[grads[n] for n in WEIGHTS], *[delta[n] for n in WEIGHTS],
            *[new_m[n] for n in WEIGHTS], *[new_v[n] for n in WEIGHTS])
```

```python
import jax
import jax.numpy as jnp
from jax import lax
from jax.experimental import pallas as pl
from jax.experimental.pallas import tpu as pltpu

F32 = jnp.float32
BF16 = jnp.bfloat16
MESH = pl.DeviceIdType.MESH

EPS = 1e-6
LRU_C = 8.0
ADAM_LR, ADAM_B1, ADAM_B2, ADAM_EPS, ADAM_WD, ADAM_STEP = 0.001, 0.9, 0.999, 1e-08, 0.01, 10

N_CHIPS = 4
HEADS_PER_GROUP = 4
VMEM_LIMIT = 56 * 1024 * 1024


def _cp(**kw):
    return pltpu.CompilerParams(vmem_limit_bytes=VMEM_LIMIT, **kw)


def _sig(x):
    return 1.0 / (1.0 + jnp.exp(-x))


def _gelu(x):
    t = jnp.tanh(0.7978845608028654 * (x + 0.044715 * x * x * x))
    return 0.5 * x * (1.0 + t), t


def _gelu_grad(x, t):
    dt = (1.0 - t * t) * 0.7978845608028654 * (1.0 + 3.0 * 0.044715 * x * x)
    return 0.5 * (1.0 + t) + 0.5 * x * dt


def _rms(xf, g):
    r = lax.rsqrt(jnp.mean(xf * xf, axis=-1, keepdims=True) + EPS)
    return xf * r * g, r


def _rms_bwd(xf, g, r, dh):
    dyg = dh * g
    dx = r * (dyg - xf * (r * r) * jnp.mean(dyg * xf, axis=-1, keepdims=True))
    return dx, dh * xf * r


def _ln_silu(u, g, b):
    mu = jnp.mean(u, axis=-1, keepdims=True)
    uc = u - mu
    rstd = lax.rsqrt(jnp.mean(uc * uc, axis=-1, keepdims=True) + EPS)
    uh = uc * rstd
    u2 = uh * g + b
    s = _sig(u2)
    return u2 * s, uh, rstd, u2, s


_DIMS = {"nn": (((1,), (0,)), ((), ())), "nt": (((1,), (1,)), ((), ())), "tn": (((0,), (0,)), ((), ()))}


class _Comm:
    def __init__(self, ins, outs, sems, start, finish):
        self.ins, self.outs, self.sems, self.start, self.finish = list(ins), list(outs), list(sems), start, finish


def _mm(name, mode, grid, a_ins, a_fn, b_in, e_ins, epi, outs, acc_shape, cache_a=None, alias=(), extra_scratch=(),
        comm=None):
    ni, nj, nk = grid
    na, ne, no = len(a_ins), len(e_ins), len(outs)
    assert cache_a is None or nk == 1
    n_fixed = (nk > 1) + (cache_a is not None)
    n_in = na + 1 + ne + len(alias)
    c_ins, c_outs, c_sems = (comm.ins, comm.outs, comm.sems) if comm else ([], [], [])

    def body(*refs):
        a_refs = refs[:na]
        b_ref = refs[na]
        e_refs = refs[na + 1:na + 1 + ne]
        comm_in = refs[n_in:n_in + len(c_ins)]
        out0 = n_in + len(c_ins)
        out_refs = refs[out0:out0 + no]
        comm_out = refs[out0 + no:out0 + no + len(c_outs)]
        scratch = refs[out0 + no + len(c_outs):]
        extra = scratch[n_fixed:n_fixed + len(extra_scratch)]
        comm_sems = scratch[n_fixed + len(extra_scratch):]
        i, j, k = pl.program_id(0), pl.program_id(1), pl.program_id(2)
        if comm:
            @pl.when((i == 0) & (j == 0) & (k == 0))
            def _():
                comm.start(comm_in, comm_out, comm_sems)
        if cache_a is not None:
            cache_ref = scratch[n_fixed - 1]

            @pl.when(j == 0)
            def _():
                cache_ref[...] = a_fn(a_refs, out_refs, i, j, k)

            a = cache_ref[...]
        else:
            a = a_fn(a_refs, out_refs, i, j, k)
        prod = lax.dot_general(a, b_ref[...], _DIMS[mode], preferred_element_type=F32)
        if nk == 1:
            epi(prod, e_refs, out_refs, i, j, extra)
        else:
            acc_ref = scratch[0]

            @pl.when(k == 0)
            def _():
                acc_ref[...] = prod

            @pl.when(k > 0)
            def _():
                acc_ref[...] += prod

            @pl.when(k == nk - 1)
            def _():
                epi(acc_ref[...], e_refs, out_refs, i, j, extra)

        if comm:
            @pl.when((i == ni - 1) & (j == nj - 1) & (k == nk - 1))
            def _():
                comm.finish(comm_in, comm_out, comm_sems)

    scratch_shapes = []
    if nk > 1:
        scratch_shapes.append(pltpu.VMEM(acc_shape, F32))
    if cache_a is not None:
        scratch_shapes.append(pltpu.VMEM(cache_a, BF16))
    any_spec = pl.BlockSpec(memory_space=pl.ANY)
    ins = (list(a_ins) + [b_in] + list(e_ins) + [(arr, any_spec) for arr, _ in alias] + [(arr, any_spec) for arr in c_ins])
    first_alias = na + 1 + ne
    res = pl.pallas_call(
        body, name=name, grid=grid,
        in_specs=[s for _, s in ins], out_specs=[s for _, s in outs] + [any_spec] * len(c_outs),
        out_shape=[o for o, _ in outs] + list(c_outs),
        scratch_shapes=scratch_shapes + list(extra_scratch) + list(c_sems),
        input_output_aliases={first_alias + n: o for n, (_, o) in enumerate(alias)},
        compiler_params=_cp(dimension_semantics=("arbitrary", "arbitrary", "arbitrary"), has_side_effects=bool(comm)),
    )(*[a for a, _ in ins])
    if comm:
        return list(res[:no]), list(res[no:])
    return res


def _bs(shape, fn):
    return pl.BlockSpec(shape, fn)


def _sds(shape, dt):
    return jax.ShapeDtypeStruct(shape, dt)


def _acc_rows(ref, val, first):
    @pl.when(first)
    def _():
        ref[...] = val

    @pl.when(jnp.logical_not(first))
    def _():
        ref[...] += val


def _fwd_norm_mm(name, x, g, w, bias, tm, tn, comm=None):
    T, D = x.shape
    N = w.shape[1]

    def a_fn(a_refs, out_refs, i, j, k):
        h, _ = _rms(a_refs[0][...], a_refs[1][...])
        hb = h.astype(BF16)
        out_refs[1][...] = hb
        return hb

    def epi(acc, e_refs, out_refs, i, j, extra):
        if bias is not None:
            acc = acc + e_refs[0][...]
        out_refs[0][...] = acc.astype(BF16)

    e_ins = [] if bias is None else [(bias, _bs((1, tn), lambda i, j, k: (0, j)))]
    return _mm(name, "nn", (T // tm, N // tn, 1),
               [(x, _bs((tm, D), lambda i, j, k: (i, 0))), (g, _bs((1, D), lambda i, j, k: (0, 0)))], a_fn,
               (w, _bs((D, tn), lambda i, j, k: (0, j))), e_ins, epi,
               [(_sds((T, N), BF16), _bs((tm, tn), lambda i, j, k: (i, j))),
                (_sds((T, D), BF16), _bs((tm, D), lambda i, j, k: (i, 0)))],
               None, cache_a=(tm, D), comm=comm)


def _fwd_ya(u1, ln_g, ln_b, w, tm):
    T, C = u1.shape
    N = w.shape[1]

    def a_fn(a_refs, out_refs, i, j, k):
        u3 = _ln_silu(a_refs[0][...].astype(F32), a_refs[1][...], a_refs[2][...])[0]
        return u3.astype(BF16)

    def epi(acc, e_refs, out_refs, i, j, extra):
        out_refs[0][...] = acc.astype(BF16)

    row = _bs((1, C), lambda i, j, k: (0, 0))
    return _mm("fwd_ya", "nn", (T // tm, 1, 1),
               [(u1, _bs((tm, C), lambda i, j, k: (i, 0))), (ln_g, row), (ln_b, row)], a_fn,
               (w, _bs((C, N), lambda i, j, k: (0, 0))), [], epi,
               [(_sds((T, N), BF16), _bs((tm, N), lambda i, j, k: (i, 0)))], None)[0]


def _fwd_yb(h, z, gb_blk, w, tm, tk):
    T, C = h.shape
    N = w.shape[1]

    def a_fn(a_refs, out_refs, i, j, k):
        ge, _ = _gelu(a_refs[1][...].astype(F32))
        return (a_refs[0][...].astype(F32) * ge).astype(BF16)

    def epi(acc, e_refs, out_refs, i, j, extra):
        out_refs[0][...] = acc.astype(BF16)

    return _mm("fwd_yb", "nn", (T // tm, 1, C // tk),
               [(h, _bs((tm, tk), lambda i, j, k: (i, k))), (z, _bs((tm, tk), lambda i, j, k: (i, gb_blk + k)))], a_fn,
               (w, _bs((tk, N), lambda i, j, k: (k, 0))), [], epi,
               [(_sds((T, N), BF16), _bs((tm, N), lambda i, j, k: (i, 0)))], (tm, N))[0]


def _fwd_x1(x, ya, yb, z, sa_blk, w, tm):
    T, D = x.shape

    def a_fn(a_refs, out_refs, i, j, k):
        ya_, yb_, sa_, sb_ = (r[...].astype(F32) for r in a_refs)
        return (_sig(sa_) * ya_ + _sig(sb_) * yb_).astype(BF16)

    def epi(acc, e_refs, out_refs, i, j, extra):
        out_refs[0][...] = e_refs[0][...] + acc

    t = _bs((tm, D), lambda i, j, k: (i, 0))
    return _mm("fwd_x1", "nn", (T // tm, 1, 1),
               [(ya, t), (yb, t), (z, _bs((tm, D), lambda i, j, k: (i, sa_blk))),
                (z, _bs((tm, D), lambda i, j, k: (i, sa_blk + 1)))], a_fn,
               (w, _bs((D, D), lambda i, j, k: (0, 0))), [(x, t)], epi,
               [(_sds((T, D), F32), t)], None)[0]


def _fwd_x2(x1, fp, w, tm, tk, comm=None):
    T, D = x1.shape
    Fd = fp.shape[1]

    def a_fn(a_refs, out_refs, i, j, k):
        f = jnp.maximum(a_refs[0][...].astype(F32), 0.0)
        return (f * f).astype(BF16)

    def epi(acc, e_refs, out_refs, i, j, extra):
        out_refs[0][...] = e_refs[0][...] + acc

    t = _bs((tm, D), lambda i, j, k: (i, 0))
    r = _mm("fwd_x2", "nn", (T // tm, 1, Fd // tk),
            [(fp, _bs((tm, tk), lambda i, j, k: (i, k)))], a_fn,
            (w, _bs((tk, D), lambda i, j, k: (k, 0))), [(x1, t)], epi,
            [(_sds((T, D), F32), t)], (tm, D), comm=comm)
    return (r[0][0], r[1]) if comm else r[0]


def _loss_head(x, g, target, tm):
    T, D = x.shape

    def body(x_ref, g_ref, t_ref, loss_ref, dx_ref, dxb_ref, dg_ref):
        i = pl.program_id(0)
        xf, gv = x_ref[...], g_ref[...]
        y, r = _rms(xf, gv)
        err = y - t_ref[...]
        part = 0.5 * jnp.sum(jnp.mean(err * err, axis=-1, keepdims=True), axis=0, keepdims=True)
        dx, dg_rows = _rms_bwd(xf, gv, r, err * (1.0 / D))
        dx_ref[...] = dx
        dxb_ref[...] = dx.astype(BF16)
        _acc_rows(loss_ref, jnp.broadcast_to(part, (1, 128)), i == 0)
        _acc_rows(dg_ref, jnp.sum(dg_rows, axis=0, keepdims=True), i == 0)

    t = _bs((tm, D), lambda i: (i, 0))
    row = _bs((1, D), lambda i: (0, 0))
    return pl.pallas_call(
        body, name="loss_head", grid=(T // tm,), in_specs=[t, row, t],
        out_specs=[_bs((1, 128), lambda i: (0, 0)), t, t, row],
        out_shape=[_sds((1, 128), F32), _sds((T, D), F32), _sds((T, D), BF16), _sds((1, D), F32)],
        compiler_params=_cp(dimension_semantics=("arbitrary",)),
    )(x, g, target)


def _adamw(name, w, g, m, v, tr):
    rows, cols = w.shape
    d1 = 1.0 - ADAM_B1 ** ADAM_STEP
    d2 = 1.0 - ADAM_B2 ** ADAM_STEP

    def body(w_ref, g_ref, m_ref, v_ref, d_ref, mo_ref, vo_ref):
        gv = g_ref[...]
        mn = ADAM_B1 * m_ref[...] + (1.0 - ADAM_B1) * gv
        vn = ADAM_B2 * v_ref[...] + (1.0 - ADAM_B2) * (gv * gv)
        d_ref[...] = -ADAM_LR * ((mn / d1) / (jnp.sqrt(vn / d2) + ADAM_EPS) + ADAM_WD * w_ref[...])
        mo_ref[...] = mn
        vo_ref[...] = vn

    t = _bs((tr, cols), lambda i: (i, 0))
    return pl.pallas_call(
        body, name=name, grid=(rows // tr,), in_specs=[t] * 4, out_specs=[t] * 3,
        out_shape=[_sds((rows, cols), F32)] * 3,
        compiler_params=_cp(dimension_semantics=("arbitrary",)),
    )(w, g, m, v)


def _ident(a_refs, out_refs, i, j, k):
    return a_refs[0][...]


def _bwd_dw(name, act, dy, ti, tj, tm, a_fn=None, a_extra=(), shard_cols=None):
    T, J = dy.shape
    I = act.shape[1]

    def epi(acc, e_refs, out_refs, i, j, extra):
        out_refs[0][...] = acc.astype(BF16)

    if shard_cols is None:
        out = (_sds((I, J), BF16), _bs((ti, tj), lambda i, j, k: (i, j)))
    else:
        per = shard_cols // tj
        assert ti == I // 2 and per * tj == shard_cols
        out = (_sds((J // shard_cols, 2, ti, shard_cols), BF16),
               _bs((None, None, ti, tj), lambda i, j, k: (lax.div(j, per), i, 0, lax.rem(j, per))))
    a_ins = [(act, _bs((tm, ti), lambda i, j, k: (k, i)))] + list(a_extra)
    return _mm(name, "tn", (I // ti, J // tj, T // tm), a_ins, a_fn or _ident,
               (dy, _bs((tm, tj), lambda i, j, k: (k, j))), [], epi, [out], (ti, tj))[0]


def _bwd_df(dxb, w2, fp, tm, tn):
    T, D = dxb.shape
    Fd = w2.shape[0]

    def epi(acc, e_refs, out_refs, i, j, extra):
        out_refs[0][...] = (acc * (2.0 * jnp.maximum(e_refs[0][...].astype(F32), 0.0))).astype(BF16)

    t = _bs((tm, tn), lambda i, j, k: (i, j))
    return _mm("bwd_df", "nt", (T // tm, Fd // tn, 1), [(dxb, _bs((tm, D), lambda i, j, k: (i, 0)))], _ident,
               (w2, _bs((tn, D), lambda i, j, k: (j, 0))), [(fp, t)], epi, [(_sds((T, Fd), BF16), t)], None)[0]


def _bwd_norm(name, dy, w, x, g, dres, tm, tk, colsum=False, comm=None):
    T, K = dy.shape
    D = w.shape[0]
    nk = K // tk

    def a_fn(a_refs, out_refs, i, j, k):
        a = a_refs[0][...]
        if colsum:
            s = jnp.sum(a.astype(F32), axis=0, keepdims=True)

            @pl.when(i == 0)
            def _():
                out_refs[3][k] = s

            @pl.when(i > 0)
            def _():
                out_refs[3][k] += s
        return a

    def epi(acc, e_refs, out_refs, i, j, extra):
        xf, gv = e_refs[0][...], e_refs[1][...]
        r = lax.rsqrt(jnp.mean(xf * xf, axis=-1, keepdims=True) + EPS)
        dx, dg_rows = _rms_bwd(xf, gv, r, acc)
        dx = dx + e_refs[2][...]
        out_refs[0][...] = dx
        out_refs[1][...] = dx.astype(BF16)
        _acc_rows(out_refs[2], jnp.sum(dg_rows, axis=0, keepdims=True), i == 0)

    t = _bs((tm, D), lambda i, j, k: (i, 0))
    row = _bs((1, D), lambda i, j, k: (0, 0))
    outs = [(_sds((T, D), F32), t), (_sds((T, D), BF16), t), (_sds((1, D), F32), row)]
    if colsum:
        outs.append((_sds((nk, 1, tk), F32), _bs((nk, 1, tk), lambda i, j, k: (0, 0, 0))))
    return _mm(name, "nt", (T // tm, 1, nk), [(dy, _bs((tm, tk), lambda i, j, k: (i, k)))], a_fn,
               (w, _bs((D, tk), lambda i, j, k: (0, k))), [(x, t), (g, row), (dres, t)], epi, outs, (tm, D), comm=comm)


def _bwd_dm(dxb, w_o, ya, yb, z, sa_blk, tm):
    T, D = dxb.shape

    def epi(acc, e_refs, out_refs, i, j, extra):
        ya_, yb_, sa_, sb_ = (r[...].astype(F32) for r in e_refs)
        ga, gb = _sig(sa_), _sig(sb_)
        out_refs[0][...] = (acc * ga).astype(BF16)
        out_refs[1][...] = (acc * gb).astype(BF16)
        stage = extra[0]
        stage[:, 0:D] = (acc * ya_ * ga * (1.0 - ga)).astype(BF16)
        stage[:, D:2 * D] = (acc * yb_ * gb * (1.0 - gb)).astype(BF16)
        pltpu.sync_copy(stage, out_refs[2].at[pl.ds(pl.multiple_of(i * tm, tm), tm), pl.ds(sa_blk * D, 2 * D)])

    t = _bs((tm, D), lambda i, j, k: (i, 0))
    return _mm("bwd_dm", "nt", (T // tm, 1, 1), [(dxb, t)], _ident, (w_o, _bs((D, D), lambda i, j, k: (0, 0))),
               [(ya, t), (yb, t), (z, _bs((tm, D), lambda i, j, k: (i, sa_blk))),
                (z, _bs((tm, D), lambda i, j, k: (i, sa_blk + 1)))], epi,
               [(_sds((T, D), BF16), t), (_sds((T, D), BF16), t),
                (_sds(z.shape, BF16), pl.BlockSpec(memory_space=pl.ANY))], None,
               extra_scratch=[pltpu.VMEM((tm, 2 * D), BF16)])


def _bwd_du3(dya, w, u1, ln_g, ln_b, tm):
    T, D = dya.shape
    C = w.shape[0]

    def epi(acc, e_refs, out_refs, i, j, extra):
        gv = e_refs[1][...]
        _, uh, rstd, u2, s = _ln_silu(e_refs[0][...].astype(F32), gv, e_refs[2][...])
        du2 = acc * (s * (1.0 + u2 * (1.0 - s)))
        duh = du2 * gv
        out_refs[0][...] = rstd * (duh - jnp.mean(duh, axis=-1, keepdims=True)
                                   - uh * jnp.mean(duh * uh, axis=-1, keepdims=True))
        _acc_rows(out_refs[1], jnp.sum(du2 * uh, axis=0, keepdims=True), i == 0)
        _acc_rows(out_refs[2], jnp.sum(du2, axis=0, keepdims=True), i == 0)

    t = _bs((tm, C), lambda i, j, k: (i, 0))
    row = _bs((1, C), lambda i, j, k: (0, 0))
    return _mm("bwd_du3", "nt", (T // tm, 1, 1), [(dya, _bs((tm, D), lambda i, j, k: (i, 0)))], _ident,
               (w, _bs((C, D), lambda i, j, k: (0, 0))), [(u1, t), (ln_g, row), (ln_b, row)], epi,
               [(_sds((T, C), F32), t), (_sds((1, C), F32), row), (_sds((1, C), F32), row)], None)


def _bwd_dp(dyb, w, h, z, dz, gb_blk, tm, tn):
    T, D = dyb.shape
    R = w.shape[0]

    def epi(acc, e_refs, out_refs, i, j, extra):
        gbv = e_refs[1][...].astype(F32)
        ge, th = _gelu(gbv)
        out_refs[0][...] = acc * ge
        out_refs[1][...] = (acc * e_refs[0][...].astype(F32) * _gelu_grad(gbv, th)).astype(BF16)

    t = _bs((tm, tn), lambda i, j, k: (i, j))
    tz = _bs((tm, tn), lambda i, j, k: (i, gb_blk + j))
    return _mm("bwd_dp", "nt", (T // tm, R // tn, 1), [(dyb, _bs((tm, D), lambda i, j, k: (i, 0)))], _ident,
               (w, _bs((tn, D), lambda i, j, k: (j, 0))), [(h, t), (z, tz)], epi,
               [(_sds((T, R), F32), t), (_sds(dz.shape, BF16), tz)], None, cache_a=None, alias=[(dz, 1)])


CONV_ROWS = 32


def _shifted_taps(x, halo, shifts, fn):
    n = CONV_ROWS + halo
    by_r = {}
    for k, s in shifts:
        by_r.setdefault(s % 8, []).append((k, s))
    for r in sorted(by_r):
        xr = x if r == 0 else pltpu.roll(x, n - r, 0)
        for k, s in by_r[r]:
            q = s - r
            fn(k, xr[q:q + CONV_ROWS])


def _conv_fwd(name, z, blk0, gate_blk0, w_pad, bias, taps, seq, tc, out_dtype):
    T = z.shape[0]
    C = w_pad.shape[1]
    nb, nj = T // seq, C // tc
    pad = 8 * ((taps - 1 + 7) // 8)
    halo = pad
    shifts = [(k, pad - (taps - 1) + k) for k in range(taps)]
    glu = gate_blk0 is not None

    def body(*refs):
        if glu:
            v_ref, g_ref, w_ref, b_ref, o_ref, p_ref = refs
        else:
            v_ref, w_ref, b_ref, o_ref, p_ref = refs
        p_ref[pl.ds(0, pad), :] = jnp.zeros((pad, tc), F32)
        u = v_ref[...].astype(F32)
        if glu:
            u = u * _sig(g_ref[...].astype(F32))
        p_ref[pl.ds(pad, seq), :] = u

        def step(c, _):
            base = pl.multiple_of(c * CONV_ROWS, CONV_ROWS)
            x = p_ref[pl.ds(base, CONV_ROWS + halo), :]
            acc = [jnp.zeros((CONV_ROWS, tc), F32) + b_ref[...]]

            def tap(k, xs):
                acc[0] = acc[0] + w_ref[k:k + 1, :] * xs

            _shifted_taps(x, halo, shifts, tap)
            o_ref[pl.ds(base, CONV_ROWS), :] = acc[0].astype(out_dtype)
            return 0

        lax.fori_loop(0, seq // CONV_ROWS, step, 0)

    zin = [(z, _bs((seq, tc), lambda b, j: (b, blk0 + j)))]
    if glu:
        zin.append((z, _bs((seq, tc), lambda b, j: (b, gate_blk0 + j))))
    ins = zin + [(w_pad, _bs((w_pad.shape[0], tc), lambda b, j: (0, j))), (bias, _bs((1, tc), lambda b, j: (0, j)))]
    return pl.pallas_call(
        body, name=name, grid=(nb, nj), in_specs=[s for _, s in ins],
        out_specs=_bs((seq, tc), lambda b, j: (b, j)), out_shape=_sds((T, C), out_dtype),
        scratch_shapes=[pltpu.VMEM((seq + pad, tc), F32)],
        compiler_params=_cp(dimension_semantics=("arbitrary", "arbitrary")),
    )(*[a for a, _ in ins])


def _conv_bwd(name, dy, z, dz, blk0, gate_blk0, w_pad, taps, seq, tc):
    T = z.shape[0]
    C = w_pad.shape[1]
    nb, nj = T // seq, C // tc
    kp = w_pad.shape[0]
    pad = 8 * ((taps - 1 + 7) // 8)
    halo = pad
    sh_du = [(k, taps - 1 - k) for k in range(taps)]
    sh_dw = [(k, pad - (taps - 1) + k) for k in range(taps)]
    glu = gate_blk0 is not None

    def body(*refs):
        if glu:
            dy_ref, v_ref, g_ref, w_ref, _dz_in, dz_out, dw_ref, db_ref, pdy, pu, du_s, wacc, ob, ob2 = refs
        else:
            dy_ref, v_ref, w_ref, _dz_in, dz_out, dw_ref, db_ref, pdy, pu, du_s, wacc, ob = refs
        j = pl.program_id(0)
        b = pl.program_id(1)
        pdy[pl.ds(seq, pad), :] = jnp.zeros((pad, tc), F32)
        pdy[pl.ds(0, seq), :] = dy_ref[...].astype(F32)
        pu[pl.ds(0, pad), :] = jnp.zeros((pad, tc), F32)
        v = v_ref[...].astype(F32)
        if glu:
            sg = _sig(g_ref[...].astype(F32))
            pu[pl.ds(pad, seq), :] = v * sg
        else:
            pu[pl.ds(pad, seq), :] = v
        wacc[...] = jnp.zeros(wacc.shape, F32)

        def step(c, dbacc):
            base = pl.multiple_of(c * CONV_ROWS, CONV_ROWS)
            xdy = pdy[pl.ds(base, CONV_ROWS + halo), :]
            acc = [jnp.zeros((CONV_ROWS, tc), F32)]

            def tap(k, xs):
                acc[0] = acc[0] + w_ref[k:k + 1, :] * xs

            _shifted_taps(xdy, halo, sh_du, tap)
            du_s[pl.ds(base, CONV_ROWS), :] = acc[0]
            dyc = xdy[0:CONV_ROWS]
            xu = pu[pl.ds(base, CONV_ROWS + halo), :]

            def wtap(k, xs):
                p = dyc * xs
                s8 = p[0:8]
                for m in range(1, CONV_ROWS // 8):
                    s8 = s8 + p[8 * m:8 * m + 8]
                wacc[pl.ds(8 * k, 8), :] += s8

            _shifted_taps(xu, halo, sh_dw, wtap)
            d8 = dyc[0:8]
            for m in range(1, CONV_ROWS // 8):
                d8 = d8 + dyc[8 * m:8 * m + 8]
            return dbacc + d8

        dbacc = lax.fori_loop(0, seq // CONV_ROWS, step, jnp.zeros((8, tc), F32))
        du = du_s[...]
        rows = pl.ds(pl.multiple_of(b * seq, seq), seq)
        if glu:
            ob[...] = (du * sg).astype(BF16)
            ob2[...] = (du * v * sg * (1.0 - sg)).astype(BF16)
            pltpu.sync_copy(ob2, dz_out.at[rows, pl.ds(pl.multiple_of((gate_blk0 + j) * tc, tc), tc)])
        else:
            ob[...] = du.astype(BF16)
        pltpu.sync_copy(ob, dz_out.at[rows, pl.ds(pl.multiple_of((blk0 + j) * tc, tc), tc)])
        dw = jnp.sum(wacc[...].reshape(kp, 8, tc), axis=1)
        _acc_rows(dw_ref, dw, b == 0)
        _acc_rows(db_ref, jnp.sum(dbacc, axis=0, keepdims=True), b == 0)

    zin = [(z, _bs((seq, tc), lambda j, b: (b, blk0 + j)))]
    if glu:
        zin.append((z, _bs((seq, tc), lambda j, b: (b, gate_blk0 + j))))
    ins = [(dy, _bs((seq, tc), lambda j, b: (b, j)))] + zin + [(w_pad, _bs((kp, tc), lambda j, b: (0, j))),
                                                               (dz, pl.BlockSpec(memory_space=pl.ANY))]
    dz_idx = len(ins) - 1
    out_specs = [pl.BlockSpec(memory_space=pl.ANY), _bs((kp, tc), lambda j, b: (0, j)), _bs((1, tc), lambda j, b: (0, j))]
    out_shape = [_sds(dz.shape, dz.dtype), _sds((kp, C), F32), _sds((1, C), F32)]
    stage = [pltpu.VMEM((seq, tc), BF16)] * (2 if glu else 1)
    return pl.pallas_call(
        body, name=name, grid=(nj, nb), in_specs=[s for _, s in ins], out_specs=out_specs, out_shape=out_shape,
        input_output_aliases={dz_idx: 0},
        scratch_shapes=[pltpu.VMEM((seq + pad, tc), F32), pltpu.VMEM((seq + pad, tc), F32),
                        pltpu.VMEM((seq, tc), F32), pltpu.VMEM((8 * kp, tc), F32)] + stage,
        compiler_params=_cp(dimension_semantics=("arbitrary", "arbitrary")),
    )(*[a for a, _ in ins])


RG_ROWS = 256


def _softplus_neg(lam):
    return jnp.maximum(-lam, 0.0) + jnp.log(1.0 + jnp.exp(-jnp.abs(lam)))


def _gates(v0c, wa_ref, wx_ref, ba, bx, sp, first_row):
    vb = v0c.astype(BF16)
    r = _sig(jnp.dot(vb, wa_ref[...], preferred_element_type=F32) + ba)
    i = _sig(jnp.dot(vb, wx_ref[...], preferred_element_type=F32) + bx)
    la = -LRU_C * r * sp
    a = jnp.exp(la)
    a2 = a * a
    x = 2.0 * la
    series = -x * (1.0 + x * (1.0 / 2) * (1.0 + x * (1.0 / 3) * (1.0 + x * (1.0 / 4) * (1.0 + x * (1.0 / 5)))))
    mult = jnp.sqrt(jnp.where(x > -0.1, series, 1.0 - a2))
    dmult = jnp.where(first_row, 0.0, -a2 / mult)
    mult = jnp.where(first_row, 1.0, mult)
    return r, i, a, mult, dmult


def _group_scan(a, b, reverse):
    n = a.shape[0]
    row = lax.broadcasted_iota(jnp.int32, a.shape, 0) & 7
    for d in (1, 2, 4):
        sh = n - d if reverse else d
        a_s, b_s = pltpu.roll(a, sh, 0), pltpu.roll(b, sh, 0)
        m = (row < 8 - d) if reverse else (row >= d)
        b = jnp.where(m, a * b_s + b, b)
        a = jnp.where(m, a * a_s, a)
    return a, b


def _group_carry(a_s, b_s, o_s, n_groups, reverse):
    cols = a_s.shape[1]

    def step(g, carry):
        g = n_groups - 1 - g if reverse else g
        rows = pl.ds(pl.multiple_of(g * 8, 8), 8)
        o = a_s[rows, :] * carry + b_s[rows, :]
        o_s[rows, :] = o
        return o[0:1, :] if reverse else o[7:8, :]

    lax.fori_loop(0, n_groups, step, jnp.zeros((1, cols), F32))


def _rglru_fwd(v0, wa, wx, ba, bx, lam, seq, comm=None):
    T, C = v0.shape
    ng, G = wa.shape[0], wa.shape[1]
    nb = T // seq

    def body(v_ref, wa_ref, wx_ref, ba_ref, bx_ref, lam_ref, h_ref, a_s, b_s, h_s):
        sp = _softplus_neg(lam_ref[...])

        def chunk(c, _):
            rows = pl.ds(pl.multiple_of(c * RG_ROWS, RG_ROWS), RG_ROWS)
            t = lax.broadcasted_iota(jnp.int32, (RG_ROWS, G), 0) + c * RG_ROWS
            v0c = v_ref[rows, :]
            _, i, a, mult, _ = _gates(v0c, wa_ref, wx_ref, ba_ref[...], bx_ref[...], sp, t == 0)
            a_g, b_g = _group_scan(a, mult * i * v0c, False)
            a_s[rows, :] = a_g
            b_s[rows, :] = b_g
            return 0

        lax.fori_loop(0, seq // RG_ROWS, chunk, 0)
        _group_carry(a_s, b_s, h_s, seq // 8, False)
        h_ref[...] = h_s[...].astype(BF16)

    t2 = _bs((seq, G), lambda b, g: (b, g))
    wsp = _bs((None, G, G), lambda b, g: (g, 0, 0))
    row = _bs((1, G), lambda b, g: (0, g))
    return _call_with_comm("rglru_fwd", body, (nb, ng), [v0, wa, wx, ba, bx, lam], [t2, wsp, wsp, row, row, row],
                           [t2], [_sds((T, C), BF16)], [pltpu.VMEM((seq, G), F32)] * 3, comm)


def _call_with_comm(name, body, grid, ins, in_specs, out_specs, out_shape, scratch, comm):
    n_in, n_out, n_s = len(ins), len(out_shape), len(scratch)
    c_ins, c_outs, c_sems = (comm.ins, comm.outs, comm.sems) if comm else ([], [], [])

    def wrapped(*refs):
        o0 = n_in + len(c_ins)
        s0 = o0 + n_out + len(c_outs)
        cin, cout, csem = refs[n_in:o0], refs[o0 + n_out:s0], refs[s0 + n_s:]
        ids = [pl.program_id(a) for a in range(len(grid))]
        if comm:
            first = _all_of([i == 0 for i in ids])

            @pl.when(first)
            def _():
                comm.start(cin, cout, csem)

        body(*refs[:n_in], *refs[o0:o0 + n_out], *refs[s0:s0 + n_s])
        if comm:
            last = _all_of([i == n - 1 for i, n in zip(ids, grid)])

            @pl.when(last)
            def _():
                comm.finish(cin, cout, csem)

    res = pl.pallas_call(
        wrapped, name=name, grid=grid, in_specs=list(in_specs) + [ANY] * len(c_ins),
        out_specs=list(out_specs) + [ANY] * len(c_outs), out_shape=list(out_shape) + list(c_outs),
        scratch_shapes=list(scratch) + list(c_sems),
        compiler_params=_cp(dimension_semantics=("arbitrary",) * len(grid), has_side_effects=bool(comm)),
    )(*ins, *c_ins)
    return (list(res[:n_out]), list(res[n_out:])) if comm else list(res)


def _all_of(conds):
    out = conds[0]
    for c in conds[1:]:
        out = out & c
    return out


def _rglru_bwd(v0, h, dh, wa, wx, ba, bx, lam, seq, comm=None):
    T, C = v0.shape
    ng, G = wa.shape[0], wa.shape[1]
    nb = T // seq
    R = RG_ROWS

    def body(v_ref, h_ref, dh_ref, wa_ref, wx_ref, ba_ref, bx_ref, lam_ref,
             dv_ref, dwa_ref, dwx_ref, dba_ref, dbx_ref, dlam_ref, a_s, b_s, q_s, hp_s):
        b = pl.program_id(1)
        lam_v = lam_ref[...]
        sp = _softplus_neg(lam_v)
        dsp_dlam = -_sig(-lam_v)

        @pl.when(b == 0)
        def _():
            dwa_ref[...] = jnp.zeros((G, G), F32)
            dwx_ref[...] = jnp.zeros((G, G), F32)
            dba_ref[...] = jnp.zeros((1, G), F32)
            dbx_ref[...] = jnp.zeros((1, G), F32)
            dlam_ref[...] = jnp.zeros((1, G), F32)

        hp_s[pl.ds(0, 8), :] = jnp.zeros((8, G), F32)
        hp_s[pl.ds(8, seq), :] = h_ref[...].astype(F32)
        q_s[pl.ds(seq, 8), :] = jnp.zeros((8, G), F32)

        def chunk1(c, _):
            rows = pl.ds(pl.multiple_of(c * R, R), R)
            t = lax.broadcasted_iota(jnp.int32, (R, G), 0) + c * R
            _, _, a, _, _ = _gates(v_ref[rows, :], wa_ref, wx_ref, ba_ref[...], bx_ref[...], sp, t == 0)
            a_g, b_g = _group_scan(a, a * dh_ref[rows, :].astype(F32), True)
            a_s[rows, :] = a_g
            b_s[rows, :] = b_g
            return 0

        lax.fori_loop(0, seq // R, chunk1, 0)
        _group_carry(a_s, b_s, q_s, seq // 8, True)

        def chunk3(c, _):
            base = pl.multiple_of(c * R, R)
            rows = pl.ds(base, R)
            t = lax.broadcasted_iota(jnp.int32, (R, G), 0) + c * R
            v0c = v_ref[rows, :]
            r, i, a, mult, dmult_dla = _gates(v0c, wa_ref, wx_ref, ba_ref[...], bx_ref[...], sp, t == 0)
            q_next = pltpu.roll(q_s[pl.ds(base, R + 8), :], R + 7, 0)[0:R]
            h_prev = pltpu.roll(hp_s[pl.ds(base, R + 8), :], R + 1, 0)[0:R]
            gt = dh_ref[rows, :].astype(F32) + q_next
            dla = gt * h_prev * a + gt * i * v0c * dmult_dla
            dpa = dla * (-LRU_C * sp) * r * (1.0 - r)
            dpx = gt * mult * v0c * i * (1.0 - i)
            dpa_b, dpx_b, v_b = dpa.astype(BF16), dpx.astype(BF16), v0c.astype(BF16)
            dv_ref[rows, :] = (gt * mult * i
                               + lax.dot_general(dpa_b, wa_ref[...], _DIMS["nt"], preferred_element_type=F32)
                               + lax.dot_general(dpx_b, wx_ref[...], _DIMS["nt"], preferred_element_type=F32))
            dwa_ref[...] += lax.dot_general(v_b, dpa_b, _DIMS["tn"], preferred_element_type=F32)
            dwx_ref[...] += lax.dot_general(v_b, dpx_b, _DIMS["tn"], preferred_element_type=F32)
            dba_ref[...] += jnp.sum(dpa, axis=0, keepdims=True)
            dbx_ref[...] += jnp.sum(dpx, axis=0, keepdims=True)
            dlam_ref[...] += jnp.sum(dla * (-LRU_C * r), axis=0, keepdims=True) * dsp_dlam
            return 0

        lax.fori_loop(0, seq // R, chunk3, 0)

    t2 = _bs((seq, G), lambda g, b: (b, g))
    wsp = _bs((None, G, G), lambda g, b: (g, 0, 0))
    row = _bs((1, G), lambda g, b: (0, g))
    return _call_with_comm(
        "rglru_bwd", body, (ng, nb), [v0, h, dh, wa, wx, ba, bx, lam], [t2, t2, t2, wsp, wsp, row, row, row],
        [t2, wsp, wsp, row, row, row],
        [_sds((T, C), F32), _sds((ng, G, G), F32), _sds((ng, G, G), F32),
         _sds((1, C), F32), _sds((1, C), F32), _sds((1, C), F32)],
        [pltpu.VMEM((seq, G), F32), pltpu.VMEM((seq, G), F32),
         pltpu.VMEM((seq + 8, G), F32), pltpu.VMEM((seq + 8, G), F32)], comm)


TC_A = 256
TC_B = 512
TAPS_A, TAPS_B = 31, 4


def _tiles(T):
    return min(512, T), min(1024, T)


GATHERED = ("w_in", "w_1", "w_a_out", "w_b_out", "w_o", "w_2", "caw", "cbw")
GATHER_KIND = {"w_in": (True, True), "w_1": (True, True), "w_a_out": (False, True), "w_b_out": (False, True),
               "w_o": (False, True), "w_2": (False, True), "caw": (True, False), "cbw": (True, False)}


def _layer_fwd(x, p, seq, nxt=None):
    T, D = x.shape
    C, R = p["ln_g"].shape[1], p["lam"].shape[1]
    tm, tl = _tiles(T)
    gb_blk, sa_blk = (2 * C + R) // TC_B, (2 * C + 2 * R) // D
    got = {}

    def gather(names):
        return None if nxt is None else _gather_comm([nxt[n] for n in names], [GATHER_KIND[n] for n in names])

    def outs(r, names):
        if nxt is None:
            return r
        got.update(zip(names, r[1]))
        return r[0]

    z, h = outs(_fwd_norm_mm("fwd_z", x, p["g_mix"], p["w_in"], p["b_in"], tl, 1024, comm=gather(["w_in"])), ["w_in"])
    u1 = _conv_fwd("conv_a_fwd", z, 0, C // TC_A, p["caw"], p["cab"], TAPS_A, seq, TC_A, BF16)
    ya = _fwd_ya(u1, p["ln_g"], p["ln_b"], p["w_a_out"], tm)
    v0 = _conv_fwd("conv_b_fwd", z, 2 * C // TC_B, None, p["cbw"], p["cbb"], TAPS_B, seq, TC_B, F32)
    mid = ["w_a_out", "w_b_out", "w_o", "caw", "cbw"]
    hr, = outs(_rglru_fwd(v0, p["wa"], p["wx"], p["b_rg_a"], p["b_rg_x"], p["lam"], seq, comm=gather(mid)), mid)
    yb = _fwd_yb(hr, z, gb_blk, p["w_b_out"], tl, TC_B)
    x1 = _fwd_x1(x, ya, yb, z, sa_blk, p["w_o"], tm)
    fp, h2 = outs(_fwd_norm_mm("fwd_f", x1, p["g_mlp"], p["w_1"], None, tl, 1024, comm=gather(["w_1"])), ["w_1"])
    x2 = _fwd_x2(x1, fp, p["w_2"], tm, 2048, comm=gather(["w_2"]))
    if nxt is not None:
        x2, (got["w_2"],) = x2
    saved = dict(x=x, z=z, h=h, u1=u1, ya=ya, v0=v0, hr=hr, yb=yb, x1=x1, fp=fp, h2=h2)
    return (x2, saved) if nxt is None else (x2, saved, got)


class _Reduce:
    def __init__(self, accs, c_arr, kcl_of):
        self.accs, self.c_arr, self.kcl_of, self.pending = accs, c_arr, kcl_of, None

    def chip_sums(self, partials):
        pgs = [a if a.ndim == 4 else a.reshape(N_CHIPS, 2, a.shape[0] // (2 * N_CHIPS), a.shape[1]) for a in partials]
        return [_sum_siblings(a, b, self.c_arr) for a, b in zip(pgs, _swap_halves(pgs))]

    def finish(self, names, sums, received, layer):
        for n, a, b in zip(names, sums, received):
            self.accs[n] = _sum_chips(a, b, self.kcl_of(layer), self.accs[n])


def _layer_bwd(dx2, dx2b, p, s, seq, red=None, layer=0):
    T, D = dx2.shape
    C, R = p["ln_g"].shape[1], p["lam"].shape[1]
    tm, tl = _tiles(T)
    gb_blk, sa_blk = (2 * C + R) // TC_B, (2 * C + 2 * R) // D
    z = s["z"]
    g = {}

    def relu2(a_refs, out_refs, i, j, k):
        f = jnp.maximum(a_refs[0][...].astype(F32), 0.0)
        return (f * f).astype(BF16)

    dfp = _bwd_df(dx2b, p["w_2"], s["fp"], tl, 1024)
    tw = min(2048, T)
    g["w_2"] = _bwd_dw("bwd_dw2", s["fp"], dx2b, 1024, D, tw, a_fn=relu2)
    if red is not None and red.pending is not None:
        above, red.pending = red.pending, None
        (dx1, dx1b, g["g_mlp"]), got = _bwd_norm("bwd_dh2", dfp, p["w_1"], s["x1"], p["g_mlp"], dx2, tm, 2048,
                                                 comm=_scatter_comm(above))
        red.finish(["w_in"], above, got, layer + 1)
    else:
        dx1, dx1b, g["g_mlp"] = _bwd_norm("bwd_dh2", dfp, p["w_1"], s["x1"], p["g_mlp"], dx2, tm, 2048)
    g["w_1"] = _bwd_dw("bwd_dw1", s["h2"], dfp, D // 2, 1024, tw, shard_cols=dfp.shape[1] // N_CHIPS)

    dya, dyb, dz = _bwd_dm(dx1b, p["w_o"], s["ya"], s["yb"], z, sa_blk, tm)

    def merged(a_refs, out_refs, i, j, k):
        ya_, yb_, sa_, sb_ = (r[...].astype(F32) for r in a_refs)
        return (_sig(sa_) * ya_ + _sig(sb_) * yb_).astype(BF16)

    tk = _bs((tm, D), lambda i, j, k: (k, 0))
    g["w_o"] = _bwd_dw("bwd_dwo", s["ya"], dx1b, D, D, tm, a_fn=merged,
                       a_extra=[(s["yb"], tk), (z, _bs((tm, D), lambda i, j, k: (k, sa_blk))),
                                (z, _bs((tm, D), lambda i, j, k: (k, sa_blk + 1)))])

    du1, g["ln_g"], g["ln_b"] = _bwd_du3(dya, p["w_a_out"], s["u1"], p["ln_g"], p["ln_b"], tm)

    def act_a(a_refs, out_refs, i, j, k):
        return _ln_silu(a_refs[0][...].astype(F32), a_refs[1][...], a_refs[2][...])[0].astype(BF16)

    rowc = _bs((1, C), lambda i, j, k: (0, 0))
    g["w_a_out"] = _bwd_dw("bwd_dwa", s["u1"], dya, C, D, tm, a_fn=act_a,
                           a_extra=[(p["ln_g"], rowc), (p["ln_b"], rowc)])
    dz, g["caw"], g["cab"] = _conv_bwd("conv_a_bwd", du1, z, dz, 0, C // TC_A, p["caw"], TAPS_A, seq, TC_A)

    dhr, dz = _bwd_dp(dyb, p["w_b_out"], s["hr"], z, dz, gb_blk, tl, TC_B)

    def act_b(a_refs, out_refs, i, j, k):
        ge, _ = _gelu(a_refs[1][...].astype(F32))
        return (a_refs[0][...].astype(F32) * ge).astype(BF16)

    tb = min(1024, T)
    g["w_b_out"] = _bwd_dw("bwd_dwb", s["hr"], dyb, TC_B, D, tb, a_fn=act_b,
                           a_extra=[(z, _bs((tb, TC_B), lambda i, j, k: (k, gb_blk + i)))])
    rg_args = (s["v0"], s["hr"], dhr, p["wa"], p["wx"], p["b_rg_a"], p["b_rg_x"], p["lam"], seq)
    if red is not None:
        five = ["w_2", "w_1", "w_o", "w_a_out", "w_b_out"]
        sums = red.chip_sums([g.pop(n) for n in five])
        rg_out, got = _rglru_bwd(*rg_args, comm=_scatter_comm(sums))
        red.finish(five, sums, got, layer)
    else:
        rg_out = _rglru_bwd(*rg_args)
    dv0, g["wa"], g["wx"], g["b_rg_a"], g["b_rg_x"], g["lam"] = rg_out
    dz, g["cbw"], g["cbb"] = _conv_bwd("conv_b_bwd", dv0, z, dz, 2 * C // TC_B, None, p["cbw"], TAPS_B, seq, TC_B)

    dx, dxb, g["g_mix"], dbin = _bwd_norm("bwd_dh", dz, p["w_in"], s["x"], p["g_mix"], dx1, tm, dz.shape[1] // 2,
                                          colsum=True)
    g["b_in"] = dbin.reshape(1, -1)
    ns = dz.shape[1] // N_CHIPS
    g["w_in"] = _bwd_dw("bwd_dwin", s["h"], dz, D // 2, ns // 2, tw, shard_cols=ns)
    if red is not None:
        red.pending = red.chip_sums([g.pop("w_in")])
    return dx, dxb, g


ANY = pl.BlockSpec(memory_space=pl.ANY)


def _mesh_pos():
    return lax.axis_index("x"), lax.axis_index("y"), lax.axis_index("c")


def _other_chips(x, y):
    return [(1 - x, y), (x, 1 - y), (1 - x, 1 - y)]


def _remote(src, dst, ssem, rsem, dev):
    return pltpu.make_async_remote_copy(src_ref=src, dst_ref=dst, send_sem=ssem, recv_sem=rsem,
                                        device_id=dev, device_id_type=MESH)


def _gather_region(src, dst, by_cols, k, half):
    rows, cols = src.shape
    nr = rows if half is None else rows // 2
    r0 = 0 if half is None else half * nr
    if by_cols:
        return dst.at[pl.ds(r0, nr), pl.ds(pl.multiple_of(k * cols, 128), cols)]
    return dst.at[pl.ds(pl.multiple_of(k * rows + r0, 8), nr), :]


def _gather_sends(src, dst, kinds, send, recv):
    x, y, c = _mesh_pos()
    cps = []
    for t in range(len(src)):
        half = c if kinds[t][1] else None
        hr = src[t].shape[0] // 2
        s_ref = src[t].at[pl.ds(c * hr, hr), :] if kinds[t][1] else src[t]
        for j, chip in enumerate(_other_chips(x, y)):
            cps.append(_remote(s_ref, _gather_region(src[t], dst[t], kinds[t][0], 2 * x + y, half),
                               send.at[t, j], recv.at[t, j], (*chip, c)))
    return cps


def _gather_finish(src, dst, kinds, send, recv, fsend, frecv):
    x, y, c = _mesh_pos()
    chips = _other_chips(x, y)
    sib = (x, y, 1 - c)
    n = len(src)
    fwd = []
    for t in range(n):
        half = c if kinds[t][1] else None
        for j, chip in enumerate(chips):
            got = _gather_region(src[t], dst[t], kinds[t][0], 2 * chip[0] + chip[1], half)
            _remote(got, got, send.at[t, j], recv.at[t, j], (*chip, c)).wait_recv()
            if kinds[t][1]:
                cp = _remote(got, got, fsend.at[t, j], frecv.at[t, j], sib)
                cp.start()
                fwd.append(cp)
    for t in range(n):
        if kinds[t][1]:
            for j, chip in enumerate(chips):
                got = _gather_region(src[t], dst[t], kinds[t][0], 2 * chip[0] + chip[1], 1 - c)
                _remote(got, got, fsend.at[t, j], frecv.at[t, j], sib).wait_recv()
    for cp in _gather_sends(src, dst, kinds, send, recv) + fwd:
        cp.wait_send()


def _gather_sems(n):
    sem = pltpu.SemaphoreType.DMA
    return [sem((n, 3)), sem((n, 3)), sem((n, 3)), sem((n, 3))]


def _gather_comm(shards, kinds):
    n = len(shards)

    def whole(s, by_cols):
        return (s.shape[0], N_CHIPS * s.shape[1]) if by_cols else (N_CHIPS * s.shape[0], s.shape[1])

    def own(src, dst, lsem):
        x, y, _ = _mesh_pos()
        return [pltpu.make_async_copy(src[t], _gather_region(src[t], dst[t], kinds[t][0], 2 * x + y, None), lsem.at[t])
                for t in range(n)]

    def start(src, dst, sems):
        for cp in own(src, dst, sems[4]) + _gather_sends(src, dst, kinds, sems[0], sems[1]):
            cp.start()

    def finish(src, dst, sems):
        _gather_finish(src, dst, kinds, *sems[:4])
        for cp in own(src, dst, sems[4]):
            cp.wait()

    return _Comm(shards, [_sds(whole(s, k[0]), s.dtype) for s, k in zip(shards, kinds)],
                 _gather_sems(n) + [pltpu.SemaphoreType.DMA((n,))], start, finish)


def _scatter_comm(ps):
    n = len(ps)

    def copies(src, dst, sems):
        x, y, c = _mesh_pos()
        return [_remote(src[t].at[2 * chip[0] + chip[1]], dst[t].at[j], sems[0].at[t, j], sems[1].at[t, j], (*chip, c))
                for t in range(n) for j, chip in enumerate(_other_chips(x, y))]

    def start(src, dst, sems):
        for cp in copies(src, dst, sems):
            cp.start()

    def finish(src, dst, sems):
        cps = copies(src, dst, sems)
        for cp in cps:
            cp.wait_recv()
        for cp in cps:
            cp.wait_send()

    sem = pltpu.SemaphoreType.DMA
    return _Comm(ps, [_sds((3,) + a.shape[1:], a.dtype) for a in ps], [sem((n, 3)), sem((n, 3))], start, finish)


def _comm_call(name, comm):
    n_i, n_o = len(comm.ins), len(comm.outs)

    def body(*refs):
        comm.start(refs[:n_i], refs[n_i:n_i + n_o], refs[n_i + n_o:])
        comm.finish(refs[:n_i], refs[n_i:n_i + n_o], refs[n_i + n_o:])

    return pl.pallas_call(
        body, name=name, in_specs=[ANY] * n_i, out_specs=[ANY] * n_o, out_shape=comm.outs, scratch_shapes=comm.sems,
        compiler_params=_cp(has_side_effects=True),
    )(*comm.ins)


def _swap_halves(pgs):
    n = len(pgs)

    def body(*refs):
        src, dst = refs[:n], refs[n:2 * n]
        send, recv = refs[2 * n:]
        x, y, c = _mesh_pos()
        cps = [_remote(src[t].at[:, 1 - c], dst[t], send.at[t], recv.at[t], (x, y, 1 - c)) for t in range(n)]
        for cp in cps:
            cp.start()
        for cp in cps:
            cp.wait_recv()
        for cp in cps:
            cp.wait_send()

    sem = pltpu.SemaphoreType.DMA
    return pl.pallas_call(
        body, name="swap_halves", in_specs=[ANY] * n, out_specs=[ANY] * n,
        out_shape=[_sds((a.shape[0],) + a.shape[2:], a.dtype) for a in pgs],
        scratch_shapes=[sem((n,)), sem((n,))], compiler_params=_cp(has_side_effects=True),
    )(*pgs)


def _join_halves(accs):
    n = len(accs)

    def body(*refs):
        buf = refs[n:2 * n]
        send, recv = refs[2 * n:]
        x, y, c = _mesh_pos()
        cps = [_remote(buf[t].at[:, c], buf[t].at[:, c], send.at[t], recv.at[t], (x, y, 1 - c)) for t in range(n)]
        for cp in cps:
            cp.start()
        for t in range(n):
            _remote(buf[t].at[:, c], buf[t].at[:, 1 - c], send.at[t], recv.at[t], (x, y, 1 - c)).wait_recv()
        for cp in cps:
            cp.wait_send()

    sem = pltpu.SemaphoreType.DMA
    return pl.pallas_call(
        body, name="join_halves", in_specs=[ANY] * n, out_specs=[ANY] * n,
        out_shape=[_sds(a.shape, a.dtype) for a in accs], scratch_shapes=[sem((n,)), sem((n,))],
        input_output_aliases={t: t for t in range(n)}, compiler_params=_cp(has_side_effects=True),
    )(*accs)


def _sum_siblings(pg, rb, c_arr):
    nk, _, hr, cols = pg.shape

    def body(c_ref, a_ref, b_ref, o_ref):
        o_ref[...] = (a_ref[...].astype(F32) + b_ref[...].astype(F32)).astype(BF16)

    return pl.pallas_call(
        body, name="sum_siblings",
        grid_spec=pltpu.PrefetchScalarGridSpec(
            num_scalar_prefetch=1, grid=(nk,),
            in_specs=[pl.BlockSpec((None, None, hr, cols), lambda k, c_ref: (k, c_ref[0], 0, 0)),
                      pl.BlockSpec((None, hr, cols), lambda k, c_ref: (k, 0, 0))],
            out_specs=pl.BlockSpec((None, hr, cols), lambda k, c_ref: (k, 0, 0))),
        out_shape=_sds((nk, hr, cols), BF16), compiler_params=_cp(dimension_semantics=("arbitrary",)),
    )(c_arr, pg, rb)


def _sum_chips(p, rb, kcl, acc):
    _, hr, cols = p.shape

    def body(k_ref, a_ref, b_ref, _acc_in, o_ref):
        o_ref[...] = a_ref[...].astype(F32) + b_ref[0].astype(F32) + b_ref[1].astype(F32) + b_ref[2].astype(F32)

    return pl.pallas_call(
        body, name="sum_chips",
        grid_spec=pltpu.PrefetchScalarGridSpec(
            num_scalar_prefetch=1, grid=(1,),
            in_specs=[pl.BlockSpec((None, hr, cols), lambda i, k_ref: (k_ref[0], 0, 0)),
                      pl.BlockSpec((3, hr, cols), lambda i, k_ref: (0, 0, 0)), ANY],
            out_specs=pl.BlockSpec((None, None, hr, cols), lambda i, k_ref: (k_ref[2], k_ref[1], 0, 0))),
        out_shape=_sds(acc.shape, F32), input_output_aliases={3: 0},
        compiler_params=_cp(dimension_semantics=("arbitrary",)),
    )(kcl, p, rb, acc)


N_DEV = 8


def _allreduce_small(part):
    _, r, lanes = part.shape

    def body(p_ref, o_ref, rbuf, s1, r1, s2, r2):
        x, y, c = _mesh_pos()
        me = 4 * x + 2 * y + c
        devs = [(d // 4, (d // 2) % 2, d % 2) for d in range(N_DEV)]
        rbuf[me] = p_ref[me]

        def each_peer(fn):
            for d in range(N_DEV):
                @pl.when(d != me)
                def _():
                    fn(d)

        each_peer(lambda d: _remote(p_ref.at[d], rbuf.at[me], s1.at[d], r1.at[me], devs[d]).start())
        each_peer(lambda d: _remote(p_ref.at[d], rbuf.at[d], s1.at[d], r1.at[d], devs[d]).wait_recv())
        total = rbuf[0]
        for d in range(1, N_DEV):
            total = total + rbuf[d]
        o_ref[me] = total
        each_peer(lambda d: _remote(o_ref.at[me], o_ref.at[me], s2.at[d], r2.at[me], devs[d]).start())
        each_peer(lambda d: _remote(o_ref.at[d], o_ref.at[d], s2.at[d], r2.at[d], devs[d]).wait_recv())
        each_peer(lambda d: _remote(p_ref.at[d], rbuf.at[me], s1.at[d], r1.at[me], devs[d]).wait_send())
        each_peer(lambda d: _remote(o_ref.at[me], o_ref.at[me], s2.at[d], r2.at[me], devs[d]).wait_send())

    sem = pltpu.SemaphoreType.DMA
    vm = pl.BlockSpec(memory_space=pltpu.VMEM)
    return pl.pallas_call(
        body, name="allreduce_small", in_specs=[vm], out_specs=vm, out_shape=_sds(part.shape, F32),
        scratch_shapes=[pltpu.VMEM(part.shape, F32), sem((N_DEV,)), sem((N_DEV,)), sem((N_DEV,)), sem((N_DEV,))],
        compiler_params=_cp(has_side_effects=True),
    )(part)


BIG = ("w_in", "w_1", "w_a_out", "w_b_out", "w_o", "w_2")
BY_COLS = {"w_in": True, "w_1": True, "w_a_out": False, "w_b_out": False, "w_o": False, "w_2": False}
WEIGHTS = ("g_mix", "w_in", "b_in", "conv_a_w", "conv_a_b", "ln_g", "ln_b", "w_a_out", "conv_b_w", "conv_b_b", "w_rg_a",
           "b_rg_a", "w_rg_x", "b_rg_x", "lam", "w_b_out", "w_o", "g_mlp", "w_1", "w_2", "g_final")
SMALL = tuple(n for n in WEIGHTS if n not in BIG)
ADAM_ROWS = 256
ADAM_SMALL_ROWS = 2048


def _block_diag(w):
    nh, dh, _ = w.shape
    ng = nh // HEADS_PER_GROUP
    w4 = w.reshape(ng, HEADS_PER_GROUP, dh, dh)
    eye = jnp.eye(HEADS_PER_GROUP, dtype=w.dtype)
    return jnp.einsum("qhij,hk->qhikj", w4, eye).reshape(ng, HEADS_PER_GROUP * dh, HEADS_PER_GROUP * dh)


def _block_diag_part(d, dh):
    ng = d.shape[0]
    eye = jnp.eye(HEADS_PER_GROUP, dtype=d.dtype)
    d5 = d.reshape(ng, HEADS_PER_GROUP, dh, HEADS_PER_GROUP, dh)
    return jnp.einsum("qhikj,hk->qhij", d5, eye).reshape(ng * HEADS_PER_GROUP, dh, dh)


PACK_LANES = 128


def _pack(arrays, row_multiple):
    parts = [a.reshape(-1, PACK_LANES) for a in arrays]
    parts = [jnp.pad(p, ((0, -p.shape[0] % 8), (0, 0))) if p.shape[0] % 8 else p for p in parts]
    rows = sum(p.shape[0] for p in parts)
    pad = -rows % row_multiple
    if pad:
        parts.append(jnp.zeros((pad, PACK_LANES), parts[0].dtype))
    return jnp.concatenate(parts, axis=0)


def _unpack(buf, like):
    buf = buf.reshape(-1, PACK_LANES)
    out, off = [], 0
    for a in like:
        n = a.size // PACK_LANES
        out.append(buf[off:off + n].reshape(a.shape))
        off += n + (-n % 8)
    return out


def kernel(x, g_mix, w_in, b_in, conv_a_w, conv_a_b, ln_g, ln_b, w_a_out, conv_b_w, conv_b_b, w_rg_a, b_rg_a, w_rg_x, b_rg_x, lam, w_b_out, w_o, g_mlp, w_1, w_2, g_final, loss_target, m_g_mix, m_w_in, m_b_in, m_conv_a_w, m_conv_a_b, m_ln_g, m_ln_b, m_w_a_out, m_conv_b_w, m_conv_b_b, m_w_rg_a, m_b_rg_a, m_w_rg_x, m_b_rg_x, m_lam, m_w_b_out, m_w_o, m_g_mlp, m_w_1, m_w_2, m_g_final, v_g_mix, v_w_in, v_b_in, v_conv_a_w, v_conv_a_b, v_ln_g, v_ln_b, v_w_a_out, v_conv_b_w, v_conv_b_b, v_w_rg_a, v_b_rg_a, v_w_rg_x, v_b_rg_x, v_lam, v_w_b_out, v_w_o, v_g_mlp, v_w_1, v_w_2, v_g_final):
    w = dict(g_mix=g_mix, w_in=w_in, b_in=b_in, conv_a_w=conv_a_w, conv_a_b=conv_a_b, ln_g=ln_g, ln_b=ln_b, w_a_out=w_a_out,
             conv_b_w=conv_b_w, conv_b_b=conv_b_b, w_rg_a=w_rg_a, b_rg_a=b_rg_a, w_rg_x=w_rg_x, b_rg_x=b_rg_x, lam=lam,
             w_b_out=w_b_out, w_o=w_o, g_mlp=g_mlp, w_1=w_1, w_2=w_2, g_final=g_final)
    m = dict(g_mix=m_g_mix, w_in=m_w_in, b_in=m_b_in, conv_a_w=m_conv_a_w, conv_a_b=m_conv_a_b, ln_g=m_ln_g, ln_b=m_ln_b,
             w_a_out=m_w_a_out, conv_b_w=m_conv_b_w, conv_b_b=m_conv_b_b, w_rg_a=m_w_rg_a, b_rg_a=m_b_rg_a, w_rg_x=m_w_rg_x,
             b_rg_x=m_b_rg_x, lam=m_lam, w_b_out=m_w_b_out, w_o=m_w_o, g_mlp=m_g_mlp, w_1=m_w_1, w_2=m_w_2, g_final=m_g_final)
    v = dict(g_mix=v_g_mix, w_in=v_w_in, b_in=v_b_in, conv_a_w=v_conv_a_w, conv_a_b=v_conv_a_b, ln_g=v_ln_g, ln_b=v_ln_b,
             w_a_out=v_w_a_out, conv_b_w=v_conv_b_w, conv_b_b=v_conv_b_b, w_rg_a=v_w_rg_a, b_rg_a=v_b_rg_a, w_rg_x=v_w_rg_x,
             b_rg_x=v_b_rg_x, lam=v_lam, w_b_out=v_w_b_out, w_o=v_w_o, g_mlp=v_g_mlp, w_1=v_w_1, w_2=v_w_2, g_final=v_g_final)
    B, S, D = x.shape
    T = B * S
    L = w_in.shape[0]
    dh = w_rg_a.shape[-1]
    taps_a, taps_b = conv_a_w.shape[1], conv_b_w.shape[1]
    assert (taps_a, taps_b) == (TAPS_A, TAPS_B)
    xi, yi, ci = _mesh_pos()
    c_arr = jnp.reshape(ci, (1,)).astype(jnp.int32)
    k_me = 2 * xi + yi

    caw_p = jnp.pad(conv_a_w, ((0, 0), (0, 32 - taps_a), (0, 0)))
    cbw_p = jnp.pad(conv_b_w, ((0, 0), (0, 8 - taps_b), (0, 0)))
    row = lambda a: a.reshape(1, -1)

    def shards_of(l):
        d = {n: w[n][l].astype(BF16) for n in BIG}
        d.update(caw=caw_p[l], cbw=cbw_p[l])
        return d

    def params_of(l, whole):
        p = dict(whole)
        p.update(cab=row(conv_a_b[l]), cbb=row(conv_b_b[l]),
                 wa=_block_diag(w_rg_a[l]).astype(BF16), wx=_block_diag(w_rg_x[l]).astype(BF16))
        for n in ("g_mix", "b_in", "ln_g", "ln_b", "b_rg_a", "b_rg_x", "lam", "g_mlp"):
            p[n] = row(w[n][l])
        return p

    first = shards_of(0)
    whole = _comm_call("gather_layer", _gather_comm([first[n] for n in GATHERED], [GATHER_KIND[n] for n in GATHERED]))
    params = [params_of(0, zip(GATHERED, whole))]
    xf = x.reshape(T, D)
    saved = []
    for l in range(L):
        if l + 1 < L:
            xf, s, whole = _layer_fwd(xf, params[l], S, nxt=shards_of(l + 1))
            params.append(params_of(l + 1, whole))
        else:
            xf, s = _layer_fwd(xf, params[l], S)
        saved.append(s)
    loss_part, dx, dxb, dg_final = _loss_head(xf, row(g_final), loss_target.reshape(T, D), _tiles(T)[0])
    loss = lax.psum(loss_part[0, 0], ("x", "y", "c"))

    half_shape = lambda a: (L, 2, a.shape[1] // 2, a.shape[2])
    accs = {n: lax.empty(half_shape(w[n]), F32) for n in BIG}
    small = {n: [None] * L for n in SMALL if n != "g_final"}
    red = _Reduce(accs, c_arr, lambda l: jnp.stack([k_me, ci, jnp.full((), l, ci.dtype)]).astype(jnp.int32))
    for l in reversed(range(L)):
        dx, dxb, g = _layer_bwd(dx, dxb, params[l], saved[l], S, red=red, layer=l)
        small["g_mix"][l], small["b_in"][l], small["g_mlp"][l] = g["g_mix"], g["b_in"], g["g_mlp"]
        small["conv_a_w"][l], small["conv_a_b"][l] = g["caw"], g["cab"]
        small["conv_b_w"][l], small["conv_b_b"][l] = g["cbw"], g["cbb"]
        small["ln_g"][l], small["ln_b"][l], small["lam"][l] = g["ln_g"], g["ln_b"], g["lam"]
        small["w_rg_a"][l], small["w_rg_x"][l] = _block_diag_part(g["wa"], dh), _block_diag_part(g["wx"], dh)
        small["b_rg_a"][l], small["b_rg_x"][l] = g["b_rg_a"], g["b_rg_x"]
    grad_x = dx.reshape(B, S, D)
    red.finish(["w_in"], red.pending, _comm_call("scatter_chips", _scatter_comm(red.pending)), 0)

    joined = _join_halves([red.accs[n] for n in BIG])
    grads = {n: a.reshape(w[n].shape) for n, a in zip(BIG, joined)}

    names = [n for n in SMALL if n != "g_final"]
    parts = [jnp.stack(small[n]) for n in names] + [dg_final]
    packed = _pack(parts, 8 * N_DEV)
    total = _allreduce_small(packed.reshape(N_DEV, packed.shape[0] // N_DEV, PACK_LANES))
    for n, a in zip(names + ["g_final"], _unpack(total, parts)):
        if n == "conv_a_w":
            a = lax.dynamic_slice_in_dim(a[:, :taps_a], k_me * conv_a_w.shape[2], conv_a_w.shape[2], axis=2)
        elif n == "conv_b_w":
            a = lax.dynamic_slice_in_dim(a[:, :taps_b], k_me * conv_b_w.shape[2], conv_b_w.shape[2], axis=2)
        grads[n] = a.reshape(w[n].shape)

    delta, new_m, new_v = {}, {}, {}
    for n in BIG:
        cols = w[n].shape[-1]
        d_, m_, v_ = _adamw("adamw_" + n, w[n].reshape(-1, cols), grads[n].reshape(-1, cols), m[n].reshape(-1, cols),
                            v[n].reshape(-1, cols), ADAM_ROWS)
        delta[n], new_m[n], new_v[n] = (a.reshape(w[n].shape) for a in (d_, m_, v_))
    like = [w[n] for n in SMALL]
    pk = lambda d: _pack([d[n] for n in SMALL], ADAM_SMALL_ROWS)
    d_, m_, v_ = _adamw("adamw_small", pk(w), pk(grads), pk(m), pk(v), ADAM_SMALL_ROWS)
    for n, a, b_, c_ in zip(SMALL, _unpack(d_, like), _unpack(m_, like), _unpack(v_, like)):
        delta[n], new_m[n], new_v[n] = a, b_, c_

    return (loss, grad_x, *[grads[n] for n in WEIGHTS], *[delta[n] for n in WEIGHTS],
            *[new_m[n] for n in WEIGHTS], *[new_v[n] for n in WEIGHTS])
```

```python
import jax
import jax.numpy as jnp
from jax import lax
from jax.experimental import pallas as pl
from jax.experimental.pallas import tpu as pltpu

F32 = jnp.float32
BF16 = jnp.bfloat16
MESH = pl.DeviceIdType.MESH

EPS = 1e-6
LRU_C = 8.0
ADAM_LR, ADAM_B1, ADAM_B2, ADAM_EPS, ADAM_WD, ADAM_STEP = 0.001, 0.9, 0.999, 1e-08, 0.01, 10

N_CHIPS = 4
HEADS_PER_GROUP = 4
VMEM_LIMIT = 56 * 1024 * 1024


def _cp(**kw):
    return pltpu.CompilerParams(vmem_limit_bytes=VMEM_LIMIT, **kw)


def _sig(x):
    return 1.0 / (1.0 + jnp.exp(-x))


def _gelu(x):
    t = jnp.tanh(0.7978845608028654 * (x + 0.044715 * x * x * x))
    return 0.5 * x * (1.0 + t), t


def _gelu_grad(x, t):
    dt = (1.0 - t * t) * 0.7978845608028654 * (1.0 + 3.0 * 0.044715 * x * x)
    return 0.5 * (1.0 + t) + 0.5 * x * dt


def _rms(xf, g):
    r = lax.rsqrt(jnp.mean(xf * xf, axis=-1, keepdims=True) + EPS)
    return xf * r * g, r


def _rms_bwd(xf, g, r, dh):
    dyg = dh * g
    dx = r * (dyg - xf * (r * r) * jnp.mean(dyg * xf, axis=-1, keepdims=True))
    return dx, dh * xf * r


def _ln_silu(u, g, b):
    mu = jnp.mean(u, axis=-1, keepdims=True)
    uc = u - mu
    rstd = lax.rsqrt(jnp.mean(uc * uc, axis=-1, keepdims=True) + EPS)
    uh = uc * rstd
    u2 = uh * g + b
    s = _sig(u2)
    return u2 * s, uh, rstd, u2, s


_DIMS = {"nn": (((1,), (0,)), ((), ())), "nt": (((1,), (1,)), ((), ())), "tn": (((0,), (0,)), ((), ()))}


class _Comm:
    def __init__(self, ins, outs, sems, start, finish):
        self.ins, self.outs, self.sems, self.start, self.finish = list(ins), list(outs), list(sems), start, finish


def _mm(name, mode, grid, a_ins, a_fn, b_in, e_ins, epi, outs, acc_shape, cache_a=None, alias=(), extra_scratch=(),
        comm=None):
    ni, nj, nk = grid
    na, ne, no = len(a_ins), len(e_ins), len(outs)
    assert cache_a is None or nk == 1
    n_fixed = (nk > 1) + (cache_a is not None)
    n_in = na + 1 + ne + len(alias)
    c_ins, c_outs, c_sems = (comm.ins, comm.outs, comm.sems) if comm else ([], [], [])

    def body(*refs):
        a_refs = refs[:na]
        b_ref = refs[na]
        e_refs = refs[na + 1:na + 1 + ne]
        comm_in = refs[n_in:n_in + len(c_ins)]
        out0 = n_in + len(c_ins)
        out_refs = refs[out0:out0 + no]
        comm_out = refs[out0 + no:out0 + no + len(c_outs)]
        scratch = refs[out0 + no + len(c_outs):]
        extra = scratch[n_fixed:n_fixed + len(extra_scratch)]
        comm_sems = scratch[n_fixed + len(extra_scratch):]
        i, j, k = pl.program_id(0), pl.program_id(1), pl.program_id(2)
        if comm:
            @pl.when((i == 0) & (j == 0) & (k == 0))
            def _():
                comm.start(comm_in, comm_out, comm_sems)
        if cache_a is not None:
            cache_ref = scratch[n_fixed - 1]

            @pl.when(j == 0)
            def _():
                cache_ref[...] = a_fn(a_refs, out_refs, i, j, k)

            a = cache_ref[...]
        else:
            a = a_fn(a_refs, out_refs, i, j, k)
        prod = lax.dot_general(a, b_ref[...], _DIMS[mode], preferred_element_type=F32)
        if nk == 1:
            epi(prod, e_refs, out_refs, i, j, extra)
        else:
            acc_ref = scratch[0]

            @pl.when(k == 0)
            def _():
                acc_ref[...] = prod

            @pl.when(k > 0)
            def _():
                acc_ref[...] += prod

            @pl.when(k == nk - 1)
            def _():
                epi(acc_ref[...], e_refs, out_refs, i, j, extra)

        if comm:
            @pl.when((i == ni - 1) & (j == nj - 1) & (k == nk - 1))
            def _():
                comm.finish(comm_in, comm_out, comm_sems)

    scratch_shapes = []
    if nk > 1:
        scratch_shapes.append(pltpu.VMEM(acc_shape, F32))
    if cache_a is not None:
        scratch_shapes.append(pltpu.VMEM(cache_a, BF16))
    any_spec = pl.BlockSpec(memory_space=pl.ANY)
    ins = (list(a_ins) + [b_in] + list(e_ins) + [(arr, any_spec) for arr, _ in alias] + [(arr, any_spec) for arr in c_ins])
    first_alias = na + 1 + ne
    res = pl.pallas_call(
        body, name=name, grid=grid,
        in_specs=[s for _, s in ins], out_specs=[s for _, s in outs] + [any_spec] * len(c_outs),
        out_shape=[o for o, _ in outs] + list(c_outs),
        scratch_shapes=scratch_shapes + list(extra_scratch) + list(c_sems),
        input_output_aliases={first_alias + n: o for n, (_, o) in enumerate(alias)},
        compiler_params=_cp(dimension_semantics=("arbitrary", "arbitrary", "arbitrary"), has_side_effects=bool(comm)),
    )(*[a for a, _ in ins])
    if comm:
        return list(res[:no]), list(res[no:])
    return res


def _bs(shape, fn):
    return pl.BlockSpec(shape, fn)


def _sds(shape, dt):
    return jax.ShapeDtypeStruct(shape, dt)


def _acc_rows(ref, val, first):
    @pl.when(first)
    def _():
        ref[...] = val

    @pl.when(jnp.logical_not(first))
    def _():
        ref[...] += val


def _fwd_norm_mm(name, x, g, w, bias, tm, tn, comm=None):
    T, D = x.shape
    N = w.shape[1]

    def a_fn(a_refs, out_refs, i, j, k):
        h, _ = _rms(a_refs[0][...], a_refs[1][...])
        hb = h.astype(BF16)
        out_refs[1][...] = hb
        return hb

    def epi(acc, e_refs, out_refs, i, j, extra):
        if bias is not None:
            acc = acc + e_refs[0][...]
        out_refs[0][...] = acc.astype(BF16)

    e_ins = [] if bias is None else [(bias, _bs((1, tn), lambda i, j, k: (0, j)))]
    return _mm(name, "nn", (T // tm, N // tn, 1),
               [(x, _bs((tm, D), lambda i, j, k: (i, 0))), (g, _bs((1, D), lambda i, j, k: (0, 0)))], a_fn,
               (w, _bs((D, tn), lambda i, j, k: (0, j))), e_ins, epi,
               [(_sds((T, N), BF16), _bs((tm, tn), lambda i, j, k: (i, j))),
                (_sds((T, D), BF16), _bs((tm, D), lambda i, j, k: (i, 0)))],
               None, cache_a=(tm, D), comm=comm)


def _fwd_ya(u1, ln_g, ln_b, w, tm):
    T, C = u1.shape
    N = w.shape[1]

    def a_fn(a_refs, out_refs, i, j, k):
        u3 = _ln_silu(a_refs[0][...].astype(F32), a_refs[1][...], a_refs[2][...])[0]
        return u3.astype(BF16)

    def epi(acc, e_refs, out_refs, i, j, extra):
        out_refs[0][...] = acc.astype(BF16)

    row = _bs((1, C), lambda i, j, k: (0, 0))
    return _mm("fwd_ya", "nn", (T // tm, 1, 1),
               [(u1, _bs((tm, C), lambda i, j, k: (i, 0))), (ln_g, row), (ln_b, row)], a_fn,
               (w, _bs((C, N), lambda i, j, k: (0, 0))), [], epi,
               [(_sds((T, N), BF16), _bs((tm, N), lambda i, j, k: (i, 0)))], None)[0]


def _fwd_yb(h, z, gb_blk, w, tm, tk):
    T, C = h.shape
    N = w.shape[1]

    def a_fn(a_refs, out_refs, i, j, k):
        ge, _ = _gelu(a_refs[1][...].astype(F32))
        return (a_refs[0][...].astype(F32) * ge).astype(BF16)

    def epi(acc, e_refs, out_refs, i, j, extra):
        out_refs[0][...] = acc.astype(BF16)

    return _mm("fwd_yb", "nn", (T // tm, 1, C // tk),
               [(h, _bs((tm, tk), lambda i, j, k: (i, k))), (z, _bs((tm, tk), lambda i, j, k: (i, gb_blk + k)))], a_fn,
               (w, _bs((tk, N), lambda i, j, k: (k, 0))), [], epi,
               [(_sds((T, N), BF16), _bs((tm, N), lambda i, j, k: (i, 0)))], (tm, N))[0]


def _fwd_x1(x, ya, yb, z, sa_blk, w, tm):
    T, D = x.shape

    def a_fn(a_refs, out_refs, i, j, k):
        ya_, yb_, sa_, sb_ = (r[...].astype(F32) for r in a_refs)
        return (_sig(sa_) * ya_ + _sig(sb_) * yb_).astype(BF16)

    def epi(acc, e_refs, out_refs, i, j, extra):
        out_refs[0][...] = e_refs[0][...] + acc

    t = _bs((tm, D), lambda i, j, k: (i, 0))
    return _mm("fwd_x1", "nn", (T // tm, 1, 1),
               [(ya, t), (yb, t), (z, _bs((tm, D), lambda i, j, k: (i, sa_blk))),
                (z, _bs((tm, D), lambda i, j, k: (i, sa_blk + 1)))], a_fn,
               (w, _bs((D, D), lambda i, j, k: (0, 0))), [(x, t)], epi,
               [(_sds((T, D), F32), t)], None)[0]


def _fwd_x2(x1, fp, w, tm, tk, comm=None):
    T, D = x1.shape
    Fd = fp.shape[1]

    def a_fn(a_refs, out_refs, i, j, k):
        f = jnp.maximum(a_refs[0][...].astype(F32), 0.0)
        return (f * f).astype(BF16)

    def epi(acc, e_refs, out_refs, i, j, extra):
        out_refs[0][...] = e_refs[0][...] + acc

    t = _bs((tm, D), lambda i, j, k: (i, 0))
    r = _mm("fwd_x2", "nn", (T // tm, 1, Fd // tk),
            [(fp, _bs((tm, tk), lambda i, j, k: (i, k)))], a_fn,
            (w, _bs((tk, D), lambda i, j, k: (k, 0))), [(x1, t)], epi,
            [(_sds((T, D), F32), t)], (tm, D), comm=comm)
    return (r[0][0], r[1]) if comm else r[0]


def _loss_head(x, g, target, tm):
    T, D = x.shape

    def body(x_ref, g_ref, t_ref, loss_ref, dx_ref, dxb_ref, dg_ref):
        i = pl.program_id(0)
        xf, gv = x_ref[...], g_ref[...]
        y, r = _rms(xf, gv)
        err = y - t_ref[...]
        part = 0.5 * jnp.sum(jnp.mean(err * err, axis=-1, keepdims=True), axis=0, keepdims=True)
        dx, dg_rows = _rms_bwd(xf, gv, r, err * (1.0 / D))
        dx_ref[...] = dx
        dxb_ref[...] = dx.astype(BF16)
        _acc_rows(loss_ref, jnp.broadcast_to(part, (1, 128)), i == 0)
        _acc_rows(dg_ref, jnp.sum(dg_rows, axis=0, keepdims=True), i == 0)

    t = _bs((tm, D), lambda i: (i, 0))
    row = _bs((1, D), lambda i: (0, 0))
    return pl.pallas_call(
        body, name="loss_head", grid=(T // tm,), in_specs=[t, row, t],
        out_specs=[_bs((1, 128), lambda i: (0, 0)), t, t, row],
        out_shape=[_sds((1, 128), F32), _sds((T, D), F32), _sds((T, D), BF16), _sds((1, D), F32)],
        compiler_params=_cp(dimension_semantics=("arbitrary",)),
    )(x, g, target)


def _adamw(name, w, g, m, v, tr):
    rows, cols = w.shape
    d1 = 1.0 - ADAM_B1 ** ADAM_STEP
    d2 = 1.0 - ADAM_B2 ** ADAM_STEP

    def body(w_ref, g_ref, m_ref, v_ref, d_ref, mo_ref, vo_ref):
        gv = g_ref[...]
        mn = ADAM_B1 * m_ref[...] + (1.0 - ADAM_B1) * gv
        vn = ADAM_B2 * v_ref[...] + (1.0 - ADAM_B2) * (gv * gv)
        d_ref[...] = -ADAM_LR * ((mn / d1) / (jnp.sqrt(vn / d2) + ADAM_EPS) + ADAM_WD * w_ref[...])
        mo_ref[...] = mn
        vo_ref[...] = vn

    t = _bs((tr, cols), lambda i: (i, 0))
    return pl.pallas_call(
        body, name=name, grid=(rows // tr,), in_specs=[t] * 4, out_specs=[t] * 3,
        out_shape=[_sds((rows, cols), F32)] * 3,
        compiler_params=_cp(dimension_semantics=("arbitrary",)),
    )(w, g, m, v)


def _ident(a_refs, out_refs, i, j, k):
    return a_refs[0][...]


def _bwd_dw(name, act, dy, ti, tj, tm, a_fn=None, a_extra=(), shard_cols=None):
    T, J = dy.shape
    I = act.shape[1]

    def epi(acc, e_refs, out_refs, i, j, extra):
        out_refs[0][...] = acc.astype(BF16)

    if shard_cols is None:
        out = (_sds((I, J), BF16), _bs((ti, tj), lambda i, j, k: (i, j)))
    else:
        per = shard_cols // tj
        assert ti == I // 2 and per * tj == shard_cols
        out = (_sds((J // shard_cols, 2, ti, shard_cols), BF16),
               _bs((None, None, ti, tj), lambda i, j, k: (lax.div(j, per), i, 0, lax.rem(j, per))))
    a_ins = [(act, _bs((tm, ti), lambda i, j, k: (k, i)))] + list(a_extra)
    return _mm(name, "tn", (I // ti, J // tj, T // tm), a_ins, a_fn or _ident,
               (dy, _bs((tm, tj), lambda i, j, k: (k, j))), [], epi, [out], (ti, tj))[0]


def _bwd_df(dxb, w2, fp, tm, tn):
    T, D = dxb.shape
    Fd = w2.shape[0]

    def epi(acc, e_refs, out_refs, i, j, extra):
        out_refs[0][...] = (acc * (2.0 * jnp.maximum(e_refs[0][...].astype(F32), 0.0))).astype(BF16)

    t = _bs((tm, tn), lambda i, j, k: (i, j))
    return _mm("bwd_df", "nt", (T // tm, Fd // tn, 1), [(dxb, _bs((tm, D), lambda i, j, k: (i, 0)))], _ident,
               (w2, _bs((tn, D), lambda i, j, k: (j, 0))), [(fp, t)], epi, [(_sds((T, Fd), BF16), t)], None)[0]


def _bwd_norm(name, dy, w, x, g, dres, tm, tk, colsum=False, comm=None):
    T, K = dy.shape
    D = w.shape[0]
    nk = K // tk

    def a_fn(a_refs, out_refs, i, j, k):
        a = a_refs[0][...]
        if colsum:
            s = jnp.sum(a.astype(F32), axis=0, keepdims=True)

            @pl.when(i == 0)
            def _():
                out_refs[3][k] = s

            @pl.when(i > 0)
            def _():
                out_refs[3][k] += s
        return a

    def epi(acc, e_refs, out_refs, i, j, extra):
        xf, gv = e_refs[0][...], e_refs[1][...]
        r = lax.rsqrt(jnp.mean(xf * xf, axis=-1, keepdims=True) + EPS)
        dx, dg_rows = _rms_bwd(xf, gv, r, acc)
        dx = dx + e_refs[2][...]
        out_refs[0][...] = dx
        out_refs[1][...] = dx.astype(BF16)
        _acc_rows(out_refs[2], jnp.sum(dg_rows, axis=0, keepdims=True), i == 0)

    t = _bs((tm, D), lambda i, j, k: (i, 0))
    row = _bs((1, D), lambda i, j, k: (0, 0))
    outs = [(_sds((T, D), F32), t), (_sds((T, D), BF16), t), (_sds((1, D), F32), row)]
    if colsum:
        outs.append((_sds((nk, 1, tk), F32), _bs((nk, 1, tk), lambda i, j, k: (0, 0, 0))))
    return _mm(name, "nt", (T // tm, 1, nk), [(dy, _bs((tm, tk), lambda i, j, k: (i, k)))], a_fn,
               (w, _bs((D, tk), lambda i, j, k: (0, k))), [(x, t), (g, row), (dres, t)], epi, outs, (tm, D), comm=comm)


def _bwd_dm(dxb, w_o, ya, yb, z, sa_blk, tm):
    T, D = dxb.shape

    def epi(acc, e_refs, out_refs, i, j, extra):
        ya_, yb_, sa_, sb_ = (r[...].astype(F32) for r in e_refs)
        ga, gb = _sig(sa_), _sig(sb_)
        out_refs[0][...] = (acc * ga).astype(BF16)
        out_refs[1][...] = (acc * gb).astype(BF16)
        stage = extra[0]
        stage[:, 0:D] = (acc * ya_ * ga * (1.0 - ga)).astype(BF16)
        stage[:, D:2 * D] = (acc * yb_ * gb * (1.0 - gb)).astype(BF16)
        pltpu.sync_copy(stage, out_refs[2].at[pl.ds(pl.multiple_of(i * tm, tm), tm), pl.ds(sa_blk * D, 2 * D)])

    t = _bs((tm, D), lambda i, j, k: (i, 0))
    return _mm("bwd_dm", "nt", (T // tm, 1, 1), [(dxb, t)], _ident, (w_o, _bs((D, D), lambda i, j, k: (0, 0))),
               [(ya, t), (yb, t), (z, _bs((tm, D), lambda i, j, k: (i, sa_blk))),
                (z, _bs((tm, D), lambda i, j, k: (i, sa_blk + 1)))], epi,
               [(_sds((T, D), BF16), t), (_sds((T, D), BF16), t),
                (_sds(z.shape, BF16), pl.BlockSpec(memory_space=pl.ANY))], None,
               extra_scratch=[pltpu.VMEM((tm, 2 * D), BF16)])


def _bwd_du3(dya, w, u1, ln_g, ln_b, tm):
    T, D = dya.shape
    C = w.shape[0]

    def epi(acc, e_refs, out_refs, i, j, extra):
        gv = e_refs[1][...]
        _, uh, rstd, u2, s = _ln_silu(e_refs[0][...].astype(F32), gv, e_refs[2][...])
        du2 = acc * (s * (1.0 + u2 * (1.0 - s)))
        duh = du2 * gv
        out_refs[0][...] = rstd * (duh - jnp.mean(duh, axis=-1, keepdims=True)
                                   - uh * jnp.mean(duh * uh, axis=-1, keepdims=True))
        _acc_rows(out_refs[1], jnp.sum(du2 * uh, axis=0, keepdims=True), i == 0)
        _acc_rows(out_refs[2], jnp.sum(du2, axis=0, keepdims=True), i == 0)

    t = _bs((tm, C), lambda i, j, k: (i, 0))
    row = _bs((1, C), lambda i, j, k: (0, 0))
    return _mm("bwd_du3", "nt", (T // tm, 1, 1), [(dya, _bs((tm, D), lambda i, j, k: (i, 0)))], _ident,
               (w, _bs((C, D), lambda i, j, k: (0, 0))), [(u1, t), (ln_g, row), (ln_b, row)], epi,
               [(_sds((T, C), F32), t), (_sds((1, C), F32), row), (_sds((1, C), F32), row)], None)


def _bwd_dp(dyb, w, h, z, dz, gb_blk, tm, tn):
    T, D = dyb.shape
    R = w.shape[0]

    def epi(acc, e_refs, out_refs, i, j, extra):
        gbv = e_refs[1][...].astype(F32)
        ge, th = _gelu(gbv)
        out_refs[0][...] = acc * ge
        out_refs[1][...] = (acc * e_refs[0][...].astype(F32) * _gelu_grad(gbv, th)).astype(BF16)

    t = _bs((tm, tn), lambda i, j, k: (i, j))
    tz = _bs((tm, tn), lambda i, j, k: (i, gb_blk + j))
    return _mm("bwd_dp", "nt", (T // tm, R // tn, 1), [(dyb, _bs((tm, D), lambda i, j, k: (i, 0)))], _ident,
               (w, _bs((tn, D), lambda i, j, k: (j, 0))), [(h, t), (z, tz)], epi,
               [(_sds((T, R), F32), t), (_sds(dz.shape, BF16), tz)], None, cache_a=None, alias=[(dz, 1)])


CONV_ROWS = 32


def _shifted_taps(x, halo, shifts, fn):
    n = CONV_ROWS + halo
    by_r = {}
    for k, s in shifts:
        by_r.setdefault(s % 8, []).append((k, s))
    for r in sorted(by_r):
        xr = x if r == 0 else pltpu.roll(x, n - r, 0)
        for k, s in by_r[r]:
            q = s - r
            fn(k, xr[q:q + CONV_ROWS])


def _conv_fwd(name, z, blk0, gate_blk0, w_pad, bias, taps, seq, tc, out_dtype, comm=None):
    T = z.shape[0]
    C = w_pad.shape[1]
    nb, nj = T // seq, C // tc
    pad = 8 * ((taps - 1 + 7) // 8)
    halo = pad
    shifts = [(k, pad - (taps - 1) + k) for k in range(taps)]
    glu = gate_blk0 is not None

    def body(*refs):
        if glu:
            v_ref, g_ref, w_ref, b_ref, o_ref, p_ref = refs
        else:
            v_ref, w_ref, b_ref, o_ref, p_ref = refs
        p_ref[pl.ds(0, pad), :] = jnp.zeros((pad, tc), F32)
        u = v_ref[...].astype(F32)
        if glu:
            u = u * _sig(g_ref[...].astype(F32))
        p_ref[pl.ds(pad, seq), :] = u

        def step(c, _):
            base = pl.multiple_of(c * CONV_ROWS, CONV_ROWS)
            x = p_ref[pl.ds(base, CONV_ROWS + halo), :]
            acc = [jnp.zeros((CONV_ROWS, tc), F32) + b_ref[...]]

            def tap(k, xs):
                acc[0] = acc[0] + w_ref[k:k + 1, :] * xs

            _shifted_taps(x, halo, shifts, tap)
            o_ref[pl.ds(base, CONV_ROWS), :] = acc[0].astype(out_dtype)
            return 0

        lax.fori_loop(0, seq // CONV_ROWS, step, 0)

    zin = [(z, _bs((seq, tc), lambda b, j: (b, blk0 + j)))]
    if glu:
        zin.append((z, _bs((seq, tc), lambda b, j: (b, gate_blk0 + j))))
    ins = zin + [(w_pad, _bs((w_pad.shape[0], tc), lambda b, j: (0, j))), (bias, _bs((1, tc), lambda b, j: (0, j)))]
    r = _call_with_comm(name, body, (nb, nj), [a for a, _ in ins], [s for _, s in ins],
                        [_bs((seq, tc), lambda b, j: (b, j))], [_sds((T, C), out_dtype)],
                        [pltpu.VMEM((seq + pad, tc), F32)], comm)
    return (r[0][0], r[1]) if comm else r[0]


def _conv_bwd(name, dy, z, dz, blk0, gate_blk0, w_pad, taps, seq, tc, comm=None):
    T = z.shape[0]
    C = w_pad.shape[1]
    nb, nj = T // seq, C // tc
    kp = w_pad.shape[0]
    pad = 8 * ((taps - 1 + 7) // 8)
    halo = pad
    sh_du = [(k, taps - 1 - k) for k in range(taps)]
    sh_dw = [(k, pad - (taps - 1) + k) for k in range(taps)]
    glu = gate_blk0 is not None

    def body(*refs):
        if glu:
            dy_ref, v_ref, g_ref, w_ref, _dz_in, dz_out, dw_ref, db_ref, pdy, pu, du_s, wacc, ob, ob2 = refs
        else:
            dy_ref, v_ref, w_ref, _dz_in, dz_out, dw_ref, db_ref, pdy, pu, du_s, wacc, ob = refs
        j = pl.program_id(0)
        b = pl.program_id(1)
        pdy[pl.ds(seq, pad), :] = jnp.zeros((pad, tc), F32)
        pdy[pl.ds(0, seq), :] = dy_ref[...].astype(F32)
        pu[pl.ds(0, pad), :] = jnp.zeros((pad, tc), F32)
        v = v_ref[...].astype(F32)
        if glu:
            sg = _sig(g_ref[...].astype(F32))
            pu[pl.ds(pad, seq), :] = v * sg
        else:
            pu[pl.ds(pad, seq), :] = v
        wacc[...] = jnp.zeros(wacc.shape, F32)

        def step(c, dbacc):
            base = pl.multiple_of(c * CONV_ROWS, CONV_ROWS)
            xdy = pdy[pl.ds(base, CONV_ROWS + halo), :]
            acc = [jnp.zeros((CONV_ROWS, tc), F32)]

            def tap(k, xs):
                acc[0] = acc[0] + w_ref[k:k + 1, :] * xs

            _shifted_taps(xdy, halo, sh_du, tap)
            du_s[pl.ds(base, CONV_ROWS), :] = acc[0]
            dyc = xdy[0:CONV_ROWS]
            xu = pu[pl.ds(base, CONV_ROWS + halo), :]

            def wtap(k, xs):
                p = dyc * xs
                s8 = p[0:8]
                for m in range(1, CONV_ROWS // 8):
                    s8 = s8 + p[8 * m:8 * m + 8]
                wacc[pl.ds(8 * k, 8), :] += s8

            _shifted_taps(xu, halo, sh_dw, wtap)
            d8 = dyc[0:8]
            for m in range(1, CONV_ROWS // 8):
                d8 = d8 + dyc[8 * m:8 * m + 8]
            return dbacc + d8

        dbacc = lax.fori_loop(0, seq // CONV_ROWS, step, jnp.zeros((8, tc), F32))
        du = du_s[...]
        rows = pl.ds(pl.multiple_of(b * seq, seq), seq)
        if glu:
            ob[...] = (du * sg).astype(BF16)
            ob2[...] = (du * v * sg * (1.0 - sg)).astype(BF16)
            pltpu.sync_copy(ob2, dz_out.at[rows, pl.ds(pl.multiple_of((gate_blk0 + j) * tc, tc), tc)])
        else:
            ob[...] = du.astype(BF16)
        pltpu.sync_copy(ob, dz_out.at[rows, pl.ds(pl.multiple_of((blk0 + j) * tc, tc), tc)])
        dw = jnp.sum(wacc[...].reshape(kp, 8, tc), axis=1)
        _acc_rows(dw_ref, dw, b == 0)
        _acc_rows(db_ref, jnp.sum(dbacc, axis=0, keepdims=True), b == 0)

    zin = [(z, _bs((seq, tc), lambda j, b: (b, blk0 + j)))]
    if glu:
        zin.append((z, _bs((seq, tc), lambda j, b: (b, gate_blk0 + j))))
    ins = [(dy, _bs((seq, tc), lambda j, b: (b, j)))] + zin + [(w_pad, _bs((kp, tc), lambda j, b: (0, j))),
                                                               (dz, pl.BlockSpec(memory_space=pl.ANY))]
    dz_idx = len(ins) - 1
    out_specs = [pl.BlockSpec(memory_space=pl.ANY), _bs((kp, tc), lambda j, b: (0, j)), _bs((1, tc), lambda j, b: (0, j))]
    out_shape = [_sds(dz.shape, dz.dtype), _sds((kp, C), F32), _sds((1, C), F32)]
    stage = [pltpu.VMEM((seq, tc), BF16)] * (2 if glu else 1)
    return _call_with_comm(
        name, body, (nj, nb), [a for a, _ in ins], [s for _, s in ins], out_specs, out_shape,
        [pltpu.VMEM((seq + pad, tc), F32), pltpu.VMEM((seq + pad, tc), F32),
         pltpu.VMEM((seq, tc), F32), pltpu.VMEM((8 * kp, tc), F32)] + stage, comm, aliases={dz_idx: 0})


RG_ROWS = 256


def _softplus_neg(lam):
    return jnp.maximum(-lam, 0.0) + jnp.log(1.0 + jnp.exp(-jnp.abs(lam)))


def _gates(v0c, wa_ref, wx_ref, ba, bx, sp, first_row):
    vb = v0c.astype(BF16)
    r = _sig(jnp.dot(vb, wa_ref[...], preferred_element_type=F32) + ba)
    i = _sig(jnp.dot(vb, wx_ref[...], preferred_element_type=F32) + bx)
    la = -LRU_C * r * sp
    a = jnp.exp(la)
    a2 = a * a
    x = 2.0 * la
    series = -x * (1.0 + x * (1.0 / 2) * (1.0 + x * (1.0 / 3) * (1.0 + x * (1.0 / 4) * (1.0 + x * (1.0 / 5)))))
    mult = jnp.sqrt(jnp.where(x > -0.1, series, 1.0 - a2))
    dmult = jnp.where(first_row, 0.0, -a2 / mult)
    mult = jnp.where(first_row, 1.0, mult)
    return r, i, a, mult, dmult


def _group_scan(a, b, reverse):
    n = a.shape[0]
    row = lax.broadcasted_iota(jnp.int32, a.shape, 0) & 7
    for d in (1, 2, 4):
        sh = n - d if reverse else d
        a_s, b_s = pltpu.roll(a, sh, 0), pltpu.roll(b, sh, 0)
        m = (row < 8 - d) if reverse else (row >= d)
        b = jnp.where(m, a * b_s + b, b)
        a = jnp.where(m, a * a_s, a)
    return a, b


def _group_carry(a_s, b_s, o_s, n_groups, reverse):
    cols = a_s.shape[1]

    def step(g, carry):
        g = n_groups - 1 - g if reverse else g
        rows = pl.ds(pl.multiple_of(g * 8, 8), 8)
        o = a_s[rows, :] * carry + b_s[rows, :]
        o_s[rows, :] = o
        return o[0:1, :] if reverse else o[7:8, :]

    lax.fori_loop(0, n_groups, step, jnp.zeros((1, cols), F32))


def _rglru_fwd(v0, wa, wx, ba, bx, lam, seq, comm=None):
    T, C = v0.shape
    ng, G = wa.shape[0], wa.shape[1]
    nb = T // seq

    def body(v_ref, wa_ref, wx_ref, ba_ref, bx_ref, lam_ref, h_ref, a_s, b_s, h_s):
        sp = _softplus_neg(lam_ref[...])

        def chunk(c, _):
            rows = pl.ds(pl.multiple_of(c * RG_ROWS, RG_ROWS), RG_ROWS)
            t = lax.broadcasted_iota(jnp.int32, (RG_ROWS, G), 0) + c * RG_ROWS
            v0c = v_ref[rows, :]
            _, i, a, mult, _ = _gates(v0c, wa_ref, wx_ref, ba_ref[...], bx_ref[...], sp, t == 0)
            a_g, b_g = _group_scan(a, mult * i * v0c, False)
            a_s[rows, :] = a_g
            b_s[rows, :] = b_g
            return 0

        lax.fori_loop(0, seq // RG_ROWS, chunk, 0)
        _group_carry(a_s, b_s, h_s, seq // 8, False)
        h_ref[...] = h_s[...].astype(BF16)

    t2 = _bs((seq, G), lambda b, g: (b, g))
    wsp = _bs((None, G, G), lambda b, g: (g, 0, 0))
    row = _bs((1, G), lambda b, g: (0, g))
    return _call_with_comm("rglru_fwd", body, (nb, ng), [v0, wa, wx, ba, bx, lam], [t2, wsp, wsp, row, row, row],
                           [t2], [_sds((T, C), BF16)], [pltpu.VMEM((seq, G), F32)] * 3, comm)


def _call_with_comm(name, body, grid, ins, in_specs, out_specs, out_shape, scratch, comm, aliases=None):
    n_in, n_out, n_s = len(ins), len(out_shape), len(scratch)
    c_ins, c_outs, c_sems = (comm.ins, comm.outs, comm.sems) if comm else ([], [], [])

    def wrapped(*refs):
        o0 = n_in + len(c_ins)
        s0 = o0 + n_out + len(c_outs)
        cin, cout, csem = refs[n_in:o0], refs[o0 + n_out:s0], refs[s0 + n_s:]
        ids = [pl.program_id(a) for a in range(len(grid))]
        if comm:
            first = _all_of([i == 0 for i in ids])

            @pl.when(first)
            def _():
                comm.start(cin, cout, csem)

        body(*refs[:n_in], *refs[o0:o0 + n_out], *refs[s0:s0 + n_s])
        if comm:
            last = _all_of([i == n - 1 for i, n in zip(ids, grid)])

            @pl.when(last)
            def _():
                comm.finish(cin, cout, csem)

    res = pl.pallas_call(
        wrapped, name=name, grid=grid, in_specs=list(in_specs) + [ANY] * len(c_ins),
        out_specs=list(out_specs) + [ANY] * len(c_outs), out_shape=list(out_shape) + list(c_outs),
        scratch_shapes=list(scratch) + list(c_sems), input_output_aliases=aliases or {},
        compiler_params=_cp(dimension_semantics=("arbitrary",) * len(grid), has_side_effects=bool(comm)),
    )(*ins, *c_ins)
    return (list(res[:n_out]), list(res[n_out:])) if comm else list(res)


def _all_of(conds):
    out = conds[0]
    for c in conds[1:]:
        out = out & c
    return out


def _rglru_bwd(v0, h, dh, wa, wx, ba, bx, lam, seq, comm=None):
    T, C = v0.shape
    ng, G = wa.shape[0], wa.shape[1]
    nb = T // seq
    R = RG_ROWS

    def body(v_ref, h_ref, dh_ref, wa_ref, wx_ref, ba_ref, bx_ref, lam_ref,
             dv_ref, dwa_ref, dwx_ref, dba_ref, dbx_ref, dlam_ref, a_s, b_s, q_s, hp_s):
        b = pl.program_id(1)
        lam_v = lam_ref[...]
        sp = _softplus_neg(lam_v)
        dsp_dlam = -_sig(-lam_v)

        @pl.when(b == 0)
        def _():
            dwa_ref[...] = jnp.zeros((G, G), F32)
            dwx_ref[...] = jnp.zeros((G, G), F32)
            dba_ref[...] = jnp.zeros((1, G), F32)
            dbx_ref[...] = jnp.zeros((1, G), F32)
            dlam_ref[...] = jnp.zeros((1, G), F32)

        hp_s[pl.ds(0, 8), :] = jnp.zeros((8, G), F32)
        hp_s[pl.ds(8, seq), :] = h_ref[...].astype(F32)
        q_s[pl.ds(seq, 8), :] = jnp.zeros((8, G), F32)

        def chunk1(c, _):
            rows = pl.ds(pl.multiple_of(c * R, R), R)
            t = lax.broadcasted_iota(jnp.int32, (R, G), 0) + c * R
            _, _, a, _, _ = _gates(v_ref[rows, :], wa_ref, wx_ref, ba_ref[...], bx_ref[...], sp, t == 0)
            a_g, b_g = _group_scan(a, a * dh_ref[rows, :].astype(F32), True)
            a_s[rows, :] = a_g
            b_s[rows, :] = b_g
            return 0

        lax.fori_loop(0, seq // R, chunk1, 0)
        _group_carry(a_s, b_s, q_s, seq // 8, True)

        def chunk3(c, _):
            base = pl.multiple_of(c * R, R)
            rows = pl.ds(base, R)
            t = lax.broadcasted_iota(jnp.int32, (R, G), 0) + c * R
            v0c = v_ref[rows, :]
            r, i, a, mult, dmult_dla = _gates(v0c, wa_ref, wx_ref, ba_ref[...], bx_ref[...], sp, t == 0)
            q_next = pltpu.roll(q_s[pl.ds(base, R + 8), :], R + 7, 0)[0:R]
            h_prev = pltpu.roll(hp_s[pl.ds(base, R + 8), :], R + 1, 0)[0:R]
            gt = dh_ref[rows, :].astype(F32) + q_next
            dla = gt * h_prev * a + gt * i * v0c * dmult_dla
            dpa = dla * (-LRU_C * sp) * r * (1.0 - r)
            dpx = gt * mult * v0c * i * (1.0 - i)
            dpa_b, dpx_b, v_b = dpa.astype(BF16), dpx.astype(BF16), v0c.astype(BF16)
            dv_ref[rows, :] = (gt * mult * i
                               + lax.dot_general(dpa_b, wa_ref[...], _DIMS["nt"], preferred_element_type=F32)
                               + lax.dot_general(dpx_b, wx_ref[...], _DIMS["nt"], preferred_element_type=F32))
            dwa_ref[...] += lax.dot_general(v_b, dpa_b, _DIMS["tn"], preferred_element_type=F32)
            dwx_ref[...] += lax.dot_general(v_b, dpx_b, _DIMS["tn"], preferred_element_type=F32)
            dba_ref[...] += jnp.sum(dpa, axis=0, keepdims=True)
            dbx_ref[...] += jnp.sum(dpx, axis=0, keepdims=True)
            dlam_ref[...] += jnp.sum(dla * (-LRU_C * r), axis=0, keepdims=True) * dsp_dlam
            return 0

        lax.fori_loop(0, seq // R, chunk3, 0)

    t2 = _bs((seq, G), lambda g, b: (b, g))
    wsp = _bs((None, G, G), lambda g, b: (g, 0, 0))
    row = _bs((1, G), lambda g, b: (0, g))
    return _call_with_comm(
        "rglru_bwd", body, (ng, nb), [v0, h, dh, wa, wx, ba, bx, lam], [t2, t2, t2, wsp, wsp, row, row, row],
        [t2, wsp, wsp, row, row, row],
        [_sds((T, C), F32), _sds((ng, G, G), F32), _sds((ng, G, G), F32),
         _sds((1, C), F32), _sds((1, C), F32), _sds((1, C), F32)],
        [pltpu.VMEM((seq, G), F32), pltpu.VMEM((seq, G), F32),
         pltpu.VMEM((seq + 8, G), F32), pltpu.VMEM((seq + 8, G), F32)], comm)


TC_A = 256
TC_B = 512
TAPS_A, TAPS_B = 31, 4


def _tiles(T):
    return min(512, T), min(1024, T)


GATHERED = ("w_in", "w_1", "w_a_out", "w_b_out", "w_o", "w_2", "caw", "cbw")
GATHER_KIND = {"w_in": (True, True), "w_1": (True, True), "w_a_out": (False, True), "w_b_out": (False, True),
               "w_o": (False, True), "w_2": (False, True), "caw": (True, False), "cbw": (True, False)}


def _layer_fwd(x, p, seq, cur=None, nxt=None):
    T, D = x.shape
    C, R = p["ln_g"].shape[1], p["lam"].shape[1]
    tm, tl = _tiles(T)
    gb_blk, sa_blk = (2 * C + R) // TC_B, (2 * C + 2 * R) // D
    p, ahead = dict(p), {}

    def gather(src, names):
        return None if src is None else _gather_comm([src[n] for n in names], [GATHER_KIND[n] for n in names])

    def outs(r, src, names, into):
        if src is None:
            return r
        into.update(zip(names, r[1]))
        return r[0]

    mid = ["w_a_out", "w_b_out", "w_o", "caw", "cbw"]
    z, h = outs(_fwd_norm_mm("fwd_z", x, p["g_mix"], p["w_in"], p["b_in"], tl, 1024, comm=gather(cur, mid)), cur, mid, p)
    u1 = outs(_conv_fwd("conv_a_fwd", z, 0, C // TC_A, p["caw"], p["cab"], TAPS_A, seq, TC_A, BF16,
                        comm=gather(cur, ["w_1"])), cur, ["w_1"], p)
    ya = _fwd_ya(u1, p["ln_g"], p["ln_b"], p["w_a_out"], tm)
    v0 = _conv_fwd("conv_b_fwd", z, 2 * C // TC_B, None, p["cbw"], p["cbb"], TAPS_B, seq, TC_B, F32)
    hr, = outs(_rglru_fwd(v0, p["wa"], p["wx"], p["b_rg_a"], p["b_rg_x"], p["lam"], seq, comm=gather(nxt, ["w_in"])),
               nxt, ["w_in"], ahead)
    yb = _fwd_yb(hr, z, gb_blk, p["w_b_out"], tl, TC_B)
    x1 = _fwd_x1(x, ya, yb, z, sa_blk, p["w_o"], tm)
    fp, h2 = outs(_fwd_norm_mm("fwd_f", x1, p["g_mlp"], p["w_1"], None, tl, 1024, comm=gather(cur, ["w_2"])),
                  cur, ["w_2"], p)
    x2 = _fwd_x2(x1, fp, p["w_2"], tm, 2048)
    saved = dict(x=x, z=z, h=h, u1=u1, ya=ya, v0=v0, hr=hr, yb=yb, x1=x1, fp=fp, h2=h2)
    return x2, saved, p, ahead.get("w_in")


class _Reduce:
    def __init__(self, accs, c_arr, kcl_of):
        self.accs, self.c_arr, self.kcl_of, self.pending = accs, c_arr, kcl_of, None

    def chip_sums(self, partials):
        pgs = [a if a.ndim == 4 else a.reshape(N_CHIPS, 2, a.shape[0] // (2 * N_CHIPS), a.shape[1]) for a in partials]
        return [_sum_siblings(a, b, self.c_arr) for a, b in zip(pgs, _swap_halves(pgs))]

    def finish(self, names, sums, received, layer):
        for n, a, b in zip(names, sums, received):
            self.accs[n] = _sum_chips(a, b, self.kcl_of(layer), self.accs[n])


def _layer_bwd(dx2, dx2b, p, s, seq, red=None, layer=0):
    T, D = dx2.shape
    C, R = p["ln_g"].shape[1], p["lam"].shape[1]
    tm, tl = _tiles(T)
    gb_blk, sa_blk = (2 * C + R) // TC_B, (2 * C + 2 * R) // D
    z = s["z"]
    g = {}

    def relu2(a_refs, out_refs, i, j, k):
        f = jnp.maximum(a_refs[0][...].astype(F32), 0.0)
        return (f * f).astype(BF16)

    dfp = _bwd_df(dx2b, p["w_2"], s["fp"], tl, 1024)
    tw = min(2048, T)
    g["w_2"] = _bwd_dw("bwd_dw2", s["fp"], dx2b, 1024, D, tw, a_fn=relu2)
    dx1, dx1b, g["g_mlp"] = _bwd_norm("bwd_dh2", dfp, p["w_1"], s["x1"], p["g_mlp"], dx2, tm, 2048)
    g["w_1"] = _bwd_dw("bwd_dw1", s["h2"], dfp, D // 2, 1024, tw, shard_cols=dfp.shape[1] // N_CHIPS)

    dya, dyb, dz = _bwd_dm(dx1b, p["w_o"], s["ya"], s["yb"], z, sa_blk, tm)

    def merged(a_refs, out_refs, i, j, k):
        ya_, yb_, sa_, sb_ = (r[...].astype(F32) for r in a_refs)
        return (_sig(sa_) * ya_ + _sig(sb_) * yb_).astype(BF16)

    tk = _bs((tm, D), lambda i, j, k: (k, 0))
    g["w_o"] = _bwd_dw("bwd_dwo", s["ya"], dx1b, D, D, tm, a_fn=merged,
                       a_extra=[(s["yb"], tk), (z, _bs((tm, D), lambda i, j, k: (k, sa_blk))),
                                (z, _bs((tm, D), lambda i, j, k: (k, sa_blk + 1)))])

    du1, g["ln_g"], g["ln_b"] = _bwd_du3(dya, p["w_a_out"], s["u1"], p["ln_g"], p["ln_b"], tm)

    def act_a(a_refs, out_refs, i, j, k):
        return _ln_silu(a_refs[0][...].astype(F32), a_refs[1][...], a_refs[2][...])[0].astype(BF16)

    rowc = _bs((1, C), lambda i, j, k: (0, 0))
    g["w_a_out"] = _bwd_dw("bwd_dwa", s["u1"], dya, C, D, tm, a_fn=act_a,
                           a_extra=[(p["ln_g"], rowc), (p["ln_b"], rowc)])
    conv_a_args = ("conv_a_bwd", du1, z, dz, 0, C // TC_A, p["caw"], TAPS_A, seq, TC_A)
    if red is not None and red.pending is not None:
        above, red.pending = red.pending, None
        (dz, g["caw"], g["cab"]), got = _conv_bwd(*conv_a_args, comm=_scatter_comm(above))
        red.finish(["w_in"], above, got, layer + 1)
    else:
        dz, g["caw"], g["cab"] = _conv_bwd(*conv_a_args)

    dhr, dz = _bwd_dp(dyb, p["w_b_out"], s["hr"], z, dz, gb_blk, tl, TC_B)

    def act_b(a_refs, out_refs, i, j, k):
        ge, _ = _gelu(a_refs[1][...].astype(F32))
        return (a_refs[0][...].astype(F32) * ge).astype(BF16)

    tb = min(1024, T)
    g["w_b_out"] = _bwd_dw("bwd_dwb", s["hr"], dyb, TC_B, D, tb, a_fn=act_b,
                           a_extra=[(z, _bs((tb, TC_B), lambda i, j, k: (k, gb_blk + i)))])
    rg_args = (s["v0"], s["hr"], dhr, p["wa"], p["wx"], p["b_rg_a"], p["b_rg_x"], p["lam"], seq)
    if red is not None:
        five = ["w_2", "w_1", "w_o", "w_a_out", "w_b_out"]
        sums = red.chip_sums([g.pop(n) for n in five])
        rg_out, got = _rglru_bwd(*rg_args, comm=_scatter_comm(sums))
        red.finish(five, sums, got, layer)
    else:
        rg_out = _rglru_bwd(*rg_args)
    dv0, g["wa"], g["wx"], g["b_rg_a"], g["b_rg_x"], g["lam"] = rg_out
    dz, g["cbw"], g["cbb"] = _conv_bwd("conv_b_bwd", dv0, z, dz, 2 * C // TC_B, None, p["cbw"], TAPS_B, seq, TC_B)

    dx, dxb, g["g_mix"], dbin = _bwd_norm("bwd_dh", dz, p["w_in"], s["x"], p["g_mix"], dx1, tm, dz.shape[1] // 2,
                                          colsum=True)
    g["b_in"] = dbin.reshape(1, -1)
    ns = dz.shape[1] // N_CHIPS
    g["w_in"] = _bwd_dw("bwd_dwin", s["h"], dz, D // 2, ns // 2, tw, shard_cols=ns)
    if red is not None:
        red.pending = red.chip_sums([g.pop("w_in")])
    return dx, dxb, g


ANY = pl.BlockSpec(memory_space=pl.ANY)


def _mesh_pos():
    return lax.axis_index("x"), lax.axis_index("y"), lax.axis_index("c")


def _other_chips(x, y):
    return [(1 - x, y), (x, 1 - y), (1 - x, 1 - y)]


def _remote(src, dst, ssem, rsem, dev):
    return pltpu.make_async_remote_copy(src_ref=src, dst_ref=dst, send_sem=ssem, recv_sem=rsem,
                                        device_id=dev, device_id_type=MESH)


def _gather_region(src, dst, by_cols, k, half):
    rows, cols = src.shape
    nr = rows if half is None else rows // 2
    r0 = 0 if half is None else half * nr
    if by_cols:
        return dst.at[pl.ds(r0, nr), pl.ds(pl.multiple_of(k * cols, 128), cols)]
    return dst.at[pl.ds(pl.multiple_of(k * rows + r0, 8), nr), :]


def _gather_sends(src, dst, kinds, send, recv):
    x, y, c = _mesh_pos()
    cps = []
    for t in range(len(src)):
        half = c if kinds[t][1] else None
        hr = src[t].shape[0] // 2
        s_ref = src[t].at[pl.ds(c * hr, hr), :] if kinds[t][1] else src[t]
        for j, chip in enumerate(_other_chips(x, y)):
            cps.append(_remote(s_ref, _gather_region(src[t], dst[t], kinds[t][0], 2 * x + y, half),
                               send.at[t, j], recv.at[t, j], (*chip, c)))
    return cps


def _gather_finish(src, dst, kinds, send, recv, fsend, frecv):
    x, y, c = _mesh_pos()
    chips = _other_chips(x, y)
    sib = (x, y, 1 - c)
    n = len(src)
    fwd = []
    for t in range(n):
        half = c if kinds[t][1] else None
        for j, chip in enumerate(chips):
            got = _gather_region(src[t], dst[t], kinds[t][0], 2 * chip[0] + chip[1], half)
            _remote(got, got, send.at[t, j], recv.at[t, j], (*chip, c)).wait_recv()
            if kinds[t][1]:
                cp = _remote(got, got, fsend.at[t, j], frecv.at[t, j], sib)
                cp.start()
                fwd.append(cp)
    for t in range(n):
        if kinds[t][1]:
            for j, chip in enumerate(chips):
                got = _gather_region(src[t], dst[t], kinds[t][0], 2 * chip[0] + chip[1], 1 - c)
                _remote(got, got, fsend.at[t, j], frecv.at[t, j], sib).wait_recv()
    for cp in _gather_sends(src, dst, kinds, send, recv) + fwd:
        cp.wait_send()


def _gather_sems(n):
    sem = pltpu.SemaphoreType.DMA
    return [sem((n, 3)), sem((n, 3)), sem((n, 3)), sem((n, 3))]


def _gather_comm(shards, kinds):
    n = len(shards)

    def whole(s, by_cols):
        return (s.shape[0], N_CHIPS * s.shape[1]) if by_cols else (N_CHIPS * s.shape[0], s.shape[1])

    def own(src, dst, lsem):
        x, y, _ = _mesh_pos()
        return [pltpu.make_async_copy(src[t], _gather_region(src[t], dst[t], kinds[t][0], 2 * x + y, None), lsem.at[t])
                for t in range(n)]

    def start(src, dst, sems):
        for cp in own(src, dst, sems[4]) + _gather_sends(src, dst, kinds, sems[0], sems[1]):
            cp.start()

    def finish(src, dst, sems):
        _gather_finish(src, dst, kinds, *sems[:4])
        for cp in own(src, dst, sems[4]):
            cp.wait()

    return _Comm(shards, [_sds(whole(s, k[0]), s.dtype) for s, k in zip(shards, kinds)],
                 _gather_sems(n) + [pltpu.SemaphoreType.DMA((n,))], start, finish)


def _scatter_comm(ps):
    n = len(ps)

    def copies(src, dst, sems):
        x, y, c = _mesh_pos()
        return [_remote(src[t].at[2 * chip[0] + chip[1]], dst[t].at[j], sems[0].at[t, j], sems[1].at[t, j], (*chip, c))
                for t in range(n) for j, chip in enumerate(_other_chips(x, y))]

    def start(src, dst, sems):
        for cp in copies(src, dst, sems):
            cp.start()

    def finish(src, dst, sems):
        cps = copies(src, dst, sems)
        for cp in cps:
            cp.wait_recv()
        for cp in cps:
            cp.wait_send()

    sem = pltpu.SemaphoreType.DMA
    return _Comm(ps, [_sds((3,) + a.shape[1:], a.dtype) for a in ps], [sem((n, 3)), sem((n, 3))], start, finish)


def _comm_call(name, comm):
    n_i, n_o = len(comm.ins), len(comm.outs)

    def body(*refs):
        comm.start(refs[:n_i], refs[n_i:n_i + n_o], refs[n_i + n_o:])
        comm.finish(refs[:n_i], refs[n_i:n_i + n_o], refs[n_i + n_o:])

    return pl.pallas_call(
        body, name=name, in_specs=[ANY] * n_i, out_specs=[ANY] * n_o, out_shape=comm.outs, scratch_shapes=comm.sems,
        compiler_params=_cp(has_side_effects=True),
    )(*comm.ins)


def _swap_halves(pgs):
    n = len(pgs)

    def body(*refs):
        src, dst = refs[:n], refs[n:2 * n]
        send, recv = refs[2 * n:]
        x, y, c = _mesh_pos()
        cps = [_remote(src[t].at[:, 1 - c], dst[t], send.at[t], recv.at[t], (x, y, 1 - c)) for t in range(n)]
        for cp in cps:
            cp.start()
        for cp in cps:
            cp.wait_recv()
        for cp in cps:
            cp.wait_send()

    sem = pltpu.SemaphoreType.DMA
    return pl.pallas_call(
        body, name="swap_halves", in_specs=[ANY] * n, out_specs=[ANY] * n,
        out_shape=[_sds((a.shape[0],) + a.shape[2:], a.dtype) for a in pgs],
        scratch_shapes=[sem((n,)), sem((n,))], compiler_params=_cp(has_side_effects=True),
    )(*pgs)


def _join_halves(accs):
    n = len(accs)

    def body(*refs):
        buf = refs[n:2 * n]
        send, recv = refs[2 * n:]
        x, y, c = _mesh_pos()
        cps = [_remote(buf[t].at[:, c], buf[t].at[:, c], send.at[t], recv.at[t], (x, y, 1 - c)) for t in range(n)]
        for cp in cps:
            cp.start()
        for t in range(n):
            _remote(buf[t].at[:, c], buf[t].at[:, 1 - c], send.at[t], recv.at[t], (x, y, 1 - c)).wait_recv()
        for cp in cps:
            cp.wait_send()

    sem = pltpu.SemaphoreType.DMA
    return pl.pallas_call(
        body, name="join_halves", in_specs=[ANY] * n, out_specs=[ANY] * n,
        out_shape=[_sds(a.shape, a.dtype) for a in accs], scratch_shapes=[sem((n,)), sem((n,))],
        input_output_aliases={t: t for t in range(n)}, compiler_params=_cp(has_side_effects=True),
    )(*accs)


def _sum_siblings(pg, rb, c_arr):
    nk, _, hr, cols = pg.shape

    def body(c_ref, a_ref, b_ref, o_ref):
        o_ref[...] = (a_ref[...].astype(F32) + b_ref[...].astype(F32)).astype(BF16)

    return pl.pallas_call(
        body, name="sum_siblings",
        grid_spec=pltpu.PrefetchScalarGridSpec(
            num_scalar_prefetch=1, grid=(nk,),
            in_specs=[pl.BlockSpec((None, None, hr, cols), lambda k, c_ref: (k, c_ref[0], 0, 0)),
                      pl.BlockSpec((None, hr, cols), lambda k, c_ref: (k, 0, 0))],
            out_specs=pl.BlockSpec((None, hr, cols), lambda k, c_ref: (k, 0, 0))),
        out_shape=_sds((nk, hr, cols), BF16), compiler_params=_cp(dimension_semantics=("arbitrary",)),
    )(c_arr, pg, rb)


def _sum_chips(p, rb, kcl, acc):
    _, hr, cols = p.shape

    def body(k_ref, a_ref, b_ref, _acc_in, o_ref):
        o_ref[...] = a_ref[...].astype(F32) + b_ref[0].astype(F32) + b_ref[1].astype(F32) + b_ref[2].astype(F32)

    return pl.pallas_call(
        body, name="sum_chips",
        grid_spec=pltpu.PrefetchScalarGridSpec(
            num_scalar_prefetch=1, grid=(1,),
            in_specs=[pl.BlockSpec((None, hr, cols), lambda i, k_ref: (k_ref[0], 0, 0)),
                      pl.BlockSpec((3, hr, cols), lambda i, k_ref: (0, 0, 0)), ANY],
            out_specs=pl.BlockSpec((None, None, hr, cols), lambda i, k_ref: (k_ref[2], k_ref[1], 0, 0))),
        out_shape=_sds(acc.shape, F32), input_output_aliases={3: 0},
        compiler_params=_cp(dimension_semantics=("arbitrary",)),
    )(kcl, p, rb, acc)


N_DEV = 8


def _allreduce_small(part):
    _, r, lanes = part.shape

    def body(p_ref, o_ref, rbuf, s1, r1, s2, r2):
        x, y, c = _mesh_pos()
        me = 4 * x + 2 * y + c
        devs = [(d // 4, (d // 2) % 2, d % 2) for d in range(N_DEV)]
        rbuf[me] = p_ref[me]

        def each_peer(fn):
            for d in range(N_DEV):
                @pl.when(d != me)
                def _():
                    fn(d)

        each_peer(lambda d: _remote(p_ref.at[d], rbuf.at[me], s1.at[d], r1.at[me], devs[d]).start())
        each_peer(lambda d: _remote(p_ref.at[d], rbuf.at[d], s1.at[d], r1.at[d], devs[d]).wait_recv())
        total = rbuf[0]
        for d in range(1, N_DEV):
            total = total + rbuf[d]
        o_ref[me] = total
        each_peer(lambda d: _remote(o_ref.at[me], o_ref.at[me], s2.at[d], r2.at[me], devs[d]).start())
        each_peer(lambda d: _remote(o_ref.at[d], o_ref.at[d], s2.at[d], r2.at[d], devs[d]).wait_recv())
        each_peer(lambda d: _remote(p_ref.at[d], rbuf.at[me], s1.at[d], r1.at[me], devs[d]).wait_send())
        each_peer(lambda d: _remote(o_ref.at[me], o_ref.at[me], s2.at[d], r2.at[me], devs[d]).wait_send())

    sem = pltpu.SemaphoreType.DMA
    vm = pl.BlockSpec(memory_space=pltpu.VMEM)
    return pl.pallas_call(
        body, name="allreduce_small", in_specs=[vm], out_specs=vm, out_shape=_sds(part.shape, F32),
        scratch_shapes=[pltpu.VMEM(part.shape, F32), sem((N_DEV,)), sem((N_DEV,)), sem((N_DEV,)), sem((N_DEV,))],
        compiler_params=_cp(has_side_effects=True),
    )(part)


BIG = ("w_in", "w_1", "w_a_out", "w_b_out", "w_o", "w_2")
BY_COLS = {"w_in": True, "w_1": True, "w_a_out": False, "w_b_out": False, "w_o": False, "w_2": False}
WEIGHTS = ("g_mix", "w_in", "b_in", "conv_a_w", "conv_a_b", "ln_g", "ln_b", "w_a_out", "conv_b_w", "conv_b_b", "w_rg_a",
           "b_rg_a", "w_rg_x", "b_rg_x", "lam", "w_b_out", "w_o", "g_mlp", "w_1", "w_2", "g_final")
SMALL = tuple(n for n in WEIGHTS if n not in BIG)
ADAM_ROWS = 256
ADAM_SMALL_ROWS = 2048


def _block_diag(w):
    nh, dh, _ = w.shape
    ng = nh // HEADS_PER_GROUP
    w4 = w.reshape(ng, HEADS_PER_GROUP, dh, dh)
    eye = jnp.eye(HEADS_PER_GROUP, dtype=w.dtype)
    return jnp.einsum("qhij,hk->qhikj", w4, eye).reshape(ng, HEADS_PER_GROUP * dh, HEADS_PER_GROUP * dh)


def _block_diag_part(d, dh):
    ng = d.shape[0]
    eye = jnp.eye(HEADS_PER_GROUP, dtype=d.dtype)
    d5 = d.reshape(ng, HEADS_PER_GROUP, dh, HEADS_PER_GROUP, dh)
    return jnp.einsum("qhikj,hk->qhij", d5, eye).reshape(ng * HEADS_PER_GROUP, dh, dh)


PACK_LANES = 128


def _pack(arrays, row_multiple):
    parts = [a.reshape(-1, PACK_LANES) for a in arrays]
    parts = [jnp.pad(p, ((0, -p.shape[0] % 8), (0, 0))) if p.shape[0] % 8 else p for p in parts]
    rows = sum(p.shape[0] for p in parts)
    pad = -rows % row_multiple
    if pad:
        parts.append(jnp.zeros((pad, PACK_LANES), parts[0].dtype))
    return jnp.concatenate(parts, axis=0)


def _unpack(buf, like):
    buf = buf.reshape(-1, PACK_LANES)
    out, off = [], 0
    for a in like:
        n = a.size // PACK_LANES
        out.append(buf[off:off + n].reshape(a.shape))
        off += n + (-n % 8)
    return out


def kernel(x, g_mix, w_in, b_in, conv_a_w, conv_a_b, ln_g, ln_b, w_a_out, conv_b_w, conv_b_b, w_rg_a, b_rg_a, w_rg_x, b_rg_x, lam, w_b_out, w_o, g_mlp, w_1, w_2, g_final, loss_target, m_g_mix, m_w_in, m_b_in, m_conv_a_w, m_conv_a_b, m_ln_g, m_ln_b, m_w_a_out, m_conv_b_w, m_conv_b_b, m_w_rg_a, m_b_rg_a, m_w_rg_x, m_b_rg_x, m_lam, m_w_b_out, m_w_o, m_g_mlp, m_w_1, m_w_2, m_g_final, v_g_mix, v_w_in, v_b_in, v_conv_a_w, v_conv_a_b, v_ln_g, v_ln_b, v_w_a_out, v_conv_b_w, v_conv_b_b, v_w_rg_a, v_b_rg_a, v_w_rg_x, v_b_rg_x, v_lam, v_w_b_out, v_w_o, v_g_mlp, v_w_1, v_w_2, v_g_final):
    w = dict(g_mix=g_mix, w_in=w_in, b_in=b_in, conv_a_w=conv_a_w, conv_a_b=conv_a_b, ln_g=ln_g, ln_b=ln_b, w_a_out=w_a_out,
             conv_b_w=conv_b_w, conv_b_b=conv_b_b, w_rg_a=w_rg_a, b_rg_a=b_rg_a, w_rg_x=w_rg_x, b_rg_x=b_rg_x, lam=lam,
             w_b_out=w_b_out, w_o=w_o, g_mlp=g_mlp, w_1=w_1, w_2=w_2, g_final=g_final)
    m = dict(g_mix=m_g_mix, w_in=m_w_in, b_in=m_b_in, conv_a_w=m_conv_a_w, conv_a_b=m_conv_a_b, ln_g=m_ln_g, ln_b=m_ln_b,
             w_a_out=m_w_a_out, conv_b_w=m_conv_b_w, conv_b_b=m_conv_b_b, w_rg_a=m_w_rg_a, b_rg_a=m_b_rg_a, w_rg_x=m_w_rg_x,
             b_rg_x=m_b_rg_x, lam=m_lam, w_b_out=m_w_b_out, w_o=m_w_o, g_mlp=m_g_mlp, w_1=m_w_1, w_2=m_w_2, g_final=m_g_final)
    v = dict(g_mix=v_g_mix, w_in=v_w_in, b_in=v_b_in, conv_a_w=v_conv_a_w, conv_a_b=v_conv_a_b, ln_g=v_ln_g, ln_b=v_ln_b,
             w_a_out=v_w_a_out, conv_b_w=v_conv_b_w, conv_b_b=v_conv_b_b, w_rg_a=v_w_rg_a, b_rg_a=v_b_rg_a, w_rg_x=v_w_rg_x,
             b_rg_x=v_b_rg_x, lam=v_lam, w_b_out=v_w_b_out, w_o=v_w_o, g_mlp=v_g_mlp, w_1=v_w_1, w_2=v_w_2, g_final=v_g_final)
    B, S, D = x.shape
    T = B * S
    L = w_in.shape[0]
    dh = w_rg_a.shape[-1]
    taps_a, taps_b = conv_a_w.shape[1], conv_b_w.shape[1]
    assert (taps_a, taps_b) == (TAPS_A, TAPS_B)
    xi, yi, ci = _mesh_pos()
    c_arr = jnp.reshape(ci, (1,)).astype(jnp.int32)
    k_me = 2 * xi + yi

    caw_p = jnp.pad(conv_a_w, ((0, 0), (0, 32 - taps_a), (0, 0)))
    cbw_p = jnp.pad(conv_b_w, ((0, 0), (0, 8 - taps_b), (0, 0)))
    row = lambda a: a.reshape(1, -1)

    def shards_of(l):
        d = {n: w[n][l].astype(BF16) for n in BIG}
        d.update(caw=caw_p[l], cbw=cbw_p[l])
        return d

    def params_of(l, w_in_whole):
        p = dict(w_in=w_in_whole, cab=row(conv_a_b[l]), cbb=row(conv_b_b[l]),
                 wa=_block_diag(w_rg_a[l]).astype(BF16), wx=_block_diag(w_rg_x[l]).astype(BF16))
        for n in ("g_mix", "b_in", "ln_g", "ln_b", "b_rg_a", "b_rg_x", "lam", "g_mlp"):
            p[n] = row(w[n][l])
        return p

    shards = [shards_of(l) for l in range(L)]
    w_in_whole, = _comm_call("gather_first", _gather_comm([shards[0]["w_in"]], [GATHER_KIND["w_in"]]))
    xf = x.reshape(T, D)
    saved, params = [], []
    for l in range(L):
        xf, s, p, w_in_whole = _layer_fwd(xf, params_of(l, w_in_whole), S, cur=shards[l],
                                          nxt=shards[l + 1] if l + 1 < L else None)
        saved.append(s)
        params.append(p)
    loss_part, dx, dxb, dg_final = _loss_head(xf, row(g_final), loss_target.reshape(T, D), _tiles(T)[0])
    loss = lax.psum(loss_part[0, 0], ("x", "y", "c"))

    half_shape = lambda a: (L, 2, a.shape[1] // 2, a.shape[2])
    accs = {n: lax.empty(half_shape(w[n]), F32) for n in BIG}
    small = {n: [None] * L for n in SMALL if n != "g_final"}
    red = _Reduce(accs, c_arr, lambda l: jnp.stack([k_me, ci, jnp.full((), l, ci.dtype)]).astype(jnp.int32))
    for l in reversed(range(L)):
        dx, dxb, g = _layer_bwd(dx, dxb, params[l], saved[l], S, red=red, layer=l)
        small["g_mix"][l], small["b_in"][l], small["g_mlp"][l] = g["g_mix"], g["b_in"], g["g_mlp"]
        small["conv_a_w"][l], small["conv_a_b"][l] = g["caw"], g["cab"]
        small["conv_b_w"][l], small["conv_b_b"][l] = g["cbw"], g["cbb"]
        small["ln_g"][l], small["ln_b"][l], small["lam"][l] = g["ln_g"], g["ln_b"], g["lam"]
        small["w_rg_a"][l], small["w_rg_x"][l] = _block_diag_part(g["wa"], dh), _block_diag_part(g["wx"], dh)
        small["b_rg_a"][l], small["b_rg_x"][l] = g["b_rg_a"], g["b_rg_x"]
    grad_x = dx.reshape(B, S, D)
    red.finish(["w_in"], red.pending, _comm_call("scatter_chips", _scatter_comm(red.pending)), 0)

    joined = _join_halves([red.accs[n] for n in BIG])
    grads = {n: a.reshape(w[n].shape) for n, a in zip(BIG, joined)}

    names = [n for n in SMALL if n != "g_final"]
    parts = [jnp.stack(small[n]) for n in names] + [dg_final]
    packed = _pack(parts, 8 * N_DEV)
    total = _allreduce_small(packed.reshape(N_DEV, packed.shape[0] // N_DEV, PACK_LANES))
    for n, a in zip(names + ["g_final"], _unpack(total, parts)):
        if n == "conv_a_w":
            a = lax.dynamic_slice_in_dim(a[:, :taps_a], k_me * conv_a_w.shape[2], conv_a_w.shape[2], axis=2)
        elif n == "conv_b_w":
            a = lax.dynamic_slice_in_dim(a[:, :taps_b], k_me * conv_b_w.shape[2], conv_b_w.shape[2], axis=2)
        grads[n] = a.reshape(w[n].shape)

    delta, new_m, new_v = {}, {}, {}
    for n in BIG:
        cols = w[n].shape[-1]
        d_, m_, v_ = _adamw("adamw_" + n, w[n].reshape(-1, cols), grads[n].reshape(-1, cols), m[n].reshape(-1, cols),
                            v[n].reshape(-1, cols), ADAM_ROWS)
        delta[n], new_m[n], new_v[n] = (a.reshape(w[n].shape) for a in (d_, m_, v_))
    for n in SMALL:
        cols = w[n].shape[-1]
        view = lambda a: a.reshape(-1, cols)
        rows = view(w[n]).shape[0]
        d_, m_, v_ = _adamw("adamw_" + n, view(w[n]), view(grads[n]), view(m[n]), view(v[n]),
                            ADAM_SMALL_ROWS if rows % ADAM_SMALL_ROWS == 0 else rows)
        delta[n], new_m[n], new_v[n] = (a.reshape(w[n].shape) for a in (d_, m_, v_))

    return (loss, grad_x, *[grads[n] for n in WEIGHTS], *[delta[n] for n in WEIGHTS],
            *[new_m[n] for n in WEIGHTS], *[new_v[n] for n in WEIGHTS])
```

```python
import jax
import jax.numpy as jnp
from jax import lax
from jax.experimental import pallas as pl
from jax.experimental.pallas import tpu as pltpu

F32 = jnp.float32
BF16 = jnp.bfloat16
MESH = pl.DeviceIdType.MESH

EPS = 1e-6
LRU_C = 8.0
ADAM_LR, ADAM_B1, ADAM_B2, ADAM_EPS, ADAM_WD, ADAM_STEP = 0.001, 0.9, 0.999, 1e-08, 0.01, 10

N_CHIPS = 4
HEADS_PER_GROUP = 4
VMEM_LIMIT = 56 * 1024 * 1024


def _cp(**kw):
    return pltpu.CompilerParams(vmem_limit_bytes=VMEM_LIMIT, **kw)


def _sig(x):
    return 1.0 / (1.0 + jnp.exp(-x))


def _gelu(x):
    t = jnp.tanh(0.7978845608028654 * (x + 0.044715 * x * x * x))
    return 0.5 * x * (1.0 + t), t


def _gelu_grad(x, t):
    dt = (1.0 - t * t) * 0.7978845608028654 * (1.0 + 3.0 * 0.044715 * x * x)
    return 0.5 * (1.0 + t) + 0.5 * x * dt


def _rms(xf, g):
    r = lax.rsqrt(jnp.mean(xf * xf, axis=-1, keepdims=True) + EPS)
    return xf * r * g, r


def _rms_bwd(xf, g, r, dh):
    dyg = dh * g
    dx = r * (dyg - xf * (r * r) * jnp.mean(dyg * xf, axis=-1, keepdims=True))
    return dx, dh * xf * r


def _ln_silu(u, g, b):
    mu = jnp.mean(u, axis=-1, keepdims=True)
    uc = u - mu
    rstd = lax.rsqrt(jnp.mean(uc * uc, axis=-1, keepdims=True) + EPS)
    uh = uc * rstd
    u2 = uh * g + b
    s = _sig(u2)
    return u2 * s, uh, rstd, u2, s


_DIMS = {"nn": (((1,), (0,)), ((), ())), "nt": (((1,), (1,)), ((), ())), "tn": (((0,), (0,)), ((), ()))}


class _Comm:
    def __init__(self, ins, outs, sems, start, finish):
        self.ins, self.outs, self.sems, self.start, self.finish = list(ins), list(outs), list(sems), start, finish


def _resident(shape):
    return pl.BlockSpec(shape, lambda i, j, k: (0,) * len(shape), pipeline_mode=pl.Buffered(1))


def _mm(name, mode, grid, a_ins, a_fn, b_in, e_ins, epi, outs, acc_shape, cache_a=None, alias=(), extra_scratch=(),
        comm=None, b_slice=None):
    ni, nj, nk = grid
    na, ne, no = len(a_ins), len(e_ins), len(outs)
    assert cache_a is None or nk == 1
    n_fixed = (nk > 1) + (cache_a is not None)
    n_in = na + 1 + ne + len(alias)
    c_ins, c_outs, c_sems = (comm.ins, comm.outs, comm.sems) if comm else ([], [], [])

    def body(*refs):
        a_refs = refs[:na]
        b_ref = refs[na]
        e_refs = refs[na + 1:na + 1 + ne]
        comm_in = refs[n_in:n_in + len(c_ins)]
        out0 = n_in + len(c_ins)
        out_refs = refs[out0:out0 + no]
        comm_out = refs[out0 + no:out0 + no + len(c_outs)]
        scratch = refs[out0 + no + len(c_outs):]
        extra = scratch[n_fixed:n_fixed + len(extra_scratch)]
        comm_sems = scratch[n_fixed + len(extra_scratch):]
        i, j, k = pl.program_id(0), pl.program_id(1), pl.program_id(2)
        if comm:
            @pl.when((i == 0) & (j == 0) & (k == 0))
            def _():
                comm.start(comm_in, comm_out, comm_sems)
        if cache_a is not None:
            cache_ref = scratch[n_fixed - 1]

            @pl.when(j == 0)
            def _():
                cache_ref[...] = a_fn(a_refs, out_refs, i, j, k)

            a = cache_ref[...]
        else:
            a = a_fn(a_refs, out_refs, i, j, k)
        if b_slice is None:
            b = b_ref[...]
        elif b_slice[0] == "cols":
            b = b_ref[:, pl.ds(pl.multiple_of(j * b_slice[1], b_slice[1]), b_slice[1])]
        else:
            b = b_ref[pl.ds(pl.multiple_of(j * b_slice[1], b_slice[1]), b_slice[1]), :]
        prod = lax.dot_general(a, b, _DIMS[mode], preferred_element_type=F32)
        if nk == 1:
            epi(prod, e_refs, out_refs, i, j, extra)
        else:
            acc_ref = scratch[0]

            @pl.when(k == 0)
            def _():
                acc_ref[...] = prod

            @pl.when(k > 0)
            def _():
                acc_ref[...] += prod

            @pl.when(k == nk - 1)
            def _():
                epi(acc_ref[...], e_refs, out_refs, i, j, extra)

        if comm:
            @pl.when((i == ni - 1) & (j == nj - 1) & (k == nk - 1))
            def _():
                comm.finish(comm_in, comm_out, comm_sems)

    scratch_shapes = []
    if nk > 1:
        scratch_shapes.append(pltpu.VMEM(acc_shape, F32))
    if cache_a is not None:
        scratch_shapes.append(pltpu.VMEM(cache_a, BF16))
    any_spec = pl.BlockSpec(memory_space=pl.ANY)
    ins = (list(a_ins) + [b_in] + list(e_ins) + [(arr, any_spec) for arr, _ in alias] + [(arr, any_spec) for arr in c_ins])
    first_alias = na + 1 + ne
    res = pl.pallas_call(
        body, name=name, grid=grid,
        in_specs=[s for _, s in ins], out_specs=[s for _, s in outs] + [any_spec] * len(c_outs),
        out_shape=[o for o, _ in outs] + list(c_outs),
        scratch_shapes=scratch_shapes + list(extra_scratch) + list(c_sems),
        input_output_aliases={first_alias + n: o for n, (_, o) in enumerate(alias)},
        compiler_params=_cp(dimension_semantics=("arbitrary", "arbitrary", "arbitrary"), has_side_effects=bool(comm)),
    )(*[a for a, _ in ins])
    if comm:
        return list(res[:no]), list(res[no:])
    return res


def _bs(shape, fn):
    return pl.BlockSpec(shape, fn)


def _sds(shape, dt):
    return jax.ShapeDtypeStruct(shape, dt)


def _acc_rows(ref, val, first):
    @pl.when(first)
    def _():
        ref[...] = val

    @pl.when(jnp.logical_not(first))
    def _():
        ref[...] += val


def _fwd_norm_mm(name, x, g, w, bias, tm, tn, comm=None):
    T, D = x.shape
    N = w.shape[1]

    def a_fn(a_refs, out_refs, i, j, k):
        h, _ = _rms(a_refs[0][...], a_refs[1][...])
        hb = h.astype(BF16)
        out_refs[1][...] = hb
        return hb

    def epi(acc, e_refs, out_refs, i, j, extra):
        if bias is not None:
            acc = acc + e_refs[0][...]
        out_refs[0][...] = acc.astype(BF16)

    e_ins = [] if bias is None else [(bias, _bs((1, tn), lambda i, j, k: (0, j)))]
    return _mm(name, "nn", (T // tm, N // tn, 1),
               [(x, _bs((tm, D), lambda i, j, k: (i, 0))), (g, _bs((1, D), lambda i, j, k: (0, 0)))], a_fn,
               (w, _resident((D, N))), e_ins, epi,
               [(_sds((T, N), BF16), _bs((tm, tn), lambda i, j, k: (i, j))),
                (_sds((T, D), BF16), _bs((tm, D), lambda i, j, k: (i, 0)))],
               None, cache_a=(tm, D), comm=comm, b_slice=("cols", tn))


def _fwd_ya(u1, ln_g, ln_b, w, tm):
    T, C = u1.shape
    N = w.shape[1]

    def a_fn(a_refs, out_refs, i, j, k):
        u3 = _ln_silu(a_refs[0][...].astype(F32), a_refs[1][...], a_refs[2][...])[0]
        return u3.astype(BF16)

    def epi(acc, e_refs, out_refs, i, j, extra):
        out_refs[0][...] = acc.astype(BF16)

    row = _bs((1, C), lambda i, j, k: (0, 0))
    return _mm("fwd_ya", "nn", (T // tm, 1, 1),
               [(u1, _bs((tm, C), lambda i, j, k: (i, 0))), (ln_g, row), (ln_b, row)], a_fn,
               (w, _bs((C, N), lambda i, j, k: (0, 0))), [], epi,
               [(_sds((T, N), BF16), _bs((tm, N), lambda i, j, k: (i, 0)))], None)[0]


def _fwd_yb(h, z, gb_blk, w, tm, tk):
    T, C = h.shape
    N = w.shape[1]

    def a_fn(a_refs, out_refs, i, j, k):
        ge, _ = _gelu(a_refs[1][...].astype(F32))
        return (a_refs[0][...].astype(F32) * ge).astype(BF16)

    def epi(acc, e_refs, out_refs, i, j, extra):
        out_refs[0][...] = acc.astype(BF16)

    return _mm("fwd_yb", "nn", (T // tm, 1, C // tk),
               [(h, _bs((tm, tk), lambda i, j, k: (i, k))), (z, _bs((tm, tk), lambda i, j, k: (i, gb_blk + k)))], a_fn,
               (w, _bs((tk, N), lambda i, j, k: (k, 0))), [], epi,
               [(_sds((T, N), BF16), _bs((tm, N), lambda i, j, k: (i, 0)))], (tm, N))[0]


def _fwd_x1(x, ya, yb, z, sa_blk, w, tm):
    T, D = x.shape

    def a_fn(a_refs, out_refs, i, j, k):
        ya_, yb_, sa_, sb_ = (r[...].astype(F32) for r in a_refs)
        return (_sig(sa_) * ya_ + _sig(sb_) * yb_).astype(BF16)

    def epi(acc, e_refs, out_refs, i, j, extra):
        out_refs[0][...] = e_refs[0][...] + acc

    t = _bs((tm, D), lambda i, j, k: (i, 0))
    return _mm("fwd_x1", "nn", (T // tm, 1, 1),
               [(ya, t), (yb, t), (z, _bs((tm, D), lambda i, j, k: (i, sa_blk))),
                (z, _bs((tm, D), lambda i, j, k: (i, sa_blk + 1)))], a_fn,
               (w, _bs((D, D), lambda i, j, k: (0, 0))), [(x, t)], epi,
               [(_sds((T, D), F32), t)], None)[0]


def _fwd_x2(x1, fp, w, tm, tk, comm=None):
    T, D = x1.shape
    Fd = fp.shape[1]

    def epi(acc, e_refs, out_refs, i, j, extra):
        out_refs[0][...] = e_refs[0][...] + acc

    t = _bs((tm, D), lambda i, j, k: (i, 0))
    whole_k = tk == Fd
    r = _mm("fwd_x2", "nn", (T // tm, 1, Fd // tk),
            [(fp, _bs((tm, tk), lambda i, j, k: (i, k)))], _relu2,
            (w, _resident((Fd, D)) if whole_k else _bs((tk, D), lambda i, j, k: (k, 0))), [(x1, t)], epi,
            [(_sds((T, D), F32), t)], (tm, D), comm=comm)
    return (r[0][0], r[1]) if comm else r[0]


def _relu2(a_refs, out_refs, i, j, k):
    f = jnp.maximum(a_refs[0][...], 0.0)
    return f * f


def _loss_head(x, g, target, tm):
    T, D = x.shape

    def body(x_ref, g_ref, t_ref, loss_ref, dx_ref, dxb_ref, dg_ref):
        i = pl.program_id(0)
        xf, gv = x_ref[...], g_ref[...]
        y, r = _rms(xf, gv)
        err = y - t_ref[...]
        part = 0.5 * jnp.sum(jnp.mean(err * err, axis=-1, keepdims=True), axis=0, keepdims=True)
        dx, dg_rows = _rms_bwd(xf, gv, r, err * (1.0 / D))
        dx_ref[...] = dx
        dxb_ref[...] = dx.astype(BF16)
        _acc_rows(loss_ref, jnp.broadcast_to(part, (1, 128)), i == 0)
        _acc_rows(dg_ref, jnp.sum(dg_rows, axis=0, keepdims=True), i == 0)

    t = _bs((tm, D), lambda i: (i, 0))
    row = _bs((1, D), lambda i: (0, 0))
    return pl.pallas_call(
        body, name="loss_head", grid=(T // tm,), in_specs=[t, row, t],
        out_specs=[_bs((1, 128), lambda i: (0, 0)), t, t, row],
        out_shape=[_sds((1, 128), F32), _sds((T, D), F32), _sds((T, D), BF16), _sds((1, D), F32)],
        compiler_params=_cp(dimension_semantics=("arbitrary",)),
    )(x, g, target)


def _adamw(name, w, g, m, v, tr):
    rows, cols = w.shape
    d1 = 1.0 - ADAM_B1 ** ADAM_STEP
    d2 = 1.0 - ADAM_B2 ** ADAM_STEP

    def body(w_ref, g_ref, m_ref, v_ref, d_ref, mo_ref, vo_ref):
        gv = g_ref[...]
        mn = ADAM_B1 * m_ref[...] + (1.0 - ADAM_B1) * gv
        vn = ADAM_B2 * v_ref[...] + (1.0 - ADAM_B2) * (gv * gv)
        d_ref[...] = -ADAM_LR * ((mn / d1) / (jnp.sqrt(vn / d2) + ADAM_EPS) + ADAM_WD * w_ref[...])
        mo_ref[...] = mn
        vo_ref[...] = vn

    t = _bs((tr, cols), lambda i: (i, 0))
    return pl.pallas_call(
        body, name=name, grid=(rows // tr,), in_specs=[t] * 4, out_specs=[t] * 3,
        out_shape=[_sds((rows, cols), F32)] * 3,
        compiler_params=_cp(dimension_semantics=("arbitrary",)),
    )(w, g, m, v)


def _ident(a_refs, out_refs, i, j, k):
    return a_refs[0][...]


def _bwd_dw(name, act, dy, ti, tj, tm, a_fn=None, a_extra=(), shard_cols=None, keep=None):
    T, J = dy.shape
    I = act.shape[1]

    def epi(acc, e_refs, out_refs, i, j, extra):
        out_refs[0][...] = acc.astype(BF16).reshape(out_refs[0].shape)

    if shard_cols is None:
        out = (_sds((I, J), BF16), _bs((ti, tj), lambda i, j, k: (i, j)))
    else:
        per = shard_cols // tj
        assert ti == I and per * tj == shard_cols
        out = (_sds((J // shard_cols, 2, I // 2, shard_cols), BF16),
               _bs((None, 2, I // 2, tj), lambda i, j, k: (lax.div(j, per), 0, 0, lax.rem(j, per))))
    assert keep is None or tm == T
    a_spec = _resident((T, I)) if keep == "act" else _bs((tm, ti), lambda i, j, k: (k, i))
    b_spec = _resident((T, J)) if keep == "dy" else _bs((tm, tj), lambda i, j, k: (k, j))
    return _mm(name, "tn", (I // ti, J // tj, T // tm), [(act, a_spec)] + list(a_extra), a_fn or _ident,
               (dy, b_spec), [], epi, [out], (ti, tj))[0]


def _bwd_df(dxb, w2, fp, tm, tn):
    T, D = dxb.shape
    Fd = w2.shape[0]

    def epi(acc, e_refs, out_refs, i, j, extra):
        out_refs[0][...] = (acc * (2.0 * jnp.maximum(e_refs[0][...].astype(F32), 0.0))).astype(BF16)

    t = _bs((tm, tn), lambda i, j, k: (i, j))
    return _mm("bwd_df", "nt", (T // tm, Fd // tn, 1), [(dxb, _bs((tm, D), lambda i, j, k: (i, 0)))], _ident,
               (w2, _resident((Fd, D))), [(fp, t)], epi, [(_sds((T, Fd), BF16), t)], None, b_slice=("rows", tn))[0]


def _bwd_norm(name, dy, w, x, g, dres, tm, tk, colsum=False, comm=None):
    T, K = dy.shape
    D = w.shape[0]
    nk = K // tk

    def a_fn(a_refs, out_refs, i, j, k):
        a = a_refs[0][...]
        if colsum:
            s = jnp.sum(a.astype(F32), axis=0, keepdims=True)

            @pl.when(i == 0)
            def _():
                out_refs[3][k] = s

            @pl.when(i > 0)
            def _():
                out_refs[3][k] += s
        return a

    def epi(acc, e_refs, out_refs, i, j, extra):
        xf, gv = e_refs[0][...], e_refs[1][...]
        r = lax.rsqrt(jnp.mean(xf * xf, axis=-1, keepdims=True) + EPS)
        dx, dg_rows = _rms_bwd(xf, gv, r, acc)
        dx = dx + e_refs[2][...]
        out_refs[0][...] = dx
        out_refs[1][...] = dx.astype(BF16)
        _acc_rows(out_refs[2], jnp.sum(dg_rows, axis=0, keepdims=True), i == 0)

    t = _bs((tm, D), lambda i, j, k: (i, 0))
    row = _bs((1, D), lambda i, j, k: (0, 0))
    outs = [(_sds((T, D), F32), t), (_sds((T, D), BF16), t), (_sds((1, D), F32), row)]
    if colsum:
        outs.append((_sds((nk, 1, tk), F32), _bs((nk, 1, tk), lambda i, j, k: (0, 0, 0))))
    return _mm(name, "nt", (T // tm, 1, nk), [(dy, _bs((tm, tk), lambda i, j, k: (i, k)))], a_fn,
               (w, _resident((D, K)) if nk == 1 else _bs((D, tk), lambda i, j, k: (0, k))),
               [(x, t), (g, row), (dres, t)], epi, outs, (tm, D), comm=comm)


def _bwd_dm(dxb, w_o, ya, yb, z, sa_blk, tm):
    T, D = dxb.shape

    def epi(acc, e_refs, out_refs, i, j, extra):
        ya_, yb_, sa_, sb_ = (r[...].astype(F32) for r in e_refs)
        ga, gb = _sig(sa_), _sig(sb_)
        out_refs[0][...] = (acc * ga).astype(BF16)
        out_refs[1][...] = (acc * gb).astype(BF16)
        stage = extra[0]
        stage[:, 0:D] = (acc * ya_ * ga * (1.0 - ga)).astype(BF16)
        stage[:, D:2 * D] = (acc * yb_ * gb * (1.0 - gb)).astype(BF16)
        pltpu.sync_copy(stage, out_refs[2].at[pl.ds(pl.multiple_of(i * tm, tm), tm), pl.ds(sa_blk * D, 2 * D)])

    t = _bs((tm, D), lambda i, j, k: (i, 0))
    return _mm("bwd_dm", "nt", (T // tm, 1, 1), [(dxb, t)], _ident, (w_o, _bs((D, D), lambda i, j, k: (0, 0))),
               [(ya, t), (yb, t), (z, _bs((tm, D), lambda i, j, k: (i, sa_blk))),
                (z, _bs((tm, D), lambda i, j, k: (i, sa_blk + 1)))], epi,
               [(_sds((T, D), BF16), t), (_sds((T, D), BF16), t),
                (_sds(z.shape, BF16), pl.BlockSpec(memory_space=pl.ANY))], None,
               extra_scratch=[pltpu.VMEM((tm, 2 * D), BF16)])


def _bwd_du3(dya, w, u1, ln_g, ln_b, tm):
    T, D = dya.shape
    C = w.shape[0]

    def epi(acc, e_refs, out_refs, i, j, extra):
        gv = e_refs[1][...]
        _, uh, rstd, u2, s = _ln_silu(e_refs[0][...].astype(F32), gv, e_refs[2][...])
        du2 = acc * (s * (1.0 + u2 * (1.0 - s)))
        duh = du2 * gv
        out_refs[0][...] = rstd * (duh - jnp.mean(duh, axis=-1, keepdims=True)
                                   - uh * jnp.mean(duh * uh, axis=-1, keepdims=True))
        _acc_rows(out_refs[1], jnp.sum(du2 * uh, axis=0, keepdims=True), i == 0)
        _acc_rows(out_refs[2], jnp.sum(du2, axis=0, keepdims=True), i == 0)

    t = _bs((tm, C), lambda i, j, k: (i, 0))
    row = _bs((1, C), lambda i, j, k: (0, 0))
    return _mm("bwd_du3", "nt", (T // tm, 1, 1), [(dya, _bs((tm, D), lambda i, j, k: (i, 0)))], _ident,
               (w, _bs((C, D), lambda i, j, k: (0, 0))), [(u1, t), (ln_g, row), (ln_b, row)], epi,
               [(_sds((T, C), F32), t), (_sds((1, C), F32), row), (_sds((1, C), F32), row)], None)


def _bwd_dp(dyb, w, h, z, dz, gb_blk, tm, tn):
    T, D = dyb.shape
    R = w.shape[0]

    def epi(acc, e_refs, out_refs, i, j, extra):
        gbv = e_refs[1][...].astype(F32)
        ge, th = _gelu(gbv)
        out_refs[0][...] = acc * ge
        out_refs[1][...] = (acc * e_refs[0][...].astype(F32) * _gelu_grad(gbv, th)).astype(BF16)

    t = _bs((tm, tn), lambda i, j, k: (i, j))
    tz = _bs((tm, tn), lambda i, j, k: (i, gb_blk + j))
    return _mm("bwd_dp", "nt", (T // tm, R // tn, 1), [(dyb, _bs((tm, D), lambda i, j, k: (i, 0)))], _ident,
               (w, _bs((tn, D), lambda i, j, k: (j, 0))), [(h, t), (z, tz)], epi,
               [(_sds((T, R), F32), t), (_sds(dz.shape, BF16), tz)], None, cache_a=None, alias=[(dz, 1)])


CONV_ROWS = 32


def _shifted_taps(x, halo, shifts, fn):
    n = CONV_ROWS + halo
    by_r = {}
    for k, s in shifts:
        by_r.setdefault(s % 8, []).append((k, s))
    for r in sorted(by_r):
        xr = x if r == 0 else pltpu.roll(x, n - r, 0)
        for k, s in by_r[r]:
            q = s - r
            fn(k, xr[q:q + CONV_ROWS])


def _conv_fwd(name, z, blk0, gate_blk0, w_pad, bias, taps, seq, tc, out_dtype, comm=None):
    T = z.shape[0]
    C = w_pad.shape[1]
    nb, nj = T // seq, C // tc
    pad = 8 * ((taps - 1 + 7) // 8)
    halo = pad
    shifts = [(k, pad - (taps - 1) + k) for k in range(taps)]
    glu = gate_blk0 is not None

    def body(*refs):
        if glu:
            v_ref, g_ref, w_ref, b_ref, o_ref, p_ref = refs
        else:
            v_ref, w_ref, b_ref, o_ref, p_ref = refs
        p_ref[pl.ds(0, pad), :] = jnp.zeros((pad, tc), F32)
        u = v_ref[...].astype(F32)
        if glu:
            u = u * _sig(g_ref[...].astype(F32))
        p_ref[pl.ds(pad, seq), :] = u

        def step(c, _):
            base = pl.multiple_of(c * CONV_ROWS, CONV_ROWS)
            x = p_ref[pl.ds(base, CONV_ROWS + halo), :]
            acc = [jnp.zeros((CONV_ROWS, tc), F32) + b_ref[...]]

            def tap(k, xs):
                acc[0] = acc[0] + w_ref[k:k + 1, :] * xs

            _shifted_taps(x, halo, shifts, tap)
            o_ref[pl.ds(base, CONV_ROWS), :] = acc[0].astype(out_dtype)
            return 0

        lax.fori_loop(0, seq // CONV_ROWS, step, 0)

    zin = [(z, _bs((seq, tc), lambda b, j: (b, blk0 + j)))]
    if glu:
        zin.append((z, _bs((seq, tc), lambda b, j: (b, gate_blk0 + j))))
    ins = zin + [(w_pad, _bs((w_pad.shape[0], tc), lambda b, j: (0, j))), (bias, _bs((1, tc), lambda b, j: (0, j)))]
    r = _call_with_comm(name, body, (nb, nj), [a for a, _ in ins], [s for _, s in ins],
                        [_bs((seq, tc), lambda b, j: (b, j))], [_sds((T, C), out_dtype)],
                        [pltpu.VMEM((seq + pad, tc), F32)], comm)
    return (r[0][0], r[1]) if comm else r[0]


def _conv_bwd(name, dy, z, dz, blk0, gate_blk0, w_pad, taps, seq, tc, comm=None):
    T = z.shape[0]
    C = w_pad.shape[1]
    nb, nj = T // seq, C // tc
    kp = w_pad.shape[0]
    pad = 8 * ((taps - 1 + 7) // 8)
    halo = pad
    sh_du = [(k, taps - 1 - k) for k in range(taps)]
    sh_dw = [(k, pad - (taps - 1) + k) for k in range(taps)]
    glu = gate_blk0 is not None

    def body(*refs):
        if glu:
            dy_ref, v_ref, g_ref, w_ref, _dz_in, dz_out, dw_ref, db_ref, pdy, pu, du_s, wacc, ob, ob2 = refs
        else:
            dy_ref, v_ref, w_ref, _dz_in, dz_out, dw_ref, db_ref, pdy, pu, du_s, wacc, ob = refs
        j = pl.program_id(0)
        b = pl.program_id(1)
        pdy[pl.ds(seq, pad), :] = jnp.zeros((pad, tc), F32)
        pdy[pl.ds(0, seq), :] = dy_ref[...].astype(F32)
        pu[pl.ds(0, pad), :] = jnp.zeros((pad, tc), F32)
        v = v_ref[...].astype(F32)
        if glu:
            sg = _sig(g_ref[...].astype(F32))
            pu[pl.ds(pad, seq), :] = v * sg
        else:
            pu[pl.ds(pad, seq), :] = v
        wacc[...] = jnp.zeros(wacc.shape, F32)

        def step(c, dbacc):
            base = pl.multiple_of(c * CONV_ROWS, CONV_ROWS)
            xdy = pdy[pl.ds(base, CONV_ROWS + halo), :]
            acc = [jnp.zeros((CONV_ROWS, tc), F32)]

            def tap(k, xs):
                acc[0] = acc[0] + w_ref[k:k + 1, :] * xs

            _shifted_taps(xdy, halo, sh_du, tap)
            du_s[pl.ds(base, CONV_ROWS), :] = acc[0]
            dyc = xdy[0:CONV_ROWS]
            xu = pu[pl.ds(base, CONV_ROWS + halo), :]

            def wtap(k, xs):
                p = dyc * xs
                s8 = p[0:8]
                for m in range(1, CONV_ROWS // 8):
                    s8 = s8 + p[8 * m:8 * m + 8]
                wacc[pl.ds(8 * k, 8), :] += s8

            _shifted_taps(xu, halo, sh_dw, wtap)
            d8 = dyc[0:8]
            for m in range(1, CONV_ROWS // 8):
                d8 = d8 + dyc[8 * m:8 * m + 8]
            return dbacc + d8

        dbacc = lax.fori_loop(0, seq // CONV_ROWS, step, jnp.zeros((8, tc), F32))
        du = du_s[...]
        rows = pl.ds(pl.multiple_of(b * seq, seq), seq)
        if glu:
            ob[...] = (du * sg).astype(BF16)
            ob2[...] = (du * v * sg * (1.0 - sg)).astype(BF16)
            pltpu.sync_copy(ob2, dz_out.at[rows, pl.ds(pl.multiple_of((gate_blk0 + j) * tc, tc), tc)])
        else:
            ob[...] = du.astype(BF16)
        pltpu.sync_copy(ob, dz_out.at[rows, pl.ds(pl.multiple_of((blk0 + j) * tc, tc), tc)])
        dw = jnp.sum(wacc[...].reshape(kp, 8, tc), axis=1)
        _acc_rows(dw_ref, dw, b == 0)
        _acc_rows(db_ref, jnp.sum(dbacc, axis=0, keepdims=True), b == 0)

    zin = [(z, _bs((seq, tc), lambda j, b: (b, blk0 + j)))]
    if glu:
        zin.append((z, _bs((seq, tc), lambda j, b: (b, gate_blk0 + j))))
    ins = [(dy, _bs((seq, tc), lambda j, b: (b, j)))] + zin + [(w_pad, _bs((kp, tc), lambda j, b: (0, j))),
                                                               (dz, pl.BlockSpec(memory_space=pl.ANY))]
    dz_idx = len(ins) - 1
    out_specs = [pl.BlockSpec(memory_space=pl.ANY), _bs((kp, tc), lambda j, b: (0, j)), _bs((1, tc), lambda j, b: (0, j))]
    out_shape = [_sds(dz.shape, dz.dtype), _sds((kp, C), F32), _sds((1, C), F32)]
    stage = [pltpu.VMEM((seq, tc), BF16)] * (2 if glu else 1)
    return _call_with_comm(
        name, body, (nj, nb), [a for a, _ in ins], [s for _, s in ins], out_specs, out_shape,
        [pltpu.VMEM((seq + pad, tc), F32), pltpu.VMEM((seq + pad, tc), F32),
         pltpu.VMEM((seq, tc), F32), pltpu.VMEM((8 * kp, tc), F32)] + stage, comm, aliases={dz_idx: 0})


RG_ROWS = 256


def _softplus_neg(lam):
    return jnp.maximum(-lam, 0.0) + jnp.log(1.0 + jnp.exp(-jnp.abs(lam)))


def _gates(v0c, wa_ref, wx_ref, ba, bx, sp):
    vb = v0c.astype(BF16)
    r = _sig(jnp.dot(vb, wa_ref[...], preferred_element_type=F32) + ba)
    i = _sig(jnp.dot(vb, wx_ref[...], preferred_element_type=F32) + bx)
    return r, i, -LRU_C * r * sp


def _decay(la, first_row):
    a = jnp.exp(la)
    a2 = a * a
    x = 2.0 * la
    series = -x * (1.0 + x * (1.0 / 2) * (1.0 + x * (1.0 / 3) * (1.0 + x * (1.0 / 4) * (1.0 + x * (1.0 / 5)))))
    mult = jnp.sqrt(jnp.where(x > -0.1, series, 1.0 - a2))
    dmult = jnp.where(first_row, 0.0, -a2 / mult)
    mult = jnp.where(first_row, 1.0, mult)
    return a, mult, dmult


def _group_scan(a, b, reverse):
    n = a.shape[0]
    row = lax.broadcasted_iota(jnp.int32, a.shape, 0) & 7
    for d in (1, 2, 4):
        sh = n - d if reverse else d
        a_s, b_s = pltpu.roll(a, sh, 0), pltpu.roll(b, sh, 0)
        m = (row < 8 - d) if reverse else (row >= d)
        b = jnp.where(m, a * b_s + b, b)
        a = jnp.where(m, a * a_s, a)
    return a, b


def _group_carry(a_s, b_s, o_s, n_groups, reverse):
    cols = a_s.shape[1]

    def step(g, carry):
        g = n_groups - 1 - g if reverse else g
        rows = pl.ds(pl.multiple_of(g * 8, 8), 8)
        o = a_s[rows, :] * carry + b_s[rows, :]
        o_s[rows, :] = o
        return o[0:1, :] if reverse else o[7:8, :]

    lax.fori_loop(0, n_groups, step, jnp.zeros((1, cols), F32))


def _rglru_fwd(v0, wa, wx, ba, bx, lam, seq, comm=None):
    T, C = v0.shape
    ng, G = wa.shape[0], wa.shape[1]
    nb = T // seq

    def body(v_ref, wa_ref, wx_ref, ba_ref, bx_ref, lam_ref, h_ref, r_ref, i_ref, la_ref, a_s, b_s, h_s):
        sp = _softplus_neg(lam_ref[...])

        def chunk(c, _):
            rows = pl.ds(pl.multiple_of(c * RG_ROWS, RG_ROWS), RG_ROWS)
            t = lax.broadcasted_iota(jnp.int32, (RG_ROWS, G), 0) + c * RG_ROWS
            v0c = v_ref[rows, :]
            r, i, la = _gates(v0c, wa_ref, wx_ref, ba_ref[...], bx_ref[...], sp)
            r_ref[rows, :] = r.astype(BF16)
            i_ref[rows, :] = i.astype(BF16)
            la_ref[rows, :] = la
            a, mult, _ = _decay(la, t == 0)
            a_g, b_g = _group_scan(a, mult * i * v0c, False)
            a_s[rows, :] = a_g
            b_s[rows, :] = b_g
            return 0

        lax.fori_loop(0, seq // RG_ROWS, chunk, 0)
        _group_carry(a_s, b_s, h_s, seq // 8, False)
        h_ref[...] = h_s[...].astype(BF16)

    t2 = _bs((seq, G), lambda b, g: (b, g))
    wsp = _bs((None, G, G), lambda b, g: (g, 0, 0))
    row = _bs((1, G), lambda b, g: (0, g))
    return _call_with_comm("rglru_fwd", body, (nb, ng), [v0, wa, wx, ba, bx, lam], [t2, wsp, wsp, row, row, row],
                           [t2] * 4, [_sds((T, C), BF16)] * 3 + [_sds((T, C), F32)], [pltpu.VMEM((seq, G), F32)] * 3, comm)


def _call_with_comm(name, body, grid, ins, in_specs, out_specs, out_shape, scratch, comm, aliases=None):
    n_in, n_out, n_s = len(ins), len(out_shape), len(scratch)
    c_ins, c_outs, c_sems = (comm.ins, comm.outs, comm.sems) if comm else ([], [], [])

    def wrapped(*refs):
        o0 = n_in + len(c_ins)
        s0 = o0 + n_out + len(c_outs)
        cin, cout, csem = refs[n_in:o0], refs[o0 + n_out:s0], refs[s0 + n_s:]
        ids = [pl.program_id(a) for a in range(len(grid))]
        if comm:
            first = _all_of([i == 0 for i in ids])

            @pl.when(first)
            def _():
                comm.start(cin, cout, csem)

        body(*refs[:n_in], *refs[o0:o0 + n_out], *refs[s0:s0 + n_s])
        if comm:
            last = _all_of([i == n - 1 for i, n in zip(ids, grid)])

            @pl.when(last)
            def _():
                comm.finish(cin, cout, csem)

    res = pl.pallas_call(
        wrapped, name=name, grid=grid, in_specs=list(in_specs) + [ANY] * len(c_ins),
        out_specs=list(out_specs) + [ANY] * len(c_outs), out_shape=list(out_shape) + list(c_outs),
        scratch_shapes=list(scratch) + list(c_sems), input_output_aliases=aliases or {},
        compiler_params=_cp(dimension_semantics=("arbitrary",) * len(grid), has_side_effects=bool(comm)),
    )(*ins, *c_ins)
    return (list(res[:n_out]), list(res[n_out:])) if comm else list(res)


def _all_of(conds):
    out = conds[0]
    for c in conds[1:]:
        out = out & c
    return out


def _rglru_bwd(v0, h, dh, r_g, i_g, la_g, wa, wx, lam, seq, comm=None):
    T, C = v0.shape
    ng, G = wa.shape[0], wa.shape[1]
    nb = T // seq
    R = RG_ROWS

    def body(v_ref, h_ref, dh_ref, r_ref, i_ref, la_ref, wa_ref, wx_ref, lam_ref,
             dv_ref, dwa_ref, dwx_ref, dba_ref, dbx_ref, dlam_ref, a_s, b_s, q_s, hp_s):
        b = pl.program_id(1)
        lam_v = lam_ref[...]
        sp = _softplus_neg(lam_v)
        dsp_dlam = -_sig(-lam_v)

        @pl.when(b == 0)
        def _():
            dwa_ref[...] = jnp.zeros((G, G), F32)
            dwx_ref[...] = jnp.zeros((G, G), F32)
            dba_ref[...] = jnp.zeros((1, G), F32)
            dbx_ref[...] = jnp.zeros((1, G), F32)
            dlam_ref[...] = jnp.zeros((1, G), F32)

        hp_s[pl.ds(0, 8), :] = jnp.zeros((8, G), F32)
        hp_s[pl.ds(8, seq), :] = h_ref[...].astype(F32)
        q_s[pl.ds(seq, 8), :] = jnp.zeros((8, G), F32)

        def chunk1(c, _):
            rows = pl.ds(pl.multiple_of(c * R, R), R)
            a = jnp.exp(la_ref[rows, :])
            a_g, b_g = _group_scan(a, a * dh_ref[rows, :].astype(F32), True)
            a_s[rows, :] = a_g
            b_s[rows, :] = b_g
            return 0

        lax.fori_loop(0, seq // R, chunk1, 0)
        _group_carry(a_s, b_s, q_s, seq // 8, True)

        def chunk3(c, _):
            base = pl.multiple_of(c * R, R)
            rows = pl.ds(base, R)
            t = lax.broadcasted_iota(jnp.int32, (R, G), 0) + c * R
            v0c = v_ref[rows, :]
            r, i = r_ref[rows, :].astype(F32), i_ref[rows, :].astype(F32)
            a, mult, dmult_dla = _decay(la_ref[rows, :], t == 0)
            q_next = pltpu.roll(q_s[pl.ds(base, R + 8), :], R + 7, 0)[0:R]
            h_prev = pltpu.roll(hp_s[pl.ds(base, R + 8), :], R + 1, 0)[0:R]
            gt = dh_ref[rows, :].astype(F32) + q_next
            dla = gt * h_prev * a + gt * i * v0c * dmult_dla
            dpa = dla * (-LRU_C * sp) * r * (1.0 - r)
            dpx = gt * mult * v0c * i * (1.0 - i)
            dpa_b, dpx_b, v_b = dpa.astype(BF16), dpx.astype(BF16), v0c.astype(BF16)
            dv_ref[rows, :] = (gt * mult * i
                               + lax.dot_general(dpa_b, wa_ref[...], _DIMS["nt"], preferred_element_type=F32)
                               + lax.dot_general(dpx_b, wx_ref[...], _DIMS["nt"], preferred_element_type=F32))
            dwa_ref[...] += lax.dot_general(v_b, dpa_b, _DIMS["tn"], preferred_element_type=F32)
            dwx_ref[...] += lax.dot_general(v_b, dpx_b, _DIMS["tn"], preferred_element_type=F32)
            dba_ref[...] += jnp.sum(dpa, axis=0, keepdims=True)
            dbx_ref[...] += jnp.sum(dpx, axis=0, keepdims=True)
            dlam_ref[...] += jnp.sum(dla * (-LRU_C * r), axis=0, keepdims=True) * dsp_dlam
            return 0

        lax.fori_loop(0, seq // R, chunk3, 0)

    t2 = _bs((seq, G), lambda g, b: (b, g))
    wsp = _bs((None, G, G), lambda g, b: (g, 0, 0))
    row = _bs((1, G), lambda g, b: (0, g))
    return _call_with_comm(
        "rglru_bwd", body, (ng, nb), [v0, h, dh, r_g, i_g, la_g, wa, wx, lam], [t2] * 6 + [wsp, wsp, row],
        [t2, wsp, wsp, row, row, row],
        [_sds((T, C), F32), _sds((ng, G, G), F32), _sds((ng, G, G), F32),
         _sds((1, C), F32), _sds((1, C), F32), _sds((1, C), F32)],
        [pltpu.VMEM((seq, G), F32), pltpu.VMEM((seq, G), F32),
         pltpu.VMEM((seq + 8, G), F32), pltpu.VMEM((seq + 8, G), F32)], comm)


TC_A = 256
TC_B = 512
TAPS_A, TAPS_B = 31, 4


def _tiles(T):
    return min(512, T), min(1024, T)


GATHERED = ("w_in", "w_1", "w_a_out", "w_b_out", "w_o", "w_2", "caw", "cbw")
GATHER_KIND = {"w_in": (True, True), "w_1": (True, True), "w_a_out": (False, True), "w_b_out": (False, True),
               "w_o": (False, True), "w_2": (False, True), "caw": (True, False), "cbw": (True, False)}


def _layer_fwd(x, p, seq, cur=None, nxt=None):
    T, D = x.shape
    C, R = p["ln_g"].shape[1], p["lam"].shape[1]
    tm, tl = _tiles(T)
    gb_blk, sa_blk = (2 * C + R) // TC_B, (2 * C + 2 * R) // D
    p, ahead = dict(p), {}

    def gather(src, names):
        return None if src is None else _gather_comm([src[n] for n in names], [GATHER_KIND[n] for n in names])

    def outs(r, src, names, into):
        if src is None:
            return r
        into.update(zip(names, r[1]))
        return r[0]

    mid = ["w_a_out", "w_b_out", "w_o", "caw", "cbw"]
    z, h = outs(_fwd_norm_mm("fwd_z", x, p["g_mix"], p["w_in"], p["b_in"], tl, 1024, comm=gather(cur, mid)), cur, mid, p)
    u1 = outs(_conv_fwd("conv_a_fwd", z, 0, C // TC_A, p["caw"], p["cab"], TAPS_A, seq, TC_A, BF16,
                        comm=gather(cur, ["w_1"])), cur, ["w_1"], p)
    ya = _fwd_ya(u1, p["ln_g"], p["ln_b"], p["w_a_out"], tm)
    v0 = _conv_fwd("conv_b_fwd", z, 2 * C // TC_B, None, p["cbw"], p["cbb"], TAPS_B, seq, TC_B, F32)
    hr, rg, ig, lag = outs(_rglru_fwd(v0, p["wa"], p["wx"], p["b_rg_a"], p["b_rg_x"], p["lam"], seq,
                                      comm=gather(nxt, ["w_in"])), nxt, ["w_in"], ahead)
    yb = _fwd_yb(hr, z, gb_blk, p["w_b_out"], tl, TC_B)
    x1 = _fwd_x1(x, ya, yb, z, sa_blk, p["w_o"], tm)
    fp, h2 = outs(_fwd_norm_mm("fwd_f", x1, p["g_mlp"], p["w_1"], None, tl, 1024, comm=gather(cur, ["w_2"])),
                  cur, ["w_2"], p)
    x2 = _fwd_x2(x1, fp, p["w_2"], tm, fp.shape[1])
    saved = dict(x=x, z=z, h=h, u1=u1, ya=ya, v0=v0, hr=hr, rg=rg, ig=ig, lag=lag, yb=yb, x1=x1, fp=fp, h2=h2)
    return x2, saved, p, ahead.get("w_in")


class _Reduce:
    def __init__(self, accs, c_arr, kcl_of):
        self.accs, self.c_arr, self.kcl_of, self.pending = accs, c_arr, kcl_of, None

    def chip_sums(self, partials):
        pgs = [a if a.ndim == 4 else a.reshape(N_CHIPS, 2, a.shape[0] // (2 * N_CHIPS), a.shape[1]) for a in partials]
        return _sum_siblings(pgs, _swap_halves(pgs), self.c_arr)

    def finish(self, names, sums, received, layer):
        done = _sum_chips(sums, received, self.kcl_of(layer), [self.accs[n] for n in names])
        self.accs.update(zip(names, done))


def _layer_bwd(dx2, dx2b, p, s, seq, red=None, layer=0):
    T, D = dx2.shape
    C, R = p["ln_g"].shape[1], p["lam"].shape[1]
    tm, tl = _tiles(T)
    gb_blk, sa_blk = (2 * C + R) // TC_B, (2 * C + 2 * R) // D
    z = s["z"]
    g = {}


    dfp = _bwd_df(dx2b, p["w_2"], s["fp"], tl, 1024)
    g["w_2"] = _bwd_dw("bwd_dw2", s["fp"], dx2b, 1024, D, T, a_fn=_relu2, keep="dy")
    dx1, dx1b, g["g_mlp"] = _bwd_norm("bwd_dh2", dfp, p["w_1"], s["x1"], p["g_mlp"], dx2, tm, dfp.shape[1])
    g["w_1"] = _bwd_dw("bwd_dw1", s["h2"], dfp, D, 1024, T, shard_cols=dfp.shape[1] // N_CHIPS, keep="act")

    dya, dyb, dz = _bwd_dm(dx1b, p["w_o"], s["ya"], s["yb"], z, sa_blk, tm)

    def merged(a_refs, out_refs, i, j, k):
        ya_, yb_, sa_, sb_ = (r[...].astype(F32) for r in a_refs)
        return (_sig(sa_) * ya_ + _sig(sb_) * yb_).astype(BF16)

    tk = _bs((tm, D), lambda i, j, k: (k, 0))
    g["w_o"] = _bwd_dw("bwd_dwo", s["ya"], dx1b, D, D, tm, a_fn=merged,
                       a_extra=[(s["yb"], tk), (z, _bs((tm, D), lambda i, j, k: (k, sa_blk))),
                                (z, _bs((tm, D), lambda i, j, k: (k, sa_blk + 1)))])

    du1, g["ln_g"], g["ln_b"] = _bwd_du3(dya, p["w_a_out"], s["u1"], p["ln_g"], p["ln_b"], tm)

    def act_a(a_refs, out_refs, i, j, k):
        return _ln_silu(a_refs[0][...].astype(F32), a_refs[1][...], a_refs[2][...])[0].astype(BF16)

    rowc = _bs((1, C), lambda i, j, k: (0, 0))
    g["w_a_out"] = _bwd_dw("bwd_dwa", s["u1"], dya, C, D, tm, a_fn=act_a,
                           a_extra=[(p["ln_g"], rowc), (p["ln_b"], rowc)])
    conv_a_args = ("conv_a_bwd", du1, z, dz, 0, C // TC_A, p["caw"], TAPS_A, seq, TC_A)
    if red is not None and red.pending is not None:
        above, red.pending = red.pending, None
        (dz, g["caw"], g["cab"]), got = _conv_bwd(*conv_a_args, comm=_scatter_comm(above))
        red.finish(["w_in"], above, got, layer + 1)
    else:
        dz, g["caw"], g["cab"] = _conv_bwd(*conv_a_args)

    dhr, dz = _bwd_dp(dyb, p["w_b_out"], s["hr"], z, dz, gb_blk, tl, TC_B)

    def act_b(a_refs, out_refs, i, j, k):
        ge, _ = _gelu(a_refs[1][...].astype(F32))
        return (a_refs[0][...].astype(F32) * ge).astype(BF16)

    tb = min(1024, T)
    g["w_b_out"] = _bwd_dw("bwd_dwb", s["hr"], dyb, TC_B, D, tb, a_fn=act_b,
                           a_extra=[(z, _bs((tb, TC_B), lambda i, j, k: (k, gb_blk + i)))])
    rg_args = (s["v0"], s["hr"], dhr, s["rg"], s["ig"], s["lag"], p["wa"], p["wx"], p["lam"], seq)
    if red is not None:
        five = ["w_2", "w_1", "w_o", "w_a_out", "w_b_out"]
        sums = red.chip_sums([g.pop(n) for n in five])
        rg_out, got = _rglru_bwd(*rg_args, comm=_scatter_comm(sums))
        red.finish(five, sums, got, layer)
    else:
        rg_out = _rglru_bwd(*rg_args)
    dv0, g["wa"], g["wx"], g["b_rg_a"], g["b_rg_x"], g["lam"] = rg_out
    dz, g["cbw"], g["cbb"] = _conv_bwd("conv_b_bwd", dv0, z, dz, 2 * C // TC_B, None, p["cbw"], TAPS_B, seq, TC_B)

    dx, dxb, g["g_mix"], dbin = _bwd_norm("bwd_dh", dz, p["w_in"], s["x"], p["g_mix"], dx1, tm, dz.shape[1],
                                          colsum=True)
    g["b_in"] = dbin.reshape(1, -1)
    ns = dz.shape[1] // N_CHIPS
    g["w_in"] = _bwd_dw("bwd_dwin", s["h"], dz, D, ns // 2, T, shard_cols=ns, keep="act")
    if red is not None:
        red.pending = red.chip_sums([g.pop("w_in")])
    return dx, dxb, g


ANY = pl.BlockSpec(memory_space=pl.ANY)


def _mesh_pos():
    return lax.axis_index("x"), lax.axis_index("y"), lax.axis_index("c")


def _other_chips(x, y):
    return [(1 - x, y), (x, 1 - y), (1 - x, 1 - y)]


def _remote(src, dst, ssem, rsem, dev):
    return pltpu.make_async_remote_copy(src_ref=src, dst_ref=dst, send_sem=ssem, recv_sem=rsem,
                                        device_id=dev, device_id_type=MESH)


def _gather_region(src, dst, by_cols, k, half):
    rows, cols = src.shape
    nr = rows if half is None else rows // 2
    r0 = 0 if half is None else half * nr
    if by_cols:
        return dst.at[pl.ds(r0, nr), pl.ds(pl.multiple_of(k * cols, 128), cols)]
    return dst.at[pl.ds(pl.multiple_of(k * rows + r0, 8), nr), :]


def _gather_sends(src, dst, kinds, send, recv):
    x, y, c = _mesh_pos()
    cps = []
    for t in range(len(src)):
        half = c if kinds[t][1] else None
        hr = src[t].shape[0] // 2
        s_ref = src[t].at[pl.ds(c * hr, hr), :] if kinds[t][1] else src[t]
        for j, chip in enumerate(_other_chips(x, y)):
            cps.append(_remote(s_ref, _gather_region(src[t], dst[t], kinds[t][0], 2 * x + y, half),
                               send.at[t, j], recv.at[t, j], (*chip, c)))
    return cps


def _gather_finish(src, dst, kinds, send, recv, fsend, frecv):
    x, y, c = _mesh_pos()
    chips = _other_chips(x, y)
    sib = (x, y, 1 - c)
    n = len(src)
    fwd = []
    for t in range(n):
        half = c if kinds[t][1] else None
        for j, chip in enumerate(chips):
            got = _gather_region(src[t], dst[t], kinds[t][0], 2 * chip[0] + chip[1], half)
            _remote(got, got, send.at[t, j], recv.at[t, j], (*chip, c)).wait_recv()
            if kinds[t][1]:
                cp = _remote(got, got, fsend.at[t, j], frecv.at[t, j], sib)
                cp.start()
                fwd.append(cp)
    for t in range(n):
        if kinds[t][1]:
            for j, chip in enumerate(chips):
                got = _gather_region(src[t], dst[t], kinds[t][0], 2 * chip[0] + chip[1], 1 - c)
                _remote(got, got, fsend.at[t, j], frecv.at[t, j], sib).wait_recv()
    for cp in _gather_sends(src, dst, kinds, send, recv) + fwd:
        cp.wait_send()


def _gather_sems(n):
    sem = pltpu.SemaphoreType.DMA
    return [sem((n, 3)), sem((n, 3)), sem((n, 3)), sem((n, 3))]


def _gather_comm(shards, kinds):
    n = len(shards)

    def whole(s, by_cols):
        return (s.shape[0], N_CHIPS * s.shape[1]) if by_cols else (N_CHIPS * s.shape[0], s.shape[1])

    def own(src, dst, lsem):
        x, y, _ = _mesh_pos()
        return [pltpu.make_async_copy(src[t], _gather_region(src[t], dst[t], kinds[t][0], 2 * x + y, None), lsem.at[t])
                for t in range(n)]

    def start(src, dst, sems):
        for cp in own(src, dst, sems[4]) + _gather_sends(src, dst, kinds, sems[0], sems[1]):
            cp.start()

    def finish(src, dst, sems):
        _gather_finish(src, dst, kinds, *sems[:4])
        for cp in own(src, dst, sems[4]):
            cp.wait()

    return _Comm(shards, [_sds(whole(s, k[0]), s.dtype) for s, k in zip(shards, kinds)],
                 _gather_sems(n) + [pltpu.SemaphoreType.DMA((n,))], start, finish)


def _scatter_comm(ps):
    n = len(ps)

    def copies(src, dst, sems):
        x, y, c = _mesh_pos()
        return [_remote(src[t].at[2 * chip[0] + chip[1]], dst[t].at[j], sems[0].at[t, j], sems[1].at[t, j], (*chip, c))
                for t in range(n) for j, chip in enumerate(_other_chips(x, y))]

    def start(src, dst, sems):
        for cp in copies(src, dst, sems):
            cp.start()

    def finish(src, dst, sems):
        cps = copies(src, dst, sems)
        for cp in cps:
            cp.wait_recv()
        for cp in cps:
            cp.wait_send()

    sem = pltpu.SemaphoreType.DMA
    return _Comm(ps, [_sds((3,) + a.shape[1:], a.dtype) for a in ps], [sem((n, 3)), sem((n, 3))], start, finish)


def _comm_call(name, comm):
    n_i, n_o = len(comm.ins), len(comm.outs)

    def body(*refs):
        comm.start(refs[:n_i], refs[n_i:n_i + n_o], refs[n_i + n_o:])
        comm.finish(refs[:n_i], refs[n_i:n_i + n_o], refs[n_i + n_o:])

    return pl.pallas_call(
        body, name=name, in_specs=[ANY] * n_i, out_specs=[ANY] * n_o, out_shape=comm.outs, scratch_shapes=comm.sems,
        compiler_params=_cp(has_side_effects=True),
    )(*comm.ins)


def _swap_halves(pgs):
    n = len(pgs)

    def body(*refs):
        src, dst = refs[:n], refs[n:2 * n]
        send, recv = refs[2 * n:]
        x, y, c = _mesh_pos()
        cps = [_remote(src[t].at[:, 1 - c], dst[t], send.at[t], recv.at[t], (x, y, 1 - c)) for t in range(n)]
        for cp in cps:
            cp.start()
        for cp in cps:
            cp.wait_recv()
        for cp in cps:
            cp.wait_send()

    sem = pltpu.SemaphoreType.DMA
    return pl.pallas_call(
        body, name="swap_halves", in_specs=[ANY] * n, out_specs=[ANY] * n,
        out_shape=[_sds((a.shape[0],) + a.shape[2:], a.dtype) for a in pgs],
        scratch_shapes=[sem((n,)), sem((n,))], compiler_params=_cp(has_side_effects=True),
    )(*pgs)


def _join_halves(accs):
    n = len(accs)

    def body(*refs):
        buf = refs[n:2 * n]
        send, recv = refs[2 * n:]
        x, y, c = _mesh_pos()
        cps = [_remote(buf[t].at[:, c], buf[t].at[:, c], send.at[t], recv.at[t], (x, y, 1 - c)) for t in range(n)]
        for cp in cps:
            cp.start()
        for t in range(n):
            _remote(buf[t].at[:, c], buf[t].at[:, 1 - c], send.at[t], recv.at[t], (x, y, 1 - c)).wait_recv()
        for cp in cps:
            cp.wait_send()

    sem = pltpu.SemaphoreType.DMA
    return pl.pallas_call(
        body, name="join_halves", in_specs=[ANY] * n, out_specs=[ANY] * n,
        out_shape=[_sds(a.shape, a.dtype) for a in accs], scratch_shapes=[sem((n,)), sem((n,))],
        input_output_aliases={t: t for t in range(n)}, compiler_params=_cp(has_side_effects=True),
    )(*accs)


def _sum_siblings(pgs, rbs, c_arr):
    n = len(pgs)
    nk = pgs[0].shape[0]

    def body(c_ref, *refs):
        for t in range(n):
            refs[2 * n + t][...] = (refs[t][...].astype(F32) + refs[n + t][...].astype(F32)).astype(BF16)

    half = lambda a: pl.BlockSpec((None,) + a.shape[2:], lambda k, c_ref: (k, 0, 0))
    return pl.pallas_call(
        body, name="sum_siblings",
        grid_spec=pltpu.PrefetchScalarGridSpec(
            num_scalar_prefetch=1, grid=(nk,),
            in_specs=[pl.BlockSpec((None, None) + a.shape[2:], lambda k, c_ref: (k, c_ref[0], 0, 0)) for a in pgs]
            + [half(a) for a in pgs],
            out_specs=[half(a) for a in pgs]),
        out_shape=[_sds((nk,) + a.shape[2:], BF16) for a in pgs],
        compiler_params=_cp(dimension_semantics=("arbitrary",)),
    )(c_arr, *pgs, *rbs)


def _sum_chips(ps, rbs, kcl, accs):
    n = len(ps)

    def body(k_ref, *refs):
        for t in range(n):
            b_ref = refs[n + t]
            refs[3 * n + t][...] = (refs[t][...].astype(F32) + b_ref[0].astype(F32) + b_ref[1].astype(F32)
                                    + b_ref[2].astype(F32))

    qr = lambda a: (a.shape[1] // 2, a.shape[2])
    return pl.pallas_call(
        body, name="sum_chips",
        grid_spec=pltpu.PrefetchScalarGridSpec(
            num_scalar_prefetch=1, grid=(2,),
            in_specs=[pl.BlockSpec((None,) + qr(a), lambda r, k_ref: (k_ref[0], r, 0)) for a in ps]
            + [pl.BlockSpec((3,) + qr(a), lambda r, k_ref: (0, r, 0)) for a in ps] + [ANY] * n,
            out_specs=[pl.BlockSpec((None, None) + qr(a), lambda r, k_ref: (k_ref[2], k_ref[1], r, 0)) for a in ps]),
        out_shape=[_sds(a.shape, F32) for a in accs], input_output_aliases={1 + 2 * n + t: t for t in range(n)},
        compiler_params=_cp(dimension_semantics=("arbitrary",)),
    )(kcl, *ps, *rbs, *accs)


N_DEV = 8


def _allreduce_small(part):
    _, r, lanes = part.shape

    def body(p_ref, o_ref, rbuf, s1, r1, s2, r2):
        x, y, c = _mesh_pos()
        me = 4 * x + 2 * y + c
        devs = [(d // 4, (d // 2) % 2, d % 2) for d in range(N_DEV)]
        rbuf[me] = p_ref[me]

        def each_peer(fn):
            for d in range(N_DEV):
                @pl.when(d != me)
                def _():
                    fn(d)

        each_peer(lambda d: _remote(p_ref.at[d], rbuf.at[me], s1.at[d], r1.at[me], devs[d]).start())
        each_peer(lambda d: _remote(p_ref.at[d], rbuf.at[d], s1.at[d], r1.at[d], devs[d]).wait_recv())
        total = rbuf[0]
        for d in range(1, N_DEV):
            total = total + rbuf[d]
        o_ref[me] = total
        each_peer(lambda d: _remote(o_ref.at[me], o_ref.at[me], s2.at[d], r2.at[me], devs[d]).start())
        each_peer(lambda d: _remote(o_ref.at[d], o_ref.at[d], s2.at[d], r2.at[d], devs[d]).wait_recv())
        each_peer(lambda d: _remote(p_ref.at[d], rbuf.at[me], s1.at[d], r1.at[me], devs[d]).wait_send())
        each_peer(lambda d: _remote(o_ref.at[me], o_ref.at[me], s2.at[d], r2.at[me], devs[d]).wait_send())

    sem = pltpu.SemaphoreType.DMA
    vm = pl.BlockSpec(memory_space=pltpu.VMEM)
    return pl.pallas_call(
        body, name="allreduce_small", in_specs=[vm], out_specs=vm, out_shape=_sds(part.shape, F32),
        scratch_shapes=[pltpu.VMEM(part.shape, F32), sem((N_DEV,)), sem((N_DEV,)), sem((N_DEV,)), sem((N_DEV,))],
        compiler_params=_cp(has_side_effects=True),
    )(part)


BIG = ("w_in", "w_1", "w_a_out", "w_b_out", "w_o", "w_2")
BY_COLS = {"w_in": True, "w_1": True, "w_a_out": False, "w_b_out": False, "w_o": False, "w_2": False}
WEIGHTS = ("g_mix", "w_in", "b_in", "conv_a_w", "conv_a_b", "ln_g", "ln_b", "w_a_out", "conv_b_w", "conv_b_b", "w_rg_a",
           "b_rg_a", "w_rg_x", "b_rg_x", "lam", "w_b_out", "w_o", "g_mlp", "w_1", "w_2", "g_final")
SMALL = tuple(n for n in WEIGHTS if n not in BIG)
ADAM_ROWS = 256
ADAM_SMALL_ROWS = 2048


def _block_diag(w):
    nh, dh, _ = w.shape
    ng = nh // HEADS_PER_GROUP
    w4 = w.reshape(ng, HEADS_PER_GROUP, dh, dh)
    eye = jnp.eye(HEADS_PER_GROUP, dtype=w.dtype)
    return jnp.einsum("qhij,hk->qhikj", w4, eye).reshape(ng, HEADS_PER_GROUP * dh, HEADS_PER_GROUP * dh)


def _block_diag_part(d, dh):
    ng = d.shape[0]
    eye = jnp.eye(HEADS_PER_GROUP, dtype=d.dtype)
    d5 = d.reshape(ng, HEADS_PER_GROUP, dh, HEADS_PER_GROUP, dh)
    return jnp.einsum("qhikj,hk->qhij", d5, eye).reshape(ng * HEADS_PER_GROUP, dh, dh)


PACK_LANES = 128


def _pack(arrays, row_multiple):
    parts = [a.reshape(-1, PACK_LANES) for a in arrays]
    parts = [jnp.pad(p, ((0, -p.shape[0] % 8), (0, 0))) if p.shape[0] % 8 else p for p in parts]
    rows = sum(p.shape[0] for p in parts)
    pad = -rows % row_multiple
    if pad:
        parts.append(jnp.zeros((pad, PACK_LANES), parts[0].dtype))
    return jnp.concatenate(parts, axis=0)


def _unpack(buf, like):
    buf = buf.reshape(-1, PACK_LANES)
    out, off = [], 0
    for a in like:
        n = a.size // PACK_LANES
        out.append(buf[off:off + n].reshape(a.shape))
        off += n + (-n % 8)
    return out


def kernel(x, g_mix, w_in, b_in, conv_a_w, conv_a_b, ln_g, ln_b, w_a_out, conv_b_w, conv_b_b, w_rg_a, b_rg_a, w_rg_x, b_rg_x, lam, w_b_out, w_o, g_mlp, w_1, w_2, g_final, loss_target, m_g_mix, m_w_in, m_b_in, m_conv_a_w, m_conv_a_b, m_ln_g, m_ln_b, m_w_a_out, m_conv_b_w, m_conv_b_b, m_w_rg_a, m_b_rg_a, m_w_rg_x, m_b_rg_x, m_lam, m_w_b_out, m_w_o, m_g_mlp, m_w_1, m_w_2, m_g_final, v_g_mix, v_w_in, v_b_in, v_conv_a_w, v_conv_a_b, v_ln_g, v_ln_b, v_w_a_out, v_conv_b_w, v_conv_b_b, v_w_rg_a, v_b_rg_a, v_w_rg_x, v_b_rg_x, v_lam, v_w_b_out, v_w_o, v_g_mlp, v_w_1, v_w_2, v_g_final):
    w = dict(g_mix=g_mix, w_in=w_in, b_in=b_in, conv_a_w=conv_a_w, conv_a_b=conv_a_b, ln_g=ln_g, ln_b=ln_b, w_a_out=w_a_out,
             conv_b_w=conv_b_w, conv_b_b=conv_b_b, w_rg_a=w_rg_a, b_rg_a=b_rg_a, w_rg_x=w_rg_x, b_rg_x=b_rg_x, lam=lam,
             w_b_out=w_b_out, w_o=w_o, g_mlp=g_mlp, w_1=w_1, w_2=w_2, g_final=g_final)
    m = dict(g_mix=m_g_mix, w_in=m_w_in, b_in=m_b_in, conv_a_w=m_conv_a_w, conv_a_b=m_conv_a_b, ln_g=m_ln_g, ln_b=m_ln_b,
             w_a_out=m_w_a_out, conv_b_w=m_conv_b_w, conv_b_b=m_conv_b_b, w_rg_a=m_w_rg_a, b_rg_a=m_b_rg_a, w_rg_x=m_w_rg_x,
             b_rg_x=m_b_rg_x, lam=m_lam, w_b_out=m_w_b_out, w_o=m_w_o, g_mlp=m_g_mlp, w_1=m_w_1, w_2=m_w_2, g_final=m_g_final)
    v = dict(g_mix=v_g_mix, w_in=v_w_in, b_in=v_b_in, conv_a_w=v_conv_a_w, conv_a_b=v_conv_a_b, ln_g=v_ln_g, ln_b=v_ln_b,
             w_a_out=v_w_a_out, conv_b_w=v_conv_b_w, conv_b_b=v_conv_b_b, w_rg_a=v_w_rg_a, b_rg_a=v_b_rg_a, w_rg_x=v_w_rg_x,
             b_rg_x=v_b_rg_x, lam=v_lam, w_b_out=v_w_b_out, w_o=v_w_o, g_mlp=v_g_mlp, w_1=v_w_1, w_2=v_w_2, g_final=v_g_final)
    B, S, D = x.shape
    T = B * S
    L = w_in.shape[0]
    dh = w_rg_a.shape[-1]
    taps_a, taps_b = conv_a_w.shape[1], conv_b_w.shape[1]
    assert (taps_a, taps_b) == (TAPS_A, TAPS_B)
    xi, yi, ci = _mesh_pos()
    c_arr = jnp.reshape(ci, (1,)).astype(jnp.int32)
    k_me = 2 * xi + yi

    caw_p = jnp.pad(conv_a_w, ((0, 0), (0, 32 - taps_a), (0, 0)))
    cbw_p = jnp.pad(conv_b_w, ((0, 0), (0, 8 - taps_b), (0, 0)))
    row = lambda a: a.reshape(1, -1)

    def shards_of(l):
        d = {n: w[n][l].astype(BF16) for n in BIG}
        d.update(caw=caw_p[l], cbw=cbw_p[l])
        return d

    def params_of(l, w_in_whole):
        p = dict(w_in=w_in_whole, cab=row(conv_a_b[l]), cbb=row(conv_b_b[l]),
                 wa=_block_diag(w_rg_a[l]).astype(BF16), wx=_block_diag(w_rg_x[l]).astype(BF16))
        for n in ("g_mix", "b_in", "ln_g", "ln_b", "b_rg_a", "b_rg_x", "lam", "g_mlp"):
            p[n] = row(w[n][l])
        return p

    shards = [shards_of(l) for l in range(L)]
    w_in_whole, = _comm_call("gather_first", _gather_comm([shards[0]["w_in"]], [GATHER_KIND["w_in"]]))
    xf = x.reshape(T, D)
    saved, params = [], []
    for l in range(L):
        xf, s, p, w_in_whole = _layer_fwd(xf, params_of(l, w_in_whole), S, cur=shards[l],
                                          nxt=shards[l + 1] if l + 1 < L else None)
        saved.append(s)
        params.append(p)
    loss_part, dx, dxb, dg_final = _loss_head(xf, row(g_final), loss_target.reshape(T, D), _tiles(T)[0])
    loss = lax.psum(loss_part[0, 0], ("x", "y", "c"))

    half_shape = lambda a: (L, 2, a.shape[1] // 2, a.shape[2])
    accs = {n: lax.empty(half_shape(w[n]), F32) for n in BIG}
    small = {n: [None] * L for n in SMALL if n != "g_final"}
    red = _Reduce(accs, c_arr, lambda l: jnp.stack([k_me, ci, jnp.full((), l, ci.dtype)]).astype(jnp.int32))
    for l in reversed(range(L)):
        dx, dxb, g = _layer_bwd(dx, dxb, params[l], saved[l], S, red=red, layer=l)
        small["g_mix"][l], small["b_in"][l], small["g_mlp"][l] = g["g_mix"], g["b_in"], g["g_mlp"]
        small["conv_a_w"][l], small["conv_a_b"][l] = g["caw"], g["cab"]
        small["conv_b_w"][l], small["conv_b_b"][l] = g["cbw"], g["cbb"]
        small["ln_g"][l], small["ln_b"][l], small["lam"][l] = g["ln_g"], g["ln_b"], g["lam"]
        small["w_rg_a"][l], small["w_rg_x"][l] = _block_diag_part(g["wa"], dh), _block_diag_part(g["wx"], dh)
        small["b_rg_a"][l], small["b_rg_x"][l] = g["b_rg_a"], g["b_rg_x"]
    grad_x = dx.reshape(B, S, D)
    red.finish(["w_in"], red.pending, _comm_call("scatter_chips", _scatter_comm(red.pending)), 0)

    joined = _join_halves([red.accs[n] for n in BIG])
    grads = {n: a.reshape(w[n].shape) for n, a in zip(BIG, joined)}

    names = [n for n in SMALL if n != "g_final"]
    parts = [jnp.stack(small[n]) for n in names] + [dg_final]
    packed = _pack(parts, 8 * N_DEV)
    total = _allreduce_small(packed.reshape(N_DEV, packed.shape[0] // N_DEV, PACK_LANES))
    for n, a in zip(names + ["g_final"], _unpack(total, parts)):
        if n == "conv_a_w":
            a = lax.dynamic_slice_in_dim(a[:, :taps_a], k_me * conv_a_w.shape[2], conv_a_w.shape[2], axis=2)
        elif n == "conv_b_w":
            a = lax.dynamic_slice_in_dim(a[:, :taps_b], k_me * conv_b_w.shape[2], conv_b_w.shape[2], axis=2)
        grads[n] = a.reshape(w[n].shape)

    delta, new_m, new_v = {}, {}, {}
    for n in BIG:
        cols = w[n].shape[-1]
        d_, m_, v_ = _adamw("adamw_" + n, w[n].reshape(-1, cols), grads[n].reshape(-1, cols), m[n].reshape(-1, cols),
                            v[n].reshape(-1, cols), ADAM_ROWS)
        delta[n], new_m[n], new_v[n] = (a.reshape(w[n].shape) for a in (d_, m_, v_))
    for n in SMALL:
        cols = w[n].shape[-1]
        view = lambda a: a.reshape(-1, cols)
        rows = view(w[n]).shape[0]
        d_, m_, v_ = _adamw("adamw_" + n, view(w[n]), view(grads[n]), view(m[n]), view(v[n]),
                            ADAM_SMALL_ROWS if rows % ADAM_SMALL_ROWS == 0 else rows)
        delta[n], new_m[n], new_v[n] = (a.reshape(w[n].shape) for a in (d_, m_, v_))

    return (loss, grad_x, *[grads[n] for n in WEIGHTS], *[delta[n] for n in WEIGHTS],
            *[new_m[n] for n in WEIGHTS], *[new_v[n] for n in WEIGHTS])
```

```python
import jax
import jax.numpy as jnp
from jax import lax
from jax.experimental import pallas as pl
from jax.experimental.pallas import tpu as pltpu

F32 = jnp.float32
BF16 = jnp.bfloat16
MESH = pl.DeviceIdType.MESH

EPS = 1e-6
LRU_C = 8.0
ADAM_LR, ADAM_B1, ADAM_B2, ADAM_EPS, ADAM_WD, ADAM_STEP = 0.001, 0.9, 0.999, 1e-08, 0.01, 10

N_CHIPS = 4
HEADS_PER_GROUP = 4
VMEM_LIMIT = 56 * 1024 * 1024


def _cp(**kw):
    return pltpu.CompilerParams(vmem_limit_bytes=VMEM_LIMIT, **kw)


def _sig(x):
    return 1.0 / (1.0 + jnp.exp(-x))


def _gelu(x):
    t = jnp.tanh(0.7978845608028654 * (x + 0.044715 * x * x * x))
    return 0.5 * x * (1.0 + t), t


def _gelu_grad(x, t):
    dt = (1.0 - t * t) * 0.7978845608028654 * (1.0 + 3.0 * 0.044715 * x * x)
    return 0.5 * (1.0 + t) + 0.5 * x * dt


def _rms(xf, g):
    r = lax.rsqrt(jnp.mean(xf * xf, axis=-1, keepdims=True) + EPS)
    return xf * r * g, r


def _rms_bwd(xf, g, r, dh):
    dyg = dh * g
    dx = r * (dyg - xf * (r * r) * jnp.mean(dyg * xf, axis=-1, keepdims=True))
    return dx, dh * xf * r


def _ln_silu(u, g, b):
    mu = jnp.mean(u, axis=-1, keepdims=True)
    uc = u - mu
    rstd = lax.rsqrt(jnp.mean(uc * uc, axis=-1, keepdims=True) + EPS)
    uh = uc * rstd
    u2 = uh * g + b
    s = _sig(u2)
    return u2 * s, uh, rstd, u2, s


_DIMS = {"nn": (((1,), (0,)), ((), ())), "nt": (((1,), (1,)), ((), ())), "tn": (((0,), (0,)), ((), ()))}


class _Comm:
    def __init__(self, ins, outs, sems, start, finish):
        self.ins, self.outs, self.sems, self.start, self.finish = list(ins), list(outs), list(sems), start, finish


def _resident(shape):
    return pl.BlockSpec(shape, lambda i, j, k: (0,) * len(shape), pipeline_mode=pl.Buffered(1))


def _mm(name, mode, grid, a_ins, a_fn, b_in, e_ins, epi, outs, acc_shape, cache_a=None, alias=(), extra_scratch=(),
        comm=None, b_slice=None):
    ni, nj, nk = grid
    na, ne, no = len(a_ins), len(e_ins), len(outs)
    assert cache_a is None or nk == 1
    n_fixed = (nk > 1) + (cache_a is not None)
    n_in = na + 1 + ne + len(alias)
    c_ins, c_outs, c_sems = (comm.ins, comm.outs, comm.sems) if comm else ([], [], [])

    def body(*refs):
        a_refs = refs[:na]
        b_ref = refs[na]
        e_refs = refs[na + 1:na + 1 + ne]
        comm_in = refs[n_in:n_in + len(c_ins)]
        out0 = n_in + len(c_ins)
        out_refs = refs[out0:out0 + no]
        comm_out = refs[out0 + no:out0 + no + len(c_outs)]
        scratch = refs[out0 + no + len(c_outs):]
        extra = scratch[n_fixed:n_fixed + len(extra_scratch)]
        comm_sems = scratch[n_fixed + len(extra_scratch):]
        i, j, k = pl.program_id(0), pl.program_id(1), pl.program_id(2)
        if comm:
            @pl.when((i == 0) & (j == 0) & (k == 0))
            def _():
                comm.start(comm_in, comm_out, comm_sems)
        if cache_a is not None:
            cache_ref = scratch[n_fixed - 1]

            @pl.when(j == 0)
            def _():
                cache_ref[...] = a_fn(a_refs, out_refs, i, j, k)

            a = cache_ref[...]
        else:
            a = a_fn(a_refs, out_refs, i, j, k)
        if b_slice is None:
            b = b_ref[...]
        elif b_slice[0] == "cols":
            b = b_ref[:, pl.ds(pl.multiple_of(j * b_slice[1], b_slice[1]), b_slice[1])]
        else:
            b = b_ref[pl.ds(pl.multiple_of(j * b_slice[1], b_slice[1]), b_slice[1]), :]
        prod = lax.dot_general(a, b, _DIMS[mode], preferred_element_type=F32)
        if nk == 1:
            epi(prod, e_refs, out_refs, i, j, extra)
        else:
            acc_ref = scratch[0]

            @pl.when(k == 0)
            def _():
                acc_ref[...] = prod

            @pl.when(k > 0)
            def _():
                acc_ref[...] += prod

            @pl.when(k == nk - 1)
            def _():
                epi(acc_ref[...], e_refs, out_refs, i, j, extra)

        if comm:
            @pl.when((i == ni - 1) & (j == nj - 1) & (k == nk - 1))
            def _():
                comm.finish(comm_in, comm_out, comm_sems)

    scratch_shapes = []
    if nk > 1:
        scratch_shapes.append(pltpu.VMEM(acc_shape, F32))
    if cache_a is not None:
        scratch_shapes.append(pltpu.VMEM(cache_a, BF16))
    any_spec = pl.BlockSpec(memory_space=pl.ANY)
    ins = (list(a_ins) + [b_in] + list(e_ins) + [(arr, any_spec) for arr, _ in alias] + [(arr, any_spec) for arr in c_ins])
    first_alias = na + 1 + ne
    res = pl.pallas_call(
        body, name=name, grid=grid,
        in_specs=[s for _, s in ins], out_specs=[s for _, s in outs] + [any_spec] * len(c_outs),
        out_shape=[o for o, _ in outs] + list(c_outs),
        scratch_shapes=scratch_shapes + list(extra_scratch) + list(c_sems),
        input_output_aliases={first_alias + n: o for n, (_, o) in enumerate(alias)},
        compiler_params=_cp(dimension_semantics=("arbitrary", "arbitrary", "arbitrary"), has_side_effects=bool(comm)),
    )(*[a for a, _ in ins])
    if comm:
        return list(res[:no]), list(res[no:])
    return res


def _bs(shape, fn):
    return pl.BlockSpec(shape, fn)


def _sds(shape, dt):
    return jax.ShapeDtypeStruct(shape, dt)


def _acc_rows(ref, val, first):
    @pl.when(first)
    def _():
        ref[...] = val

    @pl.when(jnp.logical_not(first))
    def _():
        ref[...] += val


def _fwd_norm_mm(name, x, g, w, bias, tm, tn, comm=None):
    T, D = x.shape
    N = w.shape[1]

    def a_fn(a_refs, out_refs, i, j, k):
        h, _ = _rms(a_refs[0][...], a_refs[1][...])
        hb = h.astype(BF16)
        out_refs[1][...] = hb
        return hb

    def epi(acc, e_refs, out_refs, i, j, extra):
        if bias is not None:
            acc = acc + e_refs[0][...]
        out_refs[0][...] = acc.astype(BF16)

    e_ins = [] if bias is None else [(bias, _bs((1, tn), lambda i, j, k: (0, j)))]
    return _mm(name, "nn", (T // tm, N // tn, 1),
               [(x, _bs((tm, D), lambda i, j, k: (i, 0))), (g, _bs((1, D), lambda i, j, k: (0, 0)))], a_fn,
               (w, _resident((D, N))), e_ins, epi,
               [(_sds((T, N), BF16), _bs((tm, tn), lambda i, j, k: (i, j))),
                (_sds((T, D), BF16), _bs((tm, D), lambda i, j, k: (i, 0)))],
               None, cache_a=(tm, D), comm=comm, b_slice=("cols", tn))


def _fwd_ya(u1, ln_g, ln_b, w, tm):
    T, C = u1.shape
    N = w.shape[1]

    def a_fn(a_refs, out_refs, i, j, k):
        u3 = _ln_silu(a_refs[0][...].astype(F32), a_refs[1][...], a_refs[2][...])[0]
        return u3.astype(BF16)

    def epi(acc, e_refs, out_refs, i, j, extra):
        out_refs[0][...] = acc.astype(BF16)

    row = _bs((1, C), lambda i, j, k: (0, 0))
    return _mm("fwd_ya", "nn", (T // tm, 1, 1),
               [(u1, _bs((tm, C), lambda i, j, k: (i, 0))), (ln_g, row), (ln_b, row)], a_fn,
               (w, _bs((C, N), lambda i, j, k: (0, 0))), [], epi,
               [(_sds((T, N), BF16), _bs((tm, N), lambda i, j, k: (i, 0)))], None)[0]


def _fwd_yb(h, z, gb_blk, w, tm, tk):
    T, C = h.shape
    N = w.shape[1]

    def a_fn(a_refs, out_refs, i, j, k):
        ge, _ = _gelu(a_refs[1][...].astype(F32))
        return (a_refs[0][...].astype(F32) * ge).astype(BF16)

    def epi(acc, e_refs, out_refs, i, j, extra):
        out_refs[0][...] = acc.astype(BF16)

    return _mm("fwd_yb", "nn", (T // tm, 1, C // tk),
               [(h, _bs((tm, tk), lambda i, j, k: (i, k))), (z, _bs((tm, tk), lambda i, j, k: (i, gb_blk + k)))], a_fn,
               (w, _bs((tk, N), lambda i, j, k: (k, 0))), [], epi,
               [(_sds((T, N), BF16), _bs((tm, N), lambda i, j, k: (i, 0)))], (tm, N))[0]


def _fwd_x1(x, ya, yb, z, sa_blk, w, tm):
    T, D = x.shape

    def a_fn(a_refs, out_refs, i, j, k):
        ya_, yb_, sa_, sb_ = (r[...].astype(F32) for r in a_refs)
        return (_sig(sa_) * ya_ + _sig(sb_) * yb_).astype(BF16)

    def epi(acc, e_refs, out_refs, i, j, extra):
        out_refs[0][...] = e_refs[0][...] + acc

    t = _bs((tm, D), lambda i, j, k: (i, 0))
    return _mm("fwd_x1", "nn", (T // tm, 1, 1),
               [(ya, t), (yb, t), (z, _bs((tm, D), lambda i, j, k: (i, sa_blk))),
                (z, _bs((tm, D), lambda i, j, k: (i, sa_blk + 1)))], a_fn,
               (w, _bs((D, D), lambda i, j, k: (0, 0))), [(x, t)], epi,
               [(_sds((T, D), F32), t)], None)[0]


def _fwd_x2(x1, fp, w, tm, tk, comm=None):
    T, D = x1.shape
    Fd = fp.shape[1]

    def epi(acc, e_refs, out_refs, i, j, extra):
        out_refs[0][...] = e_refs[0][...] + acc

    t = _bs((tm, D), lambda i, j, k: (i, 0))
    whole_k = tk == Fd
    r = _mm("fwd_x2", "nn", (T // tm, 1, Fd // tk),
            [(fp, _bs((tm, tk), lambda i, j, k: (i, k)))], _relu2,
            (w, _resident((Fd, D)) if whole_k else _bs((tk, D), lambda i, j, k: (k, 0))), [(x1, t)], epi,
            [(_sds((T, D), F32), t)], (tm, D), comm=comm)
    return (r[0][0], r[1]) if comm else r[0]


def _relu2(a_refs, out_refs, i, j, k):
    f = jnp.maximum(a_refs[0][...], 0.0)
    return f * f


def _loss_head(x, g, target, tm):
    T, D = x.shape

    def body(x_ref, g_ref, t_ref, loss_ref, dx_ref, dxb_ref, dg_ref):
        i = pl.program_id(0)
        xf, gv = x_ref[...], g_ref[...]
        y, r = _rms(xf, gv)
        err = y - t_ref[...]
        part = 0.5 * jnp.sum(jnp.mean(err * err, axis=-1, keepdims=True), axis=0, keepdims=True)
        dx, dg_rows = _rms_bwd(xf, gv, r, err * (1.0 / D))
        dx_ref[...] = dx
        dxb_ref[...] = dx.astype(BF16)
        _acc_rows(loss_ref, jnp.broadcast_to(part, (1, 128)), i == 0)
        _acc_rows(dg_ref, jnp.sum(dg_rows, axis=0, keepdims=True), i == 0)

    t = _bs((tm, D), lambda i: (i, 0))
    row = _bs((1, D), lambda i: (0, 0))
    return pl.pallas_call(
        body, name="loss_head", grid=(T // tm,), in_specs=[t, row, t],
        out_specs=[_bs((1, 128), lambda i: (0, 0)), t, t, row],
        out_shape=[_sds((1, 128), F32), _sds((T, D), F32), _sds((T, D), BF16), _sds((1, D), F32)],
        compiler_params=_cp(dimension_semantics=("arbitrary",)),
    )(x, g, target)


def _adamw(name, w, g, m, v, tr, comm=None):
    many = isinstance(w, (list, tuple))
    ws, gs, ms, vs = (list(a) if many else [a] for a in (w, g, m, v))
    n = len(ws)
    rows, cols = ws[0].shape
    d1 = 1.0 - ADAM_B1 ** ADAM_STEP
    d2 = 1.0 - ADAM_B2 ** ADAM_STEP

    def body(*refs):
        for q in range(n):
            w_ref, g_ref, m_ref, v_ref = (refs[a * n + q] for a in range(4))
            d_ref, mo_ref, vo_ref = (refs[(4 + a) * n + q] for a in range(3))
            gv = g_ref[...]
            mn = ADAM_B1 * m_ref[...] + (1.0 - ADAM_B1) * gv
            vn = ADAM_B2 * v_ref[...] + (1.0 - ADAM_B2) * (gv * gv)
            d_ref[...] = -ADAM_LR * ((mn / d1) / (jnp.sqrt(vn / d2) + ADAM_EPS) + ADAM_WD * w_ref[...])
            mo_ref[...] = mn
            vo_ref[...] = vn

    t = _bs((tr, cols), lambda i: (i, 0))
    r = _call_with_comm(name, body, (rows // tr,), ws + gs + ms + vs, [t] * (4 * n), [t] * (3 * n),
                        [_sds((rows, cols), F32)] * (3 * n), [], comm)
    outs, got = (r if comm else (r, None))
    res = [outs[a * n:(a + 1) * n] if many else outs[a * n] for a in range(3)]
    return (*res, got) if comm else tuple(res)


def _ident(a_refs, out_refs, i, j, k):
    return a_refs[0][...]


def _bwd_dw(name, act, dy, ti, tj, tm, a_fn=None, a_extra=(), shard_cols=None, keep=None):
    T, J = dy.shape
    I = act.shape[1]

    def epi(acc, e_refs, out_refs, i, j, extra):
        out_refs[0][...] = acc.astype(BF16).reshape(out_refs[0].shape)

    if shard_cols is None:
        out = (_sds((I, J), BF16), _bs((ti, tj), lambda i, j, k: (i, j)))
    else:
        per = shard_cols // tj
        assert ti == I and per * tj == shard_cols
        out = (_sds((J // shard_cols, 2, I // 2, shard_cols), BF16),
               _bs((None, 2, I // 2, tj), lambda i, j, k: (lax.div(j, per), 0, 0, lax.rem(j, per))))
    assert keep is None or tm == T
    a_spec = _resident((T, I)) if keep == "act" else _bs((tm, ti), lambda i, j, k: (k, i))
    b_spec = _resident((T, J)) if keep == "dy" else _bs((tm, tj), lambda i, j, k: (k, j))
    return _mm(name, "tn", (I // ti, J // tj, T // tm), [(act, a_spec)] + list(a_extra), a_fn or _ident,
               (dy, b_spec), [], epi, [out], (ti, tj))[0]


def _bwd_df(dxb, w2, fp, tm, tn, comm=None):
    T, D = dxb.shape
    Fd = w2.shape[0]

    def epi(acc, e_refs, out_refs, i, j, extra):
        out_refs[0][...] = (acc * (2.0 * jnp.maximum(e_refs[0][...].astype(F32), 0.0))).astype(BF16)

    t = _bs((tm, tn), lambda i, j, k: (i, j))
    r = _mm("bwd_df", "nt", (T // tm, Fd // tn, 1), [(dxb, _bs((tm, D), lambda i, j, k: (i, 0)))], _ident,
            (w2, _resident((Fd, D))), [(fp, t)], epi, [(_sds((T, Fd), BF16), t)], None, b_slice=("rows", tn), comm=comm)
    return (r[0][0], r[1]) if comm else r[0]


def _bwd_norm(name, dy, w, x, g, dres, tm, tk, colsum=False, comm=None):
    T, K = dy.shape
    D = w.shape[0]
    nk = K // tk

    def a_fn(a_refs, out_refs, i, j, k):
        a = a_refs[0][...]
        if colsum:
            s = jnp.sum(a.astype(F32), axis=0, keepdims=True)

            @pl.when(i == 0)
            def _():
                out_refs[3][k] = s

            @pl.when(i > 0)
            def _():
                out_refs[3][k] += s
        return a

    def epi(acc, e_refs, out_refs, i, j, extra):
        xf, gv = e_refs[0][...], e_refs[1][...]
        r = lax.rsqrt(jnp.mean(xf * xf, axis=-1, keepdims=True) + EPS)
        dx, dg_rows = _rms_bwd(xf, gv, r, acc)
        dx = dx + e_refs[2][...]
        out_refs[0][...] = dx
        out_refs[1][...] = dx.astype(BF16)
        _acc_rows(out_refs[2], jnp.sum(dg_rows, axis=0, keepdims=True), i == 0)

    t = _bs((tm, D), lambda i, j, k: (i, 0))
    row = _bs((1, D), lambda i, j, k: (0, 0))
    outs = [(_sds((T, D), F32), t), (_sds((T, D), BF16), t), (_sds((1, D), F32), row)]
    if colsum:
        outs.append((_sds((nk, 1, tk), F32), _bs((nk, 1, tk), lambda i, j, k: (0, 0, 0))))
    return _mm(name, "nt", (T // tm, 1, nk), [(dy, _bs((tm, tk), lambda i, j, k: (i, k)))], a_fn,
               (w, _resident((D, K)) if nk == 1 else _bs((D, tk), lambda i, j, k: (0, k))),
               [(x, t), (g, row), (dres, t)], epi, outs, (tm, D), comm=comm)


def _bwd_dm(dxb, w_o, ya, yb, z, sa_blk, tm):
    T, D = dxb.shape

    def epi(acc, e_refs, out_refs, i, j, extra):
        ya_, yb_, sa_, sb_ = (r[...].astype(F32) for r in e_refs)
        ga, gb = _sig(sa_), _sig(sb_)
        out_refs[0][...] = (acc * ga).astype(BF16)
        out_refs[1][...] = (acc * gb).astype(BF16)
        stage, sem = extra
        put = pltpu.make_async_copy(
            stage, out_refs[2].at[pl.ds(pl.multiple_of(i * tm, tm), tm), pl.ds(sa_blk * D, 2 * D)], sem)

        @pl.when(i > 0)
        def _():
            put.wait()

        stage[:, 0:D] = (acc * ya_ * ga * (1.0 - ga)).astype(BF16)
        stage[:, D:2 * D] = (acc * yb_ * gb * (1.0 - gb)).astype(BF16)
        put.start()

        @pl.when(i == T // tm - 1)
        def _():
            put.wait()

    t = _bs((tm, D), lambda i, j, k: (i, 0))
    return _mm("bwd_dm", "nt", (T // tm, 1, 1), [(dxb, t)], _ident, (w_o, _bs((D, D), lambda i, j, k: (0, 0))),
               [(ya, t), (yb, t), (z, _bs((tm, D), lambda i, j, k: (i, sa_blk))),
                (z, _bs((tm, D), lambda i, j, k: (i, sa_blk + 1)))], epi,
               [(_sds((T, D), BF16), t), (_sds((T, D), BF16), t),
                (_sds(z.shape, BF16), pl.BlockSpec(memory_space=pl.ANY))], None,
               extra_scratch=[pltpu.VMEM((tm, 2 * D), BF16), pltpu.SemaphoreType.DMA(())])


def _bwd_du3(dya, w, u1, ln_g, ln_b, tm):
    T, D = dya.shape
    C = w.shape[0]

    def epi(acc, e_refs, out_refs, i, j, extra):
        gv = e_refs[1][...]
        _, uh, rstd, u2, s = _ln_silu(e_refs[0][...].astype(F32), gv, e_refs[2][...])
        du2 = acc * (s * (1.0 + u2 * (1.0 - s)))
        duh = du2 * gv
        out_refs[0][...] = rstd * (duh - jnp.mean(duh, axis=-1, keepdims=True)
                                   - uh * jnp.mean(duh * uh, axis=-1, keepdims=True))
        _acc_rows(out_refs[1], jnp.sum(du2 * uh, axis=0, keepdims=True), i == 0)
        _acc_rows(out_refs[2], jnp.sum(du2, axis=0, keepdims=True), i == 0)

    t = _bs((tm, C), lambda i, j, k: (i, 0))
    row = _bs((1, C), lambda i, j, k: (0, 0))
    return _mm("bwd_du3", "nt", (T // tm, 1, 1), [(dya, _bs((tm, D), lambda i, j, k: (i, 0)))], _ident,
               (w, _bs((C, D), lambda i, j, k: (0, 0))), [(u1, t), (ln_g, row), (ln_b, row)], epi,
               [(_sds((T, C), F32), t), (_sds((1, C), F32), row), (_sds((1, C), F32), row)], None)


def _bwd_dp(dyb, w, h, z, dz, gb_blk, tm, tn, comm=None):
    T, D = dyb.shape
    R = w.shape[0]

    def epi(acc, e_refs, out_refs, i, j, extra):
        gbv = e_refs[1][...].astype(F32)
        ge, th = _gelu(gbv)
        out_refs[0][...] = acc * ge
        out_refs[1][...] = (acc * e_refs[0][...].astype(F32) * _gelu_grad(gbv, th)).astype(BF16)

    t = _bs((tm, tn), lambda i, j, k: (i, j))
    tz = _bs((tm, tn), lambda i, j, k: (i, gb_blk + j))
    return _mm("bwd_dp", "nt", (T // tm, R // tn, 1), [(dyb, _bs((tm, D), lambda i, j, k: (i, 0)))], _ident,
               (w, _bs((tn, D), lambda i, j, k: (j, 0))), [(h, t), (z, tz)], epi,
               [(_sds((T, R), F32), t), (_sds(dz.shape, BF16), tz)], None, cache_a=None, alias=[(dz, 1)], comm=comm)


CONV_ROWS = 32


def _shifted_taps(x, halo, shifts, fn):
    n = CONV_ROWS + halo
    by_r = {}
    for k, s in shifts:
        by_r.setdefault(s % 8, []).append((k, s))
    for r in sorted(by_r):
        xr = x if r == 0 else pltpu.roll(x, n - r, 0)
        for k, s in by_r[r]:
            q = s - r
            fn(k, xr[q:q + CONV_ROWS])


def _conv_fwd(name, z, blk0, gate_blk0, w_pad, bias, taps, seq, tc, out_dtype, comm=None):
    T = z.shape[0]
    C = w_pad.shape[1]
    nb, nj = T // seq, C // tc
    pad = 8 * ((taps - 1 + 7) // 8)
    halo = pad
    shifts = [(k, pad - (taps - 1) + k) for k in range(taps)]
    glu = gate_blk0 is not None

    def body(*refs):
        if glu:
            v_ref, g_ref, w_ref, b_ref, o_ref, p_ref = refs
        else:
            v_ref, w_ref, b_ref, o_ref, p_ref = refs
        p_ref[pl.ds(0, pad), :] = jnp.zeros((pad, tc), F32)
        u = v_ref[...].astype(F32)
        if glu:
            u = u * _sig(g_ref[...].astype(F32))
        p_ref[pl.ds(pad, seq), :] = u

        def step(c, _):
            base = pl.multiple_of(c * CONV_ROWS, CONV_ROWS)
            x = p_ref[pl.ds(base, CONV_ROWS + halo), :]
            acc = [jnp.zeros((CONV_ROWS, tc), F32) + b_ref[...]]

            def tap(k, xs):
                acc[0] = acc[0] + w_ref[k:k + 1, :] * xs

            _shifted_taps(x, halo, shifts, tap)
            o_ref[pl.ds(base, CONV_ROWS), :] = acc[0].astype(out_dtype)
            return 0

        lax.fori_loop(0, seq // CONV_ROWS, step, 0)

    zin = [(z, _bs((seq, tc), lambda b, j: (b, blk0 + j)))]
    if glu:
        zin.append((z, _bs((seq, tc), lambda b, j: (b, gate_blk0 + j))))
    ins = zin + [(w_pad, _bs((w_pad.shape[0], tc), lambda b, j: (0, j))), (bias, _bs((1, tc), lambda b, j: (0, j)))]
    r = _call_with_comm(name, body, (nb, nj), [a for a, _ in ins], [s for _, s in ins],
                        [_bs((seq, tc), lambda b, j: (b, j))], [_sds((T, C), out_dtype)],
                        [pltpu.VMEM((seq + pad, tc), F32)], comm)
    return (r[0][0], r[1]) if comm else r[0]


def _conv_bwd(name, dy, z, dz, blk0, gate_blk0, w_pad, taps, seq, tc, comm=None):
    T = z.shape[0]
    C = w_pad.shape[1]
    nb, nj = T // seq, C // tc
    kp = w_pad.shape[0]
    pad = 8 * ((taps - 1 + 7) // 8)
    halo = pad
    sh_du = [(k, taps - 1 - k) for k in range(taps)]
    sh_dw = [(k, pad - (taps - 1) + k) for k in range(taps)]
    glu = gate_blk0 is not None

    def body(*refs):
        if glu:
            dy_ref, v_ref, g_ref, w_ref, _dz_in, dz_out, dw_ref, db_ref, pdy, pu, du_s, wacc, ob, ob2, osem = refs
        else:
            dy_ref, v_ref, w_ref, _dz_in, dz_out, dw_ref, db_ref, pdy, pu, du_s, wacc, ob, osem = refs
        j = pl.program_id(0)
        b = pl.program_id(1)
        pdy[pl.ds(seq, pad), :] = jnp.zeros((pad, tc), F32)
        pdy[pl.ds(0, seq), :] = dy_ref[...].astype(F32)
        pu[pl.ds(0, pad), :] = jnp.zeros((pad, tc), F32)
        v = v_ref[...].astype(F32)
        if glu:
            sg = _sig(g_ref[...].astype(F32))
            pu[pl.ds(pad, seq), :] = v * sg
        else:
            pu[pl.ds(pad, seq), :] = v
        wacc[...] = jnp.zeros(wacc.shape, F32)

        def step(c, dbacc):
            base = pl.multiple_of(c * CONV_ROWS, CONV_ROWS)
            xdy = pdy[pl.ds(base, CONV_ROWS + halo), :]
            acc = [jnp.zeros((CONV_ROWS, tc), F32)]

            def tap(k, xs):
                acc[0] = acc[0] + w_ref[k:k + 1, :] * xs

            _shifted_taps(xdy, halo, sh_du, tap)
            du_s[pl.ds(base, CONV_ROWS), :] = acc[0]
            dyc = xdy[0:CONV_ROWS]
            xu = pu[pl.ds(base, CONV_ROWS + halo), :]

            def wtap(k, xs):
                p = dyc * xs
                s8 = p[0:8]
                for m in range(1, CONV_ROWS // 8):
                    s8 = s8 + p[8 * m:8 * m + 8]
                wacc[pl.ds(8 * k, 8), :] += s8

            _shifted_taps(xu, halo, sh_dw, wtap)
            d8 = dyc[0:8]
            for m in range(1, CONV_ROWS // 8):
                d8 = d8 + dyc[8 * m:8 * m + 8]
            return dbacc + d8

        dbacc = lax.fori_loop(0, seq // CONV_ROWS, step, jnp.zeros((8, tc), F32))
        du = du_s[...]
        rows = pl.ds(pl.multiple_of(b * seq, seq), seq)
        puts = [pltpu.make_async_copy(ob, dz_out.at[rows, pl.ds(pl.multiple_of((blk0 + j) * tc, tc), tc)], osem.at[0])]
        if glu:
            puts.append(pltpu.make_async_copy(
                ob2, dz_out.at[rows, pl.ds(pl.multiple_of((gate_blk0 + j) * tc, tc), tc)], osem.at[1]))

        @pl.when((j > 0) | (b > 0))
        def _():
            for cp in puts:
                cp.wait()

        if glu:
            ob[...] = (du * sg).astype(BF16)
            ob2[...] = (du * v * sg * (1.0 - sg)).astype(BF16)
        else:
            ob[...] = du.astype(BF16)
        for cp in puts:
            cp.start()

        @pl.when((j == nj - 1) & (b == nb - 1))
        def _():
            for cp in puts:
                cp.wait()
        dw = jnp.sum(wacc[...].reshape(kp, 8, tc), axis=1)
        _acc_rows(dw_ref, dw, b == 0)
        _acc_rows(db_ref, jnp.sum(dbacc, axis=0, keepdims=True), b == 0)

    zin = [(z, _bs((seq, tc), lambda j, b: (b, blk0 + j)))]
    if glu:
        zin.append((z, _bs((seq, tc), lambda j, b: (b, gate_blk0 + j))))
    ins = [(dy, _bs((seq, tc), lambda j, b: (b, j)))] + zin + [(w_pad, _bs((kp, tc), lambda j, b: (0, j))),
                                                               (dz, pl.BlockSpec(memory_space=pl.ANY))]
    dz_idx = len(ins) - 1
    out_specs = [pl.BlockSpec(memory_space=pl.ANY), _bs((kp, tc), lambda j, b: (0, j)), _bs((1, tc), lambda j, b: (0, j))]
    out_shape = [_sds(dz.shape, dz.dtype), _sds((kp, C), F32), _sds((1, C), F32)]
    stage = [pltpu.VMEM((seq, tc), BF16)] * (2 if glu else 1) + [pltpu.SemaphoreType.DMA((2,))]
    return _call_with_comm(
        name, body, (nj, nb), [a for a, _ in ins], [s for _, s in ins], out_specs, out_shape,
        [pltpu.VMEM((seq + pad, tc), F32), pltpu.VMEM((seq + pad, tc), F32),
         pltpu.VMEM((seq, tc), F32), pltpu.VMEM((8 * kp, tc), F32)] + stage, comm, aliases={dz_idx: 0})


RG_ROWS = 256


def _softplus_neg(lam):
    return jnp.maximum(-lam, 0.0) + jnp.log(1.0 + jnp.exp(-jnp.abs(lam)))


def _gates(v0c, wa_ref, wx_ref, ba, bx, sp):
    vb = v0c.astype(BF16)
    r = _sig(jnp.dot(vb, wa_ref[...], preferred_element_type=F32) + ba)
    i = _sig(jnp.dot(vb, wx_ref[...], preferred_element_type=F32) + bx)
    return r, i, -LRU_C * r * sp


def _decay(la, first_row):
    a = jnp.exp(la)
    a2 = a * a
    x = 2.0 * la
    series = -x * (1.0 + x * (1.0 / 2) * (1.0 + x * (1.0 / 3) * (1.0 + x * (1.0 / 4) * (1.0 + x * (1.0 / 5)))))
    mult = jnp.sqrt(jnp.where(x > -0.1, series, 1.0 - a2))
    dmult = jnp.where(first_row, 0.0, -a2 / mult)
    mult = jnp.where(first_row, 1.0, mult)
    return a, mult, dmult


def _group_scan(a, b, reverse):
    n = a.shape[0]
    row = lax.broadcasted_iota(jnp.int32, a.shape, 0) & 7
    for d in (1, 2, 4):
        sh = n - d if reverse else d
        a_s, b_s = pltpu.roll(a, sh, 0), pltpu.roll(b, sh, 0)
        m = (row < 8 - d) if reverse else (row >= d)
        b = jnp.where(m, a * b_s + b, b)
        a = jnp.where(m, a * a_s, a)
    return a, b


def _group_carry(a_s, b_s, o_s, n_groups, reverse):
    cols = a_s.shape[1]

    def step(g, carry):
        g = n_groups - 1 - g if reverse else g
        rows = pl.ds(pl.multiple_of(g * 8, 8), 8)
        o = a_s[rows, :] * carry + b_s[rows, :]
        o_s[rows, :] = o
        return o[0:1, :] if reverse else o[7:8, :]

    lax.fori_loop(0, n_groups, step, jnp.zeros((1, cols), F32))


def _rglru_fwd(v0, wa, wx, ba, bx, lam, seq, comm=None):
    T, C = v0.shape
    ng, G = wa.shape[0], wa.shape[1]
    nb = T // seq

    def body(v_ref, wa_ref, wx_ref, ba_ref, bx_ref, lam_ref, h_ref, r_ref, i_ref, la_ref, a_s, b_s, h_s):
        sp = _softplus_neg(lam_ref[...])

        def chunk(c, _):
            rows = pl.ds(pl.multiple_of(c * RG_ROWS, RG_ROWS), RG_ROWS)
            t = lax.broadcasted_iota(jnp.int32, (RG_ROWS, G), 0) + c * RG_ROWS
            v0c = v_ref[rows, :]
            r, i, la = _gates(v0c, wa_ref, wx_ref, ba_ref[...], bx_ref[...], sp)
            r_ref[rows, :] = r.astype(BF16)
            i_ref[rows, :] = i.astype(BF16)
            la_ref[rows, :] = la
            a, mult, _ = _decay(la, t == 0)
            a_g, b_g = _group_scan(a, mult * i * v0c, False)
            a_s[rows, :] = a_g
            b_s[rows, :] = b_g
            return 0

        lax.fori_loop(0, seq // RG_ROWS, chunk, 0)
        _group_carry(a_s, b_s, h_s, seq // 8, False)
        h_ref[...] = h_s[...].astype(BF16)

    t2 = _bs((seq, G), lambda b, g: (b, g))
    wsp = _bs((None, G, G), lambda b, g: (g, 0, 0))
    row = _bs((1, G), lambda b, g: (0, g))
    return _call_with_comm("rglru_fwd", body, (nb, ng), [v0, wa, wx, ba, bx, lam], [t2, wsp, wsp, row, row, row],
                           [t2] * 4, [_sds((T, C), BF16)] * 3 + [_sds((T, C), F32)], [pltpu.VMEM((seq, G), F32)] * 3, comm)


def _call_with_comm(name, body, grid, ins, in_specs, out_specs, out_shape, scratch, comm, aliases=None):
    n_in, n_out, n_s = len(ins), len(out_shape), len(scratch)
    c_ins, c_outs, c_sems = (comm.ins, comm.outs, comm.sems) if comm else ([], [], [])

    def wrapped(*refs):
        o0 = n_in + len(c_ins)
        s0 = o0 + n_out + len(c_outs)
        cin, cout, csem = refs[n_in:o0], refs[o0 + n_out:s0], refs[s0 + n_s:]
        ids = [pl.program_id(a) for a in range(len(grid))]
        if comm:
            first = _all_of([i == 0 for i in ids])

            @pl.when(first)
            def _():
                comm.start(cin, cout, csem)

        body(*refs[:n_in], *refs[o0:o0 + n_out], *refs[s0:s0 + n_s])
        if comm:
            last = _all_of([i == n - 1 for i, n in zip(ids, grid)])

            @pl.when(last)
            def _():
                comm.finish(cin, cout, csem)

    res = pl.pallas_call(
        wrapped, name=name, grid=grid, in_specs=list(in_specs) + [ANY] * len(c_ins),
        out_specs=list(out_specs) + [ANY] * len(c_outs), out_shape=list(out_shape) + list(c_outs),
        scratch_shapes=list(scratch) + list(c_sems), input_output_aliases=aliases or {},
        compiler_params=_cp(dimension_semantics=("arbitrary",) * len(grid), has_side_effects=bool(comm)),
    )(*ins, *c_ins)
    return (list(res[:n_out]), list(res[n_out:])) if comm else list(res)


def _all_of(conds):
    out = conds[0]
    for c in conds[1:]:
        out = out & c
    return out


def _rglru_bwd(v0, h, dh, r_g, i_g, la_g, wa, wx, lam, seq, comm=None):
    T, C = v0.shape
    ng, G = wa.shape[0], wa.shape[1]
    nb = T // seq
    R = RG_ROWS

    def body(v_ref, h_ref, dh_ref, r_ref, i_ref, la_ref, wa_ref, wx_ref, lam_ref,
             dv_ref, dwa_ref, dwx_ref, dba_ref, dbx_ref, dlam_ref, a_s, b_s, q_s, hp_s):
        b = pl.program_id(1)
        lam_v = lam_ref[...]
        sp = _softplus_neg(lam_v)
        dsp_dlam = -_sig(-lam_v)

        @pl.when(b == 0)
        def _():
            dwa_ref[...] = jnp.zeros((G, G), F32)
            dwx_ref[...] = jnp.zeros((G, G), F32)
            dba_ref[...] = jnp.zeros((1, G), F32)
            dbx_ref[...] = jnp.zeros((1, G), F32)
            dlam_ref[...] = jnp.zeros((1, G), F32)

        hp_s[pl.ds(0, 8), :] = jnp.zeros((8, G), F32)
        hp_s[pl.ds(8, seq), :] = h_ref[...].astype(F32)
        q_s[pl.ds(seq, 8), :] = jnp.zeros((8, G), F32)

        def chunk1(c, _):
            rows = pl.ds(pl.multiple_of(c * R, R), R)
            a = jnp.exp(la_ref[rows, :])
            a_g, b_g = _group_scan(a, a * dh_ref[rows, :].astype(F32), True)
            a_s[rows, :] = a_g
            b_s[rows, :] = b_g
            return 0

        lax.fori_loop(0, seq // R, chunk1, 0)
        _group_carry(a_s, b_s, q_s, seq // 8, True)

        def chunk3(c, _):
            base = pl.multiple_of(c * R, R)
            rows = pl.ds(base, R)
            t = lax.broadcasted_iota(jnp.int32, (R, G), 0) + c * R
            v0c = v_ref[rows, :]
            r, i = r_ref[rows, :].astype(F32), i_ref[rows, :].astype(F32)
            a, mult, dmult_dla = _decay(la_ref[rows, :], t == 0)
            q_next = pltpu.roll(q_s[pl.ds(base, R + 8), :], R + 7, 0)[0:R]
            h_prev = pltpu.roll(hp_s[pl.ds(base, R + 8), :], R + 1, 0)[0:R]
            gt = dh_ref[rows, :].astype(F32) + q_next
            dla = gt * h_prev * a + gt * i * v0c * dmult_dla
            dpa = dla * (-LRU_C * sp) * r * (1.0 - r)
            dpx = gt * mult * v0c * i * (1.0 - i)
            dpa_b, dpx_b, v_b = dpa.astype(BF16), dpx.astype(BF16), v0c.astype(BF16)
            dv_ref[rows, :] = (gt * mult * i
                               + lax.dot_general(dpa_b, wa_ref[...], _DIMS["nt"], preferred_element_type=F32)
                               + lax.dot_general(dpx_b, wx_ref[...], _DIMS["nt"], preferred_element_type=F32))
            dwa_ref[...] += lax.dot_general(v_b, dpa_b, _DIMS["tn"], preferred_element_type=F32)
            dwx_ref[...] += lax.dot_general(v_b, dpx_b, _DIMS["tn"], preferred_element_type=F32)
            dba_ref[...] += jnp.sum(dpa, axis=0, keepdims=True)
            dbx_ref[...] += jnp.sum(dpx, axis=0, keepdims=True)
            dlam_ref[...] += jnp.sum(dla * (-LRU_C * r), axis=0, keepdims=True) * dsp_dlam
            return 0

        lax.fori_loop(0, seq // R, chunk3, 0)

    t2 = _bs((seq, G), lambda g, b: (b, g))
    wsp = _bs((None, G, G), lambda g, b: (g, 0, 0))
    row = _bs((1, G), lambda g, b: (0, g))
    return _call_with_comm(
        "rglru_bwd", body, (ng, nb), [v0, h, dh, r_g, i_g, la_g, wa, wx, lam], [t2] * 6 + [wsp, wsp, row],
        [t2, wsp, wsp, row, row, row],
        [_sds((T, C), F32), _sds((ng, G, G), F32), _sds((ng, G, G), F32),
         _sds((1, C), F32), _sds((1, C), F32), _sds((1, C), F32)],
        [pltpu.VMEM((seq, G), F32), pltpu.VMEM((seq, G), F32),
         pltpu.VMEM((seq + 8, G), F32), pltpu.VMEM((seq + 8, G), F32)], comm)


TC_A = 256
TC_B = 512
TAPS_A, TAPS_B = 31, 4


def _tiles(T):
    return min(512, T), min(1024, T)


GATHERED = ("w_in", "w_1", "w_a_out", "w_b_out", "w_o", "w_2", "caw", "cbw")
GATHER_KIND = {"w_in": (True, True), "w_1": (True, True), "w_a_out": (False, True), "w_b_out": (False, True),
               "w_o": (False, True), "w_2": (False, True), "caw": (True, False), "cbw": (True, False)}


def _layer_fwd(x, p, seq, cur=None, nxt=None):
    T, D = x.shape
    C, R = p["ln_g"].shape[1], p["lam"].shape[1]
    tm, tl = _tiles(T)
    gb_blk, sa_blk = (2 * C + R) // TC_B, (2 * C + 2 * R) // D
    p, ahead = dict(p), {}

    def gather(src, names):
        return None if src is None else _gather_comm([src[n] for n in names], [GATHER_KIND[n] for n in names])

    def outs(r, src, names, into):
        if src is None:
            return r
        into.update(zip(names, r[1]))
        return r[0]

    mid = ["w_a_out", "w_b_out", "w_o", "caw", "cbw"]
    z, h = outs(_fwd_norm_mm("fwd_z", x, p["g_mix"], p["w_in"], p["b_in"], tl, 1024, comm=gather(cur, mid)), cur, mid, p)
    u1 = outs(_conv_fwd("conv_a_fwd", z, 0, C // TC_A, p["caw"], p["cab"], TAPS_A, seq, TC_A, BF16,
                        comm=gather(cur, ["w_1"])), cur, ["w_1"], p)
    ya = _fwd_ya(u1, p["ln_g"], p["ln_b"], p["w_a_out"], tm)
    v0 = _conv_fwd("conv_b_fwd", z, 2 * C // TC_B, None, p["cbw"], p["cbb"], TAPS_B, seq, TC_B, F32)
    hr, rg, ig, lag = outs(_rglru_fwd(v0, p["wa"], p["wx"], p["b_rg_a"], p["b_rg_x"], p["lam"], seq,
                                      comm=gather(nxt, ["w_in"])), nxt, ["w_in"], ahead)
    yb = _fwd_yb(hr, z, gb_blk, p["w_b_out"], tl, TC_B)
    x1 = _fwd_x1(x, ya, yb, z, sa_blk, p["w_o"], tm)
    fp, h2 = outs(_fwd_norm_mm("fwd_f", x1, p["g_mlp"], p["w_1"], None, tl, 1024, comm=gather(cur, ["w_2"])),
                  cur, ["w_2"], p)
    x2 = _fwd_x2(x1, fp, p["w_2"], tm, fp.shape[1])
    saved = dict(x=x, z=z, h=h, u1=u1, ya=ya, v0=v0, hr=hr, rg=rg, ig=ig, lag=lag, yb=yb, x1=x1, fp=fp, h2=h2)
    return x2, saved, p, ahead.get("w_in")


class _Reduce:
    EARLY = ("w_2", "w_1", "w_o", "w_a_out")
    LATE = ("w_b_out", "w_in")

    def __init__(self, accs, c_arr, kcl_of):
        self.accs, self.c_arr, self.kcl_of, self.late = accs, c_arr, kcl_of, None

    @staticmethod
    def pieces(partials):
        return [a if a.ndim == 4 else a.reshape(N_CHIPS, 2, a.shape[0] // (2 * N_CHIPS), a.shape[1]) for a in partials]

    def chip_sums(self, pgs, swapped):
        return _sum_siblings(pgs, swapped, self.c_arr)

    def finish(self, names, sums, received, layer):
        done = _sum_chips(sums, received, self.kcl_of(layer), [self.accs[n] for n in names])
        self.accs.update(zip(names, done))


def _layer_bwd(dx2, dx2b, p, s, seq, red=None, layer=0):
    T, D = dx2.shape
    C, R = p["ln_g"].shape[1], p["lam"].shape[1]
    tm, tl = _tiles(T)
    gb_blk, sa_blk = (2 * C + R) // TC_B, (2 * C + 2 * R) // D
    z = s["z"]
    g = {}


    late_sums = None
    if red is not None and red.late is not None:
        late, red.late = red.late, None
        dfp, got = _bwd_df(dx2b, p["w_2"], s["fp"], tl, 1024, comm=_swap_comm(late))
        late_sums = red.chip_sums(late, got)
    else:
        dfp = _bwd_df(dx2b, p["w_2"], s["fp"], tl, 1024)
    g["w_2"] = _bwd_dw("bwd_dw2", s["fp"], dx2b, 1024, D, T, a_fn=_relu2, keep="dy")
    dx1, dx1b, g["g_mlp"] = _bwd_norm("bwd_dh2", dfp, p["w_1"], s["x1"], p["g_mlp"], dx2, tm, dfp.shape[1])
    g["w_1"] = _bwd_dw("bwd_dw1", s["h2"], dfp, D, 1024, T, shard_cols=dfp.shape[1] // N_CHIPS, keep="act")

    dya, dyb, dz = _bwd_dm(dx1b, p["w_o"], s["ya"], s["yb"], z, sa_blk, tm)

    def merged(a_refs, out_refs, i, j, k):
        ya_, yb_, sa_, sb_ = (r[...].astype(F32) for r in a_refs)
        return (_sig(sa_) * ya_ + _sig(sb_) * yb_).astype(BF16)

    tk = _bs((tm, D), lambda i, j, k: (k, 0))
    g["w_o"] = _bwd_dw("bwd_dwo", s["ya"], dx1b, D, D, tm, a_fn=merged,
                       a_extra=[(s["yb"], tk), (z, _bs((tm, D), lambda i, j, k: (k, sa_blk))),
                                (z, _bs((tm, D), lambda i, j, k: (k, sa_blk + 1)))])

    du1, g["ln_g"], g["ln_b"] = _bwd_du3(dya, p["w_a_out"], s["u1"], p["ln_g"], p["ln_b"], tm)

    def act_a(a_refs, out_refs, i, j, k):
        return _ln_silu(a_refs[0][...].astype(F32), a_refs[1][...], a_refs[2][...])[0].astype(BF16)

    rowc = _bs((1, C), lambda i, j, k: (0, 0))
    g["w_a_out"] = _bwd_dw("bwd_dwa", s["u1"], dya, C, D, tm, a_fn=act_a,
                           a_extra=[(p["ln_g"], rowc), (p["ln_b"], rowc)])
    conv_a_args = ("conv_a_bwd", du1, z, dz, 0, C // TC_A, p["caw"], TAPS_A, seq, TC_A)
    if late_sums is not None:
        (dz, g["caw"], g["cab"]), got = _conv_bwd(*conv_a_args, comm=_scatter_comm(late_sums))
        red.finish(red.LATE, late_sums, got, layer + 1)
    else:
        dz, g["caw"], g["cab"] = _conv_bwd(*conv_a_args)

    dp_args = (dyb, p["w_b_out"], s["hr"], z, dz, gb_blk, tl, TC_B)
    if red is not None:
        early = red.pieces([g.pop(n) for n in red.EARLY])
        (dhr, dz), got = _bwd_dp(*dp_args, comm=_swap_comm(early))
        early_sums = red.chip_sums(early, got)
    else:
        dhr, dz = _bwd_dp(*dp_args)

    def act_b(a_refs, out_refs, i, j, k):
        ge, _ = _gelu(a_refs[1][...].astype(F32))
        return (a_refs[0][...].astype(F32) * ge).astype(BF16)

    tb = min(1024, T)
    g["w_b_out"] = _bwd_dw("bwd_dwb", s["hr"], dyb, TC_B, D, tb, a_fn=act_b,
                           a_extra=[(z, _bs((tb, TC_B), lambda i, j, k: (k, gb_blk + i)))])
    rg_args = (s["v0"], s["hr"], dhr, s["rg"], s["ig"], s["lag"], p["wa"], p["wx"], p["lam"], seq)
    if red is not None:
        rg_out, got = _rglru_bwd(*rg_args, comm=_scatter_comm(early_sums))
        red.finish(red.EARLY, early_sums, got, layer)
    else:
        rg_out = _rglru_bwd(*rg_args)
    dv0, g["wa"], g["wx"], g["b_rg_a"], g["b_rg_x"], g["lam"] = rg_out
    dz, g["cbw"], g["cbb"] = _conv_bwd("conv_b_bwd", dv0, z, dz, 2 * C // TC_B, None, p["cbw"], TAPS_B, seq, TC_B)

    dx, dxb, g["g_mix"], dbin = _bwd_norm("bwd_dh", dz, p["w_in"], s["x"], p["g_mix"], dx1, tm, dz.shape[1],
                                          colsum=True)
    g["b_in"] = dbin.reshape(1, -1)
    ns = dz.shape[1] // N_CHIPS
    g["w_in"] = _bwd_dw("bwd_dwin", s["h"], dz, D, ns // 2, T, shard_cols=ns, keep="act")
    if red is not None:
        red.late = red.pieces([g.pop(n) for n in red.LATE])
    return dx, dxb, g


ANY = pl.BlockSpec(memory_space=pl.ANY)


def _mesh_pos():
    return lax.axis_index("x"), lax.axis_index("y"), lax.axis_index("c")


def _other_chips(x, y):
    return [(1 - x, y), (x, 1 - y), (1 - x, 1 - y)]


def _remote(src, dst, ssem, rsem, dev):
    return pltpu.make_async_remote_copy(src_ref=src, dst_ref=dst, send_sem=ssem, recv_sem=rsem,
                                        device_id=dev, device_id_type=MESH)


def _gather_region(src, dst, by_cols, k, half):
    rows, cols = src.shape
    nr = rows if half is None else rows // 2
    r0 = 0 if half is None else half * nr
    if by_cols:
        return dst.at[pl.ds(r0, nr), pl.ds(pl.multiple_of(k * cols, 128), cols)]
    return dst.at[pl.ds(pl.multiple_of(k * rows + r0, 8), nr), :]


def _gather_sends(src, dst, kinds, send, recv):
    x, y, c = _mesh_pos()
    cps = []
    for t in range(len(src)):
        half = c if kinds[t][1] else None
        hr = src[t].shape[0] // 2
        s_ref = src[t].at[pl.ds(c * hr, hr), :] if kinds[t][1] else src[t]
        for j, chip in enumerate(_other_chips(x, y)):
            cps.append(_remote(s_ref, _gather_region(src[t], dst[t], kinds[t][0], 2 * x + y, half),
                               send.at[t, j], recv.at[t, j], (*chip, c)))
    return cps


def _gather_finish(src, dst, kinds, send, recv, fsend, frecv):
    x, y, c = _mesh_pos()
    chips = _other_chips(x, y)
    sib = (x, y, 1 - c)
    n = len(src)
    fwd = []
    for t in range(n):
        half = c if kinds[t][1] else None
        for j, chip in enumerate(chips):
            got = _gather_region(src[t], dst[t], kinds[t][0], 2 * chip[0] + chip[1], half)
            _remote(got, got, send.at[t, j], recv.at[t, j], (*chip, c)).wait_recv()
            if kinds[t][1]:
                cp = _remote(got, got, fsend.at[t, j], frecv.at[t, j], sib)
                cp.start()
                fwd.append(cp)
    for t in range(n):
        if kinds[t][1]:
            for j, chip in enumerate(chips):
                got = _gather_region(src[t], dst[t], kinds[t][0], 2 * chip[0] + chip[1], 1 - c)
                _remote(got, got, fsend.at[t, j], frecv.at[t, j], sib).wait_recv()
    for cp in _gather_sends(src, dst, kinds, send, recv) + fwd:
        cp.wait_send()


def _gather_sems(n):
    sem = pltpu.SemaphoreType.DMA
    return [sem((n, 3)), sem((n, 3)), sem((n, 3)), sem((n, 3))]


def _gather_comm(shards, kinds):
    n = len(shards)

    def whole(s, by_cols):
        return (s.shape[0], N_CHIPS * s.shape[1]) if by_cols else (N_CHIPS * s.shape[0], s.shape[1])

    def own(src, dst, lsem):
        x, y, _ = _mesh_pos()
        return [pltpu.make_async_copy(src[t], _gather_region(src[t], dst[t], kinds[t][0], 2 * x + y, None), lsem.at[t])
                for t in range(n)]

    def start(src, dst, sems):
        for cp in own(src, dst, sems[4]) + _gather_sends(src, dst, kinds, sems[0], sems[1]):
            cp.start()

    def finish(src, dst, sems):
        _gather_finish(src, dst, kinds, *sems[:4])
        for cp in own(src, dst, sems[4]):
            cp.wait()

    return _Comm(shards, [_sds(whole(s, k[0]), s.dtype) for s, k in zip(shards, kinds)],
                 _gather_sems(n) + [pltpu.SemaphoreType.DMA((n,))], start, finish)


def _scatter_comm(ps):
    n = len(ps)

    def copies(src, dst, sems):
        x, y, c = _mesh_pos()
        return [_remote(src[t].at[2 * chip[0] + chip[1]], dst[t].at[j], sems[0].at[t, j], sems[1].at[t, j], (*chip, c))
                for t in range(n) for j, chip in enumerate(_other_chips(x, y))]

    def start(src, dst, sems):
        for cp in copies(src, dst, sems):
            cp.start()

    def finish(src, dst, sems):
        cps = copies(src, dst, sems)
        for cp in cps:
            cp.wait_recv()
        for cp in cps:
            cp.wait_send()

    sem = pltpu.SemaphoreType.DMA
    return _Comm(ps, [_sds((3,) + a.shape[1:], a.dtype) for a in ps], [sem((n, 3)), sem((n, 3))], start, finish)


def _comm_call(name, comm):
    n_i, n_o = len(comm.ins), len(comm.outs)

    def body(*refs):
        comm.start(refs[:n_i], refs[n_i:n_i + n_o], refs[n_i + n_o:])
        comm.finish(refs[:n_i], refs[n_i:n_i + n_o], refs[n_i + n_o:])

    return pl.pallas_call(
        body, name=name, in_specs=[ANY] * n_i, out_specs=[ANY] * n_o, out_shape=comm.outs, scratch_shapes=comm.sems,
        compiler_params=_cp(has_side_effects=True),
    )(*comm.ins)


def _swap_comm(pgs):
    n = len(pgs)

    def copies(src, dst, sems):
        x, y, c = _mesh_pos()
        return [_remote(src[t].at[:, 1 - c], dst[t], sems[0].at[t], sems[1].at[t], (x, y, 1 - c)) for t in range(n)]

    def start(src, dst, sems):
        for cp in copies(src, dst, sems):
            cp.start()

    def finish(src, dst, sems):
        cps = copies(src, dst, sems)
        for cp in cps:
            cp.wait_recv()
        for cp in cps:
            cp.wait_send()

    sem = pltpu.SemaphoreType.DMA
    return _Comm(pgs, [_sds((a.shape[0],) + a.shape[2:], a.dtype) for a in pgs], [sem((n,)), sem((n,))], start, finish)


def _join_halves(accs):
    n = len(accs)

    def body(*refs):
        buf = refs[n:2 * n]
        send, recv = refs[2 * n:]
        x, y, c = _mesh_pos()
        cps = [_remote(buf[t].at[:, c], buf[t].at[:, c], send.at[t], recv.at[t], (x, y, 1 - c)) for t in range(n)]
        for cp in cps:
            cp.start()
        for t in range(n):
            _remote(buf[t].at[:, c], buf[t].at[:, 1 - c], send.at[t], recv.at[t], (x, y, 1 - c)).wait_recv()
        for cp in cps:
            cp.wait_send()

    sem = pltpu.SemaphoreType.DMA
    return pl.pallas_call(
        body, name="join_halves", in_specs=[ANY] * n, out_specs=[ANY] * n,
        out_shape=[_sds(a.shape, a.dtype) for a in accs], scratch_shapes=[sem((n,)), sem((n,))],
        input_output_aliases={t: t for t in range(n)}, compiler_params=_cp(has_side_effects=True),
    )(*accs)


def _sum_siblings(pgs, rbs, c_arr):
    n = len(pgs)
    nk = pgs[0].shape[0]

    def body(c_ref, *refs):
        for t in range(n):
            refs[2 * n + t][...] = (refs[t][...].astype(F32) + refs[n + t][...].astype(F32)).astype(BF16)

    half = lambda a: pl.BlockSpec((None,) + a.shape[2:], lambda k, c_ref: (k, 0, 0))
    return pl.pallas_call(
        body, name="sum_siblings",
        grid_spec=pltpu.PrefetchScalarGridSpec(
            num_scalar_prefetch=1, grid=(nk,),
            in_specs=[pl.BlockSpec((None, None) + a.shape[2:], lambda k, c_ref: (k, c_ref[0], 0, 0)) for a in pgs]
            + [half(a) for a in pgs],
            out_specs=[half(a) for a in pgs]),
        out_shape=[_sds((nk,) + a.shape[2:], BF16) for a in pgs],
        compiler_params=_cp(dimension_semantics=("arbitrary",)),
    )(c_arr, *pgs, *rbs)


def _sum_chips(ps, rbs, kcl, accs):
    n = len(ps)

    def body(k_ref, *refs):
        for t in range(n):
            b_ref = refs[n + t]
            refs[3 * n + t][...] = (refs[t][...].astype(F32) + b_ref[0].astype(F32) + b_ref[1].astype(F32)
                                    + b_ref[2].astype(F32))

    qr = lambda a: (a.shape[1] // 2, a.shape[2])
    return pl.pallas_call(
        body, name="sum_chips",
        grid_spec=pltpu.PrefetchScalarGridSpec(
            num_scalar_prefetch=1, grid=(2,),
            in_specs=[pl.BlockSpec((None,) + qr(a), lambda r, k_ref: (k_ref[0], r, 0)) for a in ps]
            + [pl.BlockSpec((3,) + qr(a), lambda r, k_ref: (0, r, 0)) for a in ps] + [ANY] * n,
            out_specs=[pl.BlockSpec((None, None) + qr(a), lambda r, k_ref: (k_ref[2], k_ref[1], r, 0)) for a in ps]),
        out_shape=[_sds(a.shape, F32) for a in accs], input_output_aliases={1 + 2 * n + t: t for t in range(n)},
        compiler_params=_cp(dimension_semantics=("arbitrary",)),
    )(kcl, *ps, *rbs, *accs)


N_DEV = 8


def _allreduce_small(part):
    _, r, lanes = part.shape

    def body(p_ref, o_ref, rbuf, s1, r1, s2, r2):
        x, y, c = _mesh_pos()
        me = 4 * x + 2 * y + c
        devs = [(d // 4, (d // 2) % 2, d % 2) for d in range(N_DEV)]
        rbuf[me] = p_ref[me]

        def each_peer(fn):
            for d in range(N_DEV):
                @pl.when(d != me)
                def _():
                    fn(d)

        each_peer(lambda d: _remote(p_ref.at[d], rbuf.at[me], s1.at[d], r1.at[me], devs[d]).start())
        each_peer(lambda d: _remote(p_ref.at[d], rbuf.at[d], s1.at[d], r1.at[d], devs[d]).wait_recv())
        total = rbuf[0]
        for d in range(1, N_DEV):
            total = total + rbuf[d]
        o_ref[me] = total
        each_peer(lambda d: _remote(o_ref.at[me], o_ref.at[me], s2.at[d], r2.at[me], devs[d]).start())
        each_peer(lambda d: _remote(o_ref.at[d], o_ref.at[d], s2.at[d], r2.at[d], devs[d]).wait_recv())
        each_peer(lambda d: _remote(p_ref.at[d], rbuf.at[me], s1.at[d], r1.at[me], devs[d]).wait_send())
        each_peer(lambda d: _remote(o_ref.at[me], o_ref.at[me], s2.at[d], r2.at[me], devs[d]).wait_send())

    sem = pltpu.SemaphoreType.DMA
    vm = pl.BlockSpec(memory_space=pltpu.VMEM)
    return pl.pallas_call(
        body, name="allreduce_small", in_specs=[vm], out_specs=vm, out_shape=_sds(part.shape, F32),
        scratch_shapes=[pltpu.VMEM(part.shape, F32), sem((N_DEV,)), sem((N_DEV,)), sem((N_DEV,)), sem((N_DEV,))],
        compiler_params=_cp(has_side_effects=True),
    )(part)


BIG = ("w_in", "w_1", "w_a_out", "w_b_out", "w_o", "w_2")
BY_COLS = {"w_in": True, "w_1": True, "w_a_out": False, "w_b_out": False, "w_o": False, "w_2": False}
WEIGHTS = ("g_mix", "w_in", "b_in", "conv_a_w", "conv_a_b", "ln_g", "ln_b", "w_a_out", "conv_b_w", "conv_b_b", "w_rg_a",
           "b_rg_a", "w_rg_x", "b_rg_x", "lam", "w_b_out", "w_o", "g_mlp", "w_1", "w_2", "g_final")
SMALL = tuple(n for n in WEIGHTS if n not in BIG)
ADAM_ROWS = 256
ADAM_SMALL_ROWS = 2048


def _block_diag(w):
    nh, dh, _ = w.shape
    ng = nh // HEADS_PER_GROUP
    w4 = w.reshape(ng, HEADS_PER_GROUP, dh, dh)
    eye = jnp.eye(HEADS_PER_GROUP, dtype=w.dtype)
    return jnp.einsum("qhij,hk->qhikj", w4, eye).reshape(ng, HEADS_PER_GROUP * dh, HEADS_PER_GROUP * dh)


def _block_diag_part(d, dh):
    ng = d.shape[0]
    eye = jnp.eye(HEADS_PER_GROUP, dtype=d.dtype)
    d5 = d.reshape(ng, HEADS_PER_GROUP, dh, HEADS_PER_GROUP, dh)
    return jnp.einsum("qhikj,hk->qhij", d5, eye).reshape(ng * HEADS_PER_GROUP, dh, dh)


PACK_LANES = 128


def _pack(arrays, row_multiple):
    parts = [a.reshape(-1, PACK_LANES) for a in arrays]
    parts = [jnp.pad(p, ((0, -p.shape[0] % 8), (0, 0))) if p.shape[0] % 8 else p for p in parts]
    rows = sum(p.shape[0] for p in parts)
    pad = -rows % row_multiple
    if pad:
        parts.append(jnp.zeros((pad, PACK_LANES), parts[0].dtype))
    return jnp.concatenate(parts, axis=0)


def _unpack(buf, like):
    buf = buf.reshape(-1, PACK_LANES)
    out, off = [], 0
    for a in like:
        n = a.size // PACK_LANES
        out.append(buf[off:off + n].reshape(a.shape))
        off += n + (-n % 8)
    return out


def kernel(x, g_mix, w_in, b_in, conv_a_w, conv_a_b, ln_g, ln_b, w_a_out, conv_b_w, conv_b_b, w_rg_a, b_rg_a, w_rg_x, b_rg_x, lam, w_b_out, w_o, g_mlp, w_1, w_2, g_final, loss_target, m_g_mix, m_w_in, m_b_in, m_conv_a_w, m_conv_a_b, m_ln_g, m_ln_b, m_w_a_out, m_conv_b_w, m_conv_b_b, m_w_rg_a, m_b_rg_a, m_w_rg_x, m_b_rg_x, m_lam, m_w_b_out, m_w_o, m_g_mlp, m_w_1, m_w_2, m_g_final, v_g_mix, v_w_in, v_b_in, v_conv_a_w, v_conv_a_b, v_ln_g, v_ln_b, v_w_a_out, v_conv_b_w, v_conv_b_b, v_w_rg_a, v_b_rg_a, v_w_rg_x, v_b_rg_x, v_lam, v_w_b_out, v_w_o, v_g_mlp, v_w_1, v_w_2, v_g_final):
    w = dict(g_mix=g_mix, w_in=w_in, b_in=b_in, conv_a_w=conv_a_w, conv_a_b=conv_a_b, ln_g=ln_g, ln_b=ln_b, w_a_out=w_a_out,
             conv_b_w=conv_b_w, conv_b_b=conv_b_b, w_rg_a=w_rg_a, b_rg_a=b_rg_a, w_rg_x=w_rg_x, b_rg_x=b_rg_x, lam=lam,
             w_b_out=w_b_out, w_o=w_o, g_mlp=g_mlp, w_1=w_1, w_2=w_2, g_final=g_final)
    m = dict(g_mix=m_g_mix, w_in=m_w_in, b_in=m_b_in, conv_a_w=m_conv_a_w, conv_a_b=m_conv_a_b, ln_g=m_ln_g, ln_b=m_ln_b,
             w_a_out=m_w_a_out, conv_b_w=m_conv_b_w, conv_b_b=m_conv_b_b, w_rg_a=m_w_rg_a, b_rg_a=m_b_rg_a, w_rg_x=m_w_rg_x,
             b_rg_x=m_b_rg_x, lam=m_lam, w_b_out=m_w_b_out, w_o=m_w_o, g_mlp=m_g_mlp, w_1=m_w_1, w_2=m_w_2, g_final=m_g_final)
    v = dict(g_mix=v_g_mix, w_in=v_w_in, b_in=v_b_in, conv_a_w=v_conv_a_w, conv_a_b=v_conv_a_b, ln_g=v_ln_g, ln_b=v_ln_b,
             w_a_out=v_w_a_out, conv_b_w=v_conv_b_w, conv_b_b=v_conv_b_b, w_rg_a=v_w_rg_a, b_rg_a=v_b_rg_a, w_rg_x=v_w_rg_x,
             b_rg_x=v_b_rg_x, lam=v_lam, w_b_out=v_w_b_out, w_o=v_w_o, g_mlp=v_g_mlp, w_1=v_w_1, w_2=v_w_2, g_final=v_g_final)
    B, S, D = x.shape
    T = B * S
    L = w_in.shape[0]
    dh = w_rg_a.shape[-1]
    taps_a, taps_b = conv_a_w.shape[1], conv_b_w.shape[1]
    assert (taps_a, taps_b) == (TAPS_A, TAPS_B)
    xi, yi, ci = _mesh_pos()
    c_arr = jnp.reshape(ci, (1,)).astype(jnp.int32)
    k_me = 2 * xi + yi

    caw_p = jnp.pad(conv_a_w, ((0, 0), (0, 32 - taps_a), (0, 0)))
    cbw_p = jnp.pad(conv_b_w, ((0, 0), (0, 8 - taps_b), (0, 0)))
    row = lambda a: a.reshape(1, -1)

    def shards_of(l):
        d = {n: w[n][l].astype(BF16) for n in BIG}
        d.update(caw=caw_p[l], cbw=cbw_p[l])
        return d

    def params_of(l, w_in_whole):
        p = dict(w_in=w_in_whole, cab=row(conv_a_b[l]), cbb=row(conv_b_b[l]),
                 wa=_block_diag(w_rg_a[l]).astype(BF16), wx=_block_diag(w_rg_x[l]).astype(BF16))
        for n in ("g_mix", "b_in", "ln_g", "ln_b", "b_rg_a", "b_rg_x", "lam", "g_mlp"):
            p[n] = row(w[n][l])
        return p

    shards = [shards_of(l) for l in range(L)]
    w_in_whole, = _comm_call("gather_first", _gather_comm([shards[0]["w_in"]], [GATHER_KIND["w_in"]]))
    xf = x.reshape(T, D)
    saved, params = [], []
    for l in range(L):
        xf, s, p, w_in_whole = _layer_fwd(xf, params_of(l, w_in_whole), S, cur=shards[l],
                                          nxt=shards[l + 1] if l + 1 < L else None)
        saved.append(s)
        params.append(p)
    loss_part, dx, dxb, dg_final = _loss_head(xf, row(g_final), loss_target.reshape(T, D), _tiles(T)[0])
    loss = lax.psum(loss_part[0, 0], ("x", "y", "c"))

    half_shape = lambda a: (L, 2, a.shape[1] // 2, a.shape[2])
    accs = {n: lax.empty(half_shape(w[n]), F32) for n in BIG}
    small = {n: [None] * L for n in SMALL if n != "g_final"}
    red = _Reduce(accs, c_arr, lambda l: jnp.stack([k_me, ci, jnp.full((), l, ci.dtype)]).astype(jnp.int32))
    for l in reversed(range(L)):
        dx, dxb, g = _layer_bwd(dx, dxb, params[l], saved[l], S, red=red, layer=l)
        small["g_mix"][l], small["b_in"][l], small["g_mlp"][l] = g["g_mix"], g["b_in"], g["g_mlp"]
        small["conv_a_w"][l], small["conv_a_b"][l] = g["caw"], g["cab"]
        small["conv_b_w"][l], small["conv_b_b"][l] = g["cbw"], g["cbb"]
        small["ln_g"][l], small["ln_b"][l], small["lam"][l] = g["ln_g"], g["ln_b"], g["lam"]
        small["w_rg_a"][l], small["w_rg_x"][l] = _block_diag_part(g["wa"], dh), _block_diag_part(g["wx"], dh)
        small["b_rg_a"][l], small["b_rg_x"][l] = g["b_rg_a"], g["b_rg_x"]
    grad_x = dx.reshape(B, S, D)

    delta, new_m, new_v = {}, {}, {}
    flat = lambda a: a.reshape(-1, a.shape[-1])

    def adam_big(names, comm=None):
        r = _adamw("adamw_" + names[0], *[[flat(d[n]) for n in names] for d in (w, grads, m, v)], ADAM_ROWS, comm=comm)
        for q, n in enumerate(names):
            delta[n], new_m[n], new_v[n] = (r[a][q].reshape(w[n].shape) for a in range(3))
        return r[3] if comm else None

    late_sums = red.chip_sums(red.late, _comm_call("swap_halves", _swap_comm(red.late)))
    joined = _join_halves([red.accs[n] for n in red.EARLY])
    grads = {n: a.reshape(w[n].shape) for n, a in zip(red.EARLY, joined)}
    got = adam_big(["w_2", "w_1"], comm=_scatter_comm(late_sums))
    adam_big(["w_o", "w_a_out"])
    red.finish(red.LATE, late_sums, got, 0)
    joined = _join_halves([red.accs[n] for n in red.LATE])
    grads.update({n: a.reshape(w[n].shape) for n, a in zip(red.LATE, joined)})
    adam_big(["w_b_out"])
    adam_big(["w_in"])

    names = [n for n in SMALL if n != "g_final"]
    parts = [jnp.stack(small[n]) for n in names] + [dg_final]
    packed = _pack(parts, 8 * N_DEV)
    total = _allreduce_small(packed.reshape(N_DEV, packed.shape[0] // N_DEV, PACK_LANES))
    for n, a in zip(names + ["g_final"], _unpack(total, parts)):
        if n == "conv_a_w":
            a = lax.dynamic_slice_in_dim(a[:, :taps_a], k_me * conv_a_w.shape[2], conv_a_w.shape[2], axis=2)
        elif n == "conv_b_w":
            a = lax.dynamic_slice_in_dim(a[:, :taps_b], k_me * conv_b_w.shape[2], conv_b_w.shape[2], axis=2)
        grads[n] = a.reshape(w[n].shape)

    for n in SMALL:
        cols = w[n].shape[-1]
        view = lambda a: a.reshape(-1, cols)
        rows = view(w[n]).shape[0]
        d_, m_, v_ = _adamw("adamw_" + n, view(w[n]), view(grads[n]), view(m[n]), view(v[n]),
                            ADAM_SMALL_ROWS if rows % ADAM_SMALL_ROWS == 0 else rows)
        delta[n], new_m[n], new_v[n] = (a.reshape(w[n].shape) for a in (d_, m_, v_))

    return (loss, grad_x, *[grads[n] for n in WEIGHTS], *[delta[n] for n in WEIGHTS],
            *[new_m[n] for n in WEIGHTS], *[new_v[n] for n in WEIGHTS])
```

```python
import jax
import jax.numpy as jnp
from jax import lax
from jax.experimental import pallas as pl
from jax.experimental.pallas import tpu as pltpu

F32 = jnp.float32
BF16 = jnp.bfloat16
MESH = pl.DeviceIdType.MESH

EPS = 1e-6
LRU_C = 8.0
ADAM_LR, ADAM_B1, ADAM_B2, ADAM_EPS, ADAM_WD, ADAM_STEP = 0.001, 0.9, 0.999, 1e-08, 0.01, 10

N_CHIPS = 4
HEADS_PER_GROUP = 4
VMEM_LIMIT = 56 * 1024 * 1024


def _cp(**kw):
    return pltpu.CompilerParams(vmem_limit_bytes=VMEM_LIMIT, **kw)


def _sig(x):
    return 1.0 / (1.0 + jnp.exp(-x))


def _gelu(x):
    t = jnp.tanh(0.7978845608028654 * (x + 0.044715 * x * x * x))
    return 0.5 * x * (1.0 + t), t


def _gelu_grad(x, t):
    dt = (1.0 - t * t) * 0.7978845608028654 * (1.0 + 3.0 * 0.044715 * x * x)
    return 0.5 * (1.0 + t) + 0.5 * x * dt


def _rms(xf, g):
    r = lax.rsqrt(jnp.mean(xf * xf, axis=-1, keepdims=True) + EPS)
    return xf * r * g, r


def _rms_bwd(xf, g, r, dh):
    dyg = dh * g
    dx = r * (dyg - xf * (r * r) * jnp.mean(dyg * xf, axis=-1, keepdims=True))
    return dx, dh * xf * r


def _ln_silu(u, g, b):
    mu = jnp.mean(u, axis=-1, keepdims=True)
    uc = u - mu
    rstd = lax.rsqrt(jnp.mean(uc * uc, axis=-1, keepdims=True) + EPS)
    uh = uc * rstd
    u2 = uh * g + b
    s = _sig(u2)
    return u2 * s, uh, rstd, u2, s


_DIMS = {"nn": (((1,), (0,)), ((), ())), "nt": (((1,), (1,)), ((), ())), "tn": (((0,), (0,)), ((), ()))}


class _Comm:
    def __init__(self, ins, outs, sems, start, finish):
        self.ins, self.outs, self.sems, self.start, self.finish = list(ins), list(outs), list(sems), start, finish


def _resident(shape):
    return pl.BlockSpec(shape, lambda i, j, k: (0,) * len(shape), pipeline_mode=pl.Buffered(1))


def _mm(name, mode, grid, a_ins, a_fn, b_in, e_ins, epi, outs, acc_shape, cache_a=None, alias=(), extra_scratch=(),
        comm=None, b_slice=None):
    ni, nj, nk = grid
    na, ne, no = len(a_ins), len(e_ins), len(outs)
    assert cache_a is None or nk == 1
    n_fixed = (nk > 1) + (cache_a is not None)
    n_in = na + 1 + ne + len(alias)
    c_ins, c_outs, c_sems = (comm.ins, comm.outs, comm.sems) if comm else ([], [], [])

    def body(*refs):
        a_refs = refs[:na]
        b_ref = refs[na]
        e_refs = refs[na + 1:na + 1 + ne]
        comm_in = refs[n_in:n_in + len(c_ins)]
        out0 = n_in + len(c_ins)
        out_refs = refs[out0:out0 + no]
        comm_out = refs[out0 + no:out0 + no + len(c_outs)]
        scratch = refs[out0 + no + len(c_outs):]
        extra = scratch[n_fixed:n_fixed + len(extra_scratch)]
        comm_sems = scratch[n_fixed + len(extra_scratch):]
        i, j, k = pl.program_id(0), pl.program_id(1), pl.program_id(2)
        if comm:
            @pl.when((i == 0) & (j == 0) & (k == 0))
            def _():
                comm.start(comm_in, comm_out, comm_sems)
        if cache_a is not None:
            cache_ref = scratch[n_fixed - 1]

            @pl.when(j == 0)
            def _():
                cache_ref[...] = a_fn(a_refs, out_refs, i, j, k)

            a = cache_ref[...]
        else:
            a = a_fn(a_refs, out_refs, i, j, k)
        if b_slice is None:
            b = b_ref[...]
        elif b_slice[0] == "cols":
            b = b_ref[:, pl.ds(pl.multiple_of(j * b_slice[1], b_slice[1]), b_slice[1])]
        else:
            b = b_ref[pl.ds(pl.multiple_of(j * b_slice[1], b_slice[1]), b_slice[1]), :]
        prod = lax.dot_general(a, b, _DIMS[mode], preferred_element_type=F32)
        if nk == 1:
            epi(prod, e_refs, out_refs, i, j, extra)
        else:
            acc_ref = scratch[0]

            @pl.when(k == 0)
            def _():
                acc_ref[...] = prod

            @pl.when(k > 0)
            def _():
                acc_ref[...] += prod

            @pl.when(k == nk - 1)
            def _():
                epi(acc_ref[...], e_refs, out_refs, i, j, extra)

        if comm:
            @pl.when((i == ni - 1) & (j == nj - 1) & (k == nk - 1))
            def _():
                comm.finish(comm_in, comm_out, comm_sems)

    scratch_shapes = []
    if nk > 1:
        scratch_shapes.append(pltpu.VMEM(acc_shape, F32))
    if cache_a is not None:
        scratch_shapes.append(pltpu.VMEM(cache_a, BF16))
    any_spec = pl.BlockSpec(memory_space=pl.ANY)
    ins = (list(a_ins) + [b_in] + list(e_ins) + [(arr, any_spec) for arr, _ in alias] + [(arr, any_spec) for arr in c_ins])
    first_alias = na + 1 + ne
    res = pl.pallas_call(
        body, name=name, grid=grid,
        in_specs=[s for _, s in ins], out_specs=[s for _, s in outs] + [any_spec] * len(c_outs),
        out_shape=[o for o, _ in outs] + list(c_outs),
        scratch_shapes=scratch_shapes + list(extra_scratch) + list(c_sems),
        input_output_aliases={first_alias + n: o for n, (_, o) in enumerate(alias)},
        compiler_params=_cp(dimension_semantics=("arbitrary", "arbitrary", "arbitrary"), has_side_effects=bool(comm)),
    )(*[a for a, _ in ins])
    if comm:
        return list(res[:no]), list(res[no:])
    return res


def _bs(shape, fn):
    return pl.BlockSpec(shape, fn)


def _sds(shape, dt):
    return jax.ShapeDtypeStruct(shape, dt)


def _acc_rows(ref, val, first):
    @pl.when(first)
    def _():
        ref[...] = val

    @pl.when(jnp.logical_not(first))
    def _():
        ref[...] += val


def _fwd_norm_mm(name, x, g, w, bias, tm, tn, comm=None):
    T, D = x.shape
    N = w.shape[1]

    def a_fn(a_refs, out_refs, i, j, k):
        h, _ = _rms(a_refs[0][...], a_refs[1][...])
        hb = h.astype(BF16)
        out_refs[1][...] = hb
        return hb

    def epi(acc, e_refs, out_refs, i, j, extra):
        if bias is not None:
            acc = acc + e_refs[0][...]
        out_refs[0][...] = acc.astype(BF16)

    e_ins = [] if bias is None else [(bias, _bs((1, tn), lambda i, j, k: (0, j)))]
    return _mm(name, "nn", (T // tm, N // tn, 1),
               [(x, _bs((tm, D), lambda i, j, k: (i, 0))), (g, _bs((1, D), lambda i, j, k: (0, 0)))], a_fn,
               (w, _resident((D, N))), e_ins, epi,
               [(_sds((T, N), BF16), _bs((tm, tn), lambda i, j, k: (i, j))),
                (_sds((T, D), BF16), _bs((tm, D), lambda i, j, k: (i, 0)))],
               None, cache_a=(tm, D), comm=comm, b_slice=("cols", tn))


def _fwd_ya(u1, ln_g, ln_b, w, tm):
    T, C = u1.shape
    N = w.shape[1]

    def a_fn(a_refs, out_refs, i, j, k):
        u3 = _ln_silu(a_refs[0][...].astype(F32), a_refs[1][...], a_refs[2][...])[0].astype(BF16)
        out_refs[1][...] = u3
        return u3

    def epi(acc, e_refs, out_refs, i, j, extra):
        out_refs[0][...] = acc.astype(BF16)

    row = _bs((1, C), lambda i, j, k: (0, 0))
    tc = _bs((tm, C), lambda i, j, k: (i, 0))
    return _mm("fwd_ya", "nn", (T // tm, 1, 1), [(u1, tc), (ln_g, row), (ln_b, row)], a_fn,
               (w, _bs((C, N), lambda i, j, k: (0, 0))), [], epi,
               [(_sds((T, N), BF16), _bs((tm, N), lambda i, j, k: (i, 0))), (_sds((T, C), BF16), tc)], None)


def _fwd_yb(h, z, gb_blk, w, tm, tk):
    T, C = h.shape
    N = w.shape[1]

    def a_fn(a_refs, out_refs, i, j, k):
        ge, _ = _gelu(a_refs[1][...].astype(F32))
        pv = (a_refs[0][...].astype(F32) * ge).astype(BF16)
        out_refs[1][...] = pv
        return pv

    def epi(acc, e_refs, out_refs, i, j, extra):
        out_refs[0][...] = acc.astype(BF16)

    tk_ = _bs((tm, tk), lambda i, j, k: (i, k))
    return _mm("fwd_yb", "nn", (T // tm, 1, C // tk),
               [(h, tk_), (z, _bs((tm, tk), lambda i, j, k: (i, gb_blk + k)))], a_fn,
               (w, _bs((tk, N), lambda i, j, k: (k, 0))), [], epi,
               [(_sds((T, N), BF16), _bs((tm, N), lambda i, j, k: (i, 0))), (_sds((T, C), BF16), tk_)], (tm, N))


def _fwd_x1(x, ya, yb, z, sa_blk, w, tm):
    T, D = x.shape

    def a_fn(a_refs, out_refs, i, j, k):
        ya_, yb_, sa_, sb_ = (r[...].astype(F32) for r in a_refs)
        mg = (_sig(sa_) * ya_ + _sig(sb_) * yb_).astype(BF16)
        out_refs[1][...] = mg
        return mg

    def epi(acc, e_refs, out_refs, i, j, extra):
        out_refs[0][...] = e_refs[0][...] + acc

    t = _bs((tm, D), lambda i, j, k: (i, 0))
    return _mm("fwd_x1", "nn", (T // tm, 1, 1),
               [(ya, t), (yb, t), (z, _bs((tm, D), lambda i, j, k: (i, sa_blk))),
                (z, _bs((tm, D), lambda i, j, k: (i, sa_blk + 1)))], a_fn,
               (w, _bs((D, D), lambda i, j, k: (0, 0))), [(x, t)], epi,
               [(_sds((T, D), F32), t), (_sds((T, D), BF16), t)], None)


def _fwd_x2(x1, fp, w, tm, tk, comm=None):
    T, D = x1.shape
    Fd = fp.shape[1]

    def epi(acc, e_refs, out_refs, i, j, extra):
        out_refs[0][...] = e_refs[0][...] + acc

    t = _bs((tm, D), lambda i, j, k: (i, 0))
    whole_k = tk == Fd
    r = _mm("fwd_x2", "nn", (T // tm, 1, Fd // tk),
            [(fp, _bs((tm, tk), lambda i, j, k: (i, k)))], _relu2,
            (w, _resident((Fd, D)) if whole_k else _bs((tk, D), lambda i, j, k: (k, 0))), [(x1, t)], epi,
            [(_sds((T, D), F32), t)], (tm, D), comm=comm)
    return (r[0][0], r[1]) if comm else r[0]


def _relu2(a_refs, out_refs, i, j, k):
    f = jnp.maximum(a_refs[0][...], 0.0)
    return f * f


def _loss_head(x, g, target, tm):
    T, D = x.shape

    def body(x_ref, g_ref, t_ref, loss_ref, dx_ref, dxb_ref, dg_ref):
        i = pl.program_id(0)
        xf, gv = x_ref[...], g_ref[...]
        y, r = _rms(xf, gv)
        err = y - t_ref[...]
        part = 0.5 * jnp.sum(jnp.mean(err * err, axis=-1, keepdims=True), axis=0, keepdims=True)
        dx, dg_rows = _rms_bwd(xf, gv, r, err * (1.0 / D))
        dx_ref[...] = dx
        dxb_ref[...] = dx.astype(BF16)
        _acc_rows(loss_ref, jnp.broadcast_to(part, (1, 128)), i == 0)
        _acc_rows(dg_ref, jnp.sum(dg_rows, axis=0, keepdims=True), i == 0)

    t = _bs((tm, D), lambda i: (i, 0))
    row = _bs((1, D), lambda i: (0, 0))
    return pl.pallas_call(
        body, name="loss_head", grid=(T // tm,), in_specs=[t, row, t],
        out_specs=[_bs((1, 128), lambda i: (0, 0)), t, t, row],
        out_shape=[_sds((1, 128), F32), _sds((T, D), F32), _sds((T, D), BF16), _sds((1, D), F32)],
        compiler_params=_cp(dimension_semantics=("arbitrary",)),
    )(x, g, target)


def _adamw(name, w, g, m, v, tr, comm=None):
    many = isinstance(w, (list, tuple))
    ws, gs, ms, vs = (list(a) if many else [a] for a in (w, g, m, v))
    n = len(ws)
    rows, cols = ws[0].shape
    d1 = 1.0 - ADAM_B1 ** ADAM_STEP
    d2 = 1.0 - ADAM_B2 ** ADAM_STEP

    def body(*refs):
        for q in range(n):
            w_ref, g_ref, m_ref, v_ref = (refs[a * n + q] for a in range(4))
            d_ref, mo_ref, vo_ref = (refs[(4 + a) * n + q] for a in range(3))
            gv = g_ref[...]
            mn = ADAM_B1 * m_ref[...] + (1.0 - ADAM_B1) * gv
            vn = ADAM_B2 * v_ref[...] + (1.0 - ADAM_B2) * (gv * gv)
            d_ref[...] = -ADAM_LR * ((mn / d1) / (jnp.sqrt(vn / d2) + ADAM_EPS) + ADAM_WD * w_ref[...])
            mo_ref[...] = mn
            vo_ref[...] = vn

    t = _bs((tr, cols), lambda i: (i, 0))
    r = _call_with_comm(name, body, (rows // tr,), ws + gs + ms + vs, [t] * (4 * n), [t] * (3 * n),
                        [_sds((rows, cols), F32)] * (3 * n), [], comm)
    outs, got = (r if comm else (r, None))
    res = [outs[a * n:(a + 1) * n] if many else outs[a * n] for a in range(3)]
    return (*res, got) if comm else tuple(res)


def _ident(a_refs, out_refs, i, j, k):
    return a_refs[0][...]


def _bwd_dw(name, act, dy, ti, tj, tm, a_fn=None, a_extra=(), shard_cols=None, keep=None):
    T, J = dy.shape
    I = act.shape[1]

    def epi(acc, e_refs, out_refs, i, j, extra):
        out_refs[0][...] = acc.astype(BF16).reshape(out_refs[0].shape)

    if shard_cols is None:
        out = (_sds((I, J), BF16), _bs((ti, tj), lambda i, j, k: (i, j)))
    else:
        per = shard_cols // tj
        assert ti == I and per * tj == shard_cols
        out = (_sds((J // shard_cols, 2, I // 2, shard_cols), BF16),
               _bs((None, 2, I // 2, tj), lambda i, j, k: (lax.div(j, per), 0, 0, lax.rem(j, per))))
    assert keep is None or tm == T
    a_spec = _resident((T, I)) if keep == "act" else _bs((tm, ti), lambda i, j, k: (k, i))
    b_spec = _resident((T, J)) if keep == "dy" else _bs((tm, tj), lambda i, j, k: (k, j))
    return _mm(name, "tn", (I // ti, J // tj, T // tm), [(act, a_spec)] + list(a_extra), a_fn or _ident,
               (dy, b_spec), [], epi, [out], (ti, tj))[0]


def _bwd_df(dxb, w2, fp, tm, tn, comm=None):
    T, D = dxb.shape
    Fd = w2.shape[0]

    def epi(acc, e_refs, out_refs, i, j, extra):
        out_refs[0][...] = (acc * (2.0 * jnp.maximum(e_refs[0][...].astype(F32), 0.0))).astype(BF16)

    t = _bs((tm, tn), lambda i, j, k: (i, j))
    r = _mm("bwd_df", "nt", (T // tm, Fd // tn, 1), [(dxb, _bs((tm, D), lambda i, j, k: (i, 0)))], _ident,
            (w2, _resident((Fd, D))), [(fp, t)], epi, [(_sds((T, Fd), BF16), t)], None, b_slice=("rows", tn), comm=comm)
    return (r[0][0], r[1]) if comm else r[0]


def _bwd_norm(name, dy, w, x, g, dres, tm, tk, colsum=False, comm=None):
    T, K = dy.shape
    D = w.shape[0]
    nk = K // tk

    def a_fn(a_refs, out_refs, i, j, k):
        a = a_refs[0][...]
        if colsum:
            s = jnp.sum(a.astype(F32), axis=0, keepdims=True)

            @pl.when(i == 0)
            def _():
                out_refs[3][k] = s

            @pl.when(i > 0)
            def _():
                out_refs[3][k] += s
        return a

    def epi(acc, e_refs, out_refs, i, j, extra):
        xf, gv = e_refs[0][...], e_refs[1][...]
        r = lax.rsqrt(jnp.mean(xf * xf, axis=-1, keepdims=True) + EPS)
        dx, dg_rows = _rms_bwd(xf, gv, r, acc)
        dx = dx + e_refs[2][...]
        out_refs[0][...] = dx
        out_refs[1][...] = dx.astype(BF16)
        _acc_rows(out_refs[2], jnp.sum(dg_rows, axis=0, keepdims=True), i == 0)

    t = _bs((tm, D), lambda i, j, k: (i, 0))
    row = _bs((1, D), lambda i, j, k: (0, 0))
    outs = [(_sds((T, D), F32), t), (_sds((T, D), BF16), t), (_sds((1, D), F32), row)]
    if colsum:
        outs.append((_sds((nk, 1, tk), F32), _bs((nk, 1, tk), lambda i, j, k: (0, 0, 0))))
    return _mm(name, "nt", (T // tm, 1, nk), [(dy, _bs((tm, tk), lambda i, j, k: (i, k)))], a_fn,
               (w, _resident((D, K)) if nk == 1 else _bs((D, tk), lambda i, j, k: (0, k))),
               [(x, t), (g, row), (dres, t)], epi, outs, (tm, D), comm=comm)


def _bwd_dm(dxb, w_o, ya, yb, z, sa_blk, tm):
    T, D = dxb.shape

    def epi(acc, e_refs, out_refs, i, j, extra):
        ya_, yb_, sa_, sb_ = (r[...].astype(F32) for r in e_refs)
        ga, gb = _sig(sa_), _sig(sb_)
        out_refs[0][...] = (acc * ga).astype(BF16)
        out_refs[1][...] = (acc * gb).astype(BF16)
        stage, sem = extra
        put = pltpu.make_async_copy(
            stage, out_refs[2].at[pl.ds(pl.multiple_of(i * tm, tm), tm), pl.ds(sa_blk * D, 2 * D)], sem)

        @pl.when(i > 0)
        def _():
            put.wait()

        stage[:, 0:D] = (acc * ya_ * ga * (1.0 - ga)).astype(BF16)
        stage[:, D:2 * D] = (acc * yb_ * gb * (1.0 - gb)).astype(BF16)
        put.start()

        @pl.when(i == T // tm - 1)
        def _():
            put.wait()

    t = _bs((tm, D), lambda i, j, k: (i, 0))
    return _mm("bwd_dm", "nt", (T // tm, 1, 1), [(dxb, t)], _ident, (w_o, _bs((D, D), lambda i, j, k: (0, 0))),
               [(ya, t), (yb, t), (z, _bs((tm, D), lambda i, j, k: (i, sa_blk))),
                (z, _bs((tm, D), lambda i, j, k: (i, sa_blk + 1)))], epi,
               [(_sds((T, D), BF16), t), (_sds((T, D), BF16), t),
                (_sds(z.shape, BF16), pl.BlockSpec(memory_space=pl.ANY))], None,
               extra_scratch=[pltpu.VMEM((tm, 2 * D), BF16), pltpu.SemaphoreType.DMA(())])


def _bwd_du3(dya, w, u1, ln_g, ln_b, tm):
    T, D = dya.shape
    C = w.shape[0]

    def epi(acc, e_refs, out_refs, i, j, extra):
        gv = e_refs[1][...]
        _, uh, rstd, u2, s = _ln_silu(e_refs[0][...].astype(F32), gv, e_refs[2][...])
        du2 = acc * (s * (1.0 + u2 * (1.0 - s)))
        duh = du2 * gv
        out_refs[0][...] = rstd * (duh - jnp.mean(duh, axis=-1, keepdims=True)
                                   - uh * jnp.mean(duh * uh, axis=-1, keepdims=True))
        _acc_rows(out_refs[1], jnp.sum(du2 * uh, axis=0, keepdims=True), i == 0)
        _acc_rows(out_refs[2], jnp.sum(du2, axis=0, keepdims=True), i == 0)

    t = _bs((tm, C), lambda i, j, k: (i, 0))
    row = _bs((1, C), lambda i, j, k: (0, 0))
    return _mm("bwd_du3", "nt", (T // tm, 1, 1), [(dya, _bs((tm, D), lambda i, j, k: (i, 0)))], _ident,
               (w, _bs((C, D), lambda i, j, k: (0, 0))), [(u1, t), (ln_g, row), (ln_b, row)], epi,
               [(_sds((T, C), F32), t), (_sds((1, C), F32), row), (_sds((1, C), F32), row)], None)


def _bwd_dp(dyb, w, h, z, dz, gb_blk, tm, tn, comm=None):
    T, D = dyb.shape
    R = w.shape[0]

    def epi(acc, e_refs, out_refs, i, j, extra):
        gbv = e_refs[1][...].astype(F32)
        ge, th = _gelu(gbv)
        out_refs[0][...] = acc * ge
        out_refs[1][...] = (acc * e_refs[0][...].astype(F32) * _gelu_grad(gbv, th)).astype(BF16)

    t = _bs((tm, tn), lambda i, j, k: (i, j))
    tz = _bs((tm, tn), lambda i, j, k: (i, gb_blk + j))
    return _mm("bwd_dp", "nt", (T // tm, R // tn, 1), [(dyb, _bs((tm, D), lambda i, j, k: (i, 0)))], _ident,
               (w, _bs((tn, D), lambda i, j, k: (j, 0))), [(h, t), (z, tz)], epi,
               [(_sds((T, R), F32), t), (_sds(dz.shape, BF16), tz)], None, cache_a=None, alias=[(dz, 1)], comm=comm)


CONV_ROWS = 32


def _shifted_taps(x, halo, shifts, fn):
    n = CONV_ROWS + halo
    by_r = {}
    for k, s in shifts:
        by_r.setdefault(s % 8, []).append((k, s))
    for r in sorted(by_r):
        xr = x if r == 0 else pltpu.roll(x, n - r, 0)
        for k, s in by_r[r]:
            q = s - r
            fn(k, xr[q:q + CONV_ROWS])


def _conv_fwd(name, z, blk0, gate_blk0, w_pad, bias, taps, seq, tc, out_dtype, comm=None):
    T = z.shape[0]
    C = w_pad.shape[1]
    nb, nj = T // seq, C // tc
    pad = 8 * ((taps - 1 + 7) // 8)
    halo = pad
    shifts = [(k, pad - (taps - 1) + k) for k in range(taps)]
    glu = gate_blk0 is not None

    def body(*refs):
        if glu:
            v_ref, g_ref, w_ref, b_ref, o_ref, p_ref = refs
        else:
            v_ref, w_ref, b_ref, o_ref, p_ref = refs
        p_ref[pl.ds(0, pad), :] = jnp.zeros((pad, tc), F32)
        u = v_ref[...].astype(F32)
        if glu:
            u = u * _sig(g_ref[...].astype(F32))
        p_ref[pl.ds(pad, seq), :] = u

        def step(c, _):
            base = pl.multiple_of(c * CONV_ROWS, CONV_ROWS)
            x = p_ref[pl.ds(base, CONV_ROWS + halo), :]
            acc = [jnp.zeros((CONV_ROWS, tc), F32) + b_ref[...]]

            def tap(k, xs):
                acc[0] = acc[0] + w_ref[k:k + 1, :] * xs

            _shifted_taps(x, halo, shifts, tap)
            o_ref[pl.ds(base, CONV_ROWS), :] = acc[0].astype(out_dtype)
            return 0

        lax.fori_loop(0, seq // CONV_ROWS, step, 0)

    zin = [(z, _bs((seq, tc), lambda b, j: (b, blk0 + j)))]
    if glu:
        zin.append((z, _bs((seq, tc), lambda b, j: (b, gate_blk0 + j))))
    ins = zin + [(w_pad, _bs((w_pad.shape[0], tc), lambda b, j: (0, j))), (bias, _bs((1, tc), lambda b, j: (0, j)))]
    r = _call_with_comm(name, body, (nb, nj), [a for a, _ in ins], [s for _, s in ins],
                        [_bs((seq, tc), lambda b, j: (b, j))], [_sds((T, C), out_dtype)],
                        [pltpu.VMEM((seq + pad, tc), F32)], comm)
    return (r[0][0], r[1]) if comm else r[0]


def _conv_bwd(name, dy, z, dz, blk0, gate_blk0, w_pad, taps, seq, tc, comm=None):
    T = z.shape[0]
    C = w_pad.shape[1]
    nb, nj = T // seq, C // tc
    kp = w_pad.shape[0]
    pad = 8 * ((taps - 1 + 7) // 8)
    halo = pad
    sh_du = [(k, taps - 1 - k) for k in range(taps)]
    sh_dw = [(k, pad - (taps - 1) + k) for k in range(taps)]
    glu = gate_blk0 is not None

    def body(*refs):
        if glu:
            dy_ref, v_ref, g_ref, w_ref, _dz_in, dz_out, dw_ref, db_ref, pdy, pu, du_s, wacc, ob, ob2, osem = refs
        else:
            dy_ref, v_ref, w_ref, _dz_in, dz_out, dw_ref, db_ref, pdy, pu, du_s, wacc, ob, osem = refs
        j = pl.program_id(0)
        b = pl.program_id(1)
        pdy[pl.ds(seq, pad), :] = jnp.zeros((pad, tc), F32)
        pdy[pl.ds(0, seq), :] = dy_ref[...].astype(F32)
        pu[pl.ds(0, pad), :] = jnp.zeros((pad, tc), F32)
        v = v_ref[...].astype(F32)
        if glu:
            sg = _sig(g_ref[...].astype(F32))
            pu[pl.ds(pad, seq), :] = v * sg
        else:
            pu[pl.ds(pad, seq), :] = v
        wacc[...] = jnp.zeros(wacc.shape, F32)

        def step(c, dbacc):
            base = pl.multiple_of(c * CONV_ROWS, CONV_ROWS)
            xdy = pdy[pl.ds(base, CONV_ROWS + halo), :]
            acc = [jnp.zeros((CONV_ROWS, tc), F32)]

            def tap(k, xs):
                acc[0] = acc[0] + w_ref[k:k + 1, :] * xs

            _shifted_taps(xdy, halo, sh_du, tap)
            du_s[pl.ds(base, CONV_ROWS), :] = acc[0]
            dyc = xdy[0:CONV_ROWS]
            xu = pu[pl.ds(base, CONV_ROWS + halo), :]

            def wtap(k, xs):
                p = dyc * xs
                s8 = p[0:8]
                for m in range(1, CONV_ROWS // 8):
                    s8 = s8 + p[8 * m:8 * m + 8]
                wacc[pl.ds(8 * k, 8), :] += s8

            _shifted_taps(xu, halo, sh_dw, wtap)
            d8 = dyc[0:8]
            for m in range(1, CONV_ROWS // 8):
                d8 = d8 + dyc[8 * m:8 * m + 8]
            return dbacc + d8

        dbacc = lax.fori_loop(0, seq // CONV_ROWS, step, jnp.zeros((8, tc), F32))
        du = du_s[...]
        rows = pl.ds(pl.multiple_of(b * seq, seq), seq)
        puts = [pltpu.make_async_copy(ob, dz_out.at[rows, pl.ds(pl.multiple_of((blk0 + j) * tc, tc), tc)], osem.at[0])]
        if glu:
            puts.append(pltpu.make_async_copy(
                ob2, dz_out.at[rows, pl.ds(pl.multiple_of((gate_blk0 + j) * tc, tc), tc)], osem.at[1]))

        @pl.when((j > 0) | (b > 0))
        def _():
            for cp in puts:
                cp.wait()

        if glu:
            ob[...] = (du * sg).astype(BF16)
            ob2[...] = (du * v * sg * (1.0 - sg)).astype(BF16)
        else:
            ob[...] = du.astype(BF16)
        for cp in puts:
            cp.start()

        @pl.when((j == nj - 1) & (b == nb - 1))
        def _():
            for cp in puts:
                cp.wait()
        dw = jnp.sum(wacc[...].reshape(kp, 8, tc), axis=1)
        _acc_rows(dw_ref, dw, b == 0)
        _acc_rows(db_ref, jnp.sum(dbacc, axis=0, keepdims=True), b == 0)

    zin = [(z, _bs((seq, tc), lambda j, b: (b, blk0 + j)))]
    if glu:
        zin.append((z, _bs((seq, tc), lambda j, b: (b, gate_blk0 + j))))
    ins = [(dy, _bs((seq, tc), lambda j, b: (b, j)))] + zin + [(w_pad, _bs((kp, tc), lambda j, b: (0, j))),
                                                               (dz, pl.BlockSpec(memory_space=pl.ANY))]
    dz_idx = len(ins) - 1
    out_specs = [pl.BlockSpec(memory_space=pl.ANY), _bs((kp, tc), lambda j, b: (0, j)), _bs((1, tc), lambda j, b: (0, j))]
    out_shape = [_sds(dz.shape, dz.dtype), _sds((kp, C), F32), _sds((1, C), F32)]
    stage = [pltpu.VMEM((seq, tc), BF16)] * (2 if glu else 1) + [pltpu.SemaphoreType.DMA((2,))]
    return _call_with_comm(
        name, body, (nj, nb), [a for a, _ in ins], [s for _, s in ins], out_specs, out_shape,
        [pltpu.VMEM((seq + pad, tc), F32), pltpu.VMEM((seq + pad, tc), F32),
         pltpu.VMEM((seq, tc), F32), pltpu.VMEM((8 * kp, tc), F32)] + stage, comm, aliases={dz_idx: 0})


RG_ROWS = 256


def _softplus_neg(lam):
    return jnp.maximum(-lam, 0.0) + jnp.log(1.0 + jnp.exp(-jnp.abs(lam)))


def _gates(v0c, wa_ref, wx_ref, ba, bx, sp):
    vb = v0c.astype(BF16)
    r = _sig(jnp.dot(vb, wa_ref[...], preferred_element_type=F32) + ba)
    i = _sig(jnp.dot(vb, wx_ref[...], preferred_element_type=F32) + bx)
    return r, i, -LRU_C * r * sp


def _decay(la, first_row):
    a = jnp.exp(la)
    a2 = a * a
    x = 2.0 * la
    series = -x * (1.0 + x * (1.0 / 2) * (1.0 + x * (1.0 / 3) * (1.0 + x * (1.0 / 4) * (1.0 + x * (1.0 / 5)))))
    mult = jnp.sqrt(jnp.where(x > -0.1, series, 1.0 - a2))
    dmult = jnp.where(first_row, 0.0, -a2 / mult)
    mult = jnp.where(first_row, 1.0, mult)
    return a, mult, dmult


def _group_scan(a, b, reverse):
    n = a.shape[0]
    row = lax.broadcasted_iota(jnp.int32, a.shape, 0) & 7
    for d in (1, 2, 4):
        sh = n - d if reverse else d
        a_s, b_s = pltpu.roll(a, sh, 0), pltpu.roll(b, sh, 0)
        m = (row < 8 - d) if reverse else (row >= d)
        b = jnp.where(m, a * b_s + b, b)
        a = jnp.where(m, a * a_s, a)
    return a, b


def _group_carry(a_s, b_s, o_s, n_groups, reverse):
    cols = a_s.shape[1]

    def step(g, carry):
        g = n_groups - 1 - g if reverse else g
        rows = pl.ds(pl.multiple_of(g * 8, 8), 8)
        o = a_s[rows, :] * carry + b_s[rows, :]
        o_s[rows, :] = o
        return o[0:1, :] if reverse else o[7:8, :]

    lax.fori_loop(0, n_groups, step, jnp.zeros((1, cols), F32))


def _rglru_fwd(v0, wa, wx, ba, bx, lam, seq, comm=None):
    T, C = v0.shape
    ng, G = wa.shape[0], wa.shape[1]
    nb = T // seq

    def body(v_ref, wa_ref, wx_ref, ba_ref, bx_ref, lam_ref, h_ref, r_ref, i_ref, la_ref, a_s, b_s, h_s):
        sp = _softplus_neg(lam_ref[...])

        def chunk(c, _):
            rows = pl.ds(pl.multiple_of(c * RG_ROWS, RG_ROWS), RG_ROWS)
            t = lax.broadcasted_iota(jnp.int32, (RG_ROWS, G), 0) + c * RG_ROWS
            v0c = v_ref[rows, :]
            r, i, la = _gates(v0c, wa_ref, wx_ref, ba_ref[...], bx_ref[...], sp)
            r_ref[rows, :] = r.astype(BF16)
            i_ref[rows, :] = i.astype(BF16)
            la_ref[rows, :] = la
            a, mult, _ = _decay(la, t == 0)
            a_g, b_g = _group_scan(a, mult * i * v0c, False)
            a_s[rows, :] = a_g
            b_s[rows, :] = b_g
            return 0

        lax.fori_loop(0, seq // RG_ROWS, chunk, 0)
        _group_carry(a_s, b_s, h_s, seq // 8, False)
        h_ref[...] = h_s[...].astype(BF16)

    t2 = _bs((seq, G), lambda b, g: (b, g))
    wsp = _bs((None, G, G), lambda b, g: (g, 0, 0))
    row = _bs((1, G), lambda b, g: (0, g))
    return _call_with_comm("rglru_fwd", body, (nb, ng), [v0, wa, wx, ba, bx, lam], [t2, wsp, wsp, row, row, row],
                           [t2] * 4, [_sds((T, C), BF16)] * 3 + [_sds((T, C), F32)], [pltpu.VMEM((seq, G), F32)] * 3, comm)


def _call_with_comm(name, body, grid, ins, in_specs, out_specs, out_shape, scratch, comm, aliases=None):
    n_in, n_out, n_s = len(ins), len(out_shape), len(scratch)
    c_ins, c_outs, c_sems = (comm.ins, comm.outs, comm.sems) if comm else ([], [], [])

    def wrapped(*refs):
        o0 = n_in + len(c_ins)
        s0 = o0 + n_out + len(c_outs)
        cin, cout, csem = refs[n_in:o0], refs[o0 + n_out:s0], refs[s0 + n_s:]
        ids = [pl.program_id(a) for a in range(len(grid))]
        if comm:
            first = _all_of([i == 0 for i in ids])

            @pl.when(first)
            def _():
                comm.start(cin, cout, csem)

        body(*refs[:n_in], *refs[o0:o0 + n_out], *refs[s0:s0 + n_s])
        if comm:
            last = _all_of([i == n - 1 for i, n in zip(ids, grid)])

            @pl.when(last)
            def _():
                comm.finish(cin, cout, csem)

    res = pl.pallas_call(
        wrapped, name=name, grid=grid, in_specs=list(in_specs) + [ANY] * len(c_ins),
        out_specs=list(out_specs) + [ANY] * len(c_outs), out_shape=list(out_shape) + list(c_outs),
        scratch_shapes=list(scratch) + list(c_sems), input_output_aliases=aliases or {},
        compiler_params=_cp(dimension_semantics=("arbitrary",) * len(grid), has_side_effects=bool(comm)),
    )(*ins, *c_ins)
    return (list(res[:n_out]), list(res[n_out:])) if comm else list(res)


def _all_of(conds):
    out = conds[0]
    for c in conds[1:]:
        out = out & c
    return out


def _rglru_bwd(v0, h, dh, r_g, i_g, la_g, wa, wx, lam, seq, comm=None):
    T, C = v0.shape
    ng, G = wa.shape[0], wa.shape[1]
    nb = T // seq
    R = RG_ROWS

    def body(v_ref, h_ref, dh_ref, r_ref, i_ref, la_ref, wa_ref, wx_ref, lam_ref,
             dv_ref, dwa_ref, dwx_ref, dba_ref, dbx_ref, dlam_ref, a_s, b_s, q_s, hp_s):
        b = pl.program_id(1)
        lam_v = lam_ref[...]
        sp = _softplus_neg(lam_v)
        dsp_dlam = -_sig(-lam_v)

        @pl.when(b == 0)
        def _():
            dwa_ref[...] = jnp.zeros((G, G), F32)
            dwx_ref[...] = jnp.zeros((G, G), F32)
            dba_ref[...] = jnp.zeros((1, G), F32)
            dbx_ref[...] = jnp.zeros((1, G), F32)
            dlam_ref[...] = jnp.zeros((1, G), F32)

        hp_s[pl.ds(0, 8), :] = jnp.zeros((8, G), F32)
        hp_s[pl.ds(8, seq), :] = h_ref[...].astype(F32)
        q_s[pl.ds(seq, 8), :] = jnp.zeros((8, G), F32)

        def chunk1(c, _):
            rows = pl.ds(pl.multiple_of(c * R, R), R)
            a = jnp.exp(la_ref[rows, :])
            a_g, b_g = _group_scan(a, a * dh_ref[rows, :].astype(F32), True)
            a_s[rows, :] = a_g
            b_s[rows, :] = b_g
            return 0

        lax.fori_loop(0, seq // R, chunk1, 0)
        _group_carry(a_s, b_s, q_s, seq // 8, True)

        def chunk3(c, _):
            base = pl.multiple_of(c * R, R)
            rows = pl.ds(base, R)
            t = lax.broadcasted_iota(jnp.int32, (R, G), 0) + c * R
            v0c = v_ref[rows, :]
            r, i = r_ref[rows, :].astype(F32), i_ref[rows, :].astype(F32)
            a, mult, dmult_dla = _decay(la_ref[rows, :], t == 0)
            q_next = pltpu.roll(q_s[pl.ds(base, R + 8), :], R + 7, 0)[0:R]
            h_prev = pltpu.roll(hp_s[pl.ds(base, R + 8), :], R + 1, 0)[0:R]
            gt = dh_ref[rows, :].astype(F32) + q_next
            dla = gt * h_prev * a + gt * i * v0c * dmult_dla
            dpa = dla * (-LRU_C * sp) * r * (1.0 - r)
            dpx = gt * mult * v0c * i * (1.0 - i)
            dpa_b, dpx_b, v_b = dpa.astype(BF16), dpx.astype(BF16), v0c.astype(BF16)
            dv_ref[rows, :] = (gt * mult * i
                               + lax.dot_general(dpa_b, wa_ref[...], _DIMS["nt"], preferred_element_type=F32)
                               + lax.dot_general(dpx_b, wx_ref[...], _DIMS["nt"], preferred_element_type=F32))
            dwa_ref[...] += lax.dot_general(v_b, dpa_b, _DIMS["tn"], preferred_element_type=F32)
            dwx_ref[...] += lax.dot_general(v_b, dpx_b, _DIMS["tn"], preferred_element_type=F32)
            dba_ref[...] += jnp.sum(dpa, axis=0, keepdims=True)
            dbx_ref[...] += jnp.sum(dpx, axis=0, keepdims=True)
            dlam_ref[...] += jnp.sum(dla * (-LRU_C * r), axis=0, keepdims=True) * dsp_dlam
            return 0

        lax.fori_loop(0, seq // R, chunk3, 0)

    t2 = _bs((seq, G), lambda g, b: (b, g))
    wsp = _bs((None, G, G), lambda g, b: (g, 0, 0))
    row = _bs((1, G), lambda g, b: (0, g))
    return _call_with_comm(
        "rglru_bwd", body, (ng, nb), [v0, h, dh, r_g, i_g, la_g, wa, wx, lam], [t2] * 6 + [wsp, wsp, row],
        [t2, wsp, wsp, row, row, row],
        [_sds((T, C), F32), _sds((ng, G, G), F32), _sds((ng, G, G), F32),
         _sds((1, C), F32), _sds((1, C), F32), _sds((1, C), F32)],
        [pltpu.VMEM((seq, G), F32), pltpu.VMEM((seq, G), F32),
         pltpu.VMEM((seq + 8, G), F32), pltpu.VMEM((seq + 8, G), F32)], comm)


TC_A = 256
TC_B = 512
TAPS_A, TAPS_B = 31, 4


def _tiles(T):
    return min(512, T), min(1024, T)


GATHERED = ("w_in", "w_1", "w_a_out", "w_b_out", "w_o", "w_2", "caw", "cbw")
GATHER_KIND = {"w_in": (True, True), "w_1": (True, True), "w_a_out": (False, True), "w_b_out": (False, True),
               "w_o": (False, True), "w_2": (False, True), "caw": (True, False), "cbw": (True, False)}


def _layer_fwd(x, p, seq, cur=None, nxt=None):
    T, D = x.shape
    C, R = p["ln_g"].shape[1], p["lam"].shape[1]
    tm, tl = _tiles(T)
    gb_blk, sa_blk = (2 * C + R) // TC_B, (2 * C + 2 * R) // D
    p, ahead = dict(p), {}

    def gather(src, names):
        return None if src is None else _gather_comm([src[n] for n in names], [GATHER_KIND[n] for n in names])

    def outs(r, src, names, into):
        if src is None:
            return r
        into.update(zip(names, r[1]))
        return r[0]

    mid = ["w_a_out", "w_b_out", "w_o", "caw", "cbw"]
    z, h = outs(_fwd_norm_mm("fwd_z", x, p["g_mix"], p["w_in"], p["b_in"], tl, 1024, comm=gather(cur, mid)), cur, mid, p)
    u1 = outs(_conv_fwd("conv_a_fwd", z, 0, C // TC_A, p["caw"], p["cab"], TAPS_A, seq, TC_A, BF16,
                        comm=gather(cur, ["w_1"])), cur, ["w_1"], p)
    ya, u3 = _fwd_ya(u1, p["ln_g"], p["ln_b"], p["w_a_out"], tm)
    v0 = _conv_fwd("conv_b_fwd", z, 2 * C // TC_B, None, p["cbw"], p["cbb"], TAPS_B, seq, TC_B, F32)
    hr, rg, ig, lag = outs(_rglru_fwd(v0, p["wa"], p["wx"], p["b_rg_a"], p["b_rg_x"], p["lam"], seq,
                                      comm=gather(nxt, ["w_in"])), nxt, ["w_in"], ahead)
    yb, pb = _fwd_yb(hr, z, gb_blk, p["w_b_out"], tl, TC_B)
    x1, mg = _fwd_x1(x, ya, yb, z, sa_blk, p["w_o"], tm)
    fp, h2 = outs(_fwd_norm_mm("fwd_f", x1, p["g_mlp"], p["w_1"], None, tl, 1024, comm=gather(cur, ["w_2"])),
                  cur, ["w_2"], p)
    x2 = _fwd_x2(x1, fp, p["w_2"], tm, fp.shape[1])
    saved = dict(x=x, z=z, h=h, u1=u1, u3=u3, ya=ya, v0=v0, hr=hr, rg=rg, ig=ig, lag=lag, pb=pb, yb=yb, mg=mg, x1=x1,
                 fp=fp, h2=h2)
    return x2, saved, p, ahead.get("w_in")


class _Reduce:
    EARLY = ("w_2", "w_1", "w_o", "w_a_out")
    LATE = ("w_b_out", "w_in")

    def __init__(self, accs, c_arr, kcl_of):
        self.accs, self.c_arr, self.kcl_of, self.late = accs, c_arr, kcl_of, None

    @staticmethod
    def pieces(partials):
        return [a if a.ndim == 4 else a.reshape(N_CHIPS, 2, a.shape[0] // (2 * N_CHIPS), a.shape[1]) for a in partials]

    def chip_sums(self, pgs, swapped):
        return _sum_siblings(pgs, swapped, self.c_arr)

    def finish(self, names, sums, received, layer):
        done = _sum_chips(sums, received, self.kcl_of(layer), [self.accs[n] for n in names])
        self.accs.update(zip(names, done))


def _layer_bwd(dx2, dx2b, p, s, seq, red=None, layer=0):
    T, D = dx2.shape
    C, R = p["ln_g"].shape[1], p["lam"].shape[1]
    tm, tl = _tiles(T)
    gb_blk, sa_blk = (2 * C + R) // TC_B, (2 * C + 2 * R) // D
    z = s["z"]
    g = {}


    late_sums = None
    if red is not None and red.late is not None:
        late, red.late = red.late, None
        dfp, got = _bwd_df(dx2b, p["w_2"], s["fp"], tl, 1024, comm=_swap_comm(late))
        late_sums = red.chip_sums(late, got)
    else:
        dfp = _bwd_df(dx2b, p["w_2"], s["fp"], tl, 1024)
    g["w_2"] = _bwd_dw("bwd_dw2", s["fp"], dx2b, 1024, D, T, a_fn=_relu2, keep="dy")
    dx1, dx1b, g["g_mlp"] = _bwd_norm("bwd_dh2", dfp, p["w_1"], s["x1"], p["g_mlp"], dx2, tm, dfp.shape[1])
    g["w_1"] = _bwd_dw("bwd_dw1", s["h2"], dfp, D, 1024, T, shard_cols=dfp.shape[1] // N_CHIPS, keep="act")

    dya, dyb, dz = _bwd_dm(dx1b, p["w_o"], s["ya"], s["yb"], z, sa_blk, tm)

    g["w_o"] = _bwd_dw("bwd_dwo", s["mg"], dx1b, D, D, tl)
    du1, g["ln_g"], g["ln_b"] = _bwd_du3(dya, p["w_a_out"], s["u1"], p["ln_g"], p["ln_b"], tm)
    g["w_a_out"] = _bwd_dw("bwd_dwa", s["u3"], dya, C, D, tl)
    conv_a_args = ("conv_a_bwd", du1, z, dz, 0, C // TC_A, p["caw"], TAPS_A, seq, TC_A)
    if late_sums is not None:
        (dz, g["caw"], g["cab"]), got = _conv_bwd(*conv_a_args, comm=_scatter_comm(late_sums))
        red.finish(red.LATE, late_sums, got, layer + 1)
    else:
        dz, g["caw"], g["cab"] = _conv_bwd(*conv_a_args)

    dp_args = (dyb, p["w_b_out"], s["hr"], z, dz, gb_blk, tl, TC_B)
    if red is not None:
        early = red.pieces([g.pop(n) for n in red.EARLY])
        (dhr, dz), got = _bwd_dp(*dp_args, comm=_swap_comm(early))
        early_sums = red.chip_sums(early, got)
    else:
        dhr, dz = _bwd_dp(*dp_args)

    g["w_b_out"] = _bwd_dw("bwd_dwb", s["pb"], dyb, R, D, tl)
    rg_args = (s["v0"], s["hr"], dhr, s["rg"], s["ig"], s["lag"], p["wa"], p["wx"], p["lam"], seq)
    if red is not None:
        rg_out, got = _rglru_bwd(*rg_args, comm=_scatter_comm(early_sums))
        red.finish(red.EARLY, early_sums, got, layer)
    else:
        rg_out = _rglru_bwd(*rg_args)
    dv0, g["wa"], g["wx"], g["b_rg_a"], g["b_rg_x"], g["lam"] = rg_out
    dz, g["cbw"], g["cbb"] = _conv_bwd("conv_b_bwd", dv0, z, dz, 2 * C // TC_B, None, p["cbw"], TAPS_B, seq, TC_B)

    dx, dxb, g["g_mix"], dbin = _bwd_norm("bwd_dh", dz, p["w_in"], s["x"], p["g_mix"], dx1, tm, dz.shape[1],
                                          colsum=True)
    g["b_in"] = dbin.reshape(1, -1)
    ns = dz.shape[1] // N_CHIPS
    g["w_in"] = _bwd_dw("bwd_dwin", s["h"], dz, D, ns // 2, T, shard_cols=ns, keep="act")
    if red is not None:
        red.late = red.pieces([g.pop(n) for n in red.LATE])
    return dx, dxb, g


ANY = pl.BlockSpec(memory_space=pl.ANY)


def _mesh_pos():
    return lax.axis_index("x"), lax.axis_index("y"), lax.axis_index("c")


def _other_chips(x, y):
    return [(1 - x, y), (x, 1 - y), (1 - x, 1 - y)]


def _remote(src, dst, ssem, rsem, dev):
    return pltpu.make_async_remote_copy(src_ref=src, dst_ref=dst, send_sem=ssem, recv_sem=rsem,
                                        device_id=dev, device_id_type=MESH)


def _gather_region(src, dst, by_cols, k, half):
    rows, cols = src.shape
    nr = rows if half is None else rows // 2
    r0 = 0 if half is None else half * nr
    if by_cols:
        return dst.at[pl.ds(r0, nr), pl.ds(pl.multiple_of(k * cols, 128), cols)]
    return dst.at[pl.ds(pl.multiple_of(k * rows + r0, 8), nr), :]


def _gather_sends(src, dst, kinds, send, recv):
    x, y, c = _mesh_pos()
    cps = []
    for t in range(len(src)):
        half = c if kinds[t][1] else None
        hr = src[t].shape[0] // 2
        s_ref = src[t].at[pl.ds(c * hr, hr), :] if kinds[t][1] else src[t]
        for j, chip in enumerate(_other_chips(x, y)):
            cps.append(_remote(s_ref, _gather_region(src[t], dst[t], kinds[t][0], 2 * x + y, half),
                               send.at[t, j], recv.at[t, j], (*chip, c)))
    return cps


def _gather_finish(src, dst, kinds, send, recv, fsend, frecv):
    x, y, c = _mesh_pos()
    chips = _other_chips(x, y)
    sib = (x, y, 1 - c)
    n = len(src)
    fwd = []
    for t in range(n):
        half = c if kinds[t][1] else None
        for j, chip in enumerate(chips):
            got = _gather_region(src[t], dst[t], kinds[t][0], 2 * chip[0] + chip[1], half)
            _remote(got, got, send.at[t, j], recv.at[t, j], (*chip, c)).wait_recv()
            if kinds[t][1]:
                cp = _remote(got, got, fsend.at[t, j], frecv.at[t, j], sib)
                cp.start()
                fwd.append(cp)
    for t in range(n):
        if kinds[t][1]:
            for j, chip in enumerate(chips):
                got = _gather_region(src[t], dst[t], kinds[t][0], 2 * chip[0] + chip[1], 1 - c)
                _remote(got, got, fsend.at[t, j], frecv.at[t, j], sib).wait_recv()
    for cp in _gather_sends(src, dst, kinds, send, recv) + fwd:
        cp.wait_send()


def _gather_sems(n):
    sem = pltpu.SemaphoreType.DMA
    return [sem((n, 3)), sem((n, 3)), sem((n, 3)), sem((n, 3))]


def _gather_comm(shards, kinds):
    n = len(shards)

    def whole(s, by_cols):
        return (s.shape[0], N_CHIPS * s.shape[1]) if by_cols else (N_CHIPS * s.shape[0], s.shape[1])

    def own(src, dst, lsem):
        x, y, _ = _mesh_pos()
        return [pltpu.make_async_copy(src[t], _gather_region(src[t], dst[t], kinds[t][0], 2 * x + y, None), lsem.at[t])
                for t in range(n)]

    def start(src, dst, sems):
        for cp in own(src, dst, sems[4]) + _gather_sends(src, dst, kinds, sems[0], sems[1]):
            cp.start()

    def finish(src, dst, sems):
        _gather_finish(src, dst, kinds, *sems[:4])
        for cp in own(src, dst, sems[4]):
            cp.wait()

    return _Comm(shards, [_sds(whole(s, k[0]), s.dtype) for s, k in zip(shards, kinds)],
                 _gather_sems(n) + [pltpu.SemaphoreType.DMA((n,))], start, finish)


def _scatter_comm(ps):
    n = len(ps)

    def copies(src, dst, sems):
        x, y, c = _mesh_pos()
        return [_remote(src[t].at[2 * chip[0] + chip[1]], dst[t].at[j], sems[0].at[t, j], sems[1].at[t, j], (*chip, c))
                for t in range(n) for j, chip in enumerate(_other_chips(x, y))]

    def start(src, dst, sems):
        for cp in copies(src, dst, sems):
            cp.start()

    def finish(src, dst, sems):
        cps = copies(src, dst, sems)
        for cp in cps:
            cp.wait_recv()
        for cp in cps:
            cp.wait_send()

    sem = pltpu.SemaphoreType.DMA
    return _Comm(ps, [_sds((3,) + a.shape[1:], a.dtype) for a in ps], [sem((n, 3)), sem((n, 3))], start, finish)


def _comm_call(name, comm):
    n_i, n_o = len(comm.ins), len(comm.outs)

    def body(*refs):
        comm.start(refs[:n_i], refs[n_i:n_i + n_o], refs[n_i + n_o:])
        comm.finish(refs[:n_i], refs[n_i:n_i + n_o], refs[n_i + n_o:])

    return pl.pallas_call(
        body, name=name, in_specs=[ANY] * n_i, out_specs=[ANY] * n_o, out_shape=comm.outs, scratch_shapes=comm.sems,
        compiler_params=_cp(has_side_effects=True),
    )(*comm.ins)


def _swap_comm(pgs):
    n = len(pgs)

    def copies(src, dst, sems):
        x, y, c = _mesh_pos()
        return [_remote(src[t].at[:, 1 - c], dst[t], sems[0].at[t], sems[1].at[t], (x, y, 1 - c)) for t in range(n)]

    def start(src, dst, sems):
        for cp in copies(src, dst, sems):
            cp.start()

    def finish(src, dst, sems):
        cps = copies(src, dst, sems)
        for cp in cps:
            cp.wait_recv()
        for cp in cps:
            cp.wait_send()

    sem = pltpu.SemaphoreType.DMA
    return _Comm(pgs, [_sds((a.shape[0],) + a.shape[2:], a.dtype) for a in pgs], [sem((n,)), sem((n,))], start, finish)


def _join_halves(accs):
    n = len(accs)

    def body(*refs):
        buf = refs[n:2 * n]
        send, recv = refs[2 * n:]
        x, y, c = _mesh_pos()
        cps = [_remote(buf[t].at[:, c], buf[t].at[:, c], send.at[t], recv.at[t], (x, y, 1 - c)) for t in range(n)]
        for cp in cps:
            cp.start()
        for t in range(n):
            _remote(buf[t].at[:, c], buf[t].at[:, 1 - c], send.at[t], recv.at[t], (x, y, 1 - c)).wait_recv()
        for cp in cps:
            cp.wait_send()

    sem = pltpu.SemaphoreType.DMA
    return pl.pallas_call(
        body, name="join_halves", in_specs=[ANY] * n, out_specs=[ANY] * n,
        out_shape=[_sds(a.shape, a.dtype) for a in accs], scratch_shapes=[sem((n,)), sem((n,))],
        input_output_aliases={t: t for t in range(n)}, compiler_params=_cp(has_side_effects=True),
    )(*accs)


def _sum_siblings(pgs, rbs, c_arr):
    n = len(pgs)
    nk = pgs[0].shape[0]

    def body(c_ref, *refs):
        for t in range(n):
            refs[2 * n + t][...] = (refs[t][...].astype(F32) + refs[n + t][...].astype(F32)).astype(BF16)

    half = lambda a: pl.BlockSpec((None,) + a.shape[2:], lambda k, c_ref: (k, 0, 0))
    return pl.pallas_call(
        body, name="sum_siblings",
        grid_spec=pltpu.PrefetchScalarGridSpec(
            num_scalar_prefetch=1, grid=(nk,),
            in_specs=[pl.BlockSpec((None, None) + a.shape[2:], lambda k, c_ref: (k, c_ref[0], 0, 0)) for a in pgs]
            + [half(a) for a in pgs],
            out_specs=[half(a) for a in pgs]),
        out_shape=[_sds((nk,) + a.shape[2:], BF16) for a in pgs],
        compiler_params=_cp(dimension_semantics=("arbitrary",)),
    )(c_arr, *pgs, *rbs)


def _sum_chips(ps, rbs, kcl, accs):
    n = len(ps)

    def body(k_ref, *refs):
        for t in range(n):
            b_ref = refs[n + t]
            refs[3 * n + t][...] = (refs[t][...].astype(F32) + b_ref[0].astype(F32) + b_ref[1].astype(F32)
                                    + b_ref[2].astype(F32))

    qr = lambda a: (a.shape[1] // 2, a.shape[2])
    return pl.pallas_call(
        body, name="sum_chips",
        grid_spec=pltpu.PrefetchScalarGridSpec(
            num_scalar_prefetch=1, grid=(2,),
            in_specs=[pl.BlockSpec((None,) + qr(a), lambda r, k_ref: (k_ref[0], r, 0)) for a in ps]
            + [pl.BlockSpec((3,) + qr(a), lambda r, k_ref: (0, r, 0)) for a in ps] + [ANY] * n,
            out_specs=[pl.BlockSpec((None, None) + qr(a), lambda r, k_ref: (k_ref[2], k_ref[1], r, 0)) for a in ps]),
        out_shape=[_sds(a.shape, F32) for a in accs], input_output_aliases={1 + 2 * n + t: t for t in range(n)},
        compiler_params=_cp(dimension_semantics=("arbitrary",)),
    )(kcl, *ps, *rbs, *accs)


N_DEV = 8


def _allreduce_small(parts):
    n = len(parts)

    def body(*refs):
        p_refs, o_refs, rbufs = refs[:n], refs[n:2 * n], refs[2 * n:3 * n]
        s1, r1, s2, r2 = refs[3 * n:]
        x, y, c = _mesh_pos()
        me = 4 * x + 2 * y + c
        devs = [(d // 4, (d // 2) % 2, d % 2) for d in range(N_DEV)]
        for q in range(n):
            rbufs[q][me] = p_refs[q][me]

        def each_peer(fn):
            for d in range(N_DEV):
                @pl.when(d != me)
                def _():
                    for q in range(n):
                        fn(d, q)

        def first(d, q, to_me):
            return _remote(p_refs[q].at[d], rbufs[q].at[d if to_me else me], s1.at[q, d], r1.at[q, d if to_me else me],
                           devs[d])

        def second(d, q, to_me):
            blk = d if to_me else me
            return _remote(o_refs[q].at[blk], o_refs[q].at[blk], s2.at[q, d], r2.at[q, blk], devs[d])

        each_peer(lambda d, q: first(d, q, False).start())
        each_peer(lambda d, q: first(d, q, True).wait_recv())
        for q in range(n):
            total = rbufs[q][0].astype(F32)
            for d in range(1, N_DEV):
                total = total + rbufs[q][d].astype(F32)
            o_refs[q][me] = total.astype(o_refs[q].dtype)
        each_peer(lambda d, q: second(d, q, False).start())
        each_peer(lambda d, q: second(d, q, True).wait_recv())
        each_peer(lambda d, q: first(d, q, False).wait_send())
        each_peer(lambda d, q: second(d, q, False).wait_send())

    sem = pltpu.SemaphoreType.DMA
    vm = pl.BlockSpec(memory_space=pltpu.VMEM)
    return pl.pallas_call(
        body, name="allreduce_small", in_specs=[vm] * n, out_specs=[vm] * n,
        out_shape=[_sds(a.shape, a.dtype) for a in parts],
        scratch_shapes=[pltpu.VMEM(a.shape, a.dtype) for a in parts] + [sem((n, N_DEV))] * 4,
        compiler_params=_cp(has_side_effects=True),
    )(*parts)


BIG = ("w_in", "w_1", "w_a_out", "w_b_out", "w_o", "w_2")
BY_COLS = {"w_in": True, "w_1": True, "w_a_out": False, "w_b_out": False, "w_o": False, "w_2": False}
WEIGHTS = ("g_mix", "w_in", "b_in", "conv_a_w", "conv_a_b", "ln_g", "ln_b", "w_a_out", "conv_b_w", "conv_b_b", "w_rg_a",
           "b_rg_a", "w_rg_x", "b_rg_x", "lam", "w_b_out", "w_o", "g_mlp", "w_1", "w_2", "g_final")
SMALL = tuple(n for n in WEIGHTS if n not in BIG)
ADAM_ROWS = 256
ADAM_SMALL_ROWS = 2048


def _block_diag(w):
    nh, dh, _ = w.shape
    ng = nh // HEADS_PER_GROUP
    w4 = w.reshape(ng, HEADS_PER_GROUP, dh, dh)
    eye = jnp.eye(HEADS_PER_GROUP, dtype=w.dtype)
    return jnp.einsum("qhij,hk->qhikj", w4, eye).reshape(ng, HEADS_PER_GROUP * dh, HEADS_PER_GROUP * dh)


def _block_diag_part(d, dh):
    ng = d.shape[0]
    eye = jnp.eye(HEADS_PER_GROUP, dtype=d.dtype)
    d5 = d.reshape(ng, HEADS_PER_GROUP, dh, HEADS_PER_GROUP, dh)
    return jnp.einsum("qhikj,hk->qhij", d5, eye).reshape(ng * HEADS_PER_GROUP, dh, dh)


PACK_LANES = 128


def _pack(arrays, blocks, tile_rows):
    parts = [a.reshape(-1, PACK_LANES) for a in arrays]
    parts = [jnp.pad(p, ((0, -p.shape[0] % tile_rows), (0, 0))) if p.shape[0] % tile_rows else p for p in parts]
    rows = sum(p.shape[0] for p in parts)
    pad = -rows % (blocks * tile_rows)
    if pad:
        parts.append(jnp.zeros((pad, PACK_LANES), parts[0].dtype))
    return jnp.concatenate(parts, axis=0).reshape(blocks, -1, PACK_LANES)


def _unpack(buf, like, tile_rows):
    buf = buf.reshape(-1, PACK_LANES)
    out, off = [], 0
    for a in like:
        n = a.size // PACK_LANES
        out.append(buf[off:off + n].reshape(a.shape))
        off += n + (-n % tile_rows)
    return out


def kernel(x, g_mix, w_in, b_in, conv_a_w, conv_a_b, ln_g, ln_b, w_a_out, conv_b_w, conv_b_b, w_rg_a, b_rg_a, w_rg_x, b_rg_x, lam, w_b_out, w_o, g_mlp, w_1, w_2, g_final, loss_target, m_g_mix, m_w_in, m_b_in, m_conv_a_w, m_conv_a_b, m_ln_g, m_ln_b, m_w_a_out, m_conv_b_w, m_conv_b_b, m_w_rg_a, m_b_rg_a, m_w_rg_x, m_b_rg_x, m_lam, m_w_b_out, m_w_o, m_g_mlp, m_w_1, m_w_2, m_g_final, v_g_mix, v_w_in, v_b_in, v_conv_a_w, v_conv_a_b, v_ln_g, v_ln_b, v_w_a_out, v_conv_b_w, v_conv_b_b, v_w_rg_a, v_b_rg_a, v_w_rg_x, v_b_rg_x, v_lam, v_w_b_out, v_w_o, v_g_mlp, v_w_1, v_w_2, v_g_final):
    w = dict(g_mix=g_mix, w_in=w_in, b_in=b_in, conv_a_w=conv_a_w, conv_a_b=conv_a_b, ln_g=ln_g, ln_b=ln_b, w_a_out=w_a_out,
             conv_b_w=conv_b_w, conv_b_b=conv_b_b, w_rg_a=w_rg_a, b_rg_a=b_rg_a, w_rg_x=w_rg_x, b_rg_x=b_rg_x, lam=lam,
             w_b_out=w_b_out, w_o=w_o, g_mlp=g_mlp, w_1=w_1, w_2=w_2, g_final=g_final)
    m = dict(g_mix=m_g_mix, w_in=m_w_in, b_in=m_b_in, conv_a_w=m_conv_a_w, conv_a_b=m_conv_a_b, ln_g=m_ln_g, ln_b=m_ln_b,
             w_a_out=m_w_a_out, conv_b_w=m_conv_b_w, conv_b_b=m_conv_b_b, w_rg_a=m_w_rg_a, b_rg_a=m_b_rg_a, w_rg_x=m_w_rg_x,
             b_rg_x=m_b_rg_x, lam=m_lam, w_b_out=m_w_b_out, w_o=m_w_o, g_mlp=m_g_mlp, w_1=m_w_1, w_2=m_w_2, g_final=m_g_final)
    v = dict(g_mix=v_g_mix, w_in=v_w_in, b_in=v_b_in, conv_a_w=v_conv_a_w, conv_a_b=v_conv_a_b, ln_g=v_ln_g, ln_b=v_ln_b,
             w_a_out=v_w_a_out, conv_b_w=v_conv_b_w, conv_b_b=v_conv_b_b, w_rg_a=v_w_rg_a, b_rg_a=v_b_rg_a, w_rg_x=v_w_rg_x,
             b_rg_x=v_b_rg_x, lam=v_lam, w_b_out=v_w_b_out, w_o=v_w_o, g_mlp=v_g_mlp, w_1=v_w_1, w_2=v_w_2, g_final=v_g_final)
    B, S, D = x.shape
    T = B * S
    L = w_in.shape[0]
    dh = w_rg_a.shape[-1]
    taps_a, taps_b = conv_a_w.shape[1], conv_b_w.shape[1]
    assert (taps_a, taps_b) == (TAPS_A, TAPS_B)
    xi, yi, ci = _mesh_pos()
    c_arr = jnp.reshape(ci, (1,)).astype(jnp.int32)
    k_me = 2 * xi + yi

    caw_p = jnp.pad(conv_a_w, ((0, 0), (0, 32 - taps_a), (0, 0)))
    cbw_p = jnp.pad(conv_b_w, ((0, 0), (0, 8 - taps_b), (0, 0)))
    row = lambda a: a.reshape(1, -1)

    def shards_of(l):
        d = {n: w[n][l].astype(BF16) for n in BIG}
        d.update(caw=caw_p[l], cbw=cbw_p[l])
        return d

    def params_of(l, w_in_whole):
        p = dict(w_in=w_in_whole, cab=row(conv_a_b[l]), cbb=row(conv_b_b[l]),
                 wa=_block_diag(w_rg_a[l]).astype(BF16), wx=_block_diag(w_rg_x[l]).astype(BF16))
        for n in ("g_mix", "b_in", "ln_g", "ln_b", "b_rg_a", "b_rg_x", "lam", "g_mlp"):
            p[n] = row(w[n][l])
        return p

    shards = [shards_of(l) for l in range(L)]
    w_in_whole, = _comm_call("gather_first", _gather_comm([shards[0]["w_in"]], [GATHER_KIND["w_in"]]))
    xf = x.reshape(T, D)
    saved, params = [], []
    for l in range(L):
        xf, s, p, w_in_whole = _layer_fwd(xf, params_of(l, w_in_whole), S, cur=shards[l],
                                          nxt=shards[l + 1] if l + 1 < L else None)
        saved.append(s)
        params.append(p)
    loss_part, dx, dxb, dg_final = _loss_head(xf, row(g_final), loss_target.reshape(T, D), _tiles(T)[0])
    loss = lax.psum(loss_part[0, 0], ("x", "y", "c"))

    half_shape = lambda a: (L, 2, a.shape[1] // 2, a.shape[2])
    accs = {n: lax.empty(half_shape(w[n]), F32) for n in BIG}
    small = {n: [None] * L for n in SMALL if n != "g_final"}
    red = _Reduce(accs, c_arr, lambda l: jnp.stack([k_me, ci, jnp.full((), l, ci.dtype)]).astype(jnp.int32))
    for l in reversed(range(L)):
        dx, dxb, g = _layer_bwd(dx, dxb, params[l], saved[l], S, red=red, layer=l)
        small["g_mix"][l], small["b_in"][l], small["g_mlp"][l] = g["g_mix"], g["b_in"], g["g_mlp"]
        small["conv_a_w"][l], small["conv_a_b"][l] = g["caw"], g["cab"]
        small["conv_b_w"][l], small["conv_b_b"][l] = g["cbw"], g["cbb"]
        small["ln_g"][l], small["ln_b"][l], small["lam"][l] = g["ln_g"], g["ln_b"], g["lam"]
        small["w_rg_a"][l], small["w_rg_x"][l] = _block_diag_part(g["wa"], dh), _block_diag_part(g["wx"], dh)
        small["b_rg_a"][l], small["b_rg_x"][l] = g["b_rg_a"], g["b_rg_x"]
    grad_x = dx.reshape(B, S, D)

    delta, new_m, new_v = {}, {}, {}
    flat = lambda a: a.reshape(-1, a.shape[-1])

    def adam_big(names, comm=None):
        r = _adamw("adamw_" + names[0], *[[flat(d[n]) for n in names] for d in (w, grads, m, v)], ADAM_ROWS, comm=comm)
        for q, n in enumerate(names):
            delta[n], new_m[n], new_v[n] = (r[a][q].reshape(w[n].shape) for a in range(3))
        return r[3] if comm else None

    late_sums = red.chip_sums(red.late, _comm_call("swap_halves", _swap_comm(red.late)))
    joined = _join_halves([red.accs[n] for n in red.EARLY])
    grads = {n: a.reshape(w[n].shape) for n, a in zip(red.EARLY, joined)}
    got = adam_big(["w_2", "w_1"], comm=_scatter_comm(late_sums))
    adam_big(["w_o", "w_a_out"])
    red.finish(red.LATE, late_sums, got, 0)
    joined = _join_halves([red.accs[n] for n in red.LATE])
    grads.update({n: a.reshape(w[n].shape) for n, a in zip(red.LATE, joined)})
    adam_big(["w_b_out"])
    adam_big(["w_in"])

    wide = ["w_rg_a", "w_rg_x"]
    names = [n for n in SMALL if n != "g_final" and n not in wide]
    parts = [jnp.stack(small[n]) for n in names] + [dg_final]
    parts_w = [jnp.stack(small[n]).astype(BF16) for n in wide]
    total, total_w = _allreduce_small([_pack(parts, N_DEV, 8), _pack(parts_w, N_DEV, 16)])
    summed = _unpack(total, parts, 8) + [a.astype(F32) for a in _unpack(total_w, parts_w, 16)]
    for n, a in zip(names + ["g_final"] + wide, summed):
        if n == "conv_a_w":
            a = lax.dynamic_slice_in_dim(a[:, :taps_a], k_me * conv_a_w.shape[2], conv_a_w.shape[2], axis=2)
        elif n == "conv_b_w":
            a = lax.dynamic_slice_in_dim(a[:, :taps_b], k_me * conv_b_w.shape[2], conv_b_w.shape[2], axis=2)
        grads[n] = a.reshape(w[n].shape)

    for n in SMALL:
        cols = w[n].shape[-1]
        view = lambda a: a.reshape(-1, cols)
        rows = view(w[n]).shape[0]
        d_, m_, v_ = _adamw("adamw_" + n, view(w[n]), view(grads[n]), view(m[n]), view(v[n]),
                            ADAM_SMALL_ROWS if rows % ADAM_SMALL_ROWS == 0 else rows)
        delta[n], new_m[n], new_v[n] = (a.reshape(w[n].shape) for a in (d_, m_, v_))

    return (loss, grad_x, *[grads[n] for n in WEIGHTS], *[delta[n] for n in WEIGHTS],
            *[new_m[n] for n in WEIGHTS], *[new_v[n] for n in WEIGHTS])
```

```python
import jax
import jax.numpy as jnp
from jax import lax
from jax.experimental import pallas as pl
from jax.experimental.pallas import tpu as pltpu

F32 = jnp.float32
BF16 = jnp.bfloat16
MESH = pl.DeviceIdType.MESH

EPS = 1e-6
LRU_C = 8.0
ADAM_LR, ADAM_B1, ADAM_B2, ADAM_EPS, ADAM_WD, ADAM_STEP = 0.001, 0.9, 0.999, 1e-08, 0.01, 10

N_CHIPS = 4
HEADS_PER_GROUP = 4
VMEM_LIMIT = 56 * 1024 * 1024


def _cp(**kw):
    return pltpu.CompilerParams(vmem_limit_bytes=VMEM_LIMIT, **kw)


def _sig(x):
    return 1.0 / (1.0 + jnp.exp(-x))


def _gelu(x):
    t = jnp.tanh(0.7978845608028654 * (x + 0.044715 * x * x * x))
    return 0.5 * x * (1.0 + t), t


def _gelu_grad(x, t):
    dt = (1.0 - t * t) * 0.7978845608028654 * (1.0 + 3.0 * 0.044715 * x * x)
    return 0.5 * (1.0 + t) + 0.5 * x * dt


def _rms(xf, g):
    r = lax.rsqrt(jnp.mean(xf * xf, axis=-1, keepdims=True) + EPS)
    return xf * r * g, r


def _rms_bwd(xf, g, r, dh):
    dyg = dh * g
    dx = r * (dyg - xf * (r * r) * jnp.mean(dyg * xf, axis=-1, keepdims=True))
    return dx, dh * xf * r


def _ln_silu(u, g, b):
    mu = jnp.mean(u, axis=-1, keepdims=True)
    uc = u - mu
    rstd = lax.rsqrt(jnp.mean(uc * uc, axis=-1, keepdims=True) + EPS)
    uh = uc * rstd
    u2 = uh * g + b
    s = _sig(u2)
    return u2 * s, uh, rstd, u2, s


_DIMS = {"nn": (((1,), (0,)), ((), ())), "nt": (((1,), (1,)), ((), ())), "tn": (((0,), (0,)), ((), ()))}


class _Comm:
    def __init__(self, ins, outs, sems, start, finish):
        self.ins, self.outs, self.sems, self.start, self.finish = list(ins), list(outs), list(sems), start, finish


def _resident(shape):
    return pl.BlockSpec(shape, lambda i, j, k: (0,) * len(shape), pipeline_mode=pl.Buffered(1))


def _mm(name, mode, grid, a_ins, a_fn, b_in, e_ins, epi, outs, acc_shape, cache_a=None, alias=(), extra_scratch=(),
        comm=None, b_slice=None):
    ni, nj, nk = grid
    na, ne, no = len(a_ins), len(e_ins), len(outs)
    assert cache_a is None or nk == 1
    n_fixed = (nk > 1) + (cache_a is not None)
    n_in = na + 1 + ne + len(alias)
    c_ins, c_outs, c_sems = (comm.ins, comm.outs, comm.sems) if comm else ([], [], [])

    def body(*refs):
        a_refs = refs[:na]
        b_ref = refs[na]
        e_refs = refs[na + 1:na + 1 + ne]
        comm_in = refs[n_in:n_in + len(c_ins)]
        out0 = n_in + len(c_ins)
        out_refs = refs[out0:out0 + no]
        comm_out = refs[out0 + no:out0 + no + len(c_outs)]
        scratch = refs[out0 + no + len(c_outs):]
        extra = scratch[n_fixed:n_fixed + len(extra_scratch)]
        comm_sems = scratch[n_fixed + len(extra_scratch):]
        i, j, k = pl.program_id(0), pl.program_id(1), pl.program_id(2)
        if comm:
            @pl.when((i == 0) & (j == 0) & (k == 0))
            def _():
                comm.start(comm_in, comm_out, comm_sems)
        if cache_a is not None:
            cache_ref = scratch[n_fixed - 1]

            @pl.when(j == 0)
            def _():
                cache_ref[...] = a_fn(a_refs, out_refs, i, j, k)

            a = cache_ref[...]
        else:
            a = a_fn(a_refs, out_refs, i, j, k)
        if b_slice is None:
            b = b_ref[...]
        elif b_slice[0] == "cols":
            b = b_ref[:, pl.ds(pl.multiple_of(j * b_slice[1], b_slice[1]), b_slice[1])]
        else:
            b = b_ref[pl.ds(pl.multiple_of(j * b_slice[1], b_slice[1]), b_slice[1]), :]
        prod = lax.dot_general(a, b, _DIMS[mode], preferred_element_type=F32)
        if nk == 1:
            epi(prod, e_refs, out_refs, i, j, extra)
        else:
            acc_ref = scratch[0]

            @pl.when(k == 0)
            def _():
                acc_ref[...] = prod

            @pl.when(k > 0)
            def _():
                acc_ref[...] += prod

            @pl.when(k == nk - 1)
            def _():
                epi(acc_ref[...], e_refs, out_refs, i, j, extra)

        if comm:
            @pl.when((i == ni - 1) & (j == nj - 1) & (k == nk - 1))
            def _():
                comm.finish(comm_in, comm_out, comm_sems)

    scratch_shapes = []
    if nk > 1:
        scratch_shapes.append(pltpu.VMEM(acc_shape, F32))
    if cache_a is not None:
        scratch_shapes.append(pltpu.VMEM(cache_a, BF16))
    any_spec = pl.BlockSpec(memory_space=pl.ANY)
    ins = (list(a_ins) + [b_in] + list(e_ins) + [(arr, any_spec) for arr, _ in alias] + [(arr, any_spec) for arr in c_ins])
    first_alias = na + 1 + ne
    res = pl.pallas_call(
        body, name=name, grid=grid,
        in_specs=[s for _, s in ins], out_specs=[s for _, s in outs] + [any_spec] * len(c_outs),
        out_shape=[o for o, _ in outs] + list(c_outs),
        scratch_shapes=scratch_shapes + list(extra_scratch) + list(c_sems),
        input_output_aliases={first_alias + n: o for n, (_, o) in enumerate(alias)},
        compiler_params=_cp(dimension_semantics=("arbitrary", "arbitrary", "arbitrary"), has_side_effects=bool(comm)),
    )(*[a for a, _ in ins])
    if comm:
        return list(res[:no]), list(res[no:])
    return res


def _bs(shape, fn):
    return pl.BlockSpec(shape, fn)


def _sds(shape, dt):
    return jax.ShapeDtypeStruct(shape, dt)


def _acc_rows(ref, val, first):
    @pl.when(first)
    def _():
        ref[...] = val

    @pl.when(jnp.logical_not(first))
    def _():
        ref[...] += val


def _fwd_norm_mm(name, x, g, w, bias, tm, tn, comm=None, gate_tiles=()):
    T, D = x.shape
    N = w.shape[1]

    def a_fn(a_refs, out_refs, i, j, k):
        h, _ = _rms(a_refs[0][...], a_refs[1][...])
        hb = h.astype(BF16)
        out_refs[1][...] = hb
        return hb

    def epi(acc, e_refs, out_refs, i, j, extra):
        if bias is not None:
            acc = acc + e_refs[0][...]
        if gate_tiles:
            is_gate = _any_of([j == t for t in gate_tiles])

            @pl.when(is_gate)
            def _():
                out_refs[0][...] = _sig(acc).astype(BF16)

            @pl.when(jnp.logical_not(is_gate))
            def _():
                out_refs[0][...] = acc.astype(BF16)
        else:
            out_refs[0][...] = acc.astype(BF16)

    e_ins = [] if bias is None else [(bias, _bs((1, tn), lambda i, j, k: (0, j)))]
    return _mm(name, "nn", (T // tm, N // tn, 1),
               [(x, _bs((tm, D), lambda i, j, k: (i, 0))), (g, _bs((1, D), lambda i, j, k: (0, 0)))], a_fn,
               (w, _resident((D, N))), e_ins, epi,
               [(_sds((T, N), BF16), _bs((tm, tn), lambda i, j, k: (i, j))),
                (_sds((T, D), BF16), _bs((tm, D), lambda i, j, k: (i, 0)))],
               None, cache_a=(tm, D), comm=comm, b_slice=("cols", tn))


def _fwd_ya(u1, ln_g, ln_b, w, tm):
    T, C = u1.shape
    N = w.shape[1]

    def a_fn(a_refs, out_refs, i, j, k):
        u3 = _ln_silu(a_refs[0][...].astype(F32), a_refs[1][...], a_refs[2][...])[0].astype(BF16)
        out_refs[1][...] = u3
        return u3

    def epi(acc, e_refs, out_refs, i, j, extra):
        out_refs[0][...] = acc.astype(BF16)

    row = _bs((1, C), lambda i, j, k: (0, 0))
    tc = _bs((tm, C), lambda i, j, k: (i, 0))
    return _mm("fwd_ya", "nn", (T // tm, 1, 1), [(u1, tc), (ln_g, row), (ln_b, row)], a_fn,
               (w, _bs((C, N), lambda i, j, k: (0, 0))), [], epi,
               [(_sds((T, N), BF16), _bs((tm, N), lambda i, j, k: (i, 0))), (_sds((T, C), BF16), tc)], None)


def _fwd_yb(h, z, gb_blk, w, tm, tk):
    T, C = h.shape
    N = w.shape[1]

    def a_fn(a_refs, out_refs, i, j, k):
        ge, _ = _gelu(a_refs[1][...].astype(F32))
        pv = (a_refs[0][...].astype(F32) * ge).astype(BF16)
        out_refs[1][...] = pv
        return pv

    def epi(acc, e_refs, out_refs, i, j, extra):
        out_refs[0][...] = acc.astype(BF16)

    tk_ = _bs((tm, tk), lambda i, j, k: (i, k))
    return _mm("fwd_yb", "nn", (T // tm, 1, C // tk),
               [(h, tk_), (z, _bs((tm, tk), lambda i, j, k: (i, gb_blk + k)))], a_fn,
               (w, _bs((tk, N), lambda i, j, k: (k, 0))), [], epi,
               [(_sds((T, N), BF16), _bs((tm, N), lambda i, j, k: (i, 0))), (_sds((T, C), BF16), tk_)], (tm, N))


def _fwd_x1(x, ya, yb, z, sa_blk, w, tm):
    T, D = x.shape

    def a_fn(a_refs, out_refs, i, j, k):
        ya_, yb_, sa_, sb_ = (r[...].astype(F32) for r in a_refs)
        mg = (sa_ * ya_ + sb_ * yb_).astype(BF16)
        out_refs[1][...] = mg
        return mg

    def epi(acc, e_refs, out_refs, i, j, extra):
        out_refs[0][...] = e_refs[0][...] + acc

    t = _bs((tm, D), lambda i, j, k: (i, 0))
    return _mm("fwd_x1", "nn", (T // tm, 1, 1),
               [(ya, t), (yb, t), (z, _bs((tm, D), lambda i, j, k: (i, sa_blk))),
                (z, _bs((tm, D), lambda i, j, k: (i, sa_blk + 1)))], a_fn,
               (w, _bs((D, D), lambda i, j, k: (0, 0))), [(x, t)], epi,
               [(_sds((T, D), F32), t), (_sds((T, D), BF16), t)], None)


def _fwd_x2(x1, fp, w, tm, tk, comm=None):
    T, D = x1.shape
    Fd = fp.shape[1]

    def epi(acc, e_refs, out_refs, i, j, extra):
        out_refs[0][...] = e_refs[0][...] + acc

    t = _bs((tm, D), lambda i, j, k: (i, 0))
    whole_k = tk == Fd
    r = _mm("fwd_x2", "nn", (T // tm, 1, Fd // tk),
            [(fp, _bs((tm, tk), lambda i, j, k: (i, k)))], _relu2,
            (w, _resident((Fd, D)) if whole_k else _bs((tk, D), lambda i, j, k: (k, 0))), [(x1, t)], epi,
            [(_sds((T, D), F32), t)], (tm, D), comm=comm)
    return (r[0][0], r[1]) if comm else r[0]


def _relu2(a_refs, out_refs, i, j, k):
    f = jnp.maximum(a_refs[0][...], 0.0)
    return f * f


def _loss_head(x, g, target, tm):
    T, D = x.shape

    def body(x_ref, g_ref, t_ref, loss_ref, dx_ref, dxb_ref, dg_ref):
        i = pl.program_id(0)
        xf, gv = x_ref[...], g_ref[...]
        y, r = _rms(xf, gv)
        err = y - t_ref[...]
        part = 0.5 * jnp.sum(jnp.mean(err * err, axis=-1, keepdims=True), axis=0, keepdims=True)
        dx, dg_rows = _rms_bwd(xf, gv, r, err * (1.0 / D))
        dx_ref[...] = dx
        dxb_ref[...] = dx.astype(BF16)
        _acc_rows(loss_ref, jnp.broadcast_to(part, (1, 128)), i == 0)
        _acc_rows(dg_ref, jnp.sum(dg_rows, axis=0, keepdims=True), i == 0)

    t = _bs((tm, D), lambda i: (i, 0))
    row = _bs((1, D), lambda i: (0, 0))
    return pl.pallas_call(
        body, name="loss_head", grid=(T // tm,), in_specs=[t, row, t],
        out_specs=[_bs((1, 128), lambda i: (0, 0)), t, t, row],
        out_shape=[_sds((1, 128), F32), _sds((T, D), F32), _sds((T, D), BF16), _sds((1, D), F32)],
        compiler_params=_cp(dimension_semantics=("arbitrary",)),
    )(x, g, target)


def _adamw(name, w, g, m, v, tr, comm=None):
    many = isinstance(w, (list, tuple))
    ws, gs, ms, vs = (list(a) if many else [a] for a in (w, g, m, v))
    n = len(ws)
    rows, cols = ws[0].shape
    d1 = 1.0 - ADAM_B1 ** ADAM_STEP
    d2 = 1.0 - ADAM_B2 ** ADAM_STEP

    def body(*refs):
        for q in range(n):
            w_ref, g_ref, m_ref, v_ref = (refs[a * n + q] for a in range(4))
            d_ref, mo_ref, vo_ref = (refs[(4 + a) * n + q] for a in range(3))
            gv = g_ref[...]
            mn = ADAM_B1 * m_ref[...] + (1.0 - ADAM_B1) * gv
            vn = ADAM_B2 * v_ref[...] + (1.0 - ADAM_B2) * (gv * gv)
            d_ref[...] = -ADAM_LR * ((mn / d1) / (jnp.sqrt(vn / d2) + ADAM_EPS) + ADAM_WD * w_ref[...])
            mo_ref[...] = mn
            vo_ref[...] = vn

    t = _bs((tr, cols), lambda i: (i, 0))
    r = _call_with_comm(name, body, (rows // tr,), ws + gs + ms + vs, [t] * (4 * n), [t] * (3 * n),
                        [_sds((rows, cols), F32)] * (3 * n), [], comm)
    outs, got = (r if comm else (r, None))
    res = [outs[a * n:(a + 1) * n] if many else outs[a * n] for a in range(3)]
    return (*res, got) if comm else tuple(res)


def _ident(a_refs, out_refs, i, j, k):
    return a_refs[0][...]


def _bwd_dw(name, act, dy, ti, tj, tm, a_fn=None, a_extra=(), shard_cols=None, keep=None):
    T, J = dy.shape
    I = act.shape[1]

    def epi(acc, e_refs, out_refs, i, j, extra):
        out_refs[0][...] = acc.astype(BF16).reshape(out_refs[0].shape)

    if shard_cols is None:
        out = (_sds((I, J), BF16), _bs((ti, tj), lambda i, j, k: (i, j)))
    else:
        per = shard_cols // tj
        assert ti == I and per * tj == shard_cols
        out = (_sds((J // shard_cols, 2, I // 2, shard_cols), BF16),
               _bs((None, 2, I // 2, tj), lambda i, j, k: (lax.div(j, per), 0, 0, lax.rem(j, per))))
    assert keep is None or tm == T
    a_spec = _resident((T, I)) if keep == "act" else _bs((tm, ti), lambda i, j, k: (k, i))
    b_spec = _resident((T, J)) if keep == "dy" else _bs((tm, tj), lambda i, j, k: (k, j))
    return _mm(name, "tn", (I // ti, J // tj, T // tm), [(act, a_spec)] + list(a_extra), a_fn or _ident,
               (dy, b_spec), [], epi, [out], (ti, tj))[0]


def _bwd_df(dxb, w2, fp, tm, tn, comm=None):
    T, D = dxb.shape
    Fd = w2.shape[0]

    def epi(acc, e_refs, out_refs, i, j, extra):
        out_refs[0][...] = (acc * (2.0 * jnp.maximum(e_refs[0][...].astype(F32), 0.0))).astype(BF16)

    t = _bs((tm, tn), lambda i, j, k: (i, j))
    r = _mm("bwd_df", "nt", (T // tm, Fd // tn, 1), [(dxb, _bs((tm, D), lambda i, j, k: (i, 0)))], _ident,
            (w2, _resident((Fd, D))), [(fp, t)], epi, [(_sds((T, Fd), BF16), t)], None, b_slice=("rows", tn), comm=comm)
    return (r[0][0], r[1]) if comm else r[0]


def _bwd_norm(name, dy, w, x, g, dres, tm, tk, colsum=False, comm=None):
    T, K = dy.shape
    D = w.shape[0]
    nk = K // tk

    def a_fn(a_refs, out_refs, i, j, k):
        a = a_refs[0][...]
        if colsum:
            s = jnp.sum(a.astype(F32), axis=0, keepdims=True)

            @pl.when(i == 0)
            def _():
                out_refs[3][k] = s

            @pl.when(i > 0)
            def _():
                out_refs[3][k] += s
        return a

    def epi(acc, e_refs, out_refs, i, j, extra):
        xf, gv = e_refs[0][...], e_refs[1][...]
        r = lax.rsqrt(jnp.mean(xf * xf, axis=-1, keepdims=True) + EPS)
        dx, dg_rows = _rms_bwd(xf, gv, r, acc)
        dx = dx + e_refs[2][...]
        out_refs[0][...] = dx
        out_refs[1][...] = dx.astype(BF16)
        _acc_rows(out_refs[2], jnp.sum(dg_rows, axis=0, keepdims=True), i == 0)

    t = _bs((tm, D), lambda i, j, k: (i, 0))
    row = _bs((1, D), lambda i, j, k: (0, 0))
    outs = [(_sds((T, D), F32), t), (_sds((T, D), BF16), t), (_sds((1, D), F32), row)]
    if colsum:
        outs.append((_sds((nk, 1, tk), F32), _bs((nk, 1, tk), lambda i, j, k: (0, 0, 0))))
    return _mm(name, "nt", (T // tm, 1, nk), [(dy, _bs((tm, tk), lambda i, j, k: (i, k)))], a_fn,
               (w, _resident((D, K)) if nk == 1 else _bs((D, tk), lambda i, j, k: (0, k))),
               [(x, t), (g, row), (dres, t)], epi, outs, (tm, D), comm=comm)


def _bwd_dm(dxb, w_o, ya, yb, z, sa_blk, tm):
    T, D = dxb.shape

    def epi(acc, e_refs, out_refs, i, j, extra):
        ya_, yb_, sa_, sb_ = (r[...].astype(F32) for r in e_refs)
        ga, gb = sa_, sb_
        out_refs[0][...] = (acc * ga).astype(BF16)
        out_refs[1][...] = (acc * gb).astype(BF16)
        stage, sem = extra
        put = pltpu.make_async_copy(
            stage, out_refs[2].at[pl.ds(pl.multiple_of(i * tm, tm), tm), pl.ds(sa_blk * D, 2 * D)], sem)

        @pl.when(i > 0)
        def _():
            put.wait()

        stage[:, 0:D] = (acc * ya_ * ga * (1.0 - ga)).astype(BF16)
        stage[:, D:2 * D] = (acc * yb_ * gb * (1.0 - gb)).astype(BF16)
        put.start()

        @pl.when(i == T // tm - 1)
        def _():
            put.wait()

    t = _bs((tm, D), lambda i, j, k: (i, 0))
    return _mm("bwd_dm", "nt", (T // tm, 1, 1), [(dxb, t)], _ident, (w_o, _bs((D, D), lambda i, j, k: (0, 0))),
               [(ya, t), (yb, t), (z, _bs((tm, D), lambda i, j, k: (i, sa_blk))),
                (z, _bs((tm, D), lambda i, j, k: (i, sa_blk + 1)))], epi,
               [(_sds((T, D), BF16), t), (_sds((T, D), BF16), t),
                (_sds(z.shape, BF16), pl.BlockSpec(memory_space=pl.ANY))], None,
               extra_scratch=[pltpu.VMEM((tm, 2 * D), BF16), pltpu.SemaphoreType.DMA(())])


def _bwd_du3(dya, w, u1, ln_g, ln_b, tm):
    T, D = dya.shape
    C = w.shape[0]

    def epi(acc, e_refs, out_refs, i, j, extra):
        gv = e_refs[1][...]
        _, uh, rstd, u2, s = _ln_silu(e_refs[0][...].astype(F32), gv, e_refs[2][...])
        du2 = acc * (s * (1.0 + u2 * (1.0 - s)))
        duh = du2 * gv
        out_refs[0][...] = rstd * (duh - jnp.mean(duh, axis=-1, keepdims=True)
                                   - uh * jnp.mean(duh * uh, axis=-1, keepdims=True))
        _acc_rows(out_refs[1], jnp.sum(du2 * uh, axis=0, keepdims=True), i == 0)
        _acc_rows(out_refs[2], jnp.sum(du2, axis=0, keepdims=True), i == 0)

    t = _bs((tm, C), lambda i, j, k: (i, 0))
    row = _bs((1, C), lambda i, j, k: (0, 0))
    return _mm("bwd_du3", "nt", (T // tm, 1, 1), [(dya, _bs((tm, D), lambda i, j, k: (i, 0)))], _ident,
               (w, _bs((C, D), lambda i, j, k: (0, 0))), [(u1, t), (ln_g, row), (ln_b, row)], epi,
               [(_sds((T, C), F32), t), (_sds((1, C), F32), row), (_sds((1, C), F32), row)], None)


def _bwd_dp(dyb, w, h, z, dz, gb_blk, tm, tn, comm=None):
    T, D = dyb.shape
    R = w.shape[0]

    def epi(acc, e_refs, out_refs, i, j, extra):
        gbv = e_refs[1][...].astype(F32)
        ge, th = _gelu(gbv)
        out_refs[0][...] = acc * ge
        out_refs[1][...] = (acc * e_refs[0][...].astype(F32) * _gelu_grad(gbv, th)).astype(BF16)

    t = _bs((tm, tn), lambda i, j, k: (i, j))
    tz = _bs((tm, tn), lambda i, j, k: (i, gb_blk + j))
    return _mm("bwd_dp", "nt", (T // tm, R // tn, 1), [(dyb, _bs((tm, D), lambda i, j, k: (i, 0)))], _ident,
               (w, _bs((tn, D), lambda i, j, k: (j, 0))), [(h, t), (z, tz)], epi,
               [(_sds((T, R), F32), t), (_sds(dz.shape, BF16), tz)], None, cache_a=None, alias=[(dz, 1)], comm=comm)


CONV_ROWS = 32


def _shifted_taps(x, halo, shifts, fn):
    n = CONV_ROWS + halo
    by_r = {}
    for k, s in shifts:
        by_r.setdefault(s % 8, []).append((k, s))
    for r in sorted(by_r):
        xr = x if r == 0 else pltpu.roll(x, n - r, 0)
        for k, s in by_r[r]:
            q = s - r
            fn(k, xr[q:q + CONV_ROWS])


def _conv_fwd(name, z, blk0, gate_blk0, w_pad, bias, taps, seq, tc, out_dtype, comm=None):
    T = z.shape[0]
    C = w_pad.shape[1]
    nb, nj = T // seq, C // tc
    pad = 8 * ((taps - 1 + 7) // 8)
    halo = pad
    shifts = [(k, pad - (taps - 1) + k) for k in range(taps)]
    glu = gate_blk0 is not None

    def body(*refs):
        if glu:
            v_ref, g_ref, w_ref, b_ref, o_ref, p_ref = refs
        else:
            v_ref, w_ref, b_ref, o_ref, p_ref = refs
        p_ref[pl.ds(0, pad), :] = jnp.zeros((pad, tc), F32)
        u = v_ref[...].astype(F32)
        if glu:
            u = u * g_ref[...].astype(F32)
        p_ref[pl.ds(pad, seq), :] = u

        def step(c, _):
            base = pl.multiple_of(c * CONV_ROWS, CONV_ROWS)
            x = p_ref[pl.ds(base, CONV_ROWS + halo), :]
            acc = [jnp.zeros((CONV_ROWS, tc), F32) + b_ref[...]]

            def tap(k, xs):
                acc[0] = acc[0] + w_ref[k:k + 1, :] * xs

            _shifted_taps(x, halo, shifts, tap)
            o_ref[pl.ds(base, CONV_ROWS), :] = acc[0].astype(out_dtype)
            return 0

        lax.fori_loop(0, seq // CONV_ROWS, step, 0)

    zin = [(z, _bs((seq, tc), lambda b, j: (b, blk0 + j)))]
    if glu:
        zin.append((z, _bs((seq, tc), lambda b, j: (b, gate_blk0 + j))))
    ins = zin + [(w_pad, _bs((w_pad.shape[0], tc), lambda b, j: (0, j))), (bias, _bs((1, tc), lambda b, j: (0, j)))]
    r = _call_with_comm(name, body, (nb, nj), [a for a, _ in ins], [s for _, s in ins],
                        [_bs((seq, tc), lambda b, j: (b, j))], [_sds((T, C), out_dtype)],
                        [pltpu.VMEM((seq + pad, tc), F32)], comm)
    return (r[0][0], r[1]) if comm else r[0]


def _conv_bwd(name, dy, z, dz, blk0, gate_blk0, w_pad, taps, seq, tc, comm=None):
    T = z.shape[0]
    C = w_pad.shape[1]
    nb, nj = T // seq, C // tc
    kp = w_pad.shape[0]
    pad = 8 * ((taps - 1 + 7) // 8)
    halo = pad
    sh_du = [(k, taps - 1 - k) for k in range(taps)]
    sh_dw = [(k, pad - (taps - 1) + k) for k in range(taps)]
    glu = gate_blk0 is not None

    def body(*refs):
        if glu:
            dy_ref, v_ref, g_ref, w_ref, _dz_in, dz_out, dw_ref, db_ref, pdy, pu, du_s, wacc, ob, ob2, osem = refs
        else:
            dy_ref, v_ref, w_ref, _dz_in, dz_out, dw_ref, db_ref, pdy, pu, du_s, wacc, ob, osem = refs
        j = pl.program_id(0)
        b = pl.program_id(1)
        pdy[pl.ds(seq, pad), :] = jnp.zeros((pad, tc), F32)
        pdy[pl.ds(0, seq), :] = dy_ref[...].astype(F32)
        pu[pl.ds(0, pad), :] = jnp.zeros((pad, tc), F32)
        v = v_ref[...].astype(F32)
        if glu:
            sg = g_ref[...].astype(F32)
            pu[pl.ds(pad, seq), :] = v * sg
        else:
            pu[pl.ds(pad, seq), :] = v
        wacc[...] = jnp.zeros(wacc.shape, F32)

        def step(c, dbacc):
            base = pl.multiple_of(c * CONV_ROWS, CONV_ROWS)
            xdy = pdy[pl.ds(base, CONV_ROWS + halo), :]
            acc = [jnp.zeros((CONV_ROWS, tc), F32)]

            def tap(k, xs):
                acc[0] = acc[0] + w_ref[k:k + 1, :] * xs

            _shifted_taps(xdy, halo, sh_du, tap)
            du_s[pl.ds(base, CONV_ROWS), :] = acc[0]
            dyc = xdy[0:CONV_ROWS]
            xu = pu[pl.ds(base, CONV_ROWS + halo), :]

            def wtap(k, xs):
                p = dyc * xs
                s8 = p[0:8]
                for m in range(1, CONV_ROWS // 8):
                    s8 = s8 + p[8 * m:8 * m + 8]
                wacc[pl.ds(8 * k, 8), :] += s8

            _shifted_taps(xu, halo, sh_dw, wtap)
            d8 = dyc[0:8]
            for m in range(1, CONV_ROWS // 8):
                d8 = d8 + dyc[8 * m:8 * m + 8]
            return dbacc + d8

        dbacc = lax.fori_loop(0, seq // CONV_ROWS, step, jnp.zeros((8, tc), F32))
        du = du_s[...]
        rows = pl.ds(pl.multiple_of(b * seq, seq), seq)
        puts = [pltpu.make_async_copy(ob, dz_out.at[rows, pl.ds(pl.multiple_of((blk0 + j) * tc, tc), tc)], osem.at[0])]
        if glu:
            puts.append(pltpu.make_async_copy(
                ob2, dz_out.at[rows, pl.ds(pl.multiple_of((gate_blk0 + j) * tc, tc), tc)], osem.at[1]))

        @pl.when((j > 0) | (b > 0))
        def _():
            for cp in puts:
                cp.wait()

        if glu:
            ob[...] = (du * sg).astype(BF16)
            ob2[...] = (du * v * sg * (1.0 - sg)).astype(BF16)
        else:
            ob[...] = du.astype(BF16)
        for cp in puts:
            cp.start()

        @pl.when((j == nj - 1) & (b == nb - 1))
        def _():
            for cp in puts:
                cp.wait()
        dw = jnp.sum(wacc[...].reshape(kp, 8, tc), axis=1)
        _acc_rows(dw_ref, dw, b == 0)
        _acc_rows(db_ref, jnp.sum(dbacc, axis=0, keepdims=True), b == 0)

    zin = [(z, _bs((seq, tc), lambda j, b: (b, blk0 + j)))]
    if glu:
        zin.append((z, _bs((seq, tc), lambda j, b: (b, gate_blk0 + j))))
    ins = [(dy, _bs((seq, tc), lambda j, b: (b, j)))] + zin + [(w_pad, _bs((kp, tc), lambda j, b: (0, j))),
                                                               (dz, pl.BlockSpec(memory_space=pl.ANY))]
    dz_idx = len(ins) - 1
    out_specs = [pl.BlockSpec(memory_space=pl.ANY), _bs((kp, tc), lambda j, b: (0, j)), _bs((1, tc), lambda j, b: (0, j))]
    out_shape = [_sds(dz.shape, dz.dtype), _sds((kp, C), F32), _sds((1, C), F32)]
    stage = [pltpu.VMEM((seq, tc), BF16)] * (2 if glu else 1) + [pltpu.SemaphoreType.DMA((2,))]
    return _call_with_comm(
        name, body, (nj, nb), [a for a, _ in ins], [s for _, s in ins], out_specs, out_shape,
        [pltpu.VMEM((seq + pad, tc), F32), pltpu.VMEM((seq + pad, tc), F32),
         pltpu.VMEM((seq, tc), F32), pltpu.VMEM((8 * kp, tc), F32)] + stage, comm, aliases={dz_idx: 0})


RG_ROWS = 256


def _softplus_neg(lam):
    return jnp.maximum(-lam, 0.0) + jnp.log(1.0 + jnp.exp(-jnp.abs(lam)))


def _gates(v0c, wa_ref, wx_ref, ba, bx, sp):
    vb = v0c.astype(BF16)
    r = _sig(jnp.dot(vb, wa_ref[...], preferred_element_type=F32) + ba)
    i = _sig(jnp.dot(vb, wx_ref[...], preferred_element_type=F32) + bx)
    return r, i, -LRU_C * r * sp


def _decay(la, first_row):
    a = jnp.exp(la)
    a2 = a * a
    x = 2.0 * la
    series = -x * (1.0 + x * (1.0 / 2) * (1.0 + x * (1.0 / 3) * (1.0 + x * (1.0 / 4) * (1.0 + x * (1.0 / 5)))))
    mult = jnp.sqrt(jnp.where(x > -0.1, series, 1.0 - a2))
    dmult = jnp.where(first_row, 0.0, -a2 / mult)
    mult = jnp.where(first_row, 1.0, mult)
    return a, mult, dmult


def _group_scan(a, b, reverse):
    n = a.shape[0]
    row = lax.broadcasted_iota(jnp.int32, a.shape, 0) & 7
    for d in (1, 2, 4):
        sh = n - d if reverse else d
        a_s, b_s = pltpu.roll(a, sh, 0), pltpu.roll(b, sh, 0)
        m = (row < 8 - d) if reverse else (row >= d)
        b = jnp.where(m, a * b_s + b, b)
        a = jnp.where(m, a * a_s, a)
    return a, b


def _group_carry(a_s, b_s, o_s, n_groups, reverse):
    cols = a_s.shape[1]

    def step(g, carry):
        g = n_groups - 1 - g if reverse else g
        rows = pl.ds(pl.multiple_of(g * 8, 8), 8)
        o = a_s[rows, :] * carry + b_s[rows, :]
        o_s[rows, :] = o
        return o[0:1, :] if reverse else o[7:8, :]

    lax.fori_loop(0, n_groups, step, jnp.zeros((1, cols), F32), unroll=2)


def _rglru_fwd(v0, wa, wx, ba, bx, lam, seq, comm=None):
    T, C = v0.shape
    ng, G = wa.shape[0], wa.shape[1]
    nb = T // seq

    def body(v_ref, wa_ref, wx_ref, ba_ref, bx_ref, lam_ref, h_ref, r_ref, i_ref, la_ref, a_s, b_s, h_s):
        sp = _softplus_neg(lam_ref[...])

        def chunk(c, _):
            rows = pl.ds(pl.multiple_of(c * RG_ROWS, RG_ROWS), RG_ROWS)
            t = lax.broadcasted_iota(jnp.int32, (RG_ROWS, G), 0) + c * RG_ROWS
            v0c = v_ref[rows, :]
            r, i, la = _gates(v0c, wa_ref, wx_ref, ba_ref[...], bx_ref[...], sp)
            r_ref[rows, :] = r.astype(BF16)
            i_ref[rows, :] = i.astype(BF16)
            la_ref[rows, :] = la
            a, mult, _ = _decay(la, t == 0)
            a_g, b_g = _group_scan(a, mult * i * v0c, False)
            a_s[rows, :] = a_g
            b_s[rows, :] = b_g
            return 0

        lax.fori_loop(0, seq // RG_ROWS, chunk, 0)
        _group_carry(a_s, b_s, h_s, seq // 8, False)
        h_ref[...] = h_s[...].astype(BF16)

    t2 = _bs((seq, G), lambda b, g: (b, g))
    wsp = _bs((None, G, G), lambda b, g: (g, 0, 0))
    row = _bs((1, G), lambda b, g: (0, g))
    return _call_with_comm("rglru_fwd", body, (nb, ng), [v0, wa, wx, ba, bx, lam], [t2, wsp, wsp, row, row, row],
                           [t2] * 4, [_sds((T, C), BF16)] * 3 + [_sds((T, C), F32)], [pltpu.VMEM((seq, G), F32)] * 3, comm)


def _call_with_comm(name, body, grid, ins, in_specs, out_specs, out_shape, scratch, comm, aliases=None):
    n_in, n_out, n_s = len(ins), len(out_shape), len(scratch)
    c_ins, c_outs, c_sems = (comm.ins, comm.outs, comm.sems) if comm else ([], [], [])

    def wrapped(*refs):
        o0 = n_in + len(c_ins)
        s0 = o0 + n_out + len(c_outs)
        cin, cout, csem = refs[n_in:o0], refs[o0 + n_out:s0], refs[s0 + n_s:]
        ids = [pl.program_id(a) for a in range(len(grid))]
        if comm:
            first = _all_of([i == 0 for i in ids])

            @pl.when(first)
            def _():
                comm.start(cin, cout, csem)

        body(*refs[:n_in], *refs[o0:o0 + n_out], *refs[s0:s0 + n_s])
        if comm:
            last = _all_of([i == n - 1 for i, n in zip(ids, grid)])

            @pl.when(last)
            def _():
                comm.finish(cin, cout, csem)

    res = pl.pallas_call(
        wrapped, name=name, grid=grid, in_specs=list(in_specs) + [ANY] * len(c_ins),
        out_specs=list(out_specs) + [ANY] * len(c_outs), out_shape=list(out_shape) + list(c_outs),
        scratch_shapes=list(scratch) + list(c_sems), input_output_aliases=aliases or {},
        compiler_params=_cp(dimension_semantics=("arbitrary",) * len(grid), has_side_effects=bool(comm)),
    )(*ins, *c_ins)
    return (list(res[:n_out]), list(res[n_out:])) if comm else list(res)


def _all_of(conds):
    out = conds[0]
    for c in conds[1:]:
        out = out & c
    return out


def _any_of(conds):
    out = conds[0]
    for c in conds[1:]:
        out = out | c
    return out


def _rglru_bwd(v0, h, dh, r_g, i_g, la_g, wa, wx, lam, seq, comm=None):
    T, C = v0.shape
    ng, G = wa.shape[0], wa.shape[1]
    nb = T // seq
    R = RG_ROWS

    def body(v_ref, h_ref, dh_ref, r_ref, i_ref, la_ref, wa_ref, wx_ref, lam_ref,
             dv_ref, dwa_ref, dwx_ref, dba_ref, dbx_ref, dlam_ref, a_s, b_s, q_s, hp_s):
        b = pl.program_id(1)
        lam_v = lam_ref[...]
        sp = _softplus_neg(lam_v)
        dsp_dlam = -_sig(-lam_v)

        @pl.when(b == 0)
        def _():
            dwa_ref[...] = jnp.zeros((G, G), F32)
            dwx_ref[...] = jnp.zeros((G, G), F32)
            dba_ref[...] = jnp.zeros((1, G), F32)
            dbx_ref[...] = jnp.zeros((1, G), F32)
            dlam_ref[...] = jnp.zeros((1, G), F32)

        hp_s[pl.ds(0, 8), :] = jnp.zeros((8, G), F32)
        hp_s[pl.ds(8, seq), :] = h_ref[...].astype(F32)
        q_s[pl.ds(seq, 8), :] = jnp.zeros((8, G), F32)

        def chunk1(c, _):
            rows = pl.ds(pl.multiple_of(c * R, R), R)
            a = jnp.exp(la_ref[rows, :])
            a_g, b_g = _group_scan(a, a * dh_ref[rows, :].astype(F32), True)
            a_s[rows, :] = a_g
            b_s[rows, :] = b_g
            return 0

        lax.fori_loop(0, seq // R, chunk1, 0)
        _group_carry(a_s, b_s, q_s, seq // 8, True)

        def chunk3(c, _):
            base = pl.multiple_of(c * R, R)
            rows = pl.ds(base, R)
            t = lax.broadcasted_iota(jnp.int32, (R, G), 0) + c * R
            v0c = v_ref[rows, :]
            r, i = r_ref[rows, :].astype(F32), i_ref[rows, :].astype(F32)
            a, mult, dmult_dla = _decay(la_ref[rows, :], t == 0)
            q_next = pltpu.roll(q_s[pl.ds(base, R + 8), :], R + 7, 0)[0:R]
            h_prev = pltpu.roll(hp_s[pl.ds(base, R + 8), :], R + 1, 0)[0:R]
            gt = dh_ref[rows, :].astype(F32) + q_next
            dla = gt * h_prev * a + gt * i * v0c * dmult_dla
            dpa = dla * (-LRU_C * sp) * r * (1.0 - r)
            dpx = gt * mult * v0c * i * (1.0 - i)
            dpa_b, dpx_b, v_b = dpa.astype(BF16), dpx.astype(BF16), v0c.astype(BF16)
            dv_ref[rows, :] = (gt * mult * i
                               + lax.dot_general(dpa_b, wa_ref[...], _DIMS["nt"], preferred_element_type=F32)
                               + lax.dot_general(dpx_b, wx_ref[...], _DIMS["nt"], preferred_element_type=F32))
            dwa_ref[...] += lax.dot_general(v_b, dpa_b, _DIMS["tn"], preferred_element_type=F32)
            dwx_ref[...] += lax.dot_general(v_b, dpx_b, _DIMS["tn"], preferred_element_type=F32)
            dba_ref[...] += jnp.sum(dpa, axis=0, keepdims=True)
            dbx_ref[...] += jnp.sum(dpx, axis=0, keepdims=True)
            dlam_ref[...] += jnp.sum(dla * (-LRU_C * r), axis=0, keepdims=True) * dsp_dlam
            return 0

        lax.fori_loop(0, seq // R, chunk3, 0)

    t2 = _bs((seq, G), lambda g, b: (b, g))
    wsp = _bs((None, G, G), lambda g, b: (g, 0, 0))
    row = _bs((1, G), lambda g, b: (0, g))
    return _call_with_comm(
        "rglru_bwd", body, (ng, nb), [v0, h, dh, r_g, i_g, la_g, wa, wx, lam], [t2] * 6 + [wsp, wsp, row],
        [t2, wsp, wsp, row, row, row],
        [_sds((T, C), F32), _sds((ng, G, G), F32), _sds((ng, G, G), F32),
         _sds((1, C), F32), _sds((1, C), F32), _sds((1, C), F32)],
        [pltpu.VMEM((seq, G), F32), pltpu.VMEM((seq, G), F32),
         pltpu.VMEM((seq + 8, G), F32), pltpu.VMEM((seq + 8, G), F32)], comm)


TC_A = 256
TC_B = 512
TAPS_A, TAPS_B = 31, 4


def _tiles(T):
    return min(512, T), min(1024, T)


GATHERED = ("w_in", "w_1", "w_a_out", "w_b_out", "w_o", "w_2", "caw", "cbw")
GATHER_KIND = {"w_in": (True, True), "w_1": (True, True), "w_a_out": (False, True), "w_b_out": (False, True),
               "w_o": (False, True), "w_2": (False, True), "caw": (True, False), "cbw": (True, False)}


def _layer_fwd(x, p, seq, cur=None, nxt=None):
    T, D = x.shape
    C, R = p["ln_g"].shape[1], p["lam"].shape[1]
    tm, tl = _tiles(T)
    gb_blk, sa_blk = (2 * C + R) // TC_B, (2 * C + 2 * R) // D
    p, ahead = dict(p), {}

    def gather(src, names):
        return None if src is None else _gather_comm([src[n] for n in names], [GATHER_KIND[n] for n in names])

    def outs(r, src, names, into):
        if src is None:
            return r
        into.update(zip(names, r[1]))
        return r[0]

    mid = ["w_a_out", "w_b_out", "w_o", "caw", "cbw"]
    tn = 1024
    assert C == tn and D == tn and (2 * R) % tn == 0
    gates = (C // tn, (2 * C + 2 * R) // tn, (2 * C + 2 * R + D) // tn)
    z, h = outs(_fwd_norm_mm("fwd_z", x, p["g_mix"], p["w_in"], p["b_in"], tl, tn, comm=gather(cur, mid),
                             gate_tiles=gates), cur, mid, p)
    u1 = outs(_conv_fwd("conv_a_fwd", z, 0, C // TC_A, p["caw"], p["cab"], TAPS_A, seq, TC_A, BF16,
                        comm=gather(cur, ["w_1"])), cur, ["w_1"], p)
    ya, u3 = _fwd_ya(u1, p["ln_g"], p["ln_b"], p["w_a_out"], tm)
    v0 = _conv_fwd("conv_b_fwd", z, 2 * C // TC_B, None, p["cbw"], p["cbb"], TAPS_B, seq, TC_B, F32)
    hr, rg, ig, lag = outs(_rglru_fwd(v0, p["wa"], p["wx"], p["b_rg_a"], p["b_rg_x"], p["lam"], seq,
                                      comm=gather(nxt, ["w_in"])), nxt, ["w_in"], ahead)
    yb, pb = _fwd_yb(hr, z, gb_blk, p["w_b_out"], tl, TC_B)
    x1, mg = _fwd_x1(x, ya, yb, z, sa_blk, p["w_o"], tm)
    fp, h2 = outs(_fwd_norm_mm("fwd_f", x1, p["g_mlp"], p["w_1"], None, tl, 1024, comm=gather(cur, ["w_2"])),
                  cur, ["w_2"], p)
    x2 = _fwd_x2(x1, fp, p["w_2"], tm, fp.shape[1])
    saved = dict(x=x, z=z, h=h, u1=u1, u3=u3, ya=ya, v0=v0, hr=hr, rg=rg, ig=ig, lag=lag, pb=pb, yb=yb, mg=mg, x1=x1,
                 fp=fp, h2=h2)
    return x2, saved, p, ahead.get("w_in")


class _Reduce:
    EARLY = ("w_2", "w_1", "w_o", "w_a_out")
    LATE = ("w_b_out", "w_in")

    def __init__(self, accs, c_arr, kcl_of):
        self.accs, self.c_arr, self.kcl_of, self.late = accs, c_arr, kcl_of, None

    @staticmethod
    def pieces(partials):
        return [a if a.ndim == 4 else a.reshape(N_CHIPS, 2, a.shape[0] // (2 * N_CHIPS), a.shape[1]) for a in partials]

    def chip_sums(self, pgs, swapped):
        return _sum_siblings(pgs, swapped, self.c_arr)

    def finish(self, names, sums, received, layer):
        done = _sum_chips(sums, received, self.kcl_of(layer), [self.accs[n] for n in names])
        self.accs.update(zip(names, done))


def _layer_bwd(dx2, dx2b, p, s, seq, red=None, layer=0):
    T, D = dx2.shape
    C, R = p["ln_g"].shape[1], p["lam"].shape[1]
    tm, tl = _tiles(T)
    gb_blk, sa_blk = (2 * C + R) // TC_B, (2 * C + 2 * R) // D
    z = s["z"]
    g = {}


    late_sums = None
    if red is not None and red.late is not None:
        late, red.late = red.late, None
        dfp, got = _bwd_df(dx2b, p["w_2"], s["fp"], tl, 1024, comm=_swap_comm(late))
        late_sums = red.chip_sums(late, got)
    else:
        dfp = _bwd_df(dx2b, p["w_2"], s["fp"], tl, 1024)
    g["w_2"] = _bwd_dw("bwd_dw2", s["fp"], dx2b, 1024, D, T, a_fn=_relu2, keep="dy")
    dx1, dx1b, g["g_mlp"] = _bwd_norm("bwd_dh2", dfp, p["w_1"], s["x1"], p["g_mlp"], dx2, tm, dfp.shape[1])
    g["w_1"] = _bwd_dw("bwd_dw1", s["h2"], dfp, D, 1024, T, shard_cols=dfp.shape[1] // N_CHIPS, keep="act")

    dya, dyb, dz = _bwd_dm(dx1b, p["w_o"], s["ya"], s["yb"], z, sa_blk, tm)

    g["w_o"] = _bwd_dw("bwd_dwo", s["mg"], dx1b, D, D, tl)
    du1, g["ln_g"], g["ln_b"] = _bwd_du3(dya, p["w_a_out"], s["u1"], p["ln_g"], p["ln_b"], tm)
    g["w_a_out"] = _bwd_dw("bwd_dwa", s["u3"], dya, C, D, tl)
    conv_a_args = ("conv_a_bwd", du1, z, dz, 0, C // TC_A, p["caw"], TAPS_A, seq, TC_A)
    if late_sums is not None:
        (dz, g["caw"], g["cab"]), got = _conv_bwd(*conv_a_args, comm=_scatter_comm(late_sums))
        red.finish(red.LATE, late_sums, got, layer + 1)
    else:
        dz, g["caw"], g["cab"] = _conv_bwd(*conv_a_args)

    dp_args = (dyb, p["w_b_out"], s["hr"], z, dz, gb_blk, tl, TC_B)
    if red is not None:
        early = red.pieces([g.pop(n) for n in red.EARLY])
        (dhr, dz), got = _bwd_dp(*dp_args, comm=_swap_comm(early))
        early_sums = red.chip_sums(early, got)
    else:
        dhr, dz = _bwd_dp(*dp_args)

    g["w_b_out"] = _bwd_dw("bwd_dwb", s["pb"], dyb, R, D, tl)
    rg_args = (s["v0"], s["hr"], dhr, s["rg"], s["ig"], s["lag"], p["wa"], p["wx"], p["lam"], seq)
    if red is not None:
        rg_out, got = _rglru_bwd(*rg_args, comm=_scatter_comm(early_sums))
        red.finish(red.EARLY, early_sums, got, layer)
    else:
        rg_out = _rglru_bwd(*rg_args)
    dv0, g["wa"], g["wx"], g["b_rg_a"], g["b_rg_x"], g["lam"] = rg_out
    dz, g["cbw"], g["cbb"] = _conv_bwd("conv_b_bwd", dv0, z, dz, 2 * C // TC_B, None, p["cbw"], TAPS_B, seq, TC_B)

    dx, dxb, g["g_mix"], dbin = _bwd_norm("bwd_dh", dz, p["w_in"], s["x"], p["g_mix"], dx1, tm, dz.shape[1],
                                          colsum=True)
    g["b_in"] = dbin.reshape(1, -1)
    ns = dz.shape[1] // N_CHIPS
    g["w_in"] = _bwd_dw("bwd_dwin", s["h"], dz, D, ns // 2, T, shard_cols=ns, keep="act")
    if red is not None:
        red.late = red.pieces([g.pop(n) for n in red.LATE])
    return dx, dxb, g


ANY = pl.BlockSpec(memory_space=pl.ANY)


def _mesh_pos():
    return lax.axis_index("x"), lax.axis_index("y"), lax.axis_index("c")


def _other_chips(x, y):
    return [(1 - x, y), (x, 1 - y), (1 - x, 1 - y)]


def _remote(src, dst, ssem, rsem, dev):
    return pltpu.make_async_remote_copy(src_ref=src, dst_ref=dst, send_sem=ssem, recv_sem=rsem,
                                        device_id=dev, device_id_type=MESH)


def _gather_region(src, dst, by_cols, k, half):
    rows, cols = src.shape
    nr = rows if half is None else rows // 2
    r0 = 0 if half is None else half * nr
    if by_cols:
        return dst.at[pl.ds(r0, nr), pl.ds(pl.multiple_of(k * cols, 128), cols)]
    return dst.at[pl.ds(pl.multiple_of(k * rows + r0, 8), nr), :]


def _gather_sends(src, dst, kinds, send, recv):
    x, y, c = _mesh_pos()
    cps = []
    for t in range(len(src)):
        half = c if kinds[t][1] else None
        hr = src[t].shape[0] // 2
        s_ref = src[t].at[pl.ds(c * hr, hr), :] if kinds[t][1] else src[t]
        for j, chip in enumerate(_other_chips(x, y)):
            cps.append(_remote(s_ref, _gather_region(src[t], dst[t], kinds[t][0], 2 * x + y, half),
                               send.at[t, j], recv.at[t, j], (*chip, c)))
    return cps


def _gather_finish(src, dst, kinds, send, recv, fsend, frecv):
    x, y, c = _mesh_pos()
    chips = _other_chips(x, y)
    sib = (x, y, 1 - c)
    n = len(src)
    fwd = []
    for t in range(n):
        half = c if kinds[t][1] else None
        for j, chip in enumerate(chips):
            got = _gather_region(src[t], dst[t], kinds[t][0], 2 * chip[0] + chip[1], half)
            _remote(got, got, send.at[t, j], recv.at[t, j], (*chip, c)).wait_recv()
            if kinds[t][1]:
                cp = _remote(got, got, fsend.at[t, j], frecv.at[t, j], sib)
                cp.start()
                fwd.append(cp)
    for t in range(n):
        if kinds[t][1]:
            for j, chip in enumerate(chips):
                got = _gather_region(src[t], dst[t], kinds[t][0], 2 * chip[0] + chip[1], 1 - c)
                _remote(got, got, fsend.at[t, j], frecv.at[t, j], sib).wait_recv()
    for cp in _gather_sends(src, dst, kinds, send, recv) + fwd:
        cp.wait_send()


def _gather_sems(n):
    sem = pltpu.SemaphoreType.DMA
    return [sem((n, 3)), sem((n, 3)), sem((n, 3)), sem((n, 3))]


def _gather_comm(shards, kinds):
    n = len(shards)

    def whole(s, by_cols):
        return (s.shape[0], N_CHIPS * s.shape[1]) if by_cols else (N_CHIPS * s.shape[0], s.shape[1])

    def own(src, dst, lsem):
        x, y, _ = _mesh_pos()
        return [pltpu.make_async_copy(src[t], _gather_region(src[t], dst[t], kinds[t][0], 2 * x + y, None), lsem.at[t])
                for t in range(n)]

    def start(src, dst, sems):
        for cp in own(src, dst, sems[4]) + _gather_sends(src, dst, kinds, sems[0], sems[1]):
            cp.start()

    def finish(src, dst, sems):
        _gather_finish(src, dst, kinds, *sems[:4])
        for cp in own(src, dst, sems[4]):
            cp.wait()

    return _Comm(shards, [_sds(whole(s, k[0]), s.dtype) for s, k in zip(shards, kinds)],
                 _gather_sems(n) + [pltpu.SemaphoreType.DMA((n,))], start, finish)


def _scatter_comm(ps):
    n = len(ps)

    def copies(src, dst, sems):
        x, y, c = _mesh_pos()
        return [_remote(src[t].at[2 * chip[0] + chip[1]], dst[t].at[j], sems[0].at[t, j], sems[1].at[t, j], (*chip, c))
                for t in range(n) for j, chip in enumerate(_other_chips(x, y))]

    def start(src, dst, sems):
        for cp in copies(src, dst, sems):
            cp.start()

    def finish(src, dst, sems):
        cps = copies(src, dst, sems)
        for cp in cps:
            cp.wait_recv()
        for cp in cps:
            cp.wait_send()

    sem = pltpu.SemaphoreType.DMA
    return _Comm(ps, [_sds((3,) + a.shape[1:], a.dtype) for a in ps], [sem((n, 3)), sem((n, 3))], start, finish)


def _comm_call(name, comm):
    n_i, n_o = len(comm.ins), len(comm.outs)

    def body(*refs):
        comm.start(refs[:n_i], refs[n_i:n_i + n_o], refs[n_i + n_o:])
        comm.finish(refs[:n_i], refs[n_i:n_i + n_o], refs[n_i + n_o:])

    return pl.pallas_call(
        body, name=name, in_specs=[ANY] * n_i, out_specs=[ANY] * n_o, out_shape=comm.outs, scratch_shapes=comm.sems,
        compiler_params=_cp(has_side_effects=True),
    )(*comm.ins)


def _swap_comm(pgs):
    n = len(pgs)

    def copies(src, dst, sems):
        x, y, c = _mesh_pos()
        return [_remote(src[t].at[:, 1 - c], dst[t], sems[0].at[t], sems[1].at[t], (x, y, 1 - c)) for t in range(n)]

    def start(src, dst, sems):
        for cp in copies(src, dst, sems):
            cp.start()

    def finish(src, dst, sems):
        cps = copies(src, dst, sems)
        for cp in cps:
            cp.wait_recv()
        for cp in cps:
            cp.wait_send()

    sem = pltpu.SemaphoreType.DMA
    return _Comm(pgs, [_sds((a.shape[0],) + a.shape[2:], a.dtype) for a in pgs], [sem((n,)), sem((n,))], start, finish)


def _join_halves(accs, also=None):
    n = len(accs)
    c_ins, c_outs, c_sems = (also.ins, also.outs, also.sems) if also else ([], [], [])

    def body(*refs):
        o0 = n + len(c_ins)
        buf = refs[o0:o0 + n]
        send, recv = refs[o0 + n + len(c_outs):o0 + n + len(c_outs) + 2]
        extra = (refs[n:o0], refs[o0 + n:o0 + n + len(c_outs)], refs[o0 + n + len(c_outs) + 2:])
        x, y, c = _mesh_pos()
        if also:
            also.start(*extra)
        cps = [_remote(buf[t].at[:, c], buf[t].at[:, c], send.at[t], recv.at[t], (x, y, 1 - c)) for t in range(n)]
        for cp in cps:
            cp.start()
        for t in range(n):
            _remote(buf[t].at[:, c], buf[t].at[:, 1 - c], send.at[t], recv.at[t], (x, y, 1 - c)).wait_recv()
        for cp in cps:
            cp.wait_send()
        if also:
            also.finish(*extra)

    sem = pltpu.SemaphoreType.DMA
    res = pl.pallas_call(
        body, name="join_halves", in_specs=[ANY] * (n + len(c_ins)), out_specs=[ANY] * (n + len(c_outs)),
        out_shape=[_sds(a.shape, a.dtype) for a in accs] + list(c_outs),
        scratch_shapes=[sem((n,)), sem((n,))] + list(c_sems),
        input_output_aliases={t: t for t in range(n)}, compiler_params=_cp(has_side_effects=True),
    )(*accs, *c_ins)
    return (list(res[:n]), list(res[n:])) if also else res


def _sum_siblings(pgs, rbs, c_arr):
    n = len(pgs)
    nk = pgs[0].shape[0]

    def body(c_ref, *refs):
        for t in range(n):
            refs[2 * n + t][...] = (refs[t][...].astype(F32) + refs[n + t][...].astype(F32)).astype(BF16)

    half = lambda a: pl.BlockSpec((None,) + a.shape[2:], lambda k, c_ref: (k, 0, 0))
    return pl.pallas_call(
        body, name="sum_siblings",
        grid_spec=pltpu.PrefetchScalarGridSpec(
            num_scalar_prefetch=1, grid=(nk,),
            in_specs=[pl.BlockSpec((None, None) + a.shape[2:], lambda k, c_ref: (k, c_ref[0], 0, 0)) for a in pgs]
            + [half(a) for a in pgs],
            out_specs=[half(a) for a in pgs]),
        out_shape=[_sds((nk,) + a.shape[2:], BF16) for a in pgs],
        compiler_params=_cp(dimension_semantics=("arbitrary",)),
    )(c_arr, *pgs, *rbs)


def _sum_chips(ps, rbs, kcl, accs):
    n = len(ps)

    def body(k_ref, *refs):
        for t in range(n):
            b_ref = refs[n + t]
            refs[3 * n + t][...] = (refs[t][...].astype(F32) + b_ref[0].astype(F32) + b_ref[1].astype(F32)
                                    + b_ref[2].astype(F32))

    qr = lambda a: (a.shape[1] // 2, a.shape[2])
    return pl.pallas_call(
        body, name="sum_chips",
        grid_spec=pltpu.PrefetchScalarGridSpec(
            num_scalar_prefetch=1, grid=(2,),
            in_specs=[pl.BlockSpec((None,) + qr(a), lambda r, k_ref: (k_ref[0], r, 0)) for a in ps]
            + [pl.BlockSpec((3,) + qr(a), lambda r, k_ref: (0, r, 0)) for a in ps] + [ANY] * n,
            out_specs=[pl.BlockSpec((None, None) + qr(a), lambda r, k_ref: (k_ref[2], k_ref[1], r, 0)) for a in ps]),
        out_shape=[_sds(a.shape, F32) for a in accs], input_output_aliases={1 + 2 * n + t: t for t in range(n)},
        compiler_params=_cp(dimension_semantics=("arbitrary",)),
    )(kcl, *ps, *rbs, *accs)


N_DEV = 8


def _allreduce_small(parts):
    n = len(parts)

    def body(*refs):
        p_refs, o_refs, rbufs = refs[:n], refs[n:2 * n], refs[2 * n:3 * n]
        s1, r1, s2, r2 = refs[3 * n:]
        x, y, c = _mesh_pos()
        me = 4 * x + 2 * y + c
        devs = [(d // 4, (d // 2) % 2, d % 2) for d in range(N_DEV)]
        for q in range(n):
            rbufs[q][me] = p_refs[q][me]

        def each_peer(fn):
            for d in range(N_DEV):
                @pl.when(d != me)
                def _():
                    for q in range(n):
                        fn(d, q)

        def first(d, q, to_me):
            return _remote(p_refs[q].at[d], rbufs[q].at[d if to_me else me], s1.at[q, d], r1.at[q, d if to_me else me],
                           devs[d])

        def second(d, q, to_me):
            blk = d if to_me else me
            return _remote(o_refs[q].at[blk], o_refs[q].at[blk], s2.at[q, d], r2.at[q, blk], devs[d])

        each_peer(lambda d, q: first(d, q, False).start())
        each_peer(lambda d, q: first(d, q, True).wait_recv())
        for q in range(n):
            total = rbufs[q][0].astype(F32)
            for d in range(1, N_DEV):
                total = total + rbufs[q][d].astype(F32)
            o_refs[q][me] = total.astype(o_refs[q].dtype)
        each_peer(lambda d, q: second(d, q, False).start())
        each_peer(lambda d, q: second(d, q, True).wait_recv())
        each_peer(lambda d, q: first(d, q, False).wait_send())
        each_peer(lambda d, q: second(d, q, False).wait_send())

    sem = pltpu.SemaphoreType.DMA
    vm = pl.BlockSpec(memory_space=pltpu.VMEM)
    return pl.pallas_call(
        body, name="allreduce_small", in_specs=[vm] * n, out_specs=[vm] * n,
        out_shape=[_sds(a.shape, a.dtype) for a in parts],
        scratch_shapes=[pltpu.VMEM(a.shape, a.dtype) for a in parts] + [sem((n, N_DEV))] * 4,
        compiler_params=_cp(has_side_effects=True),
    )(*parts)


BIG = ("w_in", "w_1", "w_a_out", "w_b_out", "w_o", "w_2")
BY_COLS = {"w_in": True, "w_1": True, "w_a_out": False, "w_b_out": False, "w_o": False, "w_2": False}
WEIGHTS = ("g_mix", "w_in", "b_in", "conv_a_w", "conv_a_b", "ln_g", "ln_b", "w_a_out", "conv_b_w", "conv_b_b", "w_rg_a",
           "b_rg_a", "w_rg_x", "b_rg_x", "lam", "w_b_out", "w_o", "g_mlp", "w_1", "w_2", "g_final")
SMALL = tuple(n for n in WEIGHTS if n not in BIG)
ADAM_ROWS = 256
ADAM_SMALL_ROWS = 2048


def _block_diag(w):
    nh, dh, _ = w.shape
    ng = nh // HEADS_PER_GROUP
    w4 = w.reshape(ng, HEADS_PER_GROUP, dh, dh)
    eye = jnp.eye(HEADS_PER_GROUP, dtype=w.dtype)
    return jnp.einsum("qhij,hk->qhikj", w4, eye).reshape(ng, HEADS_PER_GROUP * dh, HEADS_PER_GROUP * dh)


def _block_diag_part(d, dh):
    ng = d.shape[0]
    eye = jnp.eye(HEADS_PER_GROUP, dtype=d.dtype)
    d5 = d.reshape(ng, HEADS_PER_GROUP, dh, HEADS_PER_GROUP, dh)
    return jnp.einsum("qhikj,hk->qhij", d5, eye).reshape(ng * HEADS_PER_GROUP, dh, dh)


PACK_LANES = 128


def _pack(arrays, blocks, tile_rows):
    parts = [a.reshape(-1, PACK_LANES) for a in arrays]
    parts = [jnp.pad(p, ((0, -p.shape[0] % tile_rows), (0, 0))) if p.shape[0] % tile_rows else p for p in parts]
    rows = sum(p.shape[0] for p in parts)
    pad = -rows % (blocks * tile_rows)
    if pad:
        parts.append(jnp.zeros((pad, PACK_LANES), parts[0].dtype))
    return jnp.concatenate(parts, axis=0).reshape(blocks, -1, PACK_LANES)


def _unpack(buf, like, tile_rows):
    buf = buf.reshape(-1, PACK_LANES)
    out, off = [], 0
    for a in like:
        n = a.size // PACK_LANES
        out.append(buf[off:off + n].reshape(a.shape))
        off += n + (-n % tile_rows)
    return out


def kernel(x, g_mix, w_in, b_in, conv_a_w, conv_a_b, ln_g, ln_b, w_a_out, conv_b_w, conv_b_b, w_rg_a, b_rg_a, w_rg_x, b_rg_x, lam, w_b_out, w_o, g_mlp, w_1, w_2, g_final, loss_target, m_g_mix, m_w_in, m_b_in, m_conv_a_w, m_conv_a_b, m_ln_g, m_ln_b, m_w_a_out, m_conv_b_w, m_conv_b_b, m_w_rg_a, m_b_rg_a, m_w_rg_x, m_b_rg_x, m_lam, m_w_b_out, m_w_o, m_g_mlp, m_w_1, m_w_2, m_g_final, v_g_mix, v_w_in, v_b_in, v_conv_a_w, v_conv_a_b, v_ln_g, v_ln_b, v_w_a_out, v_conv_b_w, v_conv_b_b, v_w_rg_a, v_b_rg_a, v_w_rg_x, v_b_rg_x, v_lam, v_w_b_out, v_w_o, v_g_mlp, v_w_1, v_w_2, v_g_final):
    w = dict(g_mix=g_mix, w_in=w_in, b_in=b_in, conv_a_w=conv_a_w, conv_a_b=conv_a_b, ln_g=ln_g, ln_b=ln_b, w_a_out=w_a_out,
             conv_b_w=conv_b_w, conv_b_b=conv_b_b, w_rg_a=w_rg_a, b_rg_a=b_rg_a, w_rg_x=w_rg_x, b_rg_x=b_rg_x, lam=lam,
             w_b_out=w_b_out, w_o=w_o, g_mlp=g_mlp, w_1=w_1, w_2=w_2, g_final=g_final)
    m = dict(g_mix=m_g_mix, w_in=m_w_in, b_in=m_b_in, conv_a_w=m_conv_a_w, conv_a_b=m_conv_a_b, ln_g=m_ln_g, ln_b=m_ln_b,
             w_a_out=m_w_a_out, conv_b_w=m_conv_b_w, conv_b_b=m_conv_b_b, w_rg_a=m_w_rg_a, b_rg_a=m_b_rg_a, w_rg_x=m_w_rg_x,
             b_rg_x=m_b_rg_x, lam=m_lam, w_b_out=m_w_b_out, w_o=m_w_o, g_mlp=m_g_mlp, w_1=m_w_1, w_2=m_w_2, g_final=m_g_final)
    v = dict(g_mix=v_g_mix, w_in=v_w_in, b_in=v_b_in, conv_a_w=v_conv_a_w, conv_a_b=v_conv_a_b, ln_g=v_ln_g, ln_b=v_ln_b,
             w_a_out=v_w_a_out, conv_b_w=v_conv_b_w, conv_b_b=v_conv_b_b, w_rg_a=v_w_rg_a, b_rg_a=v_b_rg_a, w_rg_x=v_w_rg_x,
             b_rg_x=v_b_rg_x, lam=v_lam, w_b_out=v_w_b_out, w_o=v_w_o, g_mlp=v_g_mlp, w_1=v_w_1, w_2=v_w_2, g_final=v_g_final)
    B, S, D = x.shape
    T = B * S
    L = w_in.shape[0]
    dh = w_rg_a.shape[-1]
    taps_a, taps_b = conv_a_w.shape[1], conv_b_w.shape[1]
    assert (taps_a, taps_b) == (TAPS_A, TAPS_B)
    xi, yi, ci = _mesh_pos()
    c_arr = jnp.reshape(ci, (1,)).astype(jnp.int32)
    k_me = 2 * xi + yi

    caw_p = jnp.pad(conv_a_w, ((0, 0), (0, 32 - taps_a), (0, 0)))
    cbw_p = jnp.pad(conv_b_w, ((0, 0), (0, 8 - taps_b), (0, 0)))
    row = lambda a: a.reshape(1, -1)

    def shards_of(l):
        d = {n: w[n][l].astype(BF16) for n in BIG}
        d.update(caw=caw_p[l], cbw=cbw_p[l])
        return d

    def params_of(l, w_in_whole):
        p = dict(w_in=w_in_whole, cab=row(conv_a_b[l]), cbb=row(conv_b_b[l]),
                 wa=_block_diag(w_rg_a[l]).astype(BF16), wx=_block_diag(w_rg_x[l]).astype(BF16))
        for n in ("g_mix", "b_in", "ln_g", "ln_b", "b_rg_a", "b_rg_x", "lam", "g_mlp"):
            p[n] = row(w[n][l])
        return p

    shards = [shards_of(l) for l in range(L)]
    w_in_whole, = _comm_call("gather_first", _gather_comm([shards[0]["w_in"]], [GATHER_KIND["w_in"]]))
    xf = x.reshape(T, D)
    saved, params = [], []
    for l in range(L):
        xf, s, p, w_in_whole = _layer_fwd(xf, params_of(l, w_in_whole), S, cur=shards[l],
                                          nxt=shards[l + 1] if l + 1 < L else None)
        saved.append(s)
        params.append(p)
    loss_part, dx, dxb, dg_final = _loss_head(xf, row(g_final), loss_target.reshape(T, D), _tiles(T)[0])
    loss = lax.psum(loss_part[0, 0], ("x", "y", "c"))

    half_shape = lambda a: (L, 2, a.shape[1] // 2, a.shape[2])
    accs = {n: lax.empty(half_shape(w[n]), F32) for n in BIG}
    small = {n: [None] * L for n in SMALL if n != "g_final"}
    red = _Reduce(accs, c_arr, lambda l: jnp.stack([k_me, ci, jnp.full((), l, ci.dtype)]).astype(jnp.int32))
    for l in reversed(range(L)):
        dx, dxb, g = _layer_bwd(dx, dxb, params[l], saved[l], S, red=red, layer=l)
        small["g_mix"][l], small["b_in"][l], small["g_mlp"][l] = g["g_mix"], g["b_in"], g["g_mlp"]
        small["conv_a_w"][l], small["conv_a_b"][l] = g["caw"], g["cab"]
        small["conv_b_w"][l], small["conv_b_b"][l] = g["cbw"], g["cbb"]
        small["ln_g"][l], small["ln_b"][l], small["lam"][l] = g["ln_g"], g["ln_b"], g["lam"]
        small["w_rg_a"][l], small["w_rg_x"][l] = _block_diag_part(g["wa"], dh), _block_diag_part(g["wx"], dh)
        small["b_rg_a"][l], small["b_rg_x"][l] = g["b_rg_a"], g["b_rg_x"]
    grad_x = dx.reshape(B, S, D)

    delta, new_m, new_v = {}, {}, {}
    flat = lambda a: a.reshape(-1, a.shape[-1])

    def adam_big(names, comm=None):
        r = _adamw("adamw_" + names[0], *[[flat(d[n]) for n in names] for d in (w, grads, m, v)], ADAM_ROWS, comm=comm)
        for q, n in enumerate(names):
            delta[n], new_m[n], new_v[n] = (r[a][q].reshape(w[n].shape) for a in range(3))
        return r[3] if comm else None

    late_sums = red.chip_sums(red.late, _comm_call("swap_halves", _swap_comm(red.late)))
    joined, got = _join_halves([red.accs[n] for n in red.EARLY], also=_scatter_comm(late_sums))
    grads = {n: a.reshape(w[n].shape) for n, a in zip(red.EARLY, joined)}
    adam_big(["w_2", "w_1"])
    adam_big(["w_o", "w_a_out"])
    red.finish(red.LATE, late_sums, got, 0)
    joined = _join_halves([red.accs[n] for n in red.LATE])
    grads.update({n: a.reshape(w[n].shape) for n, a in zip(red.LATE, joined)})
    adam_big(["w_b_out"])
    adam_big(["w_in"])

    wide = ["w_rg_a", "w_rg_x"]
    names = [n for n in SMALL if n != "g_final" and n not in wide]
    parts = [jnp.stack(small[n]) for n in names] + [dg_final]
    parts_w = [jnp.stack(small[n]).astype(BF16) for n in wide]
    total, total_w = _allreduce_small([_pack(parts, N_DEV, 8), _pack(parts_w, N_DEV, 16)])
    summed = _unpack(total, parts, 8) + [a.astype(F32) for a in _unpack(total_w, parts_w, 16)]
    for n, a in zip(names + ["g_final"] + wide, summed):
        if n == "conv_a_w":
            a = lax.dynamic_slice_in_dim(a[:, :taps_a], k_me * conv_a_w.shape[2], conv_a_w.shape[2], axis=2)
        elif n == "conv_b_w":
            a = lax.dynamic_slice_in_dim(a[:, :taps_b], k_me * conv_b_w.shape[2], conv_b_w.shape[2], axis=2)
        grads[n] = a.reshape(w[n].shape)

    for n in SMALL:
        cols = w[n].shape[-1]
        view = lambda a: a.reshape(-1, cols)
        rows = view(w[n]).shape[0]
        d_, m_, v_ = _adamw("adamw_" + n, view(w[n]), view(grads[n]), view(m[n]), view(v[n]),
                            ADAM_SMALL_ROWS if rows % ADAM_SMALL_ROWS == 0 else rows)
        delta[n], new_m[n], new_v[n] = (a.reshape(w[n].shape) for a in (d_, m_, v_))

    return (loss, grad_x, *[grads[n] for n in WEIGHTS], *[delta[n] for n in WEIGHTS],
            *[new_m[n] for n in WEIGHTS], *[new_v[n] for n in WEIGHTS])
```

```python
import jax
import jax.numpy as jnp
from jax import lax
from jax.experimental import pallas as pl
from jax.experimental.pallas import tpu as pltpu

F32 = jnp.float32
BF16 = jnp.bfloat16
MESH = pl.DeviceIdType.MESH

EPS = 1e-6
LRU_C = 8.0
ADAM_LR, ADAM_B1, ADAM_B2, ADAM_EPS, ADAM_WD, ADAM_STEP = 0.001, 0.9, 0.999, 1e-08, 0.01, 10

N_CHIPS = 4
HEADS_PER_GROUP = 4
VMEM_LIMIT = 56 * 1024 * 1024


def _cp(**kw):
    return pltpu.CompilerParams(vmem_limit_bytes=VMEM_LIMIT, **kw)


def _sig(x):
    return 1.0 / (1.0 + jnp.exp(-x))


def _gelu(x):
    t = jnp.tanh(0.7978845608028654 * (x + 0.044715 * x * x * x))
    return 0.5 * x * (1.0 + t), t


def _gelu_grad(x, t):
    dt = (1.0 - t * t) * 0.7978845608028654 * (1.0 + 3.0 * 0.044715 * x * x)
    return 0.5 * (1.0 + t) + 0.5 * x * dt


def _rms(xf, g):
    r = lax.rsqrt(jnp.mean(xf * xf, axis=-1, keepdims=True) + EPS)
    return xf * r * g, r


def _rms_bwd(xf, g, r, dh):
    dyg = dh * g
    dx = r * (dyg - xf * (r * r) * jnp.mean(dyg * xf, axis=-1, keepdims=True))
    return dx, dh * xf * r


def _ln_silu(u, g, b):
    mu = jnp.mean(u, axis=-1, keepdims=True)
    uc = u - mu
    rstd = lax.rsqrt(jnp.mean(uc * uc, axis=-1, keepdims=True) + EPS)
    uh = uc * rstd
    u2 = uh * g + b
    s = _sig(u2)
    return u2 * s, uh, rstd, u2, s


_DIMS = {"nn": (((1,), (0,)), ((), ())), "nt": (((1,), (1,)), ((), ())), "tn": (((0,), (0,)), ((), ()))}


class _Comm:
    def __init__(self, ins, outs, sems, start, finish):
        self.ins, self.outs, self.sems, self.start, self.finish = list(ins), list(outs), list(sems), start, finish


def _resident(shape):
    return pl.BlockSpec(shape, lambda i, j, k: (0,) * len(shape), pipeline_mode=pl.Buffered(1))


def _mm(name, mode, grid, a_ins, a_fn, b_in, e_ins, epi, outs, acc_shape, cache_a=None, alias=(), extra_scratch=(),
        comm=None, b_slice=None):
    ni, nj, nk = grid
    na, ne, no = len(a_ins), len(e_ins), len(outs)
    assert cache_a is None or nk == 1
    n_fixed = (nk > 1) + (cache_a is not None)
    n_in = na + 1 + ne + len(alias)
    c_ins, c_outs, c_sems = (comm.ins, comm.outs, comm.sems) if comm else ([], [], [])

    def body(*refs):
        a_refs = refs[:na]
        b_ref = refs[na]
        e_refs = refs[na + 1:na + 1 + ne]
        comm_in = refs[n_in:n_in + len(c_ins)]
        out0 = n_in + len(c_ins)
        out_refs = refs[out0:out0 + no]
        comm_out = refs[out0 + no:out0 + no + len(c_outs)]
        scratch = refs[out0 + no + len(c_outs):]
        extra = scratch[n_fixed:n_fixed + len(extra_scratch)]
        comm_sems = scratch[n_fixed + len(extra_scratch):]
        i, j, k = pl.program_id(0), pl.program_id(1), pl.program_id(2)
        if comm:
            @pl.when((i == 0) & (j == 0) & (k == 0))
            def _():
                comm.start(comm_in, comm_out, comm_sems)
        if cache_a is not None:
            cache_ref = scratch[n_fixed - 1]

            @pl.when(j == 0)
            def _():
                cache_ref[...] = a_fn(a_refs, out_refs, i, j, k)

            a = cache_ref[...]
        else:
            a = a_fn(a_refs, out_refs, i, j, k)
        if b_slice is None:
            b = b_ref[...]
        elif b_slice[0] == "cols":
            b = b_ref[:, pl.ds(pl.multiple_of(j * b_slice[1], b_slice[1]), b_slice[1])]
        else:
            b = b_ref[pl.ds(pl.multiple_of(j * b_slice[1], b_slice[1]), b_slice[1]), :]
        prod = lax.dot_general(a, b, _DIMS[mode], preferred_element_type=F32)
        if nk == 1:
            epi(prod, e_refs, out_refs, i, j, extra)
        else:
            acc_ref = scratch[0]

            @pl.when(k == 0)
            def _():
                acc_ref[...] = prod

            @pl.when(k > 0)
            def _():
                acc_ref[...] += prod

            @pl.when(k == nk - 1)
            def _():
                epi(acc_ref[...], e_refs, out_refs, i, j, extra)

        if comm:
            @pl.when((i == ni - 1) & (j == nj - 1) & (k == nk - 1))
            def _():
                comm.finish(comm_in, comm_out, comm_sems)

    scratch_shapes = []
    if nk > 1:
        scratch_shapes.append(pltpu.VMEM(acc_shape, F32))
    if cache_a is not None:
        scratch_shapes.append(pltpu.VMEM(cache_a, BF16))
    any_spec = pl.BlockSpec(memory_space=pl.ANY)
    ins = (list(a_ins) + [b_in] + list(e_ins) + [(arr, any_spec) for arr, _ in alias] + [(arr, any_spec) for arr in c_ins])
    first_alias = na + 1 + ne
    res = pl.pallas_call(
        body, name=name, grid=grid,
        in_specs=[s for _, s in ins], out_specs=[s for _, s in outs] + [any_spec] * len(c_outs),
        out_shape=[o for o, _ in outs] + list(c_outs),
        scratch_shapes=scratch_shapes + list(extra_scratch) + list(c_sems),
        input_output_aliases={first_alias + n: o for n, (_, o) in enumerate(alias)},
        compiler_params=_cp(dimension_semantics=("arbitrary", "arbitrary", "arbitrary"), has_side_effects=bool(comm)),
    )(*[a for a, _ in ins])
    if comm:
        return list(res[:no]), list(res[no:])
    return res


def _bs(shape, fn):
    return pl.BlockSpec(shape, fn)


def _sds(shape, dt):
    return jax.ShapeDtypeStruct(shape, dt)


def _acc_rows(ref, val, first):
    @pl.when(first)
    def _():
        ref[...] = val

    @pl.when(jnp.logical_not(first))
    def _():
        ref[...] += val


def _fwd_norm_mm(name, x, g, w, bias, tm, tn, comm=None):
    T, D = x.shape
    N = w.shape[1]

    def a_fn(a_refs, out_refs, i, j, k):
        h, _ = _rms(a_refs[0][...], a_refs[1][...])
        hb = h.astype(BF16)
        out_refs[1][...] = hb
        return hb

    def epi(acc, e_refs, out_refs, i, j, extra):
        if bias is not None:
            acc = acc + e_refs[0][...]
        out_refs[0][...] = acc.astype(BF16)

    e_ins = [] if bias is None else [(bias, _bs((1, tn), lambda i, j, k: (0, j)))]
    return _mm(name, "nn", (T // tm, N // tn, 1),
               [(x, _bs((tm, D), lambda i, j, k: (i, 0))), (g, _bs((1, D), lambda i, j, k: (0, 0)))], a_fn,
               (w, _resident((D, N))), e_ins, epi,
               [(_sds((T, N), BF16), _bs((tm, tn), lambda i, j, k: (i, j))),
                (_sds((T, D), BF16), _bs((tm, D), lambda i, j, k: (i, 0)))],
               None, cache_a=(tm, D), comm=comm, b_slice=("cols", tn))


def _fwd_ya(u1, ln_g, ln_b, w, tm):
    T, C = u1.shape
    N = w.shape[1]

    def a_fn(a_refs, out_refs, i, j, k):
        u3 = _ln_silu(a_refs[0][...].astype(F32), a_refs[1][...], a_refs[2][...])[0].astype(BF16)
        out_refs[1][...] = u3
        return u3

    def epi(acc, e_refs, out_refs, i, j, extra):
        out_refs[0][...] = acc.astype(BF16)

    row = _bs((1, C), lambda i, j, k: (0, 0))
    tc = _bs((tm, C), lambda i, j, k: (i, 0))
    return _mm("fwd_ya", "nn", (T // tm, 1, 1), [(u1, tc), (ln_g, row), (ln_b, row)], a_fn,
               (w, _bs((C, N), lambda i, j, k: (0, 0))), [], epi,
               [(_sds((T, N), BF16), _bs((tm, N), lambda i, j, k: (i, 0))), (_sds((T, C), BF16), tc)], None)


def _fwd_yb(h, z, gb_blk, w, tm, tk):
    T, C = h.shape
    N = w.shape[1]

    def a_fn(a_refs, out_refs, i, j, k):
        ge, _ = _gelu(a_refs[1][...].astype(F32))
        pv = (a_refs[0][...].astype(F32) * ge).astype(BF16)
        out_refs[1][...] = pv
        return pv

    def epi(acc, e_refs, out_refs, i, j, extra):
        out_refs[0][...] = acc.astype(BF16)

    tk_ = _bs((tm, tk), lambda i, j, k: (i, k))
    return _mm("fwd_yb", "nn", (T // tm, 1, C // tk),
               [(h, tk_), (z, _bs((tm, tk), lambda i, j, k: (i, gb_blk + k)))], a_fn,
               (w, _bs((tk, N), lambda i, j, k: (k, 0))), [], epi,
               [(_sds((T, N), BF16), _bs((tm, N), lambda i, j, k: (i, 0))), (_sds((T, C), BF16), tk_)], (tm, N))


def _fwd_x1(x, ya, yb, z, sa_blk, w, tm):
    T, D = x.shape

    def a_fn(a_refs, out_refs, i, j, k):
        ya_, yb_, sa_, sb_ = (r[...].astype(F32) for r in a_refs)
        mg = (_sig(sa_) * ya_ + _sig(sb_) * yb_).astype(BF16)
        out_refs[1][...] = mg
        return mg

    def epi(acc, e_refs, out_refs, i, j, extra):
        out_refs[0][...] = e_refs[0][...] + acc

    t = _bs((tm, D), lambda i, j, k: (i, 0))
    return _mm("fwd_x1", "nn", (T // tm, 1, 1),
               [(ya, t), (yb, t), (z, _bs((tm, D), lambda i, j, k: (i, sa_blk))),
                (z, _bs((tm, D), lambda i, j, k: (i, sa_blk + 1)))], a_fn,
               (w, _bs((D, D), lambda i, j, k: (0, 0))), [(x, t)], epi,
               [(_sds((T, D), F32), t), (_sds((T, D), BF16), t)], None)


def _fwd_x2(x1, fp, w, tm, tk, comm=None):
    T, D = x1.shape
    Fd = fp.shape[1]

    def epi(acc, e_refs, out_refs, i, j, extra):
        out_refs[0][...] = e_refs[0][...] + acc

    t = _bs((tm, D), lambda i, j, k: (i, 0))
    whole_k = tk == Fd
    r = _mm("fwd_x2", "nn", (T // tm, 1, Fd // tk),
            [(fp, _bs((tm, tk), lambda i, j, k: (i, k)))], _relu2,
            (w, _resident((Fd, D)) if whole_k else _bs((tk, D), lambda i, j, k: (k, 0))), [(x1, t)], epi,
            [(_sds((T, D), F32), t)], (tm, D), comm=comm)
    return (r[0][0], r[1]) if comm else r[0]


def _relu2(a_refs, out_refs, i, j, k):
    f = jnp.maximum(a_refs[0][...], 0.0)
    return f * f


def _loss_head(x, g, target, tm):
    T, D = x.shape

    def body(x_ref, g_ref, t_ref, loss_ref, dx_ref, dxb_ref, dg_ref):
        i = pl.program_id(0)
        xf, gv = x_ref[...], g_ref[...]
        y, r = _rms(xf, gv)
        err = y - t_ref[...]
        part = 0.5 * jnp.sum(jnp.mean(err * err, axis=-1, keepdims=True), axis=0, keepdims=True)
        dx, dg_rows = _rms_bwd(xf, gv, r, err * (1.0 / D))
        dx_ref[...] = dx
        dxb_ref[...] = dx.astype(BF16)
        _acc_rows(loss_ref, jnp.broadcast_to(part, (1, 128)), i == 0)
        _acc_rows(dg_ref, jnp.sum(dg_rows, axis=0, keepdims=True), i == 0)

    t = _bs((tm, D), lambda i: (i, 0))
    row = _bs((1, D), lambda i: (0, 0))
    return pl.pallas_call(
        body, name="loss_head", grid=(T // tm,), in_specs=[t, row, t],
        out_specs=[_bs((1, 128), lambda i: (0, 0)), t, t, row],
        out_shape=[_sds((1, 128), F32), _sds((T, D), F32), _sds((T, D), BF16), _sds((1, D), F32)],
        compiler_params=_cp(dimension_semantics=("arbitrary",)),
    )(x, g, target)


def _adamw(name, w, g, m, v, tr, comm=None):
    many = isinstance(w, (list, tuple))
    ws, gs, ms, vs = (list(a) if many else [a] for a in (w, g, m, v))
    n = len(ws)
    rows, cols = ws[0].shape
    d1 = 1.0 - ADAM_B1 ** ADAM_STEP
    d2 = 1.0 - ADAM_B2 ** ADAM_STEP

    def body(*refs):
        for q in range(n):
            w_ref, g_ref, m_ref, v_ref = (refs[a * n + q] for a in range(4))
            d_ref, mo_ref, vo_ref = (refs[(4 + a) * n + q] for a in range(3))
            gv = g_ref[...]
            mn = ADAM_B1 * m_ref[...] + (1.0 - ADAM_B1) * gv
            vn = ADAM_B2 * v_ref[...] + (1.0 - ADAM_B2) * (gv * gv)
            d_ref[...] = -ADAM_LR * ((mn / d1) / (jnp.sqrt(vn / d2) + ADAM_EPS) + ADAM_WD * w_ref[...])
            mo_ref[...] = mn
            vo_ref[...] = vn

    t = _bs((tr, cols), lambda i: (i, 0))
    r = _call_with_comm(name, body, (rows // tr,), ws + gs + ms + vs, [t] * (4 * n), [t] * (3 * n),
                        [_sds((rows, cols), F32)] * (3 * n), [], comm)
    outs, got = (r if comm else (r, None))
    res = [outs[a * n:(a + 1) * n] if many else outs[a * n] for a in range(3)]
    return (*res, got) if comm else tuple(res)


def _ident(a_refs, out_refs, i, j, k):
    return a_refs[0][...]


def _bwd_dw(name, act, dy, ti, tj, tm, a_fn=None, a_extra=(), shard_cols=None, keep=None):
    T, J = dy.shape
    I = act.shape[1]

    def epi(acc, e_refs, out_refs, i, j, extra):
        out_refs[0][...] = acc.astype(BF16).reshape(out_refs[0].shape)

    if shard_cols is None:
        out = (_sds((I, J), BF16), _bs((ti, tj), lambda i, j, k: (i, j)))
    else:
        per = shard_cols // tj
        assert ti == I and per * tj == shard_cols
        out = (_sds((J // shard_cols, 2, I // 2, shard_cols), BF16),
               _bs((None, 2, I // 2, tj), lambda i, j, k: (lax.div(j, per), 0, 0, lax.rem(j, per))))
    assert keep is None or tm == T
    a_spec = _resident((T, I)) if keep == "act" else _bs((tm, ti), lambda i, j, k: (k, i))
    b_spec = _resident((T, J)) if keep == "dy" else _bs((tm, tj), lambda i, j, k: (k, j))
    return _mm(name, "tn", (I // ti, J // tj, T // tm), [(act, a_spec)] + list(a_extra), a_fn or _ident,
               (dy, b_spec), [], epi, [out], (ti, tj))[0]


def _bwd_df(dxb, w2, fp, tm, tn, comm=None):
    T, D = dxb.shape
    Fd = w2.shape[0]

    def epi(acc, e_refs, out_refs, i, j, extra):
        out_refs[0][...] = (acc * (2.0 * jnp.maximum(e_refs[0][...].astype(F32), 0.0))).astype(BF16)

    t = _bs((tm, tn), lambda i, j, k: (i, j))
    r = _mm("bwd_df", "nt", (T // tm, Fd // tn, 1), [(dxb, _bs((tm, D), lambda i, j, k: (i, 0)))], _ident,
            (w2, _resident((Fd, D))), [(fp, t)], epi, [(_sds((T, Fd), BF16), t)], None, b_slice=("rows", tn), comm=comm)
    return (r[0][0], r[1]) if comm else r[0]


def _bwd_norm(name, dy, w, x, g, dres, tm, tk, colsum=False, comm=None):
    T, K = dy.shape
    D = w.shape[0]
    nk = K // tk

    def a_fn(a_refs, out_refs, i, j, k):
        a = a_refs[0][...]
        if colsum:
            s = jnp.sum(a.astype(F32), axis=0, keepdims=True)

            @pl.when(i == 0)
            def _():
                out_refs[3][k] = s

            @pl.when(i > 0)
            def _():
                out_refs[3][k] += s
        return a

    def epi(acc, e_refs, out_refs, i, j, extra):
        xf, gv = e_refs[0][...], e_refs[1][...]
        r = lax.rsqrt(jnp.mean(xf * xf, axis=-1, keepdims=True) + EPS)
        dx, dg_rows = _rms_bwd(xf, gv, r, acc)
        dx = dx + e_refs[2][...]
        out_refs[0][...] = dx
        out_refs[1][...] = dx.astype(BF16)
        _acc_rows(out_refs[2], jnp.sum(dg_rows, axis=0, keepdims=True), i == 0)

    t = _bs((tm, D), lambda i, j, k: (i, 0))
    row = _bs((1, D), lambda i, j, k: (0, 0))
    outs = [(_sds((T, D), F32), t), (_sds((T, D), BF16), t), (_sds((1, D), F32), row)]
    if colsum:
        outs.append((_sds((nk, 1, tk), F32), _bs((nk, 1, tk), lambda i, j, k: (0, 0, 0))))
    return _mm(name, "nt", (T // tm, 1, nk), [(dy, _bs((tm, tk), lambda i, j, k: (i, k)))], a_fn,
               (w, _resident((D, K)) if nk == 1 else _bs((D, tk), lambda i, j, k: (0, k))),
               [(x, t), (g, row), (dres, t)], epi, outs, (tm, D), comm=comm)


def _bwd_dm(dxb, w_o, ya, yb, z, sa_blk, tm):
    T, D = dxb.shape

    def epi(acc, e_refs, out_refs, i, j, extra):
        ya_, yb_, sa_, sb_ = (r[...].astype(F32) for r in e_refs)
        ga, gb = _sig(sa_), _sig(sb_)
        out_refs[0][...] = (acc * ga).astype(BF16)
        out_refs[1][...] = (acc * gb).astype(BF16)
        stage, sem = extra
        put = pltpu.make_async_copy(
            stage, out_refs[2].at[pl.ds(pl.multiple_of(i * tm, tm), tm), pl.ds(sa_blk * D, 2 * D)], sem)

        @pl.when(i > 0)
        def _():
            put.wait()

        stage[:, 0:D] = (acc * ya_ * ga * (1.0 - ga)).astype(BF16)
        stage[:, D:2 * D] = (acc * yb_ * gb * (1.0 - gb)).astype(BF16)
        put.start()

        @pl.when(i == T // tm - 1)
        def _():
            put.wait()

    t = _bs((tm, D), lambda i, j, k: (i, 0))
    return _mm("bwd_dm", "nt", (T // tm, 1, 1), [(dxb, t)], _ident, (w_o, _bs((D, D), lambda i, j, k: (0, 0))),
               [(ya, t), (yb, t), (z, _bs((tm, D), lambda i, j, k: (i, sa_blk))),
                (z, _bs((tm, D), lambda i, j, k: (i, sa_blk + 1)))], epi,
               [(_sds((T, D), BF16), t), (_sds((T, D), BF16), t),
                (_sds(z.shape, BF16), pl.BlockSpec(memory_space=pl.ANY))], None,
               extra_scratch=[pltpu.VMEM((tm, 2 * D), BF16), pltpu.SemaphoreType.DMA(())])


def _bwd_du3(dya, w, u1, ln_g, ln_b, tm):
    T, D = dya.shape
    C = w.shape[0]

    def epi(acc, e_refs, out_refs, i, j, extra):
        gv = e_refs[1][...]
        _, uh, rstd, u2, s = _ln_silu(e_refs[0][...].astype(F32), gv, e_refs[2][...])
        du2 = acc * (s * (1.0 + u2 * (1.0 - s)))
        duh = du2 * gv
        out_refs[0][...] = rstd * (duh - jnp.mean(duh, axis=-1, keepdims=True)
                                   - uh * jnp.mean(duh * uh, axis=-1, keepdims=True))
        _acc_rows(out_refs[1], jnp.sum(du2 * uh, axis=0, keepdims=True), i == 0)
        _acc_rows(out_refs[2], jnp.sum(du2, axis=0, keepdims=True), i == 0)

    t = _bs((tm, C), lambda i, j, k: (i, 0))
    row = _bs((1, C), lambda i, j, k: (0, 0))
    return _mm("bwd_du3", "nt", (T // tm, 1, 1), [(dya, _bs((tm, D), lambda i, j, k: (i, 0)))], _ident,
               (w, _bs((C, D), lambda i, j, k: (0, 0))), [(u1, t), (ln_g, row), (ln_b, row)], epi,
               [(_sds((T, C), F32), t), (_sds((1, C), F32), row), (_sds((1, C), F32), row)], None)


def _bwd_dp(dyb, w, h, z, dz, gb_blk, tm, tn, comm=None):
    T, D = dyb.shape
    R = w.shape[0]

    def epi(acc, e_refs, out_refs, i, j, extra):
        gbv = e_refs[1][...].astype(F32)
        ge, th = _gelu(gbv)
        out_refs[0][...] = acc * ge
        out_refs[1][...] = (acc * e_refs[0][...].astype(F32) * _gelu_grad(gbv, th)).astype(BF16)

    t = _bs((tm, tn), lambda i, j, k: (i, j))
    tz = _bs((tm, tn), lambda i, j, k: (i, gb_blk + j))
    return _mm("bwd_dp", "nt", (T // tm, R // tn, 1), [(dyb, _bs((tm, D), lambda i, j, k: (i, 0)))], _ident,
               (w, _bs((tn, D), lambda i, j, k: (j, 0))), [(h, t), (z, tz)], epi,
               [(_sds((T, R), F32), t), (_sds(dz.shape, BF16), tz)], None, cache_a=None, alias=[(dz, 1)], comm=comm)


CONV_ROWS = 32


def _shifted_taps(x, halo, shifts, fn):
    n = CONV_ROWS + halo
    by_r = {}
    for k, s in shifts:
        by_r.setdefault(s % 8, []).append((k, s))
    for r in sorted(by_r):
        xr = x if r == 0 else pltpu.roll(x, n - r, 0)
        for k, s in by_r[r]:
            q = s - r
            fn(k, xr[q:q + CONV_ROWS])


def _conv_fwd(name, z, blk0, gate_blk0, w_pad, bias, taps, seq, tc, out_dtype, comm=None):
    T = z.shape[0]
    C = w_pad.shape[1]
    nb, nj = T // seq, C // tc
    pad = 8 * ((taps - 1 + 7) // 8)
    halo = pad
    shifts = [(k, pad - (taps - 1) + k) for k in range(taps)]
    glu = gate_blk0 is not None

    def body(*refs):
        if glu:
            v_ref, g_ref, w_ref, b_ref, o_ref, p_ref = refs
        else:
            v_ref, w_ref, b_ref, o_ref, p_ref = refs
        p_ref[pl.ds(0, pad), :] = jnp.zeros((pad, tc), F32)
        u = v_ref[...].astype(F32)
        if glu:
            u = u * _sig(g_ref[...].astype(F32))
        p_ref[pl.ds(pad, seq), :] = u

        def step(c, _):
            base = pl.multiple_of(c * CONV_ROWS, CONV_ROWS)
            x = p_ref[pl.ds(base, CONV_ROWS + halo), :]
            acc = [jnp.zeros((CONV_ROWS, tc), F32) + b_ref[...]]

            def tap(k, xs):
                acc[0] = acc[0] + w_ref[k:k + 1, :] * xs

            _shifted_taps(x, halo, shifts, tap)
            o_ref[pl.ds(base, CONV_ROWS), :] = acc[0].astype(out_dtype)
            return 0

        lax.fori_loop(0, seq // CONV_ROWS, step, 0)

    zin = [(z, _bs((seq, tc), lambda b, j: (b, blk0 + j)))]
    if glu:
        zin.append((z, _bs((seq, tc), lambda b, j: (b, gate_blk0 + j))))
    ins = zin + [(w_pad, _bs((w_pad.shape[0], tc), lambda b, j: (0, j))), (bias, _bs((1, tc), lambda b, j: (0, j)))]
    r = _call_with_comm(name, body, (nb, nj), [a for a, _ in ins], [s for _, s in ins],
                        [_bs((seq, tc), lambda b, j: (b, j))], [_sds((T, C), out_dtype)],
                        [pltpu.VMEM((seq + pad, tc), F32)], comm)
    return (r[0][0], r[1]) if comm else r[0]


def _conv_bwd(name, dy, z, dz, blk0, gate_blk0, w_pad, taps, seq, tc, comm=None):
    T = z.shape[0]
    C = w_pad.shape[1]
    nb, nj = T // seq, C // tc
    kp = w_pad.shape[0]
    pad = 8 * ((taps - 1 + 7) // 8)
    halo = pad
    sh_du = [(k, taps - 1 - k) for k in range(taps)]
    sh_dw = [(k, pad - (taps - 1) + k) for k in range(taps)]
    glu = gate_blk0 is not None

    def body(*refs):
        if glu:
            dy_ref, v_ref, g_ref, w_ref, _dz_in, dz_out, dw_ref, db_ref, pdy, pu, du_s, wacc, ob, ob2, osem = refs
        else:
            dy_ref, v_ref, w_ref, _dz_in, dz_out, dw_ref, db_ref, pdy, pu, du_s, wacc, ob, osem = refs
        j = pl.program_id(0)
        b = pl.program_id(1)
        pdy[pl.ds(seq, pad), :] = jnp.zeros((pad, tc), F32)
        pdy[pl.ds(0, seq), :] = dy_ref[...].astype(F32)
        pu[pl.ds(0, pad), :] = jnp.zeros((pad, tc), F32)
        v = v_ref[...].astype(F32)
        if glu:
            sg = _sig(g_ref[...].astype(F32))
            pu[pl.ds(pad, seq), :] = v * sg
        else:
            pu[pl.ds(pad, seq), :] = v
        wacc[...] = jnp.zeros(wacc.shape, F32)

        def step(c, dbacc):
            base = pl.multiple_of(c * CONV_ROWS, CONV_ROWS)
            xdy = pdy[pl.ds(base, CONV_ROWS + halo), :]
            acc = [jnp.zeros((CONV_ROWS, tc), F32)]

            def tap(k, xs):
                acc[0] = acc[0] + w_ref[k:k + 1, :] * xs

            _shifted_taps(xdy, halo, sh_du, tap)
            du_s[pl.ds(base, CONV_ROWS), :] = acc[0]
            dyc = xdy[0:CONV_ROWS]
            xu = pu[pl.ds(base, CONV_ROWS + halo), :]

            def wtap(k, xs):
                p = dyc * xs
                s8 = p[0:8]
                for m in range(1, CONV_ROWS // 8):
                    s8 = s8 + p[8 * m:8 * m + 8]
                wacc[pl.ds(8 * k, 8), :] += s8

            _shifted_taps(xu, halo, sh_dw, wtap)
            d8 = dyc[0:8]
            for m in range(1, CONV_ROWS // 8):
                d8 = d8 + dyc[8 * m:8 * m + 8]
            return dbacc + d8

        dbacc = lax.fori_loop(0, seq // CONV_ROWS, step, jnp.zeros((8, tc), F32))
        du = du_s[...]
        rows = pl.ds(pl.multiple_of(b * seq, seq), seq)
        puts = [pltpu.make_async_copy(ob, dz_out.at[rows, pl.ds(pl.multiple_of((blk0 + j) * tc, tc), tc)], osem.at[0])]
        if glu:
            puts.append(pltpu.make_async_copy(
                ob2, dz_out.at[rows, pl.ds(pl.multiple_of((gate_blk0 + j) * tc, tc), tc)], osem.at[1]))

        @pl.when((j > 0) | (b > 0))
        def _():
            for cp in puts:
                cp.wait()

        if glu:
            ob[...] = (du * sg).astype(BF16)
            ob2[...] = (du * v * sg * (1.0 - sg)).astype(BF16)
        else:
            ob[...] = du.astype(BF16)
        for cp in puts:
            cp.start()

        @pl.when((j == nj - 1) & (b == nb - 1))
        def _():
            for cp in puts:
                cp.wait()
        dw = jnp.sum(wacc[...].reshape(kp, 8, tc), axis=1)
        _acc_rows(dw_ref, dw, b == 0)
        _acc_rows(db_ref, jnp.sum(dbacc, axis=0, keepdims=True), b == 0)

    zin = [(z, _bs((seq, tc), lambda j, b: (b, blk0 + j)))]
    if glu:
        zin.append((z, _bs((seq, tc), lambda j, b: (b, gate_blk0 + j))))
    ins = [(dy, _bs((seq, tc), lambda j, b: (b, j)))] + zin + [(w_pad, _bs((kp, tc), lambda j, b: (0, j))),
                                                               (dz, pl.BlockSpec(memory_space=pl.ANY))]
    dz_idx = len(ins) - 1
    out_specs = [pl.BlockSpec(memory_space=pl.ANY), _bs((kp, tc), lambda j, b: (0, j)), _bs((1, tc), lambda j, b: (0, j))]
    out_shape = [_sds(dz.shape, dz.dtype), _sds((kp, C), F32), _sds((1, C), F32)]
    stage = [pltpu.VMEM((seq, tc), BF16)] * (2 if glu else 1) + [pltpu.SemaphoreType.DMA((2,))]
    return _call_with_comm(
        name, body, (nj, nb), [a for a, _ in ins], [s for _, s in ins], out_specs, out_shape,
        [pltpu.VMEM((seq + pad, tc), F32), pltpu.VMEM((seq + pad, tc), F32),
         pltpu.VMEM((seq, tc), F32), pltpu.VMEM((8 * kp, tc), F32)] + stage, comm, aliases={dz_idx: 0})


RG_ROWS = 256


def _softplus_neg(lam):
    return jnp.maximum(-lam, 0.0) + jnp.log(1.0 + jnp.exp(-jnp.abs(lam)))


def _gates(v0c, wa_ref, wx_ref, ba, bx, sp):
    vb = v0c.astype(BF16)
    r = _sig(jnp.dot(vb, wa_ref[...], preferred_element_type=F32) + ba)
    i = _sig(jnp.dot(vb, wx_ref[...], preferred_element_type=F32) + bx)
    return r, i, -LRU_C * r * sp


def _decay(la, first_row):
    a = jnp.exp(la)
    a2 = a * a
    x = 2.0 * la
    series = -x * (1.0 + x * (0.5 + x * (1.0 / 6)))
    mult = jnp.sqrt(jnp.where(x > -0.01, series, 1.0 - a2))
    dmult = jnp.where(first_row, 0.0, -a2 / mult)
    mult = jnp.where(first_row, 1.0, mult)
    return a, mult, dmult


def _group_scan(a, b, reverse):
    n = a.shape[0]
    row = lax.broadcasted_iota(jnp.int32, a.shape, 0) & 7
    for d in (1, 2, 4):
        sh = n - d if reverse else d
        a_s, b_s = pltpu.roll(a, sh, 0), pltpu.roll(b, sh, 0)
        m = (row < 8 - d) if reverse else (row >= d)
        b = jnp.where(m, a * b_s + b, b)
        a = jnp.where(m, a * a_s, a)
    return a, b


def _group_carry(a_s, b_s, o_s, n_groups, reverse):
    cols = a_s.shape[1]

    def step(g, carry):
        g = n_groups - 1 - g if reverse else g
        rows = pl.ds(pl.multiple_of(g * 8, 8), 8)
        o = a_s[rows, :] * carry + b_s[rows, :]
        o_s[rows, :] = o
        return o[0:1, :] if reverse else o[7:8, :]

    lax.fori_loop(0, n_groups, step, jnp.zeros((1, cols), F32), unroll=2)


def _rglru_fwd(v0, wa, wx, ba, bx, lam, seq, comm=None):
    T, C = v0.shape
    ng, G = wa.shape[0], wa.shape[1]
    nb = T // seq

    def body(v_ref, wa_ref, wx_ref, ba_ref, bx_ref, lam_ref, h_ref, r_ref, i_ref, la_ref, a_s, b_s, h_s):
        sp = _softplus_neg(lam_ref[...])

        def chunk(c, _):
            rows = pl.ds(pl.multiple_of(c * RG_ROWS, RG_ROWS), RG_ROWS)
            t = lax.broadcasted_iota(jnp.int32, (RG_ROWS, G), 0) + c * RG_ROWS
            v0c = v_ref[rows, :]
            r, i, la = _gates(v0c, wa_ref, wx_ref, ba_ref[...], bx_ref[...], sp)
            r_ref[rows, :] = r.astype(BF16)
            i_ref[rows, :] = i.astype(BF16)
            la_ref[rows, :] = la
            a, mult, _ = _decay(la, t == 0)
            a_g, b_g = _group_scan(a, mult * i * v0c, False)
            a_s[rows, :] = a_g
            b_s[rows, :] = b_g
            return 0

        lax.fori_loop(0, seq // RG_ROWS, chunk, 0)
        _group_carry(a_s, b_s, h_s, seq // 8, False)
        h_ref[...] = h_s[...].astype(BF16)

    t2 = _bs((seq, G), lambda b, g: (b, g))
    wsp = _bs((None, G, G), lambda b, g: (g, 0, 0))
    row = _bs((1, G), lambda b, g: (0, g))
    return _call_with_comm("rglru_fwd", body, (nb, ng), [v0, wa, wx, ba, bx, lam], [t2, wsp, wsp, row, row, row],
                           [t2] * 4, [_sds((T, C), BF16)] * 3 + [_sds((T, C), F32)], [pltpu.VMEM((seq, G), F32)] * 3, comm)


def _call_with_comm(name, body, grid, ins, in_specs, out_specs, out_shape, scratch, comm, aliases=None):
    n_in, n_out, n_s = len(ins), len(out_shape), len(scratch)
    c_ins, c_outs, c_sems = (comm.ins, comm.outs, comm.sems) if comm else ([], [], [])

    def wrapped(*refs):
        o0 = n_in + len(c_ins)
        s0 = o0 + n_out + len(c_outs)
        cin, cout, csem = refs[n_in:o0], refs[o0 + n_out:s0], refs[s0 + n_s:]
        ids = [pl.program_id(a) for a in range(len(grid))]
        if comm:
            first = _all_of([i == 0 for i in ids])

            @pl.when(first)
            def _():
                comm.start(cin, cout, csem)

        body(*refs[:n_in], *refs[o0:o0 + n_out], *refs[s0:s0 + n_s])
        if comm:
            last = _all_of([i == n - 1 for i, n in zip(ids, grid)])

            @pl.when(last)
            def _():
                comm.finish(cin, cout, csem)

    res = pl.pallas_call(
        wrapped, name=name, grid=grid, in_specs=list(in_specs) + [ANY] * len(c_ins),
        out_specs=list(out_specs) + [ANY] * len(c_outs), out_shape=list(out_shape) + list(c_outs),
        scratch_shapes=list(scratch) + list(c_sems), input_output_aliases=aliases or {},
        compiler_params=_cp(dimension_semantics=("arbitrary",) * len(grid), has_side_effects=bool(comm)),
    )(*ins, *c_ins)
    return (list(res[:n_out]), list(res[n_out:])) if comm else list(res)


def _all_of(conds):
    out = conds[0]
    for c in conds[1:]:
        out = out & c
    return out


def _rglru_bwd(v0, h, dh, r_g, i_g, la_g, wa, wx, lam, seq, comm=None):
    T, C = v0.shape
    ng, G = wa.shape[0], wa.shape[1]
    nb = T // seq
    R = RG_ROWS

    def body(v_ref, h_ref, dh_ref, r_ref, i_ref, la_ref, wa_ref, wx_ref, lam_ref,
             dv_ref, dwa_ref, dwx_ref, dba_ref, dbx_ref, dlam_ref, a_s, b_s, q_s, hp_s):
        b = pl.program_id(1)
        lam_v = lam_ref[...]
        sp = _softplus_neg(lam_v)
        dsp_dlam = -_sig(-lam_v)

        @pl.when(b == 0)
        def _():
            dwa_ref[...] = jnp.zeros((G, G), F32)
            dwx_ref[...] = jnp.zeros((G, G), F32)
            dba_ref[...] = jnp.zeros((1, G), F32)
            dbx_ref[...] = jnp.zeros((1, G), F32)
            dlam_ref[...] = jnp.zeros((1, G), F32)

        hp_s[pl.ds(0, 8), :] = jnp.zeros((8, G), F32)
        hp_s[pl.ds(8, seq), :] = h_ref[...].astype(F32)
        q_s[pl.ds(seq, 8), :] = jnp.zeros((8, G), F32)

        def chunk1(c, _):
            rows = pl.ds(pl.multiple_of(c * R, R), R)
            a = jnp.exp(la_ref[rows, :])
            a_g, b_g = _group_scan(a, a * dh_ref[rows, :].astype(F32), True)
            a_s[rows, :] = a_g
            b_s[rows, :] = b_g
            return 0

        lax.fori_loop(0, seq // R, chunk1, 0)
        _group_carry(a_s, b_s, q_s, seq // 8, True)

        def chunk3(c, _):
            base = pl.multiple_of(c * R, R)
            rows = pl.ds(base, R)
            t = lax.broadcasted_iota(jnp.int32, (R, G), 0) + c * R
            v0c = v_ref[rows, :]
            r, i = r_ref[rows, :].astype(F32), i_ref[rows, :].astype(F32)
            a, mult, dmult_dla = _decay(la_ref[rows, :], t == 0)
            q_next = pltpu.roll(q_s[pl.ds(base, R + 8), :], R + 7, 0)[0:R]
            h_prev = pltpu.roll(hp_s[pl.ds(base, R + 8), :], R + 1, 0)[0:R]
            gt = dh_ref[rows, :].astype(F32) + q_next
            dla = gt * h_prev * a + gt * i * v0c * dmult_dla
            dpa = dla * (-LRU_C * sp) * r * (1.0 - r)
            dpx = gt * mult * v0c * i * (1.0 - i)
            dpa_b, dpx_b, v_b = dpa.astype(BF16), dpx.astype(BF16), v0c.astype(BF16)
            dv_ref[rows, :] = (gt * mult * i
                               + lax.dot_general(dpa_b, wa_ref[...], _DIMS["nt"], preferred_element_type=F32)
                               + lax.dot_general(dpx_b, wx_ref[...], _DIMS["nt"], preferred_element_type=F32))
            dwa_ref[...] += lax.dot_general(v_b, dpa_b, _DIMS["tn"], preferred_element_type=F32)
            dwx_ref[...] += lax.dot_general(v_b, dpx_b, _DIMS["tn"], preferred_element_type=F32)
            dba_ref[...] += jnp.sum(dpa, axis=0, keepdims=True)
            dbx_ref[...] += jnp.sum(dpx, axis=0, keepdims=True)
            dlam_ref[...] += jnp.sum(dla * (-LRU_C * r), axis=0, keepdims=True) * dsp_dlam
            return 0

        lax.fori_loop(0, seq // R, chunk3, 0)

    t2 = _bs((seq, G), lambda g, b: (b, g))
    wsp = _bs((None, G, G), lambda g, b: (g, 0, 0))
    row = _bs((1, G), lambda g, b: (0, g))
    return _call_with_comm(
        "rglru_bwd", body, (ng, nb), [v0, h, dh, r_g, i_g, la_g, wa, wx, lam], [t2] * 6 + [wsp, wsp, row],
        [t2, wsp, wsp, row, row, row],
        [_sds((T, C), F32), _sds((ng, G, G), F32), _sds((ng, G, G), F32),
         _sds((1, C), F32), _sds((1, C), F32), _sds((1, C), F32)],
        [pltpu.VMEM((seq, G), F32), pltpu.VMEM((seq, G), F32),
         pltpu.VMEM((seq + 8, G), F32), pltpu.VMEM((seq + 8, G), F32)], comm)


TC_A = 256
TC_B = 512
TAPS_A, TAPS_B = 31, 4


def _tiles(T):
    return min(512, T), min(1024, T)


GATHERED = ("w_in", "w_1", "w_a_out", "w_b_out", "w_o", "w_2", "caw", "cbw")
GATHER_KIND = {"w_in": (True, True), "w_1": (True, True), "w_a_out": (False, True), "w_b_out": (False, True),
               "w_o": (False, True), "w_2": (False, True), "caw": (True, False), "cbw": (True, False)}


def _layer_fwd(x, p, seq, cur=None, nxt=None):
    T, D = x.shape
    C, R = p["ln_g"].shape[1], p["lam"].shape[1]
    tm, tl = _tiles(T)
    gb_blk, sa_blk = (2 * C + R) // TC_B, (2 * C + 2 * R) // D
    p, ahead = dict(p), {}

    def gather(src, names):
        return None if src is None else _gather_comm([src[n] for n in names], [GATHER_KIND[n] for n in names])

    def outs(r, src, names, into):
        if src is None:
            return r
        into.update(zip(names, r[1]))
        return r[0]

    mid = ["w_a_out", "w_b_out", "w_o", "caw", "cbw"]
    z, h = outs(_fwd_norm_mm("fwd_z", x, p["g_mix"], p["w_in"], p["b_in"], tl, 1024, comm=gather(cur, mid)), cur, mid, p)
    u1 = outs(_conv_fwd("conv_a_fwd", z, 0, C // TC_A, p["caw"], p["cab"], TAPS_A, seq, TC_A, BF16,
                        comm=gather(cur, ["w_1"])), cur, ["w_1"], p)
    ya, u3 = _fwd_ya(u1, p["ln_g"], p["ln_b"], p["w_a_out"], tm)
    v0 = _conv_fwd("conv_b_fwd", z, 2 * C // TC_B, None, p["cbw"], p["cbb"], TAPS_B, seq, TC_B, F32)
    hr, rg, ig, lag = outs(_rglru_fwd(v0, p["wa"], p["wx"], p["b_rg_a"], p["b_rg_x"], p["lam"], seq,
                                      comm=gather(nxt, ["w_in"])), nxt, ["w_in"], ahead)
    yb, pb = _fwd_yb(hr, z, gb_blk, p["w_b_out"], tl, TC_B)
    x1, mg = _fwd_x1(x, ya, yb, z, sa_blk, p["w_o"], tm)
    fp, h2 = outs(_fwd_norm_mm("fwd_f", x1, p["g_mlp"], p["w_1"], None, tl, 1024, comm=gather(cur, ["w_2"])),
                  cur, ["w_2"], p)
    x2 = _fwd_x2(x1, fp, p["w_2"], tm, fp.shape[1])
    saved = dict(x=x, z=z, h=h, u1=u1, u3=u3, ya=ya, v0=v0, hr=hr, rg=rg, ig=ig, lag=lag, pb=pb, yb=yb, mg=mg, x1=x1,
                 fp=fp, h2=h2)
    return x2, saved, p, ahead.get("w_in")


class _Reduce:
    EARLY = ("w_2", "w_1", "w_o", "w_a_out")
    LATE = ("w_b_out", "w_in")

    def __init__(self, accs, c_arr, kcl_of):
        self.accs, self.c_arr, self.kcl_of, self.late = accs, c_arr, kcl_of, None

    @staticmethod
    def pieces(partials):
        return [a if a.ndim == 4 else a.reshape(N_CHIPS, 2, a.shape[0] // (2 * N_CHIPS), a.shape[1]) for a in partials]

    def chip_sums(self, pgs, swapped):
        return _sum_siblings(pgs, swapped, self.c_arr)

    def finish(self, names, sums, received, layer):
        done = _sum_chips(sums, received, self.kcl_of(layer), [self.accs[n] for n in names])
        self.accs.update(zip(names, done))


def _layer_bwd(dx2, dx2b, p, s, seq, red=None, layer=0):
    T, D = dx2.shape
    C, R = p["ln_g"].shape[1], p["lam"].shape[1]
    tm, tl = _tiles(T)
    gb_blk, sa_blk = (2 * C + R) // TC_B, (2 * C + 2 * R) // D
    z = s["z"]
    g = {}


    late_sums = None
    if red is not None and red.late is not None:
        late, red.late = red.late, None
        dfp, got = _bwd_df(dx2b, p["w_2"], s["fp"], tl, 1024, comm=_swap_comm(late))
        late_sums = red.chip_sums(late, got)
    else:
        dfp = _bwd_df(dx2b, p["w_2"], s["fp"], tl, 1024)
    g["w_2"] = _bwd_dw("bwd_dw2", s["fp"], dx2b, 1024, D, T, a_fn=_relu2, keep="dy")
    dx1, dx1b, g["g_mlp"] = _bwd_norm("bwd_dh2", dfp, p["w_1"], s["x1"], p["g_mlp"], dx2, tm, dfp.shape[1])
    g["w_1"] = _bwd_dw("bwd_dw1", s["h2"], dfp, D, 1024, T, shard_cols=dfp.shape[1] // N_CHIPS, keep="act")

    dya, dyb, dz = _bwd_dm(dx1b, p["w_o"], s["ya"], s["yb"], z, sa_blk, tm)

    g["w_o"] = _bwd_dw("bwd_dwo", s["mg"], dx1b, D, D, tl)
    du1, g["ln_g"], g["ln_b"] = _bwd_du3(dya, p["w_a_out"], s["u1"], p["ln_g"], p["ln_b"], tm)
    g["w_a_out"] = _bwd_dw("bwd_dwa", s["u3"], dya, C, D, tl)
    conv_a_args = ("conv_a_bwd", du1, z, dz, 0, C // TC_A, p["caw"], TAPS_A, seq, TC_A)
    if late_sums is not None:
        (dz, g["caw"], g["cab"]), got = _conv_bwd(*conv_a_args, comm=_scatter_comm(late_sums))
        red.finish(red.LATE, late_sums, got, layer + 1)
    else:
        dz, g["caw"], g["cab"] = _conv_bwd(*conv_a_args)

    dp_args = (dyb, p["w_b_out"], s["hr"], z, dz, gb_blk, tl, TC_B)
    if red is not None:
        early = red.pieces([g.pop(n) for n in red.EARLY])
        (dhr, dz), got = _bwd_dp(*dp_args, comm=_swap_comm(early))
        early_sums = red.chip_sums(early, got)
    else:
        dhr, dz = _bwd_dp(*dp_args)

    g["w_b_out"] = _bwd_dw("bwd_dwb", s["pb"], dyb, R, D, tl)
    rg_args = (s["v0"], s["hr"], dhr, s["rg"], s["ig"], s["lag"], p["wa"], p["wx"], p["lam"], seq)
    if red is not None:
        rg_out, got = _rglru_bwd(*rg_args, comm=_scatter_comm(early_sums))
        red.finish(red.EARLY, early_sums, got, layer)
    else:
        rg_out = _rglru_bwd(*rg_args)
    dv0, g["wa"], g["wx"], g["b_rg_a"], g["b_rg_x"], g["lam"] = rg_out
    dz, g["cbw"], g["cbb"] = _conv_bwd("conv_b_bwd", dv0, z, dz, 2 * C // TC_B, None, p["cbw"], TAPS_B, seq, TC_B)

    dx, dxb, g["g_mix"], dbin = _bwd_norm("bwd_dh", dz, p["w_in"], s["x"], p["g_mix"], dx1, tm, dz.shape[1],
                                          colsum=True)
    g["b_in"] = dbin.reshape(1, -1)
    ns = dz.shape[1] // N_CHIPS
    g["w_in"] = _bwd_dw("bwd_dwin", s["h"], dz, D, ns // 2, T, shard_cols=ns, keep="act")
    if red is not None:
        red.late = red.pieces([g.pop(n) for n in red.LATE])
    return dx, dxb, g


ANY = pl.BlockSpec(memory_space=pl.ANY)


def _mesh_pos():
    return lax.axis_index("x"), lax.axis_index("y"), lax.axis_index("c")


def _other_chips(x, y):
    return [(1 - x, y), (x, 1 - y), (1 - x, 1 - y)]


def _remote(src, dst, ssem, rsem, dev):
    return pltpu.make_async_remote_copy(src_ref=src, dst_ref=dst, send_sem=ssem, recv_sem=rsem,
                                        device_id=dev, device_id_type=MESH)


def _gather_region(src, dst, by_cols, k, half):
    rows, cols = src.shape
    nr = rows if half is None else rows // 2
    r0 = 0 if half is None else half * nr
    if by_cols:
        return dst.at[pl.ds(r0, nr), pl.ds(pl.multiple_of(k * cols, 128), cols)]
    return dst.at[pl.ds(pl.multiple_of(k * rows + r0, 8), nr), :]


def _gather_sends(src, dst, kinds, send, recv):
    x, y, c = _mesh_pos()
    cps = []
    for t in range(len(src)):
        half = c if kinds[t][1] else None
        hr = src[t].shape[0] // 2
        s_ref = src[t].at[pl.ds(c * hr, hr), :] if kinds[t][1] else src[t]
        for j, chip in enumerate(_other_chips(x, y)):
            cps.append(_remote(s_ref, _gather_region(src[t], dst[t], kinds[t][0], 2 * x + y, half),
                               send.at[t, j], recv.at[t, j], (*chip, c)))
    return cps


def _gather_finish(src, dst, kinds, send, recv, fsend, frecv):
    x, y, c = _mesh_pos()
    chips = _other_chips(x, y)
    sib = (x, y, 1 - c)
    n = len(src)
    fwd = []
    for t in range(n):
        half = c if kinds[t][1] else None
        for j, chip in enumerate(chips):
            got = _gather_region(src[t], dst[t], kinds[t][0], 2 * chip[0] + chip[1], half)
            _remote(got, got, send.at[t, j], recv.at[t, j], (*chip, c)).wait_recv()
            if kinds[t][1]:
                cp = _remote(got, got, fsend.at[t, j], frecv.at[t, j], sib)
                cp.start()
                fwd.append(cp)
    for t in range(n):
        if kinds[t][1]:
            for j, chip in enumerate(chips):
                got = _gather_region(src[t], dst[t], kinds[t][0], 2 * chip[0] + chip[1], 1 - c)
                _remote(got, got, fsend.at[t, j], frecv.at[t, j], sib).wait_recv()
    for cp in _gather_sends(src, dst, kinds, send, recv) + fwd:
        cp.wait_send()


def _gather_sems(n):
    sem = pltpu.SemaphoreType.DMA
    return [sem((n, 3)), sem((n, 3)), sem((n, 3)), sem((n, 3))]


def _gather_comm(shards, kinds):
    n = len(shards)

    def whole(s, by_cols):
        return (s.shape[0], N_CHIPS * s.shape[1]) if by_cols else (N_CHIPS * s.shape[0], s.shape[1])

    def own(src, dst, lsem):
        x, y, _ = _mesh_pos()
        return [pltpu.make_async_copy(src[t], _gather_region(src[t], dst[t], kinds[t][0], 2 * x + y, None), lsem.at[t])
                for t in range(n)]

    def start(src, dst, sems):
        for cp in own(src, dst, sems[4]) + _gather_sends(src, dst, kinds, sems[0], sems[1]):
            cp.start()

    def finish(src, dst, sems):
        _gather_finish(src, dst, kinds, *sems[:4])
        for cp in own(src, dst, sems[4]):
            cp.wait()

    return _Comm(shards, [_sds(whole(s, k[0]), s.dtype) for s, k in zip(shards, kinds)],
                 _gather_sems(n) + [pltpu.SemaphoreType.DMA((n,))], start, finish)


def _scatter_comm(ps):
    n = len(ps)

    def copies(src, dst, sems):
        x, y, c = _mesh_pos()
        return [_remote(src[t].at[2 * chip[0] + chip[1]], dst[t].at[j], sems[0].at[t, j], sems[1].at[t, j], (*chip, c))
                for t in range(n) for j, chip in enumerate(_other_chips(x, y))]

    def start(src, dst, sems):
        for cp in copies(src, dst, sems):
            cp.start()

    def finish(src, dst, sems):
        cps = copies(src, dst, sems)
        for cp in cps:
            cp.wait_recv()
        for cp in cps:
            cp.wait_send()

    sem = pltpu.SemaphoreType.DMA
    return _Comm(ps, [_sds((3,) + a.shape[1:], a.dtype) for a in ps], [sem((n, 3)), sem((n, 3))], start, finish)


def _comm_call(name, comm):
    n_i, n_o = len(comm.ins), len(comm.outs)

    def body(*refs):
        comm.start(refs[:n_i], refs[n_i:n_i + n_o], refs[n_i + n_o:])
        comm.finish(refs[:n_i], refs[n_i:n_i + n_o], refs[n_i + n_o:])

    return pl.pallas_call(
        body, name=name, in_specs=[ANY] * n_i, out_specs=[ANY] * n_o, out_shape=comm.outs, scratch_shapes=comm.sems,
        compiler_params=_cp(has_side_effects=True),
    )(*comm.ins)


def _swap_comm(pgs):
    n = len(pgs)

    def copies(src, dst, sems):
        x, y, c = _mesh_pos()
        return [_remote(src[t].at[:, 1 - c], dst[t], sems[0].at[t], sems[1].at[t], (x, y, 1 - c)) for t in range(n)]

    def start(src, dst, sems):
        for cp in copies(src, dst, sems):
            cp.start()

    def finish(src, dst, sems):
        cps = copies(src, dst, sems)
        for cp in cps:
            cp.wait_recv()
        for cp in cps:
            cp.wait_send()

    sem = pltpu.SemaphoreType.DMA
    return _Comm(pgs, [_sds((a.shape[0],) + a.shape[2:], a.dtype) for a in pgs], [sem((n,)), sem((n,))], start, finish)


def _join_halves(accs, also=None):
    n = len(accs)
    c_ins, c_outs, c_sems = (also.ins, also.outs, also.sems) if also else ([], [], [])

    def body(*refs):
        o0 = n + len(c_ins)
        buf = refs[o0:o0 + n]
        send, recv = refs[o0 + n + len(c_outs):o0 + n + len(c_outs) + 2]
        extra = (refs[n:o0], refs[o0 + n:o0 + n + len(c_outs)], refs[o0 + n + len(c_outs) + 2:])
        x, y, c = _mesh_pos()
        if also:
            also.start(*extra)
        cps = [_remote(buf[t].at[:, c], buf[t].at[:, c], send.at[t], recv.at[t], (x, y, 1 - c)) for t in range(n)]
        for cp in cps:
            cp.start()
        for t in range(n):
            _remote(buf[t].at[:, c], buf[t].at[:, 1 - c], send.at[t], recv.at[t], (x, y, 1 - c)).wait_recv()
        for cp in cps:
            cp.wait_send()
        if also:
            also.finish(*extra)

    sem = pltpu.SemaphoreType.DMA
    res = pl.pallas_call(
        body, name="join_halves", in_specs=[ANY] * (n + len(c_ins)), out_specs=[ANY] * (n + len(c_outs)),
        out_shape=[_sds(a.shape, a.dtype) for a in accs] + list(c_outs),
        scratch_shapes=[sem((n,)), sem((n,))] + list(c_sems),
        input_output_aliases={t: t for t in range(n)}, compiler_params=_cp(has_side_effects=True),
    )(*accs, *c_ins)
    return (list(res[:n]), list(res[n:])) if also else res


def _sum_siblings(pgs, rbs, c_arr):
    n = len(pgs)
    nk = pgs[0].shape[0]

    def body(c_ref, *refs):
        for t in range(n):
            refs[2 * n + t][...] = (refs[t][...].astype(F32) + refs[n + t][...].astype(F32)).astype(BF16)

    half = lambda a: pl.BlockSpec((None,) + a.shape[2:], lambda k, c_ref: (k, 0, 0))
    return pl.pallas_call(
        body, name="sum_siblings",
        grid_spec=pltpu.PrefetchScalarGridSpec(
            num_scalar_prefetch=1, grid=(nk,),
            in_specs=[pl.BlockSpec((None, None) + a.shape[2:], lambda k, c_ref: (k, c_ref[0], 0, 0)) for a in pgs]
            + [half(a) for a in pgs],
            out_specs=[half(a) for a in pgs]),
        out_shape=[_sds((nk,) + a.shape[2:], BF16) for a in pgs],
        compiler_params=_cp(dimension_semantics=("arbitrary",)),
    )(c_arr, *pgs, *rbs)


def _sum_chips(ps, rbs, kcl, accs):
    n = len(ps)

    def body(k_ref, *refs):
        for t in range(n):
            b_ref = refs[n + t]
            refs[3 * n + t][...] = (refs[t][...].astype(F32) + b_ref[0].astype(F32) + b_ref[1].astype(F32)
                                    + b_ref[2].astype(F32))

    qr = lambda a: (a.shape[1] // 2, a.shape[2])
    return pl.pallas_call(
        body, name="sum_chips",
        grid_spec=pltpu.PrefetchScalarGridSpec(
            num_scalar_prefetch=1, grid=(2,),
            in_specs=[pl.BlockSpec((None,) + qr(a), lambda r, k_ref: (k_ref[0], r, 0)) for a in ps]
            + [pl.BlockSpec((3,) + qr(a), lambda r, k_ref: (0, r, 0)) for a in ps] + [ANY] * n,
            out_specs=[pl.BlockSpec((None, None) + qr(a), lambda r, k_ref: (k_ref[2], k_ref[1], r, 0)) for a in ps]),
        out_shape=[_sds(a.shape, F32) for a in accs], input_output_aliases={1 + 2 * n + t: t for t in range(n)},
        compiler_params=_cp(dimension_semantics=("arbitrary",)),
    )(kcl, *ps, *rbs, *accs)


N_DEV = 8


def _allreduce_small(parts):
    n = len(parts)

    def body(*refs):
        p_refs, o_refs, rbufs = refs[:n], refs[n:2 * n], refs[2 * n:3 * n]
        s1, r1, s2, r2 = refs[3 * n:]
        x, y, c = _mesh_pos()
        me = 4 * x + 2 * y + c
        devs = [(d // 4, (d // 2) % 2, d % 2) for d in range(N_DEV)]
        for q in range(n):
            rbufs[q][me] = p_refs[q][me]

        def each_peer(fn):
            for d in range(N_DEV):
                @pl.when(d != me)
                def _():
                    for q in range(n):
                        fn(d, q)

        def first(d, q, to_me):
            return _remote(p_refs[q].at[d], rbufs[q].at[d if to_me else me], s1.at[q, d], r1.at[q, d if to_me else me],
                           devs[d])

        def second(d, q, to_me):
            blk = d if to_me else me
            return _remote(o_refs[q].at[blk], o_refs[q].at[blk], s2.at[q, d], r2.at[q, blk], devs[d])

        each_peer(lambda d, q: first(d, q, False).start())
        each_peer(lambda d, q: first(d, q, True).wait_recv())
        for q in range(n):
            total = rbufs[q][0].astype(F32)
            for d in range(1, N_DEV):
                total = total + rbufs[q][d].astype(F32)
            o_refs[q][me] = total.astype(o_refs[q].dtype)
        each_peer(lambda d, q: second(d, q, False).start())
        each_peer(lambda d, q: second(d, q, True).wait_recv())
        each_peer(lambda d, q: first(d, q, False).wait_send())
        each_peer(lambda d, q: second(d, q, False).wait_send())

    sem = pltpu.SemaphoreType.DMA
    vm = pl.BlockSpec(memory_space=pltpu.VMEM)
    return pl.pallas_call(
        body, name="allreduce_small", in_specs=[vm] * n, out_specs=[vm] * n,
        out_shape=[_sds(a.shape, a.dtype) for a in parts],
        scratch_shapes=[pltpu.VMEM(a.shape, a.dtype) for a in parts] + [sem((n, N_DEV))] * 4,
        compiler_params=_cp(has_side_effects=True),
    )(*parts)


BIG = ("w_in", "w_1", "w_a_out", "w_b_out", "w_o", "w_2")
BY_COLS = {"w_in": True, "w_1": True, "w_a_out": False, "w_b_out": False, "w_o": False, "w_2": False}
WEIGHTS = ("g_mix", "w_in", "b_in", "conv_a_w", "conv_a_b", "ln_g", "ln_b", "w_a_out", "conv_b_w", "conv_b_b", "w_rg_a",
           "b_rg_a", "w_rg_x", "b_rg_x", "lam", "w_b_out", "w_o", "g_mlp", "w_1", "w_2", "g_final")
SMALL = tuple(n for n in WEIGHTS if n not in BIG)
ADAM_ROWS = 256
ADAM_SMALL_ROWS = 2048


def _block_diag(w):
    nh, dh, _ = w.shape
    ng = nh // HEADS_PER_GROUP
    w4 = w.reshape(ng, HEADS_PER_GROUP, dh, dh)
    eye = jnp.eye(HEADS_PER_GROUP, dtype=w.dtype)
    return jnp.einsum("qhij,hk->qhikj", w4, eye).reshape(ng, HEADS_PER_GROUP * dh, HEADS_PER_GROUP * dh)


def _block_diag_part(d, dh):
    ng = d.shape[0]
    eye = jnp.eye(HEADS_PER_GROUP, dtype=d.dtype)
    d5 = d.reshape(ng, HEADS_PER_GROUP, dh, HEADS_PER_GROUP, dh)
    return jnp.einsum("qhikj,hk->qhij", d5, eye).reshape(ng * HEADS_PER_GROUP, dh, dh)


PACK_LANES = 128


def _pack(arrays, blocks, tile_rows):
    parts = [a.reshape(-1, PACK_LANES) for a in arrays]
    parts = [jnp.pad(p, ((0, -p.shape[0] % tile_rows), (0, 0))) if p.shape[0] % tile_rows else p for p in parts]
    rows = sum(p.shape[0] for p in parts)
    pad = -rows % (blocks * tile_rows)
    if pad:
        parts.append(jnp.zeros((pad, PACK_LANES), parts[0].dtype))
    return jnp.concatenate(parts, axis=0).reshape(blocks, -1, PACK_LANES)


def _unpack(buf, like, tile_rows):
    buf = buf.reshape(-1, PACK_LANES)
    out, off = [], 0
    for a in like:
        n = a.size // PACK_LANES
        out.append(buf[off:off + n].reshape(a.shape))
        off += n + (-n % tile_rows)
    return out


def kernel(x, g_mix, w_in, b_in, conv_a_w, conv_a_b, ln_g, ln_b, w_a_out, conv_b_w, conv_b_b, w_rg_a, b_rg_a, w_rg_x, b_rg_x, lam, w_b_out, w_o, g_mlp, w_1, w_2, g_final, loss_target, m_g_mix, m_w_in, m_b_in, m_conv_a_w, m_conv_a_b, m_ln_g, m_ln_b, m_w_a_out, m_conv_b_w, m_conv_b_b, m_w_rg_a, m_b_rg_a, m_w_rg_x, m_b_rg_x, m_lam, m_w_b_out, m_w_o, m_g_mlp, m_w_1, m_w_2, m_g_final, v_g_mix, v_w_in, v_b_in, v_conv_a_w, v_conv_a_b, v_ln_g, v_ln_b, v_w_a_out, v_conv_b_w, v_conv_b_b, v_w_rg_a, v_b_rg_a, v_w_rg_x, v_b_rg_x, v_lam, v_w_b_out, v_w_o, v_g_mlp, v_w_1, v_w_2, v_g_final):
    w = dict(g_mix=g_mix, w_in=w_in, b_in=b_in, conv_a_w=conv_a_w, conv_a_b=conv_a_b, ln_g=ln_g, ln_b=ln_b, w_a_out=w_a_out,
             conv_b_w=conv_b_w, conv_b_b=conv_b_b, w_rg_a=w_rg_a, b_rg_a=b_rg_a, w_rg_x=w_rg_x, b_rg_x=b_rg_x, lam=lam,
             w_b_out=w_b_out, w_o=w_o, g_mlp=g_mlp, w_1=w_1, w_2=w_2, g_final=g_final)
    m = dict(g_mix=m_g_mix, w_in=m_w_in, b_in=m_b_in, conv_a_w=m_conv_a_w, conv_a_b=m_conv_a_b, ln_g=m_ln_g, ln_b=m_ln_b,
             w_a_out=m_w_a_out, conv_b_w=m_conv_b_w, conv_b_b=m_conv_b_b, w_rg_a=m_w_rg_a, b_rg_a=m_b_rg_a, w_rg_x=m_w_rg_x,
             b_rg_x=m_b_rg_x, lam=m_lam, w_b_out=m_w_b_out, w_o=m_w_o, g_mlp=m_g_mlp, w_1=m_w_1, w_2=m_w_2, g_final=m_g_final)
    v = dict(g_mix=v_g_mix, w_in=v_w_in, b_in=v_b_in, conv_a_w=v_conv_a_w, conv_a_b=v_conv_a_b, ln_g=v_ln_g, ln_b=v_ln_b,
             w_a_out=v_w_a_out, conv_b_w=v_conv_b_w, conv_b_b=v_conv_b_b, w_rg_a=v_w_rg_a, b_rg_a=v_b_rg_a, w_rg_x=v_w_rg_x,
             b_rg_x=v_b_rg_x, lam=v_lam, w_b_out=v_w_b_out, w_o=v_w_o, g_mlp=v_g_mlp, w_1=v_w_1, w_2=v_w_2, g_final=v_g_final)
    B, S, D = x.shape
    T = B * S
    L = w_in.shape[0]
    dh = w_rg_a.shape[-1]
    taps_a, taps_b = conv_a_w.shape[1], conv_b_w.shape[1]
    assert (taps_a, taps_b) == (TAPS_A, TAPS_B)
    xi, yi, ci = _mesh_pos()
    c_arr = jnp.reshape(ci, (1,)).astype(jnp.int32)
    k_me = 2 * xi + yi

    caw_p = jnp.pad(conv_a_w, ((0, 0), (0, 32 - taps_a), (0, 0)))
    cbw_p = jnp.pad(conv_b_w, ((0, 0), (0, 8 - taps_b), (0, 0)))
    row = lambda a: a.reshape(1, -1)

    def shards_of(l):
        d = {n: w[n][l].astype(BF16) for n in BIG}
        d.update(caw=caw_p[l], cbw=cbw_p[l])
        return d

    def params_of(l, w_in_whole):
        p = dict(w_in=w_in_whole, cab=row(conv_a_b[l]), cbb=row(conv_b_b[l]),
                 wa=_block_diag(w_rg_a[l]).astype(BF16), wx=_block_diag(w_rg_x[l]).astype(BF16))
        for n in ("g_mix", "b_in", "ln_g", "ln_b", "b_rg_a", "b_rg_x", "lam", "g_mlp"):
            p[n] = row(w[n][l])
        return p

    shards = [shards_of(l) for l in range(L)]
    w_in_whole, = _comm_call("gather_first", _gather_comm([shards[0]["w_in"]], [GATHER_KIND["w_in"]]))
    xf = x.reshape(T, D)
    saved, params = [], []
    for l in range(L):
        xf, s, p, w_in_whole = _layer_fwd(xf, params_of(l, w_in_whole), S, cur=shards[l],
                                          nxt=shards[l + 1] if l + 1 < L else None)
        saved.append(s)
        params.append(p)
    loss_part, dx, dxb, dg_final = _loss_head(xf, row(g_final), loss_target.reshape(T, D), _tiles(T)[0])
    loss = lax.psum(loss_part[0, 0], ("x", "y", "c"))

    half_shape = lambda a: (L, 2, a.shape[1] // 2, a.shape[2])
    accs = {n: lax.empty(half_shape(w[n]), F32) for n in BIG}
    small = {n: [None] * L for n in SMALL if n != "g_final"}
    red = _Reduce(accs, c_arr, lambda l: jnp.stack([k_me, ci, jnp.full((), l, ci.dtype)]).astype(jnp.int32))
    for l in reversed(range(L)):
        dx, dxb, g = _layer_bwd(dx, dxb, params[l], saved[l], S, red=red, layer=l)
        small["g_mix"][l], small["b_in"][l], small["g_mlp"][l] = g["g_mix"], g["b_in"], g["g_mlp"]
        small["conv_a_w"][l], small["conv_a_b"][l] = g["caw"], g["cab"]
        small["conv_b_w"][l], small["conv_b_b"][l] = g["cbw"], g["cbb"]
        small["ln_g"][l], small["ln_b"][l], small["lam"][l] = g["ln_g"], g["ln_b"], g["lam"]
        small["w_rg_a"][l], small["w_rg_x"][l] = _block_diag_part(g["wa"], dh), _block_diag_part(g["wx"], dh)
        small["b_rg_a"][l], small["b_rg_x"][l] = g["b_rg_a"], g["b_rg_x"]
    grad_x = dx.reshape(B, S, D)

    delta, new_m, new_v = {}, {}, {}
    flat = lambda a: a.reshape(-1, a.shape[-1])

    def adam_big(names, comm=None):
        r = _adamw("adamw_" + names[0], *[[flat(d[n]) for n in names] for d in (w, grads, m, v)], ADAM_ROWS, comm=comm)
        for q, n in enumerate(names):
            delta[n], new_m[n], new_v[n] = (r[a][q].reshape(w[n].shape) for a in range(3))
        return r[3] if comm else None

    late_sums = red.chip_sums(red.late, _comm_call("swap_halves", _swap_comm(red.late)))
    joined, got = _join_halves([red.accs[n] for n in red.EARLY], also=_scatter_comm(late_sums))
    grads = {n: a.reshape(w[n].shape) for n, a in zip(red.EARLY, joined)}
    adam_big(["w_2", "w_1"])
    adam_big(["w_o", "w_a_out"])
    red.finish(red.LATE, late_sums, got, 0)
    joined = _join_halves([red.accs[n] for n in red.LATE])
    grads.update({n: a.reshape(w[n].shape) for n, a in zip(red.LATE, joined)})
    adam_big(["w_b_out"])
    adam_big(["w_in"])

    wide = ["w_rg_a", "w_rg_x"]
    names = [n for n in SMALL if n != "g_final" and n not in wide]
    parts = [jnp.stack(small[n]) for n in names] + [dg_final]
    parts_w = [jnp.stack(small[n]).astype(BF16) for n in wide]
    total, total_w = _allreduce_small([_pack(parts, N_DEV, 8), _pack(parts_w, N_DEV, 16)])
    summed = _unpack(total, parts, 8) + [a.astype(F32) for a in _unpack(total_w, parts_w, 16)]
    for n, a in zip(names + ["g_final"] + wide, summed):
        if n == "conv_a_w":
            a = lax.dynamic_slice_in_dim(a[:, :taps_a], k_me * conv_a_w.shape[2], conv_a_w.shape[2], axis=2)
        elif n == "conv_b_w":
            a = lax.dynamic_slice_in_dim(a[:, :taps_b], k_me * conv_b_w.shape[2], conv_b_w.shape[2], axis=2)
        grads[n] = a.reshape(w[n].shape)

    for n in SMALL:
        cols = w[n].shape[-1]
        view = lambda a: a.reshape(-1, cols)
        rows = view(w[n]).shape[0]
        d_, m_, v_ = _adamw("adamw_" + n, view(w[n]), view(grads[n]), view(m[n]), view(v[n]),
                            ADAM_SMALL_ROWS if rows % ADAM_SMALL_ROWS == 0 else rows)
        delta[n], new_m[n], new_v[n] = (a.reshape(w[n].shape) for a in (d_, m_, v_))

    return (loss, grad_x, *[grads[n] for n in WEIGHTS], *[delta[n] for n in WEIGHTS],
            *[new_m[n] for n in WEIGHTS], *[new_v[n] for n in WEIGHTS])
```

```python
import jax
import jax.numpy as jnp
from jax import lax
from jax.experimental import pallas as pl
from jax.experimental.pallas import tpu as pltpu

F32 = jnp.float32
BF16 = jnp.bfloat16
MESH = pl.DeviceIdType.MESH

EPS = 1e-6
LRU_C = 8.0
ADAM_LR, ADAM_B1, ADAM_B2, ADAM_EPS, ADAM_WD, ADAM_STEP = 0.001, 0.9, 0.999, 1e-08, 0.01, 10

N_CHIPS = 4
HEADS_PER_GROUP = 4
VMEM_LIMIT = 56 * 1024 * 1024


def _cp(**kw):
    return pltpu.CompilerParams(vmem_limit_bytes=VMEM_LIMIT, **kw)


def _sig(x):
    return 1.0 / (1.0 + jnp.exp(-x))


def _gelu(x):
    t = jnp.tanh(0.7978845608028654 * (x + 0.044715 * x * x * x))
    return 0.5 * x * (1.0 + t), t


def _gelu_grad(x, t):
    dt = (1.0 - t * t) * 0.7978845608028654 * (1.0 + 3.0 * 0.044715 * x * x)
    return 0.5 * (1.0 + t) + 0.5 * x * dt


def _rms(xf, g):
    r = lax.rsqrt(jnp.mean(xf * xf, axis=-1, keepdims=True) + EPS)
    return xf * r * g, r


def _rms_bwd(xf, g, r, dh):
    dyg = dh * g
    dx = r * (dyg - xf * (r * r) * jnp.mean(dyg * xf, axis=-1, keepdims=True))
    return dx, dh * xf * r


def _ln_silu(u, g, b):
    mu = jnp.mean(u, axis=-1, keepdims=True)
    uc = u - mu
    rstd = lax.rsqrt(jnp.mean(uc * uc, axis=-1, keepdims=True) + EPS)
    uh = uc * rstd
    u2 = uh * g + b
    s = _sig(u2)
    return u2 * s, uh, rstd, u2, s


_DIMS = {"nn": (((1,), (0,)), ((), ())), "nt": (((1,), (1,)), ((), ())), "tn": (((0,), (0,)), ((), ()))}


class _Comm:
    def __init__(self, ins, outs, sems, start, finish):
        self.ins, self.outs, self.sems, self.start, self.finish = list(ins), list(outs), list(sems), start, finish


def _resident(shape):
    return pl.BlockSpec(shape, lambda i, j, k: (0,) * len(shape), pipeline_mode=pl.Buffered(1))


def _mm(name, mode, grid, a_ins, a_fn, b_in, e_ins, epi, outs, acc_shape, cache_a=None, alias=(), extra_scratch=(),
        comm=None, b_slice=None):
    ni, nj, nk = grid
    na, ne, no = len(a_ins), len(e_ins), len(outs)
    assert cache_a is None or nk == 1
    n_fixed = (nk > 1) + (cache_a is not None)
    n_in = na + 1 + ne + len(alias)
    c_ins, c_outs, c_sems = (comm.ins, comm.outs, comm.sems) if comm else ([], [], [])

    def body(*refs):
        a_refs = refs[:na]
        b_ref = refs[na]
        e_refs = refs[na + 1:na + 1 + ne]
        comm_in = refs[n_in:n_in + len(c_ins)]
        out0 = n_in + len(c_ins)
        out_refs = refs[out0:out0 + no]
        comm_out = refs[out0 + no:out0 + no + len(c_outs)]
        scratch = refs[out0 + no + len(c_outs):]
        extra = scratch[n_fixed:n_fixed + len(extra_scratch)]
        comm_sems = scratch[n_fixed + len(extra_scratch):]
        i, j, k = pl.program_id(0), pl.program_id(1), pl.program_id(2)
        if comm:
            @pl.when((i == 0) & (j == 0) & (k == 0))
            def _():
                comm.start(comm_in, comm_out, comm_sems)
        if cache_a is not None:
            cache_ref = scratch[n_fixed - 1]

            @pl.when(j == 0)
            def _():
                cache_ref[...] = a_fn(a_refs, out_refs, i, j, k)

            a = cache_ref[...]
        else:
            a = a_fn(a_refs, out_refs, i, j, k)
        if b_slice is None:
            b = b_ref[...]
        elif b_slice[0] == "cols":
            b = b_ref[:, pl.ds(pl.multiple_of(j * b_slice[1], b_slice[1]), b_slice[1])]
        else:
            b = b_ref[pl.ds(pl.multiple_of(j * b_slice[1], b_slice[1]), b_slice[1]), :]
        prod = lax.dot_general(a, b, _DIMS[mode], preferred_element_type=F32)
        if nk == 1:
            epi(prod, e_refs, out_refs, i, j, extra)
        else:
            acc_ref = scratch[0]

            @pl.when(k == 0)
            def _():
                acc_ref[...] = prod

            @pl.when(k > 0)
            def _():
                acc_ref[...] += prod

            @pl.when(k == nk - 1)
            def _():
                epi(acc_ref[...], e_refs, out_refs, i, j, extra)

        if comm:
            @pl.when((i == ni - 1) & (j == nj - 1) & (k == nk - 1))
            def _():
                comm.finish(comm_in, comm_out, comm_sems)

    scratch_shapes = []
    if nk > 1:
        scratch_shapes.append(pltpu.VMEM(acc_shape, F32))
    if cache_a is not None:
        scratch_shapes.append(pltpu.VMEM(cache_a, BF16))
    any_spec = pl.BlockSpec(memory_space=pl.ANY)
    ins = (list(a_ins) + [b_in] + list(e_ins) + [(arr, any_spec) for arr, _ in alias] + [(arr, any_spec) for arr in c_ins])
    first_alias = na + 1 + ne
    res = pl.pallas_call(
        body, name=name, grid=grid,
        in_specs=[s for _, s in ins], out_specs=[s for _, s in outs] + [any_spec] * len(c_outs),
        out_shape=[o for o, _ in outs] + list(c_outs),
        scratch_shapes=scratch_shapes + list(extra_scratch) + list(c_sems),
        input_output_aliases={first_alias + n: o for n, (_, o) in enumerate(alias)},
        compiler_params=_cp(dimension_semantics=("arbitrary", "arbitrary", "arbitrary"), has_side_effects=bool(comm)),
    )(*[a for a, _ in ins])
    if comm:
        return list(res[:no]), list(res[no:])
    return res


def _bs(shape, fn):
    return pl.BlockSpec(shape, fn)


def _sds(shape, dt):
    return jax.ShapeDtypeStruct(shape, dt)


def _acc_rows(ref, val, first):
    @pl.when(first)
    def _():
        ref[...] = val

    @pl.when(jnp.logical_not(first))
    def _():
        ref[...] += val


def _fwd_norm_mm(name, x, g, w, bias, tm, tn, comm=None):
    T, D = x.shape
    N = w.shape[1]

    def a_fn(a_refs, out_refs, i, j, k):
        h, _ = _rms(a_refs[0][...], a_refs[1][...])
        hb = h.astype(BF16)
        out_refs[1][...] = hb
        return hb

    def epi(acc, e_refs, out_refs, i, j, extra):
        if bias is not None:
            acc = acc + e_refs[0][...]
        out_refs[0][...] = acc.astype(BF16)

    e_ins = [] if bias is None else [(bias, _bs((1, tn), lambda i, j, k: (0, j)))]
    return _mm(name, "nn", (T // tm, N // tn, 1),
               [(x, _bs((tm, D), lambda i, j, k: (i, 0))), (g, _bs((1, D), lambda i, j, k: (0, 0)))], a_fn,
               (w, _resident((D, N))), e_ins, epi,
               [(_sds((T, N), BF16), _bs((tm, tn), lambda i, j, k: (i, j))),
                (_sds((T, D), BF16), _bs((tm, D), lambda i, j, k: (i, 0)))],
               None, cache_a=(tm, D), comm=comm, b_slice=("cols", tn))


def _fwd_ya(u1, ln_g, ln_b, w, tm):
    T, C = u1.shape
    N = w.shape[1]

    def a_fn(a_refs, out_refs, i, j, k):
        u3 = _ln_silu(a_refs[0][...].astype(F32), a_refs[1][...], a_refs[2][...])[0].astype(BF16)
        out_refs[1][...] = u3
        return u3

    def epi(acc, e_refs, out_refs, i, j, extra):
        out_refs[0][...] = acc.astype(BF16)

    row = _bs((1, C), lambda i, j, k: (0, 0))
    tc = _bs((tm, C), lambda i, j, k: (i, 0))
    return _mm("fwd_ya", "nn", (T // tm, 1, 1), [(u1, tc), (ln_g, row), (ln_b, row)], a_fn,
               (w, _bs((C, N), lambda i, j, k: (0, 0))), [], epi,
               [(_sds((T, N), BF16), _bs((tm, N), lambda i, j, k: (i, 0))), (_sds((T, C), BF16), tc)], None)


def _fwd_yb(h, z, gb_blk, w, tm, tk):
    T, C = h.shape
    N = w.shape[1]

    def a_fn(a_refs, out_refs, i, j, k):
        ge, _ = _gelu(a_refs[1][...].astype(F32))
        pv = (a_refs[0][...].astype(F32) * ge).astype(BF16)
        out_refs[1][...] = pv
        return pv

    def epi(acc, e_refs, out_refs, i, j, extra):
        out_refs[0][...] = acc.astype(BF16)

    tk_ = _bs((tm, tk), lambda i, j, k: (i, k))
    return _mm("fwd_yb", "nn", (T // tm, 1, C // tk),
               [(h, tk_), (z, _bs((tm, tk), lambda i, j, k: (i, gb_blk + k)))], a_fn,
               (w, _bs((tk, N), lambda i, j, k: (k, 0))), [], epi,
               [(_sds((T, N), BF16), _bs((tm, N), lambda i, j, k: (i, 0))), (_sds((T, C), BF16), tk_)], (tm, N))


def _fwd_x1(x, ya, yb, z, sa_blk, w, tm, comm=None):
    T, D = x.shape

    def a_fn(a_refs, out_refs, i, j, k):
        ya_, yb_, sa_, sb_ = (r[...].astype(F32) for r in a_refs)
        mg = (_sig(sa_) * ya_ + _sig(sb_) * yb_).astype(BF16)
        out_refs[1][...] = mg
        return mg

    def epi(acc, e_refs, out_refs, i, j, extra):
        out_refs[0][...] = e_refs[0][...] + acc

    t = _bs((tm, D), lambda i, j, k: (i, 0))
    return _mm("fwd_x1", "nn", (T // tm, 1, 1),
               [(ya, t), (yb, t), (z, _bs((tm, D), lambda i, j, k: (i, sa_blk))),
                (z, _bs((tm, D), lambda i, j, k: (i, sa_blk + 1)))], a_fn,
               (w, _bs((D, D), lambda i, j, k: (0, 0))), [(x, t)], epi,
               [(_sds((T, D), F32), t), (_sds((T, D), BF16), t)], None, comm=comm)


def _fwd_x2(x1, fp, w, tm, tk, comm=None):
    T, D = x1.shape
    Fd = fp.shape[1]

    def epi(acc, e_refs, out_refs, i, j, extra):
        out_refs[0][...] = e_refs[0][...] + acc

    t = _bs((tm, D), lambda i, j, k: (i, 0))
    whole_k = tk == Fd
    r = _mm("fwd_x2", "nn", (T // tm, 1, Fd // tk),
            [(fp, _bs((tm, tk), lambda i, j, k: (i, k)))], _relu2,
            (w, _resident((Fd, D)) if whole_k else _bs((tk, D), lambda i, j, k: (k, 0))), [(x1, t)], epi,
            [(_sds((T, D), F32), t)], (tm, D), comm=comm)
    return (r[0][0], r[1]) if comm else r[0]


def _relu2(a_refs, out_refs, i, j, k):
    f = jnp.maximum(a_refs[0][...], 0.0)
    return f * f


def _loss_head(x, g, target, tm):
    T, D = x.shape

    def body(x_ref, g_ref, t_ref, loss_ref, dx_ref, dxb_ref, dg_ref):
        i = pl.program_id(0)
        xf, gv = x_ref[...], g_ref[...]
        y, r = _rms(xf, gv)
        err = y - t_ref[...]
        part = 0.5 * jnp.sum(jnp.mean(err * err, axis=-1, keepdims=True), axis=0, keepdims=True)
        dx, dg_rows = _rms_bwd(xf, gv, r, err * (1.0 / D))
        dx_ref[...] = dx
        dxb_ref[...] = dx.astype(BF16)
        _acc_rows(loss_ref, jnp.broadcast_to(part, (1, 128)), i == 0)
        _acc_rows(dg_ref, jnp.sum(dg_rows, axis=0, keepdims=True), i == 0)

    t = _bs((tm, D), lambda i: (i, 0))
    row = _bs((1, D), lambda i: (0, 0))
    return pl.pallas_call(
        body, name="loss_head", grid=(T // tm,), in_specs=[t, row, t],
        out_specs=[_bs((1, 128), lambda i: (0, 0)), t, t, row],
        out_shape=[_sds((1, 128), F32), _sds((T, D), F32), _sds((T, D), BF16), _sds((1, D), F32)],
        compiler_params=_cp(dimension_semantics=("arbitrary",)),
    )(x, g, target)


def _adamw(name, w, g, m, v, tr, comm=None):
    many = isinstance(w, (list, tuple))
    ws, gs, ms, vs = (list(a) if many else [a] for a in (w, g, m, v))
    n = len(ws)
    rows, cols = ws[0].shape
    d1 = 1.0 - ADAM_B1 ** ADAM_STEP
    d2 = 1.0 - ADAM_B2 ** ADAM_STEP

    def body(*refs):
        for q in range(n):
            w_ref, g_ref, m_ref, v_ref = (refs[a * n + q] for a in range(4))
            d_ref, mo_ref, vo_ref = (refs[(4 + a) * n + q] for a in range(3))
            gv = g_ref[...]
            mn = ADAM_B1 * m_ref[...] + (1.0 - ADAM_B1) * gv
            vn = ADAM_B2 * v_ref[...] + (1.0 - ADAM_B2) * (gv * gv)
            d_ref[...] = -ADAM_LR * ((mn / d1) / (jnp.sqrt(vn / d2) + ADAM_EPS) + ADAM_WD * w_ref[...])
            mo_ref[...] = mn
            vo_ref[...] = vn

    t = _bs((tr, cols), lambda i: (i, 0))
    r = _call_with_comm(name, body, (rows // tr,), ws + gs + ms + vs, [t] * (4 * n), [t] * (3 * n),
                        [_sds((rows, cols), F32)] * (3 * n), [], comm)
    outs, got = (r if comm else (r, None))
    res = [outs[a * n:(a + 1) * n] if many else outs[a * n] for a in range(3)]
    return (*res, got) if comm else tuple(res)


def _ident(a_refs, out_refs, i, j, k):
    return a_refs[0][...]


def _bwd_dw(name, act, dy, ti, tj, tm, a_fn=None, a_extra=(), shard_cols=None, keep=None):
    T, J = dy.shape
    I = act.shape[1]

    def epi(acc, e_refs, out_refs, i, j, extra):
        out_refs[0][...] = acc.astype(BF16).reshape(out_refs[0].shape)

    if shard_cols is None:
        out = (_sds((I, J), BF16), _bs((ti, tj), lambda i, j, k: (i, j)))
    else:
        per = shard_cols // tj
        assert ti == I and per * tj == shard_cols
        out = (_sds((J // shard_cols, 2, I // 2, shard_cols), BF16),
               _bs((None, 2, I // 2, tj), lambda i, j, k: (lax.div(j, per), 0, 0, lax.rem(j, per))))
    assert keep is None or tm == T
    a_spec = _resident((T, I)) if keep == "act" else _bs((tm, ti), lambda i, j, k: (k, i))
    b_spec = _resident((T, J)) if keep == "dy" else _bs((tm, tj), lambda i, j, k: (k, j))
    return _mm(name, "tn", (I // ti, J // tj, T // tm), [(act, a_spec)] + list(a_extra), a_fn or _ident,
               (dy, b_spec), [], epi, [out], (ti, tj))[0]


def _bwd_df(dxb, w2, fp, tm, tn, comm=None):
    T, D = dxb.shape
    Fd = w2.shape[0]

    def epi(acc, e_refs, out_refs, i, j, extra):
        out_refs[0][...] = (acc * (2.0 * jnp.maximum(e_refs[0][...].astype(F32), 0.0))).astype(BF16)

    t = _bs((tm, tn), lambda i, j, k: (i, j))
    r = _mm("bwd_df", "nt", (T // tm, Fd // tn, 1), [(dxb, _bs((tm, D), lambda i, j, k: (i, 0)))], _ident,
            (w2, _resident((Fd, D))), [(fp, t)], epi, [(_sds((T, Fd), BF16), t)], None, b_slice=("rows", tn), comm=comm)
    return (r[0][0], r[1]) if comm else r[0]


def _bwd_norm(name, dy, w, x, g, dres, tm, tk, colsum=False, comm=None):
    T, K = dy.shape
    D = w.shape[0]
    nk = K // tk

    def a_fn(a_refs, out_refs, i, j, k):
        a = a_refs[0][...]
        if colsum:
            s = jnp.sum(a.astype(F32), axis=0, keepdims=True)

            @pl.when(i == 0)
            def _():
                out_refs[3][k] = s

            @pl.when(i > 0)
            def _():
                out_refs[3][k] += s
        return a

    def epi(acc, e_refs, out_refs, i, j, extra):
        xf, gv = e_refs[0][...], e_refs[1][...]
        r = lax.rsqrt(jnp.mean(xf * xf, axis=-1, keepdims=True) + EPS)
        dx, dg_rows = _rms_bwd(xf, gv, r, acc)
        dx = dx + e_refs[2][...]
        out_refs[0][...] = dx
        out_refs[1][...] = dx.astype(BF16)
        _acc_rows(out_refs[2], jnp.sum(dg_rows, axis=0, keepdims=True), i == 0)

    t = _bs((tm, D), lambda i, j, k: (i, 0))
    row = _bs((1, D), lambda i, j, k: (0, 0))
    outs = [(_sds((T, D), F32), t), (_sds((T, D), BF16), t), (_sds((1, D), F32), row)]
    if colsum:
        outs.append((_sds((nk, 1, tk), F32), _bs((nk, 1, tk), lambda i, j, k: (0, 0, 0))))
    return _mm(name, "nt", (T // tm, 1, nk), [(dy, _bs((tm, tk), lambda i, j, k: (i, k)))], a_fn,
               (w, _resident((D, K)) if nk == 1 else _bs((D, tk), lambda i, j, k: (0, k))),
               [(x, t), (g, row), (dres, t)], epi, outs, (tm, D), comm=comm)


def _bwd_dm(dxb, w_o, ya, yb, z, sa_blk, tm):
    T, D = dxb.shape

    def epi(acc, e_refs, out_refs, i, j, extra):
        ya_, yb_, sa_, sb_ = (r[...].astype(F32) for r in e_refs)
        ga, gb = _sig(sa_), _sig(sb_)
        out_refs[0][...] = (acc * ga).astype(BF16)
        out_refs[1][...] = (acc * gb).astype(BF16)
        stage, sem = extra
        put = pltpu.make_async_copy(
            stage, out_refs[2].at[pl.ds(pl.multiple_of(i * tm, tm), tm), pl.ds(sa_blk * D, 2 * D)], sem)

        @pl.when(i > 0)
        def _():
            put.wait()

        stage[:, 0:D] = (acc * ya_ * ga * (1.0 - ga)).astype(BF16)
        stage[:, D:2 * D] = (acc * yb_ * gb * (1.0 - gb)).astype(BF16)
        put.start()

        @pl.when(i == T // tm - 1)
        def _():
            put.wait()

    t = _bs((tm, D), lambda i, j, k: (i, 0))
    return _mm("bwd_dm", "nt", (T // tm, 1, 1), [(dxb, t)], _ident, (w_o, _bs((D, D), lambda i, j, k: (0, 0))),
               [(ya, t), (yb, t), (z, _bs((tm, D), lambda i, j, k: (i, sa_blk))),
                (z, _bs((tm, D), lambda i, j, k: (i, sa_blk + 1)))], epi,
               [(_sds((T, D), BF16), t), (_sds((T, D), BF16), t),
                (_sds(z.shape, BF16), pl.BlockSpec(memory_space=pl.ANY))], None,
               extra_scratch=[pltpu.VMEM((tm, 2 * D), BF16), pltpu.SemaphoreType.DMA(())])


def _bwd_du3(dya, w, u1, ln_g, ln_b, tm):
    T, D = dya.shape
    C = w.shape[0]

    def epi(acc, e_refs, out_refs, i, j, extra):
        gv = e_refs[1][...]
        _, uh, rstd, u2, s = _ln_silu(e_refs[0][...].astype(F32), gv, e_refs[2][...])
        du2 = acc * (s * (1.0 + u2 * (1.0 - s)))
        duh = du2 * gv
        out_refs[0][...] = rstd * (duh - jnp.mean(duh, axis=-1, keepdims=True)
                                   - uh * jnp.mean(duh * uh, axis=-1, keepdims=True))
        _acc_rows(out_refs[1], jnp.sum(du2 * uh, axis=0, keepdims=True), i == 0)
        _acc_rows(out_refs[2], jnp.sum(du2, axis=0, keepdims=True), i == 0)

    t = _bs((tm, C), lambda i, j, k: (i, 0))
    row = _bs((1, C), lambda i, j, k: (0, 0))
    return _mm("bwd_du3", "nt", (T // tm, 1, 1), [(dya, _bs((tm, D), lambda i, j, k: (i, 0)))], _ident,
               (w, _bs((C, D), lambda i, j, k: (0, 0))), [(u1, t), (ln_g, row), (ln_b, row)], epi,
               [(_sds((T, C), F32), t), (_sds((1, C), F32), row), (_sds((1, C), F32), row)], None)


def _bwd_dp(dyb, w, h, z, dz, gb_blk, tm, tn, comm=None):
    T, D = dyb.shape
    R = w.shape[0]

    def epi(acc, e_refs, out_refs, i, j, extra):
        gbv = e_refs[1][...].astype(F32)
        ge, th = _gelu(gbv)
        out_refs[0][...] = acc * ge
        out_refs[1][...] = (acc * e_refs[0][...].astype(F32) * _gelu_grad(gbv, th)).astype(BF16)

    t = _bs((tm, tn), lambda i, j, k: (i, j))
    tz = _bs((tm, tn), lambda i, j, k: (i, gb_blk + j))
    return _mm("bwd_dp", "nt", (T // tm, R // tn, 1), [(dyb, _bs((tm, D), lambda i, j, k: (i, 0)))], _ident,
               (w, _bs((tn, D), lambda i, j, k: (j, 0))), [(h, t), (z, tz)], epi,
               [(_sds((T, R), F32), t), (_sds(dz.shape, BF16), tz)], None, cache_a=None, alias=[(dz, 1)], comm=comm)


CONV_ROWS = 32


def _shifted_taps(x, halo, shifts, fn):
    n = CONV_ROWS + halo
    by_r = {}
    for k, s in shifts:
        by_r.setdefault(s % 8, []).append((k, s))
    for r in sorted(by_r):
        xr = x if r == 0 else pltpu.roll(x, n - r, 0)
        for k, s in by_r[r]:
            q = s - r
            fn(k, xr[q:q + CONV_ROWS])


def _conv_fwd(name, z, blk0, gate_blk0, w_pad, bias, taps, seq, tc, out_dtype, comm=None):
    T = z.shape[0]
    C = w_pad.shape[1]
    nb, nj = T // seq, C // tc
    pad = 8 * ((taps - 1 + 7) // 8)
    halo = pad
    shifts = [(k, pad - (taps - 1) + k) for k in range(taps)]
    glu = gate_blk0 is not None

    def body(*refs):
        if glu:
            v_ref, g_ref, w_ref, b_ref, o_ref, p_ref = refs
        else:
            v_ref, w_ref, b_ref, o_ref, p_ref = refs
        p_ref[pl.ds(0, pad), :] = jnp.zeros((pad, tc), F32)
        u = v_ref[...].astype(F32)
        if glu:
            u = u * _sig(g_ref[...].astype(F32))
        p_ref[pl.ds(pad, seq), :] = u

        def step(c, _):
            base = pl.multiple_of(c * CONV_ROWS, CONV_ROWS)
            x = p_ref[pl.ds(base, CONV_ROWS + halo), :]
            acc = [jnp.zeros((CONV_ROWS, tc), F32) + b_ref[...]]

            def tap(k, xs):
                acc[0] = acc[0] + w_ref[k:k + 1, :] * xs

            _shifted_taps(x, halo, shifts, tap)
            o_ref[pl.ds(base, CONV_ROWS), :] = acc[0].astype(out_dtype)
            return 0

        lax.fori_loop(0, seq // CONV_ROWS, step, 0)

    zin = [(z, _bs((seq, tc), lambda b, j: (b, blk0 + j)))]
    if glu:
        zin.append((z, _bs((seq, tc), lambda b, j: (b, gate_blk0 + j))))
    ins = zin + [(w_pad, _bs((w_pad.shape[0], tc), lambda b, j: (0, j))), (bias, _bs((1, tc), lambda b, j: (0, j)))]
    r = _call_with_comm(name, body, (nb, nj), [a for a, _ in ins], [s for _, s in ins],
                        [_bs((seq, tc), lambda b, j: (b, j))], [_sds((T, C), out_dtype)],
                        [pltpu.VMEM((seq + pad, tc), F32)], comm)
    return (r[0][0], r[1]) if comm else r[0]


def _conv_bwd(name, dy, z, dz, blk0, gate_blk0, w_pad, taps, seq, tc, comm=None):
    T = z.shape[0]
    C = w_pad.shape[1]
    nb, nj = T // seq, C // tc
    kp = w_pad.shape[0]
    pad = 8 * ((taps - 1 + 7) // 8)
    halo = pad
    sh_du = [(k, taps - 1 - k) for k in range(taps)]
    sh_dw = [(k, pad - (taps - 1) + k) for k in range(taps)]
    glu = gate_blk0 is not None

    def body(*refs):
        if glu:
            dy_ref, v_ref, g_ref, w_ref, _dz_in, dz_out, dw_ref, db_ref, pdy, pu, du_s, wacc, ob, ob2, osem = refs
        else:
            dy_ref, v_ref, w_ref, _dz_in, dz_out, dw_ref, db_ref, pdy, pu, du_s, wacc, ob, osem = refs
        j = pl.program_id(0)
        b = pl.program_id(1)
        pdy[pl.ds(seq, pad), :] = jnp.zeros((pad, tc), F32)
        pdy[pl.ds(0, seq), :] = dy_ref[...].astype(F32)
        pu[pl.ds(0, pad), :] = jnp.zeros((pad, tc), F32)
        v = v_ref[...].astype(F32)
        if glu:
            sg = _sig(g_ref[...].astype(F32))
            pu[pl.ds(pad, seq), :] = v * sg
        else:
            pu[pl.ds(pad, seq), :] = v
        wacc[...] = jnp.zeros(wacc.shape, F32)

        def step(c, dbacc):
            base = pl.multiple_of(c * CONV_ROWS, CONV_ROWS)
            xdy = pdy[pl.ds(base, CONV_ROWS + halo), :]
            acc = [jnp.zeros((CONV_ROWS, tc), F32)]

            def tap(k, xs):
                acc[0] = acc[0] + w_ref[k:k + 1, :] * xs

            _shifted_taps(xdy, halo, sh_du, tap)
            du_s[pl.ds(base, CONV_ROWS), :] = acc[0]
            dyc = xdy[0:CONV_ROWS]
            xu = pu[pl.ds(base, CONV_ROWS + halo), :]

            def wtap(k, xs):
                p = dyc * xs
                s8 = p[0:8]
                for m in range(1, CONV_ROWS // 8):
                    s8 = s8 + p[8 * m:8 * m + 8]
                wacc[pl.ds(8 * k, 8), :] += s8

            _shifted_taps(xu, halo, sh_dw, wtap)
            d8 = dyc[0:8]
            for m in range(1, CONV_ROWS // 8):
                d8 = d8 + dyc[8 * m:8 * m + 8]
            return dbacc + d8

        dbacc = lax.fori_loop(0, seq // CONV_ROWS, step, jnp.zeros((8, tc), F32))
        du = du_s[...]
        rows = pl.ds(pl.multiple_of(b * seq, seq), seq)
        puts = [pltpu.make_async_copy(ob, dz_out.at[rows, pl.ds(pl.multiple_of((blk0 + j) * tc, tc), tc)], osem.at[0])]
        if glu:
            puts.append(pltpu.make_async_copy(
                ob2, dz_out.at[rows, pl.ds(pl.multiple_of((gate_blk0 + j) * tc, tc), tc)], osem.at[1]))

        @pl.when((j > 0) | (b > 0))
        def _():
            for cp in puts:
                cp.wait()

        if glu:
            ob[...] = (du * sg).astype(BF16)
            ob2[...] = (du * v * sg * (1.0 - sg)).astype(BF16)
        else:
            ob[...] = du.astype(BF16)
        for cp in puts:
            cp.start()

        @pl.when((j == nj - 1) & (b == nb - 1))
        def _():
            for cp in puts:
                cp.wait()
        dw = jnp.sum(wacc[...].reshape(kp, 8, tc), axis=1)
        _acc_rows(dw_ref, dw, b == 0)
        _acc_rows(db_ref, jnp.sum(dbacc, axis=0, keepdims=True), b == 0)

    zin = [(z, _bs((seq, tc), lambda j, b: (b, blk0 + j)))]
    if glu:
        zin.append((z, _bs((seq, tc), lambda j, b: (b, gate_blk0 + j))))
    ins = [(dy, _bs((seq, tc), lambda j, b: (b, j)))] + zin + [(w_pad, _bs((kp, tc), lambda j, b: (0, j))),
                                                               (dz, pl.BlockSpec(memory_space=pl.ANY))]
    dz_idx = len(ins) - 1
    out_specs = [pl.BlockSpec(memory_space=pl.ANY), _bs((kp, tc), lambda j, b: (0, j)), _bs((1, tc), lambda j, b: (0, j))]
    out_shape = [_sds(dz.shape, dz.dtype), _sds((kp, C), F32), _sds((1, C), F32)]
    stage = [pltpu.VMEM((seq, tc), BF16)] * (2 if glu else 1) + [pltpu.SemaphoreType.DMA((2,))]
    return _call_with_comm(
        name, body, (nj, nb), [a for a, _ in ins], [s for _, s in ins], out_specs, out_shape,
        [pltpu.VMEM((seq + pad, tc), F32), pltpu.VMEM((seq + pad, tc), F32),
         pltpu.VMEM((seq, tc), F32), pltpu.VMEM((8 * kp, tc), F32)] + stage, comm, aliases={dz_idx: 0})


RG_ROWS = 256


def _softplus_neg(lam):
    return jnp.maximum(-lam, 0.0) + jnp.log(1.0 + jnp.exp(-jnp.abs(lam)))


def _gates(v0c, wa_ref, wx_ref, ba, bx, sp):
    vb = v0c.astype(BF16)
    r = _sig(jnp.dot(vb, wa_ref[...], preferred_element_type=F32) + ba)
    i = _sig(jnp.dot(vb, wx_ref[...], preferred_element_type=F32) + bx)
    return r, i, -LRU_C * r * sp


def _decay(la, first_row):
    a = jnp.exp(la)
    a2 = a * a
    x = 2.0 * la
    series = -x * (1.0 + x * (0.5 + x * (1.0 / 6)))
    mult = jnp.sqrt(jnp.where(x > -0.01, series, 1.0 - a2))
    dmult = jnp.where(first_row, 0.0, -a2 / mult)
    mult = jnp.where(first_row, 1.0, mult)
    return a, mult, dmult


def _group_scan(a, b, reverse):
    n = a.shape[0]
    row = lax.broadcasted_iota(jnp.int32, a.shape, 0) & 7
    for d in (1, 2, 4):
        sh = n - d if reverse else d
        a_s, b_s = pltpu.roll(a, sh, 0), pltpu.roll(b, sh, 0)
        m = (row < 8 - d) if reverse else (row >= d)
        b = jnp.where(m, a * b_s + b, b)
        a = jnp.where(m, a * a_s, a)
    return a, b


def _group_carry(a_s, b_s, o_s, n_groups, reverse):
    cols = a_s.shape[1]

    def step(g, carry):
        g = n_groups - 1 - g if reverse else g
        rows = pl.ds(pl.multiple_of(g * 8, 8), 8)
        o = a_s[rows, :] * carry + b_s[rows, :]
        o_s[rows, :] = o
        return o[0:1, :] if reverse else o[7:8, :]

    lax.fori_loop(0, n_groups, step, jnp.zeros((1, cols), F32), unroll=2)


def _rglru_fwd(v0, wa, wx, ba, bx, lam, seq, comm=None):
    T, C = v0.shape
    ng, G = wa.shape[0], wa.shape[1]
    nb = T // seq

    def body(v_ref, wa_ref, wx_ref, ba_ref, bx_ref, lam_ref, h_ref, r_ref, i_ref, la_ref, a_s, b_s, h_s):
        sp = _softplus_neg(lam_ref[...])

        def chunk(c, _):
            rows = pl.ds(pl.multiple_of(c * RG_ROWS, RG_ROWS), RG_ROWS)
            t = lax.broadcasted_iota(jnp.int32, (RG_ROWS, G), 0) + c * RG_ROWS
            v0c = v_ref[rows, :]
            r, i, la = _gates(v0c, wa_ref, wx_ref, ba_ref[...], bx_ref[...], sp)
            r_ref[rows, :] = r.astype(BF16)
            i_ref[rows, :] = i.astype(BF16)
            la_ref[rows, :] = la
            a, mult, _ = _decay(la, t == 0)
            a_g, b_g = _group_scan(a, mult * i * v0c, False)
            a_s[rows, :] = a_g
            b_s[rows, :] = b_g
            return 0

        lax.fori_loop(0, seq // RG_ROWS, chunk, 0)
        _group_carry(a_s, b_s, h_s, seq // 8, False)
        h_ref[...] = h_s[...].astype(BF16)

    t2 = _bs((seq, G), lambda b, g: (b, g))
    wsp = _bs((None, G, G), lambda b, g: (g, 0, 0))
    row = _bs((1, G), lambda b, g: (0, g))
    return _call_with_comm("rglru_fwd", body, (nb, ng), [v0, wa, wx, ba, bx, lam], [t2, wsp, wsp, row, row, row],
                           [t2] * 4, [_sds((T, C), BF16)] * 3 + [_sds((T, C), F32)], [pltpu.VMEM((seq, G), F32)] * 3, comm)


def _call_with_comm(name, body, grid, ins, in_specs, out_specs, out_shape, scratch, comm, aliases=None):
    n_in, n_out, n_s = len(ins), len(out_shape), len(scratch)
    c_ins, c_outs, c_sems = (comm.ins, comm.outs, comm.sems) if comm else ([], [], [])

    def wrapped(*refs):
        o0 = n_in + len(c_ins)
        s0 = o0 + n_out + len(c_outs)
        cin, cout, csem = refs[n_in:o0], refs[o0 + n_out:s0], refs[s0 + n_s:]
        ids = [pl.program_id(a) for a in range(len(grid))]
        if comm:
            first = _all_of([i == 0 for i in ids])

            @pl.when(first)
            def _():
                comm.start(cin, cout, csem)

        body(*refs[:n_in], *refs[o0:o0 + n_out], *refs[s0:s0 + n_s])
        if comm:
            last = _all_of([i == n - 1 for i, n in zip(ids, grid)])

            @pl.when(last)
            def _():
                comm.finish(cin, cout, csem)

    res = pl.pallas_call(
        wrapped, name=name, grid=grid, in_specs=list(in_specs) + [ANY] * len(c_ins),
        out_specs=list(out_specs) + [ANY] * len(c_outs), out_shape=list(out_shape) + list(c_outs),
        scratch_shapes=list(scratch) + list(c_sems), input_output_aliases=aliases or {},
        compiler_params=_cp(dimension_semantics=("arbitrary",) * len(grid), has_side_effects=bool(comm)),
    )(*ins, *c_ins)
    return (list(res[:n_out]), list(res[n_out:])) if comm else list(res)


def _all_of(conds):
    out = conds[0]
    for c in conds[1:]:
        out = out & c
    return out


def _rglru_bwd(v0, h, dh, r_g, i_g, la_g, wa, wx, lam, seq, comm=None):
    T, C = v0.shape
    ng, G = wa.shape[0], wa.shape[1]
    nb = T // seq
    R = RG_ROWS

    def body(v_ref, h_ref, dh_ref, r_ref, i_ref, la_ref, wa_ref, wx_ref, lam_ref,
             dv_ref, dwa_ref, dwx_ref, dba_ref, dbx_ref, dlam_ref, a_s, b_s, q_s, hp_s):
        b = pl.program_id(1)
        lam_v = lam_ref[...]
        sp = _softplus_neg(lam_v)
        dsp_dlam = -_sig(-lam_v)

        @pl.when(b == 0)
        def _():
            dwa_ref[...] = jnp.zeros((G, G), F32)
            dwx_ref[...] = jnp.zeros((G, G), F32)
            dba_ref[...] = jnp.zeros((1, G), F32)
            dbx_ref[...] = jnp.zeros((1, G), F32)
            dlam_ref[...] = jnp.zeros((1, G), F32)

        hp_s[pl.ds(0, 8), :] = jnp.zeros((8, G), F32)
        hp_s[pl.ds(8, seq), :] = h_ref[...].astype(F32)
        q_s[pl.ds(seq, 8), :] = jnp.zeros((8, G), F32)

        def chunk1(c, _):
            rows = pl.ds(pl.multiple_of(c * R, R), R)
            a = jnp.exp(la_ref[rows, :])
            a_g, b_g = _group_scan(a, a * dh_ref[rows, :].astype(F32), True)
            a_s[rows, :] = a_g
            b_s[rows, :] = b_g
            return 0

        lax.fori_loop(0, seq // R, chunk1, 0)
        _group_carry(a_s, b_s, q_s, seq // 8, True)

        def chunk3(c, _):
            base = pl.multiple_of(c * R, R)
            rows = pl.ds(base, R)
            t = lax.broadcasted_iota(jnp.int32, (R, G), 0) + c * R
            v0c = v_ref[rows, :]
            r, i = r_ref[rows, :].astype(F32), i_ref[rows, :].astype(F32)
            a, mult, dmult_dla = _decay(la_ref[rows, :], t == 0)
            q_next = pltpu.roll(q_s[pl.ds(base, R + 8), :], R + 7, 0)[0:R]
            h_prev = pltpu.roll(hp_s[pl.ds(base, R + 8), :], R + 1, 0)[0:R]
            gt = dh_ref[rows, :].astype(F32) + q_next
            dla = gt * h_prev * a + gt * i * v0c * dmult_dla
            dpa = dla * (-LRU_C * sp) * r * (1.0 - r)
            dpx = gt * mult * v0c * i * (1.0 - i)
            dpa_b, dpx_b, v_b = dpa.astype(BF16), dpx.astype(BF16), v0c.astype(BF16)
            dv_ref[rows, :] = (gt * mult * i
                               + lax.dot_general(dpa_b, wa_ref[...], _DIMS["nt"], preferred_element_type=F32)
                               + lax.dot_general(dpx_b, wx_ref[...], _DIMS["nt"], preferred_element_type=F32))
            dwa_ref[...] += lax.dot_general(v_b, dpa_b, _DIMS["tn"], preferred_element_type=F32)
            dwx_ref[...] += lax.dot_general(v_b, dpx_b, _DIMS["tn"], preferred_element_type=F32)
            dba_ref[...] += jnp.sum(dpa, axis=0, keepdims=True)
            dbx_ref[...] += jnp.sum(dpx, axis=0, keepdims=True)
            dlam_ref[...] += jnp.sum(dla * (-LRU_C * r), axis=0, keepdims=True) * dsp_dlam
            return 0

        lax.fori_loop(0, seq // R, chunk3, 0)

    t2 = _bs((seq, G), lambda g, b: (b, g))
    wsp = _bs((None, G, G), lambda g, b: (g, 0, 0))
    row = _bs((1, G), lambda g, b: (0, g))
    return _call_with_comm(
        "rglru_bwd", body, (ng, nb), [v0, h, dh, r_g, i_g, la_g, wa, wx, lam], [t2] * 6 + [wsp, wsp, row],
        [t2, wsp, wsp, row, row, row],
        [_sds((T, C), F32), _sds((ng, G, G), F32), _sds((ng, G, G), F32),
         _sds((1, C), F32), _sds((1, C), F32), _sds((1, C), F32)],
        [pltpu.VMEM((seq, G), F32), pltpu.VMEM((seq, G), F32),
         pltpu.VMEM((seq + 8, G), F32), pltpu.VMEM((seq + 8, G), F32)], comm)


TC_A = 256
TC_B = 512
TAPS_A, TAPS_B = 31, 4


def _tiles(T):
    return min(512, T), min(1024, T)


GATHERED = ("w_in", "w_1", "w_a_out", "w_b_out", "w_o", "w_2", "caw", "cbw")
GATHER_KIND = {"w_in": (True, True), "w_1": (True, True), "w_a_out": (False, True), "w_b_out": (False, True),
               "w_o": (False, True), "w_2": (False, True), "caw": (True, False), "cbw": (True, False)}


def _layer_fwd(x, p, seq, jobs=None):
    T, D = x.shape
    C, R = p["ln_g"].shape[1], p["lam"].shape[1]
    tm, tl = _tiles(T)
    gb_blk, sa_blk = (2 * C + R) // TC_B, (2 * C + 2 * R) // D
    p, ahead, jobs = dict(p), {}, jobs or {}

    def gather(call):
        js = jobs.get(call)
        return _gather_comm([s for _, s, _ in js], [GATHER_KIND[n] for n, _, _ in js]) if js else None

    def outs(r, call):
        js = jobs.get(call)
        if not js:
            return r
        for (n, _, for_next), whole in zip(js, r[1]):
            (ahead if for_next else p)[n] = whole
        return r[0]

    z, h = outs(_fwd_norm_mm("fwd_z", x, p["g_mix"], p["w_in"], p["b_in"], tl, 1024, comm=gather("fwd_z")), "fwd_z")
    u1 = outs(_conv_fwd("conv_a_fwd", z, 0, C // TC_A, p["caw"], p["cab"], TAPS_A, seq, TC_A, BF16,
                        comm=gather("conv_a_fwd")), "conv_a_fwd")
    ya, u3 = _fwd_ya(u1, p["ln_g"], p["ln_b"], p["w_a_out"], tm)
    v0 = _conv_fwd("conv_b_fwd", z, 2 * C // TC_B, None, p["cbw"], p["cbb"], TAPS_B, seq, TC_B, F32)
    hr, rg, ig, lag = outs(_rglru_fwd(v0, p["wa"], p["wx"], p["b_rg_a"], p["b_rg_x"], p["lam"], seq,
                                      comm=gather("rglru_fwd")), "rglru_fwd")
    yb, pb = _fwd_yb(hr, z, gb_blk, p["w_b_out"], tl, TC_B)
    x1, mg = outs(_fwd_x1(x, ya, yb, z, sa_blk, p["w_o"], tm, comm=gather("fwd_x1")), "fwd_x1")
    fp, h2 = outs(_fwd_norm_mm("fwd_f", x1, p["g_mlp"], p["w_1"], None, tl, 1024, comm=gather("fwd_f")), "fwd_f")
    x2 = outs(_fwd_x2(x1, fp, p["w_2"], tm, fp.shape[1], comm=gather("fwd_x2")), "fwd_x2")
    saved = dict(x=x, z=z, h=h, u1=u1, u3=u3, ya=ya, v0=v0, hr=hr, rg=rg, ig=ig, lag=lag, pb=pb, yb=yb, mg=mg, x1=x1,
                 fp=fp, h2=h2)
    return x2, saved, p, ahead


class _Reduce:
    EARLY = ("w_2", "w_1", "w_o", "w_a_out")
    LATE = ("w_b_out", "w_in")

    def __init__(self, accs, c_arr, kcl_of):
        self.accs, self.c_arr, self.kcl_of, self.late = accs, c_arr, kcl_of, None

    @staticmethod
    def pieces(partials):
        return [a if a.ndim == 4 else a.reshape(N_CHIPS, 2, a.shape[0] // (2 * N_CHIPS), a.shape[1]) for a in partials]

    def chip_sums(self, pgs, swapped):
        return _sum_siblings(pgs, swapped, self.c_arr)

    def finish(self, names, sums, received, layer):
        done = _sum_chips(sums, received, self.kcl_of(layer), [self.accs[n] for n in names])
        self.accs.update(zip(names, done))


def _layer_bwd(dx2, dx2b, p, s, seq, red=None, layer=0):
    T, D = dx2.shape
    C, R = p["ln_g"].shape[1], p["lam"].shape[1]
    tm, tl = _tiles(T)
    gb_blk, sa_blk = (2 * C + R) // TC_B, (2 * C + 2 * R) // D
    z = s["z"]
    g = {}


    late_sums = None
    if red is not None and red.late is not None:
        late, red.late = red.late, None
        dfp, got = _bwd_df(dx2b, p["w_2"], s["fp"], tl, 1024, comm=_swap_comm(late))
        late_sums = red.chip_sums(late, got)
    else:
        dfp = _bwd_df(dx2b, p["w_2"], s["fp"], tl, 1024)
    g["w_2"] = _bwd_dw("bwd_dw2", s["fp"], dx2b, 1024, D, T, a_fn=_relu2, keep="dy")
    dx1, dx1b, g["g_mlp"] = _bwd_norm("bwd_dh2", dfp, p["w_1"], s["x1"], p["g_mlp"], dx2, tm, dfp.shape[1])
    g["w_1"] = _bwd_dw("bwd_dw1", s["h2"], dfp, D, 1024, T, shard_cols=dfp.shape[1] // N_CHIPS, keep="act")

    dya, dyb, dz = _bwd_dm(dx1b, p["w_o"], s["ya"], s["yb"], z, sa_blk, tm)

    g["w_o"] = _bwd_dw("bwd_dwo", s["mg"], dx1b, D, D, tl)
    du1, g["ln_g"], g["ln_b"] = _bwd_du3(dya, p["w_a_out"], s["u1"], p["ln_g"], p["ln_b"], tm)
    g["w_a_out"] = _bwd_dw("bwd_dwa", s["u3"], dya, C, D, tl)
    conv_a_args = ("conv_a_bwd", du1, z, dz, 0, C // TC_A, p["caw"], TAPS_A, seq, TC_A)
    if late_sums is not None:
        (dz, g["caw"], g["cab"]), got = _conv_bwd(*conv_a_args, comm=_scatter_comm(late_sums))
        red.finish(red.LATE, late_sums, got, layer + 1)
    else:
        dz, g["caw"], g["cab"] = _conv_bwd(*conv_a_args)

    dp_args = (dyb, p["w_b_out"], s["hr"], z, dz, gb_blk, tl, TC_B)
    if red is not None:
        early = red.pieces([g.pop(n) for n in red.EARLY])
        (dhr, dz), got = _bwd_dp(*dp_args, comm=_swap_comm(early))
        early_sums = red.chip_sums(early, got)
    else:
        dhr, dz = _bwd_dp(*dp_args)

    g["w_b_out"] = _bwd_dw("bwd_dwb", s["pb"], dyb, R, D, tl)
    rg_args = (s["v0"], s["hr"], dhr, s["rg"], s["ig"], s["lag"], p["wa"], p["wx"], p["lam"], seq)
    if red is not None:
        rg_out, got = _rglru_bwd(*rg_args, comm=_scatter_comm(early_sums))
        red.finish(red.EARLY, early_sums, got, layer)
    else:
        rg_out = _rglru_bwd(*rg_args)
    dv0, g["wa"], g["wx"], g["b_rg_a"], g["b_rg_x"], g["lam"] = rg_out
    dz, g["cbw"], g["cbb"] = _conv_bwd("conv_b_bwd", dv0, z, dz, 2 * C // TC_B, None, p["cbw"], TAPS_B, seq, TC_B)

    dx, dxb, g["g_mix"], dbin = _bwd_norm("bwd_dh", dz, p["w_in"], s["x"], p["g_mix"], dx1, tm, dz.shape[1],
                                          colsum=True)
    g["b_in"] = dbin.reshape(1, -1)
    ns = dz.shape[1] // N_CHIPS
    g["w_in"] = _bwd_dw("bwd_dwin", s["h"], dz, D, ns // 2, T, shard_cols=ns, keep="act")
    if red is not None:
        red.late = red.pieces([g.pop(n) for n in red.LATE])
    return dx, dxb, g


ANY = pl.BlockSpec(memory_space=pl.ANY)


def _mesh_pos():
    return lax.axis_index("x"), lax.axis_index("y"), lax.axis_index("c")


def _other_chips(x, y):
    return [(1 - x, y), (x, 1 - y), (1 - x, 1 - y)]


def _remote(src, dst, ssem, rsem, dev):
    return pltpu.make_async_remote_copy(src_ref=src, dst_ref=dst, send_sem=ssem, recv_sem=rsem,
                                        device_id=dev, device_id_type=MESH)


def _gather_region(src, dst, by_cols, k, half):
    rows, cols = src.shape
    nr = rows if half is None else rows // 2
    r0 = 0 if half is None else half * nr
    if by_cols:
        return dst.at[pl.ds(r0, nr), pl.ds(pl.multiple_of(k * cols, 128), cols)]
    return dst.at[pl.ds(pl.multiple_of(k * rows + r0, 8), nr), :]


def _gather_sends(src, dst, kinds, send, recv):
    x, y, c = _mesh_pos()
    cps = []
    for t in range(len(src)):
        half = c if kinds[t][1] else None
        hr = src[t].shape[0] // 2
        s_ref = src[t].at[pl.ds(c * hr, hr), :] if kinds[t][1] else src[t]
        for j, chip in enumerate(_other_chips(x, y)):
            cps.append(_remote(s_ref, _gather_region(src[t], dst[t], kinds[t][0], 2 * x + y, half),
                               send.at[t, j], recv.at[t, j], (*chip, c)))
    return cps


def _gather_finish(src, dst, kinds, send, recv, fsend, frecv):
    x, y, c = _mesh_pos()
    chips = _other_chips(x, y)
    sib = (x, y, 1 - c)
    n = len(src)
    fwd = []
    for t in range(n):
        half = c if kinds[t][1] else None
        for j, chip in enumerate(chips):
            got = _gather_region(src[t], dst[t], kinds[t][0], 2 * chip[0] + chip[1], half)
            _remote(got, got, send.at[t, j], recv.at[t, j], (*chip, c)).wait_recv()
            if kinds[t][1]:
                cp = _remote(got, got, fsend.at[t, j], frecv.at[t, j], sib)
                cp.start()
                fwd.append(cp)
    for t in range(n):
        if kinds[t][1]:
            for j, chip in enumerate(chips):
                got = _gather_region(src[t], dst[t], kinds[t][0], 2 * chip[0] + chip[1], 1 - c)
                _remote(got, got, fsend.at[t, j], frecv.at[t, j], sib).wait_recv()
    for cp in _gather_sends(src, dst, kinds, send, recv) + fwd:
        cp.wait_send()


def _gather_sems(n):
    sem = pltpu.SemaphoreType.DMA
    return [sem((n, 3)), sem((n, 3)), sem((n, 3)), sem((n, 3))]


def _gather_comm(shards, kinds):
    n = len(shards)

    def whole(s, by_cols):
        return (s.shape[0], N_CHIPS * s.shape[1]) if by_cols else (N_CHIPS * s.shape[0], s.shape[1])

    def own(src, dst, lsem):
        x, y, _ = _mesh_pos()
        return [pltpu.make_async_copy(src[t], _gather_region(src[t], dst[t], kinds[t][0], 2 * x + y, None), lsem.at[t])
                for t in range(n)]

    def start(src, dst, sems):
        for cp in own(src, dst, sems[4]) + _gather_sends(src, dst, kinds, sems[0], sems[1]):
            cp.start()

    def finish(src, dst, sems):
        _gather_finish(src, dst, kinds, *sems[:4])
        for cp in own(src, dst, sems[4]):
            cp.wait()

    return _Comm(shards, [_sds(whole(s, k[0]), s.dtype) for s, k in zip(shards, kinds)],
                 _gather_sems(n) + [pltpu.SemaphoreType.DMA((n,))], start, finish)


def _scatter_comm(ps):
    n = len(ps)

    def copies(src, dst, sems):
        x, y, c = _mesh_pos()
        return [_remote(src[t].at[2 * chip[0] + chip[1]], dst[t].at[j], sems[0].at[t, j], sems[1].at[t, j], (*chip, c))
                for t in range(n) for j, chip in enumerate(_other_chips(x, y))]

    def start(src, dst, sems):
        for cp in copies(src, dst, sems):
            cp.start()

    def finish(src, dst, sems):
        cps = copies(src, dst, sems)
        for cp in cps:
            cp.wait_recv()
        for cp in cps:
            cp.wait_send()

    sem = pltpu.SemaphoreType.DMA
    return _Comm(ps, [_sds((3,) + a.shape[1:], a.dtype) for a in ps], [sem((n, 3)), sem((n, 3))], start, finish)


def _comm_call(name, comm):
    n_i, n_o = len(comm.ins), len(comm.outs)

    def body(*refs):
        comm.start(refs[:n_i], refs[n_i:n_i + n_o], refs[n_i + n_o:])
        comm.finish(refs[:n_i], refs[n_i:n_i + n_o], refs[n_i + n_o:])

    return pl.pallas_call(
        body, name=name, in_specs=[ANY] * n_i, out_specs=[ANY] * n_o, out_shape=comm.outs, scratch_shapes=comm.sems,
        compiler_params=_cp(has_side_effects=True),
    )(*comm.ins)


def _swap_comm(pgs):
    n = len(pgs)

    def copies(src, dst, sems):
        x, y, c = _mesh_pos()
        return [_remote(src[t].at[:, 1 - c], dst[t], sems[0].at[t], sems[1].at[t], (x, y, 1 - c)) for t in range(n)]

    def start(src, dst, sems):
        for cp in copies(src, dst, sems):
            cp.start()

    def finish(src, dst, sems):
        cps = copies(src, dst, sems)
        for cp in cps:
            cp.wait_recv()
        for cp in cps:
            cp.wait_send()

    sem = pltpu.SemaphoreType.DMA
    return _Comm(pgs, [_sds((a.shape[0],) + a.shape[2:], a.dtype) for a in pgs], [sem((n,)), sem((n,))], start, finish)


def _join_halves(accs, also=None):
    n = len(accs)
    c_ins, c_outs, c_sems = (also.ins, also.outs, also.sems) if also else ([], [], [])

    def body(*refs):
        o0 = n + len(c_ins)
        buf = refs[o0:o0 + n]
        send, recv = refs[o0 + n + len(c_outs):o0 + n + len(c_outs) + 2]
        extra = (refs[n:o0], refs[o0 + n:o0 + n + len(c_outs)], refs[o0 + n + len(c_outs) + 2:])
        x, y, c = _mesh_pos()
        if also:
            also.start(*extra)
        cps = [_remote(buf[t].at[:, c], buf[t].at[:, c], send.at[t], recv.at[t], (x, y, 1 - c)) for t in range(n)]
        for cp in cps:
            cp.start()
        for t in range(n):
            _remote(buf[t].at[:, c], buf[t].at[:, 1 - c], send.at[t], recv.at[t], (x, y, 1 - c)).wait_recv()
        for cp in cps:
            cp.wait_send()
        if also:
            also.finish(*extra)

    sem = pltpu.SemaphoreType.DMA
    res = pl.pallas_call(
        body, name="join_halves", in_specs=[ANY] * (n + len(c_ins)), out_specs=[ANY] * (n + len(c_outs)),
        out_shape=[_sds(a.shape, a.dtype) for a in accs] + list(c_outs),
        scratch_shapes=[sem((n,)), sem((n,))] + list(c_sems),
        input_output_aliases={t: t for t in range(n)}, compiler_params=_cp(has_side_effects=True),
    )(*accs, *c_ins)
    return (list(res[:n]), list(res[n:])) if also else res


def _sum_siblings(pgs, rbs, c_arr):
    n = len(pgs)
    nk = pgs[0].shape[0]

    def body(c_ref, *refs):
        for t in range(n):
            refs[2 * n + t][...] = (refs[t][...].astype(F32) + refs[n + t][...].astype(F32)).astype(BF16)

    half = lambda a: pl.BlockSpec((None,) + a.shape[2:], lambda k, c_ref: (k, 0, 0))
    return pl.pallas_call(
        body, name="sum_siblings",
        grid_spec=pltpu.PrefetchScalarGridSpec(
            num_scalar_prefetch=1, grid=(nk,),
            in_specs=[pl.BlockSpec((None, None) + a.shape[2:], lambda k, c_ref: (k, c_ref[0], 0, 0)) for a in pgs]
            + [half(a) for a in pgs],
            out_specs=[half(a) for a in pgs]),
        out_shape=[_sds((nk,) + a.shape[2:], BF16) for a in pgs],
        compiler_params=_cp(dimension_semantics=("arbitrary",)),
    )(c_arr, *pgs, *rbs)


def _sum_chips(ps, rbs, kcl, accs):
    n = len(ps)

    def body(k_ref, *refs):
        for t in range(n):
            b_ref = refs[n + t]
            refs[3 * n + t][...] = (refs[t][...].astype(F32) + b_ref[0].astype(F32) + b_ref[1].astype(F32)
                                    + b_ref[2].astype(F32))

    qr = lambda a: (a.shape[1] // 2, a.shape[2])
    return pl.pallas_call(
        body, name="sum_chips",
        grid_spec=pltpu.PrefetchScalarGridSpec(
            num_scalar_prefetch=1, grid=(2,),
            in_specs=[pl.BlockSpec((None,) + qr(a), lambda r, k_ref: (k_ref[0], r, 0)) for a in ps]
            + [pl.BlockSpec((3,) + qr(a), lambda r, k_ref: (0, r, 0)) for a in ps] + [ANY] * n,
            out_specs=[pl.BlockSpec((None, None) + qr(a), lambda r, k_ref: (k_ref[2], k_ref[1], r, 0)) for a in ps]),
        out_shape=[_sds(a.shape, F32) for a in accs], input_output_aliases={1 + 2 * n + t: t for t in range(n)},
        compiler_params=_cp(dimension_semantics=("arbitrary",)),
    )(kcl, *ps, *rbs, *accs)


N_DEV = 8


def _allreduce_small(parts):
    n = len(parts)

    def body(*refs):
        p_refs, o_refs, rbufs = refs[:n], refs[n:2 * n], refs[2 * n:3 * n]
        s1, r1, s2, r2 = refs[3 * n:]
        x, y, c = _mesh_pos()
        me = 4 * x + 2 * y + c
        devs = [(d // 4, (d // 2) % 2, d % 2) for d in range(N_DEV)]
        for q in range(n):
            rbufs[q][me] = p_refs[q][me]

        def each_peer(fn):
            for d in range(N_DEV):
                @pl.when(d != me)
                def _():
                    for q in range(n):
                        fn(d, q)

        def first(d, q, to_me):
            return _remote(p_refs[q].at[d], rbufs[q].at[d if to_me else me], s1.at[q, d], r1.at[q, d if to_me else me],
                           devs[d])

        def second(d, q, to_me):
            blk = d if to_me else me
            return _remote(o_refs[q].at[blk], o_refs[q].at[blk], s2.at[q, d], r2.at[q, blk], devs[d])

        each_peer(lambda d, q: first(d, q, False).start())
        each_peer(lambda d, q: first(d, q, True).wait_recv())
        for q in range(n):
            total = rbufs[q][0].astype(F32)
            for d in range(1, N_DEV):
                total = total + rbufs[q][d].astype(F32)
            o_refs[q][me] = total.astype(o_refs[q].dtype)
        each_peer(lambda d, q: second(d, q, False).start())
        each_peer(lambda d, q: second(d, q, True).wait_recv())
        each_peer(lambda d, q: first(d, q, False).wait_send())
        each_peer(lambda d, q: second(d, q, False).wait_send())

    sem = pltpu.SemaphoreType.DMA
    vm = pl.BlockSpec(memory_space=pltpu.VMEM)
    return pl.pallas_call(
        body, name="allreduce_small", in_specs=[vm] * n, out_specs=[vm] * n,
        out_shape=[_sds(a.shape, a.dtype) for a in parts],
        scratch_shapes=[pltpu.VMEM(a.shape, a.dtype) for a in parts] + [sem((n, N_DEV))] * 4,
        compiler_params=_cp(has_side_effects=True),
    )(*parts)


BIG = ("w_in", "w_1", "w_a_out", "w_b_out", "w_o", "w_2")
BY_COLS = {"w_in": True, "w_1": True, "w_a_out": False, "w_b_out": False, "w_o": False, "w_2": False}
WEIGHTS = ("g_mix", "w_in", "b_in", "conv_a_w", "conv_a_b", "ln_g", "ln_b", "w_a_out", "conv_b_w", "conv_b_b", "w_rg_a",
           "b_rg_a", "w_rg_x", "b_rg_x", "lam", "w_b_out", "w_o", "g_mlp", "w_1", "w_2", "g_final")
SMALL = tuple(n for n in WEIGHTS if n not in BIG)
ADAM_ROWS = 256
ADAM_SMALL_ROWS = 2048


def _block_diag(w):
    nh, dh, _ = w.shape
    ng = nh // HEADS_PER_GROUP
    w4 = w.reshape(ng, HEADS_PER_GROUP, dh, dh)
    eye = jnp.eye(HEADS_PER_GROUP, dtype=w.dtype)
    return jnp.einsum("qhij,hk->qhikj", w4, eye).reshape(ng, HEADS_PER_GROUP * dh, HEADS_PER_GROUP * dh)


def _block_diag_part(d, dh):
    ng = d.shape[0]
    eye = jnp.eye(HEADS_PER_GROUP, dtype=d.dtype)
    d5 = d.reshape(ng, HEADS_PER_GROUP, dh, HEADS_PER_GROUP, dh)
    return jnp.einsum("qhikj,hk->qhij", d5, eye).reshape(ng * HEADS_PER_GROUP, dh, dh)


PACK_LANES = 128


def _pack(arrays, blocks, tile_rows):
    parts = [a.reshape(-1, PACK_LANES) for a in arrays]
    parts = [jnp.pad(p, ((0, -p.shape[0] % tile_rows), (0, 0))) if p.shape[0] % tile_rows else p for p in parts]
    rows = sum(p.shape[0] for p in parts)
    pad = -rows % (blocks * tile_rows)
    if pad:
        parts.append(jnp.zeros((pad, PACK_LANES), parts[0].dtype))
    return jnp.concatenate(parts, axis=0).reshape(blocks, -1, PACK_LANES)


def _unpack(buf, like, tile_rows):
    buf = buf.reshape(-1, PACK_LANES)
    out, off = [], 0
    for a in like:
        n = a.size // PACK_LANES
        out.append(buf[off:off + n].reshape(a.shape))
        off += n + (-n % tile_rows)
    return out


def kernel(x, g_mix, w_in, b_in, conv_a_w, conv_a_b, ln_g, ln_b, w_a_out, conv_b_w, conv_b_b, w_rg_a, b_rg_a, w_rg_x, b_rg_x, lam, w_b_out, w_o, g_mlp, w_1, w_2, g_final, loss_target, m_g_mix, m_w_in, m_b_in, m_conv_a_w, m_conv_a_b, m_ln_g, m_ln_b, m_w_a_out, m_conv_b_w, m_conv_b_b, m_w_rg_a, m_b_rg_a, m_w_rg_x, m_b_rg_x, m_lam, m_w_b_out, m_w_o, m_g_mlp, m_w_1, m_w_2, m_g_final, v_g_mix, v_w_in, v_b_in, v_conv_a_w, v_conv_a_b, v_ln_g, v_ln_b, v_w_a_out, v_conv_b_w, v_conv_b_b, v_w_rg_a, v_b_rg_a, v_w_rg_x, v_b_rg_x, v_lam, v_w_b_out, v_w_o, v_g_mlp, v_w_1, v_w_2, v_g_final):
    w = dict(g_mix=g_mix, w_in=w_in, b_in=b_in, conv_a_w=conv_a_w, conv_a_b=conv_a_b, ln_g=ln_g, ln_b=ln_b, w_a_out=w_a_out,
             conv_b_w=conv_b_w, conv_b_b=conv_b_b, w_rg_a=w_rg_a, b_rg_a=b_rg_a, w_rg_x=w_rg_x, b_rg_x=b_rg_x, lam=lam,
             w_b_out=w_b_out, w_o=w_o, g_mlp=g_mlp, w_1=w_1, w_2=w_2, g_final=g_final)
    m = dict(g_mix=m_g_mix, w_in=m_w_in, b_in=m_b_in, conv_a_w=m_conv_a_w, conv_a_b=m_conv_a_b, ln_g=m_ln_g, ln_b=m_ln_b,
             w_a_out=m_w_a_out, conv_b_w=m_conv_b_w, conv_b_b=m_conv_b_b, w_rg_a=m_w_rg_a, b_rg_a=m_b_rg_a, w_rg_x=m_w_rg_x,
             b_rg_x=m_b_rg_x, lam=m_lam, w_b_out=m_w_b_out, w_o=m_w_o, g_mlp=m_g_mlp, w_1=m_w_1, w_2=m_w_2, g_final=m_g_final)
    v = dict(g_mix=v_g_mix, w_in=v_w_in, b_in=v_b_in, conv_a_w=v_conv_a_w, conv_a_b=v_conv_a_b, ln_g=v_ln_g, ln_b=v_ln_b,
             w_a_out=v_w_a_out, conv_b_w=v_conv_b_w, conv_b_b=v_conv_b_b, w_rg_a=v_w_rg_a, b_rg_a=v_b_rg_a, w_rg_x=v_w_rg_x,
             b_rg_x=v_b_rg_x, lam=v_lam, w_b_out=v_w_b_out, w_o=v_w_o, g_mlp=v_g_mlp, w_1=v_w_1, w_2=v_w_2, g_final=v_g_final)
    B, S, D = x.shape
    T = B * S
    L = w_in.shape[0]
    dh = w_rg_a.shape[-1]
    taps_a, taps_b = conv_a_w.shape[1], conv_b_w.shape[1]
    assert (taps_a, taps_b) == (TAPS_A, TAPS_B)
    xi, yi, ci = _mesh_pos()
    c_arr = jnp.reshape(ci, (1,)).astype(jnp.int32)
    k_me = 2 * xi + yi

    caw_p = jnp.pad(conv_a_w, ((0, 0), (0, 32 - taps_a), (0, 0)))
    cbw_p = jnp.pad(conv_b_w, ((0, 0), (0, 8 - taps_b), (0, 0)))
    row = lambda a: a.reshape(1, -1)

    def shards_of(l):
        d = {n: w[n][l].astype(BF16) for n in BIG}
        d.update(caw=caw_p[l], cbw=cbw_p[l])
        return d

    def params_of(l, whole):
        p = dict(whole, cab=row(conv_a_b[l]), cbb=row(conv_b_b[l]),
                 wa=_block_diag(w_rg_a[l]).astype(BF16), wx=_block_diag(w_rg_x[l]).astype(BF16))
        for n in ("g_mix", "b_in", "ln_g", "ln_b", "b_rg_a", "b_rg_x", "lam", "g_mlp"):
            p[n] = row(w[n][l])
        return p

    shards = [shards_of(l) for l in range(L)]
    whole = dict(zip(["w_in"], _comm_call("gather_first", _gather_comm([shards[0]["w_in"]], [GATHER_KIND["w_in"]]))))
    xf = x.reshape(T, D)
    saved, params = [], []
    for l in range(L):
        cur = lambda names: [(n, shards[l][n], False) for n in names]
        nxt = lambda names: [(n, shards[l + 1][n], True) for n in names]
        if l == 0:
            jobs = {"fwd_z": cur(["w_a_out", "w_b_out", "w_o", "caw", "cbw"]), "conv_a_fwd": cur(["w_1"]),
                    "fwd_f": cur(["w_2"])}
        else:
            jobs = {"fwd_z": cur(["w_1"]), "conv_a_fwd": cur(["w_2"])}
        if l + 1 < L:
            jobs.update({"rglru_fwd": nxt(["w_in"]), "fwd_x1": nxt(["w_o", "caw", "cbw"]),
                         "fwd_x2": nxt(["w_b_out", "w_a_out"])})
        xf, s, p, whole = _layer_fwd(xf, params_of(l, whole), S, jobs)
        saved.append(s)
        params.append(p)
    loss_part, dx, dxb, dg_final = _loss_head(xf, row(g_final), loss_target.reshape(T, D), _tiles(T)[0])
    loss = lax.psum(loss_part[0, 0], ("x", "y", "c"))

    half_shape = lambda a: (L, 2, a.shape[1] // 2, a.shape[2])
    accs = {n: lax.empty(half_shape(w[n]), F32) for n in BIG}
    small = {n: [None] * L for n in SMALL if n != "g_final"}
    red = _Reduce(accs, c_arr, lambda l: jnp.stack([k_me, ci, jnp.full((), l, ci.dtype)]).astype(jnp.int32))
    for l in reversed(range(L)):
        dx, dxb, g = _layer_bwd(dx, dxb, params[l], saved[l], S, red=red, layer=l)
        small["g_mix"][l], small["b_in"][l], small["g_mlp"][l] = g["g_mix"], g["b_in"], g["g_mlp"]
        small["conv_a_w"][l], small["conv_a_b"][l] = g["caw"], g["cab"]
        small["conv_b_w"][l], small["conv_b_b"][l] = g["cbw"], g["cbb"]
        small["ln_g"][l], small["ln_b"][l], small["lam"][l] = g["ln_g"], g["ln_b"], g["lam"]
        small["w_rg_a"][l], small["w_rg_x"][l] = _block_diag_part(g["wa"], dh), _block_diag_part(g["wx"], dh)
        small["b_rg_a"][l], small["b_rg_x"][l] = g["b_rg_a"], g["b_rg_x"]
    grad_x = dx.reshape(B, S, D)

    delta, new_m, new_v = {}, {}, {}
    flat = lambda a: a.reshape(-1, a.shape[-1])

    def adam_big(names, comm=None):
        r = _adamw("adamw_" + names[0], *[[flat(d[n]) for n in names] for d in (w, grads, m, v)], ADAM_ROWS, comm=comm)
        for q, n in enumerate(names):
            delta[n], new_m[n], new_v[n] = (r[a][q].reshape(w[n].shape) for a in range(3))
        return r[3] if comm else None

    late_sums = red.chip_sums(red.late, _comm_call("swap_halves", _swap_comm(red.late)))
    joined, got = _join_halves([red.accs[n] for n in red.EARLY], also=_scatter_comm(late_sums))
    grads = {n: a.reshape(w[n].shape) for n, a in zip(red.EARLY, joined)}
    adam_big(["w_2", "w_1"])
    adam_big(["w_o", "w_a_out"])
    red.finish(red.LATE, late_sums, got, 0)
    joined = _join_halves([red.accs[n] for n in red.LATE])
    grads.update({n: a.reshape(w[n].shape) for n, a in zip(red.LATE, joined)})
    adam_big(["w_b_out"])
    adam_big(["w_in"])

    wide = ["w_rg_a", "w_rg_x"]
    names = [n for n in SMALL if n != "g_final" and n not in wide]
    parts = [jnp.stack(small[n]) for n in names] + [dg_final]
    parts_w = [jnp.stack(small[n]).astype(BF16) for n in wide]
    total, total_w = _allreduce_small([_pack(parts, N_DEV, 8), _pack(parts_w, N_DEV, 16)])
    summed = _unpack(total, parts, 8) + [a.astype(F32) for a in _unpack(total_w, parts_w, 16)]
    for n, a in zip(names + ["g_final"] + wide, summed):
        if n == "conv_a_w":
            a = lax.dynamic_slice_in_dim(a[:, :taps_a], k_me * conv_a_w.shape[2], conv_a_w.shape[2], axis=2)
        elif n == "conv_b_w":
            a = lax.dynamic_slice_in_dim(a[:, :taps_b], k_me * conv_b_w.shape[2], conv_b_w.shape[2], axis=2)
        grads[n] = a.reshape(w[n].shape)

    for n in SMALL:
        cols = w[n].shape[-1]
        view = lambda a: a.reshape(-1, cols)
        rows = view(w[n]).shape[0]
        d_, m_, v_ = _adamw("adamw_" + n, view(w[n]), view(grads[n]), view(m[n]), view(v[n]),
                            ADAM_SMALL_ROWS if rows % ADAM_SMALL_ROWS == 0 else rows)
        delta[n], new_m[n], new_v[n] = (a.reshape(w[n].shape) for a in (d_, m_, v_))

    return (loss, grad_x, *[grads[n] for n in WEIGHTS], *[delta[n] for n in WEIGHTS],
            *[new_m[n] for n in WEIGHTS], *[new_v[n] for n in WEIGHTS])
```

```python
import jax
import jax.numpy as jnp
from jax import lax
from jax.experimental import pallas as pl
from jax.experimental.pallas import tpu as pltpu

F32 = jnp.float32
BF16 = jnp.bfloat16
MESH = pl.DeviceIdType.MESH

EPS = 1e-6
LRU_C = 8.0
ADAM_LR, ADAM_B1, ADAM_B2, ADAM_EPS, ADAM_WD, ADAM_STEP = 0.001, 0.9, 0.999, 1e-08, 0.01, 10

N_CHIPS = 4
HEADS_PER_GROUP = 4
VMEM_LIMIT = 56 * 1024 * 1024


def _cp(**kw):
    return pltpu.CompilerParams(vmem_limit_bytes=VMEM_LIMIT, **kw)


def _sig(x):
    return 1.0 / (1.0 + jnp.exp(-x))


def _gelu(x):
    t = jnp.tanh(0.7978845608028654 * (x + 0.044715 * x * x * x))
    return 0.5 * x * (1.0 + t), t


def _gelu_grad(x, t):
    dt = (1.0 - t * t) * 0.7978845608028654 * (1.0 + 3.0 * 0.044715 * x * x)
    return 0.5 * (1.0 + t) + 0.5 * x * dt


def _rms(xf, g):
    r = lax.rsqrt(jnp.mean(xf * xf, axis=-1, keepdims=True) + EPS)
    return xf * r * g, r


def _rms_bwd(xf, g, r, dh):
    dyg = dh * g
    dx = r * (dyg - xf * (r * r) * jnp.mean(dyg * xf, axis=-1, keepdims=True))
    return dx, dh * xf * r


def _ln_silu(u, g, b):
    mu = jnp.mean(u, axis=-1, keepdims=True)
    uc = u - mu
    rstd = lax.rsqrt(jnp.mean(uc * uc, axis=-1, keepdims=True) + EPS)
    uh = uc * rstd
    u2 = uh * g + b
    s = _sig(u2)
    return u2 * s, uh, rstd, u2, s


_DIMS = {"nn": (((1,), (0,)), ((), ())), "nt": (((1,), (1,)), ((), ())), "tn": (((0,), (0,)), ((), ()))}


PEER_SETS = {"chips+sibling": 1, "chips": 2, "sibling": 3}


def _handshake(peers):
    x, y, c = _mesh_pos()
    devs = ([(*chip, c) for chip in _other_chips(x, y)] if "chips" in peers else []) + \
           ([(x, y, 1 - c)] if "sibling" in peers else [])
    barrier = pltpu.get_barrier_semaphore()
    for dev in devs:
        pl.semaphore_signal(barrier, inc=1, device_id=dev, device_id_type=MESH)
    pl.semaphore_wait(barrier, len(devs))


class _Comm:
    def __init__(self, ins, outs, sems, start, finish, peers):
        self.ins, self.outs, self.sems, self.finish, self.peers = list(ins), list(outs), list(sems), finish, peers
        self.copies_start = start

    def start(self, *refs):
        _handshake(self.peers)
        self.copies_start(*refs)

    @property
    def collective_id(self):
        return PEER_SETS[self.peers]


def _resident(shape):
    return pl.BlockSpec(shape, lambda i, j, k: (0,) * len(shape), pipeline_mode=pl.Buffered(1))


def _mm(name, mode, grid, a_ins, a_fn, b_in, e_ins, epi, outs, acc_shape, cache_a=None, alias=(), extra_scratch=(),
        comm=None, b_slice=None):
    ni, nj, nk = grid
    na, ne, no = len(a_ins), len(e_ins), len(outs)
    assert cache_a is None or nk == 1
    n_fixed = (nk > 1) + (cache_a is not None)
    n_in = na + 1 + ne + len(alias)
    c_ins, c_outs, c_sems = (comm.ins, comm.outs, comm.sems) if comm else ([], [], [])

    def body(*refs):
        a_refs = refs[:na]
        b_ref = refs[na]
        e_refs = refs[na + 1:na + 1 + ne]
        comm_in = refs[n_in:n_in + len(c_ins)]
        out0 = n_in + len(c_ins)
        out_refs = refs[out0:out0 + no]
        comm_out = refs[out0 + no:out0 + no + len(c_outs)]
        scratch = refs[out0 + no + len(c_outs):]
        extra = scratch[n_fixed:n_fixed + len(extra_scratch)]
        comm_sems = scratch[n_fixed + len(extra_scratch):]
        i, j, k = pl.program_id(0), pl.program_id(1), pl.program_id(2)
        if comm:
            @pl.when((i == 0) & (j == 0) & (k == 0))
            def _():
                comm.start(comm_in, comm_out, comm_sems)
        if cache_a is not None:
            cache_ref = scratch[n_fixed - 1]

            @pl.when(j == 0)
            def _():
                cache_ref[...] = a_fn(a_refs, out_refs, i, j, k)

            a = cache_ref[...]
        else:
            a = a_fn(a_refs, out_refs, i, j, k)
        if b_slice is None:
            b = b_ref[...]
        elif b_slice[0] == "cols":
            b = b_ref[:, pl.ds(pl.multiple_of(j * b_slice[1], b_slice[1]), b_slice[1])]
        else:
            b = b_ref[pl.ds(pl.multiple_of(j * b_slice[1], b_slice[1]), b_slice[1]), :]
        prod = lax.dot_general(a, b, _DIMS[mode], preferred_element_type=F32)
        if nk == 1:
            epi(prod, e_refs, out_refs, i, j, extra)
        else:
            acc_ref = scratch[0]

            @pl.when(k == 0)
            def _():
                acc_ref[...] = prod

            @pl.when(k > 0)
            def _():
                acc_ref[...] += prod

            @pl.when(k == nk - 1)
            def _():
                epi(acc_ref[...], e_refs, out_refs, i, j, extra)

        if comm:
            @pl.when((i == ni - 1) & (j == nj - 1) & (k == nk - 1))
            def _():
                comm.finish(comm_in, comm_out, comm_sems)

    scratch_shapes = []
    if nk > 1:
        scratch_shapes.append(pltpu.VMEM(acc_shape, F32))
    if cache_a is not None:
        scratch_shapes.append(pltpu.VMEM(cache_a, BF16))
    any_spec = pl.BlockSpec(memory_space=pl.ANY)
    ins = (list(a_ins) + [b_in] + list(e_ins) + [(arr, any_spec) for arr, _ in alias] + [(arr, any_spec) for arr in c_ins])
    first_alias = na + 1 + ne
    res = pl.pallas_call(
        body, name=name, grid=grid,
        in_specs=[s for _, s in ins], out_specs=[s for _, s in outs] + [any_spec] * len(c_outs),
        out_shape=[o for o, _ in outs] + list(c_outs),
        scratch_shapes=scratch_shapes + list(extra_scratch) + list(c_sems),
        input_output_aliases={first_alias + n: o for n, (_, o) in enumerate(alias)},
        compiler_params=_cp(dimension_semantics=("arbitrary", "arbitrary", "arbitrary"), has_side_effects=bool(comm),
                            collective_id=comm.collective_id if comm else None),
    )(*[a for a, _ in ins])
    if comm:
        return list(res[:no]), list(res[no:])
    return res


def _bs(shape, fn):
    return pl.BlockSpec(shape, fn)


def _sds(shape, dt):
    return jax.ShapeDtypeStruct(shape, dt)


def _acc_rows(ref, val, first):
    @pl.when(first)
    def _():
        ref[...] = val

    @pl.when(jnp.logical_not(first))
    def _():
        ref[...] += val


def _fwd_norm_mm(name, x, g, w, bias, tm, tn, comm=None):
    T, D = x.shape
    N = w.shape[1]

    def a_fn(a_refs, out_refs, i, j, k):
        h, _ = _rms(a_refs[0][...], a_refs[1][...])
        hb = h.astype(BF16)
        out_refs[1][...] = hb
        return hb

    def epi(acc, e_refs, out_refs, i, j, extra):
        if bias is not None:
            acc = acc + e_refs[0][...]
        out_refs[0][...] = acc.astype(BF16)

    e_ins = [] if bias is None else [(bias, _bs((1, tn), lambda i, j, k: (0, j)))]
    return _mm(name, "nn", (T // tm, N // tn, 1),
               [(x, _bs((tm, D), lambda i, j, k: (i, 0))), (g, _bs((1, D), lambda i, j, k: (0, 0)))], a_fn,
               (w, _resident((D, N))), e_ins, epi,
               [(_sds((T, N), BF16), _bs((tm, tn), lambda i, j, k: (i, j))),
                (_sds((T, D), BF16), _bs((tm, D), lambda i, j, k: (i, 0)))],
               None, cache_a=(tm, D), comm=comm, b_slice=("cols", tn))


def _fwd_ya(u1, ln_g, ln_b, w, tm):
    T, C = u1.shape
    N = w.shape[1]

    def a_fn(a_refs, out_refs, i, j, k):
        u3 = _ln_silu(a_refs[0][...].astype(F32), a_refs[1][...], a_refs[2][...])[0].astype(BF16)
        out_refs[1][...] = u3
        return u3

    def epi(acc, e_refs, out_refs, i, j, extra):
        out_refs[0][...] = acc.astype(BF16)

    row = _bs((1, C), lambda i, j, k: (0, 0))
    tc = _bs((tm, C), lambda i, j, k: (i, 0))
    return _mm("fwd_ya", "nn", (T // tm, 1, 1), [(u1, tc), (ln_g, row), (ln_b, row)], a_fn,
               (w, _bs((C, N), lambda i, j, k: (0, 0))), [], epi,
               [(_sds((T, N), BF16), _bs((tm, N), lambda i, j, k: (i, 0))), (_sds((T, C), BF16), tc)], None)


def _fwd_yb(h, z, gb_blk, w, tm, tk):
    T, C = h.shape
    N = w.shape[1]

    def a_fn(a_refs, out_refs, i, j, k):
        ge, _ = _gelu(a_refs[1][...].astype(F32))
        pv = (a_refs[0][...].astype(F32) * ge).astype(BF16)
        out_refs[1][...] = pv
        return pv

    def epi(acc, e_refs, out_refs, i, j, extra):
        out_refs[0][...] = acc.astype(BF16)

    tk_ = _bs((tm, tk), lambda i, j, k: (i, k))
    return _mm("fwd_yb", "nn", (T // tm, 1, C // tk),
               [(h, tk_), (z, _bs((tm, tk), lambda i, j, k: (i, gb_blk + k)))], a_fn,
               (w, _bs((tk, N), lambda i, j, k: (k, 0))), [], epi,
               [(_sds((T, N), BF16), _bs((tm, N), lambda i, j, k: (i, 0))), (_sds((T, C), BF16), tk_)], (tm, N))


def _fwd_x1(x, ya, yb, z, sa_blk, w, tm, comm=None):
    T, D = x.shape

    def a_fn(a_refs, out_refs, i, j, k):
        ya_, yb_, sa_, sb_ = (r[...].astype(F32) for r in a_refs)
        mg = (_sig(sa_) * ya_ + _sig(sb_) * yb_).astype(BF16)
        out_refs[1][...] = mg
        return mg

    def epi(acc, e_refs, out_refs, i, j, extra):
        out_refs[0][...] = e_refs[0][...] + acc

    t = _bs((tm, D), lambda i, j, k: (i, 0))
    return _mm("fwd_x1", "nn", (T // tm, 1, 1),
               [(ya, t), (yb, t), (z, _bs((tm, D), lambda i, j, k: (i, sa_blk))),
                (z, _bs((tm, D), lambda i, j, k: (i, sa_blk + 1)))], a_fn,
               (w, _bs((D, D), lambda i, j, k: (0, 0))), [(x, t)], epi,
               [(_sds((T, D), F32), t), (_sds((T, D), BF16), t)], None, comm=comm)


def _fwd_x2(x1, fp, w, tm, tk, comm=None):
    T, D = x1.shape
    Fd = fp.shape[1]

    def epi(acc, e_refs, out_refs, i, j, extra):
        out_refs[0][...] = e_refs[0][...] + acc

    t = _bs((tm, D), lambda i, j, k: (i, 0))
    whole_k = tk == Fd
    r = _mm("fwd_x2", "nn", (T // tm, 1, Fd // tk),
            [(fp, _bs((tm, tk), lambda i, j, k: (i, k)))], _relu2,
            (w, _resident((Fd, D)) if whole_k else _bs((tk, D), lambda i, j, k: (k, 0))), [(x1, t)], epi,
            [(_sds((T, D), F32), t)], (tm, D), comm=comm)
    return (r[0][0], r[1]) if comm else r[0]


def _relu2(a_refs, out_refs, i, j, k):
    f = jnp.maximum(a_refs[0][...], 0.0)
    return f * f


def _loss_head(x, g, target, tm):
    T, D = x.shape

    def body(x_ref, g_ref, t_ref, loss_ref, dx_ref, dxb_ref, dg_ref):
        i = pl.program_id(0)
        xf, gv = x_ref[...], g_ref[...]
        y, r = _rms(xf, gv)
        err = y - t_ref[...]
        part = 0.5 * jnp.sum(jnp.mean(err * err, axis=-1, keepdims=True), axis=0, keepdims=True)
        dx, dg_rows = _rms_bwd(xf, gv, r, err * (1.0 / D))
        dx_ref[...] = dx
        dxb_ref[...] = dx.astype(BF16)
        _acc_rows(loss_ref, jnp.broadcast_to(part, (1, 128)), i == 0)
        _acc_rows(dg_ref, jnp.sum(dg_rows, axis=0, keepdims=True), i == 0)

    t = _bs((tm, D), lambda i: (i, 0))
    row = _bs((1, D), lambda i: (0, 0))
    return pl.pallas_call(
        body, name="loss_head", grid=(T // tm,), in_specs=[t, row, t],
        out_specs=[_bs((1, 128), lambda i: (0, 0)), t, t, row],
        out_shape=[_sds((1, 128), F32), _sds((T, D), F32), _sds((T, D), BF16), _sds((1, D), F32)],
        compiler_params=_cp(dimension_semantics=("arbitrary",)),
    )(x, g, target)


def _adamw(name, w, g, m, v, tr, comm=None):
    many = isinstance(w, (list, tuple))
    ws, gs, ms, vs = (list(a) if many else [a] for a in (w, g, m, v))
    n = len(ws)
    rows, cols = ws[0].shape
    d1 = 1.0 - ADAM_B1 ** ADAM_STEP
    d2 = 1.0 - ADAM_B2 ** ADAM_STEP

    def body(*refs):
        for q in range(n):
            w_ref, g_ref, m_ref, v_ref = (refs[a * n + q] for a in range(4))
            d_ref, mo_ref, vo_ref = (refs[(4 + a) * n + q] for a in range(3))
            gv = g_ref[...]
            mn = ADAM_B1 * m_ref[...] + (1.0 - ADAM_B1) * gv
            vn = ADAM_B2 * v_ref[...] + (1.0 - ADAM_B2) * (gv * gv)
            d_ref[...] = -ADAM_LR * ((mn / d1) / (jnp.sqrt(vn / d2) + ADAM_EPS) + ADAM_WD * w_ref[...])
            mo_ref[...] = mn
            vo_ref[...] = vn

    t = _bs((tr, cols), lambda i: (i, 0))
    r = _call_with_comm(name, body, (rows // tr,), ws + gs + ms + vs, [t] * (4 * n), [t] * (3 * n),
                        [_sds((rows, cols), F32)] * (3 * n), [], comm)
    outs, got = (r if comm else (r, None))
    res = [outs[a * n:(a + 1) * n] if many else outs[a * n] for a in range(3)]
    return (*res, got) if comm else tuple(res)


def _ident(a_refs, out_refs, i, j, k):
    return a_refs[0][...]


def _bwd_dw(name, act, dy, ti, tj, tm, a_fn=None, a_extra=(), shard_cols=None, keep=None):
    T, J = dy.shape
    I = act.shape[1]

    def epi(acc, e_refs, out_refs, i, j, extra):
        out_refs[0][...] = acc.astype(BF16).reshape(out_refs[0].shape)

    if shard_cols is None:
        out = (_sds((I, J), BF16), _bs((ti, tj), lambda i, j, k: (i, j)))
    else:
        per = shard_cols // tj
        assert ti == I and per * tj == shard_cols
        out = (_sds((J // shard_cols, 2, I // 2, shard_cols), BF16),
               _bs((None, 2, I // 2, tj), lambda i, j, k: (lax.div(j, per), 0, 0, lax.rem(j, per))))
    assert keep is None or tm == T
    a_spec = _resident((T, I)) if keep == "act" else _bs((tm, ti), lambda i, j, k: (k, i))
    b_spec = _resident((T, J)) if keep == "dy" else _bs((tm, tj), lambda i, j, k: (k, j))
    return _mm(name, "tn", (I // ti, J // tj, T // tm), [(act, a_spec)] + list(a_extra), a_fn or _ident,
               (dy, b_spec), [], epi, [out], (ti, tj))[0]


def _bwd_df(dxb, w2, fp, tm, tn, comm=None):
    T, D = dxb.shape
    Fd = w2.shape[0]

    def epi(acc, e_refs, out_refs, i, j, extra):
        out_refs[0][...] = (acc * (2.0 * jnp.maximum(e_refs[0][...].astype(F32), 0.0))).astype(BF16)

    t = _bs((tm, tn), lambda i, j, k: (i, j))
    r = _mm("bwd_df", "nt", (T // tm, Fd // tn, 1), [(dxb, _bs((tm, D), lambda i, j, k: (i, 0)))], _ident,
            (w2, _resident((Fd, D))), [(fp, t)], epi, [(_sds((T, Fd), BF16), t)], None, b_slice=("rows", tn), comm=comm)
    return (r[0][0], r[1]) if comm else r[0]


def _bwd_norm(name, dy, w, x, g, dres, tm, tk, colsum=False, comm=None):
    T, K = dy.shape
    D = w.shape[0]
    nk = K // tk

    def a_fn(a_refs, out_refs, i, j, k):
        a = a_refs[0][...]
        if colsum:
            s = jnp.sum(a.astype(F32), axis=0, keepdims=True)

            @pl.when(i == 0)
            def _():
                out_refs[3][k] = s

            @pl.when(i > 0)
            def _():
                out_refs[3][k] += s
        return a

    def epi(acc, e_refs, out_refs, i, j, extra):
        xf, gv = e_refs[0][...], e_refs[1][...]
        r = lax.rsqrt(jnp.mean(xf * xf, axis=-1, keepdims=True) + EPS)
        dx, dg_rows = _rms_bwd(xf, gv, r, acc)
        dx = dx + e_refs[2][...]
        out_refs[0][...] = dx
        out_refs[1][...] = dx.astype(BF16)
        _acc_rows(out_refs[2], jnp.sum(dg_rows, axis=0, keepdims=True), i == 0)

    t = _bs((tm, D), lambda i, j, k: (i, 0))
    row = _bs((1, D), lambda i, j, k: (0, 0))
    outs = [(_sds((T, D), F32), t), (_sds((T, D), BF16), t), (_sds((1, D), F32), row)]
    if colsum:
        outs.append((_sds((nk, 1, tk), F32), _bs((nk, 1, tk), lambda i, j, k: (0, 0, 0))))
    return _mm(name, "nt", (T // tm, 1, nk), [(dy, _bs((tm, tk), lambda i, j, k: (i, k)))], a_fn,
               (w, _resident((D, K)) if nk == 1 else _bs((D, tk), lambda i, j, k: (0, k))),
               [(x, t), (g, row), (dres, t)], epi, outs, (tm, D), comm=comm)


def _bwd_dm(dxb, w_o, ya, yb, z, sa_blk, tm):
    T, D = dxb.shape

    def epi(acc, e_refs, out_refs, i, j, extra):
        ya_, yb_, sa_, sb_ = (r[...].astype(F32) for r in e_refs)
        ga, gb = _sig(sa_), _sig(sb_)
        out_refs[0][...] = (acc * ga).astype(BF16)
        out_refs[1][...] = (acc * gb).astype(BF16)
        stage, sem = extra
        put = pltpu.make_async_copy(
            stage, out_refs[2].at[pl.ds(pl.multiple_of(i * tm, tm), tm), pl.ds(sa_blk * D, 2 * D)], sem)

        @pl.when(i > 0)
        def _():
            put.wait()

        stage[:, 0:D] = (acc * ya_ * ga * (1.0 - ga)).astype(BF16)
        stage[:, D:2 * D] = (acc * yb_ * gb * (1.0 - gb)).astype(BF16)
        put.start()

        @pl.when(i == T // tm - 1)
        def _():
            put.wait()

    t = _bs((tm, D), lambda i, j, k: (i, 0))
    return _mm("bwd_dm", "nt", (T // tm, 1, 1), [(dxb, t)], _ident, (w_o, _bs((D, D), lambda i, j, k: (0, 0))),
               [(ya, t), (yb, t), (z, _bs((tm, D), lambda i, j, k: (i, sa_blk))),
                (z, _bs((tm, D), lambda i, j, k: (i, sa_blk + 1)))], epi,
               [(_sds((T, D), BF16), t), (_sds((T, D), BF16), t),
                (_sds(z.shape, BF16), pl.BlockSpec(memory_space=pl.ANY))], None,
               extra_scratch=[pltpu.VMEM((tm, 2 * D), BF16), pltpu.SemaphoreType.DMA(())])


def _bwd_du3(dya, w, u1, ln_g, ln_b, tm):
    T, D = dya.shape
    C = w.shape[0]

    def epi(acc, e_refs, out_refs, i, j, extra):
        gv = e_refs[1][...]
        _, uh, rstd, u2, s = _ln_silu(e_refs[0][...].astype(F32), gv, e_refs[2][...])
        du2 = acc * (s * (1.0 + u2 * (1.0 - s)))
        duh = du2 * gv
        out_refs[0][...] = rstd * (duh - jnp.mean(duh, axis=-1, keepdims=True)
                                   - uh * jnp.mean(duh * uh, axis=-1, keepdims=True))
        _acc_rows(out_refs[1], jnp.sum(du2 * uh, axis=0, keepdims=True), i == 0)
        _acc_rows(out_refs[2], jnp.sum(du2, axis=0, keepdims=True), i == 0)

    t = _bs((tm, C), lambda i, j, k: (i, 0))
    row = _bs((1, C), lambda i, j, k: (0, 0))
    return _mm("bwd_du3", "nt", (T // tm, 1, 1), [(dya, _bs((tm, D), lambda i, j, k: (i, 0)))], _ident,
               (w, _bs((C, D), lambda i, j, k: (0, 0))), [(u1, t), (ln_g, row), (ln_b, row)], epi,
               [(_sds((T, C), F32), t), (_sds((1, C), F32), row), (_sds((1, C), F32), row)], None)


def _bwd_dp(dyb, w, h, z, dz, gb_blk, tm, tn, comm=None):
    T, D = dyb.shape
    R = w.shape[0]

    def epi(acc, e_refs, out_refs, i, j, extra):
        gbv = e_refs[1][...].astype(F32)
        ge, th = _gelu(gbv)
        out_refs[0][...] = acc * ge
        out_refs[1][...] = (acc * e_refs[0][...].astype(F32) * _gelu_grad(gbv, th)).astype(BF16)

    t = _bs((tm, tn), lambda i, j, k: (i, j))
    tz = _bs((tm, tn), lambda i, j, k: (i, gb_blk + j))
    return _mm("bwd_dp", "nt", (T // tm, R // tn, 1), [(dyb, _bs((tm, D), lambda i, j, k: (i, 0)))], _ident,
               (w, _bs((tn, D), lambda i, j, k: (j, 0))), [(h, t), (z, tz)], epi,
               [(_sds((T, R), F32), t), (_sds(dz.shape, BF16), tz)], None, cache_a=None, alias=[(dz, 1)], comm=comm)


CONV_ROWS = 32


def _shifted_taps(x, halo, shifts, fn):
    n = CONV_ROWS + halo
    by_r = {}
    for k, s in shifts:
        by_r.setdefault(s % 8, []).append((k, s))
    for r in sorted(by_r):
        xr = x if r == 0 else pltpu.roll(x, n - r, 0)
        for k, s in by_r[r]:
            q = s - r
            fn(k, xr[q:q + CONV_ROWS])


def _conv_fwd(name, z, blk0, gate_blk0, w_pad, bias, taps, seq, tc, out_dtype, comm=None):
    T = z.shape[0]
    C = w_pad.shape[1]
    nb, nj = T // seq, C // tc
    pad = 8 * ((taps - 1 + 7) // 8)
    halo = pad
    shifts = [(k, pad - (taps - 1) + k) for k in range(taps)]
    glu = gate_blk0 is not None

    def body(*refs):
        if glu:
            v_ref, g_ref, w_ref, b_ref, o_ref, p_ref = refs
        else:
            v_ref, w_ref, b_ref, o_ref, p_ref = refs
        p_ref[pl.ds(0, pad), :] = jnp.zeros((pad, tc), F32)
        u = v_ref[...].astype(F32)
        if glu:
            u = u * _sig(g_ref[...].astype(F32))
        p_ref[pl.ds(pad, seq), :] = u

        def step(c, _):
            base = pl.multiple_of(c * CONV_ROWS, CONV_ROWS)
            x = p_ref[pl.ds(base, CONV_ROWS + halo), :]
            acc = [jnp.zeros((CONV_ROWS, tc), F32) + b_ref[...]]

            def tap(k, xs):
                acc[0] = acc[0] + w_ref[k:k + 1, :] * xs

            _shifted_taps(x, halo, shifts, tap)
            o_ref[pl.ds(base, CONV_ROWS), :] = acc[0].astype(out_dtype)
            return 0

        lax.fori_loop(0, seq // CONV_ROWS, step, 0)

    zin = [(z, _bs((seq, tc), lambda b, j: (b, blk0 + j)))]
    if glu:
        zin.append((z, _bs((seq, tc), lambda b, j: (b, gate_blk0 + j))))
    ins = zin + [(w_pad, _bs((w_pad.shape[0], tc), lambda b, j: (0, j))), (bias, _bs((1, tc), lambda b, j: (0, j)))]
    r = _call_with_comm(name, body, (nb, nj), [a for a, _ in ins], [s for _, s in ins],
                        [_bs((seq, tc), lambda b, j: (b, j))], [_sds((T, C), out_dtype)],
                        [pltpu.VMEM((seq + pad, tc), F32)], comm)
    return (r[0][0], r[1]) if comm else r[0]


def _conv_bwd(name, dy, z, dz, blk0, gate_blk0, w_pad, taps, seq, tc, comm=None):
    T = z.shape[0]
    C = w_pad.shape[1]
    nb, nj = T // seq, C // tc
    kp = w_pad.shape[0]
    pad = 8 * ((taps - 1 + 7) // 8)
    halo = pad
    sh_du = [(k, taps - 1 - k) for k in range(taps)]
    sh_dw = [(k, pad - (taps - 1) + k) for k in range(taps)]
    glu = gate_blk0 is not None

    def body(*refs):
        if glu:
            dy_ref, v_ref, g_ref, w_ref, _dz_in, dz_out, dw_ref, db_ref, pdy, pu, du_s, wacc, ob, ob2, osem = refs
        else:
            dy_ref, v_ref, w_ref, _dz_in, dz_out, dw_ref, db_ref, pdy, pu, du_s, wacc, ob, osem = refs
        j = pl.program_id(0)
        b = pl.program_id(1)
        pdy[pl.ds(seq, pad), :] = jnp.zeros((pad, tc), F32)
        pdy[pl.ds(0, seq), :] = dy_ref[...].astype(F32)
        pu[pl.ds(0, pad), :] = jnp.zeros((pad, tc), F32)
        v = v_ref[...].astype(F32)
        if glu:
            sg = _sig(g_ref[...].astype(F32))
            pu[pl.ds(pad, seq), :] = v * sg
        else:
            pu[pl.ds(pad, seq), :] = v
        wacc[...] = jnp.zeros(wacc.shape, F32)

        def step(c, dbacc):
            base = pl.multiple_of(c * CONV_ROWS, CONV_ROWS)
            xdy = pdy[pl.ds(base, CONV_ROWS + halo), :]
            acc = [jnp.zeros((CONV_ROWS, tc), F32)]

            def tap(k, xs):
                acc[0] = acc[0] + w_ref[k:k + 1, :] * xs

            _shifted_taps(xdy, halo, sh_du, tap)
            du_s[pl.ds(base, CONV_ROWS), :] = acc[0]
            dyc = xdy[0:CONV_ROWS]
            xu = pu[pl.ds(base, CONV_ROWS + halo), :]

            def wtap(k, xs):
                p = dyc * xs
                s8 = p[0:8]
                for m in range(1, CONV_ROWS // 8):
                    s8 = s8 + p[8 * m:8 * m + 8]
                wacc[pl.ds(8 * k, 8), :] += s8

            _shifted_taps(xu, halo, sh_dw, wtap)
            d8 = dyc[0:8]
            for m in range(1, CONV_ROWS // 8):
                d8 = d8 + dyc[8 * m:8 * m + 8]
            return dbacc + d8

        dbacc = lax.fori_loop(0, seq // CONV_ROWS, step, jnp.zeros((8, tc), F32))
        du = du_s[...]
        rows = pl.ds(pl.multiple_of(b * seq, seq), seq)
        puts = [pltpu.make_async_copy(ob, dz_out.at[rows, pl.ds(pl.multiple_of((blk0 + j) * tc, tc), tc)], osem.at[0])]
        if glu:
            puts.append(pltpu.make_async_copy(
                ob2, dz_out.at[rows, pl.ds(pl.multiple_of((gate_blk0 + j) * tc, tc), tc)], osem.at[1]))

        @pl.when((j > 0) | (b > 0))
        def _():
            for cp in puts:
                cp.wait()

        if glu:
            ob[...] = (du * sg).astype(BF16)
            ob2[...] = (du * v * sg * (1.0 - sg)).astype(BF16)
        else:
            ob[...] = du.astype(BF16)
        for cp in puts:
            cp.start()

        @pl.when((j == nj - 1) & (b == nb - 1))
        def _():
            for cp in puts:
                cp.wait()
        dw = jnp.sum(wacc[...].reshape(kp, 8, tc), axis=1)
        _acc_rows(dw_ref, dw, b == 0)
        _acc_rows(db_ref, jnp.sum(dbacc, axis=0, keepdims=True), b == 0)

    zin = [(z, _bs((seq, tc), lambda j, b: (b, blk0 + j)))]
    if glu:
        zin.append((z, _bs((seq, tc), lambda j, b: (b, gate_blk0 + j))))
    ins = [(dy, _bs((seq, tc), lambda j, b: (b, j)))] + zin + [(w_pad, _bs((kp, tc), lambda j, b: (0, j))),
                                                               (dz, pl.BlockSpec(memory_space=pl.ANY))]
    dz_idx = len(ins) - 1
    out_specs = [pl.BlockSpec(memory_space=pl.ANY), _bs((kp, tc), lambda j, b: (0, j)), _bs((1, tc), lambda j, b: (0, j))]
    out_shape = [_sds(dz.shape, dz.dtype), _sds((kp, C), F32), _sds((1, C), F32)]
    stage = [pltpu.VMEM((seq, tc), BF16)] * (2 if glu else 1) + [pltpu.SemaphoreType.DMA((2,))]
    return _call_with_comm(
        name, body, (nj, nb), [a for a, _ in ins], [s for _, s in ins], out_specs, out_shape,
        [pltpu.VMEM((seq + pad, tc), F32), pltpu.VMEM((seq + pad, tc), F32),
         pltpu.VMEM((seq, tc), F32), pltpu.VMEM((8 * kp, tc), F32)] + stage, comm, aliases={dz_idx: 0})


RG_ROWS = 256


def _softplus_neg(lam):
    return jnp.maximum(-lam, 0.0) + jnp.log(1.0 + jnp.exp(-jnp.abs(lam)))


def _gates(v0c, wa_ref, wx_ref, ba, bx, sp):
    vb = v0c.astype(BF16)
    r = _sig(jnp.dot(vb, wa_ref[...], preferred_element_type=F32) + ba)
    i = _sig(jnp.dot(vb, wx_ref[...], preferred_element_type=F32) + bx)
    return r, i, -LRU_C * r * sp


def _decay(la, first_row):
    a = jnp.exp(la)
    a2 = a * a
    x = 2.0 * la
    series = -x * (1.0 + x * (0.5 + x * (1.0 / 6)))
    mult = jnp.sqrt(jnp.where(x > -0.01, series, 1.0 - a2))
    dmult = jnp.where(first_row, 0.0, -a2 / mult)
    mult = jnp.where(first_row, 1.0, mult)
    return a, mult, dmult


def _group_scan(a, b, reverse):
    n = a.shape[0]
    row = lax.broadcasted_iota(jnp.int32, a.shape, 0) & 7
    for d in (1, 2, 4):
        sh = n - d if reverse else d
        a_s, b_s = pltpu.roll(a, sh, 0), pltpu.roll(b, sh, 0)
        m = (row < 8 - d) if reverse else (row >= d)
        b = jnp.where(m, a * b_s + b, b)
        a = jnp.where(m, a * a_s, a)
    return a, b


def _group_carry(a_s, b_s, o_s, n_groups, reverse):
    cols = a_s.shape[1]

    def step(g, carry):
        g = n_groups - 1 - g if reverse else g
        rows = pl.ds(pl.multiple_of(g * 8, 8), 8)
        o = a_s[rows, :] * carry + b_s[rows, :]
        o_s[rows, :] = o
        return o[0:1, :] if reverse else o[7:8, :]

    lax.fori_loop(0, n_groups, step, jnp.zeros((1, cols), F32), unroll=2)


def _rglru_fwd(v0, wa, wx, ba, bx, lam, seq, comm=None):
    T, C = v0.shape
    ng, G = wa.shape[0], wa.shape[1]
    nb = T // seq

    def body(v_ref, wa_ref, wx_ref, ba_ref, bx_ref, lam_ref, h_ref, r_ref, i_ref, la_ref, a_s, b_s, h_s):
        sp = _softplus_neg(lam_ref[...])

        def chunk(c, _):
            rows = pl.ds(pl.multiple_of(c * RG_ROWS, RG_ROWS), RG_ROWS)
            t = lax.broadcasted_iota(jnp.int32, (RG_ROWS, G), 0) + c * RG_ROWS
            v0c = v_ref[rows, :]
            r, i, la = _gates(v0c, wa_ref, wx_ref, ba_ref[...], bx_ref[...], sp)
            r_ref[rows, :] = r.astype(BF16)
            i_ref[rows, :] = i.astype(BF16)
            la_ref[rows, :] = la
            a, mult, _ = _decay(la, t == 0)
            a_g, b_g = _group_scan(a, mult * i * v0c, False)
            a_s[rows, :] = a_g
            b_s[rows, :] = b_g
            return 0

        lax.fori_loop(0, seq // RG_ROWS, chunk, 0)
        _group_carry(a_s, b_s, h_s, seq // 8, False)
        h_ref[...] = h_s[...].astype(BF16)

    t2 = _bs((seq, G), lambda b, g: (b, g))
    wsp = _bs((None, G, G), lambda b, g: (g, 0, 0))
    row = _bs((1, G), lambda b, g: (0, g))
    return _call_with_comm("rglru_fwd", body, (nb, ng), [v0, wa, wx, ba, bx, lam], [t2, wsp, wsp, row, row, row],
                           [t2] * 4, [_sds((T, C), BF16)] * 3 + [_sds((T, C), F32)], [pltpu.VMEM((seq, G), F32)] * 3, comm)


def _call_with_comm(name, body, grid, ins, in_specs, out_specs, out_shape, scratch, comm, aliases=None):
    n_in, n_out, n_s = len(ins), len(out_shape), len(scratch)
    c_ins, c_outs, c_sems = (comm.ins, comm.outs, comm.sems) if comm else ([], [], [])

    def wrapped(*refs):
        o0 = n_in + len(c_ins)
        s0 = o0 + n_out + len(c_outs)
        cin, cout, csem = refs[n_in:o0], refs[o0 + n_out:s0], refs[s0 + n_s:]
        ids = [pl.program_id(a) for a in range(len(grid))]
        if comm:
            first = _all_of([i == 0 for i in ids])

            @pl.when(first)
            def _():
                comm.start(cin, cout, csem)

        body(*refs[:n_in], *refs[o0:o0 + n_out], *refs[s0:s0 + n_s])
        if comm:
            last = _all_of([i == n - 1 for i, n in zip(ids, grid)])

            @pl.when(last)
            def _():
                comm.finish(cin, cout, csem)

    res = pl.pallas_call(
        wrapped, name=name, grid=grid, in_specs=list(in_specs) + [ANY] * len(c_ins),
        out_specs=list(out_specs) + [ANY] * len(c_outs), out_shape=list(out_shape) + list(c_outs),
        scratch_shapes=list(scratch) + list(c_sems), input_output_aliases=aliases or {},
        compiler_params=_cp(dimension_semantics=("arbitrary",) * len(grid), has_side_effects=bool(comm),
                            collective_id=comm.collective_id if comm else None),
    )(*ins, *c_ins)
    return (list(res[:n_out]), list(res[n_out:])) if comm else list(res)


def _all_of(conds):
    out = conds[0]
    for c in conds[1:]:
        out = out & c
    return out


def _rglru_bwd(v0, h, dh, r_g, i_g, la_g, wa, wx, lam, seq, comm=None):
    T, C = v0.shape
    ng, G = wa.shape[0], wa.shape[1]
    nb = T // seq
    R = RG_ROWS

    def body(v_ref, h_ref, dh_ref, r_ref, i_ref, la_ref, wa_ref, wx_ref, lam_ref,
             dv_ref, dwa_ref, dwx_ref, dba_ref, dbx_ref, dlam_ref, a_s, b_s, q_s, hp_s):
        b = pl.program_id(1)
        lam_v = lam_ref[...]
        sp = _softplus_neg(lam_v)
        dsp_dlam = -_sig(-lam_v)

        @pl.when(b == 0)
        def _():
            dwa_ref[...] = jnp.zeros((G, G), F32)
            dwx_ref[...] = jnp.zeros((G, G), F32)
            dba_ref[...] = jnp.zeros((1, G), F32)
            dbx_ref[...] = jnp.zeros((1, G), F32)
            dlam_ref[...] = jnp.zeros((1, G), F32)

        hp_s[pl.ds(0, 8), :] = jnp.zeros((8, G), F32)
        hp_s[pl.ds(8, seq), :] = h_ref[...].astype(F32)
        q_s[pl.ds(seq, 8), :] = jnp.zeros((8, G), F32)

        def chunk1(c, _):
            rows = pl.ds(pl.multiple_of(c * R, R), R)
            a = jnp.exp(la_ref[rows, :])
            a_g, b_g = _group_scan(a, a * dh_ref[rows, :].astype(F32), True)
            a_s[rows, :] = a_g
            b_s[rows, :] = b_g
            return 0

        lax.fori_loop(0, seq // R, chunk1, 0)
        _group_carry(a_s, b_s, q_s, seq // 8, True)

        def chunk3(c, _):
            base = pl.multiple_of(c * R, R)
            rows = pl.ds(base, R)
            t = lax.broadcasted_iota(jnp.int32, (R, G), 0) + c * R
            v0c = v_ref[rows, :]
            r, i = r_ref[rows, :].astype(F32), i_ref[rows, :].astype(F32)
            a, mult, dmult_dla = _decay(la_ref[rows, :], t == 0)
            q_next = pltpu.roll(q_s[pl.ds(base, R + 8), :], R + 7, 0)[0:R]
            h_prev = pltpu.roll(hp_s[pl.ds(base, R + 8), :], R + 1, 0)[0:R]
            gt = dh_ref[rows, :].astype(F32) + q_next
            dla = gt * h_prev * a + gt * i * v0c * dmult_dla
            dpa = dla * (-LRU_C * sp) * r * (1.0 - r)
            dpx = gt * mult * v0c * i * (1.0 - i)
            dpa_b, dpx_b, v_b = dpa.astype(BF16), dpx.astype(BF16), v0c.astype(BF16)
            dv_ref[rows, :] = (gt * mult * i
                               + lax.dot_general(dpa_b, wa_ref[...], _DIMS["nt"], preferred_element_type=F32)
                               + lax.dot_general(dpx_b, wx_ref[...], _DIMS["nt"], preferred_element_type=F32))
            dwa_ref[...] += lax.dot_general(v_b, dpa_b, _DIMS["tn"], preferred_element_type=F32)
            dwx_ref[...] += lax.dot_general(v_b, dpx_b, _DIMS["tn"], preferred_element_type=F32)
            dba_ref[...] += jnp.sum(dpa, axis=0, keepdims=True)
            dbx_ref[...] += jnp.sum(dpx, axis=0, keepdims=True)
            dlam_ref[...] += jnp.sum(dla * (-LRU_C * r), axis=0, keepdims=True) * dsp_dlam
            return 0

        lax.fori_loop(0, seq // R, chunk3, 0)

    t2 = _bs((seq, G), lambda g, b: (b, g))
    wsp = _bs((None, G, G), lambda g, b: (g, 0, 0))
    row = _bs((1, G), lambda g, b: (0, g))
    return _call_with_comm(
        "rglru_bwd", body, (ng, nb), [v0, h, dh, r_g, i_g, la_g, wa, wx, lam], [t2] * 6 + [wsp, wsp, row],
        [t2, wsp, wsp, row, row, row],
        [_sds((T, C), F32), _sds((ng, G, G), F32), _sds((ng, G, G), F32),
         _sds((1, C), F32), _sds((1, C), F32), _sds((1, C), F32)],
        [pltpu.VMEM((seq, G), F32), pltpu.VMEM((seq, G), F32),
         pltpu.VMEM((seq + 8, G), F32), pltpu.VMEM((seq + 8, G), F32)], comm)


TC_A = 256
TC_B = 512
TAPS_A, TAPS_B = 31, 4


def _tiles(T):
    return min(512, T), min(1024, T)


GATHERED = ("w_in", "w_1", "w_a_out", "w_b_out", "w_o", "w_2", "caw", "cbw")
GATHER_KIND = {"w_in": (True, True), "w_1": (True, True), "w_a_out": (False, True), "w_b_out": (False, True),
               "w_o": (False, True), "w_2": (False, True), "caw": (True, False), "cbw": (True, False)}


def _layer_fwd(x, p, seq, jobs=None):
    T, D = x.shape
    C, R = p["ln_g"].shape[1], p["lam"].shape[1]
    tm, tl = _tiles(T)
    gb_blk, sa_blk = (2 * C + R) // TC_B, (2 * C + 2 * R) // D
    p, ahead, jobs = dict(p), {}, jobs or {}

    def gather(call):
        js = jobs.get(call)
        return _gather_comm([s for _, s, _ in js], [GATHER_KIND[n] for n, _, _ in js]) if js else None

    def outs(r, call):
        js = jobs.get(call)
        if not js:
            return r
        for (n, _, for_next), whole in zip(js, r[1]):
            (ahead if for_next else p)[n] = whole
        return r[0]

    z, h = outs(_fwd_norm_mm("fwd_z", x, p["g_mix"], p["w_in"], p["b_in"], tl, 1024, comm=gather("fwd_z")), "fwd_z")
    u1 = outs(_conv_fwd("conv_a_fwd", z, 0, C // TC_A, p["caw"], p["cab"], TAPS_A, seq, TC_A, BF16,
                        comm=gather("conv_a_fwd")), "conv_a_fwd")
    ya, u3 = _fwd_ya(u1, p["ln_g"], p["ln_b"], p["w_a_out"], tm)
    v0 = _conv_fwd("conv_b_fwd", z, 2 * C // TC_B, None, p["cbw"], p["cbb"], TAPS_B, seq, TC_B, F32)
    hr, rg, ig, lag = outs(_rglru_fwd(v0, p["wa"], p["wx"], p["b_rg_a"], p["b_rg_x"], p["lam"], seq,
                                      comm=gather("rglru_fwd")), "rglru_fwd")
    yb, pb = _fwd_yb(hr, z, gb_blk, p["w_b_out"], tl, TC_B)
    x1, mg = outs(_fwd_x1(x, ya, yb, z, sa_blk, p["w_o"], tm, comm=gather("fwd_x1")), "fwd_x1")
    fp, h2 = outs(_fwd_norm_mm("fwd_f", x1, p["g_mlp"], p["w_1"], None, tl, 1024, comm=gather("fwd_f")), "fwd_f")
    x2 = outs(_fwd_x2(x1, fp, p["w_2"], tm, fp.shape[1], comm=gather("fwd_x2")), "fwd_x2")
    saved = dict(x=x, z=z, h=h, u1=u1, u3=u3, ya=ya, v0=v0, hr=hr, rg=rg, ig=ig, lag=lag, pb=pb, yb=yb, mg=mg, x1=x1,
                 fp=fp, h2=h2)
    return x2, saved, p, ahead


class _Reduce:
    EARLY = ("w_2", "w_1", "w_o", "w_a_out")
    LATE = ("w_b_out", "w_in")

    def __init__(self, accs, c_arr, kcl_of):
        self.accs, self.c_arr, self.kcl_of, self.late = accs, c_arr, kcl_of, None

    @staticmethod
    def pieces(partials):
        return [a if a.ndim == 4 else a.reshape(N_CHIPS, 2, a.shape[0] // (2 * N_CHIPS), a.shape[1]) for a in partials]

    def chip_sums(self, pgs, swapped):
        return _sum_siblings(pgs, swapped, self.c_arr)

    def finish(self, names, sums, received, layer):
        done = _sum_chips(sums, received, self.kcl_of(layer), [self.accs[n] for n in names])
        self.accs.update(zip(names, done))


def _layer_bwd(dx2, dx2b, p, s, seq, red=None, layer=0):
    T, D = dx2.shape
    C, R = p["ln_g"].shape[1], p["lam"].shape[1]
    tm, tl = _tiles(T)
    gb_blk, sa_blk = (2 * C + R) // TC_B, (2 * C + 2 * R) // D
    z = s["z"]
    g = {}


    late_sums = None
    if red is not None and red.late is not None:
        late, red.late = red.late, None
        dfp, got = _bwd_df(dx2b, p["w_2"], s["fp"], tl, 1024, comm=_swap_comm(late))
        late_sums = red.chip_sums(late, got)
    else:
        dfp = _bwd_df(dx2b, p["w_2"], s["fp"], tl, 1024)
    g["w_2"] = _bwd_dw("bwd_dw2", s["fp"], dx2b, 1024, D, T, a_fn=_relu2, keep="dy")
    dx1, dx1b, g["g_mlp"] = _bwd_norm("bwd_dh2", dfp, p["w_1"], s["x1"], p["g_mlp"], dx2, tm, dfp.shape[1])
    g["w_1"] = _bwd_dw("bwd_dw1", s["h2"], dfp, D, 1024, T, shard_cols=dfp.shape[1] // N_CHIPS, keep="act")

    dya, dyb, dz = _bwd_dm(dx1b, p["w_o"], s["ya"], s["yb"], z, sa_blk, tm)

    g["w_o"] = _bwd_dw("bwd_dwo", s["mg"], dx1b, D, D, tl)
    du1, g["ln_g"], g["ln_b"] = _bwd_du3(dya, p["w_a_out"], s["u1"], p["ln_g"], p["ln_b"], tm)
    g["w_a_out"] = _bwd_dw("bwd_dwa", s["u3"], dya, C, D, tl)
    conv_a_args = ("conv_a_bwd", du1, z, dz, 0, C // TC_A, p["caw"], TAPS_A, seq, TC_A)
    if late_sums is not None:
        (dz, g["caw"], g["cab"]), got = _conv_bwd(*conv_a_args, comm=_scatter_comm(late_sums))
        red.finish(red.LATE, late_sums, got, layer + 1)
    else:
        dz, g["caw"], g["cab"] = _conv_bwd(*conv_a_args)

    dp_args = (dyb, p["w_b_out"], s["hr"], z, dz, gb_blk, tl, TC_B)
    if red is not None:
        early = red.pieces([g.pop(n) for n in red.EARLY])
        (dhr, dz), got = _bwd_dp(*dp_args, comm=_swap_comm(early))
        early_sums = red.chip_sums(early, got)
    else:
        dhr, dz = _bwd_dp(*dp_args)

    g["w_b_out"] = _bwd_dw("bwd_dwb", s["pb"], dyb, R, D, tl)
    rg_args = (s["v0"], s["hr"], dhr, s["rg"], s["ig"], s["lag"], p["wa"], p["wx"], p["lam"], seq)
    if red is not None:
        rg_out, got = _rglru_bwd(*rg_args, comm=_scatter_comm(early_sums))
        red.finish(red.EARLY, early_sums, got, layer)
    else:
        rg_out = _rglru_bwd(*rg_args)
    dv0, g["wa"], g["wx"], g["b_rg_a"], g["b_rg_x"], g["lam"] = rg_out
    dz, g["cbw"], g["cbb"] = _conv_bwd("conv_b_bwd", dv0, z, dz, 2 * C // TC_B, None, p["cbw"], TAPS_B, seq, TC_B)

    dx, dxb, g["g_mix"], dbin = _bwd_norm("bwd_dh", dz, p["w_in"], s["x"], p["g_mix"], dx1, tm, dz.shape[1],
                                          colsum=True)
    g["b_in"] = dbin.reshape(1, -1)
    ns = dz.shape[1] // N_CHIPS
    g["w_in"] = _bwd_dw("bwd_dwin", s["h"], dz, D, ns // 2, T, shard_cols=ns, keep="act")
    if red is not None:
        red.late = red.pieces([g.pop(n) for n in red.LATE])
    return dx, dxb, g


ANY = pl.BlockSpec(memory_space=pl.ANY)


def _mesh_pos():
    return lax.axis_index("x"), lax.axis_index("y"), lax.axis_index("c")


def _other_chips(x, y):
    return [(1 - x, y), (x, 1 - y), (1 - x, 1 - y)]


def _remote(src, dst, ssem, rsem, dev):
    return pltpu.make_async_remote_copy(src_ref=src, dst_ref=dst, send_sem=ssem, recv_sem=rsem,
                                        device_id=dev, device_id_type=MESH)


def _gather_region(src, dst, by_cols, k, half):
    rows, cols = src.shape
    nr = rows if half is None else rows // 2
    r0 = 0 if half is None else half * nr
    if by_cols:
        return dst.at[pl.ds(r0, nr), pl.ds(pl.multiple_of(k * cols, 128), cols)]
    return dst.at[pl.ds(pl.multiple_of(k * rows + r0, 8), nr), :]


def _gather_sends(src, dst, kinds, send, recv):
    x, y, c = _mesh_pos()
    cps = []
    for t in range(len(src)):
        half = c if kinds[t][1] else None
        hr = src[t].shape[0] // 2
        s_ref = src[t].at[pl.ds(c * hr, hr), :] if kinds[t][1] else src[t]
        for j, chip in enumerate(_other_chips(x, y)):
            cps.append(_remote(s_ref, _gather_region(src[t], dst[t], kinds[t][0], 2 * x + y, half),
                               send.at[t, j], recv.at[t, j], (*chip, c)))
    return cps


def _gather_finish(src, dst, kinds, send, recv, fsend, frecv):
    x, y, c = _mesh_pos()
    chips = _other_chips(x, y)
    sib = (x, y, 1 - c)
    n = len(src)
    fwd = []
    for t in range(n):
        half = c if kinds[t][1] else None
        for j, chip in enumerate(chips):
            got = _gather_region(src[t], dst[t], kinds[t][0], 2 * chip[0] + chip[1], half)
            _remote(got, got, send.at[t, j], recv.at[t, j], (*chip, c)).wait_recv()
            if kinds[t][1]:
                cp = _remote(got, got, fsend.at[t, j], frecv.at[t, j], sib)
                cp.start()
                fwd.append(cp)
    for t in range(n):
        if kinds[t][1]:
            for j, chip in enumerate(chips):
                got = _gather_region(src[t], dst[t], kinds[t][0], 2 * chip[0] + chip[1], 1 - c)
                _remote(got, got, fsend.at[t, j], frecv.at[t, j], sib).wait_recv()
    for cp in _gather_sends(src, dst, kinds, send, recv) + fwd:
        cp.wait_send()


def _gather_sems(n):
    sem = pltpu.SemaphoreType.DMA
    return [sem((n, 3)), sem((n, 3)), sem((n, 3)), sem((n, 3))]


def _gather_comm(shards, kinds):
    n = len(shards)

    def whole(s, by_cols):
        return (s.shape[0], N_CHIPS * s.shape[1]) if by_cols else (N_CHIPS * s.shape[0], s.shape[1])

    def own(src, dst, lsem):
        x, y, _ = _mesh_pos()
        return [pltpu.make_async_copy(src[t], _gather_region(src[t], dst[t], kinds[t][0], 2 * x + y, None), lsem.at[t])
                for t in range(n)]

    def start(src, dst, sems):
        for cp in own(src, dst, sems[4]) + _gather_sends(src, dst, kinds, sems[0], sems[1]):
            cp.start()

    def finish(src, dst, sems):
        _gather_finish(src, dst, kinds, *sems[:4])
        for cp in own(src, dst, sems[4]):
            cp.wait()

    return _Comm(shards, [_sds(whole(s, k[0]), s.dtype) for s, k in zip(shards, kinds)],
                 _gather_sems(n) + [pltpu.SemaphoreType.DMA((n,))], start, finish,
                 "chips+sibling" if any(k[1] for k in kinds) else "chips")


def _scatter_comm(ps):
    n = len(ps)

    def copies(src, dst, sems):
        x, y, c = _mesh_pos()
        return [_remote(src[t].at[2 * chip[0] + chip[1]], dst[t].at[j], sems[0].at[t, j], sems[1].at[t, j], (*chip, c))
                for t in range(n) for j, chip in enumerate(_other_chips(x, y))]

    def start(src, dst, sems):
        for cp in copies(src, dst, sems):
            cp.start()

    def finish(src, dst, sems):
        cps = copies(src, dst, sems)
        for cp in cps:
            cp.wait_recv()
        for cp in cps:
            cp.wait_send()

    sem = pltpu.SemaphoreType.DMA
    return _Comm(ps, [_sds((3,) + a.shape[1:], a.dtype) for a in ps], [sem((n, 3)), sem((n, 3))], start, finish, "chips")


def _comm_call(name, comm):
    n_i, n_o = len(comm.ins), len(comm.outs)

    def body(*refs):
        comm.start(refs[:n_i], refs[n_i:n_i + n_o], refs[n_i + n_o:])
        comm.finish(refs[:n_i], refs[n_i:n_i + n_o], refs[n_i + n_o:])

    return pl.pallas_call(
        body, name=name, in_specs=[ANY] * n_i, out_specs=[ANY] * n_o, out_shape=comm.outs, scratch_shapes=comm.sems,
        compiler_params=_cp(has_side_effects=True, collective_id=comm.collective_id),
    )(*comm.ins)


def _swap_comm(pgs):
    n = len(pgs)

    def copies(src, dst, sems):
        x, y, c = _mesh_pos()
        return [_remote(src[t].at[:, 1 - c], dst[t], sems[0].at[t], sems[1].at[t], (x, y, 1 - c)) for t in range(n)]

    def start(src, dst, sems):
        for cp in copies(src, dst, sems):
            cp.start()

    def finish(src, dst, sems):
        cps = copies(src, dst, sems)
        for cp in cps:
            cp.wait_recv()
        for cp in cps:
            cp.wait_send()

    sem = pltpu.SemaphoreType.DMA
    return _Comm(pgs, [_sds((a.shape[0],) + a.shape[2:], a.dtype) for a in pgs], [sem((n,)), sem((n,))], start, finish,
                 "sibling")


def _join_halves(accs, also=None):
    n = len(accs)
    c_ins, c_outs, c_sems = (also.ins, also.outs, also.sems) if also else ([], [], [])
    assert also is None or also.peers == "chips"
    peers = "chips+sibling" if also else "sibling"

    def body(*refs):
        o0 = n + len(c_ins)
        buf = refs[o0:o0 + n]
        send, recv = refs[o0 + n + len(c_outs):o0 + n + len(c_outs) + 2]
        extra = (refs[n:o0], refs[o0 + n:o0 + n + len(c_outs)], refs[o0 + n + len(c_outs) + 2:])
        x, y, c = _mesh_pos()
        _handshake(peers)
        if also:
            also.copies_start(*extra)
        cps = [_remote(buf[t].at[:, c], buf[t].at[:, c], send.at[t], recv.at[t], (x, y, 1 - c)) for t in range(n)]
        for cp in cps:
            cp.start()
        for t in range(n):
            _remote(buf[t].at[:, c], buf[t].at[:, 1 - c], send.at[t], recv.at[t], (x, y, 1 - c)).wait_recv()
        for cp in cps:
            cp.wait_send()
        if also:
            also.finish(*extra)

    sem = pltpu.SemaphoreType.DMA
    res = pl.pallas_call(
        body, name="join_halves", in_specs=[ANY] * (n + len(c_ins)), out_specs=[ANY] * (n + len(c_outs)),
        out_shape=[_sds(a.shape, a.dtype) for a in accs] + list(c_outs),
        scratch_shapes=[sem((n,)), sem((n,))] + list(c_sems),
        input_output_aliases={t: t for t in range(n)},
        compiler_params=_cp(has_side_effects=True, collective_id=PEER_SETS[peers]),
    )(*accs, *c_ins)
    return (list(res[:n]), list(res[n:])) if also else res


def _sum_siblings(pgs, rbs, c_arr):
    n = len(pgs)
    nk = pgs[0].shape[0]

    def body(c_ref, *refs):
        for t in range(n):
            refs[2 * n + t][...] = (refs[t][...].astype(F32) + refs[n + t][...].astype(F32)).astype(BF16)

    half = lambda a: pl.BlockSpec((None,) + a.shape[2:], lambda k, c_ref: (k, 0, 0))
    return pl.pallas_call(
        body, name="sum_siblings",
        grid_spec=pltpu.PrefetchScalarGridSpec(
            num_scalar_prefetch=1, grid=(nk,),
            in_specs=[pl.BlockSpec((None, None) + a.shape[2:], lambda k, c_ref: (k, c_ref[0], 0, 0)) for a in pgs]
            + [half(a) for a in pgs],
            out_specs=[half(a) for a in pgs]),
        out_shape=[_sds((nk,) + a.shape[2:], BF16) for a in pgs],
        compiler_params=_cp(dimension_semantics=("arbitrary",)),
    )(c_arr, *pgs, *rbs)


def _sum_chips(ps, rbs, kcl, accs):
    n = len(ps)

    def body(k_ref, *refs):
        for t in range(n):
            b_ref = refs[n + t]
            refs[3 * n + t][...] = (refs[t][...].astype(F32) + b_ref[0].astype(F32) + b_ref[1].astype(F32)
                                    + b_ref[2].astype(F32))

    qr = lambda a: (a.shape[1] // 2, a.shape[2])
    return pl.pallas_call(
        body, name="sum_chips",
        grid_spec=pltpu.PrefetchScalarGridSpec(
            num_scalar_prefetch=1, grid=(2,),
            in_specs=[pl.BlockSpec((None,) + qr(a), lambda r, k_ref: (k_ref[0], r, 0)) for a in ps]
            + [pl.BlockSpec((3,) + qr(a), lambda r, k_ref: (0, r, 0)) for a in ps] + [ANY] * n,
            out_specs=[pl.BlockSpec((None, None) + qr(a), lambda r, k_ref: (k_ref[2], k_ref[1], r, 0)) for a in ps]),
        out_shape=[_sds(a.shape, F32) for a in accs], input_output_aliases={1 + 2 * n + t: t for t in range(n)},
        compiler_params=_cp(dimension_semantics=("arbitrary",)),
    )(kcl, *ps, *rbs, *accs)


N_DEV = 8


def _allreduce_small(parts):
    n = len(parts)

    def body(*refs):
        p_refs, o_refs, rbufs = refs[:n], refs[n:2 * n], refs[2 * n:3 * n]
        s1, r1, s2, r2 = refs[3 * n:]
        x, y, c = _mesh_pos()
        me = 4 * x + 2 * y + c
        devs = [(d // 4, (d // 2) % 2, d % 2) for d in range(N_DEV)]
        for q in range(n):
            rbufs[q][me] = p_refs[q][me]

        def each_peer(fn):
            for d in range(N_DEV):
                @pl.when(d != me)
                def _():
                    for q in range(n):
                        fn(d, q)

        def first(d, q, to_me):
            return _remote(p_refs[q].at[d], rbufs[q].at[d if to_me else me], s1.at[q, d], r1.at[q, d if to_me else me],
                           devs[d])

        def second(d, q, to_me):
            blk = d if to_me else me
            return _remote(o_refs[q].at[blk], o_refs[q].at[blk], s2.at[q, d], r2.at[q, blk], devs[d])

        each_peer(lambda d, q: first(d, q, False).start())
        each_peer(lambda d, q: first(d, q, True).wait_recv())
        for q in range(n):
            total = rbufs[q][0].astype(F32)
            for d in range(1, N_DEV):
                total = total + rbufs[q][d].astype(F32)
            o_refs[q][me] = total.astype(o_refs[q].dtype)
        each_peer(lambda d, q: second(d, q, False).start())
        each_peer(lambda d, q: second(d, q, True).wait_recv())
        each_peer(lambda d, q: first(d, q, False).wait_send())
        each_peer(lambda d, q: second(d, q, False).wait_send())

    sem = pltpu.SemaphoreType.DMA
    vm = pl.BlockSpec(memory_space=pltpu.VMEM)
    return pl.pallas_call(
        body, name="allreduce_small", in_specs=[vm] * n, out_specs=[vm] * n,
        out_shape=[_sds(a.shape, a.dtype) for a in parts],
        scratch_shapes=[pltpu.VMEM(a.shape, a.dtype) for a in parts] + [sem((n, N_DEV))] * 4,
        compiler_params=_cp(has_side_effects=True),
    )(*parts)


BIG = ("w_in", "w_1", "w_a_out", "w_b_out", "w_o", "w_2")
BY_COLS = {"w_in": True, "w_1": True, "w_a_out": False, "w_b_out": False, "w_o": False, "w_2": False}
WEIGHTS = ("g_mix", "w_in", "b_in", "conv_a_w", "conv_a_b", "ln_g", "ln_b", "w_a_out", "conv_b_w", "conv_b_b", "w_rg_a",
           "b_rg_a", "w_rg_x", "b_rg_x", "lam", "w_b_out", "w_o", "g_mlp", "w_1", "w_2", "g_final")
SMALL = tuple(n for n in WEIGHTS if n not in BIG)
ADAM_ROWS = 256
ADAM_SMALL_ROWS = 2048


def _block_diag(w):
    nh, dh, _ = w.shape
    ng = nh // HEADS_PER_GROUP
    w4 = w.reshape(ng, HEADS_PER_GROUP, dh, dh)
    eye = jnp.eye(HEADS_PER_GROUP, dtype=w.dtype)
    return jnp.einsum("qhij,hk->qhikj", w4, eye).reshape(ng, HEADS_PER_GROUP * dh, HEADS_PER_GROUP * dh)


def _block_diag_part(d, dh):
    ng = d.shape[0]
    eye = jnp.eye(HEADS_PER_GROUP, dtype=d.dtype)
    d5 = d.reshape(ng, HEADS_PER_GROUP, dh, HEADS_PER_GROUP, dh)
    return jnp.einsum("qhikj,hk->qhij", d5, eye).reshape(ng * HEADS_PER_GROUP, dh, dh)


PACK_LANES = 128


def _pack(arrays, blocks, tile_rows):
    parts = [a.reshape(-1, PACK_LANES) for a in arrays]
    parts = [jnp.pad(p, ((0, -p.shape[0] % tile_rows), (0, 0))) if p.shape[0] % tile_rows else p for p in parts]
    rows = sum(p.shape[0] for p in parts)
    pad = -rows % (blocks * tile_rows)
    if pad:
        parts.append(jnp.zeros((pad, PACK_LANES), parts[0].dtype))
    return jnp.concatenate(parts, axis=0).reshape(blocks, -1, PACK_LANES)


def _unpack(buf, like, tile_rows):
    buf = buf.reshape(-1, PACK_LANES)
    out, off = [], 0
    for a in like:
        n = a.size // PACK_LANES
        out.append(buf[off:off + n].reshape(a.shape))
        off += n + (-n % tile_rows)
    return out


def kernel(x, g_mix, w_in, b_in, conv_a_w, conv_a_b, ln_g, ln_b, w_a_out, conv_b_w, conv_b_b, w_rg_a, b_rg_a, w_rg_x, b_rg_x, lam, w_b_out, w_o, g_mlp, w_1, w_2, g_final, loss_target, m_g_mix, m_w_in, m_b_in, m_conv_a_w, m_conv_a_b, m_ln_g, m_ln_b, m_w_a_out, m_conv_b_w, m_conv_b_b, m_w_rg_a, m_b_rg_a, m_w_rg_x, m_b_rg_x, m_lam, m_w_b_out, m_w_o, m_g_mlp, m_w_1, m_w_2, m_g_final, v_g_mix, v_w_in, v_b_in, v_conv_a_w, v_conv_a_b, v_ln_g, v_ln_b, v_w_a_out, v_conv_b_w, v_conv_b_b, v_w_rg_a, v_b_rg_a, v_w_rg_x, v_b_rg_x, v_lam, v_w_b_out, v_w_o, v_g_mlp, v_w_1, v_w_2, v_g_final):
    w = dict(g_mix=g_mix, w_in=w_in, b_in=b_in, conv_a_w=conv_a_w, conv_a_b=conv_a_b, ln_g=ln_g, ln_b=ln_b, w_a_out=w_a_out,
             conv_b_w=conv_b_w, conv_b_b=conv_b_b, w_rg_a=w_rg_a, b_rg_a=b_rg_a, w_rg_x=w_rg_x, b_rg_x=b_rg_x, lam=lam,
             w_b_out=w_b_out, w_o=w_o, g_mlp=g_mlp, w_1=w_1, w_2=w_2, g_final=g_final)
    m = dict(g_mix=m_g_mix, w_in=m_w_in, b_in=m_b_in, conv_a_w=m_conv_a_w, conv_a_b=m_conv_a_b, ln_g=m_ln_g, ln_b=m_ln_b,
             w_a_out=m_w_a_out, conv_b_w=m_conv_b_w, conv_b_b=m_conv_b_b, w_rg_a=m_w_rg_a, b_rg_a=m_b_rg_a, w_rg_x=m_w_rg_x,
             b_rg_x=m_b_rg_x, lam=m_lam, w_b_out=m_w_b_out, w_o=m_w_o, g_mlp=m_g_mlp, w_1=m_w_1, w_2=m_w_2, g_final=m_g_final)
    v = dict(g_mix=v_g_mix, w_in=v_w_in, b_in=v_b_in, conv_a_w=v_conv_a_w, conv_a_b=v_conv_a_b, ln_g=v_ln_g, ln_b=v_ln_b,
             w_a_out=v_w_a_out, conv_b_w=v_conv_b_w, conv_b_b=v_conv_b_b, w_rg_a=v_w_rg_a, b_rg_a=v_b_rg_a, w_rg_x=v_w_rg_x,
             b_rg_x=v_b_rg_x, lam=v_lam, w_b_out=v_w_b_out, w_o=v_w_o, g_mlp=v_g_mlp, w_1=v_w_1, w_2=v_w_2, g_final=v_g_final)
    B, S, D = x.shape
    T = B * S
    L = w_in.shape[0]
    dh = w_rg_a.shape[-1]
    taps_a, taps_b = conv_a_w.shape[1], conv_b_w.shape[1]
    assert (taps_a, taps_b) == (TAPS_A, TAPS_B)
    xi, yi, ci = _mesh_pos()
    c_arr = jnp.reshape(ci, (1,)).astype(jnp.int32)
    k_me = 2 * xi + yi

    caw_p = jnp.pad(conv_a_w, ((0, 0), (0, 32 - taps_a), (0, 0)))
    cbw_p = jnp.pad(conv_b_w, ((0, 0), (0, 8 - taps_b), (0, 0)))
    row = lambda a: a.reshape(1, -1)

    def shards_of(l):
        d = {n: w[n][l].astype(BF16) for n in BIG}
        d.update(caw=caw_p[l], cbw=cbw_p[l])
        return d

    def params_of(l, whole):
        p = dict(whole, cab=row(conv_a_b[l]), cbb=row(conv_b_b[l]),
                 wa=_block_diag(w_rg_a[l]).astype(BF16), wx=_block_diag(w_rg_x[l]).astype(BF16))
        for n in ("g_mix", "b_in", "ln_g", "ln_b", "b_rg_a", "b_rg_x", "lam", "g_mlp"):
            p[n] = row(w[n][l])
        return p

    shards = [shards_of(l) for l in range(L)]
    whole = dict(zip(["w_in"], _comm_call("gather_first", _gather_comm([shards[0]["w_in"]], [GATHER_KIND["w_in"]]))))
    xf = x.reshape(T, D)
    saved, params = [], []
    for l in range(L):
        cur = lambda names: [(n, shards[l][n], False) for n in names]
        nxt = lambda names: [(n, shards[l + 1][n], True) for n in names]
        if l == 0:
            jobs = {"fwd_z": cur(["w_a_out", "w_b_out", "w_o", "caw", "cbw"]), "conv_a_fwd": cur(["w_1"]),
                    "fwd_f": cur(["w_2"])}
        else:
            jobs = {"fwd_z": cur(["w_1"]), "conv_a_fwd": cur(["w_2"])}
        if l + 1 < L:
            jobs.update({"rglru_fwd": nxt(["w_in"]), "fwd_x1": nxt(["w_o", "caw", "cbw"]),
                         "fwd_x2": nxt(["w_b_out", "w_a_out"])})
        xf, s, p, whole = _layer_fwd(xf, params_of(l, whole), S, jobs)
        saved.append(s)
        params.append(p)
    loss_part, dx, dxb, dg_final = _loss_head(xf, row(g_final), loss_target.reshape(T, D), _tiles(T)[0])
    loss = lax.psum(loss_part[0, 0], ("x", "y", "c"))

    half_shape = lambda a: (L, 2, a.shape[1] // 2, a.shape[2])
    accs = {n: lax.empty(half_shape(w[n]), F32) for n in BIG}
    small = {n: [None] * L for n in SMALL if n != "g_final"}
    red = _Reduce(accs, c_arr, lambda l: jnp.stack([k_me, ci, jnp.full((), l, ci.dtype)]).astype(jnp.int32))
    for l in reversed(range(L)):
        dx, dxb, g = _layer_bwd(dx, dxb, params[l], saved[l], S, red=red, layer=l)
        small["g_mix"][l], small["b_in"][l], small["g_mlp"][l] = g["g_mix"], g["b_in"], g["g_mlp"]
        small["conv_a_w"][l], small["conv_a_b"][l] = g["caw"], g["cab"]
        small["conv_b_w"][l], small["conv_b_b"][l] = g["cbw"], g["cbb"]
        small["ln_g"][l], small["ln_b"][l], small["lam"][l] = g["ln_g"], g["ln_b"], g["lam"]
        small["w_rg_a"][l], small["w_rg_x"][l] = _block_diag_part(g["wa"], dh), _block_diag_part(g["wx"], dh)
        small["b_rg_a"][l], small["b_rg_x"][l] = g["b_rg_a"], g["b_rg_x"]
    grad_x = dx.reshape(B, S, D)

    delta, new_m, new_v = {}, {}, {}
    flat = lambda a: a.reshape(-1, a.shape[-1])

    def adam_big(names, comm=None):
        r = _adamw("adamw_" + names[0], *[[flat(d[n]) for n in names] for d in (w, grads, m, v)], ADAM_ROWS, comm=comm)
        for q, n in enumerate(names):
            delta[n], new_m[n], new_v[n] = (r[a][q].reshape(w[n].shape) for a in range(3))
        return r[3] if comm else None

    late_sums = red.chip_sums(red.late, _comm_call("swap_halves", _swap_comm(red.late)))
    joined, got = _join_halves([red.accs[n] for n in red.EARLY], also=_scatter_comm(late_sums))
    grads = {n: a.reshape(w[n].shape) for n, a in zip(red.EARLY, joined)}
    adam_big(["w_2", "w_1"])
    adam_big(["w_o", "w_a_out"])
    red.finish(red.LATE, late_sums, got, 0)
    joined = _join_halves([red.accs[n] for n in red.LATE])
    grads.update({n: a.reshape(w[n].shape) for n, a in zip(red.LATE, joined)})
    adam_big(["w_b_out"])
    adam_big(["w_in"])

    wide = ["w_rg_a", "w_rg_x"]
    names = [n for n in SMALL if n != "g_final" and n not in wide]
    parts = [jnp.stack(small[n]) for n in names] + [dg_final]
    parts_w = [jnp.stack(small[n]).astype(BF16) for n in wide]
    total, total_w = _allreduce_small([_pack(parts, N_DEV, 8), _pack(parts_w, N_DEV, 16)])
    summed = _unpack(total, parts, 8) + [a.astype(F32) for a in _unpack(total_w, parts_w, 16)]
    for n, a in zip(names + ["g_final"] + wide, summed):
        if n == "conv_a_w":
            a = lax.dynamic_slice_in_dim(a[:, :taps_a], k_me * conv_a_w.shape[2], conv_a_w.shape[2], axis=2)
        elif n == "conv_b_w":
            a = lax.dynamic_slice_in_dim(a[:, :taps_b], k_me * conv_b_w.shape[2], conv_b_w.shape[2], axis=2)
        grads[n] = a.reshape(w[n].shape)

    for n in SMALL:
        cols = w[n].shape[-1]
        view = lambda a: a.reshape(-1, cols)
        rows = view(w[n]).shape[0]
        d_, m_, v_ = _adamw("adamw_" + n, view(w[n]), view(grads[n]), view(m[n]), view(v[n]),
                            ADAM_SMALL_ROWS if rows % ADAM_SMALL_ROWS == 0 else rows)
        delta[n], new_m[n], new_v[n] = (a.reshape(w[n].shape) for a in (d_, m_, v_))

    return (loss, grad_x, *[grads[n] for n in WEIGHTS], *[delta[n] for n in WEIGHTS],
            *[new_m[n] for n in WEIGHTS], *[new_v[n] for n in WEIGHTS])
```

```python
import jax
import jax.numpy as jnp
from jax import lax
from jax.experimental import pallas as pl
from jax.experimental.pallas import tpu as pltpu

F32 = jnp.float32
BF16 = jnp.bfloat16
MESH = pl.DeviceIdType.MESH

EPS = 1e-6
LRU_C = 8.0
ADAM_LR, ADAM_B1, ADAM_B2, ADAM_EPS, ADAM_WD, ADAM_STEP = 0.001, 0.9, 0.999, 1e-08, 0.01, 10

N_CHIPS = 4
HEADS_PER_GROUP = 4
VMEM_LIMIT = 56 * 1024 * 1024


def _cp(**kw):
    return pltpu.CompilerParams(vmem_limit_bytes=VMEM_LIMIT, **kw)


def _sig(x):
    return 1.0 / (1.0 + jnp.exp(-x))


def _gelu(x):
    t = jnp.tanh(0.7978845608028654 * (x + 0.044715 * x * x * x))
    return 0.5 * x * (1.0 + t), t


def _gelu_grad(x, t):
    dt = (1.0 - t * t) * 0.7978845608028654 * (1.0 + 3.0 * 0.044715 * x * x)
    return 0.5 * (1.0 + t) + 0.5 * x * dt


def _rms(xf, g):
    r = lax.rsqrt(jnp.mean(xf * xf, axis=-1, keepdims=True) + EPS)
    return xf * r * g, r


def _rms_bwd(xf, g, r, dh):
    dyg = dh * g
    dx = r * (dyg - xf * (r * r) * jnp.mean(dyg * xf, axis=-1, keepdims=True))
    return dx, dh * xf * r


def _ln_silu(u, g, b):
    mu = jnp.mean(u, axis=-1, keepdims=True)
    uc = u - mu
    rstd = lax.rsqrt(jnp.mean(uc * uc, axis=-1, keepdims=True) + EPS)
    uh = uc * rstd
    u2 = uh * g + b
    s = _sig(u2)
    return u2 * s, uh, rstd, u2, s


_DIMS = {"nn": (((1,), (0,)), ((), ())), "nt": (((1,), (1,)), ((), ())), "tn": (((0,), (0,)), ((), ()))}


PEER_SETS = {"chips+sibling": 1, "chips": 2, "sibling": 3}


def _handshake(peers):
    x, y, c = _mesh_pos()
    devs = ([(*chip, c) for chip in _other_chips(x, y)] if "chips" in peers else []) + \
           ([(x, y, 1 - c)] if "sibling" in peers else [])
    barrier = pltpu.get_barrier_semaphore()
    for dev in devs:
        pl.semaphore_signal(barrier, inc=1, device_id=dev, device_id_type=MESH)
    pl.semaphore_wait(barrier, len(devs))


class _Comm:
    def __init__(self, ins, outs, sems, start, finish, peers):
        self.ins, self.outs, self.sems, self.finish, self.peers = list(ins), list(outs), list(sems), finish, peers
        self.copies_start = start

    def start(self, *refs):
        _handshake(self.peers)
        self.copies_start(*refs)

    @property
    def collective_id(self):
        return PEER_SETS[self.peers]


def _resident(shape):
    return pl.BlockSpec(shape, lambda i, j, k: (0,) * len(shape), pipeline_mode=pl.Buffered(1))


def _mm(name, mode, grid, a_ins, a_fn, b_in, e_ins, epi, outs, acc_shape, cache_a=None, alias=(), extra_scratch=(),
        comm=None, b_slice=None):
    ni, nj, nk = grid
    na, ne, no = len(a_ins), len(e_ins), len(outs)
    assert cache_a is None or nk == 1
    n_fixed = (nk > 1) + (cache_a is not None)
    n_in = na + 1 + ne + len(alias)
    c_ins, c_outs, c_sems = (comm.ins, comm.outs, comm.sems) if comm else ([], [], [])

    def body(*refs):
        a_refs = refs[:na]
        b_ref = refs[na]
        e_refs = refs[na + 1:na + 1 + ne]
        comm_in = refs[n_in:n_in + len(c_ins)]
        out0 = n_in + len(c_ins)
        out_refs = refs[out0:out0 + no]
        comm_out = refs[out0 + no:out0 + no + len(c_outs)]
        scratch = refs[out0 + no + len(c_outs):]
        extra = scratch[n_fixed:n_fixed + len(extra_scratch)]
        comm_sems = scratch[n_fixed + len(extra_scratch):]
        i, j, k = pl.program_id(0), pl.program_id(1), pl.program_id(2)
        if comm:
            @pl.when((i == 0) & (j == 0) & (k == 0))
            def _():
                comm.start(comm_in, comm_out, comm_sems)
        if cache_a is not None:
            cache_ref = scratch[n_fixed - 1]

            @pl.when(j == 0)
            def _():
                cache_ref[...] = a_fn(a_refs, out_refs, i, j, k)

            a = cache_ref[...]
        else:
            a = a_fn(a_refs, out_refs, i, j, k)
        if b_slice is None:
            b = b_ref[...]
        elif b_slice[0] == "cols":
            b = b_ref[:, pl.ds(pl.multiple_of(j * b_slice[1], b_slice[1]), b_slice[1])]
        else:
            b = b_ref[pl.ds(pl.multiple_of(j * b_slice[1], b_slice[1]), b_slice[1]), :]
        prod = lax.dot_general(a, b, _DIMS[mode], preferred_element_type=F32)
        if nk == 1:
            epi(prod, e_refs, out_refs, i, j, extra)
        else:
            acc_ref = scratch[0]

            @pl.when(k == 0)
            def _():
                acc_ref[...] = prod

            @pl.when(k > 0)
            def _():
                acc_ref[...] += prod

            @pl.when(k == nk - 1)
            def _():
                epi(acc_ref[...], e_refs, out_refs, i, j, extra)

        if comm:
            @pl.when((i == ni - 1) & (j == nj - 1) & (k == nk - 1))
            def _():
                comm.finish(comm_in, comm_out, comm_sems)

    scratch_shapes = []
    if nk > 1:
        scratch_shapes.append(pltpu.VMEM(acc_shape, F32))
    if cache_a is not None:
        scratch_shapes.append(pltpu.VMEM(cache_a, BF16))
    any_spec = pl.BlockSpec(memory_space=pl.ANY)
    ins = (list(a_ins) + [b_in] + list(e_ins) + [(arr, any_spec) for arr, _ in alias] + [(arr, any_spec) for arr in c_ins])
    first_alias = na + 1 + ne
    res = pl.pallas_call(
        body, name=name, grid=grid,
        in_specs=[s for _, s in ins], out_specs=[s for _, s in outs] + [any_spec] * len(c_outs),
        out_shape=[o for o, _ in outs] + list(c_outs),
        scratch_shapes=scratch_shapes + list(extra_scratch) + list(c_sems),
        input_output_aliases={first_alias + n: o for n, (_, o) in enumerate(alias)},
        compiler_params=_cp(dimension_semantics=("arbitrary", "arbitrary", "arbitrary"), has_side_effects=bool(comm),
                            collective_id=comm.collective_id if comm else None),
    )(*[a for a, _ in ins])
    if comm:
        return list(res[:no]), list(res[no:])
    return res


def _bs(shape, fn):
    return pl.BlockSpec(shape, fn)


def _sds(shape, dt):
    return jax.ShapeDtypeStruct(shape, dt)


def _acc_rows(ref, val, first):
    @pl.when(first)
    def _():
        ref[...] = val

    @pl.when(jnp.logical_not(first))
    def _():
        ref[...] += val


def _fwd_norm_mm(name, x, g, w, bias, tm, tn, comm=None):
    T, D = x.shape
    N = w.shape[1]

    def a_fn(a_refs, out_refs, i, j, k):
        h, _ = _rms(a_refs[0][...], a_refs[1][...])
        hb = h.astype(BF16)
        out_refs[1][...] = hb
        return hb

    def epi(acc, e_refs, out_refs, i, j, extra):
        if bias is not None:
            acc = acc + e_refs[0][...]
        out_refs[0][...] = acc.astype(BF16)

    e_ins = [] if bias is None else [(bias, _bs((1, tn), lambda i, j, k: (0, j)))]
    return _mm(name, "nn", (T // tm, N // tn, 1),
               [(x, _bs((tm, D), lambda i, j, k: (i, 0))), (g, _bs((1, D), lambda i, j, k: (0, 0)))], a_fn,
               (w, _resident((D, N))), e_ins, epi,
               [(_sds((T, N), BF16), _bs((tm, tn), lambda i, j, k: (i, j))),
                (_sds((T, D), BF16), _bs((tm, D), lambda i, j, k: (i, 0)))],
               None, cache_a=(tm, D), comm=comm, b_slice=("cols", tn))


def _fwd_ya(u1, ln_g, ln_b, w, tm):
    T, C = u1.shape
    N = w.shape[1]

    def a_fn(a_refs, out_refs, i, j, k):
        u3 = _ln_silu(a_refs[0][...].astype(F32), a_refs[1][...], a_refs[2][...])[0].astype(BF16)
        out_refs[1][...] = u3
        return u3

    def epi(acc, e_refs, out_refs, i, j, extra):
        out_refs[0][...] = acc.astype(BF16)

    row = _bs((1, C), lambda i, j, k: (0, 0))
    tc = _bs((tm, C), lambda i, j, k: (i, 0))
    return _mm("fwd_ya", "nn", (T // tm, 1, 1), [(u1, tc), (ln_g, row), (ln_b, row)], a_fn,
               (w, _bs((C, N), lambda i, j, k: (0, 0))), [], epi,
               [(_sds((T, N), BF16), _bs((tm, N), lambda i, j, k: (i, 0))), (_sds((T, C), BF16), tc)], None)


def _fwd_yb(h, z, gb_blk, w, tm, tk):
    T, C = h.shape
    N = w.shape[1]

    def a_fn(a_refs, out_refs, i, j, k):
        ge, _ = _gelu(a_refs[1][...].astype(F32))
        pv = (a_refs[0][...].astype(F32) * ge).astype(BF16)
        out_refs[1][...] = pv
        return pv

    def epi(acc, e_refs, out_refs, i, j, extra):
        out_refs[0][...] = acc.astype(BF16)

    tk_ = _bs((tm, tk), lambda i, j, k: (i, k))
    return _mm("fwd_yb", "nn", (T // tm, 1, C // tk),
               [(h, tk_), (z, _bs((tm, tk), lambda i, j, k: (i, gb_blk + k)))], a_fn,
               (w, _bs((tk, N), lambda i, j, k: (k, 0))), [], epi,
               [(_sds((T, N), BF16), _bs((tm, N), lambda i, j, k: (i, 0))), (_sds((T, C), BF16), tk_)], (tm, N))


def _fwd_x1(x, ya, yb, z, sa_blk, w, tm, comm=None):
    T, D = x.shape

    def a_fn(a_refs, out_refs, i, j, k):
        ya_, yb_, sa_, sb_ = (r[...].astype(F32) for r in a_refs)
        mg = (_sig(sa_) * ya_ + _sig(sb_) * yb_).astype(BF16)
        out_refs[1][...] = mg
        return mg

    def epi(acc, e_refs, out_refs, i, j, extra):
        out_refs[0][...] = e_refs[0][...] + acc

    t = _bs((tm, D), lambda i, j, k: (i, 0))
    return _mm("fwd_x1", "nn", (T // tm, 1, 1),
               [(ya, t), (yb, t), (z, _bs((tm, D), lambda i, j, k: (i, sa_blk))),
                (z, _bs((tm, D), lambda i, j, k: (i, sa_blk + 1)))], a_fn,
               (w, _bs((D, D), lambda i, j, k: (0, 0))), [(x, t)], epi,
               [(_sds((T, D), F32), t), (_sds((T, D), BF16), t)], None, comm=comm)


def _fwd_x2(x1, fp, w, tm, tk, comm=None):
    T, D = x1.shape
    Fd = fp.shape[1]

    def epi(acc, e_refs, out_refs, i, j, extra):
        out_refs[0][...] = e_refs[0][...] + acc

    t = _bs((tm, D), lambda i, j, k: (i, 0))
    whole_k = tk == Fd
    r = _mm("fwd_x2", "nn", (T // tm, 1, Fd // tk),
            [(fp, _bs((tm, tk), lambda i, j, k: (i, k)))], _relu2,
            (w, _resident((Fd, D)) if whole_k else _bs((tk, D), lambda i, j, k: (k, 0))), [(x1, t)], epi,
            [(_sds((T, D), F32), t)], (tm, D), comm=comm)
    return (r[0][0], r[1]) if comm else r[0]


def _relu2(a_refs, out_refs, i, j, k):
    f = jnp.maximum(a_refs[0][...], 0.0)
    return f * f


def _loss_head(x, g, target, tm):
    T, D = x.shape

    def body(x_ref, g_ref, t_ref, loss_ref, dx_ref, dxb_ref, dg_ref):
        i = pl.program_id(0)
        xf, gv = x_ref[...], g_ref[...]
        y, r = _rms(xf, gv)
        err = y - t_ref[...]
        part = 0.5 * jnp.sum(jnp.mean(err * err, axis=-1, keepdims=True), axis=0, keepdims=True)
        dx, dg_rows = _rms_bwd(xf, gv, r, err * (1.0 / D))
        dx_ref[...] = dx
        dxb_ref[...] = dx.astype(BF16)
        _acc_rows(loss_ref, jnp.broadcast_to(part, (1, 128)), i == 0)
        _acc_rows(dg_ref, jnp.sum(dg_rows, axis=0, keepdims=True), i == 0)

    t = _bs((tm, D), lambda i: (i, 0))
    row = _bs((1, D), lambda i: (0, 0))
    return pl.pallas_call(
        body, name="loss_head", grid=(T // tm,), in_specs=[t, row, t],
        out_specs=[_bs((1, 128), lambda i: (0, 0)), t, t, row],
        out_shape=[_sds((1, 128), F32), _sds((T, D), F32), _sds((T, D), BF16), _sds((1, D), F32)],
        compiler_params=_cp(dimension_semantics=("arbitrary",)),
    )(x, g, target)


def _adamw(name, w, g, m, v, tr, comm=None):
    many = isinstance(w, (list, tuple))
    ws, gs, ms, vs = (list(a) if many else [a] for a in (w, g, m, v))
    n = len(ws)
    rows, cols = ws[0].shape
    d1 = 1.0 - ADAM_B1 ** ADAM_STEP
    d2 = 1.0 - ADAM_B2 ** ADAM_STEP

    def body(*refs):
        for q in range(n):
            w_ref, g_ref, m_ref, v_ref = (refs[a * n + q] for a in range(4))
            d_ref, mo_ref, vo_ref = (refs[(4 + a) * n + q] for a in range(3))
            gv = g_ref[...]
            mn = ADAM_B1 * m_ref[...] + (1.0 - ADAM_B1) * gv
            vn = ADAM_B2 * v_ref[...] + (1.0 - ADAM_B2) * (gv * gv)
            d_ref[...] = -ADAM_LR * ((mn / d1) / (jnp.sqrt(vn / d2) + ADAM_EPS) + ADAM_WD * w_ref[...])
            mo_ref[...] = mn
            vo_ref[...] = vn

    t = _bs((tr, cols), lambda i: (i, 0))
    r = _call_with_comm(name, body, (rows // tr,), ws + gs + ms + vs, [t] * (4 * n), [t] * (3 * n),
                        [_sds((rows, cols), F32)] * (3 * n), [], comm)
    outs, got = (r if comm else (r, None))
    res = [outs[a * n:(a + 1) * n] if many else outs[a * n] for a in range(3)]
    return (*res, got) if comm else tuple(res)


def _ident(a_refs, out_refs, i, j, k):
    return a_refs[0][...]


def _bwd_dw(name, act, dy, ti, tj, tm, a_fn=None, a_extra=(), shard_cols=None, keep=None):
    T, J = dy.shape
    I = act.shape[1]

    def epi(acc, e_refs, out_refs, i, j, extra):
        out_refs[0][...] = acc.astype(BF16).reshape(out_refs[0].shape)

    if shard_cols is None:
        out = (_sds((I, J), BF16), _bs((ti, tj), lambda i, j, k: (i, j)))
    else:
        per = shard_cols // tj
        assert ti == I and per * tj == shard_cols
        out = (_sds((J // shard_cols, 2, I // 2, shard_cols), BF16),
               _bs((None, 2, I // 2, tj), lambda i, j, k: (lax.div(j, per), 0, 0, lax.rem(j, per))))
    assert keep is None or tm == T
    a_spec = _resident((T, I)) if keep == "act" else _bs((tm, ti), lambda i, j, k: (k, i))
    b_spec = _resident((T, J)) if keep == "dy" else _bs((tm, tj), lambda i, j, k: (k, j))
    return _mm(name, "tn", (I // ti, J // tj, T // tm), [(act, a_spec)] + list(a_extra), a_fn or _ident,
               (dy, b_spec), [], epi, [out], (ti, tj))[0]


def _bwd_df(dxb, w2, fp, tm, tn, comm=None):
    T, D = dxb.shape
    Fd = w2.shape[0]

    def epi(acc, e_refs, out_refs, i, j, extra):
        out_refs[0][...] = (acc * (2.0 * jnp.maximum(e_refs[0][...].astype(F32), 0.0))).astype(BF16)

    t = _bs((tm, tn), lambda i, j, k: (i, j))
    r = _mm("bwd_df", "nt", (T // tm, Fd // tn, 1), [(dxb, _bs((tm, D), lambda i, j, k: (i, 0)))], _ident,
            (w2, _resident((Fd, D))), [(fp, t)], epi, [(_sds((T, Fd), BF16), t)], None, b_slice=("rows", tn), comm=comm)
    return (r[0][0], r[1]) if comm else r[0]


def _bwd_norm(name, dy, w, x, g, dres, tm, tk, colsum=False, comm=None):
    T, K = dy.shape
    D = w.shape[0]
    nk = K // tk

    def a_fn(a_refs, out_refs, i, j, k):
        a = a_refs[0][...]
        if colsum:
            s = jnp.sum(a.astype(F32), axis=0, keepdims=True)

            @pl.when(i == 0)
            def _():
                out_refs[3][k] = s

            @pl.when(i > 0)
            def _():
                out_refs[3][k] += s
        return a

    def epi(acc, e_refs, out_refs, i, j, extra):
        xf, gv = e_refs[0][...], e_refs[1][...]
        r = lax.rsqrt(jnp.mean(xf * xf, axis=-1, keepdims=True) + EPS)
        dx, dg_rows = _rms_bwd(xf, gv, r, acc)
        dx = dx + e_refs[2][...]
        out_refs[0][...] = dx
        out_refs[1][...] = dx.astype(BF16)
        _acc_rows(out_refs[2], jnp.sum(dg_rows, axis=0, keepdims=True), i == 0)

    t = _bs((tm, D), lambda i, j, k: (i, 0))
    row = _bs((1, D), lambda i, j, k: (0, 0))
    outs = [(_sds((T, D), F32), t), (_sds((T, D), BF16), t), (_sds((1, D), F32), row)]
    if colsum:
        outs.append((_sds((nk, 1, tk), F32), _bs((nk, 1, tk), lambda i, j, k: (0, 0, 0))))
    return _mm(name, "nt", (T // tm, 1, nk), [(dy, _bs((tm, tk), lambda i, j, k: (i, k)))], a_fn,
               (w, _resident((D, K)) if nk == 1 else _bs((D, tk), lambda i, j, k: (0, k))),
               [(x, t), (g, row), (dres, t)], epi, outs, (tm, D), comm=comm)


def _bwd_dm(dxb, w_o, ya, yb, z, sa_blk, tm):
    T, D = dxb.shape

    def epi(acc, e_refs, out_refs, i, j, extra):
        ya_, yb_, sa_, sb_ = (r[...].astype(F32) for r in e_refs)
        ga, gb = _sig(sa_), _sig(sb_)
        out_refs[0][...] = (acc * ga).astype(BF16)
        out_refs[1][...] = (acc * gb).astype(BF16)
        stage, sem = extra
        put = pltpu.make_async_copy(
            stage, out_refs[2].at[pl.ds(pl.multiple_of(i * tm, tm), tm), pl.ds(sa_blk * D, 2 * D)], sem)

        @pl.when(i > 0)
        def _():
            put.wait()

        stage[:, 0:D] = (acc * ya_ * ga * (1.0 - ga)).astype(BF16)
        stage[:, D:2 * D] = (acc * yb_ * gb * (1.0 - gb)).astype(BF16)
        put.start()

        @pl.when(i == T // tm - 1)
        def _():
            put.wait()

    t = _bs((tm, D), lambda i, j, k: (i, 0))
    return _mm("bwd_dm", "nt", (T // tm, 1, 1), [(dxb, t)], _ident, (w_o, _bs((D, D), lambda i, j, k: (0, 0))),
               [(ya, t), (yb, t), (z, _bs((tm, D), lambda i, j, k: (i, sa_blk))),
                (z, _bs((tm, D), lambda i, j, k: (i, sa_blk + 1)))], epi,
               [(_sds((T, D), BF16), t), (_sds((T, D), BF16), t),
                (_sds(z.shape, BF16), pl.BlockSpec(memory_space=pl.ANY))], None,
               extra_scratch=[pltpu.VMEM((tm, 2 * D), BF16), pltpu.SemaphoreType.DMA(())])


def _bwd_du3(dya, w, u1, ln_g, ln_b, tm):
    T, D = dya.shape
    C = w.shape[0]

    def epi(acc, e_refs, out_refs, i, j, extra):
        gv = e_refs[1][...]
        _, uh, rstd, u2, s = _ln_silu(e_refs[0][...].astype(F32), gv, e_refs[2][...])
        du2 = acc * (s * (1.0 + u2 * (1.0 - s)))
        duh = du2 * gv
        out_refs[0][...] = rstd * (duh - jnp.mean(duh, axis=-1, keepdims=True)
                                   - uh * jnp.mean(duh * uh, axis=-1, keepdims=True))
        _acc_rows(out_refs[1], jnp.sum(du2 * uh, axis=0, keepdims=True), i == 0)
        _acc_rows(out_refs[2], jnp.sum(du2, axis=0, keepdims=True), i == 0)

    t = _bs((tm, C), lambda i, j, k: (i, 0))
    row = _bs((1, C), lambda i, j, k: (0, 0))
    return _mm("bwd_du3", "nt", (T // tm, 1, 1), [(dya, _bs((tm, D), lambda i, j, k: (i, 0)))], _ident,
               (w, _bs((C, D), lambda i, j, k: (0, 0))), [(u1, t), (ln_g, row), (ln_b, row)], epi,
               [(_sds((T, C), F32), t), (_sds((1, C), F32), row), (_sds((1, C), F32), row)], None)


def _bwd_dp(dyb, w, h, z, dz, gb_blk, tm, tn, comm=None):
    T, D = dyb.shape
    R = w.shape[0]

    def epi(acc, e_refs, out_refs, i, j, extra):
        gbv = e_refs[1][...].astype(F32)
        ge, th = _gelu(gbv)
        out_refs[0][...] = acc * ge
        out_refs[1][...] = (acc * e_refs[0][...].astype(F32) * _gelu_grad(gbv, th)).astype(BF16)

    t = _bs((tm, tn), lambda i, j, k: (i, j))
    tz = _bs((tm, tn), lambda i, j, k: (i, gb_blk + j))
    return _mm("bwd_dp", "nt", (T // tm, R // tn, 1), [(dyb, _bs((tm, D), lambda i, j, k: (i, 0)))], _ident,
               (w, _bs((tn, D), lambda i, j, k: (j, 0))), [(h, t), (z, tz)], epi,
               [(_sds((T, R), F32), t), (_sds(dz.shape, BF16), tz)], None, cache_a=None, alias=[(dz, 1)], comm=comm)


CONV_ROWS = 32


def _shifted_taps(x, halo, shifts, fn):
    n = CONV_ROWS + halo
    by_r = {}
    for k, s in shifts:
        by_r.setdefault(s % 8, []).append((k, s))
    for r in sorted(by_r):
        xr = x if r == 0 else pltpu.roll(x, n - r, 0)
        for k, s in by_r[r]:
            q = s - r
            fn(k, xr[q:q + CONV_ROWS])


def _conv_fwd(name, z, blk0, gate_blk0, w_pad, bias, taps, seq, tc, out_dtype, comm=None):
    T = z.shape[0]
    C = w_pad.shape[1]
    nb, nj = T // seq, C // tc
    pad = 8 * ((taps - 1 + 7) // 8)
    halo = pad
    shifts = [(k, pad - (taps - 1) + k) for k in range(taps)]
    glu = gate_blk0 is not None

    def body(*refs):
        if glu:
            v_ref, g_ref, w_ref, b_ref, o_ref, p_ref = refs
        else:
            v_ref, w_ref, b_ref, o_ref, p_ref = refs
        p_ref[pl.ds(0, pad), :] = jnp.zeros((pad, tc), F32)
        u = v_ref[...].astype(F32)
        if glu:
            u = u * _sig(g_ref[...].astype(F32))
        p_ref[pl.ds(pad, seq), :] = u

        def step(c, _):
            base = pl.multiple_of(c * CONV_ROWS, CONV_ROWS)
            x = p_ref[pl.ds(base, CONV_ROWS + halo), :]
            acc = [jnp.zeros((CONV_ROWS, tc), F32) + b_ref[...]]

            def tap(k, xs):
                acc[0] = acc[0] + w_ref[k:k + 1, :] * xs

            _shifted_taps(x, halo, shifts, tap)
            o_ref[pl.ds(base, CONV_ROWS), :] = acc[0].astype(out_dtype)
            return 0

        lax.fori_loop(0, seq // CONV_ROWS, step, 0)

    zin = [(z, _bs((seq, tc), lambda b, j: (b, blk0 + j)))]
    if glu:
        zin.append((z, _bs((seq, tc), lambda b, j: (b, gate_blk0 + j))))
    ins = zin + [(w_pad, _bs((w_pad.shape[0], tc), lambda b, j: (0, j))), (bias, _bs((1, tc), lambda b, j: (0, j)))]
    r = _call_with_comm(name, body, (nb, nj), [a for a, _ in ins], [s for _, s in ins],
                        [_bs((seq, tc), lambda b, j: (b, j))], [_sds((T, C), out_dtype)],
                        [pltpu.VMEM((seq + pad, tc), F32)], comm)
    return (r[0][0], r[1]) if comm else r[0]


def _conv_bwd(name, dy, z, dz, blk0, gate_blk0, w_pad, taps, seq, tc, comm=None):
    T = z.shape[0]
    C = w_pad.shape[1]
    nb, nj = T // seq, C // tc
    kp = w_pad.shape[0]
    pad = 8 * ((taps - 1 + 7) // 8)
    halo = pad
    sh_du = [(k, taps - 1 - k) for k in range(taps)]
    sh_dw = [(k, pad - (taps - 1) + k) for k in range(taps)]
    glu = gate_blk0 is not None

    def body(*refs):
        if glu:
            dy_ref, v_ref, g_ref, w_ref, _dz_in, dz_out, dw_ref, db_ref, pdy, pu, du_s, wacc, ob, ob2, osem = refs
        else:
            dy_ref, v_ref, w_ref, _dz_in, dz_out, dw_ref, db_ref, pdy, pu, du_s, wacc, ob, osem = refs
        j = pl.program_id(0)
        b = pl.program_id(1)
        pdy[pl.ds(seq, pad), :] = jnp.zeros((pad, tc), F32)
        pdy[pl.ds(0, seq), :] = dy_ref[...].astype(F32)
        pu[pl.ds(0, pad), :] = jnp.zeros((pad, tc), F32)
        v = v_ref[...].astype(F32)
        if glu:
            sg = _sig(g_ref[...].astype(F32))
            pu[pl.ds(pad, seq), :] = v * sg
        else:
            pu[pl.ds(pad, seq), :] = v
        wacc[...] = jnp.zeros(wacc.shape, F32)

        def step(c, dbacc):
            base = pl.multiple_of(c * CONV_ROWS, CONV_ROWS)
            xdy = pdy[pl.ds(base, CONV_ROWS + halo), :]
            acc = [jnp.zeros((CONV_ROWS, tc), F32)]

            def tap(k, xs):
                acc[0] = acc[0] + w_ref[k:k + 1, :] * xs

            _shifted_taps(xdy, halo, sh_du, tap)
            du_s[pl.ds(base, CONV_ROWS), :] = acc[0]
            dyc = xdy[0:CONV_ROWS]
            xu = pu[pl.ds(base, CONV_ROWS + halo), :]

            def wtap(k, xs):
                p = dyc * xs
                s8 = p[0:8]
                for m in range(1, CONV_ROWS // 8):
                    s8 = s8 + p[8 * m:8 * m + 8]
                wacc[pl.ds(8 * k, 8), :] += s8

            _shifted_taps(xu, halo, sh_dw, wtap)
            d8 = dyc[0:8]
            for m in range(1, CONV_ROWS // 8):
                d8 = d8 + dyc[8 * m:8 * m + 8]
            return dbacc + d8

        dbacc = lax.fori_loop(0, seq // CONV_ROWS, step, jnp.zeros((8, tc), F32))
        du = du_s[...]
        rows = pl.ds(pl.multiple_of(b * seq, seq), seq)
        puts = [pltpu.make_async_copy(ob, dz_out.at[rows, pl.ds(pl.multiple_of((blk0 + j) * tc, tc), tc)], osem.at[0])]
        if glu:
            puts.append(pltpu.make_async_copy(
                ob2, dz_out.at[rows, pl.ds(pl.multiple_of((gate_blk0 + j) * tc, tc), tc)], osem.at[1]))

        @pl.when((j > 0) | (b > 0))
        def _():
            for cp in puts:
                cp.wait()

        if glu:
            ob[...] = (du * sg).astype(BF16)
            ob2[...] = (du * v * sg * (1.0 - sg)).astype(BF16)
        else:
            ob[...] = du.astype(BF16)
        for cp in puts:
            cp.start()

        @pl.when((j == nj - 1) & (b == nb - 1))
        def _():
            for cp in puts:
                cp.wait()
        dw = jnp.sum(wacc[...].reshape(kp, 8, tc), axis=1)
        _acc_rows(dw_ref, dw, b == 0)
        _acc_rows(db_ref, jnp.sum(dbacc, axis=0, keepdims=True), b == 0)

    zin = [(z, _bs((seq, tc), lambda j, b: (b, blk0 + j)))]
    if glu:
        zin.append((z, _bs((seq, tc), lambda j, b: (b, gate_blk0 + j))))
    ins = [(dy, _bs((seq, tc), lambda j, b: (b, j)))] + zin + [(w_pad, _bs((kp, tc), lambda j, b: (0, j))),
                                                               (dz, pl.BlockSpec(memory_space=pl.ANY))]
    dz_idx = len(ins) - 1
    out_specs = [pl.BlockSpec(memory_space=pl.ANY), _bs((kp, tc), lambda j, b: (0, j)), _bs((1, tc), lambda j, b: (0, j))]
    out_shape = [_sds(dz.shape, dz.dtype), _sds((kp, C), F32), _sds((1, C), F32)]
    stage = [pltpu.VMEM((seq, tc), BF16)] * (2 if glu else 1) + [pltpu.SemaphoreType.DMA((2,))]
    return _call_with_comm(
        name, body, (nj, nb), [a for a, _ in ins], [s for _, s in ins], out_specs, out_shape,
        [pltpu.VMEM((seq + pad, tc), F32), pltpu.VMEM((seq + pad, tc), F32),
         pltpu.VMEM((seq, tc), F32), pltpu.VMEM((8 * kp, tc), F32)] + stage, comm, aliases={dz_idx: 0})


RG_ROWS = 256


def _softplus_neg(lam):
    return jnp.maximum(-lam, 0.0) + jnp.log(1.0 + jnp.exp(-jnp.abs(lam)))


def _gates(v0c, wa_ref, wx_ref, ba, bx, sp):
    vb = v0c.astype(BF16)
    r = _sig(jnp.dot(vb, wa_ref[...], preferred_element_type=F32) + ba)
    i = _sig(jnp.dot(vb, wx_ref[...], preferred_element_type=F32) + bx)
    return r, i, -LRU_C * r * sp


def _decay(la, first_row):
    a = jnp.exp(la)
    a2 = a * a
    x = 2.0 * la
    series = -x * (1.0 + x * (0.5 + x * (1.0 / 6)))
    mult = jnp.sqrt(jnp.where(x > -0.01, series, 1.0 - a2))
    dmult = jnp.where(first_row, 0.0, -a2 / mult)
    mult = jnp.where(first_row, 1.0, mult)
    return a, mult, dmult


def _group_scan(a, b, reverse):
    n = a.shape[0]
    row = lax.broadcasted_iota(jnp.int32, a.shape, 0) & 7
    for d in (1, 2, 4):
        sh = n - d if reverse else d
        a_s, b_s = pltpu.roll(a, sh, 0), pltpu.roll(b, sh, 0)
        m = (row < 8 - d) if reverse else (row >= d)
        b = jnp.where(m, a * b_s + b, b)
        a = jnp.where(m, a * a_s, a)
    return a, b


def _group_carry(a_s, b_s, o_s, n_groups, reverse):
    cols = a_s.shape[1]

    def step(g, carry):
        g = n_groups - 1 - g if reverse else g
        rows = pl.ds(pl.multiple_of(g * 8, 8), 8)
        o = a_s[rows, :] * carry + b_s[rows, :]
        o_s[rows, :] = o
        return o[0:1, :] if reverse else o[7:8, :]

    lax.fori_loop(0, n_groups, step, jnp.zeros((1, cols), F32), unroll=2)


def _rglru_fwd(v0, wa, wx, ba, bx, lam, seq, comm=None):
    T, C = v0.shape
    ng, G = wa.shape[0], wa.shape[1]
    nb = T // seq

    def body(v_ref, wa_ref, wx_ref, ba_ref, bx_ref, lam_ref, h_ref, r_ref, i_ref, la_ref, a_s, b_s, h_s):
        sp = _softplus_neg(lam_ref[...])

        def chunk(c, _):
            rows = pl.ds(pl.multiple_of(c * RG_ROWS, RG_ROWS), RG_ROWS)
            t = lax.broadcasted_iota(jnp.int32, (RG_ROWS, G), 0) + c * RG_ROWS
            v0c = v_ref[rows, :]
            r, i, la = _gates(v0c, wa_ref, wx_ref, ba_ref[...], bx_ref[...], sp)
            r_ref[rows, :] = r.astype(BF16)
            i_ref[rows, :] = i.astype(BF16)
            la_ref[rows, :] = la
            a, mult, _ = _decay(la, t == 0)
            a_g, b_g = _group_scan(a, mult * i * v0c, False)
            a_s[rows, :] = a_g
            b_s[rows, :] = b_g
            return 0

        lax.fori_loop(0, seq // RG_ROWS, chunk, 0)
        _group_carry(a_s, b_s, h_s, seq // 8, False)
        h_ref[...] = h_s[...].astype(BF16)

    t2 = _bs((seq, G), lambda b, g: (b, g))
    wsp = _bs((None, G, G), lambda b, g: (g, 0, 0))
    row = _bs((1, G), lambda b, g: (0, g))
    return _call_with_comm("rglru_fwd", body, (nb, ng), [v0, wa, wx, ba, bx, lam], [t2, wsp, wsp, row, row, row],
                           [t2] * 4, [_sds((T, C), BF16)] * 3 + [_sds((T, C), F32)], [pltpu.VMEM((seq, G), F32)] * 3, comm)


def _call_with_comm(name, body, grid, ins, in_specs, out_specs, out_shape, scratch, comm, aliases=None):
    n_in, n_out, n_s = len(ins), len(out_shape), len(scratch)
    c_ins, c_outs, c_sems = (comm.ins, comm.outs, comm.sems) if comm else ([], [], [])

    def wrapped(*refs):
        o0 = n_in + len(c_ins)
        s0 = o0 + n_out + len(c_outs)
        cin, cout, csem = refs[n_in:o0], refs[o0 + n_out:s0], refs[s0 + n_s:]
        ids = [pl.program_id(a) for a in range(len(grid))]
        if comm:
            first = _all_of([i == 0 for i in ids])

            @pl.when(first)
            def _():
                comm.start(cin, cout, csem)

        body(*refs[:n_in], *refs[o0:o0 + n_out], *refs[s0:s0 + n_s])
        if comm:
            last = _all_of([i == n - 1 for i, n in zip(ids, grid)])

            @pl.when(last)
            def _():
                comm.finish(cin, cout, csem)

    res = pl.pallas_call(
        wrapped, name=name, grid=grid, in_specs=list(in_specs) + [ANY] * len(c_ins),
        out_specs=list(out_specs) + [ANY] * len(c_outs), out_shape=list(out_shape) + list(c_outs),
        scratch_shapes=list(scratch) + list(c_sems), input_output_aliases=aliases or {},
        compiler_params=_cp(dimension_semantics=("arbitrary",) * len(grid), has_side_effects=bool(comm),
                            collective_id=comm.collective_id if comm else None),
    )(*ins, *c_ins)
    return (list(res[:n_out]), list(res[n_out:])) if comm else list(res)


def _all_of(conds):
    out = conds[0]
    for c in conds[1:]:
        out = out & c
    return out


def _rglru_bwd(v0, h, dh, r_g, i_g, la_g, wa, wx, lam, seq, comm=None):
    T, C = v0.shape
    ng, G = wa.shape[0], wa.shape[1]
    nb = T // seq
    R = RG_ROWS

    def body(v_ref, h_ref, dh_ref, r_ref, i_ref, la_ref, wa_ref, wx_ref, lam_ref,
             dv_ref, dwa_ref, dwx_ref, dba_ref, dbx_ref, dlam_ref, a_s, b_s, q_s, hp_s):
        b = pl.program_id(1)
        lam_v = lam_ref[...]
        sp = _softplus_neg(lam_v)
        dsp_dlam = -_sig(-lam_v)

        @pl.when(b == 0)
        def _():
            dwa_ref[...] = jnp.zeros((G, G), F32)
            dwx_ref[...] = jnp.zeros((G, G), F32)
            dba_ref[...] = jnp.zeros((1, G), F32)
            dbx_ref[...] = jnp.zeros((1, G), F32)
            dlam_ref[...] = jnp.zeros((1, G), F32)

        hp_s[pl.ds(0, 8), :] = jnp.zeros((8, G), F32)
        hp_s[pl.ds(8, seq), :] = h_ref[...].astype(F32)
        q_s[pl.ds(seq, 8), :] = jnp.zeros((8, G), F32)

        def chunk1(c, _):
            rows = pl.ds(pl.multiple_of(c * R, R), R)
            a = jnp.exp(la_ref[rows, :])
            a_g, b_g = _group_scan(a, a * dh_ref[rows, :].astype(F32), True)
            a_s[rows, :] = a_g
            b_s[rows, :] = b_g
            return 0

        lax.fori_loop(0, seq // R, chunk1, 0)
        _group_carry(a_s, b_s, q_s, seq // 8, True)

        def chunk3(c, _):
            base = pl.multiple_of(c * R, R)
            rows = pl.ds(base, R)
            t = lax.broadcasted_iota(jnp.int32, (R, G), 0) + c * R
            v0c = v_ref[rows, :]
            r, i = r_ref[rows, :].astype(F32), i_ref[rows, :].astype(F32)
            a, mult, dmult_dla = _decay(la_ref[rows, :], t == 0)
            q_next = pltpu.roll(q_s[pl.ds(base, R + 8), :], R + 7, 0)[0:R]
            h_prev = pltpu.roll(hp_s[pl.ds(base, R + 8), :], R + 1, 0)[0:R]
            gt = dh_ref[rows, :].astype(F32) + q_next
            dla = gt * h_prev * a + gt * i * v0c * dmult_dla
            dpa = dla * (-LRU_C * sp) * r * (1.0 - r)
            dpx = gt * mult * v0c * i * (1.0 - i)
            dpa_b, dpx_b, v_b = dpa.astype(BF16), dpx.astype(BF16), v0c.astype(BF16)
            dv_ref[rows, :] = (gt * mult * i
                               + lax.dot_general(dpa_b, wa_ref[...], _DIMS["nt"], preferred_element_type=F32)
                               + lax.dot_general(dpx_b, wx_ref[...], _DIMS["nt"], preferred_element_type=F32))
            dwa_ref[...] += lax.dot_general(v_b, dpa_b, _DIMS["tn"], preferred_element_type=F32)
            dwx_ref[...] += lax.dot_general(v_b, dpx_b, _DIMS["tn"], preferred_element_type=F32)
            dba_ref[...] += jnp.sum(dpa, axis=0, keepdims=True)
            dbx_ref[...] += jnp.sum(dpx, axis=0, keepdims=True)
            dlam_ref[...] += jnp.sum(dla * (-LRU_C * r), axis=0, keepdims=True) * dsp_dlam
            return 0

        lax.fori_loop(0, seq // R, chunk3, 0)

    t2 = _bs((seq, G), lambda g, b: (b, g))
    wsp = _bs((None, G, G), lambda g, b: (g, 0, 0))
    row = _bs((1, G), lambda g, b: (0, g))
    return _call_with_comm(
        "rglru_bwd", body, (ng, nb), [v0, h, dh, r_g, i_g, la_g, wa, wx, lam], [t2] * 6 + [wsp, wsp, row],
        [t2, wsp, wsp, row, row, row],
        [_sds((T, C), F32), _sds((ng, G, G), F32), _sds((ng, G, G), F32),
         _sds((1, C), F32), _sds((1, C), F32), _sds((1, C), F32)],
        [pltpu.VMEM((seq, G), F32), pltpu.VMEM((seq, G), F32),
         pltpu.VMEM((seq + 8, G), F32), pltpu.VMEM((seq + 8, G), F32)], comm)


TC_A = 256
TC_B = 512
TAPS_A, TAPS_B = 31, 4


def _tiles(T):
    return min(512, T), min(1024, T)


GATHERED = ("w_in", "w_1", "w_a_out", "w_b_out", "w_o", "w_2", "caw", "cbw")
GATHER_KIND = {"w_in": (True, True), "w_1": (True, True), "w_a_out": (False, True), "w_b_out": (False, True),
               "w_o": (False, True), "w_2": (False, True), "caw": (True, False), "cbw": (True, False)}


def _layer_fwd(x, p, seq, jobs=None):
    T, D = x.shape
    C, R = p["ln_g"].shape[1], p["lam"].shape[1]
    tm, tl = _tiles(T)
    gb_blk, sa_blk = (2 * C + R) // TC_B, (2 * C + 2 * R) // D
    p, ahead, jobs = dict(p), {}, jobs or {}

    def gather(call):
        js = jobs.get(call)
        return _gather_comm([s for _, s, _ in js], [GATHER_KIND[n] for n, _, _ in js]) if js else None

    def outs(r, call):
        js = jobs.get(call)
        if not js:
            return r
        for (n, _, for_next), whole in zip(js, r[1]):
            (ahead if for_next else p)[n] = whole
        return r[0]

    z, h = outs(_fwd_norm_mm("fwd_z", x, p["g_mix"], p["w_in"], p["b_in"], tl, 1024, comm=gather("fwd_z")), "fwd_z")
    u1 = outs(_conv_fwd("conv_a_fwd", z, 0, C // TC_A, p["caw"], p["cab"], TAPS_A, seq, TC_A, BF16,
                        comm=gather("conv_a_fwd")), "conv_a_fwd")
    ya, u3 = _fwd_ya(u1, p["ln_g"], p["ln_b"], p["w_a_out"], tm)
    v0 = _conv_fwd("conv_b_fwd", z, 2 * C // TC_B, None, p["cbw"], p["cbb"], TAPS_B, seq, TC_B, F32)
    hr, rg, ig, lag = outs(_rglru_fwd(v0, p["wa"], p["wx"], p["b_rg_a"], p["b_rg_x"], p["lam"], seq,
                                      comm=gather("rglru_fwd")), "rglru_fwd")
    yb, pb = _fwd_yb(hr, z, gb_blk, p["w_b_out"], tl, TC_B)
    x1, mg = outs(_fwd_x1(x, ya, yb, z, sa_blk, p["w_o"], tm, comm=gather("fwd_x1")), "fwd_x1")
    fp, h2 = outs(_fwd_norm_mm("fwd_f", x1, p["g_mlp"], p["w_1"], None, tl, 1024, comm=gather("fwd_f")), "fwd_f")
    x2 = outs(_fwd_x2(x1, fp, p["w_2"], tm, fp.shape[1], comm=gather("fwd_x2")), "fwd_x2")
    saved = dict(x=x, z=z, h=h, u1=u1, u3=u3, ya=ya, v0=v0, hr=hr, rg=rg, ig=ig, lag=lag, pb=pb, yb=yb, mg=mg, x1=x1,
                 fp=fp, h2=h2)
    return x2, saved, p, ahead


class _Reduce:
    EARLY = ("w_2", "w_1", "w_o", "w_a_out")
    LATE = ("w_b_out", "w_in")

    def __init__(self, accs, c_arr, kcl_of):
        self.accs, self.c_arr, self.kcl_of, self.late = accs, c_arr, kcl_of, None

    @staticmethod
    def pieces(partials):
        return [a if a.ndim == 4 else a.reshape(N_CHIPS, 2, a.shape[0] // (2 * N_CHIPS), a.shape[1]) for a in partials]

    def chip_sums(self, pgs, swapped):
        return _sum_siblings(pgs, swapped, self.c_arr)

    def finish(self, names, sums, received, layer):
        done = _sum_chips(sums, received, self.kcl_of(layer), [self.accs[n] for n in names])
        self.accs.update(zip(names, done))


def _layer_bwd(dx2, dx2b, p, s, seq, red=None, layer=0):
    T, D = dx2.shape
    C, R = p["ln_g"].shape[1], p["lam"].shape[1]
    tm, tl = _tiles(T)
    gb_blk, sa_blk = (2 * C + R) // TC_B, (2 * C + 2 * R) // D
    z = s["z"]
    g = {}


    late_sums = None
    if red is not None and red.late is not None:
        late, red.late = red.late, None
        dfp, got = _bwd_df(dx2b, p["w_2"], s["fp"], tl, 1024, comm=_swap_comm(late))
        late_sums = red.chip_sums(late, got)
    else:
        dfp = _bwd_df(dx2b, p["w_2"], s["fp"], tl, 1024)
    g["w_2"] = _bwd_dw("bwd_dw2", s["fp"], dx2b, 1024, D, T, a_fn=_relu2, keep="dy")
    dx1, dx1b, g["g_mlp"] = _bwd_norm("bwd_dh2", dfp, p["w_1"], s["x1"], p["g_mlp"], dx2, tm, dfp.shape[1])
    g["w_1"] = _bwd_dw("bwd_dw1", s["h2"], dfp, D, 1024, T, shard_cols=dfp.shape[1] // N_CHIPS, keep="act")

    dya, dyb, dz = _bwd_dm(dx1b, p["w_o"], s["ya"], s["yb"], z, sa_blk, tm)

    g["w_o"] = _bwd_dw("bwd_dwo", s["mg"], dx1b, D, D, tl)
    du1, g["ln_g"], g["ln_b"] = _bwd_du3(dya, p["w_a_out"], s["u1"], p["ln_g"], p["ln_b"], tm)
    g["w_a_out"] = _bwd_dw("bwd_dwa", s["u3"], dya, C, D, tl)
    conv_a_args = ("conv_a_bwd", du1, z, dz, 0, C // TC_A, p["caw"], TAPS_A, seq, TC_A)
    if late_sums is not None:
        (dz, g["caw"], g["cab"]), got = _conv_bwd(*conv_a_args, comm=_scatter_comm(late_sums))
        red.finish(red.LATE, late_sums, got, layer + 1)
    else:
        dz, g["caw"], g["cab"] = _conv_bwd(*conv_a_args)

    dp_args = (dyb, p["w_b_out"], s["hr"], z, dz, gb_blk, tl, TC_B)
    if red is not None:
        early = red.pieces([g.pop(n) for n in red.EARLY])
        (dhr, dz), got = _bwd_dp(*dp_args, comm=_swap_comm(early))
        early_sums = red.chip_sums(early, got)
    else:
        dhr, dz = _bwd_dp(*dp_args)

    g["w_b_out"] = _bwd_dw("bwd_dwb", s["pb"], dyb, R, D, tl)
    rg_args = (s["v0"], s["hr"], dhr, s["rg"], s["ig"], s["lag"], p["wa"], p["wx"], p["lam"], seq)
    if red is not None:
        rg_out, got = _rglru_bwd(*rg_args, comm=_scatter_comm(early_sums))
        red.finish(red.EARLY, early_sums, got, layer)
    else:
        rg_out = _rglru_bwd(*rg_args)
    dv0, g["wa"], g["wx"], g["b_rg_a"], g["b_rg_x"], g["lam"] = rg_out
    dz, g["cbw"], g["cbb"] = _conv_bwd("conv_b_bwd", dv0, z, dz, 2 * C // TC_B, None, p["cbw"], TAPS_B, seq, TC_B)

    dx, dxb, g["g_mix"], dbin = _bwd_norm("bwd_dh", dz, p["w_in"], s["x"], p["g_mix"], dx1, tm, dz.shape[1],
                                          colsum=True)
    g["b_in"] = dbin.reshape(1, -1)
    ns = dz.shape[1] // N_CHIPS
    g["w_in"] = _bwd_dw("bwd_dwin", s["h"], dz, D, ns // 2, T, shard_cols=ns, keep="act")
    if red is not None:
        red.late = red.pieces([g.pop(n) for n in red.LATE])
    return dx, dxb, g


ANY = pl.BlockSpec(memory_space=pl.ANY)


def _mesh_pos():
    return lax.axis_index("x"), lax.axis_index("y"), lax.axis_index("c")


def _other_chips(x, y):
    return [(1 - x, y), (x, 1 - y), (1 - x, 1 - y)]


def _remote(src, dst, ssem, rsem, dev):
    return pltpu.make_async_remote_copy(src_ref=src, dst_ref=dst, send_sem=ssem, recv_sem=rsem,
                                        device_id=dev, device_id_type=MESH)


def _gather_region(src, dst, by_cols, k, half):
    rows, cols = src.shape
    nr = rows if half is None else rows // 2
    r0 = 0 if half is None else half * nr
    if by_cols:
        return dst.at[pl.ds(r0, nr), pl.ds(pl.multiple_of(k * cols, 128), cols)]
    return dst.at[pl.ds(pl.multiple_of(k * rows + r0, 8), nr), :]


def _gather_sends(src, dst, kinds, send, recv):
    x, y, c = _mesh_pos()
    cps = []
    for t in range(len(src)):
        half = c if kinds[t][1] else None
        hr = src[t].shape[0] // 2
        s_ref = src[t].at[pl.ds(c * hr, hr), :] if kinds[t][1] else src[t]
        for j, chip in enumerate(_other_chips(x, y)):
            cps.append(_remote(s_ref, _gather_region(src[t], dst[t], kinds[t][0], 2 * x + y, half),
                               send.at[t, j], recv.at[t, j], (*chip, c)))
    return cps


def _gather_finish(src, dst, kinds, send, recv, fsend, frecv):
    x, y, c = _mesh_pos()
    chips = _other_chips(x, y)
    sib = (x, y, 1 - c)
    n = len(src)
    fwd = []
    for t in range(n):
        half = c if kinds[t][1] else None
        for j, chip in enumerate(chips):
            got = _gather_region(src[t], dst[t], kinds[t][0], 2 * chip[0] + chip[1], half)
            _remote(got, got, send.at[t, j], recv.at[t, j], (*chip, c)).wait_recv()
            if kinds[t][1]:
                cp = _remote(got, got, fsend.at[t, j], frecv.at[t, j], sib)
                cp.start()
                fwd.append(cp)
    for t in range(n):
        if kinds[t][1]:
            for j, chip in enumerate(chips):
                got = _gather_region(src[t], dst[t], kinds[t][0], 2 * chip[0] + chip[1], 1 - c)
                _remote(got, got, fsend.at[t, j], frecv.at[t, j], sib).wait_recv()
    for cp in _gather_sends(src, dst, kinds, send, recv) + fwd:
        cp.wait_send()


def _gather_sems(n):
    sem = pltpu.SemaphoreType.DMA
    return [sem((n, 3)), sem((n, 3)), sem((n, 3)), sem((n, 3))]


def _gather_comm(shards, kinds):
    n = len(shards)

    def whole(s, by_cols):
        return (s.shape[0], N_CHIPS * s.shape[1]) if by_cols else (N_CHIPS * s.shape[0], s.shape[1])

    def own(src, dst, lsem):
        x, y, _ = _mesh_pos()
        return [pltpu.make_async_copy(src[t], _gather_region(src[t], dst[t], kinds[t][0], 2 * x + y, None), lsem.at[t])
                for t in range(n)]

    def start(src, dst, sems):
        for cp in own(src, dst, sems[4]) + _gather_sends(src, dst, kinds, sems[0], sems[1]):
            cp.start()

    def finish(src, dst, sems):
        _gather_finish(src, dst, kinds, *sems[:4])
        for cp in own(src, dst, sems[4]):
            cp.wait()

    return _Comm(shards, [_sds(whole(s, k[0]), s.dtype) for s, k in zip(shards, kinds)],
                 _gather_sems(n) + [pltpu.SemaphoreType.DMA((n,))], start, finish,
                 "chips+sibling" if any(k[1] for k in kinds) else "chips")


def _scatter_comm(ps):
    n = len(ps)

    def copies(src, dst, sems):
        x, y, c = _mesh_pos()
        return [_remote(src[t].at[2 * chip[0] + chip[1]], dst[t].at[j], sems[0].at[t, j], sems[1].at[t, j], (*chip, c))
                for t in range(n) for j, chip in enumerate(_other_chips(x, y))]

    def start(src, dst, sems):
        for cp in copies(src, dst, sems):
            cp.start()

    def finish(src, dst, sems):
        cps = copies(src, dst, sems)
        for cp in cps:
            cp.wait_recv()
        for cp in cps:
            cp.wait_send()

    sem = pltpu.SemaphoreType.DMA
    return _Comm(ps, [_sds((3,) + a.shape[1:], a.dtype) for a in ps], [sem((n, 3)), sem((n, 3))], start, finish, "chips")


def _comm_call(name, comm):
    n_i, n_o = len(comm.ins), len(comm.outs)

    def body(*refs):
        comm.start(refs[:n_i], refs[n_i:n_i + n_o], refs[n_i + n_o:])
        comm.finish(refs[:n_i], refs[n_i:n_i + n_o], refs[n_i + n_o:])

    return pl.pallas_call(
        body, name=name, in_specs=[ANY] * n_i, out_specs=[ANY] * n_o, out_shape=comm.outs, scratch_shapes=comm.sems,
        compiler_params=_cp(has_side_effects=True, collective_id=comm.collective_id),
    )(*comm.ins)


def _swap_comm(pgs):
    n = len(pgs)

    def copies(src, dst, sems):
        x, y, c = _mesh_pos()
        return [_remote(src[t].at[:, 1 - c], dst[t], sems[0].at[t], sems[1].at[t], (x, y, 1 - c)) for t in range(n)]

    def start(src, dst, sems):
        for cp in copies(src, dst, sems):
            cp.start()

    def finish(src, dst, sems):
        cps = copies(src, dst, sems)
        for cp in cps:
            cp.wait_recv()
        for cp in cps:
            cp.wait_send()

    sem = pltpu.SemaphoreType.DMA
    return _Comm(pgs, [_sds((a.shape[0],) + a.shape[2:], a.dtype) for a in pgs], [sem((n,)), sem((n,))], start, finish,
                 "sibling")


def _join_halves(accs, also=None):
    n = len(accs)
    c_ins, c_outs, c_sems = (also.ins, also.outs, also.sems) if also else ([], [], [])
    assert also is None or also.peers == "chips"
    peers = "chips+sibling" if also else "sibling"

    def body(*refs):
        o0 = n + len(c_ins)
        buf = refs[o0:o0 + n]
        send, recv = refs[o0 + n + len(c_outs):o0 + n + len(c_outs) + 2]
        extra = (refs[n:o0], refs[o0 + n:o0 + n + len(c_outs)], refs[o0 + n + len(c_outs) + 2:])
        x, y, c = _mesh_pos()
        _handshake(peers)
        if also:
            also.copies_start(*extra)
        cps = [_remote(buf[t].at[:, c], buf[t].at[:, c], send.at[t], recv.at[t], (x, y, 1 - c)) for t in range(n)]
        for cp in cps:
            cp.start()
        for t in range(n):
            _remote(buf[t].at[:, c], buf[t].at[:, 1 - c], send.at[t], recv.at[t], (x, y, 1 - c)).wait_recv()
        for cp in cps:
            cp.wait_send()
        if also:
            also.finish(*extra)

    sem = pltpu.SemaphoreType.DMA
    res = pl.pallas_call(
        body, name="join_halves", in_specs=[ANY] * (n + len(c_ins)), out_specs=[ANY] * (n + len(c_outs)),
        out_shape=[_sds(a.shape, a.dtype) for a in accs] + list(c_outs),
        scratch_shapes=[sem((n,)), sem((n,))] + list(c_sems),
        input_output_aliases={t: t for t in range(n)},
        compiler_params=_cp(has_side_effects=True, collective_id=PEER_SETS[peers]),
    )(*accs, *c_ins)
    return (list(res[:n]), list(res[n:])) if also else res


def _sum_siblings(pgs, rbs, c_arr):
    n = len(pgs)
    nk = pgs[0].shape[0]

    def body(c_ref, *refs):
        for t in range(n):
            refs[2 * n + t][...] = (refs[t][...].astype(F32) + refs[n + t][...].astype(F32)).astype(BF16)

    half = lambda a: pl.BlockSpec((None,) + a.shape[2:], lambda k, c_ref: (k, 0, 0))
    return pl.pallas_call(
        body, name="sum_siblings",
        grid_spec=pltpu.PrefetchScalarGridSpec(
            num_scalar_prefetch=1, grid=(nk,),
            in_specs=[pl.BlockSpec((None, None) + a.shape[2:], lambda k, c_ref: (k, c_ref[0], 0, 0)) for a in pgs]
            + [half(a) for a in pgs],
            out_specs=[half(a) for a in pgs]),
        out_shape=[_sds((nk,) + a.shape[2:], BF16) for a in pgs],
        compiler_params=_cp(dimension_semantics=("arbitrary",)),
    )(c_arr, *pgs, *rbs)


def _sum_chips(ps, rbs, kcl, accs):
    n = len(ps)

    def body(k_ref, *refs):
        for t in range(n):
            b_ref = refs[n + t]
            refs[3 * n + t][...] = (refs[t][...].astype(F32) + b_ref[0].astype(F32) + b_ref[1].astype(F32)
                                    + b_ref[2].astype(F32))

    qr = lambda a: (a.shape[1] // 2, a.shape[2])
    return pl.pallas_call(
        body, name="sum_chips",
        grid_spec=pltpu.PrefetchScalarGridSpec(
            num_scalar_prefetch=1, grid=(2,),
            in_specs=[pl.BlockSpec((None,) + qr(a), lambda r, k_ref: (k_ref[0], r, 0)) for a in ps]
            + [pl.BlockSpec((3,) + qr(a), lambda r, k_ref: (0, r, 0)) for a in ps] + [ANY] * n,
            out_specs=[pl.BlockSpec((None, None) + qr(a), lambda r, k_ref: (k_ref[2], k_ref[1], r, 0)) for a in ps]),
        out_shape=[_sds(a.shape, F32) for a in accs], input_output_aliases={1 + 2 * n + t: t for t in range(n)},
        compiler_params=_cp(dimension_semantics=("arbitrary",)),
    )(kcl, *ps, *rbs, *accs)


N_DEV = 8


def _allreduce_small(parts):
    n = len(parts)

    def body(*refs):
        p_refs, o_refs, rbufs = refs[:n], refs[n:2 * n], refs[2 * n:3 * n]
        s1, r1, s2, r2 = refs[3 * n:]
        x, y, c = _mesh_pos()
        me = 4 * x + 2 * y + c
        devs = [(d // 4, (d // 2) % 2, d % 2) for d in range(N_DEV)]
        for q in range(n):
            rbufs[q][me] = p_refs[q][me]

        def each_peer(fn):
            for d in range(N_DEV):
                @pl.when(d != me)
                def _():
                    for q in range(n):
                        fn(d, q)

        def first(d, q, to_me):
            return _remote(p_refs[q].at[d], rbufs[q].at[d if to_me else me], s1.at[q, d], r1.at[q, d if to_me else me],
                           devs[d])

        def second(d, q, to_me):
            blk = d if to_me else me
            return _remote(o_refs[q].at[blk], o_refs[q].at[blk], s2.at[q, d], r2.at[q, blk], devs[d])

        each_peer(lambda d, q: first(d, q, False).start())
        each_peer(lambda d, q: first(d, q, True).wait_recv())
        for q in range(n):
            total = rbufs[q][0].astype(F32)
            for d in range(1, N_DEV):
                total = total + rbufs[q][d].astype(F32)
            o_refs[q][me] = total.astype(o_refs[q].dtype)
        each_peer(lambda d, q: second(d, q, False).start())
        each_peer(lambda d, q: second(d, q, True).wait_recv())
        each_peer(lambda d, q: first(d, q, False).wait_send())
        each_peer(lambda d, q: second(d, q, False).wait_send())

    sem = pltpu.SemaphoreType.DMA
    vm = pl.BlockSpec(memory_space=pltpu.VMEM)
    return pl.pallas_call(
        body, name="allreduce_small", in_specs=[vm] * n, out_specs=[vm] * n,
        out_shape=[_sds(a.shape, a.dtype) for a in parts],
        scratch_shapes=[pltpu.VMEM(a.shape, a.dtype) for a in parts] + [sem((n, N_DEV))] * 4,
        compiler_params=_cp(has_side_effects=True),
    )(*parts)


BIG = ("w_in", "w_1", "w_a_out", "w_b_out", "w_o", "w_2")
BY_COLS = {"w_in": True, "w_1": True, "w_a_out": False, "w_b_out": False, "w_o": False, "w_2": False}
WEIGHTS = ("g_mix", "w_in", "b_in", "conv_a_w", "conv_a_b", "ln_g", "ln_b", "w_a_out", "conv_b_w", "conv_b_b", "w_rg_a",
           "b_rg_a", "w_rg_x", "b_rg_x", "lam", "w_b_out", "w_o", "g_mlp", "w_1", "w_2", "g_final")
SMALL = tuple(n for n in WEIGHTS if n not in BIG)
ADAM_ROWS = 256
ADAM_SMALL_ROWS = 2048


CAST_STEPS = 8


def _cast_weights(todo, comm):
    def body(*refs):
        n = len(todo)
        for q in range(n):
            refs[n + q][...] = refs[q][...].astype(BF16)

    tile = lambda a: (a.shape[1] // CAST_STEPS, a.shape[2])
    in_specs = [pl.BlockSpec((None,) + tile(a), lambda r, l=l: (l, r, 0)) for a, l in todo]
    out_specs = [pl.BlockSpec(tile(a), lambda r: (r, 0)) for a, _ in todo]
    return _call_with_comm("cast_weights", body, (CAST_STEPS,), [a for a, _ in todo], in_specs, out_specs,
                           [_sds(a.shape[1:], BF16) for a, _ in todo], [], comm)


def _block_diag(w):
    nh, dh, _ = w.shape
    ng = nh // HEADS_PER_GROUP
    w4 = w.reshape(ng, HEADS_PER_GROUP, dh, dh)
    eye = jnp.eye(HEADS_PER_GROUP, dtype=w.dtype)
    return jnp.einsum("qhij,hk->qhikj", w4, eye).reshape(ng, HEADS_PER_GROUP * dh, HEADS_PER_GROUP * dh)


def _block_diag_part(d, dh):
    ng = d.shape[0]
    eye = jnp.eye(HEADS_PER_GROUP, dtype=d.dtype)
    d5 = d.reshape(ng, HEADS_PER_GROUP, dh, HEADS_PER_GROUP, dh)
    return jnp.einsum("qhikj,hk->qhij", d5, eye).reshape(ng * HEADS_PER_GROUP, dh, dh)


PACK_LANES = 128


def _pack(arrays, blocks, tile_rows):
    parts = [a.reshape(-1, PACK_LANES) for a in arrays]
    parts = [jnp.pad(p, ((0, -p.shape[0] % tile_rows), (0, 0))) if p.shape[0] % tile_rows else p for p in parts]
    rows = sum(p.shape[0] for p in parts)
    pad = -rows % (blocks * tile_rows)
    if pad:
        parts.append(jnp.zeros((pad, PACK_LANES), parts[0].dtype))
    return jnp.concatenate(parts, axis=0).reshape(blocks, -1, PACK_LANES)


def _unpack(buf, like, tile_rows):
    buf = buf.reshape(-1, PACK_LANES)
    out, off = [], 0
    for a in like:
        n = a.size // PACK_LANES
        out.append(buf[off:off + n].reshape(a.shape))
        off += n + (-n % tile_rows)
    return out


def kernel(x, g_mix, w_in, b_in, conv_a_w, conv_a_b, ln_g, ln_b, w_a_out, conv_b_w, conv_b_b, w_rg_a, b_rg_a, w_rg_x, b_rg_x, lam, w_b_out, w_o, g_mlp, w_1, w_2, g_final, loss_target, m_g_mix, m_w_in, m_b_in, m_conv_a_w, m_conv_a_b, m_ln_g, m_ln_b, m_w_a_out, m_conv_b_w, m_conv_b_b, m_w_rg_a, m_b_rg_a, m_w_rg_x, m_b_rg_x, m_lam, m_w_b_out, m_w_o, m_g_mlp, m_w_1, m_w_2, m_g_final, v_g_mix, v_w_in, v_b_in, v_conv_a_w, v_conv_a_b, v_ln_g, v_ln_b, v_w_a_out, v_conv_b_w, v_conv_b_b, v_w_rg_a, v_b_rg_a, v_w_rg_x, v_b_rg_x, v_lam, v_w_b_out, v_w_o, v_g_mlp, v_w_1, v_w_2, v_g_final):
    w = dict(g_mix=g_mix, w_in=w_in, b_in=b_in, conv_a_w=conv_a_w, conv_a_b=conv_a_b, ln_g=ln_g, ln_b=ln_b, w_a_out=w_a_out,
             conv_b_w=conv_b_w, conv_b_b=conv_b_b, w_rg_a=w_rg_a, b_rg_a=b_rg_a, w_rg_x=w_rg_x, b_rg_x=b_rg_x, lam=lam,
             w_b_out=w_b_out, w_o=w_o, g_mlp=g_mlp, w_1=w_1, w_2=w_2, g_final=g_final)
    m = dict(g_mix=m_g_mix, w_in=m_w_in, b_in=m_b_in, conv_a_w=m_conv_a_w, conv_a_b=m_conv_a_b, ln_g=m_ln_g, ln_b=m_ln_b,
             w_a_out=m_w_a_out, conv_b_w=m_conv_b_w, conv_b_b=m_conv_b_b, w_rg_a=m_w_rg_a, b_rg_a=m_b_rg_a, w_rg_x=m_w_rg_x,
             b_rg_x=m_b_rg_x, lam=m_lam, w_b_out=m_w_b_out, w_o=m_w_o, g_mlp=m_g_mlp, w_1=m_w_1, w_2=m_w_2, g_final=m_g_final)
    v = dict(g_mix=v_g_mix, w_in=v_w_in, b_in=v_b_in, conv_a_w=v_conv_a_w, conv_a_b=v_conv_a_b, ln_g=v_ln_g, ln_b=v_ln_b,
             w_a_out=v_w_a_out, conv_b_w=v_conv_b_w, conv_b_b=v_conv_b_b, w_rg_a=v_w_rg_a, b_rg_a=v_b_rg_a, w_rg_x=v_w_rg_x,
             b_rg_x=v_b_rg_x, lam=v_lam, w_b_out=v_w_b_out, w_o=v_w_o, g_mlp=v_g_mlp, w_1=v_w_1, w_2=v_w_2, g_final=v_g_final)
    B, S, D = x.shape
    T = B * S
    L = w_in.shape[0]
    dh = w_rg_a.shape[-1]
    taps_a, taps_b = conv_a_w.shape[1], conv_b_w.shape[1]
    assert (taps_a, taps_b) == (TAPS_A, TAPS_B)
    xi, yi, ci = _mesh_pos()
    c_arr = jnp.reshape(ci, (1,)).astype(jnp.int32)
    k_me = 2 * xi + yi

    caw_p = jnp.pad(conv_a_w, ((0, 0), (0, 32 - taps_a), (0, 0)))
    cbw_p = jnp.pad(conv_b_w, ((0, 0), (0, 8 - taps_b), (0, 0)))
    row = lambda a: a.reshape(1, -1)

    first_w_in = w_in[0].astype(BF16)
    todo = [(w[n], l) for l in range(L) for n in BIG if (l, n) != (0, "w_in")]
    cast, (w_in_whole,) = _cast_weights(todo, _gather_comm([first_w_in], [GATHER_KIND["w_in"]]))
    cast = iter(cast)
    bf = {(l, n): first_w_in if (l, n) == (0, "w_in") else next(cast) for l in range(L) for n in BIG}

    def shards_of(l):
        d = {n: bf[(l, n)] for n in BIG}
        d.update(caw=caw_p[l], cbw=cbw_p[l])
        return d

    def params_of(l, whole):
        p = dict(whole, cab=row(conv_a_b[l]), cbb=row(conv_b_b[l]),
                 wa=_block_diag(w_rg_a[l]).astype(BF16), wx=_block_diag(w_rg_x[l]).astype(BF16))
        for n in ("g_mix", "b_in", "ln_g", "ln_b", "b_rg_a", "b_rg_x", "lam", "g_mlp"):
            p[n] = row(w[n][l])
        return p

    shards = [shards_of(l) for l in range(L)]
    whole = {"w_in": w_in_whole}
    xf = x.reshape(T, D)
    saved, params = [], []
    for l in range(L):
        cur = lambda names: [(n, shards[l][n], False) for n in names]
        nxt = lambda names: [(n, shards[l + 1][n], True) for n in names]
        if l == 0:
            jobs = {"fwd_z": cur(["w_a_out", "w_b_out", "w_o", "caw", "cbw"]), "conv_a_fwd": cur(["w_1"]),
                    "fwd_f": cur(["w_2"])}
        else:
            jobs = {"fwd_z": cur(["w_1"]), "conv_a_fwd": cur(["w_2"])}
        if l + 1 < L:
            jobs.update({"rglru_fwd": nxt(["w_in"]), "fwd_x1": nxt(["w_o", "caw", "cbw"]),
                         "fwd_x2": nxt(["w_b_out", "w_a_out"])})
        xf, s, p, whole = _layer_fwd(xf, params_of(l, whole), S, jobs)
        saved.append(s)
        params.append(p)
    loss_part, dx, dxb, dg_final = _loss_head(xf, row(g_final), loss_target.reshape(T, D), _tiles(T)[0])
    loss = lax.psum(loss_part[0, 0], ("x", "y", "c"))

    half_shape = lambda a: (L, 2, a.shape[1] // 2, a.shape[2])
    accs = {n: lax.empty(half_shape(w[n]), F32) for n in BIG}
    small = {n: [None] * L for n in SMALL if n != "g_final"}
    red = _Reduce(accs, c_arr, lambda l: jnp.stack([k_me, ci, jnp.full((), l, ci.dtype)]).astype(jnp.int32))
    for l in reversed(range(L)):
        dx, dxb, g = _layer_bwd(dx, dxb, params[l], saved[l], S, red=red, layer=l)
        small["g_mix"][l], small["b_in"][l], small["g_mlp"][l] = g["g_mix"], g["b_in"], g["g_mlp"]
        small["conv_a_w"][l], small["conv_a_b"][l] = g["caw"], g["cab"]
        small["conv_b_w"][l], small["conv_b_b"][l] = g["cbw"], g["cbb"]
        small["ln_g"][l], small["ln_b"][l], small["lam"][l] = g["ln_g"], g["ln_b"], g["lam"]
        small["w_rg_a"][l], small["w_rg_x"][l] = _block_diag_part(g["wa"], dh), _block_diag_part(g["wx"], dh)
        small["b_rg_a"][l], small["b_rg_x"][l] = g["b_rg_a"], g["b_rg_x"]
    grad_x = dx.reshape(B, S, D)

    delta, new_m, new_v = {}, {}, {}
    flat = lambda a: a.reshape(-1, a.shape[-1])

    def adam_big(names, comm=None):
        r = _adamw("adamw_" + names[0], *[[flat(d[n]) for n in names] for d in (w, grads, m, v)], ADAM_ROWS, comm=comm)
        for q, n in enumerate(names):
            delta[n], new_m[n], new_v[n] = (r[a][q].reshape(w[n].shape) for a in range(3))
        return r[3] if comm else None

    late_sums = red.chip_sums(red.late, _comm_call("swap_halves", _swap_comm(red.late)))
    joined, got = _join_halves([red.accs[n] for n in red.EARLY], also=_scatter_comm(late_sums))
    grads = {n: a.reshape(w[n].shape) for n, a in zip(red.EARLY, joined)}
    adam_big(["w_2", "w_1"])
    adam_big(["w_o", "w_a_out"])
    red.finish(red.LATE, late_sums, got, 0)
    joined = _join_halves([red.accs[n] for n in red.LATE])
    grads.update({n: a.reshape(w[n].shape) for n, a in zip(red.LATE, joined)})
    adam_big(["w_b_out"])
    adam_big(["w_in"])

    wide = ["w_rg_a", "w_rg_x"]
    names = [n for n in SMALL if n != "g_final" and n not in wide]
    parts = [jnp.stack(small[n]) for n in names] + [dg_final]
    parts_w = [jnp.stack(small[n]).astype(BF16) for n in wide]
    total, total_w = _allreduce_small([_pack(parts, N_DEV, 8), _pack(parts_w, N_DEV, 16)])
    summed = _unpack(total, parts, 8) + [a.astype(F32) for a in _unpack(total_w, parts_w, 16)]
    for n, a in zip(names + ["g_final"] + wide, summed):
        if n == "conv_a_w":
            a = lax.dynamic_slice_in_dim(a[:, :taps_a], k_me * conv_a_w.shape[2], conv_a_w.shape[2], axis=2)
        elif n == "conv_b_w":
            a = lax.dynamic_slice_in_dim(a[:, :taps_b], k_me * conv_b_w.shape[2], conv_b_w.shape[2], axis=2)
        grads[n] = a.reshape(w[n].shape)

    for n in SMALL:
        cols = w[n].shape[-1]
        view = lambda a: a.reshape(-1, cols)
        rows = view(w[n]).shape[0]
        d_, m_, v_ = _adamw("adamw_" + n, view(w[n]), view(grads[n]), view(m[n]), view(v[n]),
                            ADAM_SMALL_ROWS if rows % ADAM_SMALL_ROWS == 0 else rows)
        delta[n], new_m[n], new_v[n] = (a.reshape(w[n].shape) for a in (d_, m_, v_))

    return (loss, grad_x, *[grads[n] for n in WEIGHTS], *[delta[n] for n in WEIGHTS],
            *[new_m[n] for n in WEIGHTS], *[new_v[n] for n in WEIGHTS])
```

```python
import jax
import jax.numpy as jnp
from jax import lax
from jax.experimental import pallas as pl
from jax.experimental.pallas import tpu as pltpu

F32 = jnp.float32
BF16 = jnp.bfloat16
MESH = pl.DeviceIdType.MESH

EPS = 1e-6
LRU_C = 8.0
ADAM_LR, ADAM_B1, ADAM_B2, ADAM_EPS, ADAM_WD, ADAM_STEP = 0.001, 0.9, 0.999, 1e-08, 0.01, 10

N_CHIPS = 4
HEADS_PER_GROUP = 4
VMEM_LIMIT = 56 * 1024 * 1024


def _cp(**kw):
    return pltpu.CompilerParams(vmem_limit_bytes=VMEM_LIMIT, **kw)


def _sig(x):
    return 1.0 / (1.0 + jnp.exp(-x))


def _gelu(x):
    t = jnp.tanh(0.7978845608028654 * (x + 0.044715 * x * x * x))
    return 0.5 * x * (1.0 + t), t


def _gelu_grad(x, t):
    dt = (1.0 - t * t) * 0.7978845608028654 * (1.0 + 3.0 * 0.044715 * x * x)
    return 0.5 * (1.0 + t) + 0.5 * x * dt


def _rms(xf, g):
    r = lax.rsqrt(jnp.mean(xf * xf, axis=-1, keepdims=True) + EPS)
    return xf * r * g, r


def _rms_bwd(xf, g, r, dh):
    dyg = dh * g
    dx = r * (dyg - xf * (r * r) * jnp.mean(dyg * xf, axis=-1, keepdims=True))
    return dx, dh * xf * r


def _ln_silu(u, g, b):
    mu = jnp.mean(u, axis=-1, keepdims=True)
    uc = u - mu
    rstd = lax.rsqrt(jnp.mean(uc * uc, axis=-1, keepdims=True) + EPS)
    uh = uc * rstd
    u2 = uh * g + b
    s = _sig(u2)
    return u2 * s, uh, rstd, u2, s


_DIMS = {"nn": (((1,), (0,)), ((), ())), "nt": (((1,), (1,)), ((), ())), "tn": (((0,), (0,)), ((), ()))}


PEER_SETS = {"chips+sibling": 1, "chips": 2, "sibling": 3}


def _handshake(peers):
    x, y, c = _mesh_pos()
    devs = ([(*chip, c) for chip in _other_chips(x, y)] if "chips" in peers else []) + \
           ([(x, y, 1 - c)] if "sibling" in peers else [])
    barrier = pltpu.get_barrier_semaphore()
    for dev in devs:
        pl.semaphore_signal(barrier, inc=1, device_id=dev, device_id_type=MESH)
    pl.semaphore_wait(barrier, len(devs))


class _Comm:
    def __init__(self, ins, outs, sems, start, finish, peers):
        self.ins, self.outs, self.sems, self.finish, self.peers = list(ins), list(outs), list(sems), finish, peers
        self.copies_start = start

    def start(self, *refs):
        _handshake(self.peers)
        self.copies_start(*refs)

    @property
    def collective_id(self):
        return PEER_SETS[self.peers]


def _resident(shape):
    return pl.BlockSpec(shape, lambda i, j, k: (0,) * len(shape), pipeline_mode=pl.Buffered(1))


def _mm(name, mode, grid, a_ins, a_fn, b_in, e_ins, epi, outs, acc_shape, cache_a=None, alias=(), extra_scratch=(),
        comm=None, b_slice=None):
    ni, nj, nk = grid
    na, ne, no = len(a_ins), len(e_ins), len(outs)
    assert cache_a is None or nk == 1
    n_fixed = (nk > 1) + (cache_a is not None)
    n_in = na + 1 + ne + len(alias)
    c_ins, c_outs, c_sems = (comm.ins, comm.outs, comm.sems) if comm else ([], [], [])

    def body(*refs):
        a_refs = refs[:na]
        b_ref = refs[na]
        e_refs = refs[na + 1:na + 1 + ne]
        comm_in = refs[n_in:n_in + len(c_ins)]
        out0 = n_in + len(c_ins)
        out_refs = refs[out0:out0 + no]
        comm_out = refs[out0 + no:out0 + no + len(c_outs)]
        scratch = refs[out0 + no + len(c_outs):]
        extra = scratch[n_fixed:n_fixed + len(extra_scratch)]
        comm_sems = scratch[n_fixed + len(extra_scratch):]
        i, j, k = pl.program_id(0), pl.program_id(1), pl.program_id(2)
        if comm:
            @pl.when((i == 0) & (j == 0) & (k == 0))
            def _():
                comm.start(comm_in, comm_out, comm_sems)
        if cache_a is not None:
            cache_ref = scratch[n_fixed - 1]

            @pl.when(j == 0)
            def _():
                cache_ref[...] = a_fn(a_refs, out_refs, i, j, k)

            a = cache_ref[...]
        else:
            a = a_fn(a_refs, out_refs, i, j, k)
        if b_slice is None:
            b = b_ref[...]
        elif b_slice[0] == "cols":
            b = b_ref[:, pl.ds(pl.multiple_of(j * b_slice[1], b_slice[1]), b_slice[1])]
        else:
            b = b_ref[pl.ds(pl.multiple_of(j * b_slice[1], b_slice[1]), b_slice[1]), :]
        prod = lax.dot_general(a, b, _DIMS[mode], preferred_element_type=F32)
        if nk == 1:
            epi(prod, e_refs, out_refs, i, j, extra)
        else:
            acc_ref = scratch[0]

            @pl.when(k == 0)
            def _():
                acc_ref[...] = prod

            @pl.when(k > 0)
            def _():
                acc_ref[...] += prod

            @pl.when(k == nk - 1)
            def _():
                epi(acc_ref[...], e_refs, out_refs, i, j, extra)

        if comm:
            @pl.when((i == ni - 1) & (j == nj - 1) & (k == nk - 1))
            def _():
                comm.finish(comm_in, comm_out, comm_sems)

    scratch_shapes = []
    if nk > 1:
        scratch_shapes.append(pltpu.VMEM(acc_shape, F32))
    if cache_a is not None:
        scratch_shapes.append(pltpu.VMEM(cache_a, BF16))
    any_spec = pl.BlockSpec(memory_space=pl.ANY)
    ins = (list(a_ins) + [b_in] + list(e_ins) + [(arr, any_spec) for arr, _ in alias] + [(arr, any_spec) for arr in c_ins])
    first_alias = na + 1 + ne
    res = pl.pallas_call(
        body, name=name, grid=grid,
        in_specs=[s for _, s in ins], out_specs=[s for _, s in outs] + [any_spec] * len(c_outs),
        out_shape=[o for o, _ in outs] + list(c_outs),
        scratch_shapes=scratch_shapes + list(extra_scratch) + list(c_sems),
        input_output_aliases={first_alias + n: o for n, (_, o) in enumerate(alias)},
        compiler_params=_cp(dimension_semantics=("arbitrary", "arbitrary", "arbitrary"), has_side_effects=bool(comm),
                            collective_id=comm.collective_id if comm else None),
    )(*[a for a, _ in ins])
    if comm:
        return list(res[:no]), list(res[no:])
    return res


def _bs(shape, fn):
    return pl.BlockSpec(shape, fn)


def _sds(shape, dt):
    return jax.ShapeDtypeStruct(shape, dt)


def _acc_rows(ref, val, first):
    @pl.when(first)
    def _():
        ref[...] = val

    @pl.when(jnp.logical_not(first))
    def _():
        ref[...] += val


def _fwd_norm_mm(name, x, g, w, bias, tm, tn, comm=None):
    T, D = x.shape
    N = w.shape[1]

    def a_fn(a_refs, out_refs, i, j, k):
        h, _ = _rms(a_refs[0][...], a_refs[1][...])
        hb = h.astype(BF16)
        out_refs[1][...] = hb
        return hb

    def epi(acc, e_refs, out_refs, i, j, extra):
        if bias is not None:
            acc = acc + e_refs[0][...]
        out_refs[0][...] = acc.astype(BF16)

    e_ins = [] if bias is None else [(bias, _bs((1, tn), lambda i, j, k: (0, j)))]
    return _mm(name, "nn", (T // tm, N // tn, 1),
               [(x, _bs((tm, D), lambda i, j, k: (i, 0))), (g, _bs((1, D), lambda i, j, k: (0, 0)))], a_fn,
               (w, _resident((D, N))), e_ins, epi,
               [(_sds((T, N), BF16), _bs((tm, tn), lambda i, j, k: (i, j))),
                (_sds((T, D), BF16), _bs((tm, D), lambda i, j, k: (i, 0)))],
               None, cache_a=(tm, D), comm=comm, b_slice=("cols", tn))


def _fwd_ya(u1, ln_g, ln_b, w, tm):
    T, C = u1.shape
    N = w.shape[1]

    def a_fn(a_refs, out_refs, i, j, k):
        u3 = _ln_silu(a_refs[0][...].astype(F32), a_refs[1][...], a_refs[2][...])[0].astype(BF16)
        out_refs[1][...] = u3
        return u3

    def epi(acc, e_refs, out_refs, i, j, extra):
        out_refs[0][...] = acc.astype(BF16)

    row = _bs((1, C), lambda i, j, k: (0, 0))
    tc = _bs((tm, C), lambda i, j, k: (i, 0))
    return _mm("fwd_ya", "nn", (T // tm, 1, 1), [(u1, tc), (ln_g, row), (ln_b, row)], a_fn,
               (w, _bs((C, N), lambda i, j, k: (0, 0))), [], epi,
               [(_sds((T, N), BF16), _bs((tm, N), lambda i, j, k: (i, 0))), (_sds((T, C), BF16), tc)], None)


def _fwd_yb(h, z, gb_blk, w, tm, tk):
    T, C = h.shape
    N = w.shape[1]

    def a_fn(a_refs, out_refs, i, j, k):
        ge, _ = _gelu(a_refs[1][...].astype(F32))
        pv = (a_refs[0][...].astype(F32) * ge).astype(BF16)
        out_refs[1][...] = pv
        return pv

    def epi(acc, e_refs, out_refs, i, j, extra):
        out_refs[0][...] = acc.astype(BF16)

    tk_ = _bs((tm, tk), lambda i, j, k: (i, k))
    return _mm("fwd_yb", "nn", (T // tm, 1, C // tk),
               [(h, tk_), (z, _bs((tm, tk), lambda i, j, k: (i, gb_blk + k)))], a_fn,
               (w, _bs((tk, N), lambda i, j, k: (k, 0))), [], epi,
               [(_sds((T, N), BF16), _bs((tm, N), lambda i, j, k: (i, 0))), (_sds((T, C), BF16), tk_)], (tm, N))


def _fwd_x1(x, ya, yb, z, sa_blk, w, tm, comm=None):
    T, D = x.shape

    def a_fn(a_refs, out_refs, i, j, k):
        ya_, yb_, sa_, sb_ = (r[...].astype(F32) for r in a_refs)
        mg = (_sig(sa_) * ya_ + _sig(sb_) * yb_).astype(BF16)
        out_refs[1][...] = mg
        return mg

    def epi(acc, e_refs, out_refs, i, j, extra):
        out_refs[0][...] = e_refs[0][...] + acc

    t = _bs((tm, D), lambda i, j, k: (i, 0))
    return _mm("fwd_x1", "nn", (T // tm, 1, 1),
               [(ya, t), (yb, t), (z, _bs((tm, D), lambda i, j, k: (i, sa_blk))),
                (z, _bs((tm, D), lambda i, j, k: (i, sa_blk + 1)))], a_fn,
               (w, _bs((D, D), lambda i, j, k: (0, 0))), [(x, t)], epi,
               [(_sds((T, D), F32), t), (_sds((T, D), BF16), t)], None, comm=comm)


def _fwd_x2(x1, fp, w, tm, tk, comm=None):
    T, D = x1.shape
    Fd = fp.shape[1]

    def epi(acc, e_refs, out_refs, i, j, extra):
        out_refs[0][...] = e_refs[0][...] + acc

    t = _bs((tm, D), lambda i, j, k: (i, 0))
    whole_k = tk == Fd
    r = _mm("fwd_x2", "nn", (T // tm, 1, Fd // tk),
            [(fp, _bs((tm, tk), lambda i, j, k: (i, k)))], _relu2,
            (w, _resident((Fd, D)) if whole_k else _bs((tk, D), lambda i, j, k: (k, 0))), [(x1, t)], epi,
            [(_sds((T, D), F32), t)], (tm, D), comm=comm)
    return (r[0][0], r[1]) if comm else r[0]


def _relu2(a_refs, out_refs, i, j, k):
    f = jnp.maximum(a_refs[0][...], 0.0)
    return f * f


def _loss_head(x, g, target, tm):
    T, D = x.shape

    def body(x_ref, g_ref, t_ref, loss_ref, dx_ref, dxb_ref, dg_ref):
        i = pl.program_id(0)
        xf, gv = x_ref[...], g_ref[...]
        y, r = _rms(xf, gv)
        err = y - t_ref[...]
        part = 0.5 * jnp.sum(jnp.mean(err * err, axis=-1, keepdims=True), axis=0, keepdims=True)
        dx, dg_rows = _rms_bwd(xf, gv, r, err * (1.0 / D))
        dx_ref[...] = dx
        dxb_ref[...] = dx.astype(BF16)
        _acc_rows(loss_ref, jnp.broadcast_to(part, (1, 128)), i == 0)
        _acc_rows(dg_ref, jnp.sum(dg_rows, axis=0, keepdims=True), i == 0)

    t = _bs((tm, D), lambda i: (i, 0))
    row = _bs((1, D), lambda i: (0, 0))
    return pl.pallas_call(
        body, name="loss_head", grid=(T // tm,), in_specs=[t, row, t],
        out_specs=[_bs((1, 128), lambda i: (0, 0)), t, t, row],
        out_shape=[_sds((1, 128), F32), _sds((T, D), F32), _sds((T, D), BF16), _sds((1, D), F32)],
        compiler_params=_cp(dimension_semantics=("arbitrary",)),
    )(x, g, target)


def _adamw(name, w, g, m, v, tr, comm=None):
    many = isinstance(w, (list, tuple))
    ws, gs, ms, vs = (list(a) if many else [a] for a in (w, g, m, v))
    n = len(ws)
    rows, cols = ws[0].shape
    d1 = 1.0 - ADAM_B1 ** ADAM_STEP
    d2 = 1.0 - ADAM_B2 ** ADAM_STEP

    def body(*refs):
        for q in range(n):
            w_ref, g_ref, m_ref, v_ref = (refs[a * n + q] for a in range(4))
            d_ref, mo_ref, vo_ref = (refs[(4 + a) * n + q] for a in range(3))
            gv = g_ref[...]
            mn = ADAM_B1 * m_ref[...] + (1.0 - ADAM_B1) * gv
            vn = ADAM_B2 * v_ref[...] + (1.0 - ADAM_B2) * (gv * gv)
            d_ref[...] = -ADAM_LR * ((mn / d1) / (jnp.sqrt(vn / d2) + ADAM_EPS) + ADAM_WD * w_ref[...])
            mo_ref[...] = mn
            vo_ref[...] = vn

    t = _bs((tr, cols), lambda i: (i, 0))
    r = _call_with_comm(name, body, (rows // tr,), ws + gs + ms + vs, [t] * (4 * n), [t] * (3 * n),
                        [_sds((rows, cols), F32)] * (3 * n), [], comm)
    outs, got = (r if comm else (r, None))
    res = [outs[a * n:(a + 1) * n] if many else outs[a * n] for a in range(3)]
    return (*res, got) if comm else tuple(res)


def _ident(a_refs, out_refs, i, j, k):
    return a_refs[0][...]


def _bwd_dw(name, act, dy, ti, tj, tm, a_fn=None, a_extra=(), shard_cols=None, keep=None):
    T, J = dy.shape
    I = act.shape[1]

    def epi(acc, e_refs, out_refs, i, j, extra):
        out_refs[0][...] = acc.astype(BF16).reshape(out_refs[0].shape)

    if shard_cols is None:
        out = (_sds((I, J), BF16), _bs((ti, tj), lambda i, j, k: (i, j)))
    else:
        per = shard_cols // tj
        assert ti == I and per * tj == shard_cols
        out = (_sds((J // shard_cols, 2, I // 2, shard_cols), BF16),
               _bs((None, 2, I // 2, tj), lambda i, j, k: (lax.div(j, per), 0, 0, lax.rem(j, per))))
    assert keep is None or tm == T
    a_spec = _resident((T, I)) if keep == "act" else _bs((tm, ti), lambda i, j, k: (k, i))
    b_spec = _resident((T, J)) if keep == "dy" else _bs((tm, tj), lambda i, j, k: (k, j))
    return _mm(name, "tn", (I // ti, J // tj, T // tm), [(act, a_spec)] + list(a_extra), a_fn or _ident,
               (dy, b_spec), [], epi, [out], (ti, tj))[0]


def _bwd_df(dxb, w2, fp, tm, tn, comm=None):
    T, D = dxb.shape
    Fd = w2.shape[0]

    def epi(acc, e_refs, out_refs, i, j, extra):
        out_refs[0][...] = (acc * (2.0 * jnp.maximum(e_refs[0][...].astype(F32), 0.0))).astype(BF16)

    t = _bs((tm, tn), lambda i, j, k: (i, j))
    r = _mm("bwd_df", "nt", (T // tm, Fd // tn, 1), [(dxb, _bs((tm, D), lambda i, j, k: (i, 0)))], _ident,
            (w2, _resident((Fd, D))), [(fp, t)], epi, [(_sds((T, Fd), BF16), t)], None, b_slice=("rows", tn), comm=comm)
    return (r[0][0], r[1]) if comm else r[0]


def _bwd_norm(name, dy, w, x, g, dres, tm, tk, colsum=False, comm=None):
    T, K = dy.shape
    D = w.shape[0]
    nk = K // tk

    def a_fn(a_refs, out_refs, i, j, k):
        a = a_refs[0][...]
        if colsum:
            s = jnp.sum(a.astype(F32), axis=0, keepdims=True)

            @pl.when(i == 0)
            def _():
                out_refs[3][k] = s

            @pl.when(i > 0)
            def _():
                out_refs[3][k] += s
        return a

    def epi(acc, e_refs, out_refs, i, j, extra):
        xf, gv = e_refs[0][...], e_refs[1][...]
        r = lax.rsqrt(jnp.mean(xf * xf, axis=-1, keepdims=True) + EPS)
        dx, dg_rows = _rms_bwd(xf, gv, r, acc)
        dx = dx + e_refs[2][...]
        out_refs[0][...] = dx
        out_refs[1][...] = dx.astype(BF16)
        _acc_rows(out_refs[2], jnp.sum(dg_rows, axis=0, keepdims=True), i == 0)

    t = _bs((tm, D), lambda i, j, k: (i, 0))
    row = _bs((1, D), lambda i, j, k: (0, 0))
    outs = [(_sds((T, D), F32), t), (_sds((T, D), BF16), t), (_sds((1, D), F32), row)]
    if colsum:
        outs.append((_sds((nk, 1, tk), F32), _bs((nk, 1, tk), lambda i, j, k: (0, 0, 0))))
    return _mm(name, "nt", (T // tm, 1, nk), [(dy, _bs((tm, tk), lambda i, j, k: (i, k)))], a_fn,
               (w, _resident((D, K)) if nk == 1 else _bs((D, tk), lambda i, j, k: (0, k))),
               [(x, t), (g, row), (dres, t)], epi, outs, (tm, D), comm=comm)


def _bwd_dm(dxb, w_o, ya, yb, z, sa_blk, tm):
    T, D = dxb.shape

    def epi(acc, e_refs, out_refs, i, j, extra):
        ya_, yb_, sa_, sb_ = (r[...].astype(F32) for r in e_refs)
        ga, gb = _sig(sa_), _sig(sb_)
        out_refs[0][...] = (acc * ga).astype(BF16)
        out_refs[1][...] = (acc * gb).astype(BF16)
        stage, sem = extra
        put = pltpu.make_async_copy(
            stage, out_refs[2].at[pl.ds(pl.multiple_of(i * tm, tm), tm), pl.ds(sa_blk * D, 2 * D)], sem)

        @pl.when(i > 0)
        def _():
            put.wait()

        stage[:, 0:D] = (acc * ya_ * ga * (1.0 - ga)).astype(BF16)
        stage[:, D:2 * D] = (acc * yb_ * gb * (1.0 - gb)).astype(BF16)
        put.start()

        @pl.when(i == T // tm - 1)
        def _():
            put.wait()

    t = _bs((tm, D), lambda i, j, k: (i, 0))
    return _mm("bwd_dm", "nt", (T // tm, 1, 1), [(dxb, t)], _ident, (w_o, _bs((D, D), lambda i, j, k: (0, 0))),
               [(ya, t), (yb, t), (z, _bs((tm, D), lambda i, j, k: (i, sa_blk))),
                (z, _bs((tm, D), lambda i, j, k: (i, sa_blk + 1)))], epi,
               [(_sds((T, D), BF16), t), (_sds((T, D), BF16), t),
                (_sds(z.shape, BF16), pl.BlockSpec(memory_space=pl.ANY))], None,
               extra_scratch=[pltpu.VMEM((tm, 2 * D), BF16), pltpu.SemaphoreType.DMA(())])


def _bwd_du3(dya, w, u1, ln_g, ln_b, tm):
    T, D = dya.shape
    C = w.shape[0]

    def epi(acc, e_refs, out_refs, i, j, extra):
        gv = e_refs[1][...]
        _, uh, rstd, u2, s = _ln_silu(e_refs[0][...].astype(F32), gv, e_refs[2][...])
        du2 = acc * (s * (1.0 + u2 * (1.0 - s)))
        duh = du2 * gv
        out_refs[0][...] = rstd * (duh - jnp.mean(duh, axis=-1, keepdims=True)
                                   - uh * jnp.mean(duh * uh, axis=-1, keepdims=True))
        _acc_rows(out_refs[1], jnp.sum(du2 * uh, axis=0, keepdims=True), i == 0)
        _acc_rows(out_refs[2], jnp.sum(du2, axis=0, keepdims=True), i == 0)

    t = _bs((tm, C), lambda i, j, k: (i, 0))
    row = _bs((1, C), lambda i, j, k: (0, 0))
    return _mm("bwd_du3", "nt", (T // tm, 1, 1), [(dya, _bs((tm, D), lambda i, j, k: (i, 0)))], _ident,
               (w, _bs((C, D), lambda i, j, k: (0, 0))), [(u1, t), (ln_g, row), (ln_b, row)], epi,
               [(_sds((T, C), F32), t), (_sds((1, C), F32), row), (_sds((1, C), F32), row)], None)


def _bwd_dp(dyb, w, h, z, dz, gb_blk, tm, tn, comm=None):
    T, D = dyb.shape
    R = w.shape[0]

    def epi(acc, e_refs, out_refs, i, j, extra):
        gbv = e_refs[1][...].astype(F32)
        ge, th = _gelu(gbv)
        out_refs[0][...] = acc * ge
        out_refs[1][...] = (acc * e_refs[0][...].astype(F32) * _gelu_grad(gbv, th)).astype(BF16)

    t = _bs((tm, tn), lambda i, j, k: (i, j))
    tz = _bs((tm, tn), lambda i, j, k: (i, gb_blk + j))
    return _mm("bwd_dp", "nt", (T // tm, R // tn, 1), [(dyb, _bs((tm, D), lambda i, j, k: (i, 0)))], _ident,
               (w, _bs((tn, D), lambda i, j, k: (j, 0))), [(h, t), (z, tz)], epi,
               [(_sds((T, R), F32), t), (_sds(dz.shape, BF16), tz)], None, cache_a=None, alias=[(dz, 1)], comm=comm)


CONV_ROWS = 64


def _shifted_taps(x, halo, shifts, fn):
    n = CONV_ROWS + halo
    by_r = {}
    for k, s in shifts:
        by_r.setdefault(s % 8, []).append((k, s))
    for r in sorted(by_r):
        xr = x if r == 0 else pltpu.roll(x, n - r, 0)
        for k, s in by_r[r]:
            q = s - r
            fn(k, xr[q:q + CONV_ROWS])


def _conv_fwd(name, z, blk0, gate_blk0, w_pad, bias, taps, seq, tc, out_dtype, comm=None):
    T = z.shape[0]
    C = w_pad.shape[1]
    nb, nj = T // seq, C // tc
    pad = 8 * ((taps - 1 + 7) // 8)
    halo = pad
    shifts = [(k, pad - (taps - 1) + k) for k in range(taps)]
    glu = gate_blk0 is not None

    def body(*refs):
        if glu:
            v_ref, g_ref, w_ref, b_ref, o_ref, p_ref = refs
        else:
            v_ref, w_ref, b_ref, o_ref, p_ref = refs
        p_ref[pl.ds(0, pad), :] = jnp.zeros((pad, tc), F32)
        u = v_ref[...].astype(F32)
        if glu:
            u = u * _sig(g_ref[...].astype(F32))
        p_ref[pl.ds(pad, seq), :] = u

        def step(c, _):
            base = pl.multiple_of(c * CONV_ROWS, CONV_ROWS)
            x = p_ref[pl.ds(base, CONV_ROWS + halo), :]
            acc = [jnp.zeros((CONV_ROWS, tc), F32) + b_ref[...]]

            def tap(k, xs):
                acc[0] = acc[0] + w_ref[k:k + 1, :] * xs

            _shifted_taps(x, halo, shifts, tap)
            o_ref[pl.ds(base, CONV_ROWS), :] = acc[0].astype(out_dtype)
            return 0

        lax.fori_loop(0, seq // CONV_ROWS, step, 0)

    zin = [(z, _bs((seq, tc), lambda b, j: (b, blk0 + j)))]
    if glu:
        zin.append((z, _bs((seq, tc), lambda b, j: (b, gate_blk0 + j))))
    ins = zin + [(w_pad, _bs((w_pad.shape[0], tc), lambda b, j: (0, j))), (bias, _bs((1, tc), lambda b, j: (0, j)))]
    r = _call_with_comm(name, body, (nb, nj), [a for a, _ in ins], [s for _, s in ins],
                        [_bs((seq, tc), lambda b, j: (b, j))], [_sds((T, C), out_dtype)],
                        [pltpu.VMEM((seq + pad, tc), F32)], comm)
    return (r[0][0], r[1]) if comm else r[0]


def _conv_bwd(name, dy, z, dz, blk0, gate_blk0, w_pad, taps, seq, tc, comm=None):
    T = z.shape[0]
    C = w_pad.shape[1]
    nb, nj = T // seq, C // tc
    kp = w_pad.shape[0]
    pad = 8 * ((taps - 1 + 7) // 8)
    halo = pad
    sh_du = [(k, taps - 1 - k) for k in range(taps)]
    sh_dw = [(k, pad - (taps - 1) + k) for k in range(taps)]
    glu = gate_blk0 is not None

    def body(*refs):
        if glu:
            dy_ref, v_ref, g_ref, w_ref, _dz_in, dz_out, dw_ref, db_ref, pdy, pu, du_s, wacc, ob, ob2, osem = refs
        else:
            dy_ref, v_ref, w_ref, _dz_in, dz_out, dw_ref, db_ref, pdy, pu, du_s, wacc, ob, osem = refs
        j = pl.program_id(0)
        b = pl.program_id(1)
        pdy[pl.ds(seq, pad), :] = jnp.zeros((pad, tc), F32)
        pdy[pl.ds(0, seq), :] = dy_ref[...].astype(F32)
        pu[pl.ds(0, pad), :] = jnp.zeros((pad, tc), F32)
        v = v_ref[...].astype(F32)
        if glu:
            sg = _sig(g_ref[...].astype(F32))
            pu[pl.ds(pad, seq), :] = v * sg
        else:
            pu[pl.ds(pad, seq), :] = v
        wacc[...] = jnp.zeros(wacc.shape, F32)

        def step(c, dbacc):
            base = pl.multiple_of(c * CONV_ROWS, CONV_ROWS)
            xdy = pdy[pl.ds(base, CONV_ROWS + halo), :]
            acc = [jnp.zeros((CONV_ROWS, tc), F32)]

            def tap(k, xs):
                acc[0] = acc[0] + w_ref[k:k + 1, :] * xs

            _shifted_taps(xdy, halo, sh_du, tap)
            du_s[pl.ds(base, CONV_ROWS), :] = acc[0]
            dyc = xdy[0:CONV_ROWS]
            xu = pu[pl.ds(base, CONV_ROWS + halo), :]

            def wtap(k, xs):
                p = dyc * xs
                s8 = p[0:8]
                for m in range(1, CONV_ROWS // 8):
                    s8 = s8 + p[8 * m:8 * m + 8]
                wacc[pl.ds(8 * k, 8), :] += s8

            _shifted_taps(xu, halo, sh_dw, wtap)
            d8 = dyc[0:8]
            for m in range(1, CONV_ROWS // 8):
                d8 = d8 + dyc[8 * m:8 * m + 8]
            return dbacc + d8

        dbacc = lax.fori_loop(0, seq // CONV_ROWS, step, jnp.zeros((8, tc), F32))
        du = du_s[...]
        rows = pl.ds(pl.multiple_of(b * seq, seq), seq)
        puts = [pltpu.make_async_copy(ob, dz_out.at[rows, pl.ds(pl.multiple_of((blk0 + j) * tc, tc), tc)], osem.at[0])]
        if glu:
            puts.append(pltpu.make_async_copy(
                ob2, dz_out.at[rows, pl.ds(pl.multiple_of((gate_blk0 + j) * tc, tc), tc)], osem.at[1]))

        @pl.when((j > 0) | (b > 0))
        def _():
            for cp in puts:
                cp.wait()

        if glu:
            ob[...] = (du * sg).astype(BF16)
            ob2[...] = (du * v * sg * (1.0 - sg)).astype(BF16)
        else:
            ob[...] = du.astype(BF16)
        for cp in puts:
            cp.start()

        @pl.when((j == nj - 1) & (b == nb - 1))
        def _():
            for cp in puts:
                cp.wait()
        dw = jnp.sum(wacc[...].reshape(kp, 8, tc), axis=1)
        _acc_rows(dw_ref, dw, b == 0)
        _acc_rows(db_ref, jnp.sum(dbacc, axis=0, keepdims=True), b == 0)

    zin = [(z, _bs((seq, tc), lambda j, b: (b, blk0 + j)))]
    if glu:
        zin.append((z, _bs((seq, tc), lambda j, b: (b, gate_blk0 + j))))
    ins = [(dy, _bs((seq, tc), lambda j, b: (b, j)))] + zin + [(w_pad, _bs((kp, tc), lambda j, b: (0, j))),
                                                               (dz, pl.BlockSpec(memory_space=pl.ANY))]
    dz_idx = len(ins) - 1
    out_specs = [pl.BlockSpec(memory_space=pl.ANY), _bs((kp, tc), lambda j, b: (0, j)), _bs((1, tc), lambda j, b: (0, j))]
    out_shape = [_sds(dz.shape, dz.dtype), _sds((kp, C), F32), _sds((1, C), F32)]
    stage = [pltpu.VMEM((seq, tc), BF16)] * (2 if glu else 1) + [pltpu.SemaphoreType.DMA((2,))]
    return _call_with_comm(
        name, body, (nj, nb), [a for a, _ in ins], [s for _, s in ins], out_specs, out_shape,
        [pltpu.VMEM((seq + pad, tc), F32), pltpu.VMEM((seq + pad, tc), F32),
         pltpu.VMEM((seq, tc), F32), pltpu.VMEM((8 * kp, tc), F32)] + stage, comm, aliases={dz_idx: 0})


RG_ROWS = 256


def _softplus_neg(lam):
    return jnp.maximum(-lam, 0.0) + jnp.log(1.0 + jnp.exp(-jnp.abs(lam)))


def _gates(v0c, wa_ref, wx_ref, ba, bx, sp):
    vb = v0c.astype(BF16)
    r = _sig(jnp.dot(vb, wa_ref[...], preferred_element_type=F32) + ba)
    i = _sig(jnp.dot(vb, wx_ref[...], preferred_element_type=F32) + bx)
    return r, i, -LRU_C * r * sp


def _decay(la, first_row):
    a = jnp.exp(la)
    a2 = a * a
    x = 2.0 * la
    series = -x * (1.0 + x * (0.5 + x * (1.0 / 6)))
    mult = jnp.sqrt(jnp.where(x > -0.01, series, 1.0 - a2))
    dmult = jnp.where(first_row, 0.0, -a2 / mult)
    mult = jnp.where(first_row, 1.0, mult)
    return a, mult, dmult


def _group_scan(a, b, reverse):
    n = a.shape[0]
    row = lax.broadcasted_iota(jnp.int32, a.shape, 0) & 7
    for d in (1, 2, 4):
        sh = n - d if reverse else d
        a_s, b_s = pltpu.roll(a, sh, 0), pltpu.roll(b, sh, 0)
        m = (row < 8 - d) if reverse else (row >= d)
        b = jnp.where(m, a * b_s + b, b)
        a = jnp.where(m, a * a_s, a)
    return a, b


def _group_carry(a_s, b_s, o_s, n_groups, reverse):
    cols = a_s.shape[1]

    def step(g, carry):
        g = n_groups - 1 - g if reverse else g
        rows = pl.ds(pl.multiple_of(g * 8, 8), 8)
        o = a_s[rows, :] * carry + b_s[rows, :]
        o_s[rows, :] = o
        return o[0:1, :] if reverse else o[7:8, :]

    lax.fori_loop(0, n_groups, step, jnp.zeros((1, cols), F32), unroll=2)


def _rglru_fwd(v0, wa, wx, ba, bx, lam, seq, comm=None):
    T, C = v0.shape
    ng, G = wa.shape[0], wa.shape[1]
    nb = T // seq

    def body(v_ref, wa_ref, wx_ref, ba_ref, bx_ref, lam_ref, h_ref, r_ref, i_ref, la_ref, a_s, b_s, h_s):
        sp = _softplus_neg(lam_ref[...])

        def chunk(c, _):
            rows = pl.ds(pl.multiple_of(c * RG_ROWS, RG_ROWS), RG_ROWS)
            t = lax.broadcasted_iota(jnp.int32, (RG_ROWS, G), 0) + c * RG_ROWS
            v0c = v_ref[rows, :]
            r, i, la = _gates(v0c, wa_ref, wx_ref, ba_ref[...], bx_ref[...], sp)
            r_ref[rows, :] = r.astype(BF16)
            i_ref[rows, :] = i.astype(BF16)
            la_ref[rows, :] = la
            a, mult, _ = _decay(la, t == 0)
            a_g, b_g = _group_scan(a, mult * i * v0c, False)
            a_s[rows, :] = a_g
            b_s[rows, :] = b_g
            return 0

        lax.fori_loop(0, seq // RG_ROWS, chunk, 0)
        _group_carry(a_s, b_s, h_s, seq // 8, False)
        h_ref[...] = h_s[...].astype(BF16)

    t2 = _bs((seq, G), lambda b, g: (b, g))
    wsp = _bs((None, G, G), lambda b, g: (g, 0, 0))
    row = _bs((1, G), lambda b, g: (0, g))
    return _call_with_comm("rglru_fwd", body, (nb, ng), [v0, wa, wx, ba, bx, lam], [t2, wsp, wsp, row, row, row],
                           [t2] * 4, [_sds((T, C), BF16)] * 3 + [_sds((T, C), F32)], [pltpu.VMEM((seq, G), F32)] * 3, comm)


def _call_with_comm(name, body, grid, ins, in_specs, out_specs, out_shape, scratch, comm, aliases=None):
    n_in, n_out, n_s = len(ins), len(out_shape), len(scratch)
    c_ins, c_outs, c_sems = (comm.ins, comm.outs, comm.sems) if comm else ([], [], [])

    def wrapped(*refs):
        o0 = n_in + len(c_ins)
        s0 = o0 + n_out + len(c_outs)
        cin, cout, csem = refs[n_in:o0], refs[o0 + n_out:s0], refs[s0 + n_s:]
        ids = [pl.program_id(a) for a in range(len(grid))]
        if comm:
            first = _all_of([i == 0 for i in ids])

            @pl.when(first)
            def _():
                comm.start(cin, cout, csem)

        body(*refs[:n_in], *refs[o0:o0 + n_out], *refs[s0:s0 + n_s])
        if comm:
            last = _all_of([i == n - 1 for i, n in zip(ids, grid)])

            @pl.when(last)
            def _():
                comm.finish(cin, cout, csem)

    res = pl.pallas_call(
        wrapped, name=name, grid=grid, in_specs=list(in_specs) + [ANY] * len(c_ins),
        out_specs=list(out_specs) + [ANY] * len(c_outs), out_shape=list(out_shape) + list(c_outs),
        scratch_shapes=list(scratch) + list(c_sems), input_output_aliases=aliases or {},
        compiler_params=_cp(dimension_semantics=("arbitrary",) * len(grid), has_side_effects=bool(comm),
                            collective_id=comm.collective_id if comm else None),
    )(*ins, *c_ins)
    return (list(res[:n_out]), list(res[n_out:])) if comm else list(res)


def _all_of(conds):
    out = conds[0]
    for c in conds[1:]:
        out = out & c
    return out


def _rglru_bwd(v0, h, dh, r_g, i_g, la_g, wa, wx, lam, seq, comm=None):
    T, C = v0.shape
    ng, G = wa.shape[0], wa.shape[1]
    nb = T // seq
    R = RG_ROWS

    def body(v_ref, h_ref, dh_ref, r_ref, i_ref, la_ref, wa_ref, wx_ref, lam_ref,
             dv_ref, dwa_ref, dwx_ref, dba_ref, dbx_ref, dlam_ref, a_s, b_s, q_s, hp_s):
        b = pl.program_id(1)
        lam_v = lam_ref[...]
        sp = _softplus_neg(lam_v)
        dsp_dlam = -_sig(-lam_v)

        @pl.when(b == 0)
        def _():
            dwa_ref[...] = jnp.zeros((G, G), F32)
            dwx_ref[...] = jnp.zeros((G, G), F32)
            dba_ref[...] = jnp.zeros((1, G), F32)
            dbx_ref[...] = jnp.zeros((1, G), F32)
            dlam_ref[...] = jnp.zeros((1, G), F32)

        hp_s[pl.ds(0, 8), :] = jnp.zeros((8, G), F32)
        hp_s[pl.ds(8, seq), :] = h_ref[...].astype(F32)
        q_s[pl.ds(seq, 8), :] = jnp.zeros((8, G), F32)

        def chunk1(c, _):
            rows = pl.ds(pl.multiple_of(c * R, R), R)
            a = jnp.exp(la_ref[rows, :])
            a_g, b_g = _group_scan(a, a * dh_ref[rows, :].astype(F32), True)
            a_s[rows, :] = a_g
            b_s[rows, :] = b_g
            return 0

        lax.fori_loop(0, seq // R, chunk1, 0)
        _group_carry(a_s, b_s, q_s, seq // 8, True)

        def chunk3(c, _):
            base = pl.multiple_of(c * R, R)
            rows = pl.ds(base, R)
            t = lax.broadcasted_iota(jnp.int32, (R, G), 0) + c * R
            v0c = v_ref[rows, :]
            r, i = r_ref[rows, :].astype(F32), i_ref[rows, :].astype(F32)
            a, mult, dmult_dla = _decay(la_ref[rows, :], t == 0)
            q_next = pltpu.roll(q_s[pl.ds(base, R + 8), :], R + 7, 0)[0:R]
            h_prev = pltpu.roll(hp_s[pl.ds(base, R + 8), :], R + 1, 0)[0:R]
            gt = dh_ref[rows, :].astype(F32) + q_next
            dla = gt * h_prev * a + gt * i * v0c * dmult_dla
            dpa = dla * (-LRU_C * sp) * r * (1.0 - r)
            dpx = gt * mult * v0c * i * (1.0 - i)
            dpa_b, dpx_b, v_b = dpa.astype(BF16), dpx.astype(BF16), v0c.astype(BF16)
            dv_ref[rows, :] = (gt * mult * i
                               + lax.dot_general(dpa_b, wa_ref[...], _DIMS["nt"], preferred_element_type=F32)
                               + lax.dot_general(dpx_b, wx_ref[...], _DIMS["nt"], preferred_element_type=F32))
            dwa_ref[...] += lax.dot_general(v_b, dpa_b, _DIMS["tn"], preferred_element_type=F32)
            dwx_ref[...] += lax.dot_general(v_b, dpx_b, _DIMS["tn"], preferred_element_type=F32)
            dba_ref[...] += jnp.sum(dpa, axis=0, keepdims=True)
            dbx_ref[...] += jnp.sum(dpx, axis=0, keepdims=True)
            dlam_ref[...] += jnp.sum(dla * (-LRU_C * r), axis=0, keepdims=True) * dsp_dlam
            return 0

        lax.fori_loop(0, seq // R, chunk3, 0)

    t2 = _bs((seq, G), lambda g, b: (b, g))
    wsp = _bs((None, G, G), lambda g, b: (g, 0, 0))
    row = _bs((1, G), lambda g, b: (0, g))
    return _call_with_comm(
        "rglru_bwd", body, (ng, nb), [v0, h, dh, r_g, i_g, la_g, wa, wx, lam], [t2] * 6 + [wsp, wsp, row],
        [t2, wsp, wsp, row, row, row],
        [_sds((T, C), F32), _sds((ng, G, G), F32), _sds((ng, G, G), F32),
         _sds((1, C), F32), _sds((1, C), F32), _sds((1, C), F32)],
        [pltpu.VMEM((seq, G), F32), pltpu.VMEM((seq, G), F32),
         pltpu.VMEM((seq + 8, G), F32), pltpu.VMEM((seq + 8, G), F32)], comm)


TC_A = 256
TC_B = 512
TAPS_A, TAPS_B = 31, 4


def _tiles(T):
    return min(512, T), min(1024, T)


GATHERED = ("w_in", "w_1", "w_a_out", "w_b_out", "w_o", "w_2", "caw", "cbw")
GATHER_KIND = {"w_in": (True, True), "w_1": (True, True), "w_a_out": (False, True), "w_b_out": (False, True),
               "w_o": (False, True), "w_2": (False, True), "caw": (True, False), "cbw": (True, False)}


def _layer_fwd(x, p, seq, jobs=None):
    T, D = x.shape
    C, R = p["ln_g"].shape[1], p["lam"].shape[1]
    tm, tl = _tiles(T)
    gb_blk, sa_blk = (2 * C + R) // TC_B, (2 * C + 2 * R) // D
    p, ahead, jobs = dict(p), {}, jobs or {}

    def gather(call):
        js = jobs.get(call)
        return _gather_comm([s for _, s, _ in js], [GATHER_KIND[n] for n, _, _ in js]) if js else None

    def outs(r, call):
        js = jobs.get(call)
        if not js:
            return r
        for (n, _, for_next), whole in zip(js, r[1]):
            (ahead if for_next else p)[n] = whole
        return r[0]

    z, h = outs(_fwd_norm_mm("fwd_z", x, p["g_mix"], p["w_in"], p["b_in"], tl, 1024, comm=gather("fwd_z")), "fwd_z")
    u1 = outs(_conv_fwd("conv_a_fwd", z, 0, C // TC_A, p["caw"], p["cab"], TAPS_A, seq, TC_A, BF16,
                        comm=gather("conv_a_fwd")), "conv_a_fwd")
    ya, u3 = _fwd_ya(u1, p["ln_g"], p["ln_b"], p["w_a_out"], tm)
    v0 = _conv_fwd("conv_b_fwd", z, 2 * C // TC_B, None, p["cbw"], p["cbb"], TAPS_B, seq, TC_B, F32)
    hr, rg, ig, lag = outs(_rglru_fwd(v0, p["wa"], p["wx"], p["b_rg_a"], p["b_rg_x"], p["lam"], seq,
                                      comm=gather("rglru_fwd")), "rglru_fwd")
    yb, pb = _fwd_yb(hr, z, gb_blk, p["w_b_out"], tl, TC_B)
    x1, mg = outs(_fwd_x1(x, ya, yb, z, sa_blk, p["w_o"], tm, comm=gather("fwd_x1")), "fwd_x1")
    fp, h2 = outs(_fwd_norm_mm("fwd_f", x1, p["g_mlp"], p["w_1"], None, tl, 1024, comm=gather("fwd_f")), "fwd_f")
    x2 = outs(_fwd_x2(x1, fp, p["w_2"], tm, fp.shape[1], comm=gather("fwd_x2")), "fwd_x2")
    saved = dict(x=x, z=z, h=h, u1=u1, u3=u3, ya=ya, v0=v0, hr=hr, rg=rg, ig=ig, lag=lag, pb=pb, yb=yb, mg=mg, x1=x1,
                 fp=fp, h2=h2)
    return x2, saved, p, ahead


class _Reduce:
    EARLY = ("w_2", "w_1", "w_o", "w_a_out")
    LATE = ("w_b_out", "w_in")

    def __init__(self, accs, c_arr, kcl_of):
        self.accs, self.c_arr, self.kcl_of, self.late = accs, c_arr, kcl_of, None

    @staticmethod
    def pieces(partials):
        return [a if a.ndim == 4 else a.reshape(N_CHIPS, 2, a.shape[0] // (2 * N_CHIPS), a.shape[1]) for a in partials]

    def chip_sums(self, pgs, swapped):
        return _sum_siblings(pgs, swapped, self.c_arr)

    def finish(self, names, sums, received, layer):
        done = _sum_chips(sums, received, self.kcl_of(layer), [self.accs[n] for n in names])
        self.accs.update(zip(names, done))


def _layer_bwd(dx2, dx2b, p, s, seq, red=None, layer=0):
    T, D = dx2.shape
    C, R = p["ln_g"].shape[1], p["lam"].shape[1]
    tm, tl = _tiles(T)
    gb_blk, sa_blk = (2 * C + R) // TC_B, (2 * C + 2 * R) // D
    z = s["z"]
    g = {}


    late_sums = None
    if red is not None and red.late is not None:
        late, red.late = red.late, None
        dfp, got = _bwd_df(dx2b, p["w_2"], s["fp"], tl, 1024, comm=_swap_comm(late))
        late_sums = red.chip_sums(late, got)
    else:
        dfp = _bwd_df(dx2b, p["w_2"], s["fp"], tl, 1024)
    g["w_2"] = _bwd_dw("bwd_dw2", s["fp"], dx2b, 1024, D, T, a_fn=_relu2, keep="dy")
    dx1, dx1b, g["g_mlp"] = _bwd_norm("bwd_dh2", dfp, p["w_1"], s["x1"], p["g_mlp"], dx2, tm, dfp.shape[1])
    g["w_1"] = _bwd_dw("bwd_dw1", s["h2"], dfp, D, 1024, T, shard_cols=dfp.shape[1] // N_CHIPS, keep="act")

    dya, dyb, dz = _bwd_dm(dx1b, p["w_o"], s["ya"], s["yb"], z, sa_blk, tm)

    g["w_o"] = _bwd_dw("bwd_dwo", s["mg"], dx1b, D, D, tl)
    du1, g["ln_g"], g["ln_b"] = _bwd_du3(dya, p["w_a_out"], s["u1"], p["ln_g"], p["ln_b"], tm)
    g["w_a_out"] = _bwd_dw("bwd_dwa", s["u3"], dya, C, D, tl)
    conv_a_args = ("conv_a_bwd", du1, z, dz, 0, C // TC_A, p["caw"], TAPS_A, seq, TC_A)
    if late_sums is not None:
        (dz, g["caw"], g["cab"]), got = _conv_bwd(*conv_a_args, comm=_scatter_comm(late_sums))
        red.finish(red.LATE, late_sums, got, layer + 1)
    else:
        dz, g["caw"], g["cab"] = _conv_bwd(*conv_a_args)

    dp_args = (dyb, p["w_b_out"], s["hr"], z, dz, gb_blk, tl, TC_B)
    if red is not None:
        early = red.pieces([g.pop(n) for n in red.EARLY])
        (dhr, dz), got = _bwd_dp(*dp_args, comm=_swap_comm(early))
        early_sums = red.chip_sums(early, got)
    else:
        dhr, dz = _bwd_dp(*dp_args)

    g["w_b_out"] = _bwd_dw("bwd_dwb", s["pb"], dyb, R, D, tl)
    rg_args = (s["v0"], s["hr"], dhr, s["rg"], s["ig"], s["lag"], p["wa"], p["wx"], p["lam"], seq)
    if red is not None:
        rg_out, got = _rglru_bwd(*rg_args, comm=_scatter_comm(early_sums))
        red.finish(red.EARLY, early_sums, got, layer)
    else:
        rg_out = _rglru_bwd(*rg_args)
    dv0, g["wa"], g["wx"], g["b_rg_a"], g["b_rg_x"], g["lam"] = rg_out
    dz, g["cbw"], g["cbb"] = _conv_bwd("conv_b_bwd", dv0, z, dz, 2 * C // TC_B, None, p["cbw"], TAPS_B, seq, TC_B)

    dx, dxb, g["g_mix"], dbin = _bwd_norm("bwd_dh", dz, p["w_in"], s["x"], p["g_mix"], dx1, tm, dz.shape[1],
                                          colsum=True)
    g["b_in"] = dbin.reshape(1, -1)
    ns = dz.shape[1] // N_CHIPS
    g["w_in"] = _bwd_dw("bwd_dwin", s["h"], dz, D, ns // 2, T, shard_cols=ns, keep="act")
    if red is not None:
        red.late = red.pieces([g.pop(n) for n in red.LATE])
    return dx, dxb, g


ANY = pl.BlockSpec(memory_space=pl.ANY)


def _mesh_pos():
    return lax.axis_index("x"), lax.axis_index("y"), lax.axis_index("c")


def _other_chips(x, y):
    return [(1 - x, y), (x, 1 - y), (1 - x, 1 - y)]


def _remote(src, dst, ssem, rsem, dev):
    return pltpu.make_async_remote_copy(src_ref=src, dst_ref=dst, send_sem=ssem, recv_sem=rsem,
                                        device_id=dev, device_id_type=MESH)


def _gather_region(src, dst, by_cols, k, half):
    rows, cols = src.shape
    nr = rows if half is None else rows // 2
    r0 = 0 if half is None else half * nr
    if by_cols:
        return dst.at[pl.ds(r0, nr), pl.ds(pl.multiple_of(k * cols, 128), cols)]
    return dst.at[pl.ds(pl.multiple_of(k * rows + r0, 8), nr), :]


def _gather_sends(src, dst, kinds, send, recv):
    x, y, c = _mesh_pos()
    cps = []
    for t in range(len(src)):
        half = c if kinds[t][1] else None
        hr = src[t].shape[0] // 2
        s_ref = src[t].at[pl.ds(c * hr, hr), :] if kinds[t][1] else src[t]
        for j, chip in enumerate(_other_chips(x, y)):
            cps.append(_remote(s_ref, _gather_region(src[t], dst[t], kinds[t][0], 2 * x + y, half),
                               send.at[t, j], recv.at[t, j], (*chip, c)))
    return cps


def _gather_finish(src, dst, kinds, send, recv, fsend, frecv):
    x, y, c = _mesh_pos()
    chips = _other_chips(x, y)
    sib = (x, y, 1 - c)
    n = len(src)
    fwd = []
    for t in range(n):
        half = c if kinds[t][1] else None
        for j, chip in enumerate(chips):
            got = _gather_region(src[t], dst[t], kinds[t][0], 2 * chip[0] + chip[1], half)
            _remote(got, got, send.at[t, j], recv.at[t, j], (*chip, c)).wait_recv()
            if kinds[t][1]:
                cp = _remote(got, got, fsend.at[t, j], frecv.at[t, j], sib)
                cp.start()
                fwd.append(cp)
    for t in range(n):
        if kinds[t][1]:
            for j, chip in enumerate(chips):
                got = _gather_region(src[t], dst[t], kinds[t][0], 2 * chip[0] + chip[1], 1 - c)
                _remote(got, got, fsend.at[t, j], frecv.at[t, j], sib).wait_recv()
    for cp in _gather_sends(src, dst, kinds, send, recv) + fwd:
        cp.wait_send()


def _gather_sems(n):
    sem = pltpu.SemaphoreType.DMA
    return [sem((n, 3)), sem((n, 3)), sem((n, 3)), sem((n, 3))]


def _gather_comm(shards, kinds):
    n = len(shards)

    def whole(s, by_cols):
        return (s.shape[0], N_CHIPS * s.shape[1]) if by_cols else (N_CHIPS * s.shape[0], s.shape[1])

    def own(src, dst, lsem):
        x, y, _ = _mesh_pos()
        return [pltpu.make_async_copy(src[t], _gather_region(src[t], dst[t], kinds[t][0], 2 * x + y, None), lsem.at[t])
                for t in range(n)]

    def start(src, dst, sems):
        for cp in own(src, dst, sems[4]) + _gather_sends(src, dst, kinds, sems[0], sems[1]):
            cp.start()

    def finish(src, dst, sems):
        _gather_finish(src, dst, kinds, *sems[:4])
        for cp in own(src, dst, sems[4]):
            cp.wait()

    return _Comm(shards, [_sds(whole(s, k[0]), s.dtype) for s, k in zip(shards, kinds)],
                 _gather_sems(n) + [pltpu.SemaphoreType.DMA((n,))], start, finish,
                 "chips+sibling" if any(k[1] for k in kinds) else "chips")


def _scatter_comm(ps):
    n = len(ps)

    def copies(src, dst, sems):
        x, y, c = _mesh_pos()
        return [_remote(src[t].at[2 * chip[0] + chip[1]], dst[t].at[j], sems[0].at[t, j], sems[1].at[t, j], (*chip, c))
                for t in range(n) for j, chip in enumerate(_other_chips(x, y))]

    def start(src, dst, sems):
        for cp in copies(src, dst, sems):
            cp.start()

    def finish(src, dst, sems):
        cps = copies(src, dst, sems)
        for cp in cps:
            cp.wait_recv()
        for cp in cps:
            cp.wait_send()

    sem = pltpu.SemaphoreType.DMA
    return _Comm(ps, [_sds((3,) + a.shape[1:], a.dtype) for a in ps], [sem((n, 3)), sem((n, 3))], start, finish, "chips")


def _comm_call(name, comm):
    n_i, n_o = len(comm.ins), len(comm.outs)

    def body(*refs):
        comm.start(refs[:n_i], refs[n_i:n_i + n_o], refs[n_i + n_o:])
        comm.finish(refs[:n_i], refs[n_i:n_i + n_o], refs[n_i + n_o:])

    return pl.pallas_call(
        body, name=name, in_specs=[ANY] * n_i, out_specs=[ANY] * n_o, out_shape=comm.outs, scratch_shapes=comm.sems,
        compiler_params=_cp(has_side_effects=True, collective_id=comm.collective_id),
    )(*comm.ins)


def _swap_comm(pgs):
    n = len(pgs)

    def copies(src, dst, sems):
        x, y, c = _mesh_pos()
        return [_remote(src[t].at[:, 1 - c], dst[t], sems[0].at[t], sems[1].at[t], (x, y, 1 - c)) for t in range(n)]

    def start(src, dst, sems):
        for cp in copies(src, dst, sems):
            cp.start()

    def finish(src, dst, sems):
        cps = copies(src, dst, sems)
        for cp in cps:
            cp.wait_recv()
        for cp in cps:
            cp.wait_send()

    sem = pltpu.SemaphoreType.DMA
    return _Comm(pgs, [_sds((a.shape[0],) + a.shape[2:], a.dtype) for a in pgs], [sem((n,)), sem((n,))], start, finish,
                 "sibling")


def _join_halves(accs, also=None):
    n = len(accs)
    c_ins, c_outs, c_sems = (also.ins, also.outs, also.sems) if also else ([], [], [])
    assert also is None or also.peers == "chips"
    peers = "chips+sibling" if also else "sibling"

    def body(*refs):
        o0 = n + len(c_ins)
        buf = refs[o0:o0 + n]
        send, recv = refs[o0 + n + len(c_outs):o0 + n + len(c_outs) + 2]
        extra = (refs[n:o0], refs[o0 + n:o0 + n + len(c_outs)], refs[o0 + n + len(c_outs) + 2:])
        x, y, c = _mesh_pos()
        _handshake(peers)
        if also:
            also.copies_start(*extra)
        cps = [_remote(buf[t].at[:, c], buf[t].at[:, c], send.at[t], recv.at[t], (x, y, 1 - c)) for t in range(n)]
        for cp in cps:
            cp.start()
        for t in range(n):
            _remote(buf[t].at[:, c], buf[t].at[:, 1 - c], send.at[t], recv.at[t], (x, y, 1 - c)).wait_recv()
        for cp in cps:
            cp.wait_send()
        if also:
            also.finish(*extra)

    sem = pltpu.SemaphoreType.DMA
    res = pl.pallas_call(
        body, name="join_halves", in_specs=[ANY] * (n + len(c_ins)), out_specs=[ANY] * (n + len(c_outs)),
        out_shape=[_sds(a.shape, a.dtype) for a in accs] + list(c_outs),
        scratch_shapes=[sem((n,)), sem((n,))] + list(c_sems),
        input_output_aliases={t: t for t in range(n)},
        compiler_params=_cp(has_side_effects=True, collective_id=PEER_SETS[peers]),
    )(*accs, *c_ins)
    return (list(res[:n]), list(res[n:])) if also else res


def _sum_siblings(pgs, rbs, c_arr):
    n = len(pgs)
    nk = pgs[0].shape[0]

    def body(c_ref, *refs):
        for t in range(n):
            refs[2 * n + t][...] = (refs[t][...].astype(F32) + refs[n + t][...].astype(F32)).astype(BF16)

    half = lambda a: pl.BlockSpec((None,) + a.shape[2:], lambda k, c_ref: (k, 0, 0))
    return pl.pallas_call(
        body, name="sum_siblings",
        grid_spec=pltpu.PrefetchScalarGridSpec(
            num_scalar_prefetch=1, grid=(nk,),
            in_specs=[pl.BlockSpec((None, None) + a.shape[2:], lambda k, c_ref: (k, c_ref[0], 0, 0)) for a in pgs]
            + [half(a) for a in pgs],
            out_specs=[half(a) for a in pgs]),
        out_shape=[_sds((nk,) + a.shape[2:], BF16) for a in pgs],
        compiler_params=_cp(dimension_semantics=("arbitrary",)),
    )(c_arr, *pgs, *rbs)


def _sum_chips(ps, rbs, kcl, accs):
    n = len(ps)

    def body(k_ref, *refs):
        for t in range(n):
            b_ref = refs[n + t]
            refs[3 * n + t][...] = (refs[t][...].astype(F32) + b_ref[0].astype(F32) + b_ref[1].astype(F32)
                                    + b_ref[2].astype(F32))

    qr = lambda a: (a.shape[1] // 2, a.shape[2])
    return pl.pallas_call(
        body, name="sum_chips",
        grid_spec=pltpu.PrefetchScalarGridSpec(
            num_scalar_prefetch=1, grid=(2,),
            in_specs=[pl.BlockSpec((None,) + qr(a), lambda r, k_ref: (k_ref[0], r, 0)) for a in ps]
            + [pl.BlockSpec((3,) + qr(a), lambda r, k_ref: (0, r, 0)) for a in ps] + [ANY] * n,
            out_specs=[pl.BlockSpec((None, None) + qr(a), lambda r, k_ref: (k_ref[2], k_ref[1], r, 0)) for a in ps]),
        out_shape=[_sds(a.shape, F32) for a in accs], input_output_aliases={1 + 2 * n + t: t for t in range(n)},
        compiler_params=_cp(dimension_semantics=("arbitrary",)),
    )(kcl, *ps, *rbs, *accs)


N_DEV = 8


def _allreduce_small(parts):
    n = len(parts)

    def body(*refs):
        p_refs, o_refs, rbufs = refs[:n], refs[n:2 * n], refs[2 * n:3 * n]
        s1, r1, s2, r2 = refs[3 * n:]
        x, y, c = _mesh_pos()
        me = 4 * x + 2 * y + c
        devs = [(d // 4, (d // 2) % 2, d % 2) for d in range(N_DEV)]
        for q in range(n):
            rbufs[q][me] = p_refs[q][me]

        def each_peer(fn):
            for d in range(N_DEV):
                @pl.when(d != me)
                def _():
                    for q in range(n):
                        fn(d, q)

        def first(d, q, to_me):
            return _remote(p_refs[q].at[d], rbufs[q].at[d if to_me else me], s1.at[q, d], r1.at[q, d if to_me else me],
                           devs[d])

        def second(d, q, to_me):
            blk = d if to_me else me
            return _remote(o_refs[q].at[blk], o_refs[q].at[blk], s2.at[q, d], r2.at[q, blk], devs[d])

        each_peer(lambda d, q: first(d, q, False).start())
        each_peer(lambda d, q: first(d, q, True).wait_recv())
        for q in range(n):
            total = rbufs[q][0].astype(F32)
            for d in range(1, N_DEV):
                total = total + rbufs[q][d].astype(F32)
            o_refs[q][me] = total.astype(o_refs[q].dtype)
        each_peer(lambda d, q: second(d, q, False).start())
        each_peer(lambda d, q: second(d, q, True).wait_recv())
        each_peer(lambda d, q: first(d, q, False).wait_send())
        each_peer(lambda d, q: second(d, q, False).wait_send())

    sem = pltpu.SemaphoreType.DMA
    vm = pl.BlockSpec(memory_space=pltpu.VMEM)
    return pl.pallas_call(
        body, name="allreduce_small", in_specs=[vm] * n, out_specs=[vm] * n,
        out_shape=[_sds(a.shape, a.dtype) for a in parts],
        scratch_shapes=[pltpu.VMEM(a.shape, a.dtype) for a in parts] + [sem((n, N_DEV))] * 4,
        compiler_params=_cp(has_side_effects=True),
    )(*parts)


BIG = ("w_in", "w_1", "w_a_out", "w_b_out", "w_o", "w_2")
BY_COLS = {"w_in": True, "w_1": True, "w_a_out": False, "w_b_out": False, "w_o": False, "w_2": False}
WEIGHTS = ("g_mix", "w_in", "b_in", "conv_a_w", "conv_a_b", "ln_g", "ln_b", "w_a_out", "conv_b_w", "conv_b_b", "w_rg_a",
           "b_rg_a", "w_rg_x", "b_rg_x", "lam", "w_b_out", "w_o", "g_mlp", "w_1", "w_2", "g_final")
SMALL = tuple(n for n in WEIGHTS if n not in BIG)
ADAM_ROWS = 256
ADAM_SMALL_ROWS = 2048


CAST_STEPS = 8


def _cast_weights(todo, comm):
    def body(*refs):
        n = len(todo)
        for q in range(n):
            refs[n + q][...] = refs[q][...].astype(BF16)

    tile = lambda a: (a.shape[1] // CAST_STEPS, a.shape[2])
    in_specs = [pl.BlockSpec((None,) + tile(a), lambda r, l=l: (l, r, 0)) for a, l in todo]
    out_specs = [pl.BlockSpec(tile(a), lambda r: (r, 0)) for a, _ in todo]
    return _call_with_comm("cast_weights", body, (CAST_STEPS,), [a for a, _ in todo], in_specs, out_specs,
                           [_sds(a.shape[1:], BF16) for a, _ in todo], [], comm)


def _block_diag(w):
    nh, dh, _ = w.shape
    ng = nh // HEADS_PER_GROUP
    w4 = w.reshape(ng, HEADS_PER_GROUP, dh, dh)
    eye = jnp.eye(HEADS_PER_GROUP, dtype=w.dtype)
    return jnp.einsum("qhij,hk->qhikj", w4, eye).reshape(ng, HEADS_PER_GROUP * dh, HEADS_PER_GROUP * dh)


def _block_diag_part(d, dh):
    ng = d.shape[0]
    eye = jnp.eye(HEADS_PER_GROUP, dtype=d.dtype)
    d5 = d.reshape(ng, HEADS_PER_GROUP, dh, HEADS_PER_GROUP, dh)
    return jnp.einsum("qhikj,hk->qhij", d5, eye).reshape(ng * HEADS_PER_GROUP, dh, dh)


PACK_LANES = 128


def _pack(arrays, blocks, tile_rows):
    parts = [a.reshape(-1, PACK_LANES) for a in arrays]
    parts = [jnp.pad(p, ((0, -p.shape[0] % tile_rows), (0, 0))) if p.shape[0] % tile_rows else p for p in parts]
    rows = sum(p.shape[0] for p in parts)
    pad = -rows % (blocks * tile_rows)
    if pad:
        parts.append(jnp.zeros((pad, PACK_LANES), parts[0].dtype))
    return jnp.concatenate(parts, axis=0).reshape(blocks, -1, PACK_LANES)


def _unpack(buf, like, tile_rows):
    buf = buf.reshape(-1, PACK_LANES)
    out, off = [], 0
    for a in like:
        n = a.size // PACK_LANES
        out.append(buf[off:off + n].reshape(a.shape))
        off += n + (-n % tile_rows)
    return out


def kernel(x, g_mix, w_in, b_in, conv_a_w, conv_a_b, ln_g, ln_b, w_a_out, conv_b_w, conv_b_b, w_rg_a, b_rg_a, w_rg_x, b_rg_x, lam, w_b_out, w_o, g_mlp, w_1, w_2, g_final, loss_target, m_g_mix, m_w_in, m_b_in, m_conv_a_w, m_conv_a_b, m_ln_g, m_ln_b, m_w_a_out, m_conv_b_w, m_conv_b_b, m_w_rg_a, m_b_rg_a, m_w_rg_x, m_b_rg_x, m_lam, m_w_b_out, m_w_o, m_g_mlp, m_w_1, m_w_2, m_g_final, v_g_mix, v_w_in, v_b_in, v_conv_a_w, v_conv_a_b, v_ln_g, v_ln_b, v_w_a_out, v_conv_b_w, v_conv_b_b, v_w_rg_a, v_b_rg_a, v_w_rg_x, v_b_rg_x, v_lam, v_w_b_out, v_w_o, v_g_mlp, v_w_1, v_w_2, v_g_final):
    w = dict(g_mix=g_mix, w_in=w_in, b_in=b_in, conv_a_w=conv_a_w, conv_a_b=conv_a_b, ln_g=ln_g, ln_b=ln_b, w_a_out=w_a_out,
             conv_b_w=conv_b_w, conv_b_b=conv_b_b, w_rg_a=w_rg_a, b_rg_a=b_rg_a, w_rg_x=w_rg_x, b_rg_x=b_rg_x, lam=lam,
             w_b_out=w_b_out, w_o=w_o, g_mlp=g_mlp, w_1=w_1, w_2=w_2, g_final=g_final)
    m = dict(g_mix=m_g_mix, w_in=m_w_in, b_in=m_b_in, conv_a_w=m_conv_a_w, conv_a_b=m_conv_a_b, ln_g=m_ln_g, ln_b=m_ln_b,
             w_a_out=m_w_a_out, conv_b_w=m_conv_b_w, conv_b_b=m_conv_b_b, w_rg_a=m_w_rg_a, b_rg_a=m_b_rg_a, w_rg_x=m_w_rg_x,
             b_rg_x=m_b_rg_x, lam=m_lam, w_b_out=m_w_b_out, w_o=m_w_o, g_mlp=m_g_mlp, w_1=m_w_1, w_2=m_w_2, g_final=m_g_final)
    v = dict(g_mix=v_g_mix, w_in=v_w_in, b_in=v_b_in, conv_a_w=v_conv_a_w, conv_a_b=v_conv_a_b, ln_g=v_ln_g, ln_b=v_ln_b,
             w_a_out=v_w_a_out, conv_b_w=v_conv_b_w, conv_b_b=v_conv_b_b, w_rg_a=v_w_rg_a, b_rg_a=v_b_rg_a, w_rg_x=v_w_rg_x,
             b_rg_x=v_b_rg_x, lam=v_lam, w_b_out=v_w_b_out, w_o=v_w_o, g_mlp=v_g_mlp, w_1=v_w_1, w_2=v_w_2, g_final=v_g_final)
    B, S, D = x.shape
    T = B * S
    L = w_in.shape[0]
    dh = w_rg_a.shape[-1]
    taps_a, taps_b = conv_a_w.shape[1], conv_b_w.shape[1]
    assert (taps_a, taps_b) == (TAPS_A, TAPS_B)
    xi, yi, ci = _mesh_pos()
    c_arr = jnp.reshape(ci, (1,)).astype(jnp.int32)
    k_me = 2 * xi + yi

    caw_p = jnp.pad(conv_a_w, ((0, 0), (0, 32 - taps_a), (0, 0)))
    cbw_p = jnp.pad(conv_b_w, ((0, 0), (0, 8 - taps_b), (0, 0)))
    row = lambda a: a.reshape(1, -1)

    first_w_in = w_in[0].astype(BF16)
    todo = [(w[n], l) for l in range(L) for n in BIG if (l, n) != (0, "w_in")]
    cast, (w_in_whole,) = _cast_weights(todo, _gather_comm([first_w_in], [GATHER_KIND["w_in"]]))
    cast = iter(cast)
    bf = {(l, n): first_w_in if (l, n) == (0, "w_in") else next(cast) for l in range(L) for n in BIG}

    def shards_of(l):
        d = {n: bf[(l, n)] for n in BIG}
        d.update(caw=caw_p[l], cbw=cbw_p[l])
        return d

    def params_of(l, whole):
        p = dict(whole, cab=row(conv_a_b[l]), cbb=row(conv_b_b[l]),
                 wa=_block_diag(w_rg_a[l]).astype(BF16), wx=_block_diag(w_rg_x[l]).astype(BF16))
        for n in ("g_mix", "b_in", "ln_g", "ln_b", "b_rg_a", "b_rg_x", "lam", "g_mlp"):
            p[n] = row(w[n][l])
        return p

    shards = [shards_of(l) for l in range(L)]
    whole = {"w_in": w_in_whole}
    xf = x.reshape(T, D)
    saved, params = [], []
    for l in range(L):
        cur = lambda names: [(n, shards[l][n], False) for n in names]
        nxt = lambda names: [(n, shards[l + 1][n], True) for n in names]
        if l == 0:
            jobs = {"fwd_z": cur(["w_a_out", "w_b_out", "w_o", "caw", "cbw"]), "conv_a_fwd": cur(["w_1"]),
                    "fwd_f": cur(["w_2"])}
        else:
            jobs = {"fwd_z": cur(["w_1"]), "conv_a_fwd": cur(["w_2"])}
        if l + 1 < L:
            jobs.update({"rglru_fwd": nxt(["w_in"]), "fwd_x1": nxt(["w_o", "caw", "cbw"]),
                         "fwd_x2": nxt(["w_b_out", "w_a_out"])})
        xf, s, p, whole = _layer_fwd(xf, params_of(l, whole), S, jobs)
        saved.append(s)
        params.append(p)
    loss_part, dx, dxb, dg_final = _loss_head(xf, row(g_final), loss_target.reshape(T, D), _tiles(T)[0])
    loss = lax.psum(loss_part[0, 0], ("x", "y", "c"))

    half_shape = lambda a: (L, 2, a.shape[1] // 2, a.shape[2])
    accs = {n: lax.empty(half_shape(w[n]), F32) for n in BIG}
    small = {n: [None] * L for n in SMALL if n != "g_final"}
    red = _Reduce(accs, c_arr, lambda l: jnp.stack([k_me, ci, jnp.full((), l, ci.dtype)]).astype(jnp.int32))
    for l in reversed(range(L)):
        dx, dxb, g = _layer_bwd(dx, dxb, params[l], saved[l], S, red=red, layer=l)
        small["g_mix"][l], small["b_in"][l], small["g_mlp"][l] = g["g_mix"], g["b_in"], g["g_mlp"]
        small["conv_a_w"][l], small["conv_a_b"][l] = g["caw"], g["cab"]
        small["conv_b_w"][l], small["conv_b_b"][l] = g["cbw"], g["cbb"]
        small["ln_g"][l], small["ln_b"][l], small["lam"][l] = g["ln_g"], g["ln_b"], g["lam"]
        small["w_rg_a"][l], small["w_rg_x"][l] = _block_diag_part(g["wa"], dh), _block_diag_part(g["wx"], dh)
        small["b_rg_a"][l], small["b_rg_x"][l] = g["b_rg_a"], g["b_rg_x"]
    grad_x = dx.reshape(B, S, D)

    delta, new_m, new_v = {}, {}, {}
    flat = lambda a: a.reshape(-1, a.shape[-1])

    def adam_big(names, comm=None):
        r = _adamw("adamw_" + names[0], *[[flat(d[n]) for n in names] for d in (w, grads, m, v)], ADAM_ROWS, comm=comm)
        for q, n in enumerate(names):
            delta[n], new_m[n], new_v[n] = (r[a][q].reshape(w[n].shape) for a in range(3))
        return r[3] if comm else None

    late_sums = red.chip_sums(red.late, _comm_call("swap_halves", _swap_comm(red.late)))
    joined, got = _join_halves([red.accs[n] for n in red.EARLY], also=_scatter_comm(late_sums))
    grads = {n: a.reshape(w[n].shape) for n, a in zip(red.EARLY, joined)}
    adam_big(["w_2", "w_1"])
    adam_big(["w_o", "w_a_out"])
    red.finish(red.LATE, late_sums, got, 0)
    joined = _join_halves([red.accs[n] for n in red.LATE])
    grads.update({n: a.reshape(w[n].shape) for n, a in zip(red.LATE, joined)})
    adam_big(["w_b_out"])
    adam_big(["w_in"])

    wide = ["w_rg_a", "w_rg_x"]
    names = [n for n in SMALL if n != "g_final" and n not in wide]
    parts = [jnp.stack(small[n]) for n in names] + [dg_final]
    parts_w = [jnp.stack(small[n]).astype(BF16) for n in wide]
    total, total_w = _allreduce_small([_pack(parts, N_DEV, 8), _pack(parts_w, N_DEV, 16)])
    summed = _unpack(total, parts, 8) + [a.astype(F32) for a in _unpack(total_w, parts_w, 16)]
    for n, a in zip(names + ["g_final"] + wide, summed):
        if n == "conv_a_w":
            a = lax.dynamic_slice_in_dim(a[:, :taps_a], k_me * conv_a_w.shape[2], conv_a_w.shape[2], axis=2)
        elif n == "conv_b_w":
            a = lax.dynamic_slice_in_dim(a[:, :taps_b], k_me * conv_b_w.shape[2], conv_b_w.shape[2], axis=2)
        grads[n] = a.reshape(w[n].shape)

    for n in SMALL:
        cols = w[n].shape[-1]
        view = lambda a: a.reshape(-1, cols)
        rows = view(w[n]).shape[0]
        d_, m_, v_ = _adamw("adamw_" + n, view(w[n]), view(grads[n]), view(m[n]), view(v[n]),
                            ADAM_SMALL_ROWS if rows % ADAM_SMALL_ROWS == 0 else rows)
        delta[n], new_m[n], new_v[n] = (a.reshape(w[n].shape) for a in (d_, m_, v_))

    return (loss, grad_x, *[grads[n] for n in WEIGHTS], *[delta[n] for n in WEIGHTS],
            *[new_m[n] for n in WEIGHTS], *[new_v[n] for n in WEIGHTS])
```

```python
import jax
import jax.numpy as jnp
from jax import lax
from jax.experimental import pallas as pl
from jax.experimental.pallas import tpu as pltpu

F32 = jnp.float32
BF16 = jnp.bfloat16
MESH = pl.DeviceIdType.MESH

EPS = 1e-6
LRU_C = 8.0
ADAM_LR, ADAM_B1, ADAM_B2, ADAM_EPS, ADAM_WD, ADAM_STEP = 0.001, 0.9, 0.999, 1e-08, 0.01, 10

N_CHIPS = 4
HEADS_PER_GROUP = 4
VMEM_LIMIT = 56 * 1024 * 1024


def _cp(**kw):
    return pltpu.CompilerParams(vmem_limit_bytes=VMEM_LIMIT, **kw)


def _sig(x):
    return 1.0 / (1.0 + jnp.exp(-x))


def _gelu(x):
    t = jnp.tanh(0.7978845608028654 * (x + 0.044715 * x * x * x))
    return 0.5 * x * (1.0 + t), t


def _gelu_grad(x, t):
    dt = (1.0 - t * t) * 0.7978845608028654 * (1.0 + 3.0 * 0.044715 * x * x)
    return 0.5 * (1.0 + t) + 0.5 * x * dt


def _rms(xf, g):
    r = lax.rsqrt(jnp.mean(xf * xf, axis=-1, keepdims=True) + EPS)
    return xf * r * g, r


def _rms_bwd(xf, g, r, dh):
    dyg = dh * g
    dx = r * (dyg - xf * (r * r) * jnp.mean(dyg * xf, axis=-1, keepdims=True))
    return dx, dh * xf * r


def _ln_silu(u, g, b):
    mu = jnp.mean(u, axis=-1, keepdims=True)
    uc = u - mu
    rstd = lax.rsqrt(jnp.mean(uc * uc, axis=-1, keepdims=True) + EPS)
    uh = uc * rstd
    u2 = uh * g + b
    s = _sig(u2)
    return u2 * s, uh, rstd, u2, s


_DIMS = {"nn": (((1,), (0,)), ((), ())), "nt": (((1,), (1,)), ((), ())), "tn": (((0,), (0,)), ((), ()))}


PEER_SETS = {"chips+sibling": 1, "chips": 2, "sibling": 3}


def _handshake(peers):
    x, y, c = _mesh_pos()
    devs = ([(*chip, c) for chip in _other_chips(x, y)] if "chips" in peers else []) + \
           ([(x, y, 1 - c)] if "sibling" in peers else [])
    barrier = pltpu.get_barrier_semaphore()
    for dev in devs:
        pl.semaphore_signal(barrier, inc=1, device_id=dev, device_id_type=MESH)
    pl.semaphore_wait(barrier, len(devs))


class _Comm:
    def __init__(self, ins, outs, sems, start, finish, peers):
        self.ins, self.outs, self.sems, self.finish, self.peers = list(ins), list(outs), list(sems), finish, peers
        self.copies_start = start

    def start(self, *refs):
        _handshake(self.peers)
        self.copies_start(*refs)

    @property
    def collective_id(self):
        return PEER_SETS[self.peers]


def _resident(shape):
    return pl.BlockSpec(shape, lambda i, j, k: (0,) * len(shape), pipeline_mode=pl.Buffered(1))


def _mm(name, mode, grid, a_ins, a_fn, b_in, e_ins, epi, outs, acc_shape, cache_a=None, alias=(), extra_scratch=(),
        comm=None, b_slice=None):
    ni, nj, nk = grid
    na, ne, no = len(a_ins), len(e_ins), len(outs)
    assert cache_a is None or nk == 1
    n_fixed = (nk > 1) + (cache_a is not None)
    n_in = na + 1 + ne + len(alias)
    c_ins, c_outs, c_sems = (comm.ins, comm.outs, comm.sems) if comm else ([], [], [])

    def body(*refs):
        a_refs = refs[:na]
        b_ref = refs[na]
        e_refs = refs[na + 1:na + 1 + ne]
        comm_in = refs[n_in:n_in + len(c_ins)]
        out0 = n_in + len(c_ins)
        out_refs = refs[out0:out0 + no]
        comm_out = refs[out0 + no:out0 + no + len(c_outs)]
        scratch = refs[out0 + no + len(c_outs):]
        extra = scratch[n_fixed:n_fixed + len(extra_scratch)]
        comm_sems = scratch[n_fixed + len(extra_scratch):]
        i, j, k = pl.program_id(0), pl.program_id(1), pl.program_id(2)
        if comm:
            @pl.when((i == 0) & (j == 0) & (k == 0))
            def _():
                comm.start(comm_in, comm_out, comm_sems)
        if cache_a is not None:
            cache_ref = scratch[n_fixed - 1]

            @pl.when(j == 0)
            def _():
                cache_ref[...] = a_fn(a_refs, out_refs, i, j, k)

            a = cache_ref[...]
        else:
            a = a_fn(a_refs, out_refs, i, j, k)
        if b_slice is None:
            b = b_ref[...]
        elif b_slice[0] == "cols":
            b = b_ref[:, pl.ds(pl.multiple_of(j * b_slice[1], b_slice[1]), b_slice[1])]
        else:
            b = b_ref[pl.ds(pl.multiple_of(j * b_slice[1], b_slice[1]), b_slice[1]), :]
        prod = lax.dot_general(a, b, _DIMS[mode], preferred_element_type=F32)
        if nk == 1:
            epi(prod, e_refs, out_refs, i, j, extra)
        else:
            acc_ref = scratch[0]

            @pl.when(k == 0)
            def _():
                acc_ref[...] = prod

            @pl.when(k > 0)
            def _():
                acc_ref[...] += prod

            @pl.when(k == nk - 1)
            def _():
                epi(acc_ref[...], e_refs, out_refs, i, j, extra)

        if comm:
            @pl.when((i == ni - 1) & (j == nj - 1) & (k == nk - 1))
            def _():
                comm.finish(comm_in, comm_out, comm_sems)

    scratch_shapes = []
    if nk > 1:
        scratch_shapes.append(pltpu.VMEM(acc_shape, F32))
    if cache_a is not None:
        scratch_shapes.append(pltpu.VMEM(cache_a, BF16))
    any_spec = pl.BlockSpec(memory_space=pl.ANY)
    ins = (list(a_ins) + [b_in] + list(e_ins) + [(arr, any_spec) for arr, _ in alias] + [(arr, any_spec) for arr in c_ins])
    first_alias = na + 1 + ne
    res = pl.pallas_call(
        body, name=name, grid=grid,
        in_specs=[s for _, s in ins], out_specs=[s for _, s in outs] + [any_spec] * len(c_outs),
        out_shape=[o for o, _ in outs] + list(c_outs),
        scratch_shapes=scratch_shapes + list(extra_scratch) + list(c_sems),
        input_output_aliases={first_alias + n: o for n, (_, o) in enumerate(alias)},
        compiler_params=_cp(dimension_semantics=("arbitrary", "arbitrary", "arbitrary"), has_side_effects=bool(comm),
                            collective_id=comm.collective_id if comm else None),
    )(*[a for a, _ in ins])
    if comm:
        return list(res[:no]), list(res[no:])
    return res


def _bs(shape, fn):
    return pl.BlockSpec(shape, fn)


def _sds(shape, dt):
    return jax.ShapeDtypeStruct(shape, dt)


def _acc_rows(ref, val, first):
    @pl.when(first)
    def _():
        ref[...] = val

    @pl.when(jnp.logical_not(first))
    def _():
        ref[...] += val


def _fwd_norm_mm(name, x, g, w, bias, tm, tn, comm=None):
    T, D = x.shape
    N = w.shape[1]

    def a_fn(a_refs, out_refs, i, j, k):
        h, _ = _rms(a_refs[0][...], a_refs[1][...])
        hb = h.astype(BF16)
        out_refs[1][...] = hb
        return hb

    def epi(acc, e_refs, out_refs, i, j, extra):
        if bias is not None:
            acc = acc + e_refs[0][...]
        out_refs[0][...] = acc.astype(BF16)

    e_ins = [] if bias is None else [(bias, _bs((1, tn), lambda i, j, k: (0, j)))]
    return _mm(name, "nn", (T // tm, N // tn, 1),
               [(x, _bs((tm, D), lambda i, j, k: (i, 0))), (g, _bs((1, D), lambda i, j, k: (0, 0)))], a_fn,
               (w, _resident((D, N))), e_ins, epi,
               [(_sds((T, N), BF16), _bs((tm, tn), lambda i, j, k: (i, j))),
                (_sds((T, D), BF16), _bs((tm, D), lambda i, j, k: (i, 0)))],
               None, cache_a=(tm, D), comm=comm, b_slice=("cols", tn))


def _fwd_ya(u1, ln_g, ln_b, w, tm):
    T, C = u1.shape
    N = w.shape[1]

    def a_fn(a_refs, out_refs, i, j, k):
        u3 = _ln_silu(a_refs[0][...].astype(F32), a_refs[1][...], a_refs[2][...])[0].astype(BF16)
        out_refs[1][...] = u3
        return u3

    def epi(acc, e_refs, out_refs, i, j, extra):
        out_refs[0][...] = acc.astype(BF16)

    row = _bs((1, C), lambda i, j, k: (0, 0))
    tc = _bs((tm, C), lambda i, j, k: (i, 0))
    return _mm("fwd_ya", "nn", (T // tm, 1, 1), [(u1, tc), (ln_g, row), (ln_b, row)], a_fn,
               (w, _bs((C, N), lambda i, j, k: (0, 0))), [], epi,
               [(_sds((T, N), BF16), _bs((tm, N), lambda i, j, k: (i, 0))), (_sds((T, C), BF16), tc)], None)


def _fwd_yb(h, z, gb_blk, w, tm, tk):
    T, C = h.shape
    N = w.shape[1]

    def a_fn(a_refs, out_refs, i, j, k):
        ge, _ = _gelu(a_refs[1][...].astype(F32))
        pv = (a_refs[0][...].astype(F32) * ge).astype(BF16)
        out_refs[1][...] = pv
        return pv

    def epi(acc, e_refs, out_refs, i, j, extra):
        out_refs[0][...] = acc.astype(BF16)

    tk_ = _bs((tm, tk), lambda i, j, k: (i, k))
    return _mm("fwd_yb", "nn", (T // tm, 1, C // tk),
               [(h, tk_), (z, _bs((tm, tk), lambda i, j, k: (i, gb_blk + k)))], a_fn,
               (w, _bs((tk, N), lambda i, j, k: (k, 0))), [], epi,
               [(_sds((T, N), BF16), _bs((tm, N), lambda i, j, k: (i, 0))), (_sds((T, C), BF16), tk_)], (tm, N))


def _fwd_x1(x, ya, yb, z, sa_blk, w, tm, comm=None):
    T, D = x.shape

    def a_fn(a_refs, out_refs, i, j, k):
        ya_, yb_, sa_, sb_ = (r[...].astype(F32) for r in a_refs)
        mg = (_sig(sa_) * ya_ + _sig(sb_) * yb_).astype(BF16)
        out_refs[1][...] = mg
        return mg

    def epi(acc, e_refs, out_refs, i, j, extra):
        out_refs[0][...] = e_refs[0][...] + acc

    t = _bs((tm, D), lambda i, j, k: (i, 0))
    return _mm("fwd_x1", "nn", (T // tm, 1, 1),
               [(ya, t), (yb, t), (z, _bs((tm, D), lambda i, j, k: (i, sa_blk))),
                (z, _bs((tm, D), lambda i, j, k: (i, sa_blk + 1)))], a_fn,
               (w, _bs((D, D), lambda i, j, k: (0, 0))), [(x, t)], epi,
               [(_sds((T, D), F32), t), (_sds((T, D), BF16), t)], None, comm=comm)


def _fwd_x2(x1, fp, w, tm, tk, comm=None):
    T, D = x1.shape
    Fd = fp.shape[1]

    def epi(acc, e_refs, out_refs, i, j, extra):
        out_refs[0][...] = e_refs[0][...] + acc

    t = _bs((tm, D), lambda i, j, k: (i, 0))
    whole_k = tk == Fd
    r = _mm("fwd_x2", "nn", (T // tm, 1, Fd // tk),
            [(fp, _bs((tm, tk), lambda i, j, k: (i, k)))], _relu2,
            (w, _resident((Fd, D)) if whole_k else _bs((tk, D), lambda i, j, k: (k, 0))), [(x1, t)], epi,
            [(_sds((T, D), F32), t)], (tm, D), comm=comm)
    return (r[0][0], r[1]) if comm else r[0]


def _relu2(a_refs, out_refs, i, j, k):
    f = jnp.maximum(a_refs[0][...], 0.0)
    return f * f


def _loss_head(x, g, target, tm):
    T, D = x.shape

    def body(x_ref, g_ref, t_ref, loss_ref, dx_ref, dxb_ref, dg_ref):
        i = pl.program_id(0)
        xf, gv = x_ref[...], g_ref[...]
        y, r = _rms(xf, gv)
        err = y - t_ref[...]
        part = 0.5 * jnp.sum(jnp.mean(err * err, axis=-1, keepdims=True), axis=0, keepdims=True)
        dx, dg_rows = _rms_bwd(xf, gv, r, err * (1.0 / D))
        dx_ref[...] = dx
        dxb_ref[...] = dx.astype(BF16)
        _acc_rows(loss_ref, jnp.broadcast_to(part, (1, 128)), i == 0)
        _acc_rows(dg_ref, jnp.sum(dg_rows, axis=0, keepdims=True), i == 0)

    t = _bs((tm, D), lambda i: (i, 0))
    row = _bs((1, D), lambda i: (0, 0))
    return pl.pallas_call(
        body, name="loss_head", grid=(T // tm,), in_specs=[t, row, t],
        out_specs=[_bs((1, 128), lambda i: (0, 0)), t, t, row],
        out_shape=[_sds((1, 128), F32), _sds((T, D), F32), _sds((T, D), BF16), _sds((1, D), F32)],
        compiler_params=_cp(dimension_semantics=("arbitrary",)),
    )(x, g, target)


def _adamw(name, w, g, m, v, tr, comm=None):
    many = isinstance(w, (list, tuple))
    ws, gs, ms, vs = (list(a) if many else [a] for a in (w, g, m, v))
    n = len(ws)
    rows, cols = ws[0].shape
    d1 = 1.0 - ADAM_B1 ** ADAM_STEP
    d2 = 1.0 - ADAM_B2 ** ADAM_STEP

    def body(*refs):
        for q in range(n):
            w_ref, g_ref, m_ref, v_ref = (refs[a * n + q] for a in range(4))
            d_ref, mo_ref, vo_ref = (refs[(4 + a) * n + q] for a in range(3))
            gv = g_ref[...]
            mn = ADAM_B1 * m_ref[...] + (1.0 - ADAM_B1) * gv
            vn = ADAM_B2 * v_ref[...] + (1.0 - ADAM_B2) * (gv * gv)
            d_ref[...] = -ADAM_LR * ((mn / d1) / (jnp.sqrt(vn / d2) + ADAM_EPS) + ADAM_WD * w_ref[...])
            mo_ref[...] = mn
            vo_ref[...] = vn

    t = _bs((tr, cols), lambda i: (i, 0))
    r = _call_with_comm(name, body, (rows // tr,), ws + gs + ms + vs, [t] * (4 * n), [t] * (3 * n),
                        [_sds((rows, cols), F32)] * (3 * n), [], comm)
    outs, got = (r if comm else (r, None))
    res = [outs[a * n:(a + 1) * n] if many else outs[a * n] for a in range(3)]
    return (*res, got) if comm else tuple(res)


def _ident(a_refs, out_refs, i, j, k):
    return a_refs[0][...]


def _bwd_dw(name, act, dy, ti, tj, tm, a_fn=None, a_extra=(), shard_cols=None, keep=None):
    T, J = dy.shape
    I = act.shape[1]

    def epi(acc, e_refs, out_refs, i, j, extra):
        out_refs[0][...] = acc.astype(BF16).reshape(out_refs[0].shape)

    if shard_cols is None:
        out = (_sds((I, J), BF16), _bs((ti, tj), lambda i, j, k: (i, j)))
    else:
        per = shard_cols // tj
        assert ti == I and per * tj == shard_cols
        out = (_sds((J // shard_cols, 2, I // 2, shard_cols), BF16),
               _bs((None, 2, I // 2, tj), lambda i, j, k: (lax.div(j, per), 0, 0, lax.rem(j, per))))
    assert keep is None or tm == T
    a_spec = _resident((T, I)) if keep == "act" else _bs((tm, ti), lambda i, j, k: (k, i))
    b_spec = _resident((T, J)) if keep == "dy" else _bs((tm, tj), lambda i, j, k: (k, j))
    return _mm(name, "tn", (I // ti, J // tj, T // tm), [(act, a_spec)] + list(a_extra), a_fn or _ident,
               (dy, b_spec), [], epi, [out], (ti, tj))[0]


def _bwd_df(dxb, w2, fp, tm, tn, comm=None):
    T, D = dxb.shape
    Fd = w2.shape[0]

    def epi(acc, e_refs, out_refs, i, j, extra):
        out_refs[0][...] = (acc * (2.0 * jnp.maximum(e_refs[0][...].astype(F32), 0.0))).astype(BF16)

    t = _bs((tm, tn), lambda i, j, k: (i, j))
    r = _mm("bwd_df", "nt", (T // tm, Fd // tn, 1), [(dxb, _bs((tm, D), lambda i, j, k: (i, 0)))], _ident,
            (w2, _resident((Fd, D))), [(fp, t)], epi, [(_sds((T, Fd), BF16), t)], None, b_slice=("rows", tn), comm=comm)
    return (r[0][0], r[1]) if comm else r[0]


def _bwd_norm(name, dy, w, x, g, dres, tm, tk, colsum=False, comm=None):
    T, K = dy.shape
    D = w.shape[0]
    nk = K // tk

    def a_fn(a_refs, out_refs, i, j, k):
        a = a_refs[0][...]
        if colsum:
            s = jnp.sum(a.astype(F32), axis=0, keepdims=True)

            @pl.when(i == 0)
            def _():
                out_refs[3][k] = s

            @pl.when(i > 0)
            def _():
                out_refs[3][k] += s
        return a

    def epi(acc, e_refs, out_refs, i, j, extra):
        xf, gv = e_refs[0][...], e_refs[1][...]
        r = lax.rsqrt(jnp.mean(xf * xf, axis=-1, keepdims=True) + EPS)
        dx, dg_rows = _rms_bwd(xf, gv, r, acc)
        dx = dx + e_refs[2][...]
        out_refs[0][...] = dx
        out_refs[1][...] = dx.astype(BF16)
        _acc_rows(out_refs[2], jnp.sum(dg_rows, axis=0, keepdims=True), i == 0)

    t = _bs((tm, D), lambda i, j, k: (i, 0))
    row = _bs((1, D), lambda i, j, k: (0, 0))
    outs = [(_sds((T, D), F32), t), (_sds((T, D), BF16), t), (_sds((1, D), F32), row)]
    if colsum:
        outs.append((_sds((nk, 1, tk), F32), _bs((nk, 1, tk), lambda i, j, k: (0, 0, 0))))
    return _mm(name, "nt", (T // tm, 1, nk), [(dy, _bs((tm, tk), lambda i, j, k: (i, k)))], a_fn,
               (w, _resident((D, K)) if nk == 1 else _bs((D, tk), lambda i, j, k: (0, k))),
               [(x, t), (g, row), (dres, t)], epi, outs, (tm, D), comm=comm)


def _bwd_dm(dxb, w_o, ya, yb, z, sa_blk, tm):
    T, D = dxb.shape

    def epi(acc, e_refs, out_refs, i, j, extra):
        ya_, yb_, sa_, sb_ = (r[...].astype(F32) for r in e_refs)
        ga, gb = _sig(sa_), _sig(sb_)
        out_refs[0][...] = (acc * ga).astype(BF16)
        out_refs[1][...] = (acc * gb).astype(BF16)
        stage, sem = extra
        put = pltpu.make_async_copy(
            stage, out_refs[2].at[pl.ds(pl.multiple_of(i * tm, tm), tm), pl.ds(sa_blk * D, 2 * D)], sem)

        @pl.when(i > 0)
        def _():
            put.wait()

        stage[:, 0:D] = (acc * ya_ * ga * (1.0 - ga)).astype(BF16)
        stage[:, D:2 * D] = (acc * yb_ * gb * (1.0 - gb)).astype(BF16)
        put.start()

        @pl.when(i == T // tm - 1)
        def _():
            put.wait()

    t = _bs((tm, D), lambda i, j, k: (i, 0))
    return _mm("bwd_dm", "nt", (T // tm, 1, 1), [(dxb, t)], _ident, (w_o, _bs((D, D), lambda i, j, k: (0, 0))),
               [(ya, t), (yb, t), (z, _bs((tm, D), lambda i, j, k: (i, sa_blk))),
                (z, _bs((tm, D), lambda i, j, k: (i, sa_blk + 1)))], epi,
               [(_sds((T, D), BF16), t), (_sds((T, D), BF16), t),
                (_sds(z.shape, BF16), pl.BlockSpec(memory_space=pl.ANY))], None,
               extra_scratch=[pltpu.VMEM((tm, 2 * D), BF16), pltpu.SemaphoreType.DMA(())])


def _bwd_du3(dya, w, u1, ln_g, ln_b, tm):
    T, D = dya.shape
    C = w.shape[0]

    def epi(acc, e_refs, out_refs, i, j, extra):
        gv = e_refs[1][...]
        _, uh, rstd, u2, s = _ln_silu(e_refs[0][...].astype(F32), gv, e_refs[2][...])
        du2 = acc * (s * (1.0 + u2 * (1.0 - s)))
        duh = du2 * gv
        out_refs[0][...] = rstd * (duh - jnp.mean(duh, axis=-1, keepdims=True)
                                   - uh * jnp.mean(duh * uh, axis=-1, keepdims=True))
        _acc_rows(out_refs[1], jnp.sum(du2 * uh, axis=0, keepdims=True), i == 0)
        _acc_rows(out_refs[2], jnp.sum(du2, axis=0, keepdims=True), i == 0)

    t = _bs((tm, C), lambda i, j, k: (i, 0))
    row = _bs((1, C), lambda i, j, k: (0, 0))
    return _mm("bwd_du3", "nt", (T // tm, 1, 1), [(dya, _bs((tm, D), lambda i, j, k: (i, 0)))], _ident,
               (w, _bs((C, D), lambda i, j, k: (0, 0))), [(u1, t), (ln_g, row), (ln_b, row)], epi,
               [(_sds((T, C), F32), t), (_sds((1, C), F32), row), (_sds((1, C), F32), row)], None)


def _bwd_dp(dyb, w, h, z, dz, gb_blk, tm, tn, comm=None):
    T, D = dyb.shape
    R = w.shape[0]

    def epi(acc, e_refs, out_refs, i, j, extra):
        gbv = e_refs[1][...].astype(F32)
        ge, th = _gelu(gbv)
        out_refs[0][...] = acc * ge
        out_refs[1][...] = (acc * e_refs[0][...].astype(F32) * _gelu_grad(gbv, th)).astype(BF16)

    t = _bs((tm, tn), lambda i, j, k: (i, j))
    tz = _bs((tm, tn), lambda i, j, k: (i, gb_blk + j))
    return _mm("bwd_dp", "nt", (T // tm, R // tn, 1), [(dyb, _bs((tm, D), lambda i, j, k: (i, 0)))], _ident,
               (w, _bs((tn, D), lambda i, j, k: (j, 0))), [(h, t), (z, tz)], epi,
               [(_sds((T, R), F32), t), (_sds(dz.shape, BF16), tz)], None, cache_a=None, alias=[(dz, 1)], comm=comm)


CONV_ROWS = 128


def _shifted_taps(x, halo, shifts, fn):
    n = CONV_ROWS + halo
    by_r = {}
    for k, s in shifts:
        by_r.setdefault(s % 8, []).append((k, s))
    for r in sorted(by_r):
        xr = x if r == 0 else pltpu.roll(x, n - r, 0)
        for k, s in by_r[r]:
            q = s - r
            fn(k, xr[q:q + CONV_ROWS])


def _conv_fwd(name, z, blk0, gate_blk0, w_pad, bias, taps, seq, tc, out_dtype, comm=None):
    T = z.shape[0]
    C = w_pad.shape[1]
    nb, nj = T // seq, C // tc
    pad = 8 * ((taps - 1 + 7) // 8)
    halo = pad
    shifts = [(k, pad - (taps - 1) + k) for k in range(taps)]
    glu = gate_blk0 is not None

    def body(*refs):
        if glu:
            v_ref, g_ref, w_ref, b_ref, o_ref, p_ref = refs
        else:
            v_ref, w_ref, b_ref, o_ref, p_ref = refs
        p_ref[pl.ds(0, pad), :] = jnp.zeros((pad, tc), F32)
        u = v_ref[...].astype(F32)
        if glu:
            u = u * _sig(g_ref[...].astype(F32))
        p_ref[pl.ds(pad, seq), :] = u

        def step(c, _):
            base = pl.multiple_of(c * CONV_ROWS, CONV_ROWS)
            x = p_ref[pl.ds(base, CONV_ROWS + halo), :]
            acc = [jnp.zeros((CONV_ROWS, tc), F32) + b_ref[...]]

            def tap(k, xs):
                acc[0] = acc[0] + w_ref[k:k + 1, :] * xs

            _shifted_taps(x, halo, shifts, tap)
            o_ref[pl.ds(base, CONV_ROWS), :] = acc[0].astype(out_dtype)
            return 0

        lax.fori_loop(0, seq // CONV_ROWS, step, 0)

    zin = [(z, _bs((seq, tc), lambda b, j: (b, blk0 + j)))]
    if glu:
        zin.append((z, _bs((seq, tc), lambda b, j: (b, gate_blk0 + j))))
    ins = zin + [(w_pad, _bs((w_pad.shape[0], tc), lambda b, j: (0, j))), (bias, _bs((1, tc), lambda b, j: (0, j)))]
    r = _call_with_comm(name, body, (nb, nj), [a for a, _ in ins], [s for _, s in ins],
                        [_bs((seq, tc), lambda b, j: (b, j))], [_sds((T, C), out_dtype)],
                        [pltpu.VMEM((seq + pad, tc), F32)], comm)
    return (r[0][0], r[1]) if comm else r[0]


def _conv_bwd(name, dy, z, dz, blk0, gate_blk0, w_pad, taps, seq, tc, comm=None):
    T = z.shape[0]
    C = w_pad.shape[1]
    nb, nj = T // seq, C // tc
    kp = w_pad.shape[0]
    pad = 8 * ((taps - 1 + 7) // 8)
    halo = pad
    sh_du = [(k, taps - 1 - k) for k in range(taps)]
    sh_dw = [(k, pad - (taps - 1) + k) for k in range(taps)]
    glu = gate_blk0 is not None

    def body(*refs):
        if glu:
            dy_ref, v_ref, g_ref, w_ref, _dz_in, dz_out, dw_ref, db_ref, pdy, pu, du_s, wacc, ob, ob2, osem = refs
        else:
            dy_ref, v_ref, w_ref, _dz_in, dz_out, dw_ref, db_ref, pdy, pu, du_s, wacc, ob, osem = refs
        j = pl.program_id(0)
        b = pl.program_id(1)
        pdy[pl.ds(seq, pad), :] = jnp.zeros((pad, tc), F32)
        pdy[pl.ds(0, seq), :] = dy_ref[...].astype(F32)
        pu[pl.ds(0, pad), :] = jnp.zeros((pad, tc), F32)
        v = v_ref[...].astype(F32)
        if glu:
            sg = _sig(g_ref[...].astype(F32))
            pu[pl.ds(pad, seq), :] = v * sg
        else:
            pu[pl.ds(pad, seq), :] = v
        wacc[...] = jnp.zeros(wacc.shape, F32)

        def step(c, dbacc):
            base = pl.multiple_of(c * CONV_ROWS, CONV_ROWS)
            xdy = pdy[pl.ds(base, CONV_ROWS + halo), :]
            acc = [jnp.zeros((CONV_ROWS, tc), F32)]

            def tap(k, xs):
                acc[0] = acc[0] + w_ref[k:k + 1, :] * xs

            _shifted_taps(xdy, halo, sh_du, tap)
            du_s[pl.ds(base, CONV_ROWS), :] = acc[0]
            dyc = xdy[0:CONV_ROWS]
            xu = pu[pl.ds(base, CONV_ROWS + halo), :]

            def wtap(k, xs):
                p = dyc * xs
                s8 = p[0:8]
                for m in range(1, CONV_ROWS // 8):
                    s8 = s8 + p[8 * m:8 * m + 8]
                wacc[pl.ds(8 * k, 8), :] += s8

            _shifted_taps(xu, halo, sh_dw, wtap)
            d8 = dyc[0:8]
            for m in range(1, CONV_ROWS // 8):
                d8 = d8 + dyc[8 * m:8 * m + 8]
            return dbacc + d8

        dbacc = lax.fori_loop(0, seq // CONV_ROWS, step, jnp.zeros((8, tc), F32))
        du = du_s[...]
        rows = pl.ds(pl.multiple_of(b * seq, seq), seq)
        puts = [pltpu.make_async_copy(ob, dz_out.at[rows, pl.ds(pl.multiple_of((blk0 + j) * tc, tc), tc)], osem.at[0])]
        if glu:
            puts.append(pltpu.make_async_copy(
                ob2, dz_out.at[rows, pl.ds(pl.multiple_of((gate_blk0 + j) * tc, tc), tc)], osem.at[1]))

        @pl.when((j > 0) | (b > 0))
        def _():
            for cp in puts:
                cp.wait()

        if glu:
            ob[...] = (du * sg).astype(BF16)
            ob2[...] = (du * v * sg * (1.0 - sg)).astype(BF16)
        else:
            ob[...] = du.astype(BF16)
        for cp in puts:
            cp.start()

        @pl.when((j == nj - 1) & (b == nb - 1))
        def _():
            for cp in puts:
                cp.wait()
        dw = jnp.sum(wacc[...].reshape(kp, 8, tc), axis=1)
        _acc_rows(dw_ref, dw, b == 0)
        _acc_rows(db_ref, jnp.sum(dbacc, axis=0, keepdims=True), b == 0)

    zin = [(z, _bs((seq, tc), lambda j, b: (b, blk0 + j)))]
    if glu:
        zin.append((z, _bs((seq, tc), lambda j, b: (b, gate_blk0 + j))))
    ins = [(dy, _bs((seq, tc), lambda j, b: (b, j)))] + zin + [(w_pad, _bs((kp, tc), lambda j, b: (0, j))),
                                                               (dz, pl.BlockSpec(memory_space=pl.ANY))]
    dz_idx = len(ins) - 1
    out_specs = [pl.BlockSpec(memory_space=pl.ANY), _bs((kp, tc), lambda j, b: (0, j)), _bs((1, tc), lambda j, b: (0, j))]
    out_shape = [_sds(dz.shape, dz.dtype), _sds((kp, C), F32), _sds((1, C), F32)]
    stage = [pltpu.VMEM((seq, tc), BF16)] * (2 if glu else 1) + [pltpu.SemaphoreType.DMA((2,))]
    return _call_with_comm(
        name, body, (nj, nb), [a for a, _ in ins], [s for _, s in ins], out_specs, out_shape,
        [pltpu.VMEM((seq + pad, tc), F32), pltpu.VMEM((seq + pad, tc), F32),
         pltpu.VMEM((seq, tc), F32), pltpu.VMEM((8 * kp, tc), F32)] + stage, comm, aliases={dz_idx: 0})


RG_ROWS = 256


def _softplus_neg(lam):
    return jnp.maximum(-lam, 0.0) + jnp.log(1.0 + jnp.exp(-jnp.abs(lam)))


def _gates(v0c, wa_ref, wx_ref, ba, bx, sp):
    vb = v0c.astype(BF16)
    r = _sig(jnp.dot(vb, wa_ref[...], preferred_element_type=F32) + ba)
    i = _sig(jnp.dot(vb, wx_ref[...], preferred_element_type=F32) + bx)
    return r, i, -LRU_C * r * sp


def _decay(la, first_row):
    a = jnp.exp(la)
    a2 = a * a
    x = 2.0 * la
    series = -x * (1.0 + x * (0.5 + x * (1.0 / 6)))
    mult = jnp.sqrt(jnp.where(x > -0.01, series, 1.0 - a2))
    dmult = jnp.where(first_row, 0.0, -a2 / mult)
    mult = jnp.where(first_row, 1.0, mult)
    return a, mult, dmult


def _group_scan(a, b, reverse):
    n = a.shape[0]
    row = lax.broadcasted_iota(jnp.int32, a.shape, 0) & 7
    for d in (1, 2, 4):
        sh = n - d if reverse else d
        a_s, b_s = pltpu.roll(a, sh, 0), pltpu.roll(b, sh, 0)
        m = (row < 8 - d) if reverse else (row >= d)
        b = jnp.where(m, a * b_s + b, b)
        a = jnp.where(m, a * a_s, a)
    return a, b


def _group_carry(a_s, b_s, o_s, n_groups, reverse):
    cols = a_s.shape[1]

    def step(g, carry):
        g = n_groups - 1 - g if reverse else g
        rows = pl.ds(pl.multiple_of(g * 8, 8), 8)
        o = a_s[rows, :] * carry + b_s[rows, :]
        o_s[rows, :] = o
        return o[0:1, :] if reverse else o[7:8, :]

    lax.fori_loop(0, n_groups, step, jnp.zeros((1, cols), F32), unroll=2)


def _rglru_fwd(v0, wa, wx, ba, bx, lam, seq, comm=None):
    T, C = v0.shape
    ng, G = wa.shape[0], wa.shape[1]
    nb = T // seq

    def body(v_ref, wa_ref, wx_ref, ba_ref, bx_ref, lam_ref, h_ref, r_ref, i_ref, la_ref, a_s, b_s, h_s):
        sp = _softplus_neg(lam_ref[...])

        def chunk(c, _):
            rows = pl.ds(pl.multiple_of(c * RG_ROWS, RG_ROWS), RG_ROWS)
            t = lax.broadcasted_iota(jnp.int32, (RG_ROWS, G), 0) + c * RG_ROWS
            v0c = v_ref[rows, :]
            r, i, la = _gates(v0c, wa_ref, wx_ref, ba_ref[...], bx_ref[...], sp)
            r_ref[rows, :] = r.astype(BF16)
            i_ref[rows, :] = i.astype(BF16)
            la_ref[rows, :] = la
            a, mult, _ = _decay(la, t == 0)
            a_g, b_g = _group_scan(a, mult * i * v0c, False)
            a_s[rows, :] = a_g
            b_s[rows, :] = b_g
            return 0

        lax.fori_loop(0, seq // RG_ROWS, chunk, 0)
        _group_carry(a_s, b_s, h_s, seq // 8, False)
        h_ref[...] = h_s[...].astype(BF16)

    t2 = _bs((seq, G), lambda b, g: (b, g))
    wsp = _bs((None, G, G), lambda b, g: (g, 0, 0))
    row = _bs((1, G), lambda b, g: (0, g))
    return _call_with_comm("rglru_fwd", body, (nb, ng), [v0, wa, wx, ba, bx, lam], [t2, wsp, wsp, row, row, row],
                           [t2] * 4, [_sds((T, C), BF16)] * 3 + [_sds((T, C), F32)], [pltpu.VMEM((seq, G), F32)] * 3, comm)


def _call_with_comm(name, body, grid, ins, in_specs, out_specs, out_shape, scratch, comm, aliases=None):
    n_in, n_out, n_s = len(ins), len(out_shape), len(scratch)
    c_ins, c_outs, c_sems = (comm.ins, comm.outs, comm.sems) if comm else ([], [], [])

    def wrapped(*refs):
        o0 = n_in + len(c_ins)
        s0 = o0 + n_out + len(c_outs)
        cin, cout, csem = refs[n_in:o0], refs[o0 + n_out:s0], refs[s0 + n_s:]
        ids = [pl.program_id(a) for a in range(len(grid))]
        if comm:
            first = _all_of([i == 0 for i in ids])

            @pl.when(first)
            def _():
                comm.start(cin, cout, csem)

        body(*refs[:n_in], *refs[o0:o0 + n_out], *refs[s0:s0 + n_s])
        if comm:
            last = _all_of([i == n - 1 for i, n in zip(ids, grid)])

            @pl.when(last)
            def _():
                comm.finish(cin, cout, csem)

    res = pl.pallas_call(
        wrapped, name=name, grid=grid, in_specs=list(in_specs) + [ANY] * len(c_ins),
        out_specs=list(out_specs) + [ANY] * len(c_outs), out_shape=list(out_shape) + list(c_outs),
        scratch_shapes=list(scratch) + list(c_sems), input_output_aliases=aliases or {},
        compiler_params=_cp(dimension_semantics=("arbitrary",) * len(grid), has_side_effects=bool(comm),
                            collective_id=comm.collective_id if comm else None),
    )(*ins, *c_ins)
    return (list(res[:n_out]), list(res[n_out:])) if comm else list(res)


def _all_of(conds):
    out = conds[0]
    for c in conds[1:]:
        out = out & c
    return out


def _rglru_bwd(v0, h, dh, r_g, i_g, la_g, wa, wx, lam, seq, comm=None):
    T, C = v0.shape
    ng, G = wa.shape[0], wa.shape[1]
    nb = T // seq
    R = RG_ROWS

    def body(v_ref, h_ref, dh_ref, r_ref, i_ref, la_ref, wa_ref, wx_ref, lam_ref,
             dv_ref, dwa_ref, dwx_ref, dba_ref, dbx_ref, dlam_ref, a_s, b_s, q_s, hp_s):
        b = pl.program_id(1)
        lam_v = lam_ref[...]
        sp = _softplus_neg(lam_v)
        dsp_dlam = -_sig(-lam_v)

        @pl.when(b == 0)
        def _():
            dwa_ref[...] = jnp.zeros((G, G), F32)
            dwx_ref[...] = jnp.zeros((G, G), F32)
            dba_ref[...] = jnp.zeros((1, G), F32)
            dbx_ref[...] = jnp.zeros((1, G), F32)
            dlam_ref[...] = jnp.zeros((1, G), F32)

        hp_s[pl.ds(0, 8), :] = jnp.zeros((8, G), F32)
        hp_s[pl.ds(8, seq), :] = h_ref[...].astype(F32)
        q_s[pl.ds(seq, 8), :] = jnp.zeros((8, G), F32)

        def chunk1(c, _):
            rows = pl.ds(pl.multiple_of(c * R, R), R)
            a = jnp.exp(la_ref[rows, :])
            a_g, b_g = _group_scan(a, a * dh_ref[rows, :].astype(F32), True)
            a_s[rows, :] = a_g
            b_s[rows, :] = b_g
            return 0

        lax.fori_loop(0, seq // R, chunk1, 0)
        _group_carry(a_s, b_s, q_s, seq // 8, True)

        def chunk3(c, _):
            base = pl.multiple_of(c * R, R)
            rows = pl.ds(base, R)
            t = lax.broadcasted_iota(jnp.int32, (R, G), 0) + c * R
            v0c = v_ref[rows, :]
            r, i = r_ref[rows, :].astype(F32), i_ref[rows, :].astype(F32)
            a, mult, dmult_dla = _decay(la_ref[rows, :], t == 0)
            q_next = pltpu.roll(q_s[pl.ds(base, R + 8), :], R + 7, 0)[0:R]
            h_prev = pltpu.roll(hp_s[pl.ds(base, R + 8), :], R + 1, 0)[0:R]
            gt = dh_ref[rows, :].astype(F32) + q_next
            dla = gt * h_prev * a + gt * i * v0c * dmult_dla
            dpa = dla * (-LRU_C * sp) * r * (1.0 - r)
            dpx = gt * mult * v0c * i * (1.0 - i)
            dpa_b, dpx_b, v_b = dpa.astype(BF16), dpx.astype(BF16), v0c.astype(BF16)
            dv_ref[rows, :] = (gt * mult * i
                               + lax.dot_general(dpa_b, wa_ref[...], _DIMS["nt"], preferred_element_type=F32)
                               + lax.dot_general(dpx_b, wx_ref[...], _DIMS["nt"], preferred_element_type=F32))
            dwa_ref[...] += lax.dot_general(v_b, dpa_b, _DIMS["tn"], preferred_element_type=F32)
            dwx_ref[...] += lax.dot_general(v_b, dpx_b, _DIMS["tn"], preferred_element_type=F32)
            dba_ref[...] += jnp.sum(dpa, axis=0, keepdims=True)
            dbx_ref[...] += jnp.sum(dpx, axis=0, keepdims=True)
            dlam_ref[...] += jnp.sum(dla * (-LRU_C * r), axis=0, keepdims=True) * dsp_dlam
            return 0

        lax.fori_loop(0, seq // R, chunk3, 0)

    t2 = _bs((seq, G), lambda g, b: (b, g))
    wsp = _bs((None, G, G), lambda g, b: (g, 0, 0))
    row = _bs((1, G), lambda g, b: (0, g))
    return _call_with_comm(
        "rglru_bwd", body, (ng, nb), [v0, h, dh, r_g, i_g, la_g, wa, wx, lam], [t2] * 6 + [wsp, wsp, row],
        [t2, wsp, wsp, row, row, row],
        [_sds((T, C), F32), _sds((ng, G, G), F32), _sds((ng, G, G), F32),
         _sds((1, C), F32), _sds((1, C), F32), _sds((1, C), F32)],
        [pltpu.VMEM((seq, G), F32), pltpu.VMEM((seq, G), F32),
         pltpu.VMEM((seq + 8, G), F32), pltpu.VMEM((seq + 8, G), F32)], comm)


TC_A = 256
TC_B = 512
TAPS_A, TAPS_B = 31, 4


def _tiles(T):
    return min(512, T), min(1024, T)


GATHERED = ("w_in", "w_1", "w_a_out", "w_b_out", "w_o", "w_2", "caw", "cbw")
GATHER_KIND = {"w_in": (True, True), "w_1": (True, True), "w_a_out": (False, True), "w_b_out": (False, True),
               "w_o": (False, True), "w_2": (False, True), "caw": (True, False), "cbw": (True, False)}


def _layer_fwd(x, p, seq, jobs=None):
    T, D = x.shape
    C, R = p["ln_g"].shape[1], p["lam"].shape[1]
    tm, tl = _tiles(T)
    gb_blk, sa_blk = (2 * C + R) // TC_B, (2 * C + 2 * R) // D
    p, ahead, jobs = dict(p), {}, jobs or {}

    def gather(call):
        js = jobs.get(call)
        return _gather_comm([s for _, s, _ in js], [GATHER_KIND[n] for n, _, _ in js]) if js else None

    def outs(r, call):
        js = jobs.get(call)
        if not js:
            return r
        for (n, _, for_next), whole in zip(js, r[1]):
            (ahead if for_next else p)[n] = whole
        return r[0]

    z, h = outs(_fwd_norm_mm("fwd_z", x, p["g_mix"], p["w_in"], p["b_in"], tl, 1024, comm=gather("fwd_z")), "fwd_z")
    u1 = outs(_conv_fwd("conv_a_fwd", z, 0, C // TC_A, p["caw"], p["cab"], TAPS_A, seq, TC_A, BF16,
                        comm=gather("conv_a_fwd")), "conv_a_fwd")
    ya, u3 = _fwd_ya(u1, p["ln_g"], p["ln_b"], p["w_a_out"], tm)
    v0 = _conv_fwd("conv_b_fwd", z, 2 * C // TC_B, None, p["cbw"], p["cbb"], TAPS_B, seq, TC_B, F32)
    hr, rg, ig, lag = outs(_rglru_fwd(v0, p["wa"], p["wx"], p["b_rg_a"], p["b_rg_x"], p["lam"], seq,
                                      comm=gather("rglru_fwd")), "rglru_fwd")
    yb, pb = _fwd_yb(hr, z, gb_blk, p["w_b_out"], tl, TC_B)
    x1, mg = outs(_fwd_x1(x, ya, yb, z, sa_blk, p["w_o"], tm, comm=gather("fwd_x1")), "fwd_x1")
    fp, h2 = outs(_fwd_norm_mm("fwd_f", x1, p["g_mlp"], p["w_1"], None, tl, 1024, comm=gather("fwd_f")), "fwd_f")
    x2 = outs(_fwd_x2(x1, fp, p["w_2"], tm, fp.shape[1], comm=gather("fwd_x2")), "fwd_x2")
    saved = dict(x=x, z=z, h=h, u1=u1, u3=u3, ya=ya, v0=v0, hr=hr, rg=rg, ig=ig, lag=lag, pb=pb, yb=yb, mg=mg, x1=x1,
                 fp=fp, h2=h2)
    return x2, saved, p, ahead


class _Reduce:
    EARLY = ("w_2", "w_1", "w_o", "w_a_out")
    LATE = ("w_b_out", "w_in")

    def __init__(self, accs, c_arr, kcl_of):
        self.accs, self.c_arr, self.kcl_of, self.late = accs, c_arr, kcl_of, None

    @staticmethod
    def pieces(partials):
        return [a if a.ndim == 4 else a.reshape(N_CHIPS, 2, a.shape[0] // (2 * N_CHIPS), a.shape[1]) for a in partials]

    def chip_sums(self, pgs, swapped):
        return _sum_siblings(pgs, swapped, self.c_arr)

    def finish(self, names, sums, received, layer):
        done = _sum_chips(sums, received, self.kcl_of(layer), [self.accs[n] for n in names])
        self.accs.update(zip(names, done))


def _layer_bwd(dx2, dx2b, p, s, seq, red=None, layer=0):
    T, D = dx2.shape
    C, R = p["ln_g"].shape[1], p["lam"].shape[1]
    tm, tl = _tiles(T)
    gb_blk, sa_blk = (2 * C + R) // TC_B, (2 * C + 2 * R) // D
    z = s["z"]
    g = {}


    late_sums = None
    if red is not None and red.late is not None:
        late, red.late = red.late, None
        dfp, got = _bwd_df(dx2b, p["w_2"], s["fp"], tl, 1024, comm=_swap_comm(late))
        late_sums = red.chip_sums(late, got)
    else:
        dfp = _bwd_df(dx2b, p["w_2"], s["fp"], tl, 1024)
    g["w_2"] = _bwd_dw("bwd_dw2", s["fp"], dx2b, 1024, D, T, a_fn=_relu2, keep="dy")
    dx1, dx1b, g["g_mlp"] = _bwd_norm("bwd_dh2", dfp, p["w_1"], s["x1"], p["g_mlp"], dx2, tm, dfp.shape[1])
    g["w_1"] = _bwd_dw("bwd_dw1", s["h2"], dfp, D, 1024, T, shard_cols=dfp.shape[1] // N_CHIPS, keep="act")

    dya, dyb, dz = _bwd_dm(dx1b, p["w_o"], s["ya"], s["yb"], z, sa_blk, tm)

    g["w_o"] = _bwd_dw("bwd_dwo", s["mg"], dx1b, D, D, tl)
    du1, g["ln_g"], g["ln_b"] = _bwd_du3(dya, p["w_a_out"], s["u1"], p["ln_g"], p["ln_b"], tm)
    g["w_a_out"] = _bwd_dw("bwd_dwa", s["u3"], dya, C, D, tl)
    conv_a_args = ("conv_a_bwd", du1, z, dz, 0, C // TC_A, p["caw"], TAPS_A, seq, TC_A)
    if late_sums is not None:
        (dz, g["caw"], g["cab"]), got = _conv_bwd(*conv_a_args, comm=_scatter_comm(late_sums))
        red.finish(red.LATE, late_sums, got, layer + 1)
    else:
        dz, g["caw"], g["cab"] = _conv_bwd(*conv_a_args)

    dp_args = (dyb, p["w_b_out"], s["hr"], z, dz, gb_blk, tl, TC_B)
    if red is not None:
        early = red.pieces([g.pop(n) for n in red.EARLY])
        (dhr, dz), got = _bwd_dp(*dp_args, comm=_swap_comm(early))
        early_sums = red.chip_sums(early, got)
    else:
        dhr, dz = _bwd_dp(*dp_args)

    g["w_b_out"] = _bwd_dw("bwd_dwb", s["pb"], dyb, R, D, tl)
    rg_args = (s["v0"], s["hr"], dhr, s["rg"], s["ig"], s["lag"], p["wa"], p["wx"], p["lam"], seq)
    if red is not None:
        rg_out, got = _rglru_bwd(*rg_args, comm=_scatter_comm(early_sums))
        red.finish(red.EARLY, early_sums, got, layer)
    else:
        rg_out = _rglru_bwd(*rg_args)
    dv0, g["wa"], g["wx"], g["b_rg_a"], g["b_rg_x"], g["lam"] = rg_out
    dz, g["cbw"], g["cbb"] = _conv_bwd("conv_b_bwd", dv0, z, dz, 2 * C // TC_B, None, p["cbw"], TAPS_B, seq, TC_B)

    dx, dxb, g["g_mix"], dbin = _bwd_norm("bwd_dh", dz, p["w_in"], s["x"], p["g_mix"], dx1, tm, dz.shape[1],
                                          colsum=True)
    g["b_in"] = dbin.reshape(1, -1)
    ns = dz.shape[1] // N_CHIPS
    g["w_in"] = _bwd_dw("bwd_dwin", s["h"], dz, D, ns // 2, T, shard_cols=ns, keep="act")
    if red is not None:
        red.late = red.pieces([g.pop(n) for n in red.LATE])
    return dx, dxb, g


ANY = pl.BlockSpec(memory_space=pl.ANY)


def _mesh_pos():
    return lax.axis_index("x"), lax.axis_index("y"), lax.axis_index("c")


def _other_chips(x, y):
    return [(1 - x, y), (x, 1 - y), (1 - x, 1 - y)]


def _remote(src, dst, ssem, rsem, dev):
    return pltpu.make_async_remote_copy(src_ref=src, dst_ref=dst, send_sem=ssem, recv_sem=rsem,
                                        device_id=dev, device_id_type=MESH)


def _gather_region(src, dst, by_cols, k, half):
    rows, cols = src.shape
    nr = rows if half is None else rows // 2
    r0 = 0 if half is None else half * nr
    if by_cols:
        return dst.at[pl.ds(r0, nr), pl.ds(pl.multiple_of(k * cols, 128), cols)]
    return dst.at[pl.ds(pl.multiple_of(k * rows + r0, 8), nr), :]


def _gather_sends(src, dst, kinds, send, recv):
    x, y, c = _mesh_pos()
    cps = []
    for t in range(len(src)):
        half = c if kinds[t][1] else None
        hr = src[t].shape[0] // 2
        s_ref = src[t].at[pl.ds(c * hr, hr), :] if kinds[t][1] else src[t]
        for j, chip in enumerate(_other_chips(x, y)):
            cps.append(_remote(s_ref, _gather_region(src[t], dst[t], kinds[t][0], 2 * x + y, half),
                               send.at[t, j], recv.at[t, j], (*chip, c)))
    return cps


def _gather_finish(src, dst, kinds, send, recv, fsend, frecv):
    x, y, c = _mesh_pos()
    chips = _other_chips(x, y)
    sib = (x, y, 1 - c)
    n = len(src)
    fwd = []
    for t in range(n):
        half = c if kinds[t][1] else None
        for j, chip in enumerate(chips):
            got = _gather_region(src[t], dst[t], kinds[t][0], 2 * chip[0] + chip[1], half)
            _remote(got, got, send.at[t, j], recv.at[t, j], (*chip, c)).wait_recv()
            if kinds[t][1]:
                cp = _remote(got, got, fsend.at[t, j], frecv.at[t, j], sib)
                cp.start()
                fwd.append(cp)
    for t in range(n):
        if kinds[t][1]:
            for j, chip in enumerate(chips):
                got = _gather_region(src[t], dst[t], kinds[t][0], 2 * chip[0] + chip[1], 1 - c)
                _remote(got, got, fsend.at[t, j], frecv.at[t, j], sib).wait_recv()
    for cp in _gather_sends(src, dst, kinds, send, recv) + fwd:
        cp.wait_send()


def _gather_sems(n):
    sem = pltpu.SemaphoreType.DMA
    return [sem((n, 3)), sem((n, 3)), sem((n, 3)), sem((n, 3))]


def _gather_comm(shards, kinds):
    n = len(shards)

    def whole(s, by_cols):
        return (s.shape[0], N_CHIPS * s.shape[1]) if by_cols else (N_CHIPS * s.shape[0], s.shape[1])

    def own(src, dst, lsem):
        x, y, _ = _mesh_pos()
        return [pltpu.make_async_copy(src[t], _gather_region(src[t], dst[t], kinds[t][0], 2 * x + y, None), lsem.at[t])
                for t in range(n)]

    def start(src, dst, sems):
        for cp in own(src, dst, sems[4]) + _gather_sends(src, dst, kinds, sems[0], sems[1]):
            cp.start()

    def finish(src, dst, sems):
        _gather_finish(src, dst, kinds, *sems[:4])
        for cp in own(src, dst, sems[4]):
            cp.wait()

    return _Comm(shards, [_sds(whole(s, k[0]), s.dtype) for s, k in zip(shards, kinds)],
                 _gather_sems(n) + [pltpu.SemaphoreType.DMA((n,))], start, finish,
                 "chips+sibling" if any(k[1] for k in kinds) else "chips")


def _scatter_comm(ps):
    n = len(ps)

    def copies(src, dst, sems):
        x, y, c = _mesh_pos()
        return [_remote(src[t].at[2 * chip[0] + chip[1]], dst[t].at[j], sems[0].at[t, j], sems[1].at[t, j], (*chip, c))
                for t in range(n) for j, chip in enumerate(_other_chips(x, y))]

    def start(src, dst, sems):
        for cp in copies(src, dst, sems):
            cp.start()

    def finish(src, dst, sems):
        cps = copies(src, dst, sems)
        for cp in cps:
            cp.wait_recv()
        for cp in cps:
            cp.wait_send()

    sem = pltpu.SemaphoreType.DMA
    return _Comm(ps, [_sds((3,) + a.shape[1:], a.dtype) for a in ps], [sem((n, 3)), sem((n, 3))], start, finish, "chips")


def _comm_call(name, comm):
    n_i, n_o = len(comm.ins), len(comm.outs)

    def body(*refs):
        comm.start(refs[:n_i], refs[n_i:n_i + n_o], refs[n_i + n_o:])
        comm.finish(refs[:n_i], refs[n_i:n_i + n_o], refs[n_i + n_o:])

    return pl.pallas_call(
        body, name=name, in_specs=[ANY] * n_i, out_specs=[ANY] * n_o, out_shape=comm.outs, scratch_shapes=comm.sems,
        compiler_params=_cp(has_side_effects=True, collective_id=comm.collective_id),
    )(*comm.ins)


def _swap_comm(pgs):
    n = len(pgs)

    def copies(src, dst, sems):
        x, y, c = _mesh_pos()
        return [_remote(src[t].at[:, 1 - c], dst[t], sems[0].at[t], sems[1].at[t], (x, y, 1 - c)) for t in range(n)]

    def start(src, dst, sems):
        for cp in copies(src, dst, sems):
            cp.start()

    def finish(src, dst, sems):
        cps = copies(src, dst, sems)
        for cp in cps:
            cp.wait_recv()
        for cp in cps:
            cp.wait_send()

    sem = pltpu.SemaphoreType.DMA
    return _Comm(pgs, [_sds((a.shape[0],) + a.shape[2:], a.dtype) for a in pgs], [sem((n,)), sem((n,))], start, finish,
                 "sibling")


def _join_halves(accs, also=None):
    n = len(accs)
    c_ins, c_outs, c_sems = (also.ins, also.outs, also.sems) if also else ([], [], [])
    assert also is None or also.peers == "chips"
    peers = "chips+sibling" if also else "sibling"

    def body(*refs):
        o0 = n + len(c_ins)
        buf = refs[o0:o0 + n]
        send, recv = refs[o0 + n + len(c_outs):o0 + n + len(c_outs) + 2]
        extra = (refs[n:o0], refs[o0 + n:o0 + n + len(c_outs)], refs[o0 + n + len(c_outs) + 2:])
        x, y, c = _mesh_pos()
        _handshake(peers)
        if also:
            also.copies_start(*extra)
        cps = [_remote(buf[t].at[:, c], buf[t].at[:, c], send.at[t], recv.at[t], (x, y, 1 - c)) for t in range(n)]
        for cp in cps:
            cp.start()
        for t in range(n):
            _remote(buf[t].at[:, c], buf[t].at[:, 1 - c], send.at[t], recv.at[t], (x, y, 1 - c)).wait_recv()
        for cp in cps:
            cp.wait_send()
        if also:
            also.finish(*extra)

    sem = pltpu.SemaphoreType.DMA
    res = pl.pallas_call(
        body, name="join_halves", in_specs=[ANY] * (n + len(c_ins)), out_specs=[ANY] * (n + len(c_outs)),
        out_shape=[_sds(a.shape, a.dtype) for a in accs] + list(c_outs),
        scratch_shapes=[sem((n,)), sem((n,))] + list(c_sems),
        input_output_aliases={t: t for t in range(n)},
        compiler_params=_cp(has_side_effects=True, collective_id=PEER_SETS[peers]),
    )(*accs, *c_ins)
    return (list(res[:n]), list(res[n:])) if also else res


def _sum_siblings(pgs, rbs, c_arr):
    n = len(pgs)
    nk = pgs[0].shape[0]

    def body(c_ref, *refs):
        for t in range(n):
            refs[2 * n + t][...] = (refs[t][...].astype(F32) + refs[n + t][...].astype(F32)).astype(BF16)

    half = lambda a: pl.BlockSpec((None,) + a.shape[2:], lambda k, c_ref: (k, 0, 0))
    return pl.pallas_call(
        body, name="sum_siblings",
        grid_spec=pltpu.PrefetchScalarGridSpec(
            num_scalar_prefetch=1, grid=(nk,),
            in_specs=[pl.BlockSpec((None, None) + a.shape[2:], lambda k, c_ref: (k, c_ref[0], 0, 0)) for a in pgs]
            + [half(a) for a in pgs],
            out_specs=[half(a) for a in pgs]),
        out_shape=[_sds((nk,) + a.shape[2:], BF16) for a in pgs],
        compiler_params=_cp(dimension_semantics=("arbitrary",)),
    )(c_arr, *pgs, *rbs)


def _sum_chips(ps, rbs, kcl, accs):
    n = len(ps)

    def body(k_ref, *refs):
        for t in range(n):
            b_ref = refs[n + t]
            refs[3 * n + t][...] = (refs[t][...].astype(F32) + b_ref[0].astype(F32) + b_ref[1].astype(F32)
                                    + b_ref[2].astype(F32))

    qr = lambda a: (a.shape[1] // 2, a.shape[2])
    return pl.pallas_call(
        body, name="sum_chips",
        grid_spec=pltpu.PrefetchScalarGridSpec(
            num_scalar_prefetch=1, grid=(2,),
            in_specs=[pl.BlockSpec((None,) + qr(a), lambda r, k_ref: (k_ref[0], r, 0)) for a in ps]
            + [pl.BlockSpec((3,) + qr(a), lambda r, k_ref: (0, r, 0)) for a in ps] + [ANY] * n,
            out_specs=[pl.BlockSpec((None, None) + qr(a), lambda r, k_ref: (k_ref[2], k_ref[1], r, 0)) for a in ps]),
        out_shape=[_sds(a.shape, F32) for a in accs], input_output_aliases={1 + 2 * n + t: t for t in range(n)},
        compiler_params=_cp(dimension_semantics=("arbitrary",)),
    )(kcl, *ps, *rbs, *accs)


N_DEV = 8


def _allreduce_small(parts):
    n = len(parts)

    def body(*refs):
        p_refs, o_refs, rbufs = refs[:n], refs[n:2 * n], refs[2 * n:3 * n]
        s1, r1, s2, r2 = refs[3 * n:]
        x, y, c = _mesh_pos()
        me = 4 * x + 2 * y + c
        devs = [(d // 4, (d // 2) % 2, d % 2) for d in range(N_DEV)]
        for q in range(n):
            rbufs[q][me] = p_refs[q][me]

        def each_peer(fn):
            for d in range(N_DEV):
                @pl.when(d != me)
                def _():
                    for q in range(n):
                        fn(d, q)

        def first(d, q, to_me):
            return _remote(p_refs[q].at[d], rbufs[q].at[d if to_me else me], s1.at[q, d], r1.at[q, d if to_me else me],
                           devs[d])

        def second(d, q, to_me):
            blk = d if to_me else me
            return _remote(o_refs[q].at[blk], o_refs[q].at[blk], s2.at[q, d], r2.at[q, blk], devs[d])

        each_peer(lambda d, q: first(d, q, False).start())
        each_peer(lambda d, q: first(d, q, True).wait_recv())
        for q in range(n):
            total = rbufs[q][0].astype(F32)
            for d in range(1, N_DEV):
                total = total + rbufs[q][d].astype(F32)
            o_refs[q][me] = total.astype(o_refs[q].dtype)
        each_peer(lambda d, q: second(d, q, False).start())
        each_peer(lambda d, q: second(d, q, True).wait_recv())
        each_peer(lambda d, q: first(d, q, False).wait_send())
        each_peer(lambda d, q: second(d, q, False).wait_send())

    sem = pltpu.SemaphoreType.DMA
    vm = pl.BlockSpec(memory_space=pltpu.VMEM)
    return pl.pallas_call(
        body, name="allreduce_small", in_specs=[vm] * n, out_specs=[vm] * n,
        out_shape=[_sds(a.shape, a.dtype) for a in parts],
        scratch_shapes=[pltpu.VMEM(a.shape, a.dtype) for a in parts] + [sem((n, N_DEV))] * 4,
        compiler_params=_cp(has_side_effects=True),
    )(*parts)


BIG = ("w_in", "w_1", "w_a_out", "w_b_out", "w_o", "w_2")
BY_COLS = {"w_in": True, "w_1": True, "w_a_out": False, "w_b_out": False, "w_o": False, "w_2": False}
WEIGHTS = ("g_mix", "w_in", "b_in", "conv_a_w", "conv_a_b", "ln_g", "ln_b", "w_a_out", "conv_b_w", "conv_b_b", "w_rg_a",
           "b_rg_a", "w_rg_x", "b_rg_x", "lam", "w_b_out", "w_o", "g_mlp", "w_1", "w_2", "g_final")
SMALL = tuple(n for n in WEIGHTS if n not in BIG)
ADAM_ROWS = 256
ADAM_SMALL_ROWS = 2048


CAST_STEPS = 8


def _cast_weights(todo, comm):
    def body(*refs):
        n = len(todo)
        for q in range(n):
            refs[n + q][...] = refs[q][...].astype(BF16)

    tile = lambda a: (a.shape[1] // CAST_STEPS, a.shape[2])
    in_specs = [pl.BlockSpec((None,) + tile(a), lambda r, l=l: (l, r, 0)) for a, l in todo]
    out_specs = [pl.BlockSpec(tile(a), lambda r: (r, 0)) for a, _ in todo]
    return _call_with_comm("cast_weights", body, (CAST_STEPS,), [a for a, _ in todo], in_specs, out_specs,
                           [_sds(a.shape[1:], BF16) for a, _ in todo], [], comm)


def _block_diag(w):
    nh, dh, _ = w.shape
    ng = nh // HEADS_PER_GROUP
    w4 = w.reshape(ng, HEADS_PER_GROUP, dh, dh)
    eye = jnp.eye(HEADS_PER_GROUP, dtype=w.dtype)
    return jnp.einsum("qhij,hk->qhikj", w4, eye).reshape(ng, HEADS_PER_GROUP * dh, HEADS_PER_GROUP * dh)


def _block_diag_part(d, dh):
    ng = d.shape[0]
    eye = jnp.eye(HEADS_PER_GROUP, dtype=d.dtype)
    d5 = d.reshape(ng, HEADS_PER_GROUP, dh, HEADS_PER_GROUP, dh)
    return jnp.einsum("qhikj,hk->qhij", d5, eye).reshape(ng * HEADS_PER_GROUP, dh, dh)


PACK_LANES = 128


def _pack(arrays, blocks, tile_rows):
    parts = [a.reshape(-1, PACK_LANES) for a in arrays]
    parts = [jnp.pad(p, ((0, -p.shape[0] % tile_rows), (0, 0))) if p.shape[0] % tile_rows else p for p in parts]
    rows = sum(p.shape[0] for p in parts)
    pad = -rows % (blocks * tile_rows)
    if pad:
        parts.append(jnp.zeros((pad, PACK_LANES), parts[0].dtype))
    return jnp.concatenate(parts, axis=0).reshape(blocks, -1, PACK_LANES)


def _unpack(buf, like, tile_rows):
    buf = buf.reshape(-1, PACK_LANES)
    out, off = [], 0
    for a in like:
        n = a.size // PACK_LANES
        out.append(buf[off:off + n].reshape(a.shape))
        off += n + (-n % tile_rows)
    return out


def kernel(x, g_mix, w_in, b_in, conv_a_w, conv_a_b, ln_g, ln_b, w_a_out, conv_b_w, conv_b_b, w_rg_a, b_rg_a, w_rg_x, b_rg_x, lam, w_b_out, w_o, g_mlp, w_1, w_2, g_final, loss_target, m_g_mix, m_w_in, m_b_in, m_conv_a_w, m_conv_a_b, m_ln_g, m_ln_b, m_w_a_out, m_conv_b_w, m_conv_b_b, m_w_rg_a, m_b_rg_a, m_w_rg_x, m_b_rg_x, m_lam, m_w_b_out, m_w_o, m_g_mlp, m_w_1, m_w_2, m_g_final, v_g_mix, v_w_in, v_b_in, v_conv_a_w, v_conv_a_b, v_ln_g, v_ln_b, v_w_a_out, v_conv_b_w, v_conv_b_b, v_w_rg_a, v_b_rg_a, v_w_rg_x, v_b_rg_x, v_lam, v_w_b_out, v_w_o, v_g_mlp, v_w_1, v_w_2, v_g_final):
    w = dict(g_mix=g_mix, w_in=w_in, b_in=b_in, conv_a_w=conv_a_w, conv_a_b=conv_a_b, ln_g=ln_g, ln_b=ln_b, w_a_out=w_a_out,
             conv_b_w=conv_b_w, conv_b_b=conv_b_b, w_rg_a=w_rg_a, b_rg_a=b_rg_a, w_rg_x=w_rg_x, b_rg_x=b_rg_x, lam=lam,
             w_b_out=w_b_out, w_o=w_o, g_mlp=g_mlp, w_1=w_1, w_2=w_2, g_final=g_final)
    m = dict(g_mix=m_g_mix, w_in=m_w_in, b_in=m_b_in, conv_a_w=m_conv_a_w, conv_a_b=m_conv_a_b, ln_g=m_ln_g, ln_b=m_ln_b,
             w_a_out=m_w_a_out, conv_b_w=m_conv_b_w, conv_b_b=m_conv_b_b, w_rg_a=m_w_rg_a, b_rg_a=m_b_rg_a, w_rg_x=m_w_rg_x,
             b_rg_x=m_b_rg_x, lam=m_lam, w_b_out=m_w_b_out, w_o=m_w_o, g_mlp=m_g_mlp, w_1=m_w_1, w_2=m_w_2, g_final=m_g_final)
    v = dict(g_mix=v_g_mix, w_in=v_w_in, b_in=v_b_in, conv_a_w=v_conv_a_w, conv_a_b=v_conv_a_b, ln_g=v_ln_g, ln_b=v_ln_b,
             w_a_out=v_w_a_out, conv_b_w=v_conv_b_w, conv_b_b=v_conv_b_b, w_rg_a=v_w_rg_a, b_rg_a=v_b_rg_a, w_rg_x=v_w_rg_x,
             b_rg_x=v_b_rg_x, lam=v_lam, w_b_out=v_w_b_out, w_o=v_w_o, g_mlp=v_g_mlp, w_1=v_w_1, w_2=v_w_2, g_final=v_g_final)
    B, S, D = x.shape
    T = B * S
    L = w_in.shape[0]
    dh = w_rg_a.shape[-1]
    taps_a, taps_b = conv_a_w.shape[1], conv_b_w.shape[1]
    assert (taps_a, taps_b) == (TAPS_A, TAPS_B)
    xi, yi, ci = _mesh_pos()
    c_arr = jnp.reshape(ci, (1,)).astype(jnp.int32)
    k_me = 2 * xi + yi

    caw_p = jnp.pad(conv_a_w, ((0, 0), (0, 32 - taps_a), (0, 0)))
    cbw_p = jnp.pad(conv_b_w, ((0, 0), (0, 8 - taps_b), (0, 0)))
    row = lambda a: a.reshape(1, -1)

    first_w_in = w_in[0].astype(BF16)
    todo = [(w[n], l) for l in range(L) for n in BIG if (l, n) != (0, "w_in")]
    cast, (w_in_whole,) = _cast_weights(todo, _gather_comm([first_w_in], [GATHER_KIND["w_in"]]))
    cast = iter(cast)
    bf = {(l, n): first_w_in if (l, n) == (0, "w_in") else next(cast) for l in range(L) for n in BIG}

    def shards_of(l):
        d = {n: bf[(l, n)] for n in BIG}
        d.update(caw=caw_p[l], cbw=cbw_p[l])
        return d

    def params_of(l, whole):
        p = dict(whole, cab=row(conv_a_b[l]), cbb=row(conv_b_b[l]),
                 wa=_block_diag(w_rg_a[l]).astype(BF16), wx=_block_diag(w_rg_x[l]).astype(BF16))
        for n in ("g_mix", "b_in", "ln_g", "ln_b", "b_rg_a", "b_rg_x", "lam", "g_mlp"):
            p[n] = row(w[n][l])
        return p

    shards = [shards_of(l) for l in range(L)]
    whole = {"w_in": w_in_whole}
    xf = x.reshape(T, D)
    saved, params = [], []
    for l in range(L):
        cur = lambda names: [(n, shards[l][n], False) for n in names]
        nxt = lambda names: [(n, shards[l + 1][n], True) for n in names]
        if l == 0:
            jobs = {"fwd_z": cur(["w_a_out", "w_b_out", "w_o", "caw", "cbw"]), "conv_a_fwd": cur(["w_1"]),
                    "fwd_f": cur(["w_2"])}
        else:
            jobs = {"fwd_z": cur(["w_1"]), "conv_a_fwd": cur(["w_2"])}
        if l + 1 < L:
            jobs.update({"rglru_fwd": nxt(["w_in"]), "fwd_x1": nxt(["w_o", "caw", "cbw"]),
                         "fwd_x2": nxt(["w_b_out", "w_a_out"])})
        xf, s, p, whole = _layer_fwd(xf, params_of(l, whole), S, jobs)
        saved.append(s)
        params.append(p)
    loss_part, dx, dxb, dg_final = _loss_head(xf, row(g_final), loss_target.reshape(T, D), _tiles(T)[0])
    loss = lax.psum(loss_part[0, 0], ("x", "y", "c"))

    half_shape = lambda a: (L, 2, a.shape[1] // 2, a.shape[2])
    accs = {n: lax.empty(half_shape(w[n]), F32) for n in BIG}
    small = {n: [None] * L for n in SMALL if n != "g_final"}
    red = _Reduce(accs, c_arr, lambda l: jnp.stack([k_me, ci, jnp.full((), l, ci.dtype)]).astype(jnp.int32))
    for l in reversed(range(L)):
        dx, dxb, g = _layer_bwd(dx, dxb, params[l], saved[l], S, red=red, layer=l)
        small["g_mix"][l], small["b_in"][l], small["g_mlp"][l] = g["g_mix"], g["b_in"], g["g_mlp"]
        small["conv_a_w"][l], small["conv_a_b"][l] = g["caw"], g["cab"]
        small["conv_b_w"][l], small["conv_b_b"][l] = g["cbw"], g["cbb"]
        small["ln_g"][l], small["ln_b"][l], small["lam"][l] = g["ln_g"], g["ln_b"], g["lam"]
        small["w_rg_a"][l], small["w_rg_x"][l] = _block_diag_part(g["wa"], dh), _block_diag_part(g["wx"], dh)
        small["b_rg_a"][l], small["b_rg_x"][l] = g["b_rg_a"], g["b_rg_x"]
    grad_x = dx.reshape(B, S, D)

    delta, new_m, new_v = {}, {}, {}
    flat = lambda a: a.reshape(-1, a.shape[-1])

    def adam_big(names, comm=None):
        r = _adamw("adamw_" + names[0], *[[flat(d[n]) for n in names] for d in (w, grads, m, v)], ADAM_ROWS, comm=comm)
        for q, n in enumerate(names):
            delta[n], new_m[n], new_v[n] = (r[a][q].reshape(w[n].shape) for a in range(3))
        return r[3] if comm else None

    late_sums = red.chip_sums(red.late, _comm_call("swap_halves", _swap_comm(red.late)))
    joined, got = _join_halves([red.accs[n] for n in red.EARLY], also=_scatter_comm(late_sums))
    grads = {n: a.reshape(w[n].shape) for n, a in zip(red.EARLY, joined)}
    adam_big(["w_2", "w_1"])
    adam_big(["w_o", "w_a_out"])
    red.finish(red.LATE, late_sums, got, 0)
    joined = _join_halves([red.accs[n] for n in red.LATE])
    grads.update({n: a.reshape(w[n].shape) for n, a in zip(red.LATE, joined)})
    adam_big(["w_b_out"])
    adam_big(["w_in"])

    wide = ["w_rg_a", "w_rg_x"]
    names = [n for n in SMALL if n != "g_final" and n not in wide]
    parts = [jnp.stack(small[n]) for n in names] + [dg_final]
    parts_w = [jnp.stack(small[n]).astype(BF16) for n in wide]
    total, total_w = _allreduce_small([_pack(parts, N_DEV, 8), _pack(parts_w, N_DEV, 16)])
    summed = _unpack(total, parts, 8) + [a.astype(F32) for a in _unpack(total_w, parts_w, 16)]
    for n, a in zip(names + ["g_final"] + wide, summed):
        if n == "conv_a_w":
            a = lax.dynamic_slice_in_dim(a[:, :taps_a], k_me * conv_a_w.shape[2], conv_a_w.shape[2], axis=2)
        elif n == "conv_b_w":
            a = lax.dynamic_slice_in_dim(a[:, :taps_b], k_me * conv_b_w.shape[2], conv_b_w.shape[2], axis=2)
        grads[n] = a.reshape(w[n].shape)

    for n in SMALL:
        cols = w[n].shape[-1]
        view = lambda a: a.reshape(-1, cols)
        rows = view(w[n]).shape[0]
        d_, m_, v_ = _adamw("adamw_" + n, view(w[n]), view(grads[n]), view(m[n]), view(v[n]),
                            ADAM_SMALL_ROWS if rows % ADAM_SMALL_ROWS == 0 else rows)
        delta[n], new_m[n], new_v[n] = (a.reshape(w[n].shape) for a in (d_, m_, v_))

    return (loss, grad_x, *[grads[n] for n in WEIGHTS], *[delta[n] for n in WEIGHTS],
            *[new_m[n] for n in WEIGHTS], *[new_v[n] for n in WEIGHTS])
```
